```python
import jax, jax.numpy as jnp
from jax import lax
import numpy as np

D_MODEL = 1024
BATCH = 8
SEQ = 8192
DEPTH = 2

EPS = 1e-6
LN_EPS = 1e-5
CONV_WIDTH = 512
CONV_K = 31
N_HEADS = 8
QK_NOPE_DIM = 64
QK_ROPE_DIM = 32
V_HEAD_DIM = 64
Q_LORA_RANK = 256
KV_LORA_RANK = 128
ROPE_BASE = 10000.0
Q_BLOCK = 128
ATTN_WIDTH = N_HEADS * V_HEAD_DIM
IN_EVEN = 2 * CONV_WIDTH + Q_LORA_RANK + KV_LORA_RANK + QK_ROPE_DIM
SPLIT_EVEN = [CONV_WIDTH, 2 * CONV_WIDTH, 2 * CONV_WIDTH + Q_LORA_RANK,
              2 * CONV_WIDTH + Q_LORA_RANK + KV_LORA_RANK]
MIX_EVEN = CONV_WIDTH + ATTN_WIDTH
SSM_WIDTH = 512
SSM_GROUP = 16
SSM_GROUPS = SSM_WIDTH // SSM_GROUP
SSM_STATE = 64
DT_MIN = 0.001
DT_MAX = 0.1
D_FF = 2816
FFN_K = 3

kernel_name = "hybrid_conv_mla_s5_convffn"


def rms_norm(x, g):
    xf = x.astype(jnp.float32)
    y = xf * lax.rsqrt(jnp.mean(xf * xf, axis=-1, keepdims=True) + EPS)
    return (y * g.astype(jnp.float32)).astype(x.dtype)


def layer_norm(x, g, b):
    xf = x.astype(jnp.float32)
    mu = jnp.mean(xf, axis=-1, keepdims=True)
    var = jnp.mean(jnp.square(xf - mu), axis=-1, keepdims=True)
    y = (xf - mu) * lax.rsqrt(var + LN_EPS)
    return (y * g.astype(jnp.float32) + b.astype(jnp.float32)).astype(x.dtype)


def causal_dwconv(u, w, b):
    k, c = w.shape
    y = lax.conv_general_dilated(u, w[:, None, :].astype(u.dtype), window_strides=(1,),
                                 padding=[(k - 1, 0)],
                                 dimension_numbers=('NWC', 'WIO', 'NWC'),
                                 feature_group_count=c)
    return y + b.astype(u.dtype)


def rope(t, pos):
    half = QK_ROPE_DIM // 2
    inv = ROPE_BASE ** (-jnp.arange(half, dtype=jnp.float32) / half)
    ang = pos.astype(jnp.float32)[:, None] * inv[None, :]
    cos, sin = jnp.cos(ang)[:, None, :], jnp.sin(ang)[:, None, :]
    tf = t.astype(jnp.float32)
    t1, t2 = tf[..., :half], tf[..., half:]
    return jnp.concatenate([t1 * cos - t2 * sin, t1 * sin + t2 * cos], axis=-1).astype(t.dtype)


def mla_attention(q_nope, q_rope, k_nope, k_rope, v):
    b, s, h, _ = q_nope.shape
    nb = s // Q_BLOCK
    scale = (QK_NOPE_DIM + QK_ROPE_DIM) ** -0.5
    k_pos = jnp.arange(s)

    def blocks(t):
        return jnp.moveaxis(t.reshape((b, nb, Q_BLOCK) + t.shape[2:]), 1, 0)

    def one_block(args):
        qn, qr, i = args
        scores = (jnp.einsum('bqhd,bkhd->bhqk', qn, k_nope).astype(jnp.float32)
                  + jnp.einsum('bqhr,bkr->bhqk', qr, k_rope).astype(jnp.float32)) * scale
        q_pos = i * Q_BLOCK + jnp.arange(Q_BLOCK)
        mask = k_pos[None, :] <= q_pos[:, None]
        scores = jnp.where(mask[None, None], scores, -jnp.inf)
        p = jax.nn.softmax(scores, axis=-1).astype(v.dtype)
        return jnp.einsum('bhqk,bkhd->bqhd', p, v)

    out = lax.map(one_block, (blocks(q_nope), blocks(q_rope), jnp.arange(nb)))
    return jnp.moveaxis(out, 0, 1).reshape(b, s, h * V_HEAD_DIM)


def conv_attn_mixer(xn, w_in, conv_w, conv_b, conv_ln_g, conv_ln_b,
                    q_norm, kv_norm, w_uq, w_ukv, w_out):
    b, s, _ = xn.shape
    h = xn @ w_in
    glu_a, glu_g, c_q, c_kv, k_r = jnp.split(h, SPLIT_EVEN, axis=-1)
    u = glu_a * jax.nn.sigmoid(glu_g)
    u = causal_dwconv(u, conv_w, conv_b)
    u = jax.nn.silu(layer_norm(u, conv_ln_g, conv_ln_b))
    pos = jnp.arange(s)
    q = (rms_norm(c_q, q_norm) @ w_uq).reshape(b, s, N_HEADS, QK_NOPE_DIM + QK_ROPE_DIM)
    q_nope, q_rope = q[..., :QK_NOPE_DIM], rope(q[..., QK_NOPE_DIM:], pos)
    kv = (rms_norm(c_kv, kv_norm) @ w_ukv).reshape(b, s, N_HEADS, QK_NOPE_DIM + V_HEAD_DIM)
    k_nope, v = kv[..., :QK_NOPE_DIM], kv[..., QK_NOPE_DIM:]
    k_rope = rope(k_r[:, :, None, :], pos)[:, :, 0, :]
    attn = mla_attention(q_nope, q_rope, k_nope, k_rope, v)
    return jnp.concatenate([u, attn], axis=-1) @ w_out


def s5_mixer(xn, w_in, log_dt, a_re, a_im, b_re, b_im, c_re, c_im, d_skip, w_glu, b_glu):
    f32 = jnp.float32
    bsz, s, _ = xn.shape
    u = (xn @ w_in).astype(f32).reshape(bsz, s, SSM_GROUPS, SSM_GROUP)
    dt = jnp.exp(log_dt.astype(f32))[:, None]
    ar, ai = a_re.astype(f32), a_im.astype(f32)
    mag = jnp.exp(ar * dt)
    lb_re, lb_im = mag * jnp.cos(ai * dt), mag * jnp.sin(ai * dt)
    den = ar * ar + ai * ai
    nr, ni = lb_re - 1.0, lb_im
    f_re = (nr * ar + ni * ai) / den
    f_im = (ni * ar - nr * ai) / den
    br, bi = b_re.astype(f32), b_im.astype(f32)
    bb_re = f_re[..., None] * br - f_im[..., None] * bi
    bb_im = f_re[..., None] * bi + f_im[..., None] * br
    bu_re = jnp.einsum('bsgc,gpc->bsgp', u, bb_re)
    bu_im = jnp.einsum('bsgc,gpc->bsgp', u, bb_im)
    lam_re = jnp.broadcast_to(lb_re, bu_re.shape)
    lam_im = jnp.broadcast_to(lb_im, bu_re.shape)

    def combine(e1, e2):
        a1r, a1i, b1r, b1i = e1
        a2r, a2i, b2r, b2i = e2
        return (a2r * a1r - a2i * a1i, a2r * a1i + a2i * a1r,
                a2r * b1r - a2i * b1i + b2r, a2r * b1i + a2i * b1r + b2i)

    _, _, x_re, x_im = lax.associative_scan(combine, (lam_re, lam_im, bu_re, bu_im), axis=1)
    y = (jnp.einsum('gcp,bsgp->bsgc', c_re.astype(f32), x_re)
         - jnp.einsum('gcp,bsgp->bsgc', c_im.astype(f32), x_im)
         + d_skip.astype(f32).reshape(SSM_GROUPS, SSM_GROUP) * u)
    y = jax.nn.gelu(y.reshape(bsz, s, SSM_WIDTH)).astype(xn.dtype)
    z = y @ w_glu + b_glu
    return z[..., :D_MODEL] * jax.nn.sigmoid(z[..., D_MODEL:])


def conv_ffn(xn, w_up, conv_w, conv_b, w_down):
    h = causal_dwconv(xn @ w_up, conv_w, conv_b)
    return (jax.nn.silu(h[..., :D_FF]) * h[..., D_FF:]) @ w_down


def _fwd_setup_inputs(seed: int = 0) -> dict:
    key = jax.random.key(seed)
    keys = iter(jax.random.split(key, 64))
    f32 = jnp.float32

    def nrm(shape, scale):
        return jax.random.normal(next(keys), shape, f32) * scale

    def gain(n):
        return 1.0 + nrm((n,), 0.01)

    d = D_MODEL
    inp = {}
    inp["x"] = nrm((BATCH, SEQ, d), 1.0)
    inp["l0_mix_norm"] = gain(d)
    inp["l0_w_in"] = nrm((d, IN_EVEN), d ** -0.5)
    inp["l0_conv_w"] = nrm((CONV_K, CONV_WIDTH), CONV_K ** -0.5)
    inp["l0_conv_b"] = nrm((CONV_WIDTH,), 0.01)
    inp["l0_conv_ln_g"] = gain(CONV_WIDTH)
    inp["l0_conv_ln_b"] = nrm((CONV_WIDTH,), 0.01)
    inp["l0_q_norm"] = gain(Q_LORA_RANK)
    inp["l0_kv_norm"] = gain(KV_LORA_RANK)
    inp["l0_w_uq"] = nrm((Q_LORA_RANK, N_HEADS * (QK_NOPE_DIM + QK_ROPE_DIM)), Q_LORA_RANK ** -0.5)
    inp["l0_w_ukv"] = nrm((KV_LORA_RANK, N_HEADS * (QK_NOPE_DIM + V_HEAD_DIM)), KV_LORA_RANK ** -0.5)
    inp["l0_w_out"] = nrm((MIX_EVEN, d), MIX_EVEN ** -0.5)
    inp["l0_ffn_norm"] = gain(d)
    inp["l0_w_up"] = nrm((d, 2 * D_FF), d ** -0.5)
    inp["l0_ffn_conv_w"] = nrm((FFN_K, 2 * D_FF), FFN_K ** -0.5)
    inp["l0_ffn_conv_b"] = nrm((2 * D_FF,), 0.01)
    inp["l0_w_down"] = nrm((D_FF, d), D_FF ** -0.5)
    inp["l1_mix_norm"] = gain(d)
    inp["l1_w_in"] = nrm((d, SSM_WIDTH), d ** -0.5)
    inp["l1_log_dt"] = jax.random.uniform(next(keys), (SSM_GROUPS,), f32,
                                          float(np.log(DT_MIN)), float(np.log(DT_MAX)))
    inp["l1_a_re"] = -0.5 + nrm((SSM_GROUPS, SSM_STATE), 0.01)
    inp["l1_a_im"] = (jnp.pi * jnp.arange(SSM_STATE, dtype=f32))[None, :] + nrm((SSM_GROUPS, SSM_STATE), 0.01)
    inp["l1_b_re"] = nrm((SSM_GROUPS, SSM_STATE, SSM_GROUP), (2 * SSM_GROUP) ** -0.5)
    inp["l1_b_im"] = nrm((SSM_GROUPS, SSM_STATE, SSM_GROUP), (2 * SSM_GROUP) ** -0.5)
    inp["l1_c_re"] = nrm((SSM_GROUPS, SSM_GROUP, SSM_STATE), SSM_STATE ** -0.5)
    inp["l1_c_im"] = nrm((SSM_GROUPS, SSM_GROUP, SSM_STATE), SSM_STATE ** -0.5)
    inp["l1_d"] = nrm((SSM_WIDTH,), 1.0)
    inp["l1_w_glu"] = nrm((SSM_WIDTH, 2 * d), SSM_WIDTH ** -0.5)
    inp["l1_b_glu"] = nrm((2 * d,), 0.01)
    inp["l1_ffn_norm"] = gain(d)
    inp["l1_w_up"] = nrm((d, 2 * D_FF), d ** -0.5)
    inp["l1_ffn_conv_w"] = nrm((FFN_K, 2 * D_FF), FFN_K ** -0.5)
    inp["l1_ffn_conv_b"] = nrm((2 * D_FF,), 0.01)
    inp["l1_w_down"] = nrm((D_FF, d), D_FF ** -0.5)
    inp["final_norm"] = gain(d)
    return inp


def _fwd_reference(x,
              l0_mix_norm, l0_w_in, l0_conv_w, l0_conv_b, l0_conv_ln_g, l0_conv_ln_b,
              l0_q_norm, l0_kv_norm, l0_w_uq, l0_w_ukv, l0_w_out,
              l0_ffn_norm, l0_w_up, l0_ffn_conv_w, l0_ffn_conv_b, l0_w_down,
              l1_mix_norm, l1_w_in, l1_log_dt, l1_a_re, l1_a_im, l1_b_re, l1_b_im,
              l1_c_re, l1_c_im, l1_d, l1_w_glu, l1_b_glu,
              l1_ffn_norm, l1_w_up, l1_ffn_conv_w, l1_ffn_conv_b, l1_w_down,
              final_norm):
    layers = [
        (conv_attn_mixer,
         (l0_w_in, l0_conv_w, l0_conv_b, l0_conv_ln_g, l0_conv_ln_b,
          l0_q_norm, l0_kv_norm, l0_w_uq, l0_w_ukv, l0_w_out),
         l0_mix_norm, (l0_w_up, l0_ffn_conv_w, l0_ffn_conv_b, l0_w_down), l0_ffn_norm),
        (s5_mixer,
         (l1_w_in, l1_log_dt, l1_a_re, l1_a_im, l1_b_re, l1_b_im,
          l1_c_re, l1_c_im, l1_d, l1_w_glu, l1_b_glu),
         l1_mix_norm, (l1_w_up, l1_ffn_conv_w, l1_ffn_conv_b, l1_w_down), l1_ffn_norm),
    ]
    for i in range(DEPTH):
        mixer, mix_params, mix_g, ffn_params, ffn_g = layers[i]
        x = x + mixer(rms_norm(x, mix_g), *mix_params)
        x = x + conv_ffn(rms_norm(x, ffn_g), *ffn_params)
    return rms_norm(x, final_norm)


import jax as _jax
import jax.numpy as _jnp

TWIN_FORMAT = 'train_step'
FWD_PARAMS = ['x', 'l0_mix_norm', 'l0_w_in', 'l0_conv_w', 'l0_conv_b', 'l0_conv_ln_g', 'l0_conv_ln_b', 'l0_q_norm', 'l0_kv_norm', 'l0_w_uq', 'l0_w_ukv', 'l0_w_out', 'l0_ffn_norm', 'l0_w_up', 'l0_ffn_conv_w', 'l0_ffn_conv_b', 'l0_w_down', 'l1_mix_norm', 'l1_w_in', 'l1_log_dt', 'l1_a_re', 'l1_a_im', 'l1_b_re', 'l1_b_im', 'l1_c_re', 'l1_c_im', 'l1_d', 'l1_w_glu', 'l1_b_glu', 'l1_ffn_norm', 'l1_w_up', 'l1_ffn_conv_w', 'l1_ffn_conv_b', 'l1_w_down', 'final_norm']
TWIN_WEIGHTS = ['l0_mix_norm', 'l0_w_in', 'l0_conv_w', 'l0_conv_b', 'l0_conv_ln_g', 'l0_conv_ln_b', 'l0_q_norm', 'l0_kv_norm', 'l0_w_uq', 'l0_w_ukv', 'l0_w_out', 'l0_ffn_norm', 'l0_w_up', 'l0_ffn_conv_w', 'l0_ffn_conv_b', 'l0_w_down', 'l1_mix_norm', 'l1_w_in', 'l1_log_dt', 'l1_a_re', 'l1_a_im', 'l1_b_re', 'l1_b_im', 'l1_c_re', 'l1_c_im', 'l1_d', 'l1_w_glu', 'l1_b_glu', 'l1_ffn_norm', 'l1_w_up', 'l1_ffn_conv_w', 'l1_ffn_conv_b', 'l1_w_down', 'final_norm']
TWIN_DIFF_INPUT = 'x'
TWIN_INPUTS = ['x', 'l0_mix_norm', 'l0_w_in', 'l0_conv_w', 'l0_conv_b', 'l0_conv_ln_g', 'l0_conv_ln_b', 'l0_q_norm', 'l0_kv_norm', 'l0_w_uq', 'l0_w_ukv', 'l0_w_out', 'l0_ffn_norm', 'l0_w_up', 'l0_ffn_conv_w', 'l0_ffn_conv_b', 'l0_w_down', 'l1_mix_norm', 'l1_w_in', 'l1_log_dt', 'l1_a_re', 'l1_a_im', 'l1_b_re', 'l1_b_im', 'l1_c_re', 'l1_c_im', 'l1_d', 'l1_w_glu', 'l1_b_glu', 'l1_ffn_norm', 'l1_w_up', 'l1_ffn_conv_w', 'l1_ffn_conv_b', 'l1_w_down', 'final_norm', 'loss_target', 'm_l0_mix_norm', 'm_l0_w_in', 'm_l0_conv_w', 'm_l0_conv_b', 'm_l0_conv_ln_g', 'm_l0_conv_ln_b', 'm_l0_q_norm', 'm_l0_kv_norm', 'm_l0_w_uq', 'm_l0_w_ukv', 'm_l0_w_out', 'm_l0_ffn_norm', 'm_l0_w_up', 'm_l0_ffn_conv_w', 'm_l0_ffn_conv_b', 'm_l0_w_down', 'm_l1_mix_norm', 'm_l1_w_in', 'm_l1_log_dt', 'm_l1_a_re', 'm_l1_a_im', 'm_l1_b_re', 'm_l1_b_im', 'm_l1_c_re', 'm_l1_c_im', 'm_l1_d', 'm_l1_w_glu', 'm_l1_b_glu', 'm_l1_ffn_norm', 'm_l1_w_up', 'm_l1_ffn_conv_w', 'm_l1_ffn_conv_b', 'm_l1_w_down', 'm_final_norm', 'v_l0_mix_norm', 'v_l0_w_in', 'v_l0_conv_w', 'v_l0_conv_b', 'v_l0_conv_ln_g', 'v_l0_conv_ln_b', 'v_l0_q_norm', 'v_l0_kv_norm', 'v_l0_w_uq', 'v_l0_w_ukv', 'v_l0_w_out', 'v_l0_ffn_norm', 'v_l0_w_up', 'v_l0_ffn_conv_w', 'v_l0_ffn_conv_b', 'v_l0_w_down', 'v_l1_mix_norm', 'v_l1_w_in', 'v_l1_log_dt', 'v_l1_a_re', 'v_l1_a_im', 'v_l1_b_re', 'v_l1_b_im', 'v_l1_c_re', 'v_l1_c_im', 'v_l1_d', 'v_l1_w_glu', 'v_l1_b_glu', 'v_l1_ffn_norm', 'v_l1_w_up', 'v_l1_ffn_conv_w', 'v_l1_ffn_conv_b', 'v_l1_w_down', 'v_final_norm']
TWIN_OUTPUTS = ['loss', 'grad_x', 'grad_l0_mix_norm', 'grad_l0_w_in', 'grad_l0_conv_w', 'grad_l0_conv_b', 'grad_l0_conv_ln_g', 'grad_l0_conv_ln_b', 'grad_l0_q_norm', 'grad_l0_kv_norm', 'grad_l0_w_uq', 'grad_l0_w_ukv', 'grad_l0_w_out', 'grad_l0_ffn_norm', 'grad_l0_w_up', 'grad_l0_ffn_conv_w', 'grad_l0_ffn_conv_b', 'grad_l0_w_down', 'grad_l1_mix_norm', 'grad_l1_w_in', 'grad_l1_log_dt', 'grad_l1_a_re', 'grad_l1_a_im', 'grad_l1_b_re', 'grad_l1_b_im', 'grad_l1_c_re', 'grad_l1_c_im', 'grad_l1_d', 'grad_l1_w_glu', 'grad_l1_b_glu', 'grad_l1_ffn_norm', 'grad_l1_w_up', 'grad_l1_ffn_conv_w', 'grad_l1_ffn_conv_b', 'grad_l1_w_down', 'grad_final_norm', 'delta_l0_mix_norm', 'delta_l0_w_in', 'delta_l0_conv_w', 'delta_l0_conv_b', 'delta_l0_conv_ln_g', 'delta_l0_conv_ln_b', 'delta_l0_q_norm', 'delta_l0_kv_norm', 'delta_l0_w_uq', 'delta_l0_w_ukv', 'delta_l0_w_out', 'delta_l0_ffn_norm', 'delta_l0_w_up', 'delta_l0_ffn_conv_w', 'delta_l0_ffn_conv_b', 'delta_l0_w_down', 'delta_l1_mix_norm', 'delta_l1_w_in', 'delta_l1_log_dt', 'delta_l1_a_re', 'delta_l1_a_im', 'delta_l1_b_re', 'delta_l1_b_im', 'delta_l1_c_re', 'delta_l1_c_im', 'delta_l1_d', 'delta_l1_w_glu', 'delta_l1_b_glu', 'delta_l1_ffn_norm', 'delta_l1_w_up', 'delta_l1_ffn_conv_w', 'delta_l1_ffn_conv_b', 'delta_l1_w_down', 'delta_final_norm', 'new_m_l0_mix_norm', 'new_m_l0_w_in', 'new_m_l0_conv_w', 'new_m_l0_conv_b', 'new_m_l0_conv_ln_g', 'new_m_l0_conv_ln_b', 'new_m_l0_q_norm', 'new_m_l0_kv_norm', 'new_m_l0_w_uq', 'new_m_l0_w_ukv', 'new_m_l0_w_out', 'new_m_l0_ffn_norm', 'new_m_l0_w_up', 'new_m_l0_ffn_conv_w', 'new_m_l0_ffn_conv_b', 'new_m_l0_w_down', 'new_m_l1_mix_norm', 'new_m_l1_w_in', 'new_m_l1_log_dt', 'new_m_l1_a_re', 'new_m_l1_a_im', 'new_m_l1_b_re', 'new_m_l1_b_im', 'new_m_l1_c_re', 'new_m_l1_c_im', 'new_m_l1_d', 'new_m_l1_w_glu', 'new_m_l1_b_glu', 'new_m_l1_ffn_norm', 'new_m_l1_w_up', 'new_m_l1_ffn_conv_w', 'new_m_l1_ffn_conv_b', 'new_m_l1_w_down', 'new_m_final_norm', 'new_v_l0_mix_norm', 'new_v_l0_w_in', 'new_v_l0_conv_w', 'new_v_l0_conv_b', 'new_v_l0_conv_ln_g', 'new_v_l0_conv_ln_b', 'new_v_l0_q_norm', 'new_v_l0_kv_norm', 'new_v_l0_w_uq', 'new_v_l0_w_ukv', 'new_v_l0_w_out', 'new_v_l0_ffn_norm', 'new_v_l0_w_up', 'new_v_l0_ffn_conv_w', 'new_v_l0_ffn_conv_b', 'new_v_l0_w_down', 'new_v_l1_mix_norm', 'new_v_l1_w_in', 'new_v_l1_log_dt', 'new_v_l1_a_re', 'new_v_l1_a_im', 'new_v_l1_b_re', 'new_v_l1_b_im', 'new_v_l1_c_re', 'new_v_l1_c_im', 'new_v_l1_d', 'new_v_l1_w_glu', 'new_v_l1_b_glu', 'new_v_l1_ffn_norm', 'new_v_l1_w_up', 'new_v_l1_ffn_conv_w', 'new_v_l1_ffn_conv_b', 'new_v_l1_w_down', 'new_v_final_norm']
TWIN_LEAF_KINDS = {'loss': 'loss', 'grad_x': 'grad_x', 'grad_l0_mix_norm': 'grad_w', 'grad_l0_w_in': 'grad_w', 'grad_l0_conv_w': 'grad_w', 'grad_l0_conv_b': 'grad_w', 'grad_l0_conv_ln_g': 'grad_w', 'grad_l0_conv_ln_b': 'grad_w', 'grad_l0_q_norm': 'grad_w', 'grad_l0_kv_norm': 'grad_w', 'grad_l0_w_uq': 'grad_w', 'grad_l0_w_ukv': 'grad_w', 'grad_l0_w_out': 'grad_w', 'grad_l0_ffn_norm': 'grad_w', 'grad_l0_w_up': 'grad_w', 'grad_l0_ffn_conv_w': 'grad_w', 'grad_l0_ffn_conv_b': 'grad_w', 'grad_l0_w_down': 'grad_w', 'grad_l1_mix_norm': 'grad_w', 'grad_l1_w_in': 'grad_w', 'grad_l1_log_dt': 'grad_w', 'grad_l1_a_re': 'grad_w', 'grad_l1_a_im': 'grad_w', 'grad_l1_b_re': 'grad_w', 'grad_l1_b_im': 'grad_w', 'grad_l1_c_re': 'grad_w', 'grad_l1_c_im': 'grad_w', 'grad_l1_d': 'grad_w', 'grad_l1_w_glu': 'grad_w', 'grad_l1_b_glu': 'grad_w', 'grad_l1_ffn_norm': 'grad_w', 'grad_l1_w_up': 'grad_w', 'grad_l1_ffn_conv_w': 'grad_w', 'grad_l1_ffn_conv_b': 'grad_w', 'grad_l1_w_down': 'grad_w', 'grad_final_norm': 'grad_w', 'delta_l0_mix_norm': 'delta_w', 'delta_l0_w_in': 'delta_w', 'delta_l0_conv_w': 'delta_w', 'delta_l0_conv_b': 'delta_w', 'delta_l0_conv_ln_g': 'delta_w', 'delta_l0_conv_ln_b': 'delta_w', 'delta_l0_q_norm': 'delta_w', 'delta_l0_kv_norm': 'delta_w', 'delta_l0_w_uq': 'delta_w', 'delta_l0_w_ukv': 'delta_w', 'delta_l0_w_out': 'delta_w', 'delta_l0_ffn_norm': 'delta_w', 'delta_l0_w_up': 'delta_w', 'delta_l0_ffn_conv_w': 'delta_w', 'delta_l0_ffn_conv_b': 'delta_w', 'delta_l0_w_down': 'delta_w', 'delta_l1_mix_norm': 'delta_w', 'delta_l1_w_in': 'delta_w', 'delta_l1_log_dt': 'delta_w', 'delta_l1_a_re': 'delta_w', 'delta_l1_a_im': 'delta_w', 'delta_l1_b_re': 'delta_w', 'delta_l1_b_im': 'delta_w', 'delta_l1_c_re': 'delta_w', 'delta_l1_c_im': 'delta_w', 'delta_l1_d': 'delta_w', 'delta_l1_w_glu': 'delta_w', 'delta_l1_b_glu': 'delta_w', 'delta_l1_ffn_norm': 'delta_w', 'delta_l1_w_up': 'delta_w', 'delta_l1_ffn_conv_w': 'delta_w', 'delta_l1_ffn_conv_b': 'delta_w', 'delta_l1_w_down': 'delta_w', 'delta_final_norm': 'delta_w', 'new_m_l0_mix_norm': 'new_m', 'new_m_l0_w_in': 'new_m', 'new_m_l0_conv_w': 'new_m', 'new_m_l0_conv_b': 'new_m', 'new_m_l0_conv_ln_g': 'new_m', 'new_m_l0_conv_ln_b': 'new_m', 'new_m_l0_q_norm': 'new_m', 'new_m_l0_kv_norm': 'new_m', 'new_m_l0_w_uq': 'new_m', 'new_m_l0_w_ukv': 'new_m', 'new_m_l0_w_out': 'new_m', 'new_m_l0_ffn_norm': 'new_m', 'new_m_l0_w_up': 'new_m', 'new_m_l0_ffn_conv_w': 'new_m', 'new_m_l0_ffn_conv_b': 'new_m', 'new_m_l0_w_down': 'new_m', 'new_m_l1_mix_norm': 'new_m', 'new_m_l1_w_in': 'new_m', 'new_m_l1_log_dt': 'new_m', 'new_m_l1_a_re': 'new_m', 'new_m_l1_a_im': 'new_m', 'new_m_l1_b_re': 'new_m', 'new_m_l1_b_im': 'new_m', 'new_m_l1_c_re': 'new_m', 'new_m_l1_c_im': 'new_m', 'new_m_l1_d': 'new_m', 'new_m_l1_w_glu': 'new_m', 'new_m_l1_b_glu': 'new_m', 'new_m_l1_ffn_norm': 'new_m', 'new_m_l1_w_up': 'new_m', 'new_m_l1_ffn_conv_w': 'new_m', 'new_m_l1_ffn_conv_b': 'new_m', 'new_m_l1_w_down': 'new_m', 'new_m_final_norm': 'new_m', 'new_v_l0_mix_norm': 'new_v', 'new_v_l0_w_in': 'new_v', 'new_v_l0_conv_w': 'new_v', 'new_v_l0_conv_b': 'new_v', 'new_v_l0_conv_ln_g': 'new_v', 'new_v_l0_conv_ln_b': 'new_v', 'new_v_l0_q_norm': 'new_v', 'new_v_l0_kv_norm': 'new_v', 'new_v_l0_w_uq': 'new_v', 'new_v_l0_w_ukv': 'new_v', 'new_v_l0_w_out': 'new_v', 'new_v_l0_ffn_norm': 'new_v', 'new_v_l0_w_up': 'new_v', 'new_v_l0_ffn_conv_w': 'new_v', 'new_v_l0_ffn_conv_b': 'new_v', 'new_v_l0_w_down': 'new_v', 'new_v_l1_mix_norm': 'new_v', 'new_v_l1_w_in': 'new_v', 'new_v_l1_log_dt': 'new_v', 'new_v_l1_a_re': 'new_v', 'new_v_l1_a_im': 'new_v', 'new_v_l1_b_re': 'new_v', 'new_v_l1_b_im': 'new_v', 'new_v_l1_c_re': 'new_v', 'new_v_l1_c_im': 'new_v', 'new_v_l1_d': 'new_v', 'new_v_l1_w_glu': 'new_v', 'new_v_l1_b_glu': 'new_v', 'new_v_l1_ffn_norm': 'new_v', 'new_v_l1_w_up': 'new_v', 'new_v_l1_ffn_conv_w': 'new_v', 'new_v_l1_ffn_conv_b': 'new_v', 'new_v_l1_w_down': 'new_v', 'new_v_final_norm': 'new_v'}


def _forward(args):
    return _fwd_reference(*[args[k] for k in FWD_PARAMS])


def _output_shape():
    def fwd():
        inp = _fwd_setup_inputs(0)
        return _fwd_reference(*[inp[k] for k in FWD_PARAMS])
    out = _jax.eval_shape(fwd)
    return out.shape, out.dtype

N_MICROBATCH = 1
ADAM_LR = 0.001
ADAM_B1 = 0.9
ADAM_B2 = 0.999
ADAM_EPS = 1e-08
ADAM_WD = 0.01
ADAM_STEP = 10
PER_EXAMPLE_BATCH_AXIS = {'x': 0, 'loss_target': 0}
SHARED_INPUTS = []
_WEIGHT_DTYPES = {'l0_mix_norm': _jnp.float32, 'l0_w_in': _jnp.float32, 'l0_conv_w': _jnp.float32, 'l0_conv_b': _jnp.float32, 'l0_conv_ln_g': _jnp.float32, 'l0_conv_ln_b': _jnp.float32, 'l0_q_norm': _jnp.float32, 'l0_kv_norm': _jnp.float32, 'l0_w_uq': _jnp.float32, 'l0_w_ukv': _jnp.float32, 'l0_w_out': _jnp.float32, 'l0_ffn_norm': _jnp.float32, 'l0_w_up': _jnp.float32, 'l0_ffn_conv_w': _jnp.float32, 'l0_ffn_conv_b': _jnp.float32, 'l0_w_down': _jnp.float32, 'l1_mix_norm': _jnp.float32, 'l1_w_in': _jnp.float32, 'l1_log_dt': _jnp.float32, 'l1_a_re': _jnp.float32, 'l1_a_im': _jnp.float32, 'l1_b_re': _jnp.float32, 'l1_b_im': _jnp.float32, 'l1_c_re': _jnp.float32, 'l1_c_im': _jnp.float32, 'l1_d': _jnp.float32, 'l1_w_glu': _jnp.float32, 'l1_b_glu': _jnp.float32, 'l1_ffn_norm': _jnp.float32, 'l1_w_up': _jnp.float32, 'l1_ffn_conv_w': _jnp.float32, 'l1_ffn_conv_b': _jnp.float32, 'l1_w_down': _jnp.float32, 'final_norm': _jnp.float32}
MOMENT_SCALE = {'l0_mix_norm': 1.492044e-01, 'l0_w_in': 1.243546e-01, 'l0_conv_w': 1.765494e-01, 'l0_conv_b': 3.716635e-01, 'l0_conv_ln_g': 2.121399e-01, 'l0_conv_ln_b': 1.787636e-01, 'l0_q_norm': 7.209490e-02, 'l0_kv_norm': 1.559768e-01, 'l0_w_uq': 4.213037e-02, 'l0_w_ukv': 5.325034e-02, 'l0_w_out': 1.263720e-01, 'l0_ffn_norm': 2.081045e-01, 'l0_w_up': 8.267490e-02, 'l0_ffn_conv_w': 8.375524e-02, 'l0_ffn_conv_b': 8.289507e-02, 'l0_w_down': 1.361272e-01, 'l1_mix_norm': 8.313139e-02, 'l1_w_in': 1.126495e-01, 'l1_log_dt': 1.270366e+01, 'l1_a_re': 1.072710e-02, 'l1_a_im': 8.756847e-03, 'l1_b_re': 5.699295e-03, 'l1_b_im': 5.816674e-03, 'l1_c_re': 8.286954e-03, 'l1_c_im': 8.277600e-03, 'l1_d': 1.123296e-01, 'l1_w_glu': 5.652927e-02, 'l1_b_glu': 9.747325e-02, 'l1_ffn_norm': 1.584542e-01, 'l1_w_up': 6.649100e-02, 'l1_ffn_conv_w': 6.587838e-02, 'l1_ffn_conv_b': 6.556284e-02, 'l1_w_down': 1.080489e-01, 'final_norm': 6.392462e+01}


def _to_microbatches(a, axis):
    t = _jnp.moveaxis(a, axis, 0)
    t = t.reshape((N_MICROBATCH, t.shape[0] // N_MICROBATCH) + t.shape[1:])
    return _jnp.moveaxis(t, 1, axis + 1)


def setup_inputs(seed: int = 0) -> dict:
    inp = _fwd_setup_inputs(seed)
    key = _jax.random.fold_in(_jax.random.key(seed), 7919)
    shape, _ = _output_shape()
    out = dict(inp)
    out["loss_target"] = _jax.random.normal(_jax.random.fold_in(key, 0), shape, _jnp.float32)
    for i, name in enumerate(TWIN_WEIGHTS):
        w = inp[name].astype(_jnp.float32)
        if MOMENT_SCALE is None:
            s = _jnp.sqrt(_jnp.mean(_jnp.square(w)) + 1e-30)
        else:
            s = MOMENT_SCALE[name]
        km, kv = _jax.random.split(_jax.random.fold_in(key, i + 1))
        out[name] = w
        out["m_" + name] = s * _jax.random.normal(km, w.shape, _jnp.float32)
        out["v_" + name] = (s * s) * _jax.random.uniform(kv, w.shape, _jnp.float32, 0.5, 1.5)
    if N_MICROBATCH > 1:
        for name, axis in PER_EXAMPLE_BATCH_AXIS.items():
            out[name] = _to_microbatches(out[name], axis)
    return {'x': out['x'], 'l0_mix_norm': out['l0_mix_norm'], 'l0_w_in': out['l0_w_in'], 'l0_conv_w': out['l0_conv_w'], 'l0_conv_b': out['l0_conv_b'], 'l0_conv_ln_g': out['l0_conv_ln_g'], 'l0_conv_ln_b': out['l0_conv_ln_b'], 'l0_q_norm': out['l0_q_norm'], 'l0_kv_norm': out['l0_kv_norm'], 'l0_w_uq': out['l0_w_uq'], 'l0_w_ukv': out['l0_w_ukv'], 'l0_w_out': out['l0_w_out'], 'l0_ffn_norm': out['l0_ffn_norm'], 'l0_w_up': out['l0_w_up'], 'l0_ffn_conv_w': out['l0_ffn_conv_w'], 'l0_ffn_conv_b': out['l0_ffn_conv_b'], 'l0_w_down': out['l0_w_down'], 'l1_mix_norm': out['l1_mix_norm'], 'l1_w_in': out['l1_w_in'], 'l1_log_dt': out['l1_log_dt'], 'l1_a_re': out['l1_a_re'], 'l1_a_im': out['l1_a_im'], 'l1_b_re': out['l1_b_re'], 'l1_b_im': out['l1_b_im'], 'l1_c_re': out['l1_c_re'], 'l1_c_im': out['l1_c_im'], 'l1_d': out['l1_d'], 'l1_w_glu': out['l1_w_glu'], 'l1_b_glu': out['l1_b_glu'], 'l1_ffn_norm': out['l1_ffn_norm'], 'l1_w_up': out['l1_w_up'], 'l1_ffn_conv_w': out['l1_ffn_conv_w'], 'l1_ffn_conv_b': out['l1_ffn_conv_b'], 'l1_w_down': out['l1_w_down'], 'final_norm': out['final_norm'], 'loss_target': out['loss_target'], 'm_l0_mix_norm': out['m_l0_mix_norm'], 'm_l0_w_in': out['m_l0_w_in'], 'm_l0_conv_w': out['m_l0_conv_w'], 'm_l0_conv_b': out['m_l0_conv_b'], 'm_l0_conv_ln_g': out['m_l0_conv_ln_g'], 'm_l0_conv_ln_b': out['m_l0_conv_ln_b'], 'm_l0_q_norm': out['m_l0_q_norm'], 'm_l0_kv_norm': out['m_l0_kv_norm'], 'm_l0_w_uq': out['m_l0_w_uq'], 'm_l0_w_ukv': out['m_l0_w_ukv'], 'm_l0_w_out': out['m_l0_w_out'], 'm_l0_ffn_norm': out['m_l0_ffn_norm'], 'm_l0_w_up': out['m_l0_w_up'], 'm_l0_ffn_conv_w': out['m_l0_ffn_conv_w'], 'm_l0_ffn_conv_b': out['m_l0_ffn_conv_b'], 'm_l0_w_down': out['m_l0_w_down'], 'm_l1_mix_norm': out['m_l1_mix_norm'], 'm_l1_w_in': out['m_l1_w_in'], 'm_l1_log_dt': out['m_l1_log_dt'], 'm_l1_a_re': out['m_l1_a_re'], 'm_l1_a_im': out['m_l1_a_im'], 'm_l1_b_re': out['m_l1_b_re'], 'm_l1_b_im': out['m_l1_b_im'], 'm_l1_c_re': out['m_l1_c_re'], 'm_l1_c_im': out['m_l1_c_im'], 'm_l1_d': out['m_l1_d'], 'm_l1_w_glu': out['m_l1_w_glu'], 'm_l1_b_glu': out['m_l1_b_glu'], 'm_l1_ffn_norm': out['m_l1_ffn_norm'], 'm_l1_w_up': out['m_l1_w_up'], 'm_l1_ffn_conv_w': out['m_l1_ffn_conv_w'], 'm_l1_ffn_conv_b': out['m_l1_ffn_conv_b'], 'm_l1_w_down': out['m_l1_w_down'], 'm_final_norm': out['m_final_norm'], 'v_l0_mix_norm': out['v_l0_mix_norm'], 'v_l0_w_in': out['v_l0_w_in'], 'v_l0_conv_w': out['v_l0_conv_w'], 'v_l0_conv_b': out['v_l0_conv_b'], 'v_l0_conv_ln_g': out['v_l0_conv_ln_g'], 'v_l0_conv_ln_b': out['v_l0_conv_ln_b'], 'v_l0_q_norm': out['v_l0_q_norm'], 'v_l0_kv_norm': out['v_l0_kv_norm'], 'v_l0_w_uq': out['v_l0_w_uq'], 'v_l0_w_ukv': out['v_l0_w_ukv'], 'v_l0_w_out': out['v_l0_w_out'], 'v_l0_ffn_norm': out['v_l0_ffn_norm'], 'v_l0_w_up': out['v_l0_w_up'], 'v_l0_ffn_conv_w': out['v_l0_ffn_conv_w'], 'v_l0_ffn_conv_b': out['v_l0_ffn_conv_b'], 'v_l0_w_down': out['v_l0_w_down'], 'v_l1_mix_norm': out['v_l1_mix_norm'], 'v_l1_w_in': out['v_l1_w_in'], 'v_l1_log_dt': out['v_l1_log_dt'], 'v_l1_a_re': out['v_l1_a_re'], 'v_l1_a_im': out['v_l1_a_im'], 'v_l1_b_re': out['v_l1_b_re'], 'v_l1_b_im': out['v_l1_b_im'], 'v_l1_c_re': out['v_l1_c_re'], 'v_l1_c_im': out['v_l1_c_im'], 'v_l1_d': out['v_l1_d'], 'v_l1_w_glu': out['v_l1_w_glu'], 'v_l1_b_glu': out['v_l1_b_glu'], 'v_l1_ffn_norm': out['v_l1_ffn_norm'], 'v_l1_w_up': out['v_l1_w_up'], 'v_l1_ffn_conv_w': out['v_l1_ffn_conv_w'], 'v_l1_ffn_conv_b': out['v_l1_ffn_conv_b'], 'v_l1_w_down': out['v_l1_w_down'], 'v_final_norm': out['v_final_norm']}


def _loss(weights, diff, rest, loss_target):
    with _jax.named_scope("forward"):
        args = {**rest, TWIN_DIFF_INPUT: diff, **{k: w.astype(_WEIGHT_DTYPES[k]) for k, w in weights.items()}}
        y = _forward(args)
    with _jax.named_scope("loss_head"):
        err = _jnp.square(y.astype(_jnp.float32) - loss_target)
        return 0.5 * _jnp.sum(_jnp.mean(err, axis=-1)) if err.ndim else 0.5 * err


def _adamw(w, g, m, v):
    m = ADAM_B1 * m + (1.0 - ADAM_B1) * g
    v = ADAM_B2 * v + (1.0 - ADAM_B2) * _jnp.square(g)
    m_hat = m / (1.0 - ADAM_B1 ** ADAM_STEP)
    v_hat = v / (1.0 - ADAM_B2 ** ADAM_STEP)
    delta = -ADAM_LR * (m_hat / (_jnp.sqrt(v_hat) + ADAM_EPS) + ADAM_WD * w)
    return delta, m, v


def reference(x, l0_mix_norm, l0_w_in, l0_conv_w, l0_conv_b, l0_conv_ln_g, l0_conv_ln_b, l0_q_norm, l0_kv_norm, l0_w_uq, l0_w_ukv, l0_w_out, l0_ffn_norm, l0_w_up, l0_ffn_conv_w, l0_ffn_conv_b, l0_w_down, l1_mix_norm, l1_w_in, l1_log_dt, l1_a_re, l1_a_im, l1_b_re, l1_b_im, l1_c_re, l1_c_im, l1_d, l1_w_glu, l1_b_glu, l1_ffn_norm, l1_w_up, l1_ffn_conv_w, l1_ffn_conv_b, l1_w_down, final_norm, loss_target, m_l0_mix_norm, m_l0_w_in, m_l0_conv_w, m_l0_conv_b, m_l0_conv_ln_g, m_l0_conv_ln_b, m_l0_q_norm, m_l0_kv_norm, m_l0_w_uq, m_l0_w_ukv, m_l0_w_out, m_l0_ffn_norm, m_l0_w_up, m_l0_ffn_conv_w, m_l0_ffn_conv_b, m_l0_w_down, m_l1_mix_norm, m_l1_w_in, m_l1_log_dt, m_l1_a_re, m_l1_a_im, m_l1_b_re, m_l1_b_im, m_l1_c_re, m_l1_c_im, m_l1_d, m_l1_w_glu, m_l1_b_glu, m_l1_ffn_norm, m_l1_w_up, m_l1_ffn_conv_w, m_l1_ffn_conv_b, m_l1_w_down, m_final_norm, v_l0_mix_norm, v_l0_w_in, v_l0_conv_w, v_l0_conv_b, v_l0_conv_ln_g, v_l0_conv_ln_b, v_l0_q_norm, v_l0_kv_norm, v_l0_w_uq, v_l0_w_ukv, v_l0_w_out, v_l0_ffn_norm, v_l0_w_up, v_l0_ffn_conv_w, v_l0_ffn_conv_b, v_l0_w_down, v_l1_mix_norm, v_l1_w_in, v_l1_log_dt, v_l1_a_re, v_l1_a_im, v_l1_b_re, v_l1_b_im, v_l1_c_re, v_l1_c_im, v_l1_d, v_l1_w_glu, v_l1_b_glu, v_l1_ffn_norm, v_l1_w_up, v_l1_ffn_conv_w, v_l1_ffn_conv_b, v_l1_w_down, v_final_norm):
    given = dict(x=x, l0_mix_norm=l0_mix_norm, l0_w_in=l0_w_in, l0_conv_w=l0_conv_w, l0_conv_b=l0_conv_b, l0_conv_ln_g=l0_conv_ln_g, l0_conv_ln_b=l0_conv_ln_b, l0_q_norm=l0_q_norm, l0_kv_norm=l0_kv_norm, l0_w_uq=l0_w_uq, l0_w_ukv=l0_w_ukv, l0_w_out=l0_w_out, l0_ffn_norm=l0_ffn_norm, l0_w_up=l0_w_up, l0_ffn_conv_w=l0_ffn_conv_w, l0_ffn_conv_b=l0_ffn_conv_b, l0_w_down=l0_w_down, l1_mix_norm=l1_mix_norm, l1_w_in=l1_w_in, l1_log_dt=l1_log_dt, l1_a_re=l1_a_re, l1_a_im=l1_a_im, l1_b_re=l1_b_re, l1_b_im=l1_b_im, l1_c_re=l1_c_re, l1_c_im=l1_c_im, l1_d=l1_d, l1_w_glu=l1_w_glu, l1_b_glu=l1_b_glu, l1_ffn_norm=l1_ffn_norm, l1_w_up=l1_w_up, l1_ffn_conv_w=l1_ffn_conv_w, l1_ffn_conv_b=l1_ffn_conv_b, l1_w_down=l1_w_down, final_norm=final_norm, loss_target=loss_target, m_l0_mix_norm=m_l0_mix_norm, m_l0_w_in=m_l0_w_in, m_l0_conv_w=m_l0_conv_w, m_l0_conv_b=m_l0_conv_b, m_l0_conv_ln_g=m_l0_conv_ln_g, m_l0_conv_ln_b=m_l0_conv_ln_b, m_l0_q_norm=m_l0_q_norm, m_l0_kv_norm=m_l0_kv_norm, m_l0_w_uq=m_l0_w_uq, m_l0_w_ukv=m_l0_w_ukv, m_l0_w_out=m_l0_w_out, m_l0_ffn_norm=m_l0_ffn_norm, m_l0_w_up=m_l0_w_up, m_l0_ffn_conv_w=m_l0_ffn_conv_w, m_l0_ffn_conv_b=m_l0_ffn_conv_b, m_l0_w_down=m_l0_w_down, m_l1_mix_norm=m_l1_mix_norm, m_l1_w_in=m_l1_w_in, m_l1_log_dt=m_l1_log_dt, m_l1_a_re=m_l1_a_re, m_l1_a_im=m_l1_a_im, m_l1_b_re=m_l1_b_re, m_l1_b_im=m_l1_b_im, m_l1_c_re=m_l1_c_re, m_l1_c_im=m_l1_c_im, m_l1_d=m_l1_d, m_l1_w_glu=m_l1_w_glu, m_l1_b_glu=m_l1_b_glu, m_l1_ffn_norm=m_l1_ffn_norm, m_l1_w_up=m_l1_w_up, m_l1_ffn_conv_w=m_l1_ffn_conv_w, m_l1_ffn_conv_b=m_l1_ffn_conv_b, m_l1_w_down=m_l1_w_down, m_final_norm=m_final_norm, v_l0_mix_norm=v_l0_mix_norm, v_l0_w_in=v_l0_w_in, v_l0_conv_w=v_l0_conv_w, v_l0_conv_b=v_l0_conv_b, v_l0_conv_ln_g=v_l0_conv_ln_g, v_l0_conv_ln_b=v_l0_conv_ln_b, v_l0_q_norm=v_l0_q_norm, v_l0_kv_norm=v_l0_kv_norm, v_l0_w_uq=v_l0_w_uq, v_l0_w_ukv=v_l0_w_ukv, v_l0_w_out=v_l0_w_out, v_l0_ffn_norm=v_l0_ffn_norm, v_l0_w_up=v_l0_w_up, v_l0_ffn_conv_w=v_l0_ffn_conv_w, v_l0_ffn_conv_b=v_l0_ffn_conv_b, v_l0_w_down=v_l0_w_down, v_l1_mix_norm=v_l1_mix_norm, v_l1_w_in=v_l1_w_in, v_l1_log_dt=v_l1_log_dt, v_l1_a_re=v_l1_a_re, v_l1_a_im=v_l1_a_im, v_l1_b_re=v_l1_b_re, v_l1_b_im=v_l1_b_im, v_l1_c_re=v_l1_c_re, v_l1_c_im=v_l1_c_im, v_l1_d=v_l1_d, v_l1_w_glu=v_l1_w_glu, v_l1_b_glu=v_l1_b_glu, v_l1_ffn_norm=v_l1_ffn_norm, v_l1_w_up=v_l1_w_up, v_l1_ffn_conv_w=v_l1_ffn_conv_w, v_l1_ffn_conv_b=v_l1_ffn_conv_b, v_l1_w_down=v_l1_w_down, v_final_norm=v_final_norm)
    weights = {n: given[n] for n in TWIN_WEIGHTS}
    shared = {n: given[n] for n in SHARED_INPUTS}
    per_example = {n: given[n] for n in ['x']}
    grad_fn = _jax.value_and_grad(_loss, argnums=(0, 1))

    def one_microbatch(ex, loss_target):
        ex = dict(ex)
        diff = ex.pop(TWIN_DIFF_INPUT)
        return grad_fn(weights, diff, {**shared, **ex}, loss_target)

    if N_MICROBATCH == 1:
        loss, (grad_w, grad_x) = one_microbatch(per_example, given["loss_target"])
    else:
        def body(carry, xs):
            loss_sum, grad_sum = carry
            l_k, (gw_k, gx_k) = one_microbatch(xs[0], xs[1])
            with _jax.named_scope("update"):
                return (loss_sum + l_k, _jax.tree.map(_jnp.add, grad_sum, gw_k)), gx_k

        init = (_jnp.zeros((), _jnp.float32), _jax.tree.map(_jnp.zeros_like, weights))
        (loss, grad_w), grad_x = _jax.lax.scan(body, init, (per_example, given["loss_target"]))
    with _jax.named_scope("update"):
        delta_w, new_m, new_v = {}, {}, {}
        for n in TWIN_WEIGHTS:
            delta_w[n], new_m[n], new_v[n] = _adamw(weights[n], grad_w[n], given["m_" + n], given["v_" + n])
    return (loss, grad_x, *[grad_w[n] for n in TWIN_WEIGHTS], *[delta_w[n] for n in TWIN_WEIGHTS],
            *[new_m[n] for n in TWIN_WEIGHTS], *[new_v[n] for n in TWIN_WEIGHTS])
```

```python
import functools

import jax
import jax.numpy as jnp
from jax import lax
from jax.experimental import pallas as pl
from jax.experimental.pallas import tpu as pltpu

F32 = jnp.float32
BF16 = jnp.bfloat16

N_DEV = 8
D_MODEL = 1024
EPS = 1e-6
LN_EPS = 1e-5
CONV_WIDTH = 512
CONV_K = 31
N_HEADS = 8
QK_NOPE = 64
QK_ROPE = 32
V_DIM = 64
HEAD_PAD = 128
Q_RANK = 256
KV_RANK = 128
ROPE_BASE = 10000.0
IN_EVEN = 2 * CONV_WIDTH + Q_RANK + KV_RANK + QK_ROPE
IN_PAD = 1536
KR_LANE = 64
SSM_WIDTH = 512
SSM_GROUP = 16
SSM_GROUPS = 32
SSM_STATE = 64
SSM_CH = SSM_GROUPS * SSM_STATE
D_FF = 2816
FFN_K = 3
ADAM_LR, ADAM_B1, ADAM_B2, ADAM_EPS, ADAM_WD, ADAM_STEP = 0.001, 0.9, 0.999, 1e-08, 0.01, 10

WEIGHTS = ['l0_mix_norm', 'l0_w_in', 'l0_conv_w', 'l0_conv_b', 'l0_conv_ln_g', 'l0_conv_ln_b', 'l0_q_norm',
           'l0_kv_norm', 'l0_w_uq', 'l0_w_ukv', 'l0_w_out', 'l0_ffn_norm', 'l0_w_up', 'l0_ffn_conv_w',
           'l0_ffn_conv_b', 'l0_w_down', 'l1_mix_norm', 'l1_w_in', 'l1_log_dt', 'l1_a_re', 'l1_a_im', 'l1_b_re',
           'l1_b_im', 'l1_c_re', 'l1_c_im', 'l1_d', 'l1_w_glu', 'l1_b_glu', 'l1_ffn_norm', 'l1_w_up',
           'l1_ffn_conv_w', 'l1_ffn_conv_b', 'l1_w_down', 'final_norm']
SHARDED = {'l0_w_in': 1, 'l0_conv_w': 1, 'l0_w_uq': 1, 'l0_w_ukv': 1, 'l0_w_out': 0, 'l0_w_up': 1,
           'l0_ffn_conv_w': 1, 'l0_w_down': 0, 'l1_w_in': 0, 'l1_w_glu': 1, 'l1_w_up': 1, 'l1_ffn_conv_w': 1,
           'l1_w_down': 0}
SMALL_SHARDED = ('l0_conv_w', 'l0_ffn_conv_w', 'l1_ffn_conv_w')
REPLICATED = [n for n in WEIGHTS if n not in SHARDED]


def _pick(n, cands):
    for c in cands:
        if n % c == 0:
            return c
    raise ValueError(f"no tile for {n}")


def _mm(a, b, *, ta=False, tb=False, res=None, out_dtype=F32, name):
    m, kd = (a.shape[1], a.shape[0]) if ta else a.shape
    kd2, n = (b.shape[1], b.shape[0]) if tb else b.shape
    assert kd == kd2, (a.shape, b.shape, ta, tb)
    tm = _pick(m, (1024, 512, 256, 128))
    tn = _pick(n, (512, 384, 256, 128))
    tk = _pick(kd, (1024, 512, 256, 128))
    nk = kd // tk
    dn = (((0 if ta else 1,), (1 if tb else 0,)), ((), ()))

    def body(*refs):
        if res is None:
            a_ref, b_ref, o_ref, acc_ref = refs
            r_ref = None
        else:
            a_ref, b_ref, r_ref, o_ref, acc_ref = refs
        k = pl.program_id(2)
        p = lax.dot_general(a_ref[...].astype(BF16), b_ref[...].astype(BF16), dn, preferred_element_type=F32)

        @pl.when(k == 0)
        def _():
            acc_ref[...] = p

        @pl.when(k > 0)
        def _():
            acc_ref[...] += p

        @pl.when(k == nk - 1)
        def _():
            out = acc_ref[...]
            if r_ref is not None:
                out = out + r_ref[...]
            o_ref[...] = out.astype(out_dtype)

    a_spec = pl.BlockSpec((tk, tm), lambda i, j, k: (k, i)) if ta else pl.BlockSpec((tm, tk), lambda i, j, k: (i, k))
    b_spec = pl.BlockSpec((tn, tk), lambda i, j, k: (j, k)) if tb else pl.BlockSpec((tk, tn), lambda i, j, k: (k, j))
    o_spec = pl.BlockSpec((tm, tn), lambda i, j, k: (i, j))
    in_specs, args = [a_spec, b_spec], [a, b]
    if res is not None:
        in_specs.append(o_spec)
        args.append(res)
    return pl.pallas_call(
        body, grid=(m // tm, n // tn, nk), in_specs=in_specs, out_specs=o_spec,
        out_shape=jax.ShapeDtypeStruct((m, n), out_dtype), scratch_shapes=[pltpu.VMEM((tm, tn), F32)],
        compiler_params=pltpu.CompilerParams(dimension_semantics=("parallel", "parallel", "arbitrary")),
        name=name)(*args)


def _rowwise(fn, rows, bcasts, row_outs, red_outs, *, ts, name):
    s = rows[0][0].shape[0]
    nr, nb, nro, nre = len(rows), len(bcasts), len(row_outs), len(red_outs)

    def body(*refs):
        i = pl.program_id(0)
        outs = fn(*[r[...] for r in refs[:nr + nb]])
        if not isinstance(outs, (tuple, list)):
            outs = (outs,)
        o_refs = refs[nr + nb:]
        for q in range(nro):
            o_refs[q][...] = outs[q].astype(o_refs[q].dtype)
        for q in range(nro, nro + nre):
            @pl.when(i == 0)
            def _(q=q):
                o_refs[q][...] = outs[q]

            @pl.when(i > 0)
            def _(q=q):
                o_refs[q][...] += outs[q]

    in_specs = [pl.BlockSpec((ts, w), functools.partial(lambda i, cb: (i, cb), cb=cb)) for (_, w, cb) in rows]
    in_specs += [pl.BlockSpec(b.shape, functools.partial(lambda i, nd: (0,) * nd, nd=b.ndim)) for b in bcasts]
    out_specs = [pl.BlockSpec((ts, w), lambda i: (i, 0)) for (w, _) in row_outs]
    out_specs += [pl.BlockSpec((r, w), lambda i: (0, 0)) for (r, w) in red_outs]
    out_shape = [jax.ShapeDtypeStruct((s, w), dt) for (w, dt) in row_outs]
    out_shape += [jax.ShapeDtypeStruct((r, w), F32) for (r, w) in red_outs]
    return pl.pallas_call(
        body, grid=(s // ts,), in_specs=in_specs, out_specs=out_specs, out_shape=out_shape,
        compiler_params=pltpu.CompilerParams(dimension_semantics=("arbitrary",)), name=name,
    )(*[r[0] for r in rows], *bcasts)


def _rms(x, g):
    return x * lax.rsqrt(jnp.mean(x * x, axis=-1, keepdims=True) + EPS) * g


def _rms_fwd(x, g, name):
    return _rowwise(lambda xv, gv: _rms(xv, gv), [(x, x.shape[1], 0)], [g.reshape(1, -1)],
                    [(x.shape[1], BF16)], [], ts=512, name=name)[0]


def _rms_bwd(x, g, dxn, dres, name):
    d = x.shape[1]

    def fn(xv, dv, rv, gv):
        _, vjp = jax.vjp(_rms, xv, gv)
        dx, dg = vjp(dv.astype(F32))
        return rv + dx, dg

    return _rowwise(fn, [(x, d, 0), (dxn, d, 0), (dres, d, 0)], [g.reshape(1, -1)], [(d, F32)], [(1, d)],
                    ts=512, name=name)


def _conv_fwd(xins, w, woffs, params, poffs, pre, post, outs, *, k_taps, hb, ts, tc, n_cb, name):
    s = xins[0][0].shape[0]
    n_s, nx, ncv, npar, no = s // ts, len(xins), len(woffs), len(params), len(outs)
    rpb = ts // hb

    def body(*refs):
        mains, halos = refs[:nx], refs[nx:2 * nx]
        w_refs = refs[2 * nx:2 * nx + ncv]
        p_refs = refs[2 * nx + ncv:2 * nx + ncv + npar]
        o_refs = refs[2 * nx + ncv + npar:2 * nx + ncv + npar + no]
        u_s = refs[2 * nx + ncv + npar + no:]
        i = pl.program_id(1)
        um = pre(*[r[...].astype(F32) for r in mains])
        uh = pre(*[r[...].astype(F32) for r in halos])
        first = (i > 0).astype(F32)
        cs = []
        for q in range(ncv):
            u_s[q][pl.ds(0, hb), :] = uh[q] * first
            u_s[q][pl.ds(hb, ts), :] = um[q]
            acc = jnp.zeros((ts, tc), F32)
            for t in range(k_taps):
                acc = acc + w_refs[q][pl.ds(t, 1), :] * u_s[q][pl.ds(hb - (k_taps - 1) + t, ts), :]
            cs.append(acc)
        res = post(cs, [r[...] for r in p_refs])
        for q in range(no):
            o_refs[q][...] = res[q].astype(o_refs[q].dtype)

    in_specs = [pl.BlockSpec((ts, tc), functools.partial(lambda jc, i, off: (i, off + jc), off=off)) for _, off in xins]
    in_specs += [pl.BlockSpec((hb, tc), functools.partial(lambda jc, i, off: (jnp.maximum(i * rpb - 1, 0), off + jc), off=off))
                 for _, off in xins]
    in_specs += [pl.BlockSpec((w.shape[0], tc), functools.partial(lambda jc, i, off: (0, off + jc), off=off)) for off in woffs]
    in_specs += [pl.BlockSpec((1, tc), functools.partial(lambda jc, i, off: (0, off + jc), off=off)) for off in poffs]
    out_specs = [pl.BlockSpec((ts, tc), lambda jc, i: (i, jc)) for _ in outs]
    out_shape = [jax.ShapeDtypeStruct((s, wd), dt) for wd, dt in outs]
    return pl.pallas_call(
        body, grid=(n_cb, n_s), in_specs=in_specs, out_specs=out_specs, out_shape=out_shape,
        scratch_shapes=[pltpu.VMEM((hb + ts, tc), F32) for _ in range(ncv)],
        compiler_params=pltpu.CompilerParams(dimension_semantics=("parallel", "arbitrary")), name=name,
    )(*[a for a, _ in xins], *[a for a, _ in xins], *([w] * ncv), *params)


def _conv_bwd(xins, w, woffs, params, poffs, pre, post, douts, dx_dtype, *, k_taps, hb, ts, tc, n_cb, name):
    s = xins[0][0].shape[0]
    n_s, nx, ncv, npar, ndo = s // ts, len(xins), len(woffs), len(params), len(douts)
    rpb = ts // hb
    n_hb = s // hb
    kp = w.shape[0]

    def body(*refs):
        pos = 0

        def take(n):
            nonlocal pos
            out = refs[pos:pos + n]
            pos += n
            return out

        mains, prevs, nexts = take(nx), take(nx), take(nx)
        d_mains, d_nexts = take(ndo), take(ndo)
        w_refs, p_refs = take(ncv), take(npar)
        dx_refs, dw_refs, dp_refs = take(nx), take(ncv), take(npar)
        u_s, dc_s = take(ncv), take(ncv)
        i = pl.program_id(1)
        xm = [r[...].astype(F32) for r in mains]
        um = pre(*xm)
        up = pre(*[r[...].astype(F32) for r in prevs])
        un = pre(*[r[...].astype(F32) for r in nexts])
        first = (i > 0).astype(F32)
        last = (i < n_s - 1).astype(F32)
        pv = [r[...] for r in p_refs]
        c_main, c_next = [], []
        for q in range(ncv):
            u_s[q][pl.ds(0, hb), :] = up[q] * first
            u_s[q][pl.ds(hb, ts), :] = um[q]
            u_s[q][pl.ds(hb + ts, hb), :] = un[q]
            acc = jnp.zeros((ts + hb, tc), F32)
            for t in range(k_taps):
                acc = acc + w_refs[q][pl.ds(t, 1), :] * u_s[q][pl.ds(hb - (k_taps - 1) + t, ts + hb), :]
            c_main.append(acc[:ts])
            c_next.append(acc[ts:])
        _, vjp_m = jax.vjp(lambda c, p: tuple(post(c, p)), c_main, pv)
        dc_m, dpar = vjp_m(tuple(r[...].astype(F32) for r in d_mains))
        _, vjp_n = jax.vjp(lambda c: tuple(post(c, pv)), c_next)
        (dc_n,) = vjp_n(tuple(r[...].astype(F32) * last for r in d_nexts))
        dus = []
        for q in range(ncv):
            dc_s[q][pl.ds(0, ts), :] = dc_m[q]
            dc_s[q][pl.ds(ts, hb), :] = dc_n[q]

            @pl.when(i == 0)
            def _(q=q):
                dw_refs[q][...] = jnp.zeros((kp, tc), F32)

            acc = jnp.zeros((ts, tc), F32)
            for t in range(k_taps):
                acc = acc + w_refs[q][pl.ds(t, 1), :] * dc_s[q][pl.ds(k_taps - 1 - t, ts), :]
                dw_refs[q][pl.ds(t, 1), :] += jnp.sum(
                    dc_m[q] * u_s[q][pl.ds(hb - (k_taps - 1) + t, ts), :], axis=0, keepdims=True)
            dus.append(acc)
        _, vjp_p = jax.vjp(lambda *xv: tuple(pre(*xv)), *xm)
        dxs = vjp_p(tuple(dus))
        for q in range(nx):
            dx_refs[q][...] = dxs[q].astype(dx_refs[q].dtype)
        for q in range(npar):
            @pl.when(i == 0)
            def _(q=q):
                dp_refs[q][...] = dpar[q]

            @pl.when(i > 0)
            def _(q=q):
                dp_refs[q][...] += dpar[q]

    def main_spec(off):
        return pl.BlockSpec((ts, tc), functools.partial(lambda jc, i, off: (i, off + jc), off=off))

    def prev_spec(off):
        return pl.BlockSpec((hb, tc), functools.partial(lambda jc, i, off: (jnp.maximum(i * rpb - 1, 0), off + jc), off=off))

    def next_spec(off):
        return pl.BlockSpec((hb, tc), functools.partial(lambda jc, i, off: (jnp.minimum((i + 1) * rpb, n_hb - 1), off + jc), off=off))

    in_specs = [main_spec(off) for _, off in xins] + [prev_spec(off) for _, off in xins] + [next_spec(off) for _, off in xins]
    in_specs += [main_spec(off) for _, off in douts] + [next_spec(off) for _, off in douts]
    in_specs += [pl.BlockSpec((kp, tc), functools.partial(lambda jc, i, off: (0, off + jc), off=off)) for off in woffs]
    in_specs += [pl.BlockSpec((1, tc), functools.partial(lambda jc, i, off: (0, off + jc), off=off)) for off in poffs]
    out_specs = [pl.BlockSpec((ts, tc), lambda jc, i: (i, jc)) for _ in xins]
    out_specs += [pl.BlockSpec((kp, tc), lambda jc, i: (0, jc)) for _ in woffs]
    out_specs += [pl.BlockSpec((1, tc), lambda jc, i: (0, jc)) for _ in params]
    width = n_cb * tc
    out_shape = [jax.ShapeDtypeStruct((s, width), dx_dtype) for _ in xins]
    out_shape += [jax.ShapeDtypeStruct((kp, width), F32) for _ in woffs]
    out_shape += [jax.ShapeDtypeStruct((1, width), F32) for _ in params]
    xa = [a for a, _ in xins]
    da = [a for a, _ in douts]
    return pl.pallas_call(
        body, grid=(n_cb, n_s), in_specs=in_specs, out_specs=out_specs, out_shape=out_shape,
        scratch_shapes=[pltpu.VMEM((hb + ts + hb, tc), F32) for _ in range(ncv)]
        + [pltpu.VMEM((ts + hb, tc), F32) for _ in range(ncv)],
        compiler_params=pltpu.CompilerParams(dimension_semantics=("parallel", "arbitrary")), name=name,
    )(*xa, *xa, *xa, *da, *da, *([w] * ncv), *params)


def _glu_pre(a, g):
    return [a * jax.nn.sigmoid(g)]


def _ln_silu_post(cs, ps):
    c = cs[0] + ps[0]
    mu = jnp.mean(c, axis=-1, keepdims=True)
    var = jnp.mean(jnp.square(c - mu), axis=-1, keepdims=True)
    y = (c - mu) * lax.rsqrt(var + LN_EPS) * ps[1] + ps[2]
    return [jax.nn.silu(y)]


def _id_pre(a, b):
    return [a, b]


def _gate_post(cs, ps):
    return [jax.nn.silu(cs[0] + ps[0]) * (cs[1] + ps[1])]


CONVA = dict(k_taps=CONV_K, hb=32, ts=512, tc=CONV_WIDTH, n_cb=1)
GATE_TC = 256
GATE = dict(k_taps=FFN_K, hb=8, ts=512, tc=GATE_TC, n_cb=D_FF // GATE_TC)
GATE_OFF = D_FF // GATE_TC


def _rope_tables(s):
    half = QK_ROPE // 2
    inv = ROPE_BASE ** (-jnp.arange(half, dtype=F32) / half)
    ang = jnp.arange(s, dtype=F32)[:, None] * inv[None, :]
    cos, sin = jnp.cos(ang), jnp.sin(ang)
    z = lambda n: jnp.zeros((s, n), F32)
    c = jnp.concatenate([jnp.ones((s, QK_NOPE), F32), cos, cos, z(HEAD_PAD - QK_NOPE - QK_ROPE)], axis=1)
    s1 = jnp.concatenate([z(QK_NOPE), -sin, z(HEAD_PAD - QK_NOPE - half)], axis=1)
    s2 = jnp.concatenate([z(QK_NOPE + half), sin, z(HEAD_PAD - QK_NOPE - QK_ROPE)], axis=1)
    return c, s1, s2


def _rot(t, c, s1, s2):
    half = QK_ROPE // 2
    return t * c + pltpu.roll(t, HEAD_PAD - half, 1) * s1 + pltpu.roll(t, half, 1) * s2


def _rot_t(d, c, s1, s2):
    half = QK_ROPE // 2
    return d * c + pltpu.roll(d * s1, half, 1) + pltpu.roll(d * s2, HEAD_PAD - half, 1)


def _heads(v):
    return [v[:, h * HEAD_PAD:(h + 1) * HEAD_PAD] for h in range(N_HEADS)]


def _rope_fwd(qf, kvf, hmat, tabs, name):
    w = N_HEADS * HEAD_PAD

    def fn(q, k, v, kr, c, s1, s2):
        krr = _rot(kr, c, s1, s2)
        qo = jnp.concatenate([_rot(t, c, s1, s2) for t in _heads(q)], axis=1)
        ko = jnp.concatenate([t + krr for t in _heads(k)], axis=1)
        return qo, ko, v

    rows = [(qf, w, 0), (kvf, w, 0), (kvf, w, 1), (hmat, HEAD_PAD, IN_PAD // HEAD_PAD - 1)]
    rows += [(t, HEAD_PAD, 0) for t in tabs]
    return _rowwise(fn, rows, [], [(w, BF16)] * 3, [], ts=512, name=name)


def _rope_bwd(dq, dk, dv, tabs, name):
    w = N_HEADS * HEAD_PAD

    def fn(dqv, dkv, dvv, c, s1, s2):
        dqo = jnp.concatenate([_rot_t(t, c, s1, s2) for t in _heads(dqv)], axis=1)
        ksum = functools.reduce(lambda a, b: a + b, _heads(dkv))
        return dqo, jnp.concatenate([dkv, dvv], axis=1), _rot_t(ksum, c, s1, s2)

    rows = [(dq, w, 0), (dk, w, 0), (dv, w, 0)] + [(t, HEAD_PAD, 0) for t in tabs]
    return _rowwise(fn, rows, [], [(w, BF16), (2 * w, BF16), (HEAD_PAD, F32)], [], ts=512, name=name)


ATT_T = 512
ATT_SCALE = (QK_NOPE + QK_ROPE) ** -0.5


def _causal_mask(sc, qi, kj, t, transposed):
    r = lax.broadcasted_iota(jnp.int32, (t, t), 0)
    c = lax.broadcasted_iota(jnp.int32, (t, t), 1)
    ok = (kj * t + r <= qi * t + c) if transposed else (kj * t + c <= qi * t + r)
    return jnp.where(ok, sc, -jnp.inf)


def _nt(a, b):
    return lax.dot_general(a, b, (((1,), (1,)), ((), ())), preferred_element_type=F32)


def _attn_fwd(q, k, v, name):
    s = q.shape[0]
    t = min(ATT_T, s)
    n = s // t

    def body(q_ref, k_ref, v_ref, o_ref, lse_ref, m_s, l_s, acc_s):
        i, j = pl.program_id(1), pl.program_id(2)

        @pl.when(j == 0)
        def _():
            m_s[...] = jnp.full((t, 1), -jnp.inf, F32)
            l_s[...] = jnp.zeros((t, 1), F32)
            acc_s[...] = jnp.zeros((t, HEAD_PAD), F32)

        @pl.when(j <= i)
        def _():
            sc = _causal_mask(_nt(q_ref[...], k_ref[...]) * ATT_SCALE, i, j, t, False)
            m_new = jnp.maximum(m_s[...], jnp.max(sc, axis=-1, keepdims=True))
            alpha = jnp.exp(m_s[...] - m_new)
            p = jnp.exp(sc - m_new)
            l_s[...] = alpha * l_s[...] + jnp.sum(p, axis=-1, keepdims=True)
            acc_s[...] = alpha * acc_s[...] + jnp.dot(p.astype(BF16), v_ref[...], preferred_element_type=F32)
            m_s[...] = m_new

        @pl.when(j == n - 1)
        def _():
            o_ref[...] = (acc_s[...] / l_s[...]).astype(BF16)
            lse_ref[...] = jnp.broadcast_to(m_s[...] + jnp.log(l_s[...]), (t, HEAD_PAD))

    q_spec = pl.BlockSpec((t, HEAD_PAD), lambda h, i, j: (i, h))
    kv_spec = pl.BlockSpec((t, HEAD_PAD), lambda h, i, j: (jnp.minimum(i, j), h))
    return pl.pallas_call(
        body, grid=(N_HEADS, n, n), in_specs=[q_spec, kv_spec, kv_spec], out_specs=[q_spec, q_spec],
        out_shape=[jax.ShapeDtypeStruct(q.shape, BF16), jax.ShapeDtypeStruct(q.shape, F32)],
        scratch_shapes=[pltpu.VMEM((t, 1), F32), pltpu.VMEM((t, 1), F32), pltpu.VMEM((t, HEAD_PAD), F32)],
        compiler_params=pltpu.CompilerParams(dimension_semantics=("parallel", "parallel", "arbitrary")), name=name,
    )(q, k, v)


def _attn_bwd_q(q, k, v, cat, dcat, lse, name):
    s = q.shape[0]
    t = min(ATT_T, s)
    n = s // t
    ob = CONV_WIDTH // HEAD_PAD

    def body(q_ref, k_ref, v_ref, o_ref, do_ref, lse_ref, dq_ref, lser_ref, dltr_ref, dq_s, dl_s):
        i, j = pl.program_id(1), pl.program_id(2)

        @pl.when(j == 0)
        def _():
            dl_s[...] = jnp.sum(do_ref[...] * o_ref[...].astype(F32), axis=-1, keepdims=True)
            dq_s[...] = jnp.zeros((t, HEAD_PAD), F32)

        @pl.when(j <= i)
        def _():
            sc = _causal_mask(_nt(q_ref[...], k_ref[...]) * ATT_SCALE, i, j, t, False)
            p = jnp.exp(sc - lse_ref[:, 0:1])
            dp = _nt(do_ref[...].astype(BF16), v_ref[...])
            ds = p * (dp - dl_s[...]) * ATT_SCALE
            dq_s[...] += jnp.dot(ds.astype(BF16), k_ref[...], preferred_element_type=F32)

        @pl.when(j == n - 1)
        def _():
            dq_ref[...] = dq_s[...]
            lser_ref[...] = jnp.transpose(lse_ref[...])[0:8, :]
            dltr_ref[...] = jnp.transpose(jnp.broadcast_to(dl_s[...], (t, HEAD_PAD)))[0:8, :]

    q_spec = pl.BlockSpec((t, HEAD_PAD), lambda h, i, j: (i, h))
    kv_spec = pl.BlockSpec((t, HEAD_PAD), lambda h, i, j: (jnp.minimum(i, j), h))
    o_spec = pl.BlockSpec((t, HEAD_PAD), lambda h, i, j: (i, ob + h))
    row_spec = pl.BlockSpec((None, 8, t), lambda h, i, j: (h, 0, i))
    return pl.pallas_call(
        body, grid=(N_HEADS, n, n), in_specs=[q_spec, kv_spec, kv_spec, o_spec, o_spec, q_spec],
        out_specs=[q_spec, row_spec, row_spec],
        out_shape=[jax.ShapeDtypeStruct(q.shape, F32), jax.ShapeDtypeStruct((N_HEADS, 8, s), F32),
                   jax.ShapeDtypeStruct((N_HEADS, 8, s), F32)],
        scratch_shapes=[pltpu.VMEM((t, HEAD_PAD), F32), pltpu.VMEM((t, 1), F32)],
        compiler_params=pltpu.CompilerParams(dimension_semantics=("parallel", "parallel", "arbitrary")), name=name,
    )(q, k, v, cat, dcat, lse)


def _attn_bwd_kv(q, k, v, dcat, lse_r, dlt_r, name):
    s = q.shape[0]
    t = min(ATT_T, s)
    n = s // t
    ob = CONV_WIDTH // HEAD_PAD

    def body(k_ref, v_ref, q_ref, do_ref, lse_ref, dl_ref, dk_ref, dv_ref, dk_s, dv_s):
        j, i = pl.program_id(1), pl.program_id(2)

        @pl.when(i == 0)
        def _():
            dk_s[...] = jnp.zeros((t, HEAD_PAD), F32)
            dv_s[...] = jnp.zeros((t, HEAD_PAD), F32)

        @pl.when(i >= j)
        def _():
            do = do_ref[...].astype(BF16)
            sc = _causal_mask(_nt(k_ref[...], q_ref[...]) * ATT_SCALE, i, j, t, True)
            p = jnp.exp(sc - lse_ref[0:1, :])
            dp = _nt(v_ref[...], do)
            ds = p * (dp - dl_ref[0:1, :]) * ATT_SCALE
            dv_s[...] += jnp.dot(p.astype(BF16), do, preferred_element_type=F32)
            dk_s[...] += jnp.dot(ds.astype(BF16), q_ref[...], preferred_element_type=F32)

        @pl.when(i == n - 1)
        def _():
            dk_ref[...] = dk_s[...]
            dv_ref[...] = dv_s[...]

    kv_spec = pl.BlockSpec((t, HEAD_PAD), lambda h, j, i: (j, h))
    q_spec = pl.BlockSpec((t, HEAD_PAD), lambda h, j, i: (jnp.maximum(i, j), h))
    do_spec = pl.BlockSpec((t, HEAD_PAD), lambda h, j, i: (jnp.maximum(i, j), ob + h))
    row_spec = pl.BlockSpec((None, 8, t), lambda h, j, i: (h, 0, jnp.maximum(i, j)))
    return pl.pallas_call(
        body, grid=(N_HEADS, n, n), in_specs=[kv_spec, kv_spec, q_spec, do_spec, row_spec, row_spec],
        out_specs=[kv_spec, kv_spec],
        out_shape=[jax.ShapeDtypeStruct(q.shape, F32), jax.ShapeDtypeStruct(q.shape, F32)],
        scratch_shapes=[pltpu.VMEM((t, HEAD_PAD), F32), pltpu.VMEM((t, HEAD_PAD), F32)],
        compiler_params=pltpu.CompilerParams(dimension_semantics=("parallel", "parallel", "arbitrary")), name=name,
    )(k, v, q, dcat, lse_r, dlt_r)


SCAN_T = 256
SCAN_C = 512


def _scan(b, lam, *, reverse, xs=None, name):
    s = b.shape[0]
    t = min(SCAN_T, s)
    n_t, n_c = s // t, SSM_CH // SCAN_C
    with_dlam = xs is not None

    def shift(a, d, row):
        if d >= 8:
            z = jnp.zeros((d, SCAN_C), F32)
            return jnp.concatenate([a[d:], z], axis=0) if reverse else jnp.concatenate([z, a[:t - d]], axis=0)
        if reverse:
            return jnp.where(row < t - d, pltpu.roll(a, t - d, 0), 0.0)
        return jnp.where(row >= d, pltpu.roll(a, d, 0), 0.0)

    def body(*refs):
        if with_dlam:
            br_ref, bi_ref, lr_ref, li_ref, xr_ref, xi_ref, or_ref, oi_ref, dlr_ref, dli_ref, cr_s, ci_s = refs
        else:
            br_ref, bi_ref, lr_ref, li_ref, or_ref, oi_ref, cr_s, ci_s = refs
        k = pl.program_id(1)

        @pl.when(k == 0)
        def _():
            cr_s[...] = jnp.zeros((1, SCAN_C), F32)
            ci_s[...] = jnp.zeros((1, SCAN_C), F32)

        row = lax.broadcasted_iota(jnp.int32, (t, SCAN_C), 0)
        lr = lr_ref[...]
        li = -li_ref[...] if reverse else li_ref[...]
        cr, ci = cr_s[...], ci_s[...]
        edge = (row == t - 1) if reverse else (row == 0)
        ar = br_ref[...] + jnp.where(edge, lr * cr - li * ci, 0.0)
        ai = bi_ref[...] + jnp.where(edge, lr * ci + li * cr, 0.0)
        d = 1
        while d < t:
            sr, si = shift(ar, d, row), shift(ai, d, row)
            ar, ai = ar + lr * sr - li * si, ai + lr * si + li * sr
            lr, li = lr * lr - li * li, 2.0 * lr * li
            d *= 2
        or_ref[...] = ar
        oi_ref[...] = ai
        if with_dlam:
            gr = jnp.where(edge, cr, shift(ar, 1, row))
            gi = jnp.where(edge, ci, shift(ai, 1, row))
            xr, xi = xr_ref[...], xi_ref[...]
            pr = jnp.sum(xr * gr + xi * gi, axis=0, keepdims=True)
            pi = jnp.sum(xr * gi - xi * gr, axis=0, keepdims=True)

            @pl.when(k == 0)
            def _():
                dlr_ref[...] = pr
                dli_ref[...] = pi

            @pl.when(k > 0)
            def _():
                dlr_ref[...] += pr
                dli_ref[...] += pi
        if reverse:
            cr_s[...] = ar[0:1, :]
            ci_s[...] = ai[0:1, :]
        else:
            cr_s[...] = ar[t - 1:t, :]
            ci_s[...] = ai[t - 1:t, :]

    tm = (lambda k: n_t - 1 - k) if reverse else (lambda k: k)
    re_spec = pl.BlockSpec((t, SCAN_C), lambda c, k: (tm(k), c))
    im_spec = pl.BlockSpec((t, SCAN_C), lambda c, k: (tm(k), c + n_c))
    lre_spec = pl.BlockSpec((1, SCAN_C), lambda c, k: (0, c))
    lim_spec = pl.BlockSpec((1, SCAN_C), lambda c, k: (0, c + n_c))
    in_specs = [re_spec, im_spec, lre_spec, lim_spec]
    args = [b, b, lam, lam]
    out_specs = [re_spec, re_spec]
    out_shape = [jax.ShapeDtypeStruct((s, SSM_CH), F32), jax.ShapeDtypeStruct((s, SSM_CH), F32)]
    if with_dlam:
        in_specs += [re_spec, im_spec]
        args += [xs, xs]
        out_specs += [lre_spec, lre_spec]
        out_shape += [jax.ShapeDtypeStruct((1, SSM_CH), F32), jax.ShapeDtypeStruct((1, SSM_CH), F32)]
    return pl.pallas_call(
        body, grid=(n_c, n_t), in_specs=in_specs, out_specs=out_specs, out_shape=out_shape,
        scratch_shapes=[pltpu.VMEM((1, SCAN_C), F32), pltpu.VMEM((1, SCAN_C), F32)],
        compiler_params=pltpu.CompilerParams(dimension_semantics=("parallel", "arbitrary")), name=name,
    )(*args)


def _s5_disc(log_dt, a_re, a_im, b_re, b_im):
    dt = jnp.exp(log_dt)[:, None]
    mag = jnp.exp(a_re * dt)
    lb_re, lb_im = mag * jnp.cos(a_im * dt), mag * jnp.sin(a_im * dt)
    den = a_re * a_re + a_im * a_im
    nr, ni = lb_re - 1.0, lb_im
    f_re = (nr * a_re + ni * a_im) / den
    f_im = (ni * a_re - nr * a_im) / den
    bb_re = f_re[..., None] * b_re - f_im[..., None] * b_im
    bb_im = f_re[..., None] * b_im + f_im[..., None] * b_re
    return lb_re, lb_im, bb_re, bb_im


def _bd(a):
    g, i, j = a.shape
    return jnp.einsum('gij,gh->gihj', a, jnp.eye(g, dtype=a.dtype)).reshape(g * i, g * j)


def _unbd(m, i, j):
    g = m.shape[0] // i
    return jnp.einsum('gihj,gh->gij', m.reshape(g, i, g, j), jnp.eye(g, dtype=m.dtype))


def _exchange(src, *, gather, name):
    shape = src.shape if not gather else (N_DEV,) + src.shape

    def body(src_ref, out_ref, send_sems, recv_sems, local_sem):
        x, y, c = lax.axis_index("x"), lax.axis_index("y"), lax.axis_index("c")
        me = 4 * x + 2 * y + c

        def piece(slot):
            return src_ref if gather else src_ref.at[slot]

        mine = pltpu.make_async_copy(piece(me), out_ref.at[me], local_sem)
        mine.start()
        copies = []
        for r in range(1, N_DEV):
            px, py, pc = x ^ (r >> 2), y ^ ((r >> 1) & 1), c ^ (r & 1)
            peer = 4 * px + 2 * py + pc
            copies.append(pltpu.make_async_remote_copy(
                src_ref=piece(peer), dst_ref=out_ref.at[me], send_sem=send_sems.at[r - 1],
                recv_sem=recv_sems.at[r - 1], device_id=(px, py, pc), device_id_type=pl.DeviceIdType.MESH))
        for cp in copies:
            cp.start()
        for cp in copies:
            cp.wait_recv()
        for cp in copies:
            cp.wait_send()
        mine.wait()

    return pl.pallas_call(
        body, out_shape=jax.ShapeDtypeStruct(shape, src.dtype),
        in_specs=[pl.BlockSpec(memory_space=pl.ANY)], out_specs=pl.BlockSpec(memory_space=pl.ANY),
        scratch_shapes=[pltpu.SemaphoreType.DMA((N_DEV - 1,)), pltpu.SemaphoreType.DMA((N_DEV - 1,)),
                        pltpu.SemaphoreType.DMA],
        compiler_params=pltpu.CompilerParams(has_side_effects=True), name=name,
    )(src)


FLAT_W = 512


def _adamw(parts, w, m, v, name):
    r = w.shape[0]
    ts = _pick(r, (256, 128, 64, 32, 16, 8))

    def body(p_ref, w_ref, m_ref, v_ref, g_ref, d_ref, nm_ref, nv_ref):
        g = p_ref[0]
        for d in range(1, N_DEV):
            g = g + p_ref[d]
        m2 = ADAM_B1 * m_ref[...] + (1.0 - ADAM_B1) * g
        v2 = ADAM_B2 * v_ref[...] + (1.0 - ADAM_B2) * jnp.square(g)
        m_hat = m2 / (1.0 - ADAM_B1 ** ADAM_STEP)
        v_hat = v2 / (1.0 - ADAM_B2 ** ADAM_STEP)
        g_ref[...] = g
        d_ref[...] = -ADAM_LR * (m_hat / (jnp.sqrt(v_hat) + ADAM_EPS) + ADAM_WD * w_ref[...])
        nm_ref[...] = m2
        nv_ref[...] = v2

    spec = pl.BlockSpec((ts, FLAT_W), lambda i: (i, 0))
    return pl.pallas_call(
        body, grid=(r // ts,), in_specs=[pl.BlockSpec((N_DEV, ts, FLAT_W), lambda i: (0, i, 0)), spec, spec, spec],
        out_specs=[spec] * 4, out_shape=[jax.ShapeDtypeStruct((r, FLAT_W), F32)] * 4,
        compiler_params=pltpu.CompilerParams(dimension_semantics=("parallel",)), name=name,
    )(parts, w, m, v)


def _flat(arrs, lead=()):
    nl = len(lead)
    v = jnp.concatenate([a.reshape(lead + (-1,)) for a in arrs], axis=nl)
    n = v.shape[nl]
    pad = (-n) % (8 * FLAT_W)
    v = jnp.pad(v, [(0, 0)] * nl + [(0, pad)])
    return v.reshape(lead + (-1, FLAT_W))


def _unflat(flat, shapes):
    v = flat.reshape(flat.shape[:-2] + (-1,))
    out, off = [], 0
    for sh in shapes:
        n = 1
        for d in sh:
            n *= d
        out.append(v[..., off:off + n].reshape(v.shape[:-1] + tuple(sh)))
        off += n
    return out


def _pad_cols(w, n):
    return jnp.pad(w, ((0, 0), (0, n - w.shape[1])))


def _prep_weights(p):
    q = {}
    w_in = p['l0_w_in']
    z = lambda n: jnp.zeros((D_MODEL, n), w_in.dtype)
    q['w_in'] = jnp.concatenate([w_in[:, :IN_EVEN - QK_ROPE], z(KR_LANE), w_in[:, IN_EVEN - QK_ROPE:],
                                 z(HEAD_PAD - KR_LANE - QK_ROPE)], axis=1)
    dqk = QK_NOPE + QK_ROPE
    q['w_uq'] = jnp.pad(p['l0_w_uq'].reshape(Q_RANK, N_HEADS, dqk), ((0, 0), (0, 0), (0, HEAD_PAD - dqk))
                        ).reshape(Q_RANK, N_HEADS * HEAD_PAD)
    ukv = p['l0_w_ukv'].reshape(KV_RANK, N_HEADS, 2, QK_NOPE)
    padh = lambda a: jnp.pad(a, ((0, 0), (0, 0), (0, HEAD_PAD - QK_NOPE))).reshape(KV_RANK, N_HEADS * HEAD_PAD)
    q['w_ukv'] = jnp.concatenate([padh(ukv[:, :, 0]), padh(ukv[:, :, 1])], axis=1)
    wo = p['l0_w_out']
    wo_a = jnp.pad(wo[CONV_WIDTH:].reshape(N_HEADS, V_DIM, D_MODEL), ((0, 0), (0, HEAD_PAD - V_DIM), (0, 0)))
    q['w_out'] = jnp.concatenate([wo[:CONV_WIDTH], wo_a.reshape(N_HEADS * HEAD_PAD, D_MODEL)], axis=0)
    return q


def _unprep_grads(g):
    out = {}
    d = g['w_in']
    out['l0_w_in'] = jnp.concatenate([d[:, :IN_EVEN - QK_ROPE],
                                      d[:, IN_EVEN - QK_ROPE + KR_LANE:IN_EVEN + KR_LANE]], axis=1)
    dqk = QK_NOPE + QK_ROPE
    out['l0_w_uq'] = g['w_uq'].reshape(Q_RANK, N_HEADS, HEAD_PAD)[:, :, :dqk].reshape(Q_RANK, N_HEADS * dqk)
    d = g['w_ukv'].reshape(KV_RANK, 2, N_HEADS, HEAD_PAD)[:, :, :, :QK_NOPE]
    out['l0_w_ukv'] = jnp.transpose(d, (0, 2, 1, 3)).reshape(KV_RANK, N_HEADS * 2 * QK_NOPE)
    d = g['w_out']
    da = d[CONV_WIDTH:].reshape(N_HEADS, HEAD_PAD, D_MODEL)[:, :V_DIM].reshape(N_HEADS * V_DIM, D_MODEL)
    out['l0_w_out'] = jnp.concatenate([d[:CONV_WIDTH], da], axis=0)
    return out


def _pad_taps(w):
    return jnp.pad(w, ((0, (-w.shape[0]) % 8), (0, 0)))


def _ffn_fwd(x, p, pre, tag):
    xn = _rms_fwd(x, p[pre + 'ffn_norm'], f"{tag}_ffn_norm")
    hu = _mm(xn, p[pre + 'w_up'], name=f"{tag}_ffn_up")
    wt = _pad_taps(p[pre + 'ffn_conv_w'])
    bias = p[pre + 'ffn_conv_b'].reshape(1, -1)
    (act,) = _conv_fwd([(hu, 0), (hu, GATE_OFF)], wt, [0, GATE_OFF], [bias, bias], [0, GATE_OFF],
                       _id_pre, _gate_post, [(D_FF, BF16)], name=f"{tag}_ffn_gate", **GATE)
    y = _mm(act, p[pre + 'w_down'], res=x, name=f"{tag}_ffn_down")
    return y, (x, xn, hu, act, wt, bias)


def _ffn_bwd(dy, saved, p, pre, tag, grads):
    x, xn, hu, act, wt, bias = saved
    dact = _mm(dy, p[pre + 'w_down'], tb=True, name=f"{tag}_ffn_down_dx")
    grads[pre + 'w_down'] = _mm(act, dy, ta=True, name=f"{tag}_ffn_down_dw")
    dh1, dh2, dw1, dw2, db1, db2 = _conv_bwd(
        [(hu, 0), (hu, GATE_OFF)], wt, [0, GATE_OFF], [bias, bias], [0, GATE_OFF], _id_pre, _gate_post,
        [(dact, 0)], BF16, name=f"{tag}_ffn_gate_bwd", **GATE)
    dhu = jnp.concatenate([dh1, dh2], axis=1)
    grads[pre + 'ffn_conv_w'] = jnp.concatenate([dw1, dw2], axis=1)[:FFN_K]
    grads[pre + 'ffn_conv_b'] = jnp.concatenate([db1, db2], axis=1).reshape(-1)
    dxn = _mm(dhu, p[pre + 'w_up'], tb=True, name=f"{tag}_ffn_up_dx")
    grads[pre + 'w_up'] = _mm(xn, dhu, ta=True, name=f"{tag}_ffn_up_dw")
    dx, dg = _rms_bwd(x, p[pre + 'ffn_norm'], dxn, dy, f"{tag}_ffn_norm_bwd")
    grads[pre + 'ffn_norm'] = dg.reshape(-1)
    return dx


def _mla_fwd(x, p, q, tabs):
    xn = _rms_fwd(x, p['l0_mix_norm'], "l0_mix_norm")
    hmat = _mm(xn, q['w_in'], name="l0_in")
    wt = _pad_taps(p['l0_conv_w'])
    cpar = [p['l0_conv_b'].reshape(1, -1), p['l0_conv_ln_g'].reshape(1, -1), p['l0_conv_ln_b'].reshape(1, -1)]
    (u,) = _conv_fwd([(hmat, 0), (hmat, 1)], wt, [0], cpar, [0, 0, 0], _glu_pre, _ln_silu_post,
                     [(CONV_WIDTH, BF16)], name="l0_conv", **CONVA)
    qn, kvn = p['l0_q_norm'].reshape(1, -1), p['l0_kv_norm'].reshape(1, -1)
    cqn, ckvn = _rowwise(lambda a, b, ga, gb: (_rms(a, ga), _rms(b, gb)),
                         [(hmat, Q_RANK, 2 * CONV_WIDTH // Q_RANK), (hmat, KV_RANK, (2 * CONV_WIDTH + Q_RANK) // KV_RANK)],
                         [qn, kvn], [(Q_RANK, BF16), (KV_RANK, BF16)], [], ts=512, name="l0_latent_norm")
    qf = _mm(cqn, q['w_uq'], name="l0_uq")
    kvf = _mm(ckvn, q['w_ukv'], name="l0_ukv")
    q_rot, k_full, v = _rope_fwd(qf, kvf, hmat, tabs, "l0_rope")
    o, lse = _attn_fwd(q_rot, k_full, v, "l0_attn")
    cat = jnp.concatenate([u, o], axis=1)
    y = _mm(cat, q['w_out'], res=x, name="l0_out")
    return y, (x, xn, hmat, wt, cpar, qn, kvn, cqn, ckvn, q_rot, k_full, v, lse, cat)


def _mla_bwd(dy, saved, p, q, tabs, grads, gq):
    x, xn, hmat, wt, cpar, qn, kvn, cqn, ckvn, q_rot, k_full, v, lse, cat = saved
    dcat = _mm(dy, q['w_out'], tb=True, name="l0_out_dx")
    gq['w_out'] = _mm(cat, dy, ta=True, name="l0_out_dw")
    dq, lse_r, dlt_r = _attn_bwd_q(q_rot, k_full, v, cat, dcat, lse, "l0_attn_bwd_q")
    dk, dv = _attn_bwd_kv(q_rot, k_full, v, dcat, lse_r, dlt_r, "l0_attn_bwd_kv")
    dqf, dkvf, dkr = _rope_bwd(dq, dk, dv, tabs, "l0_rope_bwd")
    dcqn = _mm(dqf, q['w_uq'], tb=True, name="l0_uq_dx")
    gq['w_uq'] = _mm(cqn, dqf, ta=True, name="l0_uq_dw")
    dckvn = _mm(dkvf, q['w_ukv'], tb=True, name="l0_ukv_dx")
    gq['w_ukv'] = _mm(ckvn, dkvf, ta=True, name="l0_ukv_dw")

    def lat_bwd(a, b, da, db, ga, gb):
        _, vjp = jax.vjp(lambda a_, b_, ga_, gb_: (_rms(a_, ga_), _rms(b_, gb_)), a, b, ga, gb)
        return vjp((da, db))

    dcq, dckv, dqn, dkvn = _rowwise(
        lat_bwd, [(hmat, Q_RANK, 2 * CONV_WIDTH // Q_RANK), (hmat, KV_RANK, (2 * CONV_WIDTH + Q_RANK) // KV_RANK),
                  (dcqn, Q_RANK, 0), (dckvn, KV_RANK, 0)],
        [qn, kvn], [(Q_RANK, F32), (KV_RANK, F32)], [(1, Q_RANK), (1, KV_RANK)], ts=512, name="l0_latent_norm_bwd")
    grads['l0_q_norm'], grads['l0_kv_norm'] = dqn.reshape(-1), dkvn.reshape(-1)
    da, dg, dwt, dcb, dlg, dlb = _conv_bwd(
        [(hmat, 0), (hmat, 1)], wt, [0], cpar, [0, 0, 0], _glu_pre, _ln_silu_post, [(dcat, 0)], F32,
        name="l0_conv_bwd", **CONVA)
    grads['l0_conv_w'] = dwt[:CONV_K]
    grads['l0_conv_b'], grads['l0_conv_ln_g'], grads['l0_conv_ln_b'] = dcb.reshape(-1), dlg.reshape(-1), dlb.reshape(-1)
    dh = jnp.concatenate([da, dg, dcq, dckv, dkr], axis=1)
    dxn = _mm(dh, q['w_in'], tb=True, name="l0_in_dx")
    gq['w_in'] = _mm(xn, dh, ta=True, name="l0_in_dw")
    dx, dgn = _rms_bwd(x, p['l0_mix_norm'], dxn, dy, "l0_mix_norm_bwd")
    grads['l0_mix_norm'] = dgn.reshape(-1)
    return dx


def _gelu_skip(yc, u, d):
    return jax.nn.gelu(yc + d * u)


def _glu_out(z1, z2, b1, b2, x):
    return x + (z1 + b1) * jax.nn.sigmoid(z2 + b2)


def _s5_fwd(x, p):
    xn = _rms_fwd(x, p['l1_mix_norm'], "l1_mix_norm")
    u = _mm(xn, p['l1_w_in'], name="l1_in")
    lb_re, lb_im, bb_re, bb_im = _s5_disc(p['l1_log_dt'], p['l1_a_re'], p['l1_a_im'], p['l1_b_re'], p['l1_b_im'])
    lam = jnp.concatenate([lb_re.reshape(1, -1), lb_im.reshape(1, -1)], axis=1)
    tr = lambda a: jnp.transpose(a, (0, 2, 1))
    bmat = jnp.concatenate([_bd(tr(bb_re)), _bd(tr(bb_im))], axis=1)
    cmat = jnp.concatenate([_bd(tr(p['l1_c_re'])), -_bd(tr(p['l1_c_im']))], axis=0)
    bu = _mm(u, bmat, name="l1_bu")
    xr, xi = _scan(bu, lam, reverse=False, name="l1_scan")
    xs = jnp.concatenate([xr, xi], axis=1)
    yc = _mm(xs, cmat, name="l1_cx")
    dsk = p['l1_d'].reshape(1, -1)
    (y,) = _rowwise(_gelu_skip, [(yc, SSM_WIDTH, 0), (u, SSM_WIDTH, 0)], [dsk], [(SSM_WIDTH, BF16)], [],
                    ts=512, name="l1_gelu")
    z = _mm(y, p['l1_w_glu'], name="l1_glu")
    bg = p['l1_b_glu'].reshape(1, -1)
    (out,) = _rowwise(lambda z1, z2, xv, b1, b2: _glu_out(z1, z2, b1, b2, xv),
                      [(z, D_MODEL, 0), (z, D_MODEL, 1), (x, D_MODEL, 0)], [bg[:, :D_MODEL], bg[:, D_MODEL:]],
                      [(D_MODEL, F32)], [], ts=512, name="l1_glu_out")
    return out, (x, xn, u, lam, bmat, cmat, xs, yc, dsk, y, z, bg)


def _s5_bwd(dy, saved, p, grads):
    x, xn, u, lam, bmat, cmat, xs, yc, dsk, y, z, bg = saved

    def glu_bwd(z1, z2, dv, b1, b2):
        _, vjp = jax.vjp(lambda a, b, c, d: (a + c) * jax.nn.sigmoid(b + d), z1, z2, b1, b2)
        d1, d2, db1, db2 = vjp(dv)
        return jnp.concatenate([d1, d2], axis=1), db1, db2

    dz, db1, db2 = _rowwise(glu_bwd, [(z, D_MODEL, 0), (z, D_MODEL, 1), (dy, D_MODEL, 0)],
                            [bg[:, :D_MODEL], bg[:, D_MODEL:]], [(2 * D_MODEL, BF16)], [(1, D_MODEL), (1, D_MODEL)],
                            ts=512, name="l1_glu_out_bwd")
    grads['l1_b_glu'] = jnp.concatenate([db1, db2], axis=1).reshape(-1)
    dyv = _mm(dz, p['l1_w_glu'], tb=True, name="l1_glu_dx")
    grads['l1_w_glu'] = _mm(y, dz, ta=True, name="l1_glu_dw")

    def gelu_bwd(ycv, uv, dv, dk):
        _, vjp = jax.vjp(_gelu_skip, ycv, uv, dk)
        return vjp(dv)

    dyc, du_skip, dd = _rowwise(gelu_bwd, [(yc, SSM_WIDTH, 0), (u, SSM_WIDTH, 0), (dyv, SSM_WIDTH, 0)], [dsk],
                                [(SSM_WIDTH, F32), (SSM_WIDTH, F32)], [(1, SSM_WIDTH)], ts=512, name="l1_gelu_bwd")
    grads['l1_d'] = dd.reshape(-1)
    dxs = _mm(dyc, cmat, tb=True, name="l1_cx_dx")
    dcmat = _mm(xs, dyc, ta=True, name="l1_cx_dw")
    gr, gi, dlr, dli = _scan(dxs, lam, reverse=True, xs=xs, name="l1_scan_bwd")
    gs = jnp.concatenate([gr, gi], axis=1)
    du = _mm(gs, bmat, tb=True, res=du_skip, name="l1_bu_dx")
    dbmat = _mm(u, gs, ta=True, name="l1_bu_dw")
    tr = lambda a: jnp.transpose(a, (0, 2, 1))
    grads['l1_c_re'] = tr(_unbd(dcmat[:SSM_CH], SSM_STATE, SSM_GROUP))
    grads['l1_c_im'] = -tr(_unbd(dcmat[SSM_CH:], SSM_STATE, SSM_GROUP))
    dbb_re = tr(_unbd(dbmat[:, :SSM_CH], SSM_GROUP, SSM_STATE))
    dbb_im = tr(_unbd(dbmat[:, SSM_CH:], SSM_GROUP, SSM_STATE))
    names = ['l1_log_dt', 'l1_a_re', 'l1_a_im', 'l1_b_re', 'l1_b_im']
    _, vjp = jax.vjp(_s5_disc, *[p[n] for n in names])
    for n, gval in zip(names, vjp((dlr.reshape(SSM_GROUPS, SSM_STATE), dli.reshape(SSM_GROUPS, SSM_STATE), dbb_re, dbb_im))):
        grads[n] = gval
    dxn = _mm(du, p['l1_w_in'], tb=True, name="l1_in_dx")
    grads['l1_w_in'] = _mm(xn, du, ta=True, name="l1_in_dw")
    dx, dgn = _rms_bwd(x, p['l1_mix_norm'], dxn, dy, "l1_mix_norm_bwd")
    grads['l1_mix_norm'] = dgn.reshape(-1)
    return dx


def _loss_head(x, g, target):
    d = x.shape[1]

    def fn(xv, tv, gv):
        y, vjp = jax.vjp(_rms, xv, gv)
        err = y - tv
        part = 0.5 * jnp.sum(jnp.mean(jnp.square(err), axis=-1, keepdims=True), axis=0, keepdims=True)
        dx, dg = vjp(err * (1.0 / d))
        return dx, jnp.broadcast_to(part, (1, 128)), dg

    return _rowwise(fn, [(x, d, 0), (target, d, 0)], [g.reshape(1, -1)], [(d, F32)], [(1, 128), (1, d)], ts=512,
                    name="loss_head")


def _local_step(x, target, p):
    q = _prep_weights(p)
    tabs = _rope_tables(x.shape[0])
    x1, s_mla = _mla_fwd(x, p, q, tabs)
    x2, s_f0 = _ffn_fwd(x1, p, 'l0_', "l0")
    x3, s_s5 = _s5_fwd(x2, p)
    x4, s_f1 = _ffn_fwd(x3, p, 'l1_', "l1")
    dx4, loss, dgf = _loss_head(x4, p['final_norm'], target)
    grads, gq = {'final_norm': dgf.reshape(-1)}, {}
    dx3 = _ffn_bwd(dx4, s_f1, p, 'l1_', "l1", grads)
    dx2 = _s5_bwd(dx3, s_s5, p, grads)
    dx1 = _ffn_bwd(dx2, s_f0, p, 'l0_', "l0", grads)
    dx0 = _mla_bwd(dx1, s_mla, p, q, tabs, grads, gq)
    grads.update(_unprep_grads(gq))
    return loss[0, 0], dx0, grads


def _shard_shape(name, full_shape):
    sh = list(full_shape)
    sh[SHARDED[name]] //= N_DEV
    return tuple(sh)


def _assemble(name, stacked):
    ax = SHARDED[name]
    if ax == 0:
        return stacked.reshape((-1,) + stacked.shape[2:])
    return jnp.transpose(stacked, (1, 0, 2)).reshape(stacked.shape[1], -1)


def _split(name, full):
    if SHARDED[name] == 0:
        return full.reshape(N_DEV, -1)
    r, c = full.shape
    return jnp.transpose(full.reshape(r, N_DEV, c // N_DEV), (1, 0, 2)).reshape(N_DEV, -1)


def kernel(x, l0_mix_norm, l0_w_in, l0_conv_w, l0_conv_b, l0_conv_ln_g, l0_conv_ln_b, l0_q_norm, l0_kv_norm, l0_w_uq, l0_w_ukv, l0_w_out, l0_ffn_norm, l0_w_up, l0_ffn_conv_w, l0_ffn_conv_b, l0_w_down, l1_mix_norm, l1_w_in, l1_log_dt, l1_a_re, l1_a_im, l1_b_re, l1_b_im, l1_c_re, l1_c_im, l1_d, l1_w_glu, l1_b_glu, l1_ffn_norm, l1_w_up, l1_ffn_conv_w, l1_ffn_conv_b, l1_w_down, final_norm, loss_target, m_l0_mix_norm, m_l0_w_in, m_l0_conv_w, m_l0_conv_b, m_l0_conv_ln_g, m_l0_conv_ln_b, m_l0_q_norm, m_l0_kv_norm, m_l0_w_uq, m_l0_w_ukv, m_l0_w_out, m_l0_ffn_norm, m_l0_w_up, m_l0_ffn_conv_w, m_l0_ffn_conv_b, m_l0_w_down, m_l1_mix_norm, m_l1_w_in, m_l1_log_dt, m_l1_a_re, m_l1_a_im, m_l1_b_re, m_l1_b_im, m_l1_c_re, m_l1_c_im, m_l1_d, m_l1_w_glu, m_l1_b_glu, m_l1_ffn_norm, m_l1_w_up, m_l1_ffn_conv_w, m_l1_ffn_conv_b, m_l1_w_down, m_final_norm, v_l0_mix_norm, v_l0_w_in, v_l0_conv_w, v_l0_conv_b, v_l0_conv_ln_g, v_l0_conv_ln_b, v_l0_q_norm, v_l0_kv_norm, v_l0_w_uq, v_l0_w_ukv, v_l0_w_out, v_l0_ffn_norm, v_l0_w_up, v_l0_ffn_conv_w, v_l0_ffn_conv_b, v_l0_w_down, v_l1_mix_norm, v_l1_w_in, v_l1_log_dt, v_l1_a_re, v_l1_a_im, v_l1_b_re, v_l1_b_im, v_l1_c_re, v_l1_c_im, v_l1_d, v_l1_w_glu, v_l1_b_glu, v_l1_ffn_norm, v_l1_w_up, v_l1_ffn_conv_w, v_l1_ffn_conv_b, v_l1_w_down, v_final_norm):
    args = dict(locals())
    w = {n: args[n] for n in WEIGHTS}
    m = {n: args['m_' + n] for n in WEIGHTS}
    v = {n: args['v_' + n] for n in WEIGHTS}
    big = [n for n in SHARDED if n not in SMALL_SHARDED]

    got_big = _exchange(_flat([w[n].astype(BF16) for n in big]), gather=True, name="gather_matrices")
    got_small = _exchange(_flat([w[n] for n in SMALL_SHARDED]), gather=True, name="gather_taps")
    p = {n: w[n] for n in REPLICATED}
    for names, got in ((big, got_big), (SMALL_SHARDED, got_small)):
        for n, st in zip(names, _unflat(got, [w[n].shape for n in names])):
            p[n] = _assemble(n, st)

    loss, dx, grads = _local_step(x[0], loss_target[0], p)

    sh_names = list(SHARDED)
    send = _flat([_split(n, grads[n]) for n in sh_names], lead=(N_DEV,))
    parts = _exchange(send, gather=False, name="scatter_grads")
    flat3 = lambda d: _flat([d[n] for n in sh_names])
    sh_out = _adamw(parts, flat3(w), flat3(m), flat3(v), "adamw_sharded")
    rparts = _exchange(_flat([grads[n] for n in REPLICATED]), gather=True, name="gather_small_grads")
    flatr = lambda d: _flat([d[n] for n in REPLICATED])
    rp_out = _adamw(rparts, flatr(w), flatr(m), flatr(v), "adamw_replicated")

    res = [dict(), dict(), dict(), dict()]
    for kind in range(4):
        for n, a in zip(sh_names, _unflat(sh_out[kind], [w[n].shape for n in sh_names])):
            res[kind][n] = a
        for n, a in zip(REPLICATED, _unflat(rp_out[kind], [w[n].shape for n in REPLICATED])):
            res[kind][n] = a
    total = lax.psum(loss, ("x", "y", "c"))
    return (total, dx[None], *[res[0][n] for n in WEIGHTS], *[res[1][n] for n in WEIGHTS],
            *[res[2][n] for n in WEIGHTS], *[res[3][n] for n in WEIGHTS])
```

```python
import functools

import jax
import jax.numpy as jnp
from jax import lax
from jax.experimental import pallas as pl
from jax.experimental.pallas import tpu as pltpu

F32 = jnp.float32
BF16 = jnp.bfloat16

N_DEV = 8
D_MODEL = 1024
EPS = 1e-6
LN_EPS = 1e-5
CONV_WIDTH = 512
CONV_K = 31
N_HEADS = 8
QK_NOPE = 64
QK_ROPE = 32
V_DIM = 64
HEAD_PAD = 128
Q_RANK = 256
KV_RANK = 128
ROPE_BASE = 10000.0
IN_EVEN = 2 * CONV_WIDTH + Q_RANK + KV_RANK + QK_ROPE
IN_PAD = 1536
KR_LANE = 64
SSM_WIDTH = 512
SSM_GROUP = 16
SSM_GROUPS = 32
SSM_STATE = 64
SSM_CH = SSM_GROUPS * SSM_STATE
D_FF = 2816
FFN_K = 3
ADAM_LR, ADAM_B1, ADAM_B2, ADAM_EPS, ADAM_WD, ADAM_STEP = 0.001, 0.9, 0.999, 1e-08, 0.01, 10

WEIGHTS = ['l0_mix_norm', 'l0_w_in', 'l0_conv_w', 'l0_conv_b', 'l0_conv_ln_g', 'l0_conv_ln_b', 'l0_q_norm',
           'l0_kv_norm', 'l0_w_uq', 'l0_w_ukv', 'l0_w_out', 'l0_ffn_norm', 'l0_w_up', 'l0_ffn_conv_w',
           'l0_ffn_conv_b', 'l0_w_down', 'l1_mix_norm', 'l1_w_in', 'l1_log_dt', 'l1_a_re', 'l1_a_im', 'l1_b_re',
           'l1_b_im', 'l1_c_re', 'l1_c_im', 'l1_d', 'l1_w_glu', 'l1_b_glu', 'l1_ffn_norm', 'l1_w_up',
           'l1_ffn_conv_w', 'l1_ffn_conv_b', 'l1_w_down', 'final_norm']
SHARDED = {'l0_w_in': 1, 'l0_conv_w': 1, 'l0_w_uq': 1, 'l0_w_ukv': 1, 'l0_w_out': 0, 'l0_w_up': 1,
           'l0_ffn_conv_w': 1, 'l0_w_down': 0, 'l1_w_in': 0, 'l1_w_glu': 1, 'l1_w_up': 1, 'l1_ffn_conv_w': 1,
           'l1_w_down': 0}
SMALL_SHARDED = ('l0_conv_w', 'l0_ffn_conv_w', 'l1_ffn_conv_w')
REPLICATED = [n for n in WEIGHTS if n not in SHARDED]


def _pick(n, cands):
    for c in cands:
        if n % c == 0:
            return c
    raise ValueError(f"no tile for {n}")


def _mm(a, b, *, ta=False, tb=False, res=None, out_dtype=F32, name):
    m, kd = (a.shape[1], a.shape[0]) if ta else a.shape
    kd2, n = (b.shape[1], b.shape[0]) if tb else b.shape
    assert kd == kd2, (a.shape, b.shape, ta, tb)
    tm = _pick(m, (1024, 512, 256, 128))
    tn = _pick(n, (512, 384, 256, 128))
    tk = _pick(kd, (1024, 512, 256, 128))
    nk = kd // tk
    dn = (((0 if ta else 1,), (1 if tb else 0,)), ((), ()))

    def body(*refs):
        if res is None:
            a_ref, b_ref, o_ref, acc_ref = refs
            r_ref = None
        else:
            a_ref, b_ref, r_ref, o_ref, acc_ref = refs
        k = pl.program_id(2)
        p = lax.dot_general(a_ref[...].astype(BF16), b_ref[...].astype(BF16), dn, preferred_element_type=F32)

        @pl.when(k == 0)
        def _():
            acc_ref[...] = p

        @pl.when(k > 0)
        def _():
            acc_ref[...] += p

        @pl.when(k == nk - 1)
        def _():
            out = acc_ref[...]
            if r_ref is not None:
                out = out + r_ref[...]
            o_ref[...] = out.astype(out_dtype)

    a_spec = pl.BlockSpec((tk, tm), lambda i, j, k: (k, i)) if ta else pl.BlockSpec((tm, tk), lambda i, j, k: (i, k))
    b_spec = pl.BlockSpec((tn, tk), lambda i, j, k: (j, k)) if tb else pl.BlockSpec((tk, tn), lambda i, j, k: (k, j))
    o_spec = pl.BlockSpec((tm, tn), lambda i, j, k: (i, j))
    in_specs, args = [a_spec, b_spec], [a, b]
    if res is not None:
        in_specs.append(o_spec)
        args.append(res)
    return pl.pallas_call(
        body, grid=(m // tm, n // tn, nk), in_specs=in_specs, out_specs=o_spec,
        out_shape=jax.ShapeDtypeStruct((m, n), out_dtype), scratch_shapes=[pltpu.VMEM((tm, tn), F32)],
        compiler_params=pltpu.CompilerParams(dimension_semantics=("parallel", "parallel", "arbitrary")),
        name=name)(*args)


def _rowwise(fn, rows, bcasts, row_outs, red_outs, *, ts, name):
    s = rows[0][0].shape[0]
    nr, nb, nro, nre = len(rows), len(bcasts), len(row_outs), len(red_outs)

    def body(*refs):
        i = pl.program_id(0)
        outs = fn(*[r[...] for r in refs[:nr + nb]])
        if not isinstance(outs, (tuple, list)):
            outs = (outs,)
        o_refs = refs[nr + nb:]
        for q in range(nro):
            o_refs[q][...] = outs[q].astype(o_refs[q].dtype)
        for q in range(nro, nro + nre):
            @pl.when(i == 0)
            def _(q=q):
                o_refs[q][...] = outs[q]

            @pl.when(i > 0)
            def _(q=q):
                o_refs[q][...] += outs[q]

    in_specs = [pl.BlockSpec((ts, w), functools.partial(lambda i, cb: (i, cb), cb=cb)) for (_, w, cb) in rows]
    in_specs += [pl.BlockSpec(b.shape, functools.partial(lambda i, nd: (0,) * nd, nd=b.ndim)) for b in bcasts]
    out_specs = [pl.BlockSpec((ts, w), lambda i: (i, 0)) for (w, _) in row_outs]
    out_specs += [pl.BlockSpec((r, w), lambda i: (0, 0)) for (r, w) in red_outs]
    out_shape = [jax.ShapeDtypeStruct((s, w), dt) for (w, dt) in row_outs]
    out_shape += [jax.ShapeDtypeStruct((r, w), F32) for (r, w) in red_outs]
    return pl.pallas_call(
        body, grid=(s // ts,), in_specs=in_specs, out_specs=out_specs, out_shape=out_shape,
        compiler_params=pltpu.CompilerParams(dimension_semantics=("arbitrary",)), name=name,
    )(*[r[0] for r in rows], *bcasts)


def _rms(x, g):
    return x * lax.rsqrt(jnp.mean(x * x, axis=-1, keepdims=True) + EPS) * g


def _rms_fwd(x, g, name):
    return _rowwise(lambda xv, gv: _rms(xv, gv), [(x, x.shape[1], 0)], [g.reshape(1, -1)],
                    [(x.shape[1], BF16)], [], ts=512, name=name)[0]


def _rms_bwd(x, g, dxn, dres, name):
    d = x.shape[1]

    def fn(xv, dv, rv, gv):
        _, vjp = jax.vjp(_rms, xv, gv)
        dx, dg = vjp(dv.astype(F32))
        return rv + dx, dg

    return _rowwise(fn, [(x, d, 0), (dxn, d, 0), (dres, d, 0)], [g.reshape(1, -1)], [(d, F32)], [(1, d)],
                    ts=512, name=name)


def _conv_fwd(xins, w, woffs, params, poffs, pre, post, outs, *, k_taps, hb, ts, tc, to, n_cb, name):
    s = xins[0][0].shape[0]
    n_s, nx, ncv, npar, no = s // ts, len(xins), len(woffs), len(params), len(outs)
    rpb = ts // hb

    def body(*refs):
        mains, halos = refs[:nx], refs[nx:2 * nx]
        w_refs = refs[2 * nx:2 * nx + ncv]
        p_refs = refs[2 * nx + ncv:2 * nx + ncv + npar]
        o_refs = refs[2 * nx + ncv + npar:2 * nx + ncv + npar + no]
        u_s = refs[2 * nx + ncv + npar + no:]
        i = pl.program_id(1)
        um = pre(*[r[...].astype(F32) for r in mains])
        uh = pre(*[r[...].astype(F32) for r in halos])
        first = (i > 0).astype(F32)
        cs = []
        for q in range(ncv):
            u_s[q][pl.ds(0, hb), :] = uh[q] * first
            u_s[q][pl.ds(hb, ts), :] = um[q]
            acc = jnp.zeros((ts, tc), F32)
            for t in range(k_taps):
                acc = acc + w_refs[q][pl.ds(t, 1), :] * u_s[q][pl.ds(hb - (k_taps - 1) + t, ts), :]
            cs.append(acc)
        res = post(cs, [r[...] for r in p_refs])
        for q in range(no):
            o_refs[q][...] = res[q].astype(o_refs[q].dtype)

    in_specs = [pl.BlockSpec((ts, tc), functools.partial(lambda jc, i, off: (i, off + jc), off=off)) for _, off in xins]
    in_specs += [pl.BlockSpec((hb, tc), functools.partial(lambda jc, i, off: (jnp.maximum(i * rpb - 1, 0), off + jc), off=off))
                 for _, off in xins]
    in_specs += [pl.BlockSpec((w.shape[0], tc), functools.partial(lambda jc, i, off: (0, off + jc), off=off)) for off in woffs]
    in_specs += [pl.BlockSpec((1, tc), functools.partial(lambda jc, i, off: (0, off + jc), off=off)) for off in poffs]
    out_specs = [pl.BlockSpec((ts, to), lambda jc, i: (i, jc)) for _ in outs]
    out_shape = [jax.ShapeDtypeStruct((s, wd), dt) for wd, dt in outs]
    return pl.pallas_call(
        body, grid=(n_cb, n_s), in_specs=in_specs, out_specs=out_specs, out_shape=out_shape,
        scratch_shapes=[pltpu.VMEM((hb + ts, tc), F32) for _ in range(ncv)],
        compiler_params=pltpu.CompilerParams(dimension_semantics=("parallel", "arbitrary")), name=name,
    )(*[a for a, _ in xins], *[a for a, _ in xins], *([w] * ncv), *params)


def _conv_bwd(xins, w, woffs, params, poffs, pre, post, douts, dx_dtype, *, k_taps, hb, ts, tc, to, n_cb, name):
    s = xins[0][0].shape[0]
    n_s, nx, ncv, npar, ndo = s // ts, len(xins), len(woffs), len(params), len(douts)
    rpb = ts // hb
    n_hb = s // hb
    kp = w.shape[0]

    def body(*refs):
        pos = 0

        def take(n):
            nonlocal pos
            out = refs[pos:pos + n]
            pos += n
            return out

        mains, prevs, nexts = take(nx), take(nx), take(nx)
        d_mains, d_nexts = take(ndo), take(ndo)
        w_refs, p_refs = take(ncv), take(npar)
        dx_refs, dw_refs, dp_refs = take(nx), take(ncv), take(npar)
        u_s, dc_s = take(ncv), take(ncv)
        i = pl.program_id(1)
        xm = [r[...].astype(F32) for r in mains]
        um = pre(*xm)
        up = pre(*[r[...].astype(F32) for r in prevs])
        un = pre(*[r[...].astype(F32) for r in nexts])
        first = (i > 0).astype(F32)
        last = (i < n_s - 1).astype(F32)
        pv = [r[...] for r in p_refs]
        c_main, c_next = [], []
        for q in range(ncv):
            u_s[q][pl.ds(0, hb), :] = up[q] * first
            u_s[q][pl.ds(hb, ts), :] = um[q]
            u_s[q][pl.ds(hb + ts, hb), :] = un[q]
            acc = jnp.zeros((ts + hb, tc), F32)
            for t in range(k_taps):
                acc = acc + w_refs[q][pl.ds(t, 1), :] * u_s[q][pl.ds(hb - (k_taps - 1) + t, ts + hb), :]
            c_main.append(acc[:ts])
            c_next.append(acc[ts:])
        _, vjp_m = jax.vjp(lambda c, p: tuple(post(c, p)), c_main, pv)
        dc_m, dpar = vjp_m(tuple(r[...].astype(F32) for r in d_mains))
        _, vjp_n = jax.vjp(lambda c: tuple(post(c, pv)), c_next)
        (dc_n,) = vjp_n(tuple(r[...].astype(F32) * last for r in d_nexts))
        dus = []
        for q in range(ncv):
            dc_s[q][pl.ds(0, ts), :] = dc_m[q]
            dc_s[q][pl.ds(ts, hb), :] = dc_n[q]

            @pl.when(i == 0)
            def _(q=q):
                dw_refs[q][...] = jnp.zeros((kp, tc), F32)

            acc = jnp.zeros((ts, tc), F32)
            for t in range(k_taps):
                acc = acc + w_refs[q][pl.ds(t, 1), :] * dc_s[q][pl.ds(k_taps - 1 - t, ts), :]
                dw_refs[q][pl.ds(t, 1), :] += jnp.sum(
                    dc_m[q] * u_s[q][pl.ds(hb - (k_taps - 1) + t, ts), :], axis=0, keepdims=True)
            dus.append(acc)
        _, vjp_p = jax.vjp(lambda *xv: tuple(pre(*xv)), *xm)
        dxs = vjp_p(tuple(dus))
        for q in range(nx):
            dx_refs[q][...] = dxs[q].astype(dx_refs[q].dtype)
        for q in range(npar):
            @pl.when(i == 0)
            def _(q=q):
                dp_refs[q][...] = dpar[q]

            @pl.when(i > 0)
            def _(q=q):
                dp_refs[q][...] += dpar[q]

    def main_spec(off, wd=tc):
        return pl.BlockSpec((ts, wd), functools.partial(lambda jc, i, off: (i, off + jc), off=off))

    def prev_spec(off):
        return pl.BlockSpec((hb, tc), functools.partial(lambda jc, i, off: (jnp.maximum(i * rpb - 1, 0), off + jc), off=off))

    def next_spec(off, wd=tc):
        return pl.BlockSpec((hb, wd), functools.partial(lambda jc, i, off: (jnp.minimum((i + 1) * rpb, n_hb - 1), off + jc), off=off))

    in_specs = [main_spec(off) for _, off in xins] + [prev_spec(off) for _, off in xins] + [next_spec(off) for _, off in xins]
    in_specs += [main_spec(off, to) for _, off in douts] + [next_spec(off, to) for _, off in douts]
    in_specs += [pl.BlockSpec((kp, tc), functools.partial(lambda jc, i, off: (0, off + jc), off=off)) for off in woffs]
    in_specs += [pl.BlockSpec((1, tc), functools.partial(lambda jc, i, off: (0, off + jc), off=off)) for off in poffs]
    out_specs = [pl.BlockSpec((ts, tc), lambda jc, i: (i, jc)) for _ in xins]
    out_specs += [pl.BlockSpec((kp, tc), lambda jc, i: (0, jc)) for _ in woffs]
    out_specs += [pl.BlockSpec((1, tc), lambda jc, i: (0, jc)) for _ in params]
    width = n_cb * tc
    out_shape = [jax.ShapeDtypeStruct((s, width), dx_dtype) for _ in xins]
    out_shape += [jax.ShapeDtypeStruct((kp, width), F32) for _ in woffs]
    out_shape += [jax.ShapeDtypeStruct((1, width), F32) for _ in params]
    xa = [a for a, _ in xins]
    da = [a for a, _ in douts]
    return pl.pallas_call(
        body, grid=(n_cb, n_s), in_specs=in_specs, out_specs=out_specs, out_shape=out_shape,
        scratch_shapes=[pltpu.VMEM((hb + ts + hb, tc), F32) for _ in range(ncv)]
        + [pltpu.VMEM((ts + hb, tc), F32) for _ in range(ncv)],
        compiler_params=pltpu.CompilerParams(dimension_semantics=("parallel", "arbitrary")), name=name,
    )(*xa, *xa, *xa, *da, *da, *([w] * ncv), *params)


def _glu_pre(a, g):
    return [a * jax.nn.sigmoid(g)]


def _ln_silu_post(cs, ps):
    c = cs[0] + ps[0]
    mu = jnp.mean(c, axis=-1, keepdims=True)
    var = jnp.mean(jnp.square(c - mu), axis=-1, keepdims=True)
    y = (c - mu) * lax.rsqrt(var + LN_EPS) * ps[1] + ps[2]
    return [jax.nn.silu(y)]


def _id_pre(a):
    return [a]


GATE_TO = 256


def _gate_post(cs, ps):
    h = cs[0] + ps[0]
    return [jax.nn.silu(h[:, :GATE_TO]) * h[:, GATE_TO:]]


def _interleave(w):
    r = w.shape[0]
    return jnp.transpose(w.reshape(r, 2, D_FF // GATE_TO, GATE_TO), (0, 2, 1, 3)).reshape(r, 2 * D_FF)


def _deinterleave(w):
    r = w.shape[0]
    return jnp.transpose(w.reshape(r, D_FF // GATE_TO, 2, GATE_TO), (0, 2, 1, 3)).reshape(r, 2 * D_FF)


CONVA = dict(k_taps=CONV_K, hb=32, ts=512, tc=CONV_WIDTH, to=CONV_WIDTH, n_cb=1)
GATE = dict(k_taps=FFN_K, hb=8, ts=512, tc=2 * GATE_TO, to=GATE_TO, n_cb=D_FF // GATE_TO)


def _rope_tables(s):
    half = QK_ROPE // 2
    inv = ROPE_BASE ** (-jnp.arange(half, dtype=F32) / half)
    ang = jnp.arange(s, dtype=F32)[:, None] * inv[None, :]
    cos, sin = jnp.cos(ang), jnp.sin(ang)
    z = lambda n: jnp.zeros((s, n), F32)
    c = jnp.concatenate([jnp.ones((s, QK_NOPE), F32), cos, cos, z(HEAD_PAD - QK_NOPE - QK_ROPE)], axis=1)
    s1 = jnp.concatenate([z(QK_NOPE), -sin, z(HEAD_PAD - QK_NOPE - half)], axis=1)
    s2 = jnp.concatenate([z(QK_NOPE + half), sin, z(HEAD_PAD - QK_NOPE - QK_ROPE)], axis=1)
    return c, s1, s2


def _rot(t, c, s1, s2):
    half = QK_ROPE // 2
    return t * c + pltpu.roll(t, HEAD_PAD - half, 1) * s1 + pltpu.roll(t, half, 1) * s2


def _rot_t(d, c, s1, s2):
    half = QK_ROPE // 2
    return d * c + pltpu.roll(d * s1, half, 1) + pltpu.roll(d * s2, HEAD_PAD - half, 1)


def _heads(v):
    return [v[:, h * HEAD_PAD:(h + 1) * HEAD_PAD] for h in range(N_HEADS)]


def _rope_fwd(qf, kvf, hmat, tabs, name):
    w = N_HEADS * HEAD_PAD

    def fn(q, k, v, kr, c, s1, s2):
        krr = _rot(kr, c, s1, s2)
        qo = jnp.concatenate([_rot(t, c, s1, s2) for t in _heads(q)], axis=1)
        ko = jnp.concatenate([t + krr for t in _heads(k)], axis=1)
        return qo, ko, v

    rows = [(qf, w, 0), (kvf, w, 0), (kvf, w, 1), (hmat, HEAD_PAD, IN_PAD // HEAD_PAD - 1)]
    rows += [(t, HEAD_PAD, 0) for t in tabs]
    return _rowwise(fn, rows, [], [(w, BF16)] * 3, [], ts=512, name=name)


def _rope_bwd(dq, dk, dv, tabs, name):
    w = N_HEADS * HEAD_PAD

    def fn(dqv, dkv, dvv, c, s1, s2):
        dqo = jnp.concatenate([_rot_t(t, c, s1, s2) for t in _heads(dqv)], axis=1)
        ksum = functools.reduce(lambda a, b: a + b, _heads(dkv))
        return dqo, jnp.concatenate([dkv, dvv], axis=1), _rot_t(ksum, c, s1, s2)

    rows = [(dq, w, 0), (dk, w, 0), (dv, w, 0)] + [(t, HEAD_PAD, 0) for t in tabs]
    return _rowwise(fn, rows, [], [(w, BF16), (2 * w, BF16), (HEAD_PAD, F32)], [], ts=512, name=name)


ATT_T = 512
ATT_SCALE = (QK_NOPE + QK_ROPE) ** -0.5
LOG2E = 1.4426950408889634
ATT_C2 = ATT_SCALE * LOG2E


def _causal_mask(sc, qi, kj, t, transposed):
    r = lax.broadcasted_iota(jnp.int32, (t, t), 0)
    c = lax.broadcasted_iota(jnp.int32, (t, t), 1)
    ok = (kj * t + r <= qi * t + c) if transposed else (kj * t + c <= qi * t + r)
    return jnp.where(ok, sc, -jnp.inf)


def _nt(a, b):
    return lax.dot_general(a, b, (((1,), (1,)), ((), ())), preferred_element_type=F32)


def _attn_fwd(q, k, v, name):
    s = q.shape[0]
    t = min(ATT_T, s)
    n = s // t

    def body(q_ref, k_ref, v_ref, o_ref, lse_ref, m_s, l_s, acc_s):
        i, j = pl.program_id(1), pl.program_id(2)

        @pl.when(j == 0)
        def _():
            m_s[...] = jnp.full((t, 1), -jnp.inf, F32)
            l_s[...] = jnp.zeros((t, 1), F32)
            acc_s[...] = jnp.zeros((t, HEAD_PAD), F32)

        def step(diag):
            sc = _nt(q_ref[...], k_ref[...])
            if diag:
                sc = _causal_mask(sc, i, j, t, False)
            m_new = jnp.maximum(m_s[...], jnp.max(sc, axis=-1, keepdims=True))
            alpha = jnp.exp2((m_s[...] - m_new) * ATT_C2)
            p = jnp.exp2((sc - m_new) * ATT_C2)
            l_s[...] = alpha * l_s[...] + jnp.sum(p, axis=-1, keepdims=True)
            acc_s[...] = alpha * acc_s[...] + jnp.dot(p.astype(BF16), v_ref[...], preferred_element_type=F32)
            m_s[...] = m_new

        pl.when(j < i)(functools.partial(step, False))
        pl.when(j == i)(functools.partial(step, True))

        @pl.when(j == n - 1)
        def _():
            o_ref[...] = (acc_s[...] / l_s[...]).astype(BF16)
            lse_ref[...] = jnp.broadcast_to(m_s[...] * ATT_SCALE + jnp.log(l_s[...]), (t, HEAD_PAD))

    q_spec = pl.BlockSpec((t, HEAD_PAD), lambda h, i, j: (i, h))
    kv_spec = pl.BlockSpec((t, HEAD_PAD), lambda h, i, j: (jnp.minimum(i, j), h))
    return pl.pallas_call(
        body, grid=(N_HEADS, n, n), in_specs=[q_spec, kv_spec, kv_spec], out_specs=[q_spec, q_spec],
        out_shape=[jax.ShapeDtypeStruct(q.shape, BF16), jax.ShapeDtypeStruct(q.shape, F32)],
        scratch_shapes=[pltpu.VMEM((t, 1), F32), pltpu.VMEM((t, 1), F32), pltpu.VMEM((t, HEAD_PAD), F32)],
        compiler_params=pltpu.CompilerParams(dimension_semantics=("parallel", "parallel", "arbitrary")), name=name,
    )(q, k, v)


def _attn_bwd_q(q, k, v, cat, dcat, lse, name):
    s = q.shape[0]
    t = min(ATT_T, s)
    n = s // t
    ob = CONV_WIDTH // HEAD_PAD

    def body(q_ref, k_ref, v_ref, o_ref, do_ref, lse_ref, dq_ref, lser_ref, dltr_ref, dq_s, dl_s):
        i, j = pl.program_id(1), pl.program_id(2)

        @pl.when(j == 0)
        def _():
            dl_s[...] = jnp.sum(do_ref[...] * o_ref[...].astype(F32), axis=-1, keepdims=True)
            dq_s[...] = jnp.zeros((t, HEAD_PAD), F32)

        def step(diag):
            sc = _nt(q_ref[...], k_ref[...])
            if diag:
                sc = _causal_mask(sc, i, j, t, False)
            p = jnp.exp2(sc * ATT_C2 - lse_ref[:, 0:1] * LOG2E)
            dp = _nt(do_ref[...].astype(BF16), v_ref[...])
            ds = p * (dp - dl_s[...])
            dq_s[...] += jnp.dot(ds.astype(BF16), k_ref[...], preferred_element_type=F32)

        pl.when(j < i)(functools.partial(step, False))
        pl.when(j == i)(functools.partial(step, True))

        @pl.when(j == n - 1)
        def _():
            dq_ref[...] = dq_s[...] * ATT_SCALE
            lser_ref[...] = jnp.transpose(lse_ref[...] * LOG2E)[0:8, :]
            dltr_ref[...] = jnp.transpose(jnp.broadcast_to(dl_s[...], (t, HEAD_PAD)))[0:8, :]

    q_spec = pl.BlockSpec((t, HEAD_PAD), lambda h, i, j: (i, h))
    kv_spec = pl.BlockSpec((t, HEAD_PAD), lambda h, i, j: (jnp.minimum(i, j), h))
    o_spec = pl.BlockSpec((t, HEAD_PAD), lambda h, i, j: (i, ob + h))
    row_spec = pl.BlockSpec((None, 8, t), lambda h, i, j: (h, 0, i))
    return pl.pallas_call(
        body, grid=(N_HEADS, n, n), in_specs=[q_spec, kv_spec, kv_spec, o_spec, o_spec, q_spec],
        out_specs=[q_spec, row_spec, row_spec],
        out_shape=[jax.ShapeDtypeStruct(q.shape, F32), jax.ShapeDtypeStruct((N_HEADS, 8, s), F32),
                   jax.ShapeDtypeStruct((N_HEADS, 8, s), F32)],
        scratch_shapes=[pltpu.VMEM((t, HEAD_PAD), F32), pltpu.VMEM((t, 1), F32)],
        compiler_params=pltpu.CompilerParams(dimension_semantics=("parallel", "parallel", "arbitrary")), name=name,
    )(q, k, v, cat, dcat, lse)


def _attn_bwd_kv(q, k, v, dcat, lse_r, dlt_r, name):
    s = q.shape[0]
    t = min(ATT_T, s)
    n = s // t
    ob = CONV_WIDTH // HEAD_PAD

    def body(k_ref, v_ref, q_ref, do_ref, lse_ref, dl_ref, dk_ref, dv_ref, dk_s, dv_s):
        j, i = pl.program_id(1), pl.program_id(2)

        @pl.when(i == 0)
        def _():
            dk_s[...] = jnp.zeros((t, HEAD_PAD), F32)
            dv_s[...] = jnp.zeros((t, HEAD_PAD), F32)

        def step(diag):
            do = do_ref[...].astype(BF16)
            sc = _nt(k_ref[...], q_ref[...])
            if diag:
                sc = _causal_mask(sc, i, j, t, True)
            p = jnp.exp2(sc * ATT_C2 - lse_ref[0:1, :])
            dp = _nt(v_ref[...], do)
            ds = p * (dp - dl_ref[0:1, :])
            dv_s[...] += jnp.dot(p.astype(BF16), do, preferred_element_type=F32)
            dk_s[...] += jnp.dot(ds.astype(BF16), q_ref[...], preferred_element_type=F32)

        pl.when(i > j)(functools.partial(step, False))
        pl.when(i == j)(functools.partial(step, True))

        @pl.when(i == n - 1)
        def _():
            dk_ref[...] = dk_s[...] * ATT_SCALE
            dv_ref[...] = dv_s[...]

    kv_spec = pl.BlockSpec((t, HEAD_PAD), lambda h, j, i: (j, h))
    q_spec = pl.BlockSpec((t, HEAD_PAD), lambda h, j, i: (jnp.maximum(i, j), h))
    do_spec = pl.BlockSpec((t, HEAD_PAD), lambda h, j, i: (jnp.maximum(i, j), ob + h))
    row_spec = pl.BlockSpec((None, 8, t), lambda h, j, i: (h, 0, jnp.maximum(i, j)))
    return pl.pallas_call(
        body, grid=(N_HEADS, n, n), in_specs=[kv_spec, kv_spec, q_spec, do_spec, row_spec, row_spec],
        out_specs=[kv_spec, kv_spec],
        out_shape=[jax.ShapeDtypeStruct(q.shape, F32), jax.ShapeDtypeStruct(q.shape, F32)],
        scratch_shapes=[pltpu.VMEM((t, HEAD_PAD), F32), pltpu.VMEM((t, HEAD_PAD), F32)],
        compiler_params=pltpu.CompilerParams(dimension_semantics=("parallel", "parallel", "arbitrary")), name=name,
    )(k, v, q, dcat, lse_r, dlt_r)


SCAN_T = 256
SCAN_C = 512


def _scan(b, lam, *, reverse, xs=None, name):
    s = b.shape[0]
    t = min(SCAN_T, s)
    n_t, n_c = s // t, SSM_CH // SCAN_C
    with_dlam = xs is not None

    def shift(a, d, row):
        if d >= 8:
            z = jnp.zeros((d, SCAN_C), F32)
            return jnp.concatenate([a[d:], z], axis=0) if reverse else jnp.concatenate([z, a[:t - d]], axis=0)
        if reverse:
            return jnp.where(row < t - d, pltpu.roll(a, t - d, 0), 0.0)
        return jnp.where(row >= d, pltpu.roll(a, d, 0), 0.0)

    def body(*refs):
        if with_dlam:
            b_ref, lam_ref, x_ref, o_ref, dl_ref, c_s = refs
        else:
            b_ref, lam_ref, o_ref, c_s = refs
        k = pl.program_id(0)

        @pl.when(k == 0)
        def _():
            c_s[...] = jnp.zeros((1, 2 * SSM_CH), F32)
            if with_dlam:
                dl_ref[...] = jnp.zeros((1, 2 * SSM_CH), F32)

        row = lax.broadcasted_iota(jnp.int32, (t, SCAN_C), 0)
        edge = (row == t - 1) if reverse else (row == 0)
        for ch in range(n_c):
            re = pl.ds(ch * SCAN_C, SCAN_C)
            im = pl.ds(SSM_CH + ch * SCAN_C, SCAN_C)
            lr = lam_ref[:, re]
            li = -lam_ref[:, im] if reverse else lam_ref[:, im]
            cr, ci = c_s[:, re], c_s[:, im]
            ar = b_ref[:, re] + jnp.where(edge, lr * cr - li * ci, 0.0)
            ai = b_ref[:, im] + jnp.where(edge, lr * ci + li * cr, 0.0)
            d = 1
            while d < t:
                sr, si = shift(ar, d, row), shift(ai, d, row)
                ar, ai = ar + lr * sr - li * si, ai + lr * si + li * sr
                lr, li = lr * lr - li * li, 2.0 * lr * li
                d *= 2
            o_ref[:, re] = ar
            o_ref[:, im] = ai
            if with_dlam:
                gr = jnp.where(edge, cr, shift(ar, 1, row))
                gi = jnp.where(edge, ci, shift(ai, 1, row))
                xr, xi = x_ref[:, re], x_ref[:, im]
                dl_ref[:, re] += jnp.sum(xr * gr + xi * gi, axis=0, keepdims=True)
                dl_ref[:, im] += jnp.sum(xr * gi - xi * gr, axis=0, keepdims=True)
            last = 0 if reverse else t - 1
            c_s[:, re] = ar[last:last + 1, :]
            c_s[:, im] = ai[last:last + 1, :]

    tm = (lambda k: (n_t - 1 - k, 0)) if reverse else (lambda k: (k, 0))
    blk = pl.BlockSpec((t, 2 * SSM_CH), tm)
    vec = pl.BlockSpec((1, 2 * SSM_CH), lambda k: (0, 0))
    in_specs, args = [blk, vec], [b, lam]
    out_specs, out_shape = [blk], [jax.ShapeDtypeStruct((s, 2 * SSM_CH), F32)]
    if with_dlam:
        in_specs.append(blk)
        args.append(xs)
        out_specs.append(vec)
        out_shape.append(jax.ShapeDtypeStruct((1, 2 * SSM_CH), F32))
    return pl.pallas_call(
        body, grid=(n_t,), in_specs=in_specs, out_specs=out_specs, out_shape=out_shape,
        scratch_shapes=[pltpu.VMEM((1, 2 * SSM_CH), F32)],
        compiler_params=pltpu.CompilerParams(dimension_semantics=("arbitrary",)), name=name,
    )(*args)


def _s5_disc(log_dt, a_re, a_im, b_re, b_im):
    dt = jnp.exp(log_dt)[:, None]
    mag = jnp.exp(a_re * dt)
    lb_re, lb_im = mag * jnp.cos(a_im * dt), mag * jnp.sin(a_im * dt)
    den = a_re * a_re + a_im * a_im
    nr, ni = lb_re - 1.0, lb_im
    f_re = (nr * a_re + ni * a_im) / den
    f_im = (ni * a_re - nr * a_im) / den
    bb_re = f_re[..., None] * b_re - f_im[..., None] * b_im
    bb_im = f_re[..., None] * b_im + f_im[..., None] * b_re
    return lb_re, lb_im, bb_re, bb_im


def _bd(a):
    g, i, j = a.shape
    eye = jnp.eye(g, dtype=a.dtype)
    return (a[:, :, None, :] * eye[:, None, :, None]).reshape(g * i, g * j)


def _unbd(m, i, j):
    g = m.shape[0] // i
    eye = jnp.eye(g, dtype=m.dtype)
    return jnp.sum(m.reshape(g, i, g, j) * eye[:, None, :, None], axis=2)


def _exchange(src, *, gather, name):
    shape = src.shape if not gather else (N_DEV,) + src.shape

    def body(src_ref, out_ref, send_sems, recv_sems, local_sem):
        x, y, c = lax.axis_index("x"), lax.axis_index("y"), lax.axis_index("c")
        me = 4 * x + 2 * y + c

        def piece(slot):
            return src_ref if gather else src_ref.at[slot]

        mine = pltpu.make_async_copy(piece(me), out_ref.at[me], local_sem)
        mine.start()
        copies = []
        for r in range(1, N_DEV):
            px, py, pc = x ^ (r >> 2), y ^ ((r >> 1) & 1), c ^ (r & 1)
            peer = 4 * px + 2 * py + pc
            copies.append(pltpu.make_async_remote_copy(
                src_ref=piece(peer), dst_ref=out_ref.at[me], send_sem=send_sems.at[r - 1],
                recv_sem=recv_sems.at[r - 1], device_id=(px, py, pc), device_id_type=pl.DeviceIdType.MESH))
        for cp in copies:
            cp.start()
        for cp in copies:
            cp.wait_recv()
        for cp in copies:
            cp.wait_send()
        mine.wait()

    return pl.pallas_call(
        body, out_shape=jax.ShapeDtypeStruct(shape, src.dtype),
        in_specs=[pl.BlockSpec(memory_space=pl.ANY)], out_specs=pl.BlockSpec(memory_space=pl.ANY),
        scratch_shapes=[pltpu.SemaphoreType.DMA((N_DEV - 1,)), pltpu.SemaphoreType.DMA((N_DEV - 1,)),
                        pltpu.SemaphoreType.DMA],
        compiler_params=pltpu.CompilerParams(has_side_effects=True), name=name,
    )(src)


FLAT_W = 512
FLAT_ROWS = 256


def _adamw(parts, w, m, v, name):
    r = w.shape[0]
    ts = FLAT_ROWS

    def body(p_ref, w_ref, m_ref, v_ref, g_ref, d_ref, nm_ref, nv_ref):
        g = p_ref[0].astype(F32)
        for d in range(1, N_DEV):
            g = g + p_ref[d].astype(F32)
        m2 = ADAM_B1 * m_ref[...] + (1.0 - ADAM_B1) * g
        v2 = ADAM_B2 * v_ref[...] + (1.0 - ADAM_B2) * jnp.square(g)
        m_hat = m2 / (1.0 - ADAM_B1 ** ADAM_STEP)
        v_hat = v2 / (1.0 - ADAM_B2 ** ADAM_STEP)
        g_ref[...] = g
        d_ref[...] = -ADAM_LR * (m_hat / (jnp.sqrt(v_hat) + ADAM_EPS) + ADAM_WD * w_ref[...])
        nm_ref[...] = m2
        nv_ref[...] = v2

    spec = pl.BlockSpec((ts, FLAT_W), lambda i: (i, 0))
    return pl.pallas_call(
        body, grid=(r // ts,), in_specs=[pl.BlockSpec((N_DEV, ts, FLAT_W), lambda i: (0, i, 0)), spec, spec, spec],
        out_specs=[spec] * 4, out_shape=[jax.ShapeDtypeStruct((r, FLAT_W), F32)] * 4,
        compiler_params=pltpu.CompilerParams(dimension_semantics=("parallel",)), name=name,
    )(parts, w, m, v)


def _flat(arrs, lead=()):
    nl = len(lead)
    v = jnp.concatenate([a.reshape(lead + (-1,)) for a in arrs], axis=nl)
    n = v.shape[nl]
    pad = (-n) % (FLAT_ROWS * FLAT_W)
    v = jnp.pad(v, [(0, 0)] * nl + [(0, pad)])
    return v.reshape(lead + (-1, FLAT_W))


def _unflat(flat, shapes):
    v = flat.reshape(flat.shape[:-2] + (-1,))
    out, off = [], 0
    for sh in shapes:
        n = 1
        for d in sh:
            n *= d
        out.append(v[..., off:off + n].reshape(v.shape[:-1] + tuple(sh)))
        off += n
    return out


def _pad_cols(w, n):
    return jnp.pad(w, ((0, 0), (0, n - w.shape[1])))


def _prep_weights(p):
    q = {}
    w_in = p['l0_w_in']
    z = lambda n: jnp.zeros((D_MODEL, n), w_in.dtype)
    q['w_in'] = jnp.concatenate([w_in[:, :IN_EVEN - QK_ROPE], z(KR_LANE), w_in[:, IN_EVEN - QK_ROPE:],
                                 z(HEAD_PAD - KR_LANE - QK_ROPE)], axis=1)
    dqk = QK_NOPE + QK_ROPE
    q['w_uq'] = jnp.pad(p['l0_w_uq'].reshape(Q_RANK, N_HEADS, dqk), ((0, 0), (0, 0), (0, HEAD_PAD - dqk))
                        ).reshape(Q_RANK, N_HEADS * HEAD_PAD)
    ukv = p['l0_w_ukv'].reshape(KV_RANK, N_HEADS, 2, QK_NOPE)
    padh = lambda a: jnp.pad(a, ((0, 0), (0, 0), (0, HEAD_PAD - QK_NOPE))).reshape(KV_RANK, N_HEADS * HEAD_PAD)
    q['w_ukv'] = jnp.concatenate([padh(ukv[:, :, 0]), padh(ukv[:, :, 1])], axis=1)
    wo = p['l0_w_out']
    wo_a = jnp.pad(wo[CONV_WIDTH:].reshape(N_HEADS, V_DIM, D_MODEL), ((0, 0), (0, HEAD_PAD - V_DIM), (0, 0)))
    q['w_out'] = jnp.concatenate([wo[:CONV_WIDTH], wo_a.reshape(N_HEADS * HEAD_PAD, D_MODEL)], axis=0)
    return q


def _unprep_grads(g):
    out = {}
    d = g['w_in']
    out['l0_w_in'] = jnp.concatenate([d[:, :IN_EVEN - QK_ROPE],
                                      d[:, IN_EVEN - QK_ROPE + KR_LANE:IN_EVEN + KR_LANE]], axis=1)
    dqk = QK_NOPE + QK_ROPE
    out['l0_w_uq'] = g['w_uq'].reshape(Q_RANK, N_HEADS, HEAD_PAD)[:, :, :dqk].reshape(Q_RANK, N_HEADS * dqk)
    d = g['w_ukv'].reshape(KV_RANK, 2, N_HEADS, HEAD_PAD)[:, :, :, :QK_NOPE]
    out['l0_w_ukv'] = jnp.transpose(d, (0, 2, 1, 3)).reshape(KV_RANK, N_HEADS * 2 * QK_NOPE)
    d = g['w_out']
    da = d[CONV_WIDTH:].reshape(N_HEADS, HEAD_PAD, D_MODEL)[:, :V_DIM].reshape(N_HEADS * V_DIM, D_MODEL)
    out['l0_w_out'] = jnp.concatenate([d[:CONV_WIDTH], da], axis=0)
    return out


def _pad_taps(w):
    return jnp.pad(w, ((0, (-w.shape[0]) % 8), (0, 0)))


def _ffn_fwd(x, p, pre, tag):
    xn = _rms_fwd(x, p[pre + 'ffn_norm'], f"{tag}_ffn_norm")
    w_up = _interleave(p[pre + 'w_up'])
    hu = _mm(xn, w_up, name=f"{tag}_ffn_up")
    wt = _interleave(_pad_taps(p[pre + 'ffn_conv_w']))
    bias = _interleave(p[pre + 'ffn_conv_b'].reshape(1, -1))
    (act,) = _conv_fwd([(hu, 0)], wt, [0], [bias], [0], _id_pre, _gate_post, [(D_FF, BF16)],
                       name=f"{tag}_ffn_gate", **GATE)
    y = _mm(act, p[pre + 'w_down'], res=x, name=f"{tag}_ffn_down")
    return y, (x, xn, hu, act, wt, bias, w_up)


def _ffn_bwd(dy, saved, p, pre, tag, grads):
    x, xn, hu, act, wt, bias, w_up = saved
    dact = _mm(dy, p[pre + 'w_down'], tb=True, name=f"{tag}_ffn_down_dx")
    grads[pre + 'w_down'] = _mm(act, dy, ta=True, name=f"{tag}_ffn_down_dw")
    dhu, dwt, dbias = _conv_bwd([(hu, 0)], wt, [0], [bias], [0], _id_pre, _gate_post, [(dact, 0)], BF16,
                                name=f"{tag}_ffn_gate_bwd", **GATE)
    grads[pre + 'ffn_conv_w'] = _deinterleave(dwt)[:FFN_K]
    grads[pre + 'ffn_conv_b'] = _deinterleave(dbias).reshape(-1)
    dxn = _mm(dhu, w_up, tb=True, name=f"{tag}_ffn_up_dx")
    grads[pre + 'w_up'] = _deinterleave(_mm(xn, dhu, ta=True, name=f"{tag}_ffn_up_dw"))
    dx, dg = _rms_bwd(x, p[pre + 'ffn_norm'], dxn, dy, f"{tag}_ffn_norm_bwd")
    grads[pre + 'ffn_norm'] = dg.reshape(-1)
    return dx


def _mla_fwd(x, p, q, tabs):
    xn = _rms_fwd(x, p['l0_mix_norm'], "l0_mix_norm")
    hmat = _mm(xn, q['w_in'], name="l0_in")
    wt = _pad_taps(p['l0_conv_w'])
    cpar = [p['l0_conv_b'].reshape(1, -1), p['l0_conv_ln_g'].reshape(1, -1), p['l0_conv_ln_b'].reshape(1, -1)]
    (u,) = _conv_fwd([(hmat, 0), (hmat, 1)], wt, [0], cpar, [0, 0, 0], _glu_pre, _ln_silu_post,
                     [(CONV_WIDTH, BF16)], name="l0_conv", **CONVA)
    qn, kvn = p['l0_q_norm'].reshape(1, -1), p['l0_kv_norm'].reshape(1, -1)
    cqn, ckvn = _rowwise(lambda a, b, ga, gb: (_rms(a, ga), _rms(b, gb)),
                         [(hmat, Q_RANK, 2 * CONV_WIDTH // Q_RANK), (hmat, KV_RANK, (2 * CONV_WIDTH + Q_RANK) // KV_RANK)],
                         [qn, kvn], [(Q_RANK, BF16), (KV_RANK, BF16)], [], ts=512, name="l0_latent_norm")
    qf = _mm(cqn, q['w_uq'], name="l0_uq")
    kvf = _mm(ckvn, q['w_ukv'], name="l0_ukv")
    q_rot, k_full, v = _rope_fwd(qf, kvf, hmat, tabs, "l0_rope")
    o, lse = _attn_fwd(q_rot, k_full, v, "l0_attn")
    cat = jnp.concatenate([u, o], axis=1)
    y = _mm(cat, q['w_out'], res=x, name="l0_out")
    return y, (x, xn, hmat, wt, cpar, qn, kvn, cqn, ckvn, q_rot, k_full, v, lse, cat)


def _mla_bwd(dy, saved, p, q, tabs, grads, gq):
    x, xn, hmat, wt, cpar, qn, kvn, cqn, ckvn, q_rot, k_full, v, lse, cat = saved
    dcat = _mm(dy, q['w_out'], tb=True, name="l0_out_dx")
    gq['w_out'] = _mm(cat, dy, ta=True, name="l0_out_dw")
    dq, lse_r, dlt_r = _attn_bwd_q(q_rot, k_full, v, cat, dcat, lse, "l0_attn_bwd_q")
    dk, dv = _attn_bwd_kv(q_rot, k_full, v, dcat, lse_r, dlt_r, "l0_attn_bwd_kv")
    dqf, dkvf, dkr = _rope_bwd(dq, dk, dv, tabs, "l0_rope_bwd")
    dcqn = _mm(dqf, q['w_uq'], tb=True, name="l0_uq_dx")
    gq['w_uq'] = _mm(cqn, dqf, ta=True, name="l0_uq_dw")
    dckvn = _mm(dkvf, q['w_ukv'], tb=True, name="l0_ukv_dx")
    gq['w_ukv'] = _mm(ckvn, dkvf, ta=True, name="l0_ukv_dw")

    def lat_bwd(a, b, da, db, ga, gb):
        _, vjp = jax.vjp(lambda a_, b_, ga_, gb_: (_rms(a_, ga_), _rms(b_, gb_)), a, b, ga, gb)
        return vjp((da, db))

    dcq, dckv, dqn, dkvn = _rowwise(
        lat_bwd, [(hmat, Q_RANK, 2 * CONV_WIDTH // Q_RANK), (hmat, KV_RANK, (2 * CONV_WIDTH + Q_RANK) // KV_RANK),
                  (dcqn, Q_RANK, 0), (dckvn, KV_RANK, 0)],
        [qn, kvn], [(Q_RANK, F32), (KV_RANK, F32)], [(1, Q_RANK), (1, KV_RANK)], ts=512, name="l0_latent_norm_bwd")
    grads['l0_q_norm'], grads['l0_kv_norm'] = dqn.reshape(-1), dkvn.reshape(-1)
    da, dg, dwt, dcb, dlg, dlb = _conv_bwd(
        [(hmat, 0), (hmat, 1)], wt, [0], cpar, [0, 0, 0], _glu_pre, _ln_silu_post, [(dcat, 0)], F32,
        name="l0_conv_bwd", **CONVA)
    grads['l0_conv_w'] = dwt[:CONV_K]
    grads['l0_conv_b'], grads['l0_conv_ln_g'], grads['l0_conv_ln_b'] = dcb.reshape(-1), dlg.reshape(-1), dlb.reshape(-1)
    dh = jnp.concatenate([da, dg, dcq, dckv, dkr], axis=1)
    dxn = _mm(dh, q['w_in'], tb=True, name="l0_in_dx")
    gq['w_in'] = _mm(xn, dh, ta=True, name="l0_in_dw")
    dx, dgn = _rms_bwd(x, p['l0_mix_norm'], dxn, dy, "l0_mix_norm_bwd")
    grads['l0_mix_norm'] = dgn.reshape(-1)
    return dx


def _gelu_skip(yc, u, d):
    return jax.nn.gelu(yc + d * u)


def _glu_out(z1, z2, b1, b2, x):
    return x + (z1 + b1) * jax.nn.sigmoid(z2 + b2)


def _s5_fwd(x, p):
    xn = _rms_fwd(x, p['l1_mix_norm'], "l1_mix_norm")
    u = _mm(xn, p['l1_w_in'], name="l1_in")
    lb_re, lb_im, bb_re, bb_im = _s5_disc(p['l1_log_dt'], p['l1_a_re'], p['l1_a_im'], p['l1_b_re'], p['l1_b_im'])
    lam = jnp.concatenate([lb_re.reshape(1, -1), lb_im.reshape(1, -1)], axis=1)
    tr = lambda a: jnp.transpose(a, (0, 2, 1))
    bmat = jnp.concatenate([_bd(tr(bb_re)), _bd(tr(bb_im))], axis=1)
    cmat = jnp.concatenate([_bd(tr(p['l1_c_re'])), -_bd(tr(p['l1_c_im']))], axis=0)
    bu = _mm(u, bmat, name="l1_bu")
    (xs,) = _scan(bu, lam, reverse=False, name="l1_scan")
    yc = _mm(xs, cmat, name="l1_cx")
    dsk = p['l1_d'].reshape(1, -1)
    (y,) = _rowwise(_gelu_skip, [(yc, SSM_WIDTH, 0), (u, SSM_WIDTH, 0)], [dsk], [(SSM_WIDTH, BF16)], [],
                    ts=512, name="l1_gelu")
    z = _mm(y, p['l1_w_glu'], name="l1_glu")
    bg = p['l1_b_glu'].reshape(1, -1)
    (out,) = _rowwise(lambda z1, z2, xv, b1, b2: _glu_out(z1, z2, b1, b2, xv),
                      [(z, D_MODEL, 0), (z, D_MODEL, 1), (x, D_MODEL, 0)], [bg[:, :D_MODEL], bg[:, D_MODEL:]],
                      [(D_MODEL, F32)], [], ts=512, name="l1_glu_out")
    return out, (x, xn, u, lam, bmat, cmat, xs, yc, dsk, y, z, bg)


def _s5_bwd(dy, saved, p, grads):
    x, xn, u, lam, bmat, cmat, xs, yc, dsk, y, z, bg = saved

    def glu_bwd(z1, z2, dv, b1, b2):
        _, vjp = jax.vjp(lambda a, b, c, d: (a + c) * jax.nn.sigmoid(b + d), z1, z2, b1, b2)
        d1, d2, db1, db2 = vjp(dv)
        return jnp.concatenate([d1, d2], axis=1), db1, db2

    dz, db1, db2 = _rowwise(glu_bwd, [(z, D_MODEL, 0), (z, D_MODEL, 1), (dy, D_MODEL, 0)],
                            [bg[:, :D_MODEL], bg[:, D_MODEL:]], [(2 * D_MODEL, BF16)], [(1, D_MODEL), (1, D_MODEL)],
                            ts=512, name="l1_glu_out_bwd")
    grads['l1_b_glu'] = jnp.concatenate([db1, db2], axis=1).reshape(-1)
    dyv = _mm(dz, p['l1_w_glu'], tb=True, name="l1_glu_dx")
    grads['l1_w_glu'] = _mm(y, dz, ta=True, name="l1_glu_dw")

    def gelu_bwd(ycv, uv, dv, dk):
        _, vjp = jax.vjp(_gelu_skip, ycv, uv, dk)
        return vjp(dv)

    dyc, du_skip, dd = _rowwise(gelu_bwd, [(yc, SSM_WIDTH, 0), (u, SSM_WIDTH, 0), (dyv, SSM_WIDTH, 0)], [dsk],
                                [(SSM_WIDTH, F32), (SSM_WIDTH, F32)], [(1, SSM_WIDTH)], ts=512, name="l1_gelu_bwd")
    grads['l1_d'] = dd.reshape(-1)
    dxs = _mm(dyc, cmat, tb=True, name="l1_cx_dx")
    dcmat = _mm(xs, dyc, ta=True, name="l1_cx_dw")
    gs, dlam = _scan(dxs, lam, reverse=True, xs=xs, name="l1_scan_bwd")
    dlr, dli = dlam[:, :SSM_CH], dlam[:, SSM_CH:]
    du = _mm(gs, bmat, tb=True, res=du_skip, name="l1_bu_dx")
    dbmat = _mm(u, gs, ta=True, name="l1_bu_dw")
    tr = lambda a: jnp.transpose(a, (0, 2, 1))
    grads['l1_c_re'] = tr(_unbd(dcmat[:SSM_CH], SSM_STATE, SSM_GROUP))
    grads['l1_c_im'] = -tr(_unbd(dcmat[SSM_CH:], SSM_STATE, SSM_GROUP))
    dbb_re = tr(_unbd(dbmat[:, :SSM_CH], SSM_GROUP, SSM_STATE))
    dbb_im = tr(_unbd(dbmat[:, SSM_CH:], SSM_GROUP, SSM_STATE))
    names = ['l1_log_dt', 'l1_a_re', 'l1_a_im', 'l1_b_re', 'l1_b_im']
    _, vjp = jax.vjp(_s5_disc, *[p[n] for n in names])
    for n, gval in zip(names, vjp((dlr.reshape(SSM_GROUPS, SSM_STATE), dli.reshape(SSM_GROUPS, SSM_STATE), dbb_re, dbb_im))):
        grads[n] = gval
    dxn = _mm(du, p['l1_w_in'], tb=True, name="l1_in_dx")
    grads['l1_w_in'] = _mm(xn, du, ta=True, name="l1_in_dw")
    dx, dgn = _rms_bwd(x, p['l1_mix_norm'], dxn, dy, "l1_mix_norm_bwd")
    grads['l1_mix_norm'] = dgn.reshape(-1)
    return dx


def _loss_head(x, g, target):
    d = x.shape[1]

    def fn(xv, tv, gv):
        y, vjp = jax.vjp(_rms, xv, gv)
        err = y - tv
        part = 0.5 * jnp.sum(jnp.mean(jnp.square(err), axis=-1, keepdims=True), axis=0, keepdims=True)
        dx, dg = vjp(err * (1.0 / d))
        return dx, jnp.broadcast_to(part, (1, 128)), dg

    return _rowwise(fn, [(x, d, 0), (target, d, 0)], [g.reshape(1, -1)], [(d, F32)], [(1, 128), (1, d)], ts=512,
                    name="loss_head")


def _local_step(x, target, p):
    q = _prep_weights(p)
    tabs = _rope_tables(x.shape[0])
    x1, s_mla = _mla_fwd(x, p, q, tabs)
    x2, s_f0 = _ffn_fwd(x1, p, 'l0_', "l0")
    x3, s_s5 = _s5_fwd(x2, p)
    x4, s_f1 = _ffn_fwd(x3, p, 'l1_', "l1")
    dx4, loss, dgf = _loss_head(x4, p['final_norm'], target)
    grads, gq = {'final_norm': dgf.reshape(-1)}, {}
    dx3 = _ffn_bwd(dx4, s_f1, p, 'l1_', "l1", grads)
    dx2 = _s5_bwd(dx3, s_s5, p, grads)
    dx1 = _ffn_bwd(dx2, s_f0, p, 'l0_', "l0", grads)
    dx0 = _mla_bwd(dx1, s_mla, p, q, tabs, grads, gq)
    grads.update(_unprep_grads(gq))
    return loss[0, 0], dx0, grads


def _shard_shape(name, full_shape):
    sh = list(full_shape)
    sh[SHARDED[name]] //= N_DEV
    return tuple(sh)


def _assemble(name, stacked):
    ax = SHARDED[name]
    if ax == 0:
        return stacked.reshape((-1,) + stacked.shape[2:])
    return jnp.transpose(stacked, (1, 0, 2)).reshape(stacked.shape[1], -1)


def _split(name, full):
    if SHARDED[name] == 0:
        return full.reshape(N_DEV, -1)
    r, c = full.shape
    return jnp.transpose(full.reshape(r, N_DEV, c // N_DEV), (1, 0, 2)).reshape(N_DEV, -1)


def kernel(x, l0_mix_norm, l0_w_in, l0_conv_w, l0_conv_b, l0_conv_ln_g, l0_conv_ln_b, l0_q_norm, l0_kv_norm, l0_w_uq, l0_w_ukv, l0_w_out, l0_ffn_norm, l0_w_up, l0_ffn_conv_w, l0_ffn_conv_b, l0_w_down, l1_mix_norm, l1_w_in, l1_log_dt, l1_a_re, l1_a_im, l1_b_re, l1_b_im, l1_c_re, l1_c_im, l1_d, l1_w_glu, l1_b_glu, l1_ffn_norm, l1_w_up, l1_ffn_conv_w, l1_ffn_conv_b, l1_w_down, final_norm, loss_target, m_l0_mix_norm, m_l0_w_in, m_l0_conv_w, m_l0_conv_b, m_l0_conv_ln_g, m_l0_conv_ln_b, m_l0_q_norm, m_l0_kv_norm, m_l0_w_uq, m_l0_w_ukv, m_l0_w_out, m_l0_ffn_norm, m_l0_w_up, m_l0_ffn_conv_w, m_l0_ffn_conv_b, m_l0_w_down, m_l1_mix_norm, m_l1_w_in, m_l1_log_dt, m_l1_a_re, m_l1_a_im, m_l1_b_re, m_l1_b_im, m_l1_c_re, m_l1_c_im, m_l1_d, m_l1_w_glu, m_l1_b_glu, m_l1_ffn_norm, m_l1_w_up, m_l1_ffn_conv_w, m_l1_ffn_conv_b, m_l1_w_down, m_final_norm, v_l0_mix_norm, v_l0_w_in, v_l0_conv_w, v_l0_conv_b, v_l0_conv_ln_g, v_l0_conv_ln_b, v_l0_q_norm, v_l0_kv_norm, v_l0_w_uq, v_l0_w_ukv, v_l0_w_out, v_l0_ffn_norm, v_l0_w_up, v_l0_ffn_conv_w, v_l0_ffn_conv_b, v_l0_w_down, v_l1_mix_norm, v_l1_w_in, v_l1_log_dt, v_l1_a_re, v_l1_a_im, v_l1_b_re, v_l1_b_im, v_l1_c_re, v_l1_c_im, v_l1_d, v_l1_w_glu, v_l1_b_glu, v_l1_ffn_norm, v_l1_w_up, v_l1_ffn_conv_w, v_l1_ffn_conv_b, v_l1_w_down, v_final_norm):
    args = dict(locals())
    w = {n: args[n] for n in WEIGHTS}
    m = {n: args['m_' + n] for n in WEIGHTS}
    v = {n: args['v_' + n] for n in WEIGHTS}
    big = [n for n in SHARDED if n not in SMALL_SHARDED]

    got_big = _exchange(_flat([w[n].astype(BF16) for n in big]), gather=True, name="gather_matrices")
    got_small = _exchange(_flat([w[n] for n in SMALL_SHARDED]), gather=True, name="gather_taps")
    p = {n: w[n] for n in REPLICATED}
    for names, got in ((big, got_big), (SMALL_SHARDED, got_small)):
        for n, st in zip(names, _unflat(got, [w[n].shape for n in names])):
            p[n] = _assemble(n, st)

    loss, dx, grads = _local_step(x[0], loss_target[0], p)

    sh_names = list(SHARDED)
    send = _flat([_split(n, grads[n]).astype(BF16) for n in sh_names], lead=(N_DEV,))
    parts = _exchange(send, gather=False, name="scatter_grads")
    flat3 = lambda d: _flat([d[n] for n in sh_names])
    sh_out = _adamw(parts, flat3(w), flat3(m), flat3(v), "adamw_sharded")
    rparts = _exchange(_flat([grads[n] for n in REPLICATED]), gather=True, name="gather_small_grads")
    flatr = lambda d: _flat([d[n] for n in REPLICATED])
    rp_out = _adamw(rparts, flatr(w), flatr(m), flatr(v), "adamw_replicated")

    res = [dict(), dict(), dict(), dict()]
    for kind in range(4):
        for n, a in zip(sh_names, _unflat(sh_out[kind], [w[n].shape for n in sh_names])):
            res[kind][n] = a
        for n, a in zip(REPLICATED, _unflat(rp_out[kind], [w[n].shape for n in REPLICATED])):
            res[kind][n] = a
    total = lax.psum(loss, ("x", "y", "c"))
    return (total, dx[None], *[res[0][n] for n in WEIGHTS], *[res[1][n] for n in WEIGHTS],
            *[res[2][n] for n in WEIGHTS], *[res[3][n] for n in WEIGHTS])
```

```python
import functools

import jax
import jax.numpy as jnp
from jax import lax
from jax.experimental import pallas as pl
from jax.experimental.pallas import tpu as pltpu

F32 = jnp.float32
BF16 = jnp.bfloat16

N_DEV = 8
D_MODEL = 1024
EPS = 1e-6
LN_EPS = 1e-5
CONV_WIDTH = 512
CONV_K = 31
N_HEADS = 8
QK_NOPE = 64
QK_ROPE = 32
V_DIM = 64
HEAD_PAD = 128
Q_RANK = 256
KV_RANK = 128
ROPE_BASE = 10000.0
IN_EVEN = 2 * CONV_WIDTH + Q_RANK + KV_RANK + QK_ROPE
IN_PAD = 1536
KR_LANE = 64
SSM_WIDTH = 512
SSM_GROUP = 16
SSM_GROUPS = 32
SSM_STATE = 64
SSM_CH = SSM_GROUPS * SSM_STATE
D_FF = 2816
FF_SHARD = 2 * D_FF // N_DEV
FF_HALF = N_DEV // 2
FFN_K = 3
TAP_ROWS = 8
ADAM_LR, ADAM_B1, ADAM_B2, ADAM_EPS, ADAM_WD, ADAM_STEP = 0.001, 0.9, 0.999, 1e-08, 0.01, 10

WEIGHTS = ['l0_mix_norm', 'l0_w_in', 'l0_conv_w', 'l0_conv_b', 'l0_conv_ln_g', 'l0_conv_ln_b', 'l0_q_norm',
           'l0_kv_norm', 'l0_w_uq', 'l0_w_ukv', 'l0_w_out', 'l0_ffn_norm', 'l0_w_up', 'l0_ffn_conv_w',
           'l0_ffn_conv_b', 'l0_w_down', 'l1_mix_norm', 'l1_w_in', 'l1_log_dt', 'l1_a_re', 'l1_a_im', 'l1_b_re',
           'l1_b_im', 'l1_c_re', 'l1_c_im', 'l1_d', 'l1_w_glu', 'l1_b_glu', 'l1_ffn_norm', 'l1_w_up',
           'l1_ffn_conv_w', 'l1_ffn_conv_b', 'l1_w_down', 'final_norm']
SHARDED = {'l0_w_in': 1, 'l0_conv_w': 1, 'l0_w_uq': 1, 'l0_w_ukv': 1, 'l0_w_out': 0, 'l0_w_up': 1,
           'l0_ffn_conv_w': 1, 'l0_w_down': 0, 'l1_w_in': 0, 'l1_w_glu': 1, 'l1_w_up': 1, 'l1_ffn_conv_w': 1,
           'l1_w_down': 0}
TAPS = ('l0_conv_w', 'l0_ffn_conv_w', 'l1_ffn_conv_w')
REPLICATED = [n for n in WEIGHTS if n not in SHARDED]


def _tile(n, cands):
    for c in cands:
        if n % c == 0:
            return c
    return n


def _same(g):
    return g


def _mm(a, b, *, ta=False, tb=False, res=None, out_dtype=F32, name, ga=None, gb=None, go=None, groups=1):
    a2, b2 = (a.shape[1:] if ga else a.shape), (b.shape[1:] if gb else b.shape)
    m, kd = (a2[1], a2[0]) if ta else a2
    kd2, n = (b2[1], b2[0]) if tb else b2
    assert kd == kd2, (a.shape, b.shape, ta, tb)
    tm = _tile(m, (1024, 512, 256, 128))
    tn = _tile(n, (512, 384, 256, 128))
    tk = _tile(kd, (1024, 512, 256, 128))
    nk = kd // tk
    summed = go is None and (ga is not None or gb is not None)
    nkk = nk * (groups if summed else 1)
    dn = (((0 if ta else 1,), (1 if tb else 0,)), ((), ()))

    def body(*refs):
        if res is None:
            a_ref, b_ref, o_ref, acc_ref = refs
            r_ref = None
        else:
            a_ref, b_ref, r_ref, o_ref, acc_ref = refs
        k = pl.program_id(3)
        p = lax.dot_general(a_ref[...].astype(BF16), b_ref[...].astype(BF16), dn, preferred_element_type=F32)

        @pl.when(k == 0)
        def _():
            acc_ref[...] = p

        @pl.when(k > 0)
        def _():
            acc_ref[...] += p

        @pl.when(k == nkk - 1)
        def _():
            out = acc_ref[...]
            if r_ref is not None:
                out = out + r_ref[...]
            o_ref[...] = out.astype(out_dtype)

    def spec(shape2, idx2, gmap):
        if gmap is None:
            return pl.BlockSpec(shape2, lambda g, i, j, kk: idx2(i, j, kk % nk))
        if summed:
            return pl.BlockSpec((None,) + shape2, lambda g, i, j, kk: (gmap(kk // nk),) + idx2(i, j, kk % nk))
        return pl.BlockSpec((None,) + shape2, lambda g, i, j, kk: (gmap(g),) + idx2(i, j, kk))

    a_spec = spec((tk, tm), lambda i, j, k: (k, i), ga) if ta else spec((tm, tk), lambda i, j, k: (i, k), ga)
    b_spec = spec((tn, tk), lambda i, j, k: (j, k), gb) if tb else spec((tk, tn), lambda i, j, k: (k, j), gb)
    o_spec = spec((tm, tn), lambda i, j, k: (i, j), go)
    in_specs, args = [a_spec, b_spec], [a, b]
    if res is not None:
        in_specs.append(o_spec)
        args.append(res)
    out_shape = (groups, m, n) if go else (m, n)
    return pl.pallas_call(
        body, grid=(groups if go else 1, m // tm, n // tn, nkk), in_specs=in_specs, out_specs=o_spec,
        out_shape=jax.ShapeDtypeStruct(out_shape, out_dtype), scratch_shapes=[pltpu.VMEM((tm, tn), F32)],
        compiler_params=pltpu.CompilerParams(dimension_semantics=("parallel", "parallel", "parallel", "arbitrary")),
        name=name)(*args)


def _rowwise(fn, rows, bcasts, row_outs, red_outs, *, ts, name):
    s = rows[0][0].shape[0]
    nr, nb, nro, nre = len(rows), len(bcasts), len(row_outs), len(red_outs)

    def body(*refs):
        i = pl.program_id(0)
        outs = fn(*[r[...] for r in refs[:nr + nb]])
        if not isinstance(outs, (tuple, list)):
            outs = (outs,)
        o_refs = refs[nr + nb:]
        for q in range(nro):
            o_refs[q][...] = outs[q].astype(o_refs[q].dtype)
        for q in range(nro, nro + nre):
            @pl.when(i == 0)
            def _(q=q):
                o_refs[q][...] = outs[q]

            @pl.when(i > 0)
            def _(q=q):
                o_refs[q][...] += outs[q]

    in_specs = [pl.BlockSpec((ts, w), functools.partial(lambda i, cb: (i, cb), cb=cb)) for (_, w, cb) in rows]
    in_specs += [pl.BlockSpec(b.shape, functools.partial(lambda i, nd: (0,) * nd, nd=b.ndim)) for b in bcasts]
    out_specs = [pl.BlockSpec((ts, w), lambda i: (i, 0)) for (w, _) in row_outs]
    out_specs += [pl.BlockSpec((r, w), lambda i: (0, 0)) for (r, w) in red_outs]
    out_shape = [jax.ShapeDtypeStruct((s, w), dt) for (w, dt) in row_outs]
    out_shape += [jax.ShapeDtypeStruct((r, w), F32) for (r, w) in red_outs]
    return pl.pallas_call(
        body, grid=(s // ts,), in_specs=in_specs, out_specs=out_specs, out_shape=out_shape,
        compiler_params=pltpu.CompilerParams(dimension_semantics=("arbitrary",)), name=name,
    )(*[r[0] for r in rows], *bcasts)


def _rms(x, g):
    return x * lax.rsqrt(jnp.mean(x * x, axis=-1, keepdims=True) + EPS) * g


def _rms_fwd(x, g, name):
    return _rowwise(lambda xv, gv: _rms(xv, gv), [(x, x.shape[1], 0)], [g.reshape(1, -1)],
                    [(x.shape[1], BF16)], [], ts=512, name=name)[0]


def _rms_bwd(x, g, dxn, dres, name):
    d = x.shape[1]

    def fn(xv, dv, rv, gv):
        _, vjp = jax.vjp(_rms, xv, gv)
        dx, dg = vjp(dv.astype(F32))
        return rv + dx, dg

    return _rowwise(fn, [(x, d, 0), (dxn, d, 0), (dres, d, 0)], [g.reshape(1, -1)], [(d, F32)], [(1, d)],
                    ts=512, name=name)


def _cspec(mode, off, rows, width, rowblk, n_rb):
    if mode == 'col':
        return pl.BlockSpec((rows, width), lambda jc, i: (rowblk(i), off + jc))
    return pl.BlockSpec((rows, width), lambda jc, i: ((off + jc) * n_rb + rowblk(i), 0))


def _conv_fwd(xins, w, woffs, params, poffs, pre, post, outs, *, mode, s, k_taps, hb, ts, tc, n_cb, name):
    n_s, nx, ncv, npar, no = s // ts, len(xins), len(woffs), len(params), len(outs)
    rpb = ts // hb
    kp = TAP_ROWS * ((k_taps + TAP_ROWS - 1) // TAP_ROWS)
    pr = 1 if mode == 'col' else TAP_ROWS

    def body(*refs):
        mains, halos = refs[:nx], refs[nx:2 * nx]
        w_refs = refs[2 * nx:2 * nx + ncv]
        p_refs = refs[2 * nx + ncv:2 * nx + ncv + npar]
        o_refs = refs[2 * nx + ncv + npar:2 * nx + ncv + npar + no]
        u_s = refs[2 * nx + ncv + npar + no:]
        i = pl.program_id(1)
        um = pre(*[r[...].astype(F32) for r in mains])
        uh = pre(*[r[...].astype(F32) for r in halos])
        first = (i > 0).astype(F32)
        cs = []
        for q in range(ncv):
            u_s[q][pl.ds(0, hb), :] = uh[q] * first
            u_s[q][pl.ds(hb, ts), :] = um[q]
            acc = jnp.zeros((ts, tc), F32)
            for t in range(k_taps):
                acc = acc + w_refs[q][pl.ds(t, 1), :] * u_s[q][pl.ds(hb - (k_taps - 1) + t, ts), :]
            cs.append(acc)
        res = post(cs, [r[0:1, :] for r in p_refs])
        for q in range(no):
            o_refs[q][...] = res[q].astype(o_refs[q].dtype)

    main = lambda i: i
    prev = lambda i: jnp.maximum(i * rpb - 1, 0)
    zero = lambda i: 0
    in_specs = [_cspec(mode, off, ts, tc, main, n_s) for _, off in xins]
    in_specs += [_cspec(mode, off, hb, tc, prev, s // hb) for _, off in xins]
    in_specs += [_cspec(mode, off, kp, tc, zero, 1) for off in woffs]
    in_specs += [_cspec(mode, off, pr, tc, zero, 1) for off in poffs]
    out_specs = [_cspec(mode, 0, ts, tc, main, n_s) for _ in outs]
    oshape = (s, n_cb * tc) if mode == 'col' else (n_cb * s, tc)
    out_shape = [jax.ShapeDtypeStruct(oshape, dt) for dt in outs]
    return pl.pallas_call(
        body, grid=(n_cb, n_s), in_specs=in_specs, out_specs=out_specs, out_shape=out_shape,
        scratch_shapes=[pltpu.VMEM((hb + ts, tc), F32) for _ in range(ncv)],
        compiler_params=pltpu.CompilerParams(dimension_semantics=("parallel", "arbitrary")), name=name,
    )(*[a for a, _ in xins], *[a for a, _ in xins], *([w] * ncv), *params)


def _conv_bwd(xins, w, woffs, params, poffs, pre, post, douts, dx_dtype, *, mode, s, k_taps, hb, ts, tc, n_cb, name):
    n_s, nx, ncv, npar, ndo = s // ts, len(xins), len(woffs), len(params), len(douts)
    rpb = ts // hb
    n_hb = s // hb
    kp = TAP_ROWS * ((k_taps + TAP_ROWS - 1) // TAP_ROWS)
    pr = 1 if mode == 'col' else TAP_ROWS

    def body(*refs):
        pos = 0

        def take(n):
            nonlocal pos
            out = refs[pos:pos + n]
            pos += n
            return out

        mains, prevs, nexts = take(nx), take(nx), take(nx)
        d_mains, d_nexts = take(ndo), take(ndo)
        w_refs, p_refs = take(ncv), take(npar)
        dx_refs, dw_refs, dp_refs = take(nx), take(ncv), take(npar)
        u_s, dc_s = take(ncv), take(ncv)
        i = pl.program_id(1)
        xm = [r[...].astype(F32) for r in mains]
        um = pre(*xm)
        up = pre(*[r[...].astype(F32) for r in prevs])
        un = pre(*[r[...].astype(F32) for r in nexts])
        first = (i > 0).astype(F32)
        last = (i < n_s - 1).astype(F32)
        pv = [r[0:1, :] for r in p_refs]
        c_main, c_next = [], []
        for q in range(ncv):
            u_s[q][pl.ds(0, hb), :] = up[q] * first
            u_s[q][pl.ds(hb, ts), :] = um[q]
            u_s[q][pl.ds(hb + ts, hb), :] = un[q]
            acc = jnp.zeros((ts + hb, tc), F32)
            for t in range(k_taps):
                acc = acc + w_refs[q][pl.ds(t, 1), :] * u_s[q][pl.ds(hb - (k_taps - 1) + t, ts + hb), :]
            c_main.append(acc[:ts])
            c_next.append(acc[ts:])
        _, vjp_m = jax.vjp(lambda c, p: tuple(post(c, p)), c_main, pv)
        dc_m, dpar = vjp_m(tuple(r[...].astype(F32) for r in d_mains))
        _, vjp_n = jax.vjp(lambda c: tuple(post(c, pv)), c_next)
        (dc_n,) = vjp_n(tuple(r[...].astype(F32) * last for r in d_nexts))
        dus = []
        for q in range(ncv):
            dc_s[q][pl.ds(0, ts), :] = dc_m[q]
            dc_s[q][pl.ds(ts, hb), :] = dc_n[q]

            @pl.when(i == 0)
            def _(q=q):
                dw_refs[q][...] = jnp.zeros((kp, tc), F32)

            acc = jnp.zeros((ts, tc), F32)
            for t in range(k_taps):
                acc = acc + w_refs[q][pl.ds(t, 1), :] * dc_s[q][pl.ds(k_taps - 1 - t, ts), :]
                dw_refs[q][pl.ds(t, 1), :] += jnp.sum(
                    dc_m[q] * u_s[q][pl.ds(hb - (k_taps - 1) + t, ts), :], axis=0, keepdims=True)
            dus.append(acc)
        _, vjp_p = jax.vjp(lambda *xv: tuple(pre(*xv)), *xm)
        dxs = vjp_p(tuple(dus))
        for q in range(nx):
            dx_refs[q][...] = dxs[q].astype(dx_refs[q].dtype)
        for q in range(npar):
            @pl.when(i == 0)
            def _(q=q):
                dp_refs[q][...] = jnp.zeros((pr, tc), F32)

            dp_refs[q][0:1, :] += dpar[q]

    main = lambda i: i
    prev = lambda i: jnp.maximum(i * rpb - 1, 0)
    nxt = lambda i: jnp.minimum((i + 1) * rpb, n_hb - 1)
    zero = lambda i: 0
    in_specs = [_cspec(mode, off, ts, tc, main, n_s) for _, off in xins]
    in_specs += [_cspec(mode, off, hb, tc, prev, n_hb) for _, off in xins]
    in_specs += [_cspec(mode, off, hb, tc, nxt, n_hb) for _, off in xins]
    in_specs += [_cspec(mode, off, ts, tc, main, n_s) for _, off in douts]
    in_specs += [_cspec(mode, off, hb, tc, nxt, n_hb) for _, off in douts]
    in_specs += [_cspec(mode, off, kp, tc, zero, 1) for off in woffs]
    in_specs += [_cspec(mode, off, pr, tc, zero, 1) for off in poffs]
    out_specs = [_cspec(mode, 0, ts, tc, main, n_s) for _ in xins]
    out_specs += [_cspec(mode, 0, kp, tc, zero, 1) for _ in woffs]
    out_specs += [_cspec(mode, 0, pr, tc, zero, 1) for _ in params]

    def shape(rows):
        return (rows, n_cb * tc) if mode == 'col' else (n_cb * rows, tc)

    out_shape = [jax.ShapeDtypeStruct(shape(s), dx_dtype) for _ in xins]
    out_shape += [jax.ShapeDtypeStruct(shape(kp), F32) for _ in woffs]
    out_shape += [jax.ShapeDtypeStruct(shape(pr), F32) for _ in params]
    xa = [a for a, _ in xins]
    da = [a for a, _ in douts]
    return pl.pallas_call(
        body, grid=(n_cb, n_s), in_specs=in_specs, out_specs=out_specs, out_shape=out_shape,
        scratch_shapes=[pltpu.VMEM((hb + ts + hb, tc), F32) for _ in range(ncv)]
        + [pltpu.VMEM((ts + hb, tc), F32) for _ in range(ncv)],
        compiler_params=pltpu.CompilerParams(dimension_semantics=("parallel", "arbitrary")), name=name,
    )(*xa, *xa, *xa, *da, *da, *([w] * ncv), *params)


def _glu_pre(a, g):
    return [a * jax.nn.sigmoid(g)]


def _ln_silu_post(cs, ps):
    c = cs[0] + ps[0]
    mu = jnp.mean(c, axis=-1, keepdims=True)
    var = jnp.mean(jnp.square(c - mu), axis=-1, keepdims=True)
    y = (c - mu) * lax.rsqrt(var + LN_EPS) * ps[1] + ps[2]
    return [jax.nn.silu(y)]


def _pair_pre(a, b):
    return [a, b]


def _gate_post(cs, ps):
    return [jax.nn.silu(cs[0] + ps[0]) * (cs[1] + ps[1])]


def _conva(s):
    return dict(mode='col', s=s, k_taps=CONV_K, hb=32, ts=512, tc=CONV_WIDTH, n_cb=1)


def _gate(s):
    return dict(mode='row', s=s, k_taps=FFN_K, hb=8, ts=512, tc=FF_SHARD, n_cb=FF_HALF)


def _rope_tables(s):
    half = QK_ROPE // 2
    inv = ROPE_BASE ** (-jnp.arange(half, dtype=F32) / half)
    ang = jnp.arange(s, dtype=F32)[:, None] * inv[None, :]
    cos, sin = jnp.cos(ang), jnp.sin(ang)
    z = lambda n: jnp.zeros((s, n), F32)
    c = jnp.concatenate([jnp.ones((s, QK_NOPE), F32), cos, cos, z(HEAD_PAD - QK_NOPE - QK_ROPE)], axis=1)
    s1 = jnp.concatenate([z(QK_NOPE), -sin, z(HEAD_PAD - QK_NOPE - half)], axis=1)
    s2 = jnp.concatenate([z(QK_NOPE + half), sin, z(HEAD_PAD - QK_NOPE - QK_ROPE)], axis=1)
    return c, s1, s2


def _rot(t, c, s1, s2):
    half = QK_ROPE // 2
    return t * c + pltpu.roll(t, HEAD_PAD - half, 1) * s1 + pltpu.roll(t, half, 1) * s2


def _rot_t(d, c, s1, s2):
    half = QK_ROPE // 2
    return d * c + pltpu.roll(d * s1, half, 1) + pltpu.roll(d * s2, HEAD_PAD - half, 1)


def _heads(v):
    return [v[:, h * HEAD_PAD:(h + 1) * HEAD_PAD] for h in range(N_HEADS)]


def _rope_fwd(qf, kvf, hmat, tabs, name):
    w = N_HEADS * HEAD_PAD

    def fn(q, k, v, kr, c, s1, s2):
        krr = _rot(kr, c, s1, s2)
        qo = jnp.concatenate([_rot(t, c, s1, s2) for t in _heads(q)], axis=1)
        ko = jnp.concatenate([t + krr for t in _heads(k)], axis=1)
        return qo, ko, v

    rows = [(qf, w, 0), (kvf, w, 0), (kvf, w, 1), (hmat, HEAD_PAD, IN_PAD // HEAD_PAD - 1)]
    rows += [(t, HEAD_PAD, 0) for t in tabs]
    return _rowwise(fn, rows, [], [(w, BF16)] * 3, [], ts=512, name=name)


def _rope_bwd(dq, dk, dv, tabs, name):
    w = N_HEADS * HEAD_PAD

    def fn(dqv, dkv, dvv, c, s1, s2):
        dqo = jnp.concatenate([_rot_t(t, c, s1, s2) for t in _heads(dqv)], axis=1)
        ksum = functools.reduce(lambda a, b: a + b, _heads(dkv))
        return dqo, jnp.concatenate([dkv, dvv], axis=1), _rot_t(ksum, c, s1, s2)

    rows = [(dq, w, 0), (dk, w, 0), (dv, w, 0)] + [(t, HEAD_PAD, 0) for t in tabs]
    return _rowwise(fn, rows, [], [(w, BF16), (2 * w, BF16), (HEAD_PAD, F32)], [], ts=512, name=name)


ATT_Q = 1024
ATT_SUB = 256
ATT_KV = 512
ATT_SCALE = (QK_NOPE + QK_ROPE) ** -0.5
LOG2E = 1.4426950408889634
ATT_C2 = ATT_SCALE * LOG2E


def _nt(a, b):
    return lax.dot_general(a, b, (((1,), (1,)), ((), ())), preferred_element_type=F32)


def _lanes(x, w):
    return x if w == HEAD_PAD else jnp.tile(x, (1, w // HEAD_PAD))


def _tri(w, transposed):
    r = lax.broadcasted_iota(jnp.int32, (w, w), 0)
    c = lax.broadcasted_iota(jnp.int32, (w, w), 1)
    return (r <= c) if transposed else (c <= r)


def _attn_fwd(q, k, v, name):
    s = q.shape[0]
    tq = min(ATT_Q, s)
    nsub, per = tq // ATT_SUB, tq // ATT_KV

    def body(q_ref, k_ref, v_ref, o_ref, lse_ref, m_s, l_s, acc_s):
        i = pl.program_id(1)
        m_s[...] = jnp.full((tq, HEAD_PAD), -jnp.inf, F32)
        l_s[...] = jnp.zeros((tq, HEAD_PAD), F32)
        acc_s[...] = jnp.zeros((tq, HEAD_PAD), F32)

        def update(r, kb, vb, diag):
            rows = pl.ds(r * ATT_SUB, ATT_SUB)
            w = kb.shape[0]
            sc = _nt(q_ref[rows, :], kb)
            if diag:
                sc = jnp.where(_tri(w, False), sc, -jnp.inf)
            m_prev = m_s[rows, :]
            m_next = jnp.maximum(m_prev, jnp.max(sc, axis=1, keepdims=True))
            p = jnp.exp2((sc - _lanes(m_next, w)) * ATT_C2)
            alpha = jnp.exp2((m_prev - m_next) * ATT_C2)
            l_s[rows, :] = alpha * l_s[rows, :] + jnp.sum(p, axis=1, keepdims=True)
            acc_s[rows, :] = alpha * acc_s[rows, :] + jnp.dot(p.astype(BF16), vb, preferred_element_type=F32)
            m_s[rows, :] = m_next

        def below(j, carry):
            at = pl.ds(pl.multiple_of(j * ATT_KV, ATT_KV), ATT_KV)
            kb, vb = k_ref[at, :], v_ref[at, :]
            for r in range(nsub):
                update(r, kb, vb, False)
            return carry

        lax.fori_loop(0, i * per, below, 0)
        for r in range(nsub):
            for c in range(r + 1):
                at = pl.ds(pl.multiple_of(i * tq + c * ATT_SUB, ATT_SUB), ATT_SUB)
                update(r, k_ref[at, :], v_ref[at, :], c == r)
        o_ref[...] = (acc_s[...] / l_s[...]).astype(BF16)
        lse_ref[...] = m_s[...] * ATT_SCALE + jnp.log(l_s[...])

    q_spec = pl.BlockSpec((tq, HEAD_PAD), lambda h, i: (i, h))
    kv_spec = pl.BlockSpec((s, HEAD_PAD), lambda h, i: (0, h))
    return pl.pallas_call(
        body, grid=(N_HEADS, s // tq), in_specs=[q_spec, kv_spec, kv_spec], out_specs=[q_spec, q_spec],
        out_shape=[jax.ShapeDtypeStruct(q.shape, BF16), jax.ShapeDtypeStruct(q.shape, F32)],
        scratch_shapes=[pltpu.VMEM((tq, HEAD_PAD), F32)] * 3,
        compiler_params=pltpu.CompilerParams(dimension_semantics=("parallel", "arbitrary")), name=name,
    )(q, k, v)


def _attn_bwd_q(q, k, v, cat, dcat, lse, name):
    s = q.shape[0]
    tq = min(ATT_Q, s)
    nsub, per = tq // ATT_SUB, tq // ATT_KV
    ob = CONV_WIDTH // HEAD_PAD

    def body(q_ref, k_ref, v_ref, o_ref, do_ref, lse_ref, dq_ref, lser_ref, dltr_ref, dq_s, dl_s, l2_s):
        i = pl.program_id(1)
        dl_s[...] = jnp.broadcast_to(jnp.sum(do_ref[...] * o_ref[...].astype(F32), axis=1, keepdims=True),
                                     (tq, HEAD_PAD))
        l2_s[...] = lse_ref[...] * LOG2E
        dq_s[...] = jnp.zeros((tq, HEAD_PAD), F32)

        def update(r, kb, vb, diag):
            rows = pl.ds(r * ATT_SUB, ATT_SUB)
            w = kb.shape[0]
            sc = _nt(q_ref[rows, :], kb)
            if diag:
                sc = jnp.where(_tri(w, False), sc, -jnp.inf)
            p = jnp.exp2(sc * ATT_C2 - _lanes(l2_s[rows, :], w))
            dp = _nt(do_ref[rows, :].astype(BF16), vb)
            ds = p * (dp - _lanes(dl_s[rows, :], w))
            dq_s[rows, :] += jnp.dot(ds.astype(BF16), kb, preferred_element_type=F32)

        def below(j, carry):
            at = pl.ds(pl.multiple_of(j * ATT_KV, ATT_KV), ATT_KV)
            kb, vb = k_ref[at, :], v_ref[at, :]
            for r in range(nsub):
                update(r, kb, vb, False)
            return carry

        lax.fori_loop(0, i * per, below, 0)
        for r in range(nsub):
            for c in range(r + 1):
                at = pl.ds(pl.multiple_of(i * tq + c * ATT_SUB, ATT_SUB), ATT_SUB)
                update(r, k_ref[at, :], v_ref[at, :], c == r)
        dq_ref[...] = dq_s[...] * ATT_SCALE
        for c in range(per):
            at = pl.ds(c * ATT_KV, ATT_KV)
            lser_ref[c] = jnp.transpose(l2_s[at, :])[0:8, :]
            dltr_ref[c] = jnp.transpose(dl_s[at, :])[0:8, :]

    q_spec = pl.BlockSpec((tq, HEAD_PAD), lambda h, i: (i, h))
    kv_spec = pl.BlockSpec((s, HEAD_PAD), lambda h, i: (0, h))
    o_spec = pl.BlockSpec((tq, HEAD_PAD), lambda h, i: (i, ob + h))
    row_spec = pl.BlockSpec((None, per, 8, ATT_KV), lambda h, i: (h, i, 0, 0))
    rows = jax.ShapeDtypeStruct((N_HEADS, s // ATT_KV, 8, ATT_KV), F32)
    return pl.pallas_call(
        body, grid=(N_HEADS, s // tq), in_specs=[q_spec, kv_spec, kv_spec, o_spec, o_spec, q_spec],
        out_specs=[q_spec, row_spec, row_spec], out_shape=[jax.ShapeDtypeStruct(q.shape, F32), rows, rows],
        scratch_shapes=[pltpu.VMEM((tq, HEAD_PAD), F32)] * 3,
        compiler_params=pltpu.CompilerParams(dimension_semantics=("parallel", "arbitrary")), name=name,
    )(q, k, v, cat, dcat, lse)


def _attn_bwd_kv(q, k, v, dcat, lse_r, dlt_r, name):
    s = q.shape[0]
    tk = min(ATT_Q, s)
    nsub, per, n_chunks = tk // ATT_SUB, tk // ATT_KV, s // ATT_KV
    ob = CONV_WIDTH // HEAD_PAD

    def body(k_ref, v_ref, q_ref, do_ref, lse_ref, dl_ref, dk_ref, dv_ref, dk_s, dv_s):
        j = pl.program_id(1)
        dk_s[...] = jnp.zeros((tk, HEAD_PAD), F32)
        dv_s[...] = jnp.zeros((tk, HEAD_PAD), F32)

        def update(r, qb, dob, lrow, drow, diag):
            rows = pl.ds(r * ATT_SUB, ATT_SUB)
            sc = _nt(k_ref[rows, :], qb)
            if diag:
                sc = jnp.where(_tri(qb.shape[0], True), sc, -jnp.inf)
            p = jnp.exp2(sc * ATT_C2 - lrow)
            dp = _nt(v_ref[rows, :], dob)
            ds = p * (dp - drow)
            dv_s[rows, :] += jnp.dot(p.astype(BF16), dob, preferred_element_type=F32)
            dk_s[rows, :] += jnp.dot(ds.astype(BF16), qb, preferred_element_type=F32)

        def above(ic, carry):
            at = pl.ds(pl.multiple_of(ic * ATT_KV, ATT_KV), ATT_KV)
            qb, dob = q_ref[at, :], do_ref[at, :].astype(BF16)
            lrow, drow = lse_ref[ic, 0:1, :], dl_ref[ic, 0:1, :]
            for r in range(nsub):
                update(r, qb, dob, lrow, drow, False)
            return carry

        lax.fori_loop((j + 1) * per, n_chunks, above, 0)
        for r in range(nsub):
            for c in range(r, nsub):
                at = pl.ds(pl.multiple_of(j * tk + c * ATT_SUB, ATT_SUB), ATT_SUB)
                ic = j * per + (c * ATT_SUB) // ATT_KV
                lo = (c * ATT_SUB) % ATT_KV
                update(r, q_ref[at, :], do_ref[at, :].astype(BF16), lse_ref[ic, 0:1, lo:lo + ATT_SUB],
                       dl_ref[ic, 0:1, lo:lo + ATT_SUB], c == r)
        dk_ref[...] = dk_s[...] * ATT_SCALE
        dv_ref[...] = dv_s[...]

    kv_spec = pl.BlockSpec((tk, HEAD_PAD), lambda h, j: (j, h))
    q_spec = pl.BlockSpec((s, HEAD_PAD), lambda h, j: (0, h))
    do_spec = pl.BlockSpec((s, HEAD_PAD), lambda h, j: (0, ob + h))
    row_spec = pl.BlockSpec((None, n_chunks, 8, ATT_KV), lambda h, j: (h, 0, 0, 0))
    return pl.pallas_call(
        body, grid=(N_HEADS, s // tk), in_specs=[kv_spec, kv_spec, q_spec, do_spec, row_spec, row_spec],
        out_specs=[kv_spec, kv_spec],
        out_shape=[jax.ShapeDtypeStruct(q.shape, F32), jax.ShapeDtypeStruct(q.shape, F32)],
        scratch_shapes=[pltpu.VMEM((tk, HEAD_PAD), F32)] * 2,
        compiler_params=pltpu.CompilerParams(dimension_semantics=("parallel", "arbitrary")), name=name,
    )(k, v, q, dcat, lse_r, dlt_r)


SCAN_T = 256
SCAN_C = 512


def _scan(b, lam, *, reverse, xs=None, name):
    s = b.shape[0]
    t = min(SCAN_T, s)
    n_t, n_c = s // t, SSM_CH // SCAN_C
    with_dlam = xs is not None

    def shift(a, d, row):
        if d >= 8:
            z = jnp.zeros((d, SCAN_C), F32)
            return jnp.concatenate([a[d:], z], axis=0) if reverse else jnp.concatenate([z, a[:t - d]], axis=0)
        if reverse:
            return jnp.where(row < t - d, pltpu.roll(a, t - d, 0), 0.0)
        return jnp.where(row >= d, pltpu.roll(a, d, 0), 0.0)

    def body(*refs):
        if with_dlam:
            b_ref, lam_ref, x_ref, o_ref, dl_ref, c_s = refs
        else:
            b_ref, lam_ref, o_ref, c_s = refs
        k = pl.program_id(0)

        @pl.when(k == 0)
        def _():
            c_s[...] = jnp.zeros((1, 2 * SSM_CH), F32)
            if with_dlam:
                dl_ref[...] = jnp.zeros((1, 2 * SSM_CH), F32)

        row = lax.broadcasted_iota(jnp.int32, (t, SCAN_C), 0)
        edge = (row == t - 1) if reverse else (row == 0)
        for ch in range(n_c):
            re = pl.ds(ch * SCAN_C, SCAN_C)
            im = pl.ds(SSM_CH + ch * SCAN_C, SCAN_C)
            lr = lam_ref[:, re]
            li = -lam_ref[:, im] if reverse else lam_ref[:, im]
            cr, ci = c_s[:, re], c_s[:, im]
            ar = b_ref[:, re] + jnp.where(edge, lr * cr - li * ci, 0.0)
            ai = b_ref[:, im] + jnp.where(edge, lr * ci + li * cr, 0.0)
            d = 1
            while d < t:
                sr, si = shift(ar, d, row), shift(ai, d, row)
                ar, ai = ar + lr * sr - li * si, ai + lr * si + li * sr
                lr, li = lr * lr - li * li, 2.0 * lr * li
                d *= 2
            o_ref[:, re] = ar
            o_ref[:, im] = ai
            if with_dlam:
                gr = jnp.where(edge, cr, shift(ar, 1, row))
                gi = jnp.where(edge, ci, shift(ai, 1, row))
                xr, xi = x_ref[:, re], x_ref[:, im]
                dl_ref[:, re] += jnp.sum(xr * gr + xi * gi, axis=0, keepdims=True)
                dl_ref[:, im] += jnp.sum(xr * gi - xi * gr, axis=0, keepdims=True)
            last = 0 if reverse else t - 1
            c_s[:, re] = ar[last:last + 1, :]
            c_s[:, im] = ai[last:last + 1, :]

    tm = (lambda k: (n_t - 1 - k, 0)) if reverse else (lambda k: (k, 0))
    blk = pl.BlockSpec((t, 2 * SSM_CH), tm)
    vec = pl.BlockSpec((1, 2 * SSM_CH), lambda k: (0, 0))
    in_specs, args = [blk, vec], [b, lam]
    out_specs, out_shape = [blk], [jax.ShapeDtypeStruct((s, 2 * SSM_CH), F32)]
    if with_dlam:
        in_specs.append(blk)
        args.append(xs)
        out_specs.append(vec)
        out_shape.append(jax.ShapeDtypeStruct((1, 2 * SSM_CH), F32))
    return pl.pallas_call(
        body, grid=(n_t,), in_specs=in_specs, out_specs=out_specs, out_shape=out_shape,
        scratch_shapes=[pltpu.VMEM((1, 2 * SSM_CH), F32)],
        compiler_params=pltpu.CompilerParams(dimension_semantics=("arbitrary",)), name=name,
    )(*args)


def _s5_disc(log_dt, a_re, a_im, b_re, b_im):
    dt = jnp.exp(log_dt)[:, None]
    mag = jnp.exp(a_re * dt)
    lb_re, lb_im = mag * jnp.cos(a_im * dt), mag * jnp.sin(a_im * dt)
    den = a_re * a_re + a_im * a_im
    nr, ni = lb_re - 1.0, lb_im
    f_re = (nr * a_re + ni * a_im) / den
    f_im = (ni * a_re - nr * a_im) / den
    bb_re = f_re[..., None] * b_re - f_im[..., None] * b_im
    bb_im = f_re[..., None] * b_im + f_im[..., None] * b_re
    return lb_re, lb_im, bb_re, bb_im


def _bd(a):
    g, i, j = a.shape
    eye = jnp.eye(g, dtype=a.dtype)
    return (a[:, :, None, :] * eye[:, None, :, None]).reshape(g * i, g * j)


def _unbd(m, i, j):
    g = m.shape[0] // i
    eye = jnp.eye(g, dtype=m.dtype)
    return jnp.sum(m.reshape(g, i, g, j) * eye[:, None, :, None], axis=2)


def _exchange(arrs, modes, name):
    n = len(arrs)
    shapes = [a.shape if md == 'scatter' else (N_DEV,) + a.shape for a, md in zip(arrs, modes)]

    def body(*refs):
        srcs, outs = refs[:n], refs[n:2 * n]
        send_sems, recv_sems, local_sems = refs[2 * n:]
        x, y, c = lax.axis_index("x"), lax.axis_index("y"), lax.axis_index("c")
        me = 4 * x + 2 * y + c

        def piece(q, slot):
            return srcs[q].at[slot] if modes[q] == 'scatter' else srcs[q]

        mine = [pltpu.make_async_copy(piece(q, me), outs[q].at[me], local_sems.at[q]) for q in range(n)]
        for cp in mine:
            cp.start()
        copies = []
        for r in range(1, N_DEV):
            px, py, pc = x ^ (r >> 2), y ^ ((r >> 1) & 1), c ^ (r & 1)
            peer = 4 * px + 2 * py + pc
            for q in range(n):
                copies.append(pltpu.make_async_remote_copy(
                    src_ref=piece(q, peer), dst_ref=outs[q].at[me], send_sem=send_sems.at[(r - 1) * n + q],
                    recv_sem=recv_sems.at[(r - 1) * n + q], device_id=(px, py, pc),
                    device_id_type=pl.DeviceIdType.MESH))
        for cp in copies:
            cp.start()
        for cp in copies:
            cp.wait_recv()
        for cp in copies:
            cp.wait_send()
        for cp in mine:
            cp.wait()

    any_spec = pl.BlockSpec(memory_space=pl.ANY)
    return pl.pallas_call(
        body, out_shape=[jax.ShapeDtypeStruct(sh, a.dtype) for sh, a in zip(shapes, arrs)],
        in_specs=[any_spec] * n, out_specs=[any_spec] * n,
        scratch_shapes=[pltpu.SemaphoreType.DMA(((N_DEV - 1) * n,)), pltpu.SemaphoreType.DMA(((N_DEV - 1) * n,)),
                        pltpu.SemaphoreType.DMA((n,))],
        compiler_params=pltpu.CompilerParams(has_side_effects=True), name=name,
    )(*arrs)


def _adamw(parts, w, m, v, name):
    r, c = w.shape
    tr = _tile(r, (256, 128))

    def body(p_ref, w_ref, m_ref, v_ref, g_ref, d_ref, nm_ref, nv_ref):
        g = p_ref[0].astype(F32)
        for d in range(1, N_DEV):
            g = g + p_ref[d].astype(F32)
        m2 = ADAM_B1 * m_ref[...] + (1.0 - ADAM_B1) * g
        v2 = ADAM_B2 * v_ref[...] + (1.0 - ADAM_B2) * jnp.square(g)
        m_hat = m2 / (1.0 - ADAM_B1 ** ADAM_STEP)
        v_hat = v2 / (1.0 - ADAM_B2 ** ADAM_STEP)
        g_ref[...] = g
        d_ref[...] = -ADAM_LR * (m_hat / (jnp.sqrt(v_hat) + ADAM_EPS) + ADAM_WD * w_ref[...])
        nm_ref[...] = m2
        nv_ref[...] = v2

    spec = pl.BlockSpec((tr, c), lambda i: (i, 0))
    return pl.pallas_call(
        body, grid=(r // tr,), in_specs=[pl.BlockSpec((N_DEV, tr, c), lambda i: (0, i, 0)), spec, spec, spec],
        out_specs=[spec] * 4, out_shape=[jax.ShapeDtypeStruct((r, c), F32)] * 4,
        compiler_params=pltpu.CompilerParams(dimension_semantics=("parallel",)), name=name,
    )(parts, w, m, v)


FLAT_W = 512
FLAT_ROWS = 256


def _flat(arrs):
    v = jnp.concatenate([a.reshape(-1) for a in arrs])
    return jnp.pad(v, (0, (-v.shape[0]) % (FLAT_ROWS * FLAT_W))).reshape(-1, FLAT_W)


def _unflat(flat, shapes):
    v = flat.reshape(-1)
    out, off = [], 0
    for sh in shapes:
        n = 1
        for d in sh:
            n *= d
        out.append(v[off:off + n].reshape(sh))
        off += n
    return out


def _full(name, stacked):
    if SHARDED[name] == 0:
        return stacked.reshape((-1,) + stacked.shape[2:])
    return jnp.transpose(stacked, (1, 0, 2)).reshape(stacked.shape[1], -1)


def _shards(name, full):
    if SHARDED[name] == 0:
        return full.reshape((N_DEV, -1) + full.shape[1:])
    r, c = full.shape
    return jnp.transpose(full.reshape(r, N_DEV, c // N_DEV), (1, 0, 2))


def _prep_weights(p):
    q = {}
    w_in = p['l0_w_in']
    z = lambda n: jnp.zeros((D_MODEL, n), w_in.dtype)
    q['w_in'] = jnp.concatenate([w_in[:, :IN_EVEN - QK_ROPE], z(KR_LANE), w_in[:, IN_EVEN - QK_ROPE:],
                                 z(HEAD_PAD - KR_LANE - QK_ROPE)], axis=1)
    dqk = QK_NOPE + QK_ROPE
    q['w_uq'] = jnp.pad(p['l0_w_uq'].reshape(Q_RANK, N_HEADS, dqk), ((0, 0), (0, 0), (0, HEAD_PAD - dqk))
                        ).reshape(Q_RANK, N_HEADS * HEAD_PAD)
    ukv = p['l0_w_ukv'].reshape(KV_RANK, N_HEADS, 2, QK_NOPE)
    padh = lambda a: jnp.pad(a, ((0, 0), (0, 0), (0, HEAD_PAD - QK_NOPE))).reshape(KV_RANK, N_HEADS * HEAD_PAD)
    q['w_ukv'] = jnp.concatenate([padh(ukv[:, :, 0]), padh(ukv[:, :, 1])], axis=1)
    wo = p['l0_w_out']
    wo_a = jnp.pad(wo[CONV_WIDTH:].reshape(N_HEADS, V_DIM, D_MODEL), ((0, 0), (0, HEAD_PAD - V_DIM), (0, 0)))
    q['w_out'] = jnp.concatenate([wo[:CONV_WIDTH], wo_a.reshape(N_HEADS * HEAD_PAD, D_MODEL)], axis=0)
    return q


def _unprep_grads(g):
    out = {}
    d = g['w_in']
    out['l0_w_in'] = jnp.concatenate([d[:, :IN_EVEN - QK_ROPE],
                                      d[:, IN_EVEN - QK_ROPE + KR_LANE:IN_EVEN + KR_LANE]], axis=1)
    dqk = QK_NOPE + QK_ROPE
    out['l0_w_uq'] = g['w_uq'].reshape(Q_RANK, N_HEADS, HEAD_PAD)[:, :, :dqk].reshape(Q_RANK, N_HEADS * dqk)
    d = g['w_ukv'].reshape(KV_RANK, 2, N_HEADS, HEAD_PAD)[:, :, :, :QK_NOPE]
    out['l0_w_ukv'] = jnp.transpose(d, (0, 2, 1, 3)).reshape(KV_RANK, N_HEADS * 2 * QK_NOPE)
    d = g['w_out']
    da = d[CONV_WIDTH:].reshape(N_HEADS, HEAD_PAD, D_MODEL)[:, :V_DIM].reshape(N_HEADS * V_DIM, D_MODEL)
    out['l0_w_out'] = jnp.concatenate([d[:CONV_WIDTH], da], axis=0)
    return out


def _pad_rows(w, rows):
    return jnp.pad(w, [(0, 0)] * (w.ndim - 2) + [(0, rows - w.shape[-2]), (0, 0)])


def _ffn_fwd(x, rep, got, pre, tag):
    s = x.shape[0]
    xn = _rms_fwd(x, rep[pre + 'ffn_norm'], f"{tag}_ffn_norm")
    w_up = got[pre + 'w_up']
    hu = _mm(xn, w_up, gb=_same, go=_same, groups=N_DEV, name=f"{tag}_ffn_up").reshape(N_DEV * s, FF_SHARD)
    taps = _pad_rows(got[pre + 'ffn_conv_w'], TAP_ROWS).reshape(N_DEV * TAP_ROWS, FF_SHARD)
    bias = _pad_rows(rep[pre + 'ffn_conv_b'].reshape(N_DEV, 1, FF_SHARD), TAP_ROWS).reshape(N_DEV * TAP_ROWS, FF_SHARD)
    (act,) = _conv_fwd([(hu, 0), (hu, FF_HALF)], taps, [0, FF_HALF], [bias, bias], [0, FF_HALF], _pair_pre,
                       _gate_post, [BF16], name=f"{tag}_ffn_gate", **_gate(s))
    act = act.reshape(FF_HALF, s, FF_SHARD)
    w_down = got[pre + 'w_down'].reshape(FF_HALF, FF_SHARD, D_MODEL)
    y = _mm(act, w_down, ga=_same, gb=_same, groups=FF_HALF, res=x, name=f"{tag}_ffn_down")
    return y, (x, xn, hu, act, taps, bias, w_up, w_down)


def _ffn_bwd(dy, saved, rep, pre, tag, grads, gsh):
    x, xn, hu, act, taps, bias, w_up, w_down = saved
    s = x.shape[0]
    dact = _mm(dy, w_down, tb=True, gb=_same, go=_same, groups=FF_HALF, name=f"{tag}_ffn_down_dx")
    gsh[pre + 'w_down'] = _mm(act, dy, ta=True, ga=_same, go=_same, groups=FF_HALF, out_dtype=BF16,
                              name=f"{tag}_ffn_down_dw").reshape(N_DEV, FF_SHARD // 2, D_MODEL)
    dha, dhb, dwa, dwb, dba, dbb = _conv_bwd(
        [(hu, 0), (hu, FF_HALF)], taps, [0, FF_HALF], [bias, bias], [0, FF_HALF], _pair_pre, _gate_post,
        [(dact.reshape(FF_HALF * s, FF_SHARD), 0)], BF16, name=f"{tag}_ffn_gate_bwd", **_gate(s))
    dha, dhb = dha.reshape(FF_HALF, s, FF_SHARD), dhb.reshape(FF_HALF, s, FF_SHARD)
    dtaps = jnp.concatenate([dwa, dwb], axis=0).reshape(N_DEV, TAP_ROWS, FF_SHARD)
    gsh[pre + 'ffn_conv_w'] = dtaps[:, :FFN_K].astype(BF16)
    grads[pre + 'ffn_conv_b'] = jnp.concatenate([dba, dbb], axis=0).reshape(N_DEV, TAP_ROWS, FF_SHARD)[:, 0].reshape(-1)
    upper = lambda g: g + FF_HALF
    dxn = _mm(dha, w_up, tb=True, ga=_same, gb=_same, groups=FF_HALF, name=f"{tag}_ffn_up_dx_a")
    dxn = _mm(dhb, w_up, tb=True, ga=_same, gb=upper, groups=FF_HALF, res=dxn, name=f"{tag}_ffn_up_dx_b")
    dwu = [_mm(xn, dh, ta=True, gb=_same, go=_same, groups=FF_HALF, out_dtype=BF16, name=f"{tag}_ffn_up_dw_{t}")
           for t, dh in (("a", dha), ("b", dhb))]
    gsh[pre + 'w_up'] = jnp.concatenate(dwu, axis=0)
    dx, dg = _rms_bwd(x, rep[pre + 'ffn_norm'], dxn, dy, f"{tag}_ffn_norm_bwd")
    grads[pre + 'ffn_norm'] = dg.reshape(-1)
    return dx


def _mla_fwd(x, rep, taps, q, tabs):
    s = x.shape[0]
    xn = _rms_fwd(x, rep['l0_mix_norm'], "l0_mix_norm")
    hmat = _mm(xn, q['w_in'], name="l0_in")
    wt = _pad_rows(taps, 4 * TAP_ROWS)
    cpar = [rep['l0_conv_b'].reshape(1, -1), rep['l0_conv_ln_g'].reshape(1, -1), rep['l0_conv_ln_b'].reshape(1, -1)]
    (u,) = _conv_fwd([(hmat, 0), (hmat, 1)], wt, [0], cpar, [0, 0, 0], _glu_pre, _ln_silu_post, [BF16],
                     name="l0_conv", **_conva(s))
    qn, kvn = rep['l0_q_norm'].reshape(1, -1), rep['l0_kv_norm'].reshape(1, -1)
    cqn, ckvn = _rowwise(lambda a, b, ga, gb: (_rms(a, ga), _rms(b, gb)),
                         [(hmat, Q_RANK, 2 * CONV_WIDTH // Q_RANK), (hmat, KV_RANK, (2 * CONV_WIDTH + Q_RANK) // KV_RANK)],
                         [qn, kvn], [(Q_RANK, BF16), (KV_RANK, BF16)], [], ts=512, name="l0_latent_norm")
    qf = _mm(cqn, q['w_uq'], name="l0_uq")
    kvf = _mm(ckvn, q['w_ukv'], name="l0_ukv")
    q_rot, k_full, v = _rope_fwd(qf, kvf, hmat, tabs, "l0_rope")
    o, lse = _attn_fwd(q_rot, k_full, v, "l0_attn")
    cat = jnp.concatenate([u, o], axis=1)
    y = _mm(cat, q['w_out'], res=x, name="l0_out")
    return y, (x, xn, hmat, wt, cpar, qn, kvn, cqn, ckvn, q_rot, k_full, v, lse, cat)


def _mla_bwd(dy, saved, rep, q, tabs, grads, gq):
    x, xn, hmat, wt, cpar, qn, kvn, cqn, ckvn, q_rot, k_full, v, lse, cat = saved
    s = x.shape[0]
    dcat = _mm(dy, q['w_out'], tb=True, name="l0_out_dx")
    gq['w_out'] = _mm(cat, dy, ta=True, name="l0_out_dw")
    dq, lse_r, dlt_r = _attn_bwd_q(q_rot, k_full, v, cat, dcat, lse, "l0_attn_bwd_q")
    dk, dv = _attn_bwd_kv(q_rot, k_full, v, dcat, lse_r, dlt_r, "l0_attn_bwd_kv")
    dqf, dkvf, dkr = _rope_bwd(dq, dk, dv, tabs, "l0_rope_bwd")
    dcqn = _mm(dqf, q['w_uq'], tb=True, name="l0_uq_dx")
    gq['w_uq'] = _mm(cqn, dqf, ta=True, name="l0_uq_dw")
    dckvn = _mm(dkvf, q['w_ukv'], tb=True, name="l0_ukv_dx")
    gq['w_ukv'] = _mm(ckvn, dkvf, ta=True, name="l0_ukv_dw")

    def lat_bwd(a, b, da, db, ga, gb):
        _, vjp = jax.vjp(lambda a_, b_, ga_, gb_: (_rms(a_, ga_), _rms(b_, gb_)), a, b, ga, gb)
        return vjp((da, db))

    dcq, dckv, dqn, dkvn = _rowwise(
        lat_bwd, [(hmat, Q_RANK, 2 * CONV_WIDTH // Q_RANK), (hmat, KV_RANK, (2 * CONV_WIDTH + Q_RANK) // KV_RANK),
                  (dcqn, Q_RANK, 0), (dckvn, KV_RANK, 0)],
        [qn, kvn], [(Q_RANK, F32), (KV_RANK, F32)], [(1, Q_RANK), (1, KV_RANK)], ts=512, name="l0_latent_norm_bwd")
    grads['l0_q_norm'], grads['l0_kv_norm'] = dqn.reshape(-1), dkvn.reshape(-1)
    da, dg, dwt, dcb, dlg, dlb = _conv_bwd(
        [(hmat, 0), (hmat, 1)], wt, [0], cpar, [0, 0, 0], _glu_pre, _ln_silu_post, [(dcat, 0)], F32,
        name="l0_conv_bwd", **_conva(s))
    gq['conv_w'] = dwt[:CONV_K]
    grads['l0_conv_b'], grads['l0_conv_ln_g'], grads['l0_conv_ln_b'] = dcb.reshape(-1), dlg.reshape(-1), dlb.reshape(-1)
    dh = jnp.concatenate([da, dg, dcq, dckv, dkr], axis=1)
    dxn = _mm(dh, q['w_in'], tb=True, name="l0_in_dx")
    gq['w_in'] = _mm(xn, dh, ta=True, name="l0_in_dw")
    dx, dgn = _rms_bwd(x, rep['l0_mix_norm'], dxn, dy, "l0_mix_norm_bwd")
    grads['l0_mix_norm'] = dgn.reshape(-1)
    return dx


def _gelu_skip(yc, u, d):
    return jax.nn.gelu(yc + d * u)


def _glu_out(z1, z2, b1, b2, x):
    return x + (z1 + b1) * jax.nn.sigmoid(z2 + b2)


def _s5_fwd(x, rep, w_in, w_glu):
    xn = _rms_fwd(x, rep['l1_mix_norm'], "l1_mix_norm")
    u = _mm(xn, w_in, name="l1_in")
    lb_re, lb_im, bb_re, bb_im = _s5_disc(rep['l1_log_dt'], rep['l1_a_re'], rep['l1_a_im'], rep['l1_b_re'],
                                          rep['l1_b_im'])
    lam = jnp.concatenate([lb_re.reshape(1, -1), lb_im.reshape(1, -1)], axis=1)
    tr = lambda a: jnp.transpose(a, (0, 2, 1))
    bmat = jnp.concatenate([_bd(tr(bb_re)), _bd(tr(bb_im))], axis=1)
    cmat = jnp.concatenate([_bd(tr(rep['l1_c_re'])), -_bd(tr(rep['l1_c_im']))], axis=0)
    bu = _mm(u, bmat, name="l1_bu")
    (xs,) = _scan(bu, lam, reverse=False, name="l1_scan")
    yc = _mm(xs, cmat, name="l1_cx")
    dsk = rep['l1_d'].reshape(1, -1)
    (y,) = _rowwise(_gelu_skip, [(yc, SSM_WIDTH, 0), (u, SSM_WIDTH, 0)], [dsk], [(SSM_WIDTH, BF16)], [],
                    ts=512, name="l1_gelu")
    z = _mm(y, w_glu, name="l1_glu")
    bg = rep['l1_b_glu'].reshape(1, -1)
    (out,) = _rowwise(lambda z1, z2, xv, b1, b2: _glu_out(z1, z2, b1, b2, xv),
                      [(z, D_MODEL, 0), (z, D_MODEL, 1), (x, D_MODEL, 0)], [bg[:, :D_MODEL], bg[:, D_MODEL:]],
                      [(D_MODEL, F32)], [], ts=512, name="l1_glu_out")
    return out, (x, xn, u, lam, bmat, cmat, xs, yc, dsk, y, z, bg, w_in, w_glu)


def _s5_bwd(dy, saved, rep, grads, gq):
    x, xn, u, lam, bmat, cmat, xs, yc, dsk, y, z, bg, w_in, w_glu = saved

    def glu_bwd(z1, z2, dv, b1, b2):
        _, vjp = jax.vjp(lambda a, b, c, d: (a + c) * jax.nn.sigmoid(b + d), z1, z2, b1, b2)
        d1, d2, db1, db2 = vjp(dv)
        return jnp.concatenate([d1, d2], axis=1), db1, db2

    dz, db1, db2 = _rowwise(glu_bwd, [(z, D_MODEL, 0), (z, D_MODEL, 1), (dy, D_MODEL, 0)],
                            [bg[:, :D_MODEL], bg[:, D_MODEL:]], [(2 * D_MODEL, BF16)], [(1, D_MODEL), (1, D_MODEL)],
                            ts=512, name="l1_glu_out_bwd")
    grads['l1_b_glu'] = jnp.concatenate([db1, db2], axis=1).reshape(-1)
    dyv = _mm(dz, w_glu, tb=True, name="l1_glu_dx")
    gq['l1_w_glu'] = _mm(y, dz, ta=True, name="l1_glu_dw")

    def gelu_bwd(ycv, uv, dv, dk):
        _, vjp = jax.vjp(_gelu_skip, ycv, uv, dk)
        return vjp(dv)

    dyc, du_skip, dd = _rowwise(gelu_bwd, [(yc, SSM_WIDTH, 0), (u, SSM_WIDTH, 0), (dyv, SSM_WIDTH, 0)], [dsk],
                                [(SSM_WIDTH, F32), (SSM_WIDTH, F32)], [(1, SSM_WIDTH)], ts=512, name="l1_gelu_bwd")
    grads['l1_d'] = dd.reshape(-1)
    dxs = _mm(dyc, cmat, tb=True, name="l1_cx_dx")
    dcmat = _mm(xs, dyc, ta=True, name="l1_cx_dw")
    gs, dlam = _scan(dxs, lam, reverse=True, xs=xs, name="l1_scan_bwd")
    dlr, dli = dlam[:, :SSM_CH], dlam[:, SSM_CH:]
    du = _mm(gs, bmat, tb=True, res=du_skip, name="l1_bu_dx")
    dbmat = _mm(u, gs, ta=True, name="l1_bu_dw")
    tr = lambda a: jnp.transpose(a, (0, 2, 1))
    grads['l1_c_re'] = tr(_unbd(dcmat[:SSM_CH], SSM_STATE, SSM_GROUP))
    grads['l1_c_im'] = -tr(_unbd(dcmat[SSM_CH:], SSM_STATE, SSM_GROUP))
    dbb_re = tr(_unbd(dbmat[:, :SSM_CH], SSM_GROUP, SSM_STATE))
    dbb_im = tr(_unbd(dbmat[:, SSM_CH:], SSM_GROUP, SSM_STATE))
    names = ['l1_log_dt', 'l1_a_re', 'l1_a_im', 'l1_b_re', 'l1_b_im']
    _, vjp = jax.vjp(_s5_disc, *[rep[n] for n in names])
    for n, gval in zip(names, vjp((dlr.reshape(SSM_GROUPS, SSM_STATE), dli.reshape(SSM_GROUPS, SSM_STATE), dbb_re, dbb_im))):
        grads[n] = gval
    dxn = _mm(du, w_in, tb=True, name="l1_in_dx")
    gq['l1_w_in'] = _mm(xn, du, ta=True, name="l1_in_dw")
    dx, dgn = _rms_bwd(x, rep['l1_mix_norm'], dxn, dy, "l1_mix_norm_bwd")
    grads['l1_mix_norm'] = dgn.reshape(-1)
    return dx


def _loss_head(x, g, target):
    d = x.shape[1]

    def fn(xv, tv, gv):
        y, vjp = jax.vjp(_rms, xv, gv)
        err = y - tv
        part = 0.5 * jnp.sum(jnp.mean(jnp.square(err), axis=-1, keepdims=True), axis=0, keepdims=True)
        dx, dg = vjp(err * (1.0 / d))
        return dx, jnp.broadcast_to(part, (1, 128)), dg

    return _rowwise(fn, [(x, d, 0), (target, d, 0)], [g.reshape(1, -1)], [(d, F32)], [(1, 128), (1, d)], ts=512,
                    name="loss_head")


def _local_step(x, target, rep, got):
    even = {n: _full(n, got[n]) for n in ('l0_w_in', 'l0_w_uq', 'l0_w_ukv', 'l0_w_out')}
    q = _prep_weights(even)
    tabs = _rope_tables(x.shape[0])
    x1, s_mla = _mla_fwd(x, rep, _full('l0_conv_w', got['l0_conv_w']), q, tabs)
    x2, s_f0 = _ffn_fwd(x1, rep, got, 'l0_', "l0")
    x3, s_s5 = _s5_fwd(x2, rep, _full('l1_w_in', got['l1_w_in']), _full('l1_w_glu', got['l1_w_glu']))
    x4, s_f1 = _ffn_fwd(x3, rep, got, 'l1_', "l1")
    dx4, loss, dgf = _loss_head(x4, rep['final_norm'], target)
    grads, gq, gsh = {'final_norm': dgf.reshape(-1)}, {}, {}
    dx3 = _ffn_bwd(dx4, s_f1, rep, 'l1_', "l1", grads, gsh)
    dx2 = _s5_bwd(dx3, s_s5, rep, grads, gq)
    dx1 = _ffn_bwd(dx2, s_f0, rep, 'l0_', "l0", grads, gsh)
    dx0 = _mla_bwd(dx1, s_mla, rep, q, tabs, grads, gq)
    full = _unprep_grads(gq)
    full.update({'l0_conv_w': gq['conv_w'], 'l1_w_in': gq['l1_w_in'], 'l1_w_glu': gq['l1_w_glu']})
    for n, g in full.items():
        gsh[n] = _shards(n, g).astype(BF16)
    return loss[0, 0], dx0, grads, gsh


def kernel(x, l0_mix_norm, l0_w_in, l0_conv_w, l0_conv_b, l0_conv_ln_g, l0_conv_ln_b, l0_q_norm, l0_kv_norm, l0_w_uq, l0_w_ukv, l0_w_out, l0_ffn_norm, l0_w_up, l0_ffn_conv_w, l0_ffn_conv_b, l0_w_down, l1_mix_norm, l1_w_in, l1_log_dt, l1_a_re, l1_a_im, l1_b_re, l1_b_im, l1_c_re, l1_c_im, l1_d, l1_w_glu, l1_b_glu, l1_ffn_norm, l1_w_up, l1_ffn_conv_w, l1_ffn_conv_b, l1_w_down, final_norm, loss_target, m_l0_mix_norm, m_l0_w_in, m_l0_conv_w, m_l0_conv_b, m_l0_conv_ln_g, m_l0_conv_ln_b, m_l0_q_norm, m_l0_kv_norm, m_l0_w_uq, m_l0_w_ukv, m_l0_w_out, m_l0_ffn_norm, m_l0_w_up, m_l0_ffn_conv_w, m_l0_ffn_conv_b, m_l0_w_down, m_l1_mix_norm, m_l1_w_in, m_l1_log_dt, m_l1_a_re, m_l1_a_im, m_l1_b_re, m_l1_b_im, m_l1_c_re, m_l1_c_im, m_l1_d, m_l1_w_glu, m_l1_b_glu, m_l1_ffn_norm, m_l1_w_up, m_l1_ffn_conv_w, m_l1_ffn_conv_b, m_l1_w_down, m_final_norm, v_l0_mix_norm, v_l0_w_in, v_l0_conv_w, v_l0_conv_b, v_l0_conv_ln_g, v_l0_conv_ln_b, v_l0_q_norm, v_l0_kv_norm, v_l0_w_uq, v_l0_w_ukv, v_l0_w_out, v_l0_ffn_norm, v_l0_w_up, v_l0_ffn_conv_w, v_l0_ffn_conv_b, v_l0_w_down, v_l1_mix_norm, v_l1_w_in, v_l1_log_dt, v_l1_a_re, v_l1_a_im, v_l1_b_re, v_l1_b_im, v_l1_c_re, v_l1_c_im, v_l1_d, v_l1_w_glu, v_l1_b_glu, v_l1_ffn_norm, v_l1_w_up, v_l1_ffn_conv_w, v_l1_ffn_conv_b, v_l1_w_down, v_final_norm):
    args = dict(locals())
    w = {n: args[n] for n in WEIGHTS}
    m = {n: args['m_' + n] for n in WEIGHTS}
    v = {n: args['v_' + n] for n in WEIGHTS}
    sh_names = list(SHARDED)

    got = _exchange([w[n] if n in TAPS else w[n].astype(BF16) for n in sh_names], ['gather'] * len(sh_names),
                    "gather_weights")
    got = dict(zip(sh_names, got))
    rep = {n: w[n] for n in REPLICATED}

    loss, dx, grads, gsh = _local_step(x[0], loss_target[0], rep, got)

    recv = _exchange([gsh[n] for n in sh_names] + [_flat([grads[n] for n in REPLICATED])],
                     ['scatter'] * len(sh_names) + ['gather'], "exchange_grads")
    res = [dict(), dict(), dict(), dict()]
    for n, parts in zip(sh_names, recv):
        for kind, a in enumerate(_adamw(parts, w[n], m[n], v[n], "adamw_" + n)):
            res[kind][n] = a
    flatr = lambda d: _flat([d[n] for n in REPLICATED])
    rp_out = _adamw(recv[-1], flatr(w), flatr(m), flatr(v), "adamw_replicated")
    for kind in range(4):
        for n, a in zip(REPLICATED, _unflat(rp_out[kind], [w[n].shape for n in REPLICATED])):
            res[kind][n] = a
    total = lax.psum(loss, ("x", "y", "c"))
    return (total, dx[None], *[res[0][n] for n in WEIGHTS], *[res[1][n] for n in WEIGHTS],
            *[res[2][n] for n in WEIGHTS], *[res[3][n] for n in WEIGHTS])
```

```python
import functools

import jax
import jax.numpy as jnp
from jax import lax
from jax.experimental import pallas as pl
from jax.experimental.pallas import tpu as pltpu

F32 = jnp.float32
BF16 = jnp.bfloat16

N_DEV = 8
D_MODEL = 1024
EPS = 1e-6
LN_EPS = 1e-5
CONV_WIDTH = 512
CONV_K = 31
N_HEADS = 8
QK_NOPE = 64
QK_ROPE = 32
V_DIM = 64
HEAD_PAD = 128
Q_RANK = 256
KV_RANK = 128
ROPE_BASE = 10000.0
IN_EVEN = 2 * CONV_WIDTH + Q_RANK + KV_RANK + QK_ROPE
IN_PAD = 1536
KR_LANE = 64
SSM_WIDTH = 512
SSM_GROUP = 16
SSM_GROUPS = 32
SSM_STATE = 64
SSM_CH = SSM_GROUPS * SSM_STATE
D_FF = 2816
FF_SHARD = 2 * D_FF // N_DEV
FF_HALF = N_DEV // 2
FFN_K = 3
TAP_ROWS = 8
ADAM_LR, ADAM_B1, ADAM_B2, ADAM_EPS, ADAM_WD, ADAM_STEP = 0.001, 0.9, 0.999, 1e-08, 0.01, 10

WEIGHTS = ['l0_mix_norm', 'l0_w_in', 'l0_conv_w', 'l0_conv_b', 'l0_conv_ln_g', 'l0_conv_ln_b', 'l0_q_norm',
           'l0_kv_norm', 'l0_w_uq', 'l0_w_ukv', 'l0_w_out', 'l0_ffn_norm', 'l0_w_up', 'l0_ffn_conv_w',
           'l0_ffn_conv_b', 'l0_w_down', 'l1_mix_norm', 'l1_w_in', 'l1_log_dt', 'l1_a_re', 'l1_a_im', 'l1_b_re',
           'l1_b_im', 'l1_c_re', 'l1_c_im', 'l1_d', 'l1_w_glu', 'l1_b_glu', 'l1_ffn_norm', 'l1_w_up',
           'l1_ffn_conv_w', 'l1_ffn_conv_b', 'l1_w_down', 'final_norm']
SHARDED = {'l0_w_in': 1, 'l0_conv_w': 1, 'l0_w_uq': 1, 'l0_w_ukv': 1, 'l0_w_out': 0, 'l0_w_up': 1,
           'l0_ffn_conv_w': 1, 'l0_w_down': 0, 'l1_w_in': 0, 'l1_w_glu': 1, 'l1_w_up': 1, 'l1_ffn_conv_w': 1,
           'l1_w_down': 0}
TAPS = ('l0_conv_w', 'l0_ffn_conv_w', 'l1_ffn_conv_w')
REPLICATED = [n for n in WEIGHTS if n not in SHARDED]


def _tile(n, cands):
    for c in cands:
        if n % c == 0:
            return c
    return n


def _same(g):
    return g


def _mm(a, b, *, ta=False, tb=False, res=None, out_dtype=F32, name, ga=None, gb=None, go=None, groups=1):
    a2, b2 = (a.shape[1:] if ga else a.shape), (b.shape[1:] if gb else b.shape)
    m, kd = (a2[1], a2[0]) if ta else a2
    kd2, n = (b2[1], b2[0]) if tb else b2
    assert kd == kd2, (a.shape, b.shape, ta, tb)
    tm = _tile(m, (1024, 512, 256, 128))
    tn = _tile(n, (512, 384, 256, 128))
    tk = _tile(kd, (1024, 512, 256, 128))
    nk = kd // tk
    summed = go is None and (ga is not None or gb is not None)
    nkk = nk * (groups if summed else 1)
    dn = (((0 if ta else 1,), (1 if tb else 0,)), ((), ()))

    def body(*refs):
        if res is None:
            a_ref, b_ref, o_ref, acc_ref = refs
            r_ref = None
        else:
            a_ref, b_ref, r_ref, o_ref, acc_ref = refs
        k = pl.program_id(3)
        p = lax.dot_general(a_ref[...].astype(BF16), b_ref[...].astype(BF16), dn, preferred_element_type=F32)

        @pl.when(k == 0)
        def _():
            acc_ref[...] = p

        @pl.when(k > 0)
        def _():
            acc_ref[...] += p

        @pl.when(k == nkk - 1)
        def _():
            out = acc_ref[...]
            if r_ref is not None:
                out = out + r_ref[...]
            o_ref[...] = out.astype(out_dtype)

    def spec(shape2, idx2, gmap):
        if gmap is None:
            return pl.BlockSpec(shape2, lambda g, i, j, kk: idx2(i, j, kk % nk))
        if summed:
            return pl.BlockSpec((None,) + shape2, lambda g, i, j, kk: (gmap(kk // nk),) + idx2(i, j, kk % nk))
        return pl.BlockSpec((None,) + shape2, lambda g, i, j, kk: (gmap(g),) + idx2(i, j, kk))

    a_spec = spec((tk, tm), lambda i, j, k: (k, i), ga) if ta else spec((tm, tk), lambda i, j, k: (i, k), ga)
    b_spec = spec((tn, tk), lambda i, j, k: (j, k), gb) if tb else spec((tk, tn), lambda i, j, k: (k, j), gb)
    o_spec = spec((tm, tn), lambda i, j, k: (i, j), go)
    in_specs, args = [a_spec, b_spec], [a, b]
    if res is not None:
        in_specs.append(o_spec)
        args.append(res)
    out_shape = (groups, m, n) if go else (m, n)
    return pl.pallas_call(
        body, grid=(groups if go else 1, m // tm, n // tn, nkk), in_specs=in_specs, out_specs=o_spec,
        out_shape=jax.ShapeDtypeStruct(out_shape, out_dtype), scratch_shapes=[pltpu.VMEM((tm, tn), F32)],
        compiler_params=pltpu.CompilerParams(dimension_semantics=("parallel", "parallel", "parallel", "arbitrary")),
        name=name)(*args)


def _rowwise(fn, rows, bcasts, row_outs, red_outs, *, ts, name):
    s = rows[0][0].shape[0]
    nr, nb, nro, nre = len(rows), len(bcasts), len(row_outs), len(red_outs)

    def body(*refs):
        i = pl.program_id(0)
        outs = fn(*[r[...] for r in refs[:nr + nb]])
        if not isinstance(outs, (tuple, list)):
            outs = (outs,)
        o_refs = refs[nr + nb:]
        for q in range(nro):
            o_refs[q][...] = outs[q].astype(o_refs[q].dtype)
        for q in range(nro, nro + nre):
            @pl.when(i == 0)
            def _(q=q):
                o_refs[q][...] = outs[q]

            @pl.when(i > 0)
            def _(q=q):
                o_refs[q][...] += outs[q]

    in_specs = [pl.BlockSpec((ts, w), functools.partial(lambda i, cb: (i, cb), cb=cb)) for (_, w, cb) in rows]
    in_specs += [pl.BlockSpec(b.shape, functools.partial(lambda i, nd: (0,) * nd, nd=b.ndim)) for b in bcasts]
    out_specs = [pl.BlockSpec((ts, w), lambda i: (i, 0)) for (w, _) in row_outs]
    out_specs += [pl.BlockSpec((r, w), lambda i: (0, 0)) for (r, w) in red_outs]
    out_shape = [jax.ShapeDtypeStruct((s, w), dt) for (w, dt) in row_outs]
    out_shape += [jax.ShapeDtypeStruct((r, w), F32) for (r, w) in red_outs]
    return pl.pallas_call(
        body, grid=(s // ts,), in_specs=in_specs, out_specs=out_specs, out_shape=out_shape,
        compiler_params=pltpu.CompilerParams(dimension_semantics=("arbitrary",)), name=name,
    )(*[r[0] for r in rows], *bcasts)


def _rms(x, g):
    return x * lax.rsqrt(jnp.mean(x * x, axis=-1, keepdims=True) + EPS) * g


def _rms_fwd(x, g, name):
    return _rowwise(lambda xv, gv: _rms(xv, gv), [(x, x.shape[1], 0)], [g.reshape(1, -1)],
                    [(x.shape[1], BF16)], [], ts=512, name=name)[0]


def _rms_bwd(x, g, dxn, dres, name):
    d = x.shape[1]

    def fn(xv, dv, rv, gv):
        _, vjp = jax.vjp(_rms, xv, gv)
        dx, dg = vjp(dv.astype(F32))
        return rv + dx, dg

    return _rowwise(fn, [(x, d, 0), (dxn, d, 0), (dres, d, 0)], [g.reshape(1, -1)], [(d, F32)], [(1, d)],
                    ts=512, name=name)


def _cspec(mode, off, rows, width, rowblk, n_rb):
    if mode == 'col':
        return pl.BlockSpec((rows, width), lambda jc, i: (rowblk(i), off + jc))
    return pl.BlockSpec((rows, width), lambda jc, i: ((off + jc) * n_rb + rowblk(i), 0))


def _conv_fwd(xins, w, woffs, params, poffs, pre, post, outs, *, mode, s, k_taps, hb, ts, tc, rc, n_cb, name):
    n_s, nx, ncv, npar, no = s // ts, len(xins), len(woffs), len(params), len(outs)
    rpb = ts // hb
    kp = TAP_ROWS * ((k_taps + TAP_ROWS - 1) // TAP_ROWS)
    pr = 1 if mode == 'col' else TAP_ROWS

    def body(*refs):
        mains, halos = refs[:nx], refs[nx:2 * nx]
        w_refs = refs[2 * nx:2 * nx + ncv]
        p_refs = refs[2 * nx + ncv:2 * nx + ncv + npar]
        o_refs = refs[2 * nx + ncv + npar:2 * nx + ncv + npar + no]
        u_s = refs[2 * nx + ncv + npar + no:2 * nx + ncv + npar + no + ncv]
        win_s = refs[2 * nx + ncv + npar + no + ncv:]
        i = pl.program_id(1)
        um = pre(*[r[...].astype(F32) for r in mains])
        uh = pre(*[r[...].astype(F32) for r in halos])
        first = (i > 0).astype(F32)
        for q in range(ncv):
            u_s[q][pl.ds(0, hb), :] = uh[q] * first
            u_s[q][pl.ds(hb, ts), :] = um[q]
        pv = [r[0:1, :] for r in p_refs]

        def chunk(ci, carry):
            r0 = pl.multiple_of(ci * rc, rc)
            cs = []
            for q in range(ncv):
                win_s[q][...] = u_s[q][pl.ds(r0, rc + hb), :]
                acc = jnp.zeros((rc, tc), F32)
                for t in range(k_taps):
                    acc = acc + w_refs[q][pl.ds(t, 1), :] * win_s[q][pl.ds(hb - (k_taps - 1) + t, rc), :]
                cs.append(acc)
            res = post(cs, pv)
            for q in range(no):
                o_refs[q][pl.ds(r0, rc), :] = res[q].astype(o_refs[q].dtype)
            return carry

        lax.fori_loop(0, ts // rc, chunk, 0)

    main = lambda i: i
    prev = lambda i: jnp.maximum(i * rpb - 1, 0)
    zero = lambda i: 0
    in_specs = [_cspec(mode, off, ts, tc, main, n_s) for _, off in xins]
    in_specs += [_cspec(mode, off, hb, tc, prev, s // hb) for _, off in xins]
    in_specs += [_cspec(mode, off, kp, tc, zero, 1) for off in woffs]
    in_specs += [_cspec(mode, off, pr, tc, zero, 1) for off in poffs]
    out_specs = [_cspec(mode, 0, ts, tc, main, n_s) for _ in outs]
    oshape = (s, n_cb * tc) if mode == 'col' else (n_cb * s, tc)
    out_shape = [jax.ShapeDtypeStruct(oshape, dt) for dt in outs]
    return pl.pallas_call(
        body, grid=(n_cb, n_s), in_specs=in_specs, out_specs=out_specs, out_shape=out_shape,
        scratch_shapes=[pltpu.VMEM((hb + ts, tc), F32) for _ in range(ncv)]
        + [pltpu.VMEM((hb + rc, tc), F32) for _ in range(ncv)],
        compiler_params=pltpu.CompilerParams(dimension_semantics=("parallel", "arbitrary")), name=name,
    )(*[a for a, _ in xins], *[a for a, _ in xins], *([w] * ncv), *params)


def _conv_bwd(xins, w, woffs, params, poffs, pre, post, douts, dx_dtype, *, mode, s, k_taps, hb, ts, tc, rc, n_cb, name):
    n_s, nx, ncv, npar, ndo = s // ts, len(xins), len(woffs), len(params), len(douts)
    rpb = ts // hb
    n_hb = s // hb
    kp = TAP_ROWS * ((k_taps + TAP_ROWS - 1) // TAP_ROWS)
    pr = 1 if mode == 'col' else TAP_ROWS

    def body(*refs):
        pos = 0

        def take(n):
            nonlocal pos
            out = refs[pos:pos + n]
            pos += n
            return out

        mains, prevs, nexts = take(nx), take(nx), take(nx)
        d_mains, d_nexts = take(ndo), take(ndo)
        w_refs, p_refs = take(ncv), take(npar)
        dx_refs, dw_refs, dp_refs = take(nx), take(ncv), take(npar)
        u_s, d_s = take(ncv), take(ndo)
        win_s, dcw_s, dwa_s = take(ncv), take(ncv), take(ncv)
        i = pl.program_id(1)
        um = pre(*[r[...].astype(F32) for r in mains])
        up = pre(*[r[...].astype(F32) for r in prevs])
        un = pre(*[r[...].astype(F32) for r in nexts])
        first = (i > 0).astype(F32)
        last = (i < n_s - 1).astype(F32)
        for q in range(ncv):
            u_s[q][pl.ds(0, hb), :] = up[q] * first
            u_s[q][pl.ds(hb, ts), :] = um[q]
            u_s[q][pl.ds(hb + ts, hb), :] = un[q]
            dwa_s[q][...] = jnp.zeros((kp * 8, tc), F32)
        for q in range(ndo):
            d_s[q][pl.ds(0, ts), :] = d_mains[q][...].astype(F32)
            d_s[q][pl.ds(ts, hb), :] = d_nexts[q][...].astype(F32) * last
        pv = [r[0:1, :] for r in p_refs]
        few_taps = k_taps <= 4

        def chunk(ci, dpar):
            r0 = pl.multiple_of(ci * rc, rc)
            c_own, c_next, shifted = [], [], []
            for q in range(ncv):
                win_s[q][...] = u_s[q][pl.ds(r0, rc + 2 * hb), :]
                acc = jnp.zeros((rc + hb, tc), F32)
                taps = []
                for t in range(k_taps):
                    ut = win_s[q][pl.ds(hb - (k_taps - 1) + t, rc + hb), :]
                    acc = acc + w_refs[q][pl.ds(t, 1), :] * ut
                    taps.append(ut[:rc] if few_taps else None)
                c_own.append(acc[:rc])
                c_next.append(acc[rc:])
                shifted.append(taps)
            _, vjp_o = jax.vjp(lambda c, p: tuple(post(c, p)), c_own, pv)
            dc_own, dp_own = vjp_o(tuple(r[pl.ds(r0, rc), :] for r in d_s))
            _, vjp_n = jax.vjp(lambda c: tuple(post(c, pv)), c_next)
            (dc_next,) = vjp_n(tuple(r[pl.ds(r0 + rc, hb), :] for r in d_s))
            dus = []
            for q in range(ncv):
                dcw_s[q][pl.ds(0, rc), :] = dc_own[q]
                dcw_s[q][pl.ds(rc, hb), :] = dc_next[q]
                acc = jnp.zeros((rc, tc), F32)
                for t in range(k_taps):
                    acc = acc + w_refs[q][pl.ds(t, 1), :] * dcw_s[q][pl.ds(k_taps - 1 - t, rc), :]
                    ut = shifted[q][t] if few_taps else win_s[q][pl.ds(hb - (k_taps - 1) + t, rc), :]
                    dwa_s[q][pl.ds(8 * t, 8), :] += jnp.sum((dc_own[q] * ut).reshape(rc // 8, 8, tc), axis=0)
                dus.append(acc)
            xm = [r[pl.ds(r0, rc), :].astype(F32) for r in mains]
            _, vjp_p = jax.vjp(lambda *xv: tuple(pre(*xv)), *xm)
            dxs = vjp_p(tuple(dus))
            for q in range(nx):
                dx_refs[q][pl.ds(r0, rc), :] = dxs[q].astype(dx_refs[q].dtype)
            return tuple(a + b for a, b in zip(dpar, dp_own))

        dpar = lax.fori_loop(0, ts // rc, chunk, tuple(jnp.zeros((1, tc), F32) for _ in range(npar)))
        for q in range(ncv):
            @pl.when(i == 0)
            def _(q=q):
                dw_refs[q][...] = jnp.zeros((kp, tc), F32)

            for t in range(k_taps):
                dw_refs[q][pl.ds(t, 1), :] += jnp.sum(dwa_s[q][pl.ds(8 * t, 8), :], axis=0, keepdims=True)
        for q in range(npar):
            @pl.when(i == 0)
            def _(q=q):
                dp_refs[q][...] = jnp.zeros((pr, tc), F32)

            dp_refs[q][0:1, :] += dpar[q]

    main = lambda i: i
    prev = lambda i: jnp.maximum(i * rpb - 1, 0)
    nxt = lambda i: jnp.minimum((i + 1) * rpb, n_hb - 1)
    zero = lambda i: 0
    in_specs = [_cspec(mode, off, ts, tc, main, n_s) for _, off in xins]
    in_specs += [_cspec(mode, off, hb, tc, prev, n_hb) for _, off in xins]
    in_specs += [_cspec(mode, off, hb, tc, nxt, n_hb) for _, off in xins]
    in_specs += [_cspec(mode, off, ts, tc, main, n_s) for _, off in douts]
    in_specs += [_cspec(mode, off, hb, tc, nxt, n_hb) for _, off in douts]
    in_specs += [_cspec(mode, off, kp, tc, zero, 1) for off in woffs]
    in_specs += [_cspec(mode, off, pr, tc, zero, 1) for off in poffs]
    out_specs = [_cspec(mode, 0, ts, tc, main, n_s) for _ in xins]
    out_specs += [_cspec(mode, 0, kp, tc, zero, 1) for _ in woffs]
    out_specs += [_cspec(mode, 0, pr, tc, zero, 1) for _ in params]

    def shape(rows):
        return (rows, n_cb * tc) if mode == 'col' else (n_cb * rows, tc)

    out_shape = [jax.ShapeDtypeStruct(shape(s), dx_dtype) for _ in xins]
    out_shape += [jax.ShapeDtypeStruct(shape(kp), F32) for _ in woffs]
    out_shape += [jax.ShapeDtypeStruct(shape(pr), F32) for _ in params]
    xa = [a for a, _ in xins]
    da = [a for a, _ in douts]
    return pl.pallas_call(
        body, grid=(n_cb, n_s), in_specs=in_specs, out_specs=out_specs, out_shape=out_shape,
        scratch_shapes=[pltpu.VMEM((hb + ts + hb, tc), F32) for _ in range(ncv)]
        + [pltpu.VMEM((ts + hb, tc), F32) for _ in range(ndo)]
        + [pltpu.VMEM((rc + 2 * hb, tc), F32) for _ in range(ncv)]
        + [pltpu.VMEM((rc + hb, tc), F32) for _ in range(ncv)]
        + [pltpu.VMEM((kp * 8, tc), F32) for _ in range(ncv)],
        compiler_params=pltpu.CompilerParams(dimension_semantics=("parallel", "arbitrary")), name=name,
    )(*xa, *xa, *xa, *da, *da, *([w] * ncv), *params)


def _glu_pre(a, g):
    return [a * jax.nn.sigmoid(g)]


def _ln_silu_post(cs, ps):
    c = cs[0] + ps[0]
    mu = jnp.mean(c, axis=-1, keepdims=True)
    var = jnp.mean(jnp.square(c - mu), axis=-1, keepdims=True)
    y = (c - mu) * lax.rsqrt(var + LN_EPS) * ps[1] + ps[2]
    return [jax.nn.silu(y)]


def _pair_pre(a, b):
    return [a, b]


def _gate_post(cs, ps):
    return [jax.nn.silu(cs[0] + ps[0]) * (cs[1] + ps[1])]


def _conva(s):
    return dict(mode='col', s=s, k_taps=CONV_K, hb=32, ts=512, tc=CONV_WIDTH, rc=64, n_cb=1)


def _gate(s):
    return dict(mode='row', s=s, k_taps=FFN_K, hb=8, ts=512, tc=FF_SHARD, rc=32, n_cb=FF_HALF)


def _rope_tables(s):
    half = QK_ROPE // 2
    inv = ROPE_BASE ** (-jnp.arange(half, dtype=F32) / half)
    ang = jnp.arange(s, dtype=F32)[:, None] * inv[None, :]
    cos, sin = jnp.cos(ang), jnp.sin(ang)
    z = lambda n: jnp.zeros((s, n), F32)
    c = jnp.concatenate([jnp.ones((s, QK_NOPE), F32), cos, cos, z(HEAD_PAD - QK_NOPE - QK_ROPE)], axis=1)
    s1 = jnp.concatenate([z(QK_NOPE), -sin, z(HEAD_PAD - QK_NOPE - half)], axis=1)
    s2 = jnp.concatenate([z(QK_NOPE + half), sin, z(HEAD_PAD - QK_NOPE - QK_ROPE)], axis=1)
    return c, s1, s2


def _rot(t, c, s1, s2):
    half = QK_ROPE // 2
    return t * c + pltpu.roll(t, HEAD_PAD - half, 1) * s1 + pltpu.roll(t, half, 1) * s2


def _rot_t(d, c, s1, s2):
    half = QK_ROPE // 2
    return d * c + pltpu.roll(d * s1, half, 1) + pltpu.roll(d * s2, HEAD_PAD - half, 1)


def _heads(v):
    return [v[:, h * HEAD_PAD:(h + 1) * HEAD_PAD] for h in range(N_HEADS)]


def _rope_fwd(qf, kvf, hmat, tabs, name):
    w = N_HEADS * HEAD_PAD

    def fn(q, k, v, kr, c, s1, s2):
        krr = _rot(kr, c, s1, s2)
        qo = jnp.concatenate([_rot(t, c, s1, s2) for t in _heads(q)], axis=1)
        ko = jnp.concatenate([t + krr for t in _heads(k)], axis=1)
        lane = lax.broadcasted_iota(jnp.int32, v.shape, 1) & (HEAD_PAD - 1)
        return qo, ko, jnp.where(lane == ONES_LANE, 1.0, v)

    rows = [(qf, w, 0), (kvf, w, 0), (kvf, w, 1), (hmat, HEAD_PAD, IN_PAD // HEAD_PAD - 1)]
    rows += [(t, HEAD_PAD, 0) for t in tabs]
    return _rowwise(fn, rows, [], [(w, BF16)] * 3, [], ts=512, name=name)


def _rope_bwd(dq, dk, dv, tabs, name):
    w = N_HEADS * HEAD_PAD

    def fn(dqv, dkv, dvv, c, s1, s2):
        dqo = jnp.concatenate([_rot_t(t, c, s1, s2) for t in _heads(dqv)], axis=1)
        ksum = functools.reduce(lambda a, b: a + b, _heads(dkv))
        return dqo, jnp.concatenate([dkv, dvv], axis=1), _rot_t(ksum, c, s1, s2)

    rows = [(dq, w, 0), (dk, w, 0), (dv, w, 0)] + [(t, HEAD_PAD, 0) for t in tabs]
    return _rowwise(fn, rows, [], [(w, BF16), (2 * w, BF16), (HEAD_PAD, F32)], [], ts=512, name=name)


ATT_Q = 1024
ATT_SUB = 256
ATT_KV = 512
ATT_SCALE = (QK_NOPE + QK_ROPE) ** -0.5
LOG2E = 1.4426950408889634
ATT_C2 = ATT_SCALE * LOG2E
ONES_LANE = V_DIM


def _nt(a, b):
    return lax.dot_general(a, b, (((1,), (1,)), ((), ())), preferred_element_type=F32)


def _lanes(x, w):
    return x if w == HEAD_PAD else jnp.tile(x, (1, w // HEAD_PAD))


def _tri(w, transposed):
    r = lax.broadcasted_iota(jnp.int32, (w, w), 0)
    c = lax.broadcasted_iota(jnp.int32, (w, w), 1)
    return (r <= c) if transposed else (c <= r)


def _attn_fwd(q, k, v, name):
    s = q.shape[0]
    tq = min(ATT_Q, s)
    nsub, per = tq // ATT_SUB, tq // ATT_KV

    def body(q_ref, k_ref, v_ref, o_ref, lse_ref, m_s, acc_s):
        i = pl.program_id(1)
        m_s[...] = jnp.full((tq, HEAD_PAD), -jnp.inf, F32)
        acc_s[...] = jnp.zeros((tq, HEAD_PAD), F32)

        def update(r, kb, vb, diag):
            rows = pl.ds(r * ATT_SUB, ATT_SUB)
            w = kb.shape[0]
            sc = _nt(q_ref[rows, :], kb)
            if diag:
                sc = jnp.where(_tri(w, False), sc, -jnp.inf)
            m_prev = m_s[rows, :]
            m_next = jnp.maximum(m_prev, jnp.max(sc, axis=1, keepdims=True))
            p = jnp.exp2((sc - _lanes(m_next, w)) * ATT_C2)
            alpha = jnp.exp2((m_prev - m_next) * ATT_C2)
            acc_s[rows, :] = alpha * acc_s[rows, :] + jnp.dot(p.astype(BF16), vb, preferred_element_type=F32)
            m_s[rows, :] = m_next

        def below(j, carry):
            at = pl.ds(pl.multiple_of(j * ATT_KV, ATT_KV), ATT_KV)
            kb, vb = k_ref[at, :], v_ref[at, :]
            for r in range(nsub):
                update(r, kb, vb, False)
            return carry

        lax.fori_loop(0, i * per, below, 0)
        for r in range(nsub):
            for c in range(r + 1):
                at = pl.ds(pl.multiple_of(i * tq + c * ATT_SUB, ATT_SUB), ATT_SUB)
                update(r, k_ref[at, :], v_ref[at, :], c == r)
        l = acc_s[:, ONES_LANE:ONES_LANE + 1]
        o_ref[...] = (acc_s[...] / l).astype(BF16)
        lse_ref[...] = m_s[...] * ATT_SCALE + jnp.log(l)

    q_spec = pl.BlockSpec((tq, HEAD_PAD), lambda h, i: (i, h))
    kv_spec = pl.BlockSpec((s, HEAD_PAD), lambda h, i: (0, h))
    return pl.pallas_call(
        body, grid=(N_HEADS, s // tq), in_specs=[q_spec, kv_spec, kv_spec], out_specs=[q_spec, q_spec],
        out_shape=[jax.ShapeDtypeStruct(q.shape, BF16), jax.ShapeDtypeStruct(q.shape, F32)],
        scratch_shapes=[pltpu.VMEM((tq, HEAD_PAD), F32)] * 2,
        compiler_params=pltpu.CompilerParams(dimension_semantics=("parallel", "arbitrary")), name=name,
    )(q, k, v)


def _attn_bwd_q(q, k, v, cat, dcat, lse, name):
    s = q.shape[0]
    tq = min(ATT_Q, s)
    nsub, per = tq // ATT_SUB, tq // ATT_KV
    ob = CONV_WIDTH // HEAD_PAD

    def body(q_ref, k_ref, v_ref, o_ref, do_ref, lse_ref, dq_ref, lser_ref, dltr_ref, dq_s, dl_s, l2_s):
        i = pl.program_id(1)
        dl_s[...] = jnp.broadcast_to(jnp.sum(do_ref[...] * o_ref[...].astype(F32), axis=1, keepdims=True),
                                     (tq, HEAD_PAD))
        l2_s[...] = lse_ref[...] * LOG2E
        dq_s[...] = jnp.zeros((tq, HEAD_PAD), F32)

        def update(r, kb, vb, diag):
            rows = pl.ds(r * ATT_SUB, ATT_SUB)
            w = kb.shape[0]
            sc = _nt(q_ref[rows, :], kb)
            if diag:
                sc = jnp.where(_tri(w, False), sc, -jnp.inf)
            p = jnp.exp2(sc * ATT_C2 - _lanes(l2_s[rows, :], w))
            dp = _nt(do_ref[rows, :].astype(BF16), vb)
            ds = p * (dp - _lanes(dl_s[rows, :], w))
            dq_s[rows, :] += jnp.dot(ds.astype(BF16), kb, preferred_element_type=F32)

        def below(j, carry):
            at = pl.ds(pl.multiple_of(j * ATT_KV, ATT_KV), ATT_KV)
            kb, vb = k_ref[at, :], v_ref[at, :]
            for r in range(nsub):
                update(r, kb, vb, False)
            return carry

        lax.fori_loop(0, i * per, below, 0)
        for r in range(nsub):
            for c in range(r + 1):
                at = pl.ds(pl.multiple_of(i * tq + c * ATT_SUB, ATT_SUB), ATT_SUB)
                update(r, k_ref[at, :], v_ref[at, :], c == r)
        dq_ref[...] = dq_s[...] * ATT_SCALE
        for c in range(per):
            at = pl.ds(c * ATT_KV, ATT_KV)
            lser_ref[c] = jnp.transpose(l2_s[at, :])[0:8, :]
            dltr_ref[c] = jnp.transpose(dl_s[at, :])[0:8, :]

    q_spec = pl.BlockSpec((tq, HEAD_PAD), lambda h, i: (i, h))
    kv_spec = pl.BlockSpec((s, HEAD_PAD), lambda h, i: (0, h))
    o_spec = pl.BlockSpec((tq, HEAD_PAD), lambda h, i: (i, ob + h))
    row_spec = pl.BlockSpec((None, per, 8, ATT_KV), lambda h, i: (h, i, 0, 0))
    rows = jax.ShapeDtypeStruct((N_HEADS, s // ATT_KV, 8, ATT_KV), F32)
    return pl.pallas_call(
        body, grid=(N_HEADS, s // tq), in_specs=[q_spec, kv_spec, kv_spec, o_spec, o_spec, q_spec],
        out_specs=[q_spec, row_spec, row_spec], out_shape=[jax.ShapeDtypeStruct(q.shape, F32), rows, rows],
        scratch_shapes=[pltpu.VMEM((tq, HEAD_PAD), F32)] * 3,
        compiler_params=pltpu.CompilerParams(dimension_semantics=("parallel", "arbitrary")), name=name,
    )(q, k, v, cat, dcat, lse)


def _attn_bwd_kv(q, k, v, dcat, lse_r, dlt_r, name):
    s = q.shape[0]
    tk = min(ATT_Q, s)
    nsub, per, n_chunks = tk // ATT_SUB, tk // ATT_KV, s // ATT_KV
    ob = CONV_WIDTH // HEAD_PAD

    def body(k_ref, v_ref, q_ref, do_ref, lse_ref, dl_ref, dk_ref, dv_ref, dk_s, dv_s):
        j = pl.program_id(1)
        dk_s[...] = jnp.zeros((tk, HEAD_PAD), F32)
        dv_s[...] = jnp.zeros((tk, HEAD_PAD), F32)

        def update(r, qb, dob, lrow, drow, diag):
            rows = pl.ds(r * ATT_SUB, ATT_SUB)
            sc = _nt(k_ref[rows, :], qb)
            if diag:
                sc = jnp.where(_tri(qb.shape[0], True), sc, -jnp.inf)
            p = jnp.exp2(sc * ATT_C2 - lrow)
            dp = _nt(v_ref[rows, :], dob)
            ds = p * (dp - drow)
            dv_s[rows, :] += jnp.dot(p.astype(BF16), dob, preferred_element_type=F32)
            dk_s[rows, :] += jnp.dot(ds.astype(BF16), qb, preferred_element_type=F32)

        def above(ic, carry):
            at = pl.ds(pl.multiple_of(ic * ATT_KV, ATT_KV), ATT_KV)
            qb, dob = q_ref[at, :], do_ref[at, :].astype(BF16)
            lrow, drow = lse_ref[ic, 0:1, :], dl_ref[ic, 0:1, :]
            for r in range(nsub):
                update(r, qb, dob, lrow, drow, False)
            return carry

        lax.fori_loop((j + 1) * per, n_chunks, above, 0)
        for r in range(nsub):
            for c in range(r, nsub):
                at = pl.ds(pl.multiple_of(j * tk + c * ATT_SUB, ATT_SUB), ATT_SUB)
                ic = j * per + (c * ATT_SUB) // ATT_KV
                lo = (c * ATT_SUB) % ATT_KV
                update(r, q_ref[at, :], do_ref[at, :].astype(BF16), lse_ref[ic, 0:1, lo:lo + ATT_SUB],
                       dl_ref[ic, 0:1, lo:lo + ATT_SUB], c == r)
        dk_ref[...] = dk_s[...] * ATT_SCALE
        dv_ref[...] = dv_s[...]

    kv_spec = pl.BlockSpec((tk, HEAD_PAD), lambda h, j: (j, h))
    q_spec = pl.BlockSpec((s, HEAD_PAD), lambda h, j: (0, h))
    do_spec = pl.BlockSpec((s, HEAD_PAD), lambda h, j: (0, ob + h))
    row_spec = pl.BlockSpec((None, n_chunks, 8, ATT_KV), lambda h, j: (h, 0, 0, 0))
    return pl.pallas_call(
        body, grid=(N_HEADS, s // tk), in_specs=[kv_spec, kv_spec, q_spec, do_spec, row_spec, row_spec],
        out_specs=[kv_spec, kv_spec],
        out_shape=[jax.ShapeDtypeStruct(q.shape, F32), jax.ShapeDtypeStruct(q.shape, F32)],
        scratch_shapes=[pltpu.VMEM((tk, HEAD_PAD), F32)] * 2,
        compiler_params=pltpu.CompilerParams(dimension_semantics=("parallel", "arbitrary")), name=name,
    )(k, v, q, dcat, lse_r, dlt_r)


SCAN_T = 256
SCAN_C = 512


def _scan(b, lam, *, reverse, xs=None, name):
    s = b.shape[0]
    t = min(SCAN_T, s)
    n_t, n_c = s // t, SSM_CH // SCAN_C
    with_dlam = xs is not None

    def shift(a, d, row):
        if d >= 8:
            z = jnp.zeros((d, SCAN_C), F32)
            return jnp.concatenate([a[d:], z], axis=0) if reverse else jnp.concatenate([z, a[:t - d]], axis=0)
        if reverse:
            return jnp.where(row < t - d, pltpu.roll(a, t - d, 0), 0.0)
        return jnp.where(row >= d, pltpu.roll(a, d, 0), 0.0)

    def body(*refs):
        if with_dlam:
            b_ref, lam_ref, x_ref, o_ref, dl_ref, c_s = refs
        else:
            b_ref, lam_ref, o_ref, c_s = refs
        k = pl.program_id(0)

        @pl.when(k == 0)
        def _():
            c_s[...] = jnp.zeros((1, 2 * SSM_CH), F32)
            if with_dlam:
                dl_ref[...] = jnp.zeros((1, 2 * SSM_CH), F32)

        row = lax.broadcasted_iota(jnp.int32, (t, SCAN_C), 0)
        edge = (row == t - 1) if reverse else (row == 0)
        for ch in range(n_c):
            re = pl.ds(ch * SCAN_C, SCAN_C)
            im = pl.ds(SSM_CH + ch * SCAN_C, SCAN_C)
            lr = lam_ref[:, re]
            li = -lam_ref[:, im] if reverse else lam_ref[:, im]
            cr, ci = c_s[:, re], c_s[:, im]
            ar = b_ref[:, re] + jnp.where(edge, lr * cr - li * ci, 0.0)
            ai = b_ref[:, im] + jnp.where(edge, lr * ci + li * cr, 0.0)
            d = 1
            while d < t:
                sr, si = shift(ar, d, row), shift(ai, d, row)
                ar, ai = ar + lr * sr - li * si, ai + lr * si + li * sr
                lr, li = lr * lr - li * li, 2.0 * lr * li
                d *= 2
            o_ref[:, re] = ar
            o_ref[:, im] = ai
            if with_dlam:
                gr = jnp.where(edge, cr, shift(ar, 1, row))
                gi = jnp.where(edge, ci, shift(ai, 1, row))
                xr, xi = x_ref[:, re], x_ref[:, im]
                dl_ref[:, re] += jnp.sum(xr * gr + xi * gi, axis=0, keepdims=True)
                dl_ref[:, im] += jnp.sum(xr * gi - xi * gr, axis=0, keepdims=True)
            last = 0 if reverse else t - 1
            c_s[:, re] = ar[last:last + 1, :]
            c_s[:, im] = ai[last:last + 1, :]

    tm = (lambda k: (n_t - 1 - k, 0)) if reverse else (lambda k: (k, 0))
    blk = pl.BlockSpec((t, 2 * SSM_CH), tm)
    vec = pl.BlockSpec((1, 2 * SSM_CH), lambda k: (0, 0))
    in_specs, args = [blk, vec], [b, lam]
    out_specs, out_shape = [blk], [jax.ShapeDtypeStruct((s, 2 * SSM_CH), F32)]
    if with_dlam:
        in_specs.append(blk)
        args.append(xs)
        out_specs.append(vec)
        out_shape.append(jax.ShapeDtypeStruct((1, 2 * SSM_CH), F32))
    return pl.pallas_call(
        body, grid=(n_t,), in_specs=in_specs, out_specs=out_specs, out_shape=out_shape,
        scratch_shapes=[pltpu.VMEM((1, 2 * SSM_CH), F32)],
        compiler_params=pltpu.CompilerParams(dimension_semantics=("arbitrary",)), name=name,
    )(*args)


def _s5_disc(log_dt, a_re, a_im, b_re, b_im):
    dt = jnp.exp(log_dt)[:, None]
    mag = jnp.exp(a_re * dt)
    lb_re, lb_im = mag * jnp.cos(a_im * dt), mag * jnp.sin(a_im * dt)
    den = a_re * a_re + a_im * a_im
    nr, ni = lb_re - 1.0, lb_im
    f_re = (nr * a_re + ni * a_im) / den
    f_im = (ni * a_re - nr * a_im) / den
    bb_re = f_re[..., None] * b_re - f_im[..., None] * b_im
    bb_im = f_re[..., None] * b_im + f_im[..., None] * b_re
    return lb_re, lb_im, bb_re, bb_im


def _bd(a):
    g, i, j = a.shape
    eye = jnp.eye(g, dtype=a.dtype)
    return (a[:, :, None, :] * eye[:, None, :, None]).reshape(g * i, g * j)


def _unbd(m, i, j):
    g = m.shape[0] // i
    eye = jnp.eye(g, dtype=m.dtype)
    return jnp.sum(m.reshape(g, i, g, j) * eye[:, None, :, None], axis=2)


def _exchange(arrs, modes, name):
    n = len(arrs)
    shapes = [a.shape if md == 'scatter' else (N_DEV,) + a.shape for a, md in zip(arrs, modes)]

    def body(*refs):
        srcs, outs = refs[:n], refs[n:2 * n]
        send_sems, recv_sems, local_sems = refs[2 * n:]
        x, y, c = lax.axis_index("x"), lax.axis_index("y"), lax.axis_index("c")
        me = 4 * x + 2 * y + c

        def piece(q, slot):
            return srcs[q].at[slot] if modes[q] == 'scatter' else srcs[q]

        mine = [pltpu.make_async_copy(piece(q, me), outs[q].at[me], local_sems.at[q]) for q in range(n)]
        for cp in mine:
            cp.start()
        copies = []
        for r in range(1, N_DEV):
            px, py, pc = x ^ (r >> 2), y ^ ((r >> 1) & 1), c ^ (r & 1)
            peer = 4 * px + 2 * py + pc
            for q in range(n):
                copies.append(pltpu.make_async_remote_copy(
                    src_ref=piece(q, peer), dst_ref=outs[q].at[me], send_sem=send_sems.at[(r - 1) * n + q],
                    recv_sem=recv_sems.at[(r - 1) * n + q], device_id=(px, py, pc),
                    device_id_type=pl.DeviceIdType.MESH))
        for cp in copies:
            cp.start()
        for cp in copies:
            cp.wait_recv()
        for cp in copies:
            cp.wait_send()
        for cp in mine:
            cp.wait()

    any_spec = pl.BlockSpec(memory_space=pl.ANY)
    return pl.pallas_call(
        body, out_shape=[jax.ShapeDtypeStruct(sh, a.dtype) for sh, a in zip(shapes, arrs)],
        in_specs=[any_spec] * n, out_specs=[any_spec] * n,
        scratch_shapes=[pltpu.SemaphoreType.DMA(((N_DEV - 1) * n,)), pltpu.SemaphoreType.DMA(((N_DEV - 1) * n,)),
                        pltpu.SemaphoreType.DMA((n,))],
        compiler_params=pltpu.CompilerParams(has_side_effects=True), name=name,
    )(*arrs)


def _peers(x, y, c):
    out = []
    for r in range(1, N_DEV):
        px, py, pc = x ^ (r >> 2), y ^ ((r >> 1) & 1), c ^ (r & 1)
        out.append((r, (px, py, pc), 4 * px + 2 * py + pc))
    return out


def _exchange_start(arrs, modes, carry, name):
    n = len(arrs)
    shapes = [a.shape if md == 'scatter' else (N_DEV,) + a.shape for a, md in zip(arrs, modes)]
    lands = [lax.empty(sh, a.dtype) for sh, a in zip(shapes, arrs)]

    def body(*refs):
        srcs, zones = refs[:n], refs[n:2 * n]
        send_sems, recv_sems = refs[2 * n + 1], refs[2 * n + 2]
        local_sems = refs[-1]
        x, y, c = lax.axis_index("x"), lax.axis_index("y"), lax.axis_index("c")
        me = 4 * x + 2 * y + c

        def piece(q, slot):
            return srcs[q].at[slot] if modes[q] == 'scatter' else srcs[q]

        mine = [pltpu.make_async_copy(piece(q, me), zones[q].at[me], local_sems.at[q]) for q in range(n)]
        for cp in mine:
            cp.start()
        for r, pos, peer in _peers(x, y, c):
            for q in range(n):
                pltpu.make_async_remote_copy(
                    src_ref=piece(q, peer), dst_ref=zones[q].at[me], send_sem=send_sems.at[(r - 1) * n + q],
                    recv_sem=recv_sems.at[(r - 1) * n + q], device_id=pos, device_id_type=pl.DeviceIdType.MESH).start()
        for cp in mine:
            cp.wait()

    hbm = pl.BlockSpec(memory_space=pltpu.HBM)
    sem = pl.BlockSpec(memory_space=pltpu.SEMAPHORE)
    thru = arrs + lands + [carry]
    sems = pltpu.SemaphoreType.DMA(((N_DEV - 1) * n,))
    outs = pl.pallas_call(
        body, name=name, out_shape=(sems, sems, *[pltpu.HBM(a.shape, a.dtype) for a in thru]),
        in_specs=[hbm] * len(thru), out_specs=(sem, sem, *[hbm] * len(thru)),
        input_output_aliases={q: 2 + q for q in range(len(thru))},
        scratch_shapes=[pltpu.SemaphoreType.DMA((n,))],
        compiler_params=pltpu.CompilerParams(has_side_effects=pltpu.SideEffectType.DATAFLOW_SIDE_EFFECTING),
    )(*[pltpu.with_memory_space_constraint(a, pltpu.HBM) for a in thru])
    return dict(send=outs[0], recv=outs[1], srcs=list(outs[2:2 + n]), lands=list(outs[2 + n:2 + 2 * n]),
                modes=modes), outs[-1]


def _exchange_wait(flight, after, name):
    n = len(flight['srcs'])
    modes = flight['modes']

    def body(*refs):
        srcs, zones = refs[:n], refs[n:2 * n]
        send_sems, recv_sems = refs[2 * n], refs[2 * n + 1]
        x, y, c = lax.axis_index("x"), lax.axis_index("y"), lax.axis_index("c")
        me = 4 * x + 2 * y + c
        for r, pos, peer in _peers(x, y, c):
            for q in range(n):
                src = srcs[q].at[peer] if modes[q] == 'scatter' else srcs[q]
                cp = pltpu.make_async_remote_copy(
                    src_ref=src, dst_ref=zones[q].at[me], send_sem=send_sems.at[(r - 1) * n + q],
                    recv_sem=recv_sems.at[(r - 1) * n + q], device_id=pos, device_id_type=pl.DeviceIdType.MESH)
                cp.wait_send()
                cp.wait_recv()

    hbm = pl.BlockSpec(memory_space=pltpu.HBM)
    sem = pl.BlockSpec(memory_space=pltpu.SEMAPHORE)
    bufs = flight['srcs'] + flight['lands']
    outs = pl.pallas_call(
        body, name=name, out_shape=tuple(pltpu.HBM(a.shape, a.dtype) for a in bufs),
        in_specs=[hbm] * (2 * n) + [sem, sem, pl.BlockSpec(memory_space=pl.ANY)], out_specs=tuple([hbm] * (2 * n)),
        input_output_aliases={q: q for q in range(2 * n)},
        compiler_params=pltpu.CompilerParams(has_side_effects=pltpu.SideEffectType.DATAFLOW_SIDE_EFFECTING),
    )(*bufs, flight['send'], flight['recv'], after)
    return list(outs[n:])


def _adamw(parts, w, m, v, name):
    r, c = w.shape
    tr = _tile(r, (256, 128))

    def body(p_ref, w_ref, m_ref, v_ref, g_ref, d_ref, nm_ref, nv_ref):
        g = p_ref[0].astype(F32)
        for d in range(1, N_DEV):
            g = g + p_ref[d].astype(F32)
        m2 = ADAM_B1 * m_ref[...] + (1.0 - ADAM_B1) * g
        v2 = ADAM_B2 * v_ref[...] + (1.0 - ADAM_B2) * jnp.square(g)
        m_hat = m2 / (1.0 - ADAM_B1 ** ADAM_STEP)
        v_hat = v2 / (1.0 - ADAM_B2 ** ADAM_STEP)
        g_ref[...] = g
        d_ref[...] = -ADAM_LR * (m_hat / (jnp.sqrt(v_hat) + ADAM_EPS) + ADAM_WD * w_ref[...])
        nm_ref[...] = m2
        nv_ref[...] = v2

    spec = pl.BlockSpec((tr, c), lambda i: (i, 0))
    return pl.pallas_call(
        body, grid=(r // tr,), in_specs=[pl.BlockSpec((N_DEV, tr, c), lambda i: (0, i, 0)), spec, spec, spec],
        out_specs=[spec] * 4, out_shape=[jax.ShapeDtypeStruct((r, c), F32)] * 4,
        compiler_params=pltpu.CompilerParams(dimension_semantics=("parallel",)), name=name,
    )(parts, w, m, v)


FLAT_W = 512
FLAT_ROWS = 256


def _flat(arrs):
    v = jnp.concatenate([a.reshape(-1) for a in arrs])
    return jnp.pad(v, (0, (-v.shape[0]) % (FLAT_ROWS * FLAT_W))).reshape(-1, FLAT_W)


def _unflat(flat, shapes):
    v = flat.reshape(-1)
    out, off = [], 0
    for sh in shapes:
        n = 1
        for d in sh:
            n *= d
        out.append(v[off:off + n].reshape(sh))
        off += n
    return out


def _full(name, stacked):
    if SHARDED[name] == 0:
        return stacked.reshape((-1,) + stacked.shape[2:])
    return jnp.transpose(stacked, (1, 0, 2)).reshape(stacked.shape[1], -1)


def _shards(name, full):
    if SHARDED[name] == 0:
        return full.reshape((N_DEV, -1) + full.shape[1:])
    r, c = full.shape
    return jnp.transpose(full.reshape(r, N_DEV, c // N_DEV), (1, 0, 2))


def _prep_weights(p):
    q = {}
    w_in = p['l0_w_in']
    z = lambda n: jnp.zeros((D_MODEL, n), w_in.dtype)
    q['w_in'] = jnp.concatenate([w_in[:, :IN_EVEN - QK_ROPE], z(KR_LANE), w_in[:, IN_EVEN - QK_ROPE:],
                                 z(HEAD_PAD - KR_LANE - QK_ROPE)], axis=1)
    dqk = QK_NOPE + QK_ROPE
    q['w_uq'] = jnp.pad(p['l0_w_uq'].reshape(Q_RANK, N_HEADS, dqk), ((0, 0), (0, 0), (0, HEAD_PAD - dqk))
                        ).reshape(Q_RANK, N_HEADS * HEAD_PAD)
    ukv = p['l0_w_ukv'].reshape(KV_RANK, N_HEADS, 2, QK_NOPE)
    padh = lambda a: jnp.pad(a, ((0, 0), (0, 0), (0, HEAD_PAD - QK_NOPE))).reshape(KV_RANK, N_HEADS * HEAD_PAD)
    q['w_ukv'] = jnp.concatenate([padh(ukv[:, :, 0]), padh(ukv[:, :, 1])], axis=1)
    wo = p['l0_w_out']
    wo_a = jnp.pad(wo[CONV_WIDTH:].reshape(N_HEADS, V_DIM, D_MODEL), ((0, 0), (0, HEAD_PAD - V_DIM), (0, 0)))
    q['w_out'] = jnp.concatenate([wo[:CONV_WIDTH], wo_a.reshape(N_HEADS * HEAD_PAD, D_MODEL)], axis=0)
    return q


def _unprep_grads(g):
    out = {}
    d = g['w_in']
    out['l0_w_in'] = jnp.concatenate([d[:, :IN_EVEN - QK_ROPE],
                                      d[:, IN_EVEN - QK_ROPE + KR_LANE:IN_EVEN + KR_LANE]], axis=1)
    dqk = QK_NOPE + QK_ROPE
    out['l0_w_uq'] = g['w_uq'].reshape(Q_RANK, N_HEADS, HEAD_PAD)[:, :, :dqk].reshape(Q_RANK, N_HEADS * dqk)
    d = g['w_ukv'].reshape(KV_RANK, 2, N_HEADS, HEAD_PAD)[:, :, :, :QK_NOPE]
    out['l0_w_ukv'] = jnp.transpose(d, (0, 2, 1, 3)).reshape(KV_RANK, N_HEADS * 2 * QK_NOPE)
    d = g['w_out']
    da = d[CONV_WIDTH:].reshape(N_HEADS, HEAD_PAD, D_MODEL)[:, :V_DIM].reshape(N_HEADS * V_DIM, D_MODEL)
    out['l0_w_out'] = jnp.concatenate([d[:CONV_WIDTH], da], axis=0)
    return out


def _pad_rows(w, rows):
    return jnp.pad(w, [(0, 0)] * (w.ndim - 2) + [(0, rows - w.shape[-2]), (0, 0)])


def _ffn_fwd(x, rep, got, pre, tag):
    s = x.shape[0]
    xn = _rms_fwd(x, rep[pre + 'ffn_norm'], f"{tag}_ffn_norm")
    w_up = got[pre + 'w_up']
    hu = _mm(xn, w_up, gb=_same, go=_same, groups=N_DEV, name=f"{tag}_ffn_up").reshape(N_DEV * s, FF_SHARD)
    taps = _pad_rows(got[pre + 'ffn_conv_w'], TAP_ROWS).reshape(N_DEV * TAP_ROWS, FF_SHARD)
    bias = _pad_rows(rep[pre + 'ffn_conv_b'].reshape(N_DEV, 1, FF_SHARD), TAP_ROWS).reshape(N_DEV * TAP_ROWS, FF_SHARD)
    (act,) = _conv_fwd([(hu, 0), (hu, FF_HALF)], taps, [0, FF_HALF], [bias, bias], [0, FF_HALF], _pair_pre,
                       _gate_post, [BF16], name=f"{tag}_ffn_gate", **_gate(s))
    act = act.reshape(FF_HALF, s, FF_SHARD)
    w_down = got[pre + 'w_down'].reshape(FF_HALF, FF_SHARD, D_MODEL)
    y = _mm(act, w_down, ga=_same, gb=_same, groups=FF_HALF, res=x, name=f"{tag}_ffn_down")
    return y, (x, xn, hu, act, taps, bias, w_up, w_down)


def _ffn_bwd(dy, saved, rep, pre, tag, grads, gsh):
    x, xn, hu, act, taps, bias, w_up, w_down = saved
    s = x.shape[0]
    dact = _mm(dy, w_down, tb=True, gb=_same, go=_same, groups=FF_HALF, name=f"{tag}_ffn_down_dx")
    gsh[pre + 'w_down'] = _mm(act, dy, ta=True, ga=_same, go=_same, groups=FF_HALF, out_dtype=BF16,
                              name=f"{tag}_ffn_down_dw").reshape(N_DEV, FF_SHARD // 2, D_MODEL)
    dha, dhb, dwa, dwb, dba, dbb = _conv_bwd(
        [(hu, 0), (hu, FF_HALF)], taps, [0, FF_HALF], [bias, bias], [0, FF_HALF], _pair_pre, _gate_post,
        [(dact.reshape(FF_HALF * s, FF_SHARD), 0)], BF16, name=f"{tag}_ffn_gate_bwd", **_gate(s))
    dha, dhb = dha.reshape(FF_HALF, s, FF_SHARD), dhb.reshape(FF_HALF, s, FF_SHARD)
    dtaps = jnp.concatenate([dwa, dwb], axis=0).reshape(N_DEV, TAP_ROWS, FF_SHARD)
    gsh[pre + 'ffn_conv_w'] = dtaps[:, :FFN_K].astype(BF16)
    grads[pre + 'ffn_conv_b'] = jnp.concatenate([dba, dbb], axis=0).reshape(N_DEV, TAP_ROWS, FF_SHARD)[:, 0].reshape(-1)
    upper = lambda g: g + FF_HALF
    dxn = _mm(dha, w_up, tb=True, ga=_same, gb=_same, groups=FF_HALF, name=f"{tag}_ffn_up_dx_a")
    dxn = _mm(dhb, w_up, tb=True, ga=_same, gb=upper, groups=FF_HALF, res=dxn, name=f"{tag}_ffn_up_dx_b")
    dwu = [_mm(xn, dh, ta=True, gb=_same, go=_same, groups=FF_HALF, out_dtype=BF16, name=f"{tag}_ffn_up_dw_{t}")
           for t, dh in (("a", dha), ("b", dhb))]
    gsh[pre + 'w_up'] = jnp.concatenate(dwu, axis=0)
    dx, dg = _rms_bwd(x, rep[pre + 'ffn_norm'], dxn, dy, f"{tag}_ffn_norm_bwd")
    grads[pre + 'ffn_norm'] = dg.reshape(-1)
    return dx


def _mla_fwd(x, rep, taps, q, tabs):
    s = x.shape[0]
    xn = _rms_fwd(x, rep['l0_mix_norm'], "l0_mix_norm")
    hmat = _mm(xn, q['w_in'], name="l0_in")
    wt = _pad_rows(taps, 4 * TAP_ROWS)
    cpar = [rep['l0_conv_b'].reshape(1, -1), rep['l0_conv_ln_g'].reshape(1, -1), rep['l0_conv_ln_b'].reshape(1, -1)]
    (u,) = _conv_fwd([(hmat, 0), (hmat, 1)], wt, [0], cpar, [0, 0, 0], _glu_pre, _ln_silu_post, [BF16],
                     name="l0_conv", **_conva(s))
    qn, kvn = rep['l0_q_norm'].reshape(1, -1), rep['l0_kv_norm'].reshape(1, -1)
    cqn, ckvn = _rowwise(lambda a, b, ga, gb: (_rms(a, ga), _rms(b, gb)),
                         [(hmat, Q_RANK, 2 * CONV_WIDTH // Q_RANK), (hmat, KV_RANK, (2 * CONV_WIDTH + Q_RANK) // KV_RANK)],
                         [qn, kvn], [(Q_RANK, BF16), (KV_RANK, BF16)], [], ts=512, name="l0_latent_norm")
    qf = _mm(cqn, q['w_uq'], name="l0_uq")
    kvf = _mm(ckvn, q['w_ukv'], name="l0_ukv")
    q_rot, k_full, v = _rope_fwd(qf, kvf, hmat, tabs, "l0_rope")
    o, lse = _attn_fwd(q_rot, k_full, v, "l0_attn")
    cat = jnp.concatenate([u, o], axis=1)
    y = _mm(cat, q['w_out'], res=x, name="l0_out")
    return y, (x, xn, hmat, wt, cpar, qn, kvn, cqn, ckvn, q_rot, k_full, v, lse, cat)


def _mla_bwd(dy, saved, rep, q, tabs, grads, gq):
    x, xn, hmat, wt, cpar, qn, kvn, cqn, ckvn, q_rot, k_full, v, lse, cat = saved
    s = x.shape[0]
    dcat = _mm(dy, q['w_out'], tb=True, name="l0_out_dx")
    gq['w_out'] = _mm(cat, dy, ta=True, name="l0_out_dw")
    dq, lse_r, dlt_r = _attn_bwd_q(q_rot, k_full, v, cat, dcat, lse, "l0_attn_bwd_q")
    dk, dv = _attn_bwd_kv(q_rot, k_full, v, dcat, lse_r, dlt_r, "l0_attn_bwd_kv")
    dqf, dkvf, dkr = _rope_bwd(dq, dk, dv, tabs, "l0_rope_bwd")
    dcqn = _mm(dqf, q['w_uq'], tb=True, name="l0_uq_dx")
    gq['w_uq'] = _mm(cqn, dqf, ta=True, name="l0_uq_dw")
    dckvn = _mm(dkvf, q['w_ukv'], tb=True, name="l0_ukv_dx")
    gq['w_ukv'] = _mm(ckvn, dkvf, ta=True, name="l0_ukv_dw")

    def lat_bwd(a, b, da, db, ga, gb):
        _, vjp = jax.vjp(lambda a_, b_, ga_, gb_: (_rms(a_, ga_), _rms(b_, gb_)), a, b, ga, gb)
        return vjp((da, db))

    dcq, dckv, dqn, dkvn = _rowwise(
        lat_bwd, [(hmat, Q_RANK, 2 * CONV_WIDTH // Q_RANK), (hmat, KV_RANK, (2 * CONV_WIDTH + Q_RANK) // KV_RANK),
                  (dcqn, Q_RANK, 0), (dckvn, KV_RANK, 0)],
        [qn, kvn], [(Q_RANK, F32), (KV_RANK, F32)], [(1, Q_RANK), (1, KV_RANK)], ts=512, name="l0_latent_norm_bwd")
    grads['l0_q_norm'], grads['l0_kv_norm'] = dqn.reshape(-1), dkvn.reshape(-1)
    da, dg, dwt, dcb, dlg, dlb = _conv_bwd(
        [(hmat, 0), (hmat, 1)], wt, [0], cpar, [0, 0, 0], _glu_pre, _ln_silu_post, [(dcat, 0)], F32,
        name="l0_conv_bwd", **_conva(s))
    gq['conv_w'] = dwt[:CONV_K]
    grads['l0_conv_b'], grads['l0_conv_ln_g'], grads['l0_conv_ln_b'] = dcb.reshape(-1), dlg.reshape(-1), dlb.reshape(-1)
    dh = jnp.concatenate([da, dg, dcq, dckv, dkr], axis=1)
    dxn = _mm(dh, q['w_in'], tb=True, name="l0_in_dx")
    gq['w_in'] = _mm(xn, dh, ta=True, name="l0_in_dw")
    dx, dgn = _rms_bwd(x, rep['l0_mix_norm'], dxn, dy, "l0_mix_norm_bwd")
    grads['l0_mix_norm'] = dgn.reshape(-1)
    return dx


def _gelu_skip(yc, u, d):
    return jax.nn.gelu(yc + d * u)


def _glu_out(z1, z2, b1, b2, x):
    return x + (z1 + b1) * jax.nn.sigmoid(z2 + b2)


def _s5_fwd(x, rep, w_in, w_glu):
    xn = _rms_fwd(x, rep['l1_mix_norm'], "l1_mix_norm")
    u = _mm(xn, w_in, name="l1_in")
    lb_re, lb_im, bb_re, bb_im = _s5_disc(rep['l1_log_dt'], rep['l1_a_re'], rep['l1_a_im'], rep['l1_b_re'],
                                          rep['l1_b_im'])
    lam = jnp.concatenate([lb_re.reshape(1, -1), lb_im.reshape(1, -1)], axis=1)
    tr = lambda a: jnp.transpose(a, (0, 2, 1))
    bmat = jnp.concatenate([_bd(tr(bb_re)), _bd(tr(bb_im))], axis=1)
    cmat = jnp.concatenate([_bd(tr(rep['l1_c_re'])), -_bd(tr(rep['l1_c_im']))], axis=0)
    bu = _mm(u, bmat, name="l1_bu")
    (xs,) = _scan(bu, lam, reverse=False, name="l1_scan")
    yc = _mm(xs, cmat, name="l1_cx")
    dsk = rep['l1_d'].reshape(1, -1)
    (y,) = _rowwise(_gelu_skip, [(yc, SSM_WIDTH, 0), (u, SSM_WIDTH, 0)], [dsk], [(SSM_WIDTH, BF16)], [],
                    ts=512, name="l1_gelu")
    z = _mm(y, w_glu, name="l1_glu")
    bg = rep['l1_b_glu'].reshape(1, -1)
    (out,) = _rowwise(lambda z1, z2, xv, b1, b2: _glu_out(z1, z2, b1, b2, xv),
                      [(z, D_MODEL, 0), (z, D_MODEL, 1), (x, D_MODEL, 0)], [bg[:, :D_MODEL], bg[:, D_MODEL:]],
                      [(D_MODEL, F32)], [], ts=512, name="l1_glu_out")
    return out, (x, xn, u, lam, bmat, cmat, xs, yc, dsk, y, z, bg, w_in, w_glu)


def _s5_bwd(dy, saved, rep, grads, gq):
    x, xn, u, lam, bmat, cmat, xs, yc, dsk, y, z, bg, w_in, w_glu = saved

    def glu_bwd(z1, z2, dv, b1, b2):
        _, vjp = jax.vjp(lambda a, b, c, d: (a + c) * jax.nn.sigmoid(b + d), z1, z2, b1, b2)
        d1, d2, db1, db2 = vjp(dv)
        return jnp.concatenate([d1, d2], axis=1), db1, db2

    dz, db1, db2 = _rowwise(glu_bwd, [(z, D_MODEL, 0), (z, D_MODEL, 1), (dy, D_MODEL, 0)],
                            [bg[:, :D_MODEL], bg[:, D_MODEL:]], [(2 * D_MODEL, BF16)], [(1, D_MODEL), (1, D_MODEL)],
                            ts=512, name="l1_glu_out_bwd")
    grads['l1_b_glu'] = jnp.concatenate([db1, db2], axis=1).reshape(-1)
    dyv = _mm(dz, w_glu, tb=True, name="l1_glu_dx")
    gq['l1_w_glu'] = _mm(y, dz, ta=True, name="l1_glu_dw")

    def gelu_bwd(ycv, uv, dv, dk):
        _, vjp = jax.vjp(_gelu_skip, ycv, uv, dk)
        return vjp(dv)

    dyc, du_skip, dd = _rowwise(gelu_bwd, [(yc, SSM_WIDTH, 0), (u, SSM_WIDTH, 0), (dyv, SSM_WIDTH, 0)], [dsk],
                                [(SSM_WIDTH, F32), (SSM_WIDTH, F32)], [(1, SSM_WIDTH)], ts=512, name="l1_gelu_bwd")
    grads['l1_d'] = dd.reshape(-1)
    dxs = _mm(dyc, cmat, tb=True, name="l1_cx_dx")
    dcmat = _mm(xs, dyc, ta=True, name="l1_cx_dw")
    gs, dlam = _scan(dxs, lam, reverse=True, xs=xs, name="l1_scan_bwd")
    dlr, dli = dlam[:, :SSM_CH], dlam[:, SSM_CH:]
    du = _mm(gs, bmat, tb=True, res=du_skip, name="l1_bu_dx")
    dbmat = _mm(u, gs, ta=True, name="l1_bu_dw")
    tr = lambda a: jnp.transpose(a, (0, 2, 1))
    grads['l1_c_re'] = tr(_unbd(dcmat[:SSM_CH], SSM_STATE, SSM_GROUP))
    grads['l1_c_im'] = -tr(_unbd(dcmat[SSM_CH:], SSM_STATE, SSM_GROUP))
    dbb_re = tr(_unbd(dbmat[:, :SSM_CH], SSM_GROUP, SSM_STATE))
    dbb_im = tr(_unbd(dbmat[:, SSM_CH:], SSM_GROUP, SSM_STATE))
    names = ['l1_log_dt', 'l1_a_re', 'l1_a_im', 'l1_b_re', 'l1_b_im']
    _, vjp = jax.vjp(_s5_disc, *[rep[n] for n in names])
    for n, gval in zip(names, vjp((dlr.reshape(SSM_GROUPS, SSM_STATE), dli.reshape(SSM_GROUPS, SSM_STATE), dbb_re, dbb_im))):
        grads[n] = gval
    dxn = _mm(du, w_in, tb=True, name="l1_in_dx")
    gq['l1_w_in'] = _mm(xn, du, ta=True, name="l1_in_dw")
    dx, dgn = _rms_bwd(x, rep['l1_mix_norm'], dxn, dy, "l1_mix_norm_bwd")
    grads['l1_mix_norm'] = dgn.reshape(-1)
    return dx


def _loss_head(x, g, target):
    d = x.shape[1]

    def fn(xv, tv, gv):
        y, vjp = jax.vjp(_rms, xv, gv)
        err = y - tv
        part = 0.5 * jnp.sum(jnp.mean(jnp.square(err), axis=-1, keepdims=True), axis=0, keepdims=True)
        dx, dg = vjp(err * (1.0 / d))
        return dx, jnp.broadcast_to(part, (1, 128)), dg

    return _rowwise(fn, [(x, d, 0), (target, d, 0)], [g.reshape(1, -1)], [(d, F32)], [(1, 128), (1, d)], ts=512,
                    name="loss_head")


FIRST = ('l0_w_in', 'l0_conv_w', 'l0_w_uq', 'l0_w_ukv', 'l0_w_out')
REST = tuple(n for n in SHARDED if n not in FIRST)
GRADS_L1 = ('l1_w_in', 'l1_w_glu', 'l1_w_up', 'l1_ffn_conv_w', 'l1_w_down')
GRADS_L0_FFN = ('l0_w_up', 'l0_ffn_conv_w', 'l0_w_down')


def _local_step(x, target, rep, got, wait_rest, send_grads):
    even = {n: _full(n, got[n]) for n in ('l0_w_in', 'l0_w_uq', 'l0_w_ukv', 'l0_w_out')}
    q = _prep_weights(even)
    tabs = _rope_tables(x.shape[0])
    x1, s_mla = _mla_fwd(x, rep, _full('l0_conv_w', got['l0_conv_w']), q, tabs)
    got = wait_rest(x1)
    x2, s_f0 = _ffn_fwd(x1, rep, got, 'l0_', "l0")
    x3, s_s5 = _s5_fwd(x2, rep, _full('l1_w_in', got['l1_w_in']), _full('l1_w_glu', got['l1_w_glu']))
    x4, s_f1 = _ffn_fwd(x3, rep, got, 'l1_', "l1")
    dx4, loss, dgf = _loss_head(x4, rep['final_norm'], target)
    grads, gq, gsh = {'final_norm': dgf.reshape(-1)}, {}, {}
    dx3 = _ffn_bwd(dx4, s_f1, rep, 'l1_', "l1", grads, gsh)
    dx2 = _s5_bwd(dx3, s_s5, rep, grads, gq)
    for n in ('l1_w_in', 'l1_w_glu'):
        gsh[n] = _shards(n, gq[n]).astype(BF16)
    dx2 = send_grads(GRADS_L1, gsh, dx2)
    dx1 = _ffn_bwd(dx2, s_f0, rep, 'l0_', "l0", grads, gsh)
    dx1 = send_grads(GRADS_L0_FFN, gsh, dx1)
    dx0 = _mla_bwd(dx1, s_mla, rep, q, tabs, grads, gq)
    full = _unprep_grads(gq)
    full['l0_conv_w'] = gq['conv_w']
    return loss[0, 0], dx0, grads, {n: _shards(n, full[n]).astype(BF16) for n in FIRST}


def kernel(x, l0_mix_norm, l0_w_in, l0_conv_w, l0_conv_b, l0_conv_ln_g, l0_conv_ln_b, l0_q_norm, l0_kv_norm, l0_w_uq, l0_w_ukv, l0_w_out, l0_ffn_norm, l0_w_up, l0_ffn_conv_w, l0_ffn_conv_b, l0_w_down, l1_mix_norm, l1_w_in, l1_log_dt, l1_a_re, l1_a_im, l1_b_re, l1_b_im, l1_c_re, l1_c_im, l1_d, l1_w_glu, l1_b_glu, l1_ffn_norm, l1_w_up, l1_ffn_conv_w, l1_ffn_conv_b, l1_w_down, final_norm, loss_target, m_l0_mix_norm, m_l0_w_in, m_l0_conv_w, m_l0_conv_b, m_l0_conv_ln_g, m_l0_conv_ln_b, m_l0_q_norm, m_l0_kv_norm, m_l0_w_uq, m_l0_w_ukv, m_l0_w_out, m_l0_ffn_norm, m_l0_w_up, m_l0_ffn_conv_w, m_l0_ffn_conv_b, m_l0_w_down, m_l1_mix_norm, m_l1_w_in, m_l1_log_dt, m_l1_a_re, m_l1_a_im, m_l1_b_re, m_l1_b_im, m_l1_c_re, m_l1_c_im, m_l1_d, m_l1_w_glu, m_l1_b_glu, m_l1_ffn_norm, m_l1_w_up, m_l1_ffn_conv_w, m_l1_ffn_conv_b, m_l1_w_down, m_final_norm, v_l0_mix_norm, v_l0_w_in, v_l0_conv_w, v_l0_conv_b, v_l0_conv_ln_g, v_l0_conv_ln_b, v_l0_q_norm, v_l0_kv_norm, v_l0_w_uq, v_l0_w_ukv, v_l0_w_out, v_l0_ffn_norm, v_l0_w_up, v_l0_ffn_conv_w, v_l0_ffn_conv_b, v_l0_w_down, v_l1_mix_norm, v_l1_w_in, v_l1_log_dt, v_l1_a_re, v_l1_a_im, v_l1_b_re, v_l1_b_im, v_l1_c_re, v_l1_c_im, v_l1_d, v_l1_w_glu, v_l1_b_glu, v_l1_ffn_norm, v_l1_w_up, v_l1_ffn_conv_w, v_l1_ffn_conv_b, v_l1_w_down, v_final_norm):
    args = dict(locals())
    w = {n: args[n] for n in WEIGHTS}
    m = {n: args['m_' + n] for n in WEIGHTS}
    v = {n: args['v_' + n] for n in WEIGHTS}
    payload = lambda n: w[n] if n in TAPS else w[n].astype(BF16)
    got = dict(zip(FIRST, _exchange([payload(n) for n in FIRST], ['gather'] * len(FIRST), "gather_first")))
    rep = {n: w[n] for n in REPLICATED}
    rest, got['l0_conv_w'] = _exchange_start([payload(n) for n in REST], ['gather'] * len(REST), got['l0_conv_w'],
                                             "gather_rest_start")
    wait_rest = lambda after: dict(zip(REST, _exchange_wait(rest, after, "gather_rest_wait")))
    flights = []

    def send_grads(names, gsh, carry):
        tag = "grads_" + names[0][:2]
        flight, carry = _exchange_start([gsh[n] for n in names], ['scatter'] * len(names), carry, tag + "_start")
        flights.append((names, flight, tag + "_wait"))
        return carry

    loss, dx, grads, gsh = _local_step(x[0], loss_target[0], rep, got, wait_rest, send_grads)

    last = _exchange([gsh[n] for n in FIRST] + [_flat([grads[n] for n in REPLICATED])],
                     ['scatter'] * len(FIRST) + ['gather'], "exchange_grads")
    recv = dict(zip(FIRST, last))
    for names, flight, name in flights:
        recv.update(zip(names, _exchange_wait(flight, last[-1], name)))
    res = [dict(), dict(), dict(), dict()]
    for n in SHARDED:
        for kind, a in enumerate(_adamw(recv[n], w[n], m[n], v[n], "adamw_" + n)):
            res[kind][n] = a
    flatr = lambda d: _flat([d[n] for n in REPLICATED])
    rp_out = _adamw(last[-1], flatr(w), flatr(m), flatr(v), "adamw_replicated")
    for kind in range(4):
        for n, a in zip(REPLICATED, _unflat(rp_out[kind], [w[n].shape for n in REPLICATED])):
            res[kind][n] = a
    total = lax.psum(loss, ("x", "y", "c"))
    return (total, dx[None], *[res[0][n] for n in WEIGHTS], *[res[1][n] for n in WEIGHTS],
            *[res[2][n] for n in WEIGHTS], *[res[3][n] for n in WEIGHTS])
```

```python
import functools

import jax
import jax.numpy as jnp
from jax import lax
from jax.experimental import pallas as pl
from jax.experimental.pallas import tpu as pltpu

F32 = jnp.float32
BF16 = jnp.bfloat16

N_DEV = 8
D_MODEL = 1024
EPS = 1e-6
LN_EPS = 1e-5
CONV_WIDTH = 512
CONV_K = 31
N_HEADS = 8
QK_NOPE = 64
QK_ROPE = 32
V_DIM = 64
HEAD_PAD = 128
Q_RANK = 256
KV_RANK = 128
ROPE_BASE = 10000.0
IN_EVEN = 2 * CONV_WIDTH + Q_RANK + KV_RANK + QK_ROPE
IN_PAD = 1536
KR_LANE = 64
SSM_WIDTH = 512
SSM_GROUP = 16
SSM_GROUPS = 32
SSM_STATE = 64
SSM_CH = SSM_GROUPS * SSM_STATE
D_FF = 2816
FF_SHARD = 2 * D_FF // N_DEV
FF_HALF = N_DEV // 2
FFN_K = 3
TAP_ROWS = 8
ADAM_LR, ADAM_B1, ADAM_B2, ADAM_EPS, ADAM_WD, ADAM_STEP = 0.001, 0.9, 0.999, 1e-08, 0.01, 10

WEIGHTS = ['l0_mix_norm', 'l0_w_in', 'l0_conv_w', 'l0_conv_b', 'l0_conv_ln_g', 'l0_conv_ln_b', 'l0_q_norm',
           'l0_kv_norm', 'l0_w_uq', 'l0_w_ukv', 'l0_w_out', 'l0_ffn_norm', 'l0_w_up', 'l0_ffn_conv_w',
           'l0_ffn_conv_b', 'l0_w_down', 'l1_mix_norm', 'l1_w_in', 'l1_log_dt', 'l1_a_re', 'l1_a_im', 'l1_b_re',
           'l1_b_im', 'l1_c_re', 'l1_c_im', 'l1_d', 'l1_w_glu', 'l1_b_glu', 'l1_ffn_norm', 'l1_w_up',
           'l1_ffn_conv_w', 'l1_ffn_conv_b', 'l1_w_down', 'final_norm']
SHARDED = {'l0_w_in': 1, 'l0_conv_w': 1, 'l0_w_uq': 1, 'l0_w_ukv': 1, 'l0_w_out': 0, 'l0_w_up': 1,
           'l0_ffn_conv_w': 1, 'l0_w_down': 0, 'l1_w_in': 0, 'l1_w_glu': 1, 'l1_w_up': 1, 'l1_ffn_conv_w': 1,
           'l1_w_down': 0}
TAPS = ('l0_conv_w', 'l0_ffn_conv_w', 'l1_ffn_conv_w')
REPLICATED = [n for n in WEIGHTS if n not in SHARDED]


def _tile(n, cands):
    for c in cands:
        if n % c == 0:
            return c
    return n


def _same(g):
    return g


def _mm(a, b, *, ta=False, tb=False, res=None, out_dtype=F32, name, ga=None, gb=None, go=None, groups=1,
        sparse=None):
    a2, b2 = (a.shape[1:] if ga else a.shape), (b.shape[1:] if gb else b.shape)
    m, kd = (a2[1], a2[0]) if ta else a2
    kd2, n = (b2[1], b2[0]) if tb else b2
    assert kd == kd2, (a.shape, b.shape, ta, tb)
    tm = _tile(m, (1024, 512, 256, 128))
    tn = _tile(n, (512, 384, 256, 128))
    tk = _tile(kd, (1024, 512, 256, 128))
    nk = kd // tk
    a_m = b_n = k_of = None
    if sparse is not None:
        (tm, tn, tk), (m, n), nk = sparse['tiles'], sparse['out'], sparse['nk']
        a_m, b_n, k_of = sparse.get('a_m'), sparse.get('b_n'), sparse.get('k_of')
    a_m = a_m or (lambda i, j: i)
    b_n = b_n or (lambda i, j: j)
    k_of = k_of or (lambda i, j, k: k)
    summed = go is None and (ga is not None or gb is not None)
    nkk = nk * (groups if summed else 1)
    dn = (((0 if ta else 1,), (1 if tb else 0,)), ((), ()))

    def body(*refs):
        if res is None:
            a_ref, b_ref, o_ref, acc_ref = refs
            r_ref = None
        else:
            a_ref, b_ref, r_ref, o_ref, acc_ref = refs
        k = pl.program_id(3)
        p = lax.dot_general(a_ref[...].astype(BF16), b_ref[...].astype(BF16), dn, preferred_element_type=F32)

        @pl.when(k == 0)
        def _():
            acc_ref[...] = p

        @pl.when(k > 0)
        def _():
            acc_ref[...] += p

        @pl.when(k == nkk - 1)
        def _():
            out = acc_ref[...]
            if r_ref is not None:
                out = out + r_ref[...]
            o_ref[...] = out.astype(out_dtype)

    def spec(shape2, idx2, gmap):
        if gmap is None:
            return pl.BlockSpec(shape2, lambda g, i, j, kk: idx2(i, j, kk % nk))
        if summed:
            return pl.BlockSpec((None,) + shape2, lambda g, i, j, kk: (gmap(kk // nk),) + idx2(i, j, kk % nk))
        return pl.BlockSpec((None,) + shape2, lambda g, i, j, kk: (gmap(g),) + idx2(i, j, kk))

    a_idx = (lambda i, j, k: (k_of(i, j, k), a_m(i, j))) if ta else (lambda i, j, k: (a_m(i, j), k_of(i, j, k)))
    b_idx = (lambda i, j, k: (b_n(i, j), k_of(i, j, k))) if tb else (lambda i, j, k: (k_of(i, j, k), b_n(i, j)))
    a_spec = spec((tk, tm) if ta else (tm, tk), a_idx, ga)
    b_spec = spec((tn, tk) if tb else (tk, tn), b_idx, gb)
    o_spec = spec((tm, tn), lambda i, j, k: (i, j), go)
    in_specs, args = [a_spec, b_spec], [a, b]
    if res is not None:
        in_specs.append(o_spec)
        args.append(res)
    out_shape = (groups, m, n) if go else (m, n)
    return pl.pallas_call(
        body, grid=(groups if go else 1, m // tm, n // tn, nkk), in_specs=in_specs, out_specs=o_spec,
        out_shape=jax.ShapeDtypeStruct(out_shape, out_dtype), scratch_shapes=[pltpu.VMEM((tm, tn), F32)],
        compiler_params=pltpu.CompilerParams(dimension_semantics=("parallel", "parallel", "parallel", "arbitrary")),
        name=name)(*args)


def _rowwise(fn, rows, bcasts, row_outs, red_outs, *, ts, name):
    s = rows[0][0].shape[0]
    nr, nb, nro, nre = len(rows), len(bcasts), len(row_outs), len(red_outs)

    def body(*refs):
        i = pl.program_id(0)
        outs = fn(*[r[...] for r in refs[:nr + nb]])
        if not isinstance(outs, (tuple, list)):
            outs = (outs,)
        o_refs = refs[nr + nb:]
        for q in range(nro):
            o_refs[q][...] = outs[q].astype(o_refs[q].dtype)
        for q in range(nro, nro + nre):
            @pl.when(i == 0)
            def _(q=q):
                o_refs[q][...] = outs[q]

            @pl.when(i > 0)
            def _(q=q):
                o_refs[q][...] += outs[q]

    in_specs = [pl.BlockSpec((ts, w), functools.partial(lambda i, cb: (i, cb), cb=cb)) for (_, w, cb) in rows]
    in_specs += [pl.BlockSpec(b.shape, functools.partial(lambda i, nd: (0,) * nd, nd=b.ndim)) for b in bcasts]
    out_specs = [pl.BlockSpec((ts, w), lambda i: (i, 0)) for (w, _) in row_outs]
    out_specs += [pl.BlockSpec((r, w), lambda i: (0, 0)) for (r, w) in red_outs]
    out_shape = [jax.ShapeDtypeStruct((s, w), dt) for (w, dt) in row_outs]
    out_shape += [jax.ShapeDtypeStruct((r, w), F32) for (r, w) in red_outs]
    return pl.pallas_call(
        body, grid=(s // ts,), in_specs=in_specs, out_specs=out_specs, out_shape=out_shape,
        compiler_params=pltpu.CompilerParams(dimension_semantics=("arbitrary",)), name=name,
    )(*[r[0] for r in rows], *bcasts)


def _rms(x, g):
    return x * lax.rsqrt(jnp.mean(x * x, axis=-1, keepdims=True) + EPS) * g


def _rms_fwd(x, g, name):
    return _rowwise(lambda xv, gv: _rms(xv, gv), [(x, x.shape[1], 0)], [g.reshape(1, -1)],
                    [(x.shape[1], BF16)], [], ts=512, name=name)[0]


def _rms_bwd(x, g, dxn, dres, name):
    d = x.shape[1]

    def fn(xv, dv, rv, gv):
        _, vjp = jax.vjp(_rms, xv, gv)
        dx, dg = vjp(dv.astype(F32))
        return rv + dx, dg

    return _rowwise(fn, [(x, d, 0), (dxn, d, 0), (dres, d, 0)], [g.reshape(1, -1)], [(d, F32)], [(1, d)],
                    ts=512, name=name)


def _cspec(mode, off, rows, width, rowblk, n_rb):
    if mode == 'col':
        return pl.BlockSpec((rows, width), lambda jc, i: (rowblk(i), off + jc))
    return pl.BlockSpec((rows, width), lambda jc, i: ((off + jc) * n_rb + rowblk(i), 0))


TAP_SPREAD = 24


def _stage_shape(k_taps, n, tc):
    return (8, n + TAP_SPREAD, tc) if k_taps > 8 else (1, 8, 128)


def _tap_windows(ref, offsets, n, stage):
    if len(offsets) <= 8:
        return [functools.partial(lambda v: v, ref[pl.ds(o, n), :]) for o in offsets]
    lows = {}
    for o in offsets:
        lows[o % 8] = min(o, lows.get(o % 8, o))
    for r, lo in lows.items():
        span = n + max(o for o in offsets if o % 8 == r) - lo
        stage[r, pl.ds(0, span), :] = ref[pl.ds(lo, span), :]
    return [functools.partial(lambda o: stage[o % 8, pl.ds(o - lows[o % 8], n), :], o) for o in offsets]


def _conv_fwd(xins, w, woffs, params, poffs, pre, post, outs, *, mode, s, k_taps, hb, ts, tc, rc, n_cb, name):
    n_s, nx, ncv, npar, no = s // ts, len(xins), len(woffs), len(params), len(outs)
    rpb = ts // hb
    kp = TAP_ROWS * ((k_taps + TAP_ROWS - 1) // TAP_ROWS)
    pr = 1 if mode == 'col' else TAP_ROWS

    def body(*refs):
        mains, halos = refs[:nx], refs[nx:2 * nx]
        w_refs = refs[2 * nx:2 * nx + ncv]
        p_refs = refs[2 * nx + ncv:2 * nx + ncv + npar]
        o_refs = refs[2 * nx + ncv + npar:2 * nx + ncv + npar + no]
        u_s = refs[2 * nx + ncv + npar + no:2 * nx + ncv + npar + no + ncv]
        win_s = refs[2 * nx + ncv + npar + no + ncv:2 * nx + ncv + npar + no + 2 * ncv]
        stage_s = refs[2 * nx + ncv + npar + no + 2 * ncv:]
        i = pl.program_id(1)
        um = pre(*[r[...].astype(F32) for r in mains])
        uh = pre(*[r[...].astype(F32) for r in halos])
        first = (i > 0).astype(F32)
        for q in range(ncv):
            u_s[q][pl.ds(0, hb), :] = uh[q] * first
            u_s[q][pl.ds(hb, ts), :] = um[q]
        pv = [r[0:1, :] for r in p_refs]

        def chunk(ci, carry):
            r0 = pl.multiple_of(ci * rc, rc)
            cs = []
            for q in range(ncv):
                win_s[q][...] = u_s[q][pl.ds(r0, rc + hb), :]
                acc = jnp.zeros((rc, tc), F32)
                offsets = [hb - (k_taps - 1) + t for t in range(k_taps)]
                for t, ut in enumerate(_tap_windows(win_s[q], offsets, rc, stage_s[q])):
                    acc = acc + w_refs[q][pl.ds(t, 1), :] * ut()
                cs.append(acc)
            res = post(cs, pv)
            for q in range(no):
                o_refs[q][pl.ds(r0, rc), :] = res[q].astype(o_refs[q].dtype)
            return carry

        lax.fori_loop(0, ts // rc, chunk, 0)

    main = lambda i: i
    prev = lambda i: jnp.maximum(i * rpb - 1, 0)
    zero = lambda i: 0
    in_specs = [_cspec(mode, off, ts, tc, main, n_s) for _, off in xins]
    in_specs += [_cspec(mode, off, hb, tc, prev, s // hb) for _, off in xins]
    in_specs += [_cspec(mode, off, kp, tc, zero, 1) for off in woffs]
    in_specs += [_cspec(mode, off, pr, tc, zero, 1) for off in poffs]
    out_specs = [_cspec(mode, 0, ts, tc, main, n_s) for _ in outs]
    oshape = (s, n_cb * tc) if mode == 'col' else (n_cb * s, tc)
    out_shape = [jax.ShapeDtypeStruct(oshape, dt) for dt in outs]
    return pl.pallas_call(
        body, grid=(n_cb, n_s), in_specs=in_specs, out_specs=out_specs, out_shape=out_shape,
        scratch_shapes=[pltpu.VMEM((hb + ts, tc), F32) for _ in range(ncv)]
        + [pltpu.VMEM((hb + rc, tc), F32) for _ in range(ncv)]
        + [pltpu.VMEM(_stage_shape(k_taps, rc, tc), F32) for _ in range(ncv)],
        compiler_params=pltpu.CompilerParams(dimension_semantics=("parallel", "arbitrary")), name=name,
    )(*[a for a, _ in xins], *[a for a, _ in xins], *([w] * ncv), *params)


def _conv_bwd(xins, w, woffs, params, poffs, pre, post, douts, dx_dtype, *, mode, s, k_taps, hb, ts, tc, rc, n_cb, name):
    n_s, nx, ncv, npar, ndo = s // ts, len(xins), len(woffs), len(params), len(douts)
    rpb = ts // hb
    n_hb = s // hb
    kp = TAP_ROWS * ((k_taps + TAP_ROWS - 1) // TAP_ROWS)
    pr = 1 if mode == 'col' else TAP_ROWS

    def body(*refs):
        pos = 0

        def take(n):
            nonlocal pos
            out = refs[pos:pos + n]
            pos += n
            return out

        mains, prevs, nexts = take(nx), take(nx), take(nx)
        d_mains, d_nexts = take(ndo), take(ndo)
        w_refs, p_refs = take(ncv), take(npar)
        dx_refs, dw_refs, dp_refs = take(nx), take(ncv), take(npar)
        u_s, d_s = take(ncv), take(ndo)
        win_s, dcw_s, dwa_s = take(ncv), take(ncv), take(ncv)
        stu_s, std_s = take(ncv), take(ncv)
        i = pl.program_id(1)
        um = pre(*[r[...].astype(F32) for r in mains])
        up = pre(*[r[...].astype(F32) for r in prevs])
        un = pre(*[r[...].astype(F32) for r in nexts])
        first = (i > 0).astype(F32)
        last = (i < n_s - 1).astype(F32)
        for q in range(ncv):
            u_s[q][pl.ds(0, hb), :] = up[q] * first
            u_s[q][pl.ds(hb, ts), :] = um[q]
            u_s[q][pl.ds(hb + ts, hb), :] = un[q]
            dwa_s[q][...] = jnp.zeros((kp * 8, tc), F32)
        for q in range(ndo):
            d_s[q][pl.ds(0, ts), :] = d_mains[q][...].astype(F32)
            d_s[q][pl.ds(ts, hb), :] = d_nexts[q][...].astype(F32) * last
        pv = [r[0:1, :] for r in p_refs]
        fwd_offsets = [hb - (k_taps - 1) + t for t in range(k_taps)]

        def chunk(ci, dpar):
            r0 = pl.multiple_of(ci * rc, rc)
            c_own, c_next, shifted = [], [], []
            for q in range(ncv):
                win_s[q][...] = u_s[q][pl.ds(r0, rc + 2 * hb), :]
                acc = jnp.zeros((rc + hb, tc), F32)
                taps = _tap_windows(win_s[q], fwd_offsets, rc + hb, stu_s[q])
                for t, ut in enumerate(taps):
                    acc = acc + w_refs[q][pl.ds(t, 1), :] * ut()
                c_own.append(acc[:rc])
                c_next.append(acc[rc:])
                shifted.append(taps)
            _, vjp_o = jax.vjp(lambda c, p: tuple(post(c, p)), c_own, pv)
            dc_own, dp_own = vjp_o(tuple(r[pl.ds(r0, rc), :] for r in d_s))
            _, vjp_n = jax.vjp(lambda c: tuple(post(c, pv)), c_next)
            (dc_next,) = vjp_n(tuple(r[pl.ds(r0 + rc, hb), :] for r in d_s))
            dus = []
            for q in range(ncv):
                dcw_s[q][pl.ds(0, rc), :] = dc_own[q]
                dcw_s[q][pl.ds(rc, hb), :] = dc_next[q]
                acc = jnp.zeros((rc, tc), F32)
                bwd_offsets = [k_taps - 1 - t for t in range(k_taps)]
                for t, dct in enumerate(_tap_windows(dcw_s[q], bwd_offsets, rc, std_s[q])):
                    acc = acc + w_refs[q][pl.ds(t, 1), :] * dct()
                    prod = dc_own[q] * shifted[q][t]()[:rc]
                    dwa_s[q][pl.ds(8 * t, 8), :] += jnp.sum(prod.reshape(rc // 8, 8, tc), axis=0)
                dus.append(acc)
            xm = [r[pl.ds(r0, rc), :].astype(F32) for r in mains]
            _, vjp_p = jax.vjp(lambda *xv: tuple(pre(*xv)), *xm)
            dxs = vjp_p(tuple(dus))
            for q in range(nx):
                dx_refs[q][pl.ds(r0, rc), :] = dxs[q].astype(dx_refs[q].dtype)
            return tuple(a + b for a, b in zip(dpar, dp_own))

        dpar = lax.fori_loop(0, ts // rc, chunk, tuple(jnp.zeros((1, tc), F32) for _ in range(npar)))
        for q in range(ncv):
            @pl.when(i == 0)
            def _(q=q):
                dw_refs[q][...] = jnp.zeros((kp, tc), F32)

            for t in range(k_taps):
                dw_refs[q][pl.ds(t, 1), :] += jnp.sum(dwa_s[q][pl.ds(8 * t, 8), :], axis=0, keepdims=True)
        for q in range(npar):
            @pl.when(i == 0)
            def _(q=q):
                dp_refs[q][...] = jnp.zeros((pr, tc), F32)

            dp_refs[q][0:1, :] += dpar[q]

    main = lambda i: i
    prev = lambda i: jnp.maximum(i * rpb - 1, 0)
    nxt = lambda i: jnp.minimum((i + 1) * rpb, n_hb - 1)
    zero = lambda i: 0
    in_specs = [_cspec(mode, off, ts, tc, main, n_s) for _, off in xins]
    in_specs += [_cspec(mode, off, hb, tc, prev, n_hb) for _, off in xins]
    in_specs += [_cspec(mode, off, hb, tc, nxt, n_hb) for _, off in xins]
    in_specs += [_cspec(mode, off, ts, tc, main, n_s) for _, off in douts]
    in_specs += [_cspec(mode, off, hb, tc, nxt, n_hb) for _, off in douts]
    in_specs += [_cspec(mode, off, kp, tc, zero, 1) for off in woffs]
    in_specs += [_cspec(mode, off, pr, tc, zero, 1) for off in poffs]
    out_specs = [_cspec(mode, 0, ts, tc, main, n_s) for _ in xins]
    out_specs += [_cspec(mode, 0, kp, tc, zero, 1) for _ in woffs]
    out_specs += [_cspec(mode, 0, pr, tc, zero, 1) for _ in params]

    def shape(rows):
        return (rows, n_cb * tc) if mode == 'col' else (n_cb * rows, tc)

    out_shape = [jax.ShapeDtypeStruct(shape(s), dx_dtype) for _ in xins]
    out_shape += [jax.ShapeDtypeStruct(shape(kp), F32) for _ in woffs]
    out_shape += [jax.ShapeDtypeStruct(shape(pr), F32) for _ in params]
    xa = [a for a, _ in xins]
    da = [a for a, _ in douts]
    return pl.pallas_call(
        body, grid=(n_cb, n_s), in_specs=in_specs, out_specs=out_specs, out_shape=out_shape,
        scratch_shapes=[pltpu.VMEM((hb + ts + hb, tc), F32) for _ in range(ncv)]
        + [pltpu.VMEM((ts + hb, tc), F32) for _ in range(ndo)]
        + [pltpu.VMEM((rc + 2 * hb, tc), F32) for _ in range(ncv)]
        + [pltpu.VMEM((rc + hb, tc), F32) for _ in range(ncv)]
        + [pltpu.VMEM((kp * 8, tc), F32) for _ in range(ncv)]
        + [pltpu.VMEM(_stage_shape(k_taps, rc + hb, tc), F32) for _ in range(ncv)]
        + [pltpu.VMEM(_stage_shape(k_taps, rc, tc), F32) for _ in range(ncv)],
        compiler_params=pltpu.CompilerParams(dimension_semantics=("parallel", "arbitrary")), name=name,
    )(*xa, *xa, *xa, *da, *da, *([w] * ncv), *params)


def _glu_pre(a, g):
    return [a * jax.nn.sigmoid(g)]


def _ln_silu_post(cs, ps):
    c = cs[0] + ps[0]
    mu = jnp.mean(c, axis=-1, keepdims=True)
    var = jnp.mean(jnp.square(c - mu), axis=-1, keepdims=True)
    y = (c - mu) * lax.rsqrt(var + LN_EPS) * ps[1] + ps[2]
    return [jax.nn.silu(y)]


def _pair_pre(a, b):
    return [a, b]


def _gate_post(cs, ps):
    return [jax.nn.silu(cs[0] + ps[0]) * (cs[1] + ps[1])]


def _conva(s):
    return dict(mode='col', s=s, k_taps=CONV_K, hb=32, ts=512, tc=CONV_WIDTH, rc=64, n_cb=1)


def _gate(s):
    return dict(mode='row', s=s, k_taps=FFN_K, hb=8, ts=512, tc=FF_SHARD, rc=32, n_cb=FF_HALF)


def _rope_tables(s):
    half = QK_ROPE // 2
    inv = ROPE_BASE ** (-jnp.arange(half, dtype=F32) / half)
    ang = jnp.arange(s, dtype=F32)[:, None] * inv[None, :]
    cos, sin = jnp.cos(ang), jnp.sin(ang)
    z = lambda n: jnp.zeros((s, n), F32)
    c = jnp.concatenate([jnp.ones((s, QK_NOPE), F32), cos, cos, z(HEAD_PAD - QK_NOPE - QK_ROPE)], axis=1)
    s1 = jnp.concatenate([z(QK_NOPE), -sin, z(HEAD_PAD - QK_NOPE - half)], axis=1)
    s2 = jnp.concatenate([z(QK_NOPE + half), sin, z(HEAD_PAD - QK_NOPE - QK_ROPE)], axis=1)
    return c, s1, s2


def _rot(t, c, s1, s2):
    half = QK_ROPE // 2
    return t * c + pltpu.roll(t, HEAD_PAD - half, 1) * s1 + pltpu.roll(t, half, 1) * s2


def _rot_t(d, c, s1, s2):
    half = QK_ROPE // 2
    return d * c + pltpu.roll(d * s1, half, 1) + pltpu.roll(d * s2, HEAD_PAD - half, 1)


def _heads(v):
    return [v[:, h * HEAD_PAD:(h + 1) * HEAD_PAD] for h in range(N_HEADS)]


def _rope_fwd(qf, kvf, hmat, tabs, name):
    w = N_HEADS * HEAD_PAD

    def fn(q, k, v, kr, c, s1, s2):
        krr = _rot(kr, c, s1, s2)
        qo = jnp.concatenate([_rot(t, c, s1, s2) for t in _heads(q)], axis=1)
        ko = jnp.concatenate([t + krr for t in _heads(k)], axis=1)
        lane = lax.broadcasted_iota(jnp.int32, v.shape, 1) & (HEAD_PAD - 1)
        return qo, ko, jnp.where(lane == ONES_LANE, 1.0, v)

    rows = [(qf, w, 0), (kvf, w, 0), (kvf, w, 1), (hmat, HEAD_PAD, IN_PAD // HEAD_PAD - 1)]
    rows += [(t, HEAD_PAD, 0) for t in tabs]
    return _rowwise(fn, rows, [], [(w, BF16)] * 3, [], ts=512, name=name)


def _rope_bwd(dq, dk, dv, tabs, name):
    w = N_HEADS * HEAD_PAD

    def fn(dqv, dkv, dvv, c, s1, s2):
        dqo = jnp.concatenate([_rot_t(t, c, s1, s2) for t in _heads(dqv)], axis=1)
        ksum = functools.reduce(lambda a, b: a + b, _heads(dkv))
        return dqo, jnp.concatenate([dkv, dvv], axis=1), _rot_t(ksum, c, s1, s2)

    rows = [(dq, w, 0), (dk, w, 0), (dv, w, 0)] + [(t, HEAD_PAD, 0) for t in tabs]
    return _rowwise(fn, rows, [], [(w, BF16), (2 * w, BF16), (HEAD_PAD, F32)], [], ts=512, name=name)


ATT_Q = 1024
ATT_SUB = 512
ATT_KV = 1024
ATT_SCALE = (QK_NOPE + QK_ROPE) ** -0.5
LOG2E = 1.4426950408889634
ATT_C2 = ATT_SCALE * LOG2E
ONES_LANE = V_DIM


def _nt(a, b):
    return lax.dot_general(a, b, (((1,), (1,)), ((), ())), preferred_element_type=F32)


def _lanes(x, w):
    return x if w == HEAD_PAD else jnp.tile(x, (1, w // HEAD_PAD))


def _tri(w, transposed):
    r = lax.broadcasted_iota(jnp.int32, (w, w), 0)
    c = lax.broadcasted_iota(jnp.int32, (w, w), 1)
    return (r <= c) if transposed else (c <= r)


def _attn_fwd(q, k, v, name):
    s = q.shape[0]
    tq, kvc = min(ATT_Q, s), min(ATT_KV, s)
    nsub, per = tq // ATT_SUB, tq // kvc

    def body(q_ref, k_ref, v_ref, o_ref, lse_ref, m_s, acc_s):
        i = pl.program_id(1)
        m_s[...] = jnp.full((tq, HEAD_PAD), -jnp.inf, F32)
        acc_s[...] = jnp.zeros((tq, HEAD_PAD), F32)

        def update(r0, n, kb, vb, diag):
            rows = pl.ds(r0, n)
            w = kb.shape[0]
            sc = _nt(q_ref[rows, :], kb)
            if diag:
                sc = jnp.where(_tri(w, False), sc, -jnp.inf)
            m_prev = m_s[rows, :]
            m_next = jnp.maximum(m_prev, jnp.max(sc, axis=1, keepdims=True))
            p = jnp.exp2((sc - _lanes(m_next, w)) * ATT_C2)
            alpha = jnp.exp2((m_prev - m_next) * ATT_C2)
            acc_s[rows, :] = alpha * acc_s[rows, :] + jnp.dot(p.astype(BF16), vb, preferred_element_type=F32)
            m_s[rows, :] = m_next

        def below(j, carry):
            at = pl.ds(pl.multiple_of(j * kvc, kvc), kvc)
            update(0, tq, k_ref[at, :], v_ref[at, :], False)
            return carry

        lax.fori_loop(0, i * per, below, 0)
        for r in range(nsub):
            for c in range(r + 1):
                at = pl.ds(pl.multiple_of(i * tq + c * ATT_SUB, ATT_SUB), ATT_SUB)
                update(r * ATT_SUB, ATT_SUB, k_ref[at, :], v_ref[at, :], c == r)
        l = acc_s[:, ONES_LANE:ONES_LANE + 1]
        o_ref[...] = (acc_s[...] / l).astype(BF16)
        lse_ref[...] = m_s[...] * ATT_SCALE + jnp.log(l)

    q_spec = pl.BlockSpec((tq, HEAD_PAD), lambda h, i: (i, h))
    kv_spec = pl.BlockSpec((s, HEAD_PAD), lambda h, i: (0, h))
    return pl.pallas_call(
        body, grid=(N_HEADS, s // tq), in_specs=[q_spec, kv_spec, kv_spec], out_specs=[q_spec, q_spec],
        out_shape=[jax.ShapeDtypeStruct(q.shape, BF16), jax.ShapeDtypeStruct(q.shape, F32)],
        scratch_shapes=[pltpu.VMEM((tq, HEAD_PAD), F32)] * 2,
        compiler_params=pltpu.CompilerParams(dimension_semantics=("parallel", "arbitrary")), name=name,
    )(q, k, v)


def _attn_bwd_q(q, k, v, cat, dcat, lse, name):
    s = q.shape[0]
    tq, kvc = min(ATT_Q, s), min(ATT_KV, s)
    nsub, per = tq // ATT_SUB, tq // kvc
    ob = CONV_WIDTH // HEAD_PAD

    def body(q_ref, k_ref, v_ref, o_ref, do_ref, lse_ref, dq_ref, lser_ref, dltr_ref, dq_s, dl_s, l2_s):
        i = pl.program_id(1)
        dl_s[...] = jnp.broadcast_to(jnp.sum(do_ref[...] * o_ref[...].astype(F32), axis=1, keepdims=True),
                                     (tq, HEAD_PAD))
        l2_s[...] = lse_ref[...] * LOG2E
        dq_s[...] = jnp.zeros((tq, HEAD_PAD), F32)

        def update(r0, n, kb, vb, diag):
            rows = pl.ds(r0, n)
            w = kb.shape[0]
            sc = _nt(q_ref[rows, :], kb)
            if diag:
                sc = jnp.where(_tri(w, False), sc, -jnp.inf)
            p = jnp.exp2(sc * ATT_C2 - _lanes(l2_s[rows, :], w))
            dp = _nt(do_ref[rows, :].astype(BF16), vb)
            ds = p * (dp - _lanes(dl_s[rows, :], w))
            dq_s[rows, :] += jnp.dot(ds.astype(BF16), kb, preferred_element_type=F32)

        def below(j, carry):
            at = pl.ds(pl.multiple_of(j * kvc, kvc), kvc)
            update(0, tq, k_ref[at, :], v_ref[at, :], False)
            return carry

        lax.fori_loop(0, i * per, below, 0)
        for r in range(nsub):
            for c in range(r + 1):
                at = pl.ds(pl.multiple_of(i * tq + c * ATT_SUB, ATT_SUB), ATT_SUB)
                update(r * ATT_SUB, ATT_SUB, k_ref[at, :], v_ref[at, :], c == r)
        dq_ref[...] = dq_s[...] * ATT_SCALE
        for c in range(per):
            at = pl.ds(c * kvc, kvc)
            lser_ref[c] = jnp.transpose(l2_s[at, :])[0:8, :]
            dltr_ref[c] = jnp.transpose(dl_s[at, :])[0:8, :]

    q_spec = pl.BlockSpec((tq, HEAD_PAD), lambda h, i: (i, h))
    kv_spec = pl.BlockSpec((s, HEAD_PAD), lambda h, i: (0, h))
    o_spec = pl.BlockSpec((tq, HEAD_PAD), lambda h, i: (i, ob + h))
    row_spec = pl.BlockSpec((None, per, 8, kvc), lambda h, i: (h, i, 0, 0))
    rows = jax.ShapeDtypeStruct((N_HEADS, s // kvc, 8, kvc), F32)
    return pl.pallas_call(
        body, grid=(N_HEADS, s // tq), in_specs=[q_spec, kv_spec, kv_spec, o_spec, o_spec, q_spec],
        out_specs=[q_spec, row_spec, row_spec], out_shape=[jax.ShapeDtypeStruct(q.shape, F32), rows, rows],
        scratch_shapes=[pltpu.VMEM((tq, HEAD_PAD), F32)] * 3,
        compiler_params=pltpu.CompilerParams(dimension_semantics=("parallel", "arbitrary")), name=name,
    )(q, k, v, cat, dcat, lse)


def _attn_bwd_kv(q, k, v, dcat, lse_r, dlt_r, name):
    s = q.shape[0]
    tk, kvc = min(ATT_Q, s), min(ATT_KV, s)
    nsub, per, n_chunks = tk // ATT_SUB, tk // kvc, s // kvc
    ob = CONV_WIDTH // HEAD_PAD

    def body(k_ref, v_ref, q_ref, do_ref, lse_ref, dl_ref, dk_ref, dv_ref, dk_s, dv_s):
        j = pl.program_id(1)
        dk_s[...] = jnp.zeros((tk, HEAD_PAD), F32)
        dv_s[...] = jnp.zeros((tk, HEAD_PAD), F32)

        def update(r0, n, qb, dob, lrow, drow, diag):
            rows = pl.ds(r0, n)
            sc = _nt(k_ref[rows, :], qb)
            if diag:
                sc = jnp.where(_tri(qb.shape[0], True), sc, -jnp.inf)
            p = jnp.exp2(sc * ATT_C2 - lrow)
            dp = _nt(v_ref[rows, :], dob)
            ds = p * (dp - drow)
            dv_s[rows, :] += jnp.dot(p.astype(BF16), dob, preferred_element_type=F32)
            dk_s[rows, :] += jnp.dot(ds.astype(BF16), qb, preferred_element_type=F32)

        def above(ic, carry):
            at = pl.ds(pl.multiple_of(ic * kvc, kvc), kvc)
            update(0, tk, q_ref[at, :], do_ref[at, :].astype(BF16), lse_ref[ic, 0:1, :], dl_ref[ic, 0:1, :], False)
            return carry

        lax.fori_loop((j + 1) * per, n_chunks, above, 0)
        for r in range(nsub):
            for c in range(r, nsub):
                at = pl.ds(pl.multiple_of(j * tk + c * ATT_SUB, ATT_SUB), ATT_SUB)
                ic = j * per + (c * ATT_SUB) // kvc
                lo = (c * ATT_SUB) % kvc
                update(r * ATT_SUB, ATT_SUB, q_ref[at, :], do_ref[at, :].astype(BF16),
                       lse_ref[ic, 0:1, lo:lo + ATT_SUB], dl_ref[ic, 0:1, lo:lo + ATT_SUB], c == r)
        dk_ref[...] = dk_s[...] * ATT_SCALE
        dv_ref[...] = dv_s[...]

    kv_spec = pl.BlockSpec((tk, HEAD_PAD), lambda h, j: (j, h))
    q_spec = pl.BlockSpec((s, HEAD_PAD), lambda h, j: (0, h))
    do_spec = pl.BlockSpec((s, HEAD_PAD), lambda h, j: (0, ob + h))
    row_spec = pl.BlockSpec((None, n_chunks, 8, kvc), lambda h, j: (h, 0, 0, 0))
    return pl.pallas_call(
        body, grid=(N_HEADS, s // tk), in_specs=[kv_spec, kv_spec, q_spec, do_spec, row_spec, row_spec],
        out_specs=[kv_spec, kv_spec],
        out_shape=[jax.ShapeDtypeStruct(q.shape, F32), jax.ShapeDtypeStruct(q.shape, F32)],
        scratch_shapes=[pltpu.VMEM((tk, HEAD_PAD), F32)] * 2,
        compiler_params=pltpu.CompilerParams(dimension_semantics=("parallel", "arbitrary")), name=name,
    )(k, v, q, dcat, lse_r, dlt_r)


SCAN_T = 256
SCAN_C = 512


def _scan(b, lam, *, reverse, xs=None, name):
    s = b.shape[0]
    t = min(SCAN_T, s)
    n_t, n_c = s // t, SSM_CH // SCAN_C
    with_dlam = xs is not None

    def shift(a, d, row):
        if d >= 8:
            z = jnp.zeros((d, SCAN_C), F32)
            return jnp.concatenate([a[d:], z], axis=0) if reverse else jnp.concatenate([z, a[:t - d]], axis=0)
        if reverse:
            return jnp.where(row < t - d, pltpu.roll(a, t - d, 0), 0.0)
        return jnp.where(row >= d, pltpu.roll(a, d, 0), 0.0)

    def body(*refs):
        if with_dlam:
            b_ref, lam_ref, x_ref, o_ref, dl_ref, c_s = refs
        else:
            b_ref, lam_ref, o_ref, c_s = refs
        k = pl.program_id(0)

        @pl.when(k == 0)
        def _():
            c_s[...] = jnp.zeros((1, 2 * SSM_CH), F32)
            if with_dlam:
                dl_ref[...] = jnp.zeros((1, 2 * SSM_CH), F32)

        row = lax.broadcasted_iota(jnp.int32, (t, SCAN_C), 0)
        edge = (row == t - 1) if reverse else (row == 0)
        for ch in range(n_c):
            re = pl.ds(ch * SCAN_C, SCAN_C)
            im = pl.ds(SSM_CH + ch * SCAN_C, SCAN_C)
            lr = lam_ref[:, re]
            li = -lam_ref[:, im] if reverse else lam_ref[:, im]
            cr, ci = c_s[:, re], c_s[:, im]
            ar = b_ref[:, re] + jnp.where(edge, lr * cr - li * ci, 0.0)
            ai = b_ref[:, im] + jnp.where(edge, lr * ci + li * cr, 0.0)
            d = 1
            while d < t:
                sr, si = shift(ar, d, row), shift(ai, d, row)
                ar, ai = ar + lr * sr - li * si, ai + lr * si + li * sr
                lr, li = lr * lr - li * li, 2.0 * lr * li
                d *= 2
            o_ref[:, re] = ar
            o_ref[:, im] = ai
            if with_dlam:
                gr = jnp.where(edge, cr, shift(ar, 1, row))
                gi = jnp.where(edge, ci, shift(ai, 1, row))
                xr, xi = x_ref[:, re], x_ref[:, im]
                dl_ref[:, re] += jnp.sum(xr * gr + xi * gi, axis=0, keepdims=True)
                dl_ref[:, im] += jnp.sum(xr * gi - xi * gr, axis=0, keepdims=True)
            last = 0 if reverse else t - 1
            c_s[:, re] = ar[last:last + 1, :]
            c_s[:, im] = ai[last:last + 1, :]

    tm = (lambda k: (n_t - 1 - k, 0)) if reverse else (lambda k: (k, 0))
    blk = pl.BlockSpec((t, 2 * SSM_CH), tm)
    vec = pl.BlockSpec((1, 2 * SSM_CH), lambda k: (0, 0))
    in_specs, args = [blk, vec], [b, lam]
    out_specs, out_shape = [blk], [jax.ShapeDtypeStruct((s, 2 * SSM_CH), F32)]
    if with_dlam:
        in_specs.append(blk)
        args.append(xs)
        out_specs.append(vec)
        out_shape.append(jax.ShapeDtypeStruct((1, 2 * SSM_CH), F32))
    return pl.pallas_call(
        body, grid=(n_t,), in_specs=in_specs, out_specs=out_specs, out_shape=out_shape,
        scratch_shapes=[pltpu.VMEM((1, 2 * SSM_CH), F32)],
        compiler_params=pltpu.CompilerParams(dimension_semantics=("arbitrary",)), name=name,
    )(*args)


def _s5_disc(log_dt, a_re, a_im, b_re, b_im):
    dt = jnp.exp(log_dt)[:, None]
    mag = jnp.exp(a_re * dt)
    lb_re, lb_im = mag * jnp.cos(a_im * dt), mag * jnp.sin(a_im * dt)
    den = a_re * a_re + a_im * a_im
    nr, ni = lb_re - 1.0, lb_im
    f_re = (nr * a_re + ni * a_im) / den
    f_im = (ni * a_re - nr * a_im) / den
    bb_re = f_re[..., None] * b_re - f_im[..., None] * b_im
    bb_im = f_re[..., None] * b_im + f_im[..., None] * b_re
    return lb_re, lb_im, bb_re, bb_im


def _bd(a):
    g, i, j = a.shape
    eye = jnp.eye(g, dtype=a.dtype)
    return (a[:, :, None, :] * eye[:, None, :, None]).reshape(g * i, g * j)


S5_TILE_GROUPS = HEAD_PAD // SSM_GROUP


def _s5_sparse(s):
    nb = SSM_GROUPS // S5_TILE_GROUPS
    cw, sw = S5_TILE_GROUPS * SSM_GROUP, S5_TILE_GROUPS * SSM_STATE
    tm = min(1024, s)
    return dict(
        expand=dict(tiles=(tm, sw, cw), out=(s, 2 * SSM_CH), nk=1, k_of=lambda i, j, k: j % nb),
        reduce=dict(tiles=(tm, cw, sw), out=(s, SSM_WIDTH), nk=2, k_of=lambda i, j, k: j + nb * k),
        wide_t=dict(tiles=(sw, cw, tm), out=(2 * SSM_CH, cw), nk=s // tm, b_n=lambda i, j: i % nb),
        narrow_t=dict(tiles=(cw, sw, tm), out=(cw, 2 * SSM_CH), nk=s // tm, a_m=lambda i, j: j % nb))


def _exchange(arrs, modes, name):
    n = len(arrs)
    shapes = [a.shape if md == 'scatter' else (N_DEV,) + a.shape for a, md in zip(arrs, modes)]

    def body(*refs):
        srcs, outs = refs[:n], refs[n:2 * n]
        send_sems, recv_sems, local_sems = refs[2 * n:]
        x, y, c = lax.axis_index("x"), lax.axis_index("y"), lax.axis_index("c")
        me = 4 * x + 2 * y + c

        def piece(q, slot):
            return srcs[q].at[slot] if modes[q] == 'scatter' else srcs[q]

        mine = [pltpu.make_async_copy(piece(q, me), outs[q].at[me], local_sems.at[q]) for q in range(n)]
        for cp in mine:
            cp.start()
        copies = []
        for r in range(1, N_DEV):
            px, py, pc = x ^ (r >> 2), y ^ ((r >> 1) & 1), c ^ (r & 1)
            peer = 4 * px + 2 * py + pc
            for q in range(n):
                copies.append(pltpu.make_async_remote_copy(
                    src_ref=piece(q, peer), dst_ref=outs[q].at[me], send_sem=send_sems.at[(r - 1) * n + q],
                    recv_sem=recv_sems.at[(r - 1) * n + q], device_id=(px, py, pc),
                    device_id_type=pl.DeviceIdType.MESH))
        for cp in copies:
            cp.start()
        for cp in copies:
            cp.wait_recv()
        for cp in copies:
            cp.wait_send()
        for cp in mine:
            cp.wait()

    any_spec = pl.BlockSpec(memory_space=pl.ANY)
    return pl.pallas_call(
        body, out_shape=[jax.ShapeDtypeStruct(sh, a.dtype) for sh, a in zip(shapes, arrs)],
        in_specs=[any_spec] * n, out_specs=[any_spec] * n,
        scratch_shapes=[pltpu.SemaphoreType.DMA(((N_DEV - 1) * n,)), pltpu.SemaphoreType.DMA(((N_DEV - 1) * n,)),
                        pltpu.SemaphoreType.DMA((n,))],
        compiler_params=pltpu.CompilerParams(has_side_effects=True), name=name,
    )(*arrs)


def _peers(x, y, c):
    out = []
    for r in range(1, N_DEV):
        px, py, pc = x ^ (r >> 2), y ^ ((r >> 1) & 1), c ^ (r & 1)
        out.append((r, (px, py, pc), 4 * px + 2 * py + pc))
    return out


def _exchange_start(arrs, modes, carry, name):
    n = len(arrs)
    shapes = [a.shape if md == 'scatter' else (N_DEV,) + a.shape for a, md in zip(arrs, modes)]
    lands = [lax.empty(sh, a.dtype) for sh, a in zip(shapes, arrs)]

    def body(*refs):
        srcs, zones = refs[:n], refs[n:2 * n]
        send_sems, recv_sems = refs[2 * n + 1], refs[2 * n + 2]
        local_sems = refs[-1]
        x, y, c = lax.axis_index("x"), lax.axis_index("y"), lax.axis_index("c")
        me = 4 * x + 2 * y + c

        def piece(q, slot):
            return srcs[q].at[slot] if modes[q] == 'scatter' else srcs[q]

        mine = [pltpu.make_async_copy(piece(q, me), zones[q].at[me], local_sems.at[q]) for q in range(n)]
        for cp in mine:
            cp.start()
        for r, pos, peer in _peers(x, y, c):
            for q in range(n):
                pltpu.make_async_remote_copy(
                    src_ref=piece(q, peer), dst_ref=zones[q].at[me], send_sem=send_sems.at[(r - 1) * n + q],
                    recv_sem=recv_sems.at[(r - 1) * n + q], device_id=pos, device_id_type=pl.DeviceIdType.MESH).start()
        for cp in mine:
            cp.wait()

    hbm = pl.BlockSpec(memory_space=pltpu.HBM)
    sem = pl.BlockSpec(memory_space=pltpu.SEMAPHORE)
    thru = arrs + lands + [carry]
    sems = pltpu.SemaphoreType.DMA(((N_DEV - 1) * n,))
    outs = pl.pallas_call(
        body, name=name, out_shape=(sems, sems, *[pltpu.HBM(a.shape, a.dtype) for a in thru]),
        in_specs=[hbm] * len(thru), out_specs=(sem, sem, *[hbm] * len(thru)),
        input_output_aliases={q: 2 + q for q in range(len(thru))},
        scratch_shapes=[pltpu.SemaphoreType.DMA((n,))],
        compiler_params=pltpu.CompilerParams(has_side_effects=pltpu.SideEffectType.DATAFLOW_SIDE_EFFECTING),
    )(*[pltpu.with_memory_space_constraint(a, pltpu.HBM) for a in thru])
    return dict(send=outs[0], recv=outs[1], srcs=list(outs[2:2 + n]), lands=list(outs[2 + n:2 + 2 * n]),
                modes=modes), outs[-1]


def _exchange_wait(flight, after, name):
    n = len(flight['srcs'])
    modes = flight['modes']

    def body(*refs):
        srcs, zones = refs[:n], refs[n:2 * n]
        send_sems, recv_sems = refs[2 * n], refs[2 * n + 1]
        x, y, c = lax.axis_index("x"), lax.axis_index("y"), lax.axis_index("c")
        me = 4 * x + 2 * y + c
        for r, pos, peer in _peers(x, y, c):
            for q in range(n):
                src = srcs[q].at[peer] if modes[q] == 'scatter' else srcs[q]
                cp = pltpu.make_async_remote_copy(
                    src_ref=src, dst_ref=zones[q].at[me], send_sem=send_sems.at[(r - 1) * n + q],
                    recv_sem=recv_sems.at[(r - 1) * n + q], device_id=pos, device_id_type=pl.DeviceIdType.MESH)
                cp.wait_send()
                cp.wait_recv()

    hbm = pl.BlockSpec(memory_space=pltpu.HBM)
    sem = pl.BlockSpec(memory_space=pltpu.SEMAPHORE)
    bufs = flight['srcs'] + flight['lands']
    outs = pl.pallas_call(
        body, name=name, out_shape=tuple(pltpu.HBM(a.shape, a.dtype) for a in bufs),
        in_specs=[hbm] * (2 * n) + [sem, sem, pl.BlockSpec(memory_space=pl.ANY)], out_specs=tuple([hbm] * (2 * n)),
        input_output_aliases={q: q for q in range(2 * n)},
        compiler_params=pltpu.CompilerParams(has_side_effects=pltpu.SideEffectType.DATAFLOW_SIDE_EFFECTING),
    )(*bufs, flight['send'], flight['recv'], after)
    return list(outs[n:])


def _adamw(parts, w, m, v, name):
    r, c = w.shape
    tr = _tile(r, (256, 128))

    def body(p_ref, w_ref, m_ref, v_ref, g_ref, d_ref, nm_ref, nv_ref):
        g = p_ref[0].astype(F32)
        for d in range(1, N_DEV):
            g = g + p_ref[d].astype(F32)
        m2 = ADAM_B1 * m_ref[...] + (1.0 - ADAM_B1) * g
        v2 = ADAM_B2 * v_ref[...] + (1.0 - ADAM_B2) * jnp.square(g)
        m_hat = m2 / (1.0 - ADAM_B1 ** ADAM_STEP)
        v_hat = v2 / (1.0 - ADAM_B2 ** ADAM_STEP)
        g_ref[...] = g
        d_ref[...] = -ADAM_LR * (m_hat / (jnp.sqrt(v_hat) + ADAM_EPS) + ADAM_WD * w_ref[...])
        nm_ref[...] = m2
        nv_ref[...] = v2

    spec = pl.BlockSpec((tr, c), lambda i: (i, 0))
    return pl.pallas_call(
        body, grid=(r // tr,), in_specs=[pl.BlockSpec((N_DEV, tr, c), lambda i: (0, i, 0)), spec, spec, spec],
        out_specs=[spec] * 4, out_shape=[jax.ShapeDtypeStruct((r, c), F32)] * 4,
        compiler_params=pltpu.CompilerParams(dimension_semantics=("parallel",)), name=name,
    )(parts, w, m, v)


FLAT_W = 512
FLAT_ROWS = 256


def _flat(arrs):
    v = jnp.concatenate([a.reshape(-1) for a in arrs])
    return jnp.pad(v, (0, (-v.shape[0]) % (FLAT_ROWS * FLAT_W))).reshape(-1, FLAT_W)


def _unflat(flat, shapes):
    v = flat.reshape(-1)
    out, off = [], 0
    for sh in shapes:
        n = 1
        for d in sh:
            n *= d
        out.append(v[off:off + n].reshape(sh))
        off += n
    return out


def _full(name, stacked):
    if SHARDED[name] == 0:
        return stacked.reshape((-1,) + stacked.shape[2:])
    return jnp.transpose(stacked, (1, 0, 2)).reshape(stacked.shape[1], -1)


def _shards(name, full):
    if SHARDED[name] == 0:
        return full.reshape((N_DEV, -1) + full.shape[1:])
    r, c = full.shape
    return jnp.transpose(full.reshape(r, N_DEV, c // N_DEV), (1, 0, 2))


def _prep_weights(p):
    q = {}
    w_in = p['l0_w_in']
    z = lambda n: jnp.zeros((D_MODEL, n), w_in.dtype)
    q['w_in'] = jnp.concatenate([w_in[:, :IN_EVEN - QK_ROPE], z(KR_LANE), w_in[:, IN_EVEN - QK_ROPE:],
                                 z(HEAD_PAD - KR_LANE - QK_ROPE)], axis=1)
    dqk = QK_NOPE + QK_ROPE
    q['w_uq'] = jnp.pad(p['l0_w_uq'].reshape(Q_RANK, N_HEADS, dqk), ((0, 0), (0, 0), (0, HEAD_PAD - dqk))
                        ).reshape(Q_RANK, N_HEADS * HEAD_PAD)
    ukv = p['l0_w_ukv'].reshape(KV_RANK, N_HEADS, 2, QK_NOPE)
    padh = lambda a: jnp.pad(a, ((0, 0), (0, 0), (0, HEAD_PAD - QK_NOPE))).reshape(KV_RANK, N_HEADS * HEAD_PAD)
    q['w_ukv'] = jnp.concatenate([padh(ukv[:, :, 0]), padh(ukv[:, :, 1])], axis=1)
    wo = p['l0_w_out']
    wo_a = jnp.pad(wo[CONV_WIDTH:].reshape(N_HEADS, V_DIM, D_MODEL), ((0, 0), (0, HEAD_PAD - V_DIM), (0, 0)))
    q['w_out'] = jnp.concatenate([wo[:CONV_WIDTH], wo_a.reshape(N_HEADS * HEAD_PAD, D_MODEL)], axis=0)
    return q


def _unprep_grads(g):
    out = {}
    d = g['w_in']
    out['l0_w_in'] = jnp.concatenate([d[:, :IN_EVEN - QK_ROPE],
                                      d[:, IN_EVEN - QK_ROPE + KR_LANE:IN_EVEN + KR_LANE]], axis=1)
    dqk = QK_NOPE + QK_ROPE
    out['l0_w_uq'] = g['w_uq'].reshape(Q_RANK, N_HEADS, HEAD_PAD)[:, :, :dqk].reshape(Q_RANK, N_HEADS * dqk)
    d = g['w_ukv'].reshape(KV_RANK, 2, N_HEADS, HEAD_PAD)[:, :, :, :QK_NOPE]
    out['l0_w_ukv'] = jnp.transpose(d, (0, 2, 1, 3)).reshape(KV_RANK, N_HEADS * 2 * QK_NOPE)
    d = g['w_out']
    da = d[CONV_WIDTH:].reshape(N_HEADS, HEAD_PAD, D_MODEL)[:, :V_DIM].reshape(N_HEADS * V_DIM, D_MODEL)
    out['l0_w_out'] = jnp.concatenate([d[:CONV_WIDTH], da], axis=0)
    return out


def _pad_rows(w, rows):
    return jnp.pad(w, [(0, 0)] * (w.ndim - 2) + [(0, rows - w.shape[-2]), (0, 0)])


def _ffn_fwd(x, rep, got, pre, tag):
    s = x.shape[0]
    xn = _rms_fwd(x, rep[pre + 'ffn_norm'], f"{tag}_ffn_norm")
    w_up = got[pre + 'w_up']
    hu = _mm(xn, w_up, gb=_same, go=_same, groups=N_DEV, name=f"{tag}_ffn_up").reshape(N_DEV * s, FF_SHARD)
    taps = _pad_rows(got[pre + 'ffn_conv_w'], TAP_ROWS).reshape(N_DEV * TAP_ROWS, FF_SHARD)
    bias = _pad_rows(rep[pre + 'ffn_conv_b'].reshape(N_DEV, 1, FF_SHARD), TAP_ROWS).reshape(N_DEV * TAP_ROWS, FF_SHARD)
    (act,) = _conv_fwd([(hu, 0), (hu, FF_HALF)], taps, [0, FF_HALF], [bias, bias], [0, FF_HALF], _pair_pre,
                       _gate_post, [BF16], name=f"{tag}_ffn_gate", **_gate(s))
    act = act.reshape(FF_HALF, s, FF_SHARD)
    w_down = got[pre + 'w_down'].reshape(FF_HALF, FF_SHARD, D_MODEL)
    y = _mm(act, w_down, ga=_same, gb=_same, groups=FF_HALF, res=x, name=f"{tag}_ffn_down")
    return y, (x, xn, hu, act, taps, bias, w_up, w_down)


def _ffn_bwd(dy, saved, rep, pre, tag, grads, gsh):
    x, xn, hu, act, taps, bias, w_up, w_down = saved
    s = x.shape[0]
    dact = _mm(dy, w_down, tb=True, gb=_same, go=_same, groups=FF_HALF, name=f"{tag}_ffn_down_dx")
    gsh[pre + 'w_down'] = _mm(act, dy, ta=True, ga=_same, go=_same, groups=FF_HALF, out_dtype=BF16,
                              name=f"{tag}_ffn_down_dw").reshape(N_DEV, FF_SHARD // 2, D_MODEL)
    dha, dhb, dwa, dwb, dba, dbb = _conv_bwd(
        [(hu, 0), (hu, FF_HALF)], taps, [0, FF_HALF], [bias, bias], [0, FF_HALF], _pair_pre, _gate_post,
        [(dact.reshape(FF_HALF * s, FF_SHARD), 0)], BF16, name=f"{tag}_ffn_gate_bwd", **_gate(s))
    dha, dhb = dha.reshape(FF_HALF, s, FF_SHARD), dhb.reshape(FF_HALF, s, FF_SHARD)
    dtaps = jnp.concatenate([dwa, dwb], axis=0).reshape(N_DEV, TAP_ROWS, FF_SHARD)
    gsh[pre + 'ffn_conv_w'] = dtaps[:, :FFN_K].astype(BF16)
    grads[pre + 'ffn_conv_b'] = jnp.concatenate([dba, dbb], axis=0).reshape(N_DEV, TAP_ROWS, FF_SHARD)[:, 0].reshape(-1)
    upper = lambda g: g + FF_HALF
    dxn = _mm(dha, w_up, tb=True, ga=_same, gb=_same, groups=FF_HALF, name=f"{tag}_ffn_up_dx_a")
    dxn = _mm(dhb, w_up, tb=True, ga=_same, gb=upper, groups=FF_HALF, res=dxn, name=f"{tag}_ffn_up_dx_b")
    dwu = [_mm(xn, dh, ta=True, gb=_same, go=_same, groups=FF_HALF, out_dtype=BF16, name=f"{tag}_ffn_up_dw_{t}")
           for t, dh in (("a", dha), ("b", dhb))]
    gsh[pre + 'w_up'] = jnp.concatenate(dwu, axis=0)
    dx, dg = _rms_bwd(x, rep[pre + 'ffn_norm'], dxn, dy, f"{tag}_ffn_norm_bwd")
    grads[pre + 'ffn_norm'] = dg.reshape(-1)
    return dx


def _mla_fwd(x, rep, taps, q, tabs):
    s = x.shape[0]
    xn = _rms_fwd(x, rep['l0_mix_norm'], "l0_mix_norm")
    hmat = _mm(xn, q['w_in'], name="l0_in")
    wt = _pad_rows(taps, 4 * TAP_ROWS)
    cpar = [rep['l0_conv_b'].reshape(1, -1), rep['l0_conv_ln_g'].reshape(1, -1), rep['l0_conv_ln_b'].reshape(1, -1)]
    (u,) = _conv_fwd([(hmat, 0), (hmat, 1)], wt, [0], cpar, [0, 0, 0], _glu_pre, _ln_silu_post, [BF16],
                     name="l0_conv", **_conva(s))
    qn, kvn = rep['l0_q_norm'].reshape(1, -1), rep['l0_kv_norm'].reshape(1, -1)
    cqn, ckvn = _rowwise(lambda a, b, ga, gb: (_rms(a, ga), _rms(b, gb)),
                         [(hmat, Q_RANK, 2 * CONV_WIDTH // Q_RANK), (hmat, KV_RANK, (2 * CONV_WIDTH + Q_RANK) // KV_RANK)],
                         [qn, kvn], [(Q_RANK, BF16), (KV_RANK, BF16)], [], ts=512, name="l0_latent_norm")
    qf = _mm(cqn, q['w_uq'], name="l0_uq")
    kvf = _mm(ckvn, q['w_ukv'], name="l0_ukv")
    q_rot, k_full, v = _rope_fwd(qf, kvf, hmat, tabs, "l0_rope")
    o, lse = _attn_fwd(q_rot, k_full, v, "l0_attn")
    cat = jnp.concatenate([u, o], axis=1)
    y = _mm(cat, q['w_out'], res=x, name="l0_out")
    return y, (x, xn, hmat, wt, cpar, qn, kvn, cqn, ckvn, q_rot, k_full, v, lse, cat)


def _mla_bwd(dy, saved, rep, q, tabs, grads, gq):
    x, xn, hmat, wt, cpar, qn, kvn, cqn, ckvn, q_rot, k_full, v, lse, cat = saved
    s = x.shape[0]
    dcat = _mm(dy, q['w_out'], tb=True, name="l0_out_dx")
    gq['w_out'] = _mm(cat, dy, ta=True, name="l0_out_dw")
    dq, lse_r, dlt_r = _attn_bwd_q(q_rot, k_full, v, cat, dcat, lse, "l0_attn_bwd_q")
    dk, dv = _attn_bwd_kv(q_rot, k_full, v, dcat, lse_r, dlt_r, "l0_attn_bwd_kv")
    dqf, dkvf, dkr = _rope_bwd(dq, dk, dv, tabs, "l0_rope_bwd")
    dcqn = _mm(dqf, q['w_uq'], tb=True, name="l0_uq_dx")
    gq['w_uq'] = _mm(cqn, dqf, ta=True, name="l0_uq_dw")
    dckvn = _mm(dkvf, q['w_ukv'], tb=True, name="l0_ukv_dx")
    gq['w_ukv'] = _mm(ckvn, dkvf, ta=True, name="l0_ukv_dw")

    def lat_bwd(a, b, da, db, ga, gb):
        _, vjp = jax.vjp(lambda a_, b_, ga_, gb_: (_rms(a_, ga_), _rms(b_, gb_)), a, b, ga, gb)
        return vjp((da, db))

    dcq, dckv, dqn, dkvn = _rowwise(
        lat_bwd, [(hmat, Q_RANK, 2 * CONV_WIDTH // Q_RANK), (hmat, KV_RANK, (2 * CONV_WIDTH + Q_RANK) // KV_RANK),
                  (dcqn, Q_RANK, 0), (dckvn, KV_RANK, 0)],
        [qn, kvn], [(Q_RANK, F32), (KV_RANK, F32)], [(1, Q_RANK), (1, KV_RANK)], ts=512, name="l0_latent_norm_bwd")
    grads['l0_q_norm'], grads['l0_kv_norm'] = dqn.reshape(-1), dkvn.reshape(-1)
    da, dg, dwt, dcb, dlg, dlb = _conv_bwd(
        [(hmat, 0), (hmat, 1)], wt, [0], cpar, [0, 0, 0], _glu_pre, _ln_silu_post, [(dcat, 0)], F32,
        name="l0_conv_bwd", **_conva(s))
    gq['conv_w'] = dwt[:CONV_K]
    grads['l0_conv_b'], grads['l0_conv_ln_g'], grads['l0_conv_ln_b'] = dcb.reshape(-1), dlg.reshape(-1), dlb.reshape(-1)
    dh = jnp.concatenate([da, dg, dcq, dckv, dkr], axis=1)
    dxn = _mm(dh, q['w_in'], tb=True, name="l0_in_dx")
    gq['w_in'] = _mm(xn, dh, ta=True, name="l0_in_dw")
    dx, dgn = _rms_bwd(x, rep['l0_mix_norm'], dxn, dy, "l0_mix_norm_bwd")
    grads['l0_mix_norm'] = dgn.reshape(-1)
    return dx


def _gelu_skip(yc, u, d):
    return jax.nn.gelu(yc + d * u)


def _glu_out(z1, z2, b1, b2, x):
    return x + (z1 + b1) * jax.nn.sigmoid(z2 + b2)


def _s5_fwd(x, rep, w_in, w_glu):
    xn = _rms_fwd(x, rep['l1_mix_norm'], "l1_mix_norm")
    u = _mm(xn, w_in, name="l1_in")
    lb_re, lb_im, bb_re, bb_im = _s5_disc(rep['l1_log_dt'], rep['l1_a_re'], rep['l1_a_im'], rep['l1_b_re'],
                                          rep['l1_b_im'])
    lam = jnp.concatenate([lb_re.reshape(1, -1), lb_im.reshape(1, -1)], axis=1)
    tr = lambda a: jnp.transpose(a, (0, 2, 1))
    bmat = jnp.concatenate([_bd(tr(bb_re)), _bd(tr(bb_im))], axis=1)
    cmat = jnp.concatenate([_bd(tr(rep['l1_c_re'])), -_bd(tr(rep['l1_c_im']))], axis=0)
    sp = _s5_sparse(x.shape[0])
    bu = _mm(u, bmat, sparse=sp['expand'], name="l1_bu")
    (xs,) = _scan(bu, lam, reverse=False, name="l1_scan")
    yc = _mm(xs, cmat, sparse=sp['reduce'], name="l1_cx")
    dsk = rep['l1_d'].reshape(1, -1)
    (y,) = _rowwise(_gelu_skip, [(yc, SSM_WIDTH, 0), (u, SSM_WIDTH, 0)], [dsk], [(SSM_WIDTH, BF16)], [],
                    ts=512, name="l1_gelu")
    z = _mm(y, w_glu, name="l1_glu")
    bg = rep['l1_b_glu'].reshape(1, -1)
    (out,) = _rowwise(lambda z1, z2, xv, b1, b2: _glu_out(z1, z2, b1, b2, xv),
                      [(z, D_MODEL, 0), (z, D_MODEL, 1), (x, D_MODEL, 0)], [bg[:, :D_MODEL], bg[:, D_MODEL:]],
                      [(D_MODEL, F32)], [], ts=512, name="l1_glu_out")
    return out, (x, xn, u, lam, bmat, cmat, xs, yc, dsk, y, z, bg, w_in, w_glu)


def _s5_bwd(dy, saved, rep, grads, gq):
    x, xn, u, lam, bmat, cmat, xs, yc, dsk, y, z, bg, w_in, w_glu = saved

    def glu_bwd(z1, z2, dv, b1, b2):
        _, vjp = jax.vjp(lambda a, b, c, d: (a + c) * jax.nn.sigmoid(b + d), z1, z2, b1, b2)
        d1, d2, db1, db2 = vjp(dv)
        return jnp.concatenate([d1, d2], axis=1), db1, db2

    dz, db1, db2 = _rowwise(glu_bwd, [(z, D_MODEL, 0), (z, D_MODEL, 1), (dy, D_MODEL, 0)],
                            [bg[:, :D_MODEL], bg[:, D_MODEL:]], [(2 * D_MODEL, BF16)], [(1, D_MODEL), (1, D_MODEL)],
                            ts=512, name="l1_glu_out_bwd")
    grads['l1_b_glu'] = jnp.concatenate([db1, db2], axis=1).reshape(-1)
    dyv = _mm(dz, w_glu, tb=True, name="l1_glu_dx")
    gq['l1_w_glu'] = _mm(y, dz, ta=True, name="l1_glu_dw")

    def gelu_bwd(ycv, uv, dv, dk):
        _, vjp = jax.vjp(_gelu_skip, ycv, uv, dk)
        return vjp(dv)

    dyc, du_skip, dd = _rowwise(gelu_bwd, [(yc, SSM_WIDTH, 0), (u, SSM_WIDTH, 0), (dyv, SSM_WIDTH, 0)], [dsk],
                                [(SSM_WIDTH, F32), (SSM_WIDTH, F32)], [(1, SSM_WIDTH)], ts=512, name="l1_gelu_bwd")
    grads['l1_d'] = dd.reshape(-1)
    sp = _s5_sparse(x.shape[0])
    dxs = _mm(dyc, cmat, tb=True, sparse=sp['expand'], name="l1_cx_dx")
    dcm = _mm(xs, dyc, ta=True, sparse=sp['wide_t'], name="l1_cx_dw")
    gs, dlam = _scan(dxs, lam, reverse=True, xs=xs, name="l1_scan_bwd")
    dlr, dli = dlam[:, :SSM_CH], dlam[:, SSM_CH:]
    du = _mm(gs, bmat, tb=True, res=du_skip, sparse=sp['reduce'], name="l1_bu_dx")
    dbm = _mm(u, gs, ta=True, sparse=sp['narrow_t'], name="l1_bu_dw")
    eye = jnp.eye(S5_TILE_GROUPS, dtype=F32)
    nb = SSM_GROUPS // S5_TILE_GROUPS
    dcm = dcm.reshape(2, nb, S5_TILE_GROUPS, SSM_STATE, S5_TILE_GROUPS, SSM_GROUP)
    dcm = jnp.sum(dcm * eye[None, None, :, None, :, None], axis=4).reshape(2, SSM_GROUPS, SSM_STATE, SSM_GROUP)
    tr = lambda a: jnp.transpose(a, (0, 2, 1))
    grads['l1_c_re'], grads['l1_c_im'] = tr(dcm[0]), -tr(dcm[1])
    dbm = dbm.reshape(S5_TILE_GROUPS, SSM_GROUP, 2, nb, S5_TILE_GROUPS, SSM_STATE)
    dbm = jnp.sum(dbm * eye[:, None, None, None, :, None], axis=0)
    dbm = jnp.transpose(dbm, (1, 2, 3, 4, 0)).reshape(2, SSM_GROUPS, SSM_STATE, SSM_GROUP)
    dbb_re, dbb_im = dbm[0], dbm[1]
    names = ['l1_log_dt', 'l1_a_re', 'l1_a_im', 'l1_b_re', 'l1_b_im']
    _, vjp = jax.vjp(_s5_disc, *[rep[n] for n in names])
    for n, gval in zip(names, vjp((dlr.reshape(SSM_GROUPS, SSM_STATE), dli.reshape(SSM_GROUPS, SSM_STATE), dbb_re, dbb_im))):
        grads[n] = gval
    dxn = _mm(du, w_in, tb=True, name="l1_in_dx")
    gq['l1_w_in'] = _mm(xn, du, ta=True, name="l1_in_dw")
    dx, dgn = _rms_bwd(x, rep['l1_mix_norm'], dxn, dy, "l1_mix_norm_bwd")
    grads['l1_mix_norm'] = dgn.reshape(-1)
    return dx


def _loss_head(x, g, target):
    d = x.shape[1]

    def fn(xv, tv, gv):
        y, vjp = jax.vjp(_rms, xv, gv)
        err = y - tv
        part = 0.5 * jnp.sum(jnp.mean(jnp.square(err), axis=-1, keepdims=True), axis=0, keepdims=True)
        dx, dg = vjp(err * (1.0 / d))
        return dx, jnp.broadcast_to(part, (1, 128)), dg

    return _rowwise(fn, [(x, d, 0), (target, d, 0)], [g.reshape(1, -1)], [(d, F32)], [(1, 128), (1, d)], ts=512,
                    name="loss_head")


FIRST = ('l0_w_in', 'l0_conv_w', 'l0_w_uq', 'l0_w_ukv', 'l0_w_out')
REST = tuple(n for n in SHARDED if n not in FIRST)
GRADS_L1 = ('l1_w_in', 'l1_w_glu', 'l1_w_up', 'l1_ffn_conv_w', 'l1_w_down')
GRADS_L0_FFN = ('l0_w_up', 'l0_ffn_conv_w', 'l0_w_down')


def _local_step(x, target, rep, got, wait_rest, send_grads):
    even = {n: _full(n, got[n]) for n in ('l0_w_in', 'l0_w_uq', 'l0_w_ukv', 'l0_w_out')}
    q = _prep_weights(even)
    tabs = _rope_tables(x.shape[0])
    x1, s_mla = _mla_fwd(x, rep, _full('l0_conv_w', got['l0_conv_w']), q, tabs)
    got = wait_rest(x1)
    x2, s_f0 = _ffn_fwd(x1, rep, got, 'l0_', "l0")
    x3, s_s5 = _s5_fwd(x2, rep, _full('l1_w_in', got['l1_w_in']), _full('l1_w_glu', got['l1_w_glu']))
    x4, s_f1 = _ffn_fwd(x3, rep, got, 'l1_', "l1")
    dx4, loss, dgf = _loss_head(x4, rep['final_norm'], target)
    grads, gq, gsh = {'final_norm': dgf.reshape(-1)}, {}, {}
    dx3 = _ffn_bwd(dx4, s_f1, rep, 'l1_', "l1", grads, gsh)
    dx2 = _s5_bwd(dx3, s_s5, rep, grads, gq)
    for n in ('l1_w_in', 'l1_w_glu'):
        gsh[n] = _shards(n, gq[n]).astype(BF16)
    dx2 = send_grads(GRADS_L1, gsh, dx2)
    dx1 = _ffn_bwd(dx2, s_f0, rep, 'l0_', "l0", grads, gsh)
    dx1 = send_grads(GRADS_L0_FFN, gsh, dx1)
    dx0 = _mla_bwd(dx1, s_mla, rep, q, tabs, grads, gq)
    full = _unprep_grads(gq)
    full['l0_conv_w'] = gq['conv_w']
    return loss[0, 0], dx0, grads, {n: _shards(n, full[n]).astype(BF16) for n in FIRST}


def kernel(x, l0_mix_norm, l0_w_in, l0_conv_w, l0_conv_b, l0_conv_ln_g, l0_conv_ln_b, l0_q_norm, l0_kv_norm, l0_w_uq, l0_w_ukv, l0_w_out, l0_ffn_norm, l0_w_up, l0_ffn_conv_w, l0_ffn_conv_b, l0_w_down, l1_mix_norm, l1_w_in, l1_log_dt, l1_a_re, l1_a_im, l1_b_re, l1_b_im, l1_c_re, l1_c_im, l1_d, l1_w_glu, l1_b_glu, l1_ffn_norm, l1_w_up, l1_ffn_conv_w, l1_ffn_conv_b, l1_w_down, final_norm, loss_target, m_l0_mix_norm, m_l0_w_in, m_l0_conv_w, m_l0_conv_b, m_l0_conv_ln_g, m_l0_conv_ln_b, m_l0_q_norm, m_l0_kv_norm, m_l0_w_uq, m_l0_w_ukv, m_l0_w_out, m_l0_ffn_norm, m_l0_w_up, m_l0_ffn_conv_w, m_l0_ffn_conv_b, m_l0_w_down, m_l1_mix_norm, m_l1_w_in, m_l1_log_dt, m_l1_a_re, m_l1_a_im, m_l1_b_re, m_l1_b_im, m_l1_c_re, m_l1_c_im, m_l1_d, m_l1_w_glu, m_l1_b_glu, m_l1_ffn_norm, m_l1_w_up, m_l1_ffn_conv_w, m_l1_ffn_conv_b, m_l1_w_down, m_final_norm, v_l0_mix_norm, v_l0_w_in, v_l0_conv_w, v_l0_conv_b, v_l0_conv_ln_g, v_l0_conv_ln_b, v_l0_q_norm, v_l0_kv_norm, v_l0_w_uq, v_l0_w_ukv, v_l0_w_out, v_l0_ffn_norm, v_l0_w_up, v_l0_ffn_conv_w, v_l0_ffn_conv_b, v_l0_w_down, v_l1_mix_norm, v_l1_w_in, v_l1_log_dt, v_l1_a_re, v_l1_a_im, v_l1_b_re, v_l1_b_im, v_l1_c_re, v_l1_c_im, v_l1_d, v_l1_w_glu, v_l1_b_glu, v_l1_ffn_norm, v_l1_w_up, v_l1_ffn_conv_w, v_l1_ffn_conv_b, v_l1_w_down, v_final_norm):
    args = dict(locals())
    w = {n: args[n] for n in WEIGHTS}
    m = {n: args['m_' + n] for n in WEIGHTS}
    v = {n: args['v_' + n] for n in WEIGHTS}
    payload = lambda n: w[n] if n in TAPS else w[n].astype(BF16)
    got = dict(zip(FIRST, _exchange([payload(n) for n in FIRST], ['gather'] * len(FIRST), "gather_first")))
    rep = {n: w[n] for n in REPLICATED}
    rest, got['l0_conv_w'] = _exchange_start([payload(n) for n in REST], ['gather'] * len(REST), got['l0_conv_w'],
                                             "gather_rest_start")
    wait_rest = lambda after: dict(zip(REST, _exchange_wait(rest, after, "gather_rest_wait")))
    flights = []

    def send_grads(names, gsh, carry):
        tag = "grads_" + names[0][:2]
        flight, carry = _exchange_start([gsh[n] for n in names], ['scatter'] * len(names), carry, tag + "_start")
        flights.append((names, flight, tag + "_wait"))
        return carry

    loss, dx, grads, gsh = _local_step(x[0], loss_target[0], rep, got, wait_rest, send_grads)

    last = _exchange([gsh[n] for n in FIRST] + [_flat([grads[n] for n in REPLICATED])],
                     ['scatter'] * len(FIRST) + ['gather'], "exchange_grads")
    recv = dict(zip(FIRST, last))
    for names, flight, name in flights:
        recv.update(zip(names, _exchange_wait(flight, last[-1], name)))
    res = [dict(), dict(), dict(), dict()]
    for n in SHARDED:
        for kind, a in enumerate(_adamw(recv[n], w[n], m[n], v[n], "adamw_" + n)):
            res[kind][n] = a
    flatr = lambda d: _flat([d[n] for n in REPLICATED])
    rp_out = _adamw(last[-1], flatr(w), flatr(m), flatr(v), "adamw_replicated")
    for kind in range(4):
        for n, a in zip(REPLICATED, _unflat(rp_out[kind], [w[n].shape for n in REPLICATED])):
            res[kind][n] = a
    total = lax.psum(loss, ("x", "y", "c"))
    return (total, dx[None], *[res[0][n] for n in WEIGHTS], *[res[1][n] for n in WEIGHTS],
            *[res[2][n] for n in WEIGHTS], *[res[3][n] for n in WEIGHTS])
```

```python
import functools

import jax
import jax.numpy as jnp
from jax import lax
from jax.experimental import pallas as pl
from jax.experimental.pallas import tpu as pltpu

F32 = jnp.float32
BF16 = jnp.bfloat16

N_DEV = 8
D_MODEL = 1024
EPS = 1e-6
LN_EPS = 1e-5
CONV_WIDTH = 512
CONV_K = 31
N_HEADS = 8
QK_NOPE = 64
QK_ROPE = 32
V_DIM = 64
HEAD_PAD = 128
Q_RANK = 256
KV_RANK = 128
ROPE_BASE = 10000.0
IN_EVEN = 2 * CONV_WIDTH + Q_RANK + KV_RANK + QK_ROPE
IN_PAD = 1536
KR_LANE = 64
SSM_WIDTH = 512
SSM_GROUP = 16
SSM_GROUPS = 32
SSM_STATE = 64
SSM_CH = SSM_GROUPS * SSM_STATE
D_FF = 2816
FF_SHARD = 2 * D_FF // N_DEV
FF_HALF = N_DEV // 2
FFN_K = 3
TAP_ROWS = 8
ADAM_LR, ADAM_B1, ADAM_B2, ADAM_EPS, ADAM_WD, ADAM_STEP = 0.001, 0.9, 0.999, 1e-08, 0.01, 10

WEIGHTS = ['l0_mix_norm', 'l0_w_in', 'l0_conv_w', 'l0_conv_b', 'l0_conv_ln_g', 'l0_conv_ln_b', 'l0_q_norm',
           'l0_kv_norm', 'l0_w_uq', 'l0_w_ukv', 'l0_w_out', 'l0_ffn_norm', 'l0_w_up', 'l0_ffn_conv_w',
           'l0_ffn_conv_b', 'l0_w_down', 'l1_mix_norm', 'l1_w_in', 'l1_log_dt', 'l1_a_re', 'l1_a_im', 'l1_b_re',
           'l1_b_im', 'l1_c_re', 'l1_c_im', 'l1_d', 'l1_w_glu', 'l1_b_glu', 'l1_ffn_norm', 'l1_w_up',
           'l1_ffn_conv_w', 'l1_ffn_conv_b', 'l1_w_down', 'final_norm']
SHARDED = {'l0_w_in': 1, 'l0_conv_w': 1, 'l0_w_uq': 1, 'l0_w_ukv': 1, 'l0_w_out': 0, 'l0_w_up': 1,
           'l0_ffn_conv_w': 1, 'l0_w_down': 0, 'l1_w_in': 0, 'l1_w_glu': 1, 'l1_w_up': 1, 'l1_ffn_conv_w': 1,
           'l1_w_down': 0}
TAPS = ('l0_conv_w', 'l0_ffn_conv_w', 'l1_ffn_conv_w')
REPLICATED = [n for n in WEIGHTS if n not in SHARDED]


def _tile(n, cands):
    for c in cands:
        if n % c == 0:
            return c
    return n


def _same(g):
    return g


def _mm(a, b, *, ta=False, tb=False, res=None, out_dtype=F32, name, ga=None, gb=None, go=None, groups=1,
        sparse=None):
    a2, b2 = (a.shape[1:] if ga else a.shape), (b.shape[1:] if gb else b.shape)
    m, kd = (a2[1], a2[0]) if ta else a2
    kd2, n = (b2[1], b2[0]) if tb else b2
    assert kd == kd2, (a.shape, b.shape, ta, tb)
    tm = _tile(m, (1024, 512, 256, 128))
    tn = _tile(n, (512, 384, 256, 128))
    tk = _tile(kd, (1024, 512, 256, 128))
    nk = kd // tk
    a_m = b_n = k_of = None
    if sparse is not None:
        (tm, tn, tk), (m, n), nk = sparse['tiles'], sparse['out'], sparse['nk']
        a_m, b_n, k_of = sparse.get('a_m'), sparse.get('b_n'), sparse.get('k_of')
    a_m = a_m or (lambda i, j: i)
    b_n = b_n or (lambda i, j: j)
    k_of = k_of or (lambda i, j, k: k)
    summed = go is None and (ga is not None or gb is not None)
    nkk = nk * (groups if summed else 1)
    dn = (((0 if ta else 1,), (1 if tb else 0,)), ((), ()))

    def body(*refs):
        if res is None:
            a_ref, b_ref, o_ref, acc_ref = refs
            r_ref = None
        else:
            a_ref, b_ref, r_ref, o_ref, acc_ref = refs
        k = pl.program_id(3)
        p = lax.dot_general(a_ref[...].astype(BF16), b_ref[...].astype(BF16), dn, preferred_element_type=F32)

        @pl.when(k == 0)
        def _():
            acc_ref[...] = p

        @pl.when(k > 0)
        def _():
            acc_ref[...] += p

        @pl.when(k == nkk - 1)
        def _():
            out = acc_ref[...]
            if r_ref is not None:
                out = out + r_ref[...]
            o_ref[...] = out.astype(out_dtype)

    def spec(shape2, idx2, gmap):
        if gmap is None:
            return pl.BlockSpec(shape2, lambda g, i, j, kk: idx2(i, j, kk % nk))
        if summed:
            return pl.BlockSpec((None,) + shape2, lambda g, i, j, kk: (gmap(kk // nk),) + idx2(i, j, kk % nk))
        return pl.BlockSpec((None,) + shape2, lambda g, i, j, kk: (gmap(g),) + idx2(i, j, kk))

    a_idx = (lambda i, j, k: (k_of(i, j, k), a_m(i, j))) if ta else (lambda i, j, k: (a_m(i, j), k_of(i, j, k)))
    b_idx = (lambda i, j, k: (b_n(i, j), k_of(i, j, k))) if tb else (lambda i, j, k: (k_of(i, j, k), b_n(i, j)))
    a_spec = spec((tk, tm) if ta else (tm, tk), a_idx, ga)
    b_spec = spec((tn, tk) if tb else (tk, tn), b_idx, gb)
    o_spec = spec((tm, tn), lambda i, j, k: (i, j), go)
    in_specs, args = [a_spec, b_spec], [a, b]
    if res is not None:
        in_specs.append(o_spec)
        args.append(res)
    out_shape = (groups, m, n) if go else (m, n)
    return pl.pallas_call(
        body, grid=(groups if go else 1, m // tm, n // tn, nkk), in_specs=in_specs, out_specs=o_spec,
        out_shape=jax.ShapeDtypeStruct(out_shape, out_dtype), scratch_shapes=[pltpu.VMEM((tm, tn), F32)],
        compiler_params=pltpu.CompilerParams(dimension_semantics=("parallel", "parallel", "parallel", "arbitrary")),
        name=name)(*args)


def _rowwise(fn, rows, bcasts, row_outs, red_outs, *, ts, name):
    s = rows[0][0].shape[0]
    nr, nb, nro, nre = len(rows), len(bcasts), len(row_outs), len(red_outs)

    def body(*refs):
        i = pl.program_id(0)
        outs = fn(*[r[...] for r in refs[:nr + nb]])
        if not isinstance(outs, (tuple, list)):
            outs = (outs,)
        o_refs = refs[nr + nb:]
        for q in range(nro):
            o_refs[q][...] = outs[q].astype(o_refs[q].dtype)
        for q in range(nro, nro + nre):
            @pl.when(i == 0)
            def _(q=q):
                o_refs[q][...] = outs[q]

            @pl.when(i > 0)
            def _(q=q):
                o_refs[q][...] += outs[q]

    in_specs = [pl.BlockSpec((ts, w), functools.partial(lambda i, cb: (i, cb), cb=cb)) for (_, w, cb) in rows]
    in_specs += [pl.BlockSpec(b.shape, functools.partial(lambda i, nd: (0,) * nd, nd=b.ndim)) for b in bcasts]
    out_specs = [pl.BlockSpec((ts, w), lambda i: (i, 0)) for (w, _) in row_outs]
    out_specs += [pl.BlockSpec((r, w), lambda i: (0, 0)) for (r, w) in red_outs]
    out_shape = [jax.ShapeDtypeStruct((s, w), dt) for (w, dt) in row_outs]
    out_shape += [jax.ShapeDtypeStruct((r, w), F32) for (r, w) in red_outs]
    return pl.pallas_call(
        body, grid=(s // ts,), in_specs=in_specs, out_specs=out_specs, out_shape=out_shape,
        compiler_params=pltpu.CompilerParams(dimension_semantics=("arbitrary",)), name=name,
    )(*[r[0] for r in rows], *bcasts)


def _rms(x, g):
    return x * lax.rsqrt(jnp.mean(x * x, axis=-1, keepdims=True) + EPS) * g


def _rms_fwd(x, g, name):
    return _rowwise(lambda xv, gv: _rms(xv, gv), [(x, x.shape[1], 0)], [g.reshape(1, -1)],
                    [(x.shape[1], BF16)], [], ts=512, name=name)[0]


def _rms_bwd(x, g, dxn, dres, name):
    d = x.shape[1]

    def fn(xv, dv, rv, gv):
        _, vjp = jax.vjp(_rms, xv, gv)
        dx, dg = vjp(dv.astype(F32))
        return rv + dx, dg

    return _rowwise(fn, [(x, d, 0), (dxn, d, 0), (dres, d, 0)], [g.reshape(1, -1)], [(d, F32)], [(1, d)],
                    ts=512, name=name)


def _cspec(mode, off, rows, width, rowblk, n_rb):
    if mode == 'col':
        return pl.BlockSpec((rows, width), lambda jc, i: (rowblk(i), off + jc))
    return pl.BlockSpec((rows, width), lambda jc, i: ((off + jc) * n_rb + rowblk(i), 0))


TAP_SPREAD = 24


def _stage_shape(k_taps, n, tc):
    return (8, n + TAP_SPREAD, tc) if k_taps > 8 else (1, 8, 128)


def _tap_windows(ref, offsets, n, stage):
    if len(offsets) <= 8:
        return [functools.partial(lambda v: v, ref[pl.ds(o, n), :]) for o in offsets]
    lows = {}
    for o in offsets:
        lows[o % 8] = min(o, lows.get(o % 8, o))
    for r, lo in lows.items():
        span = n + max(o for o in offsets if o % 8 == r) - lo
        stage[r, pl.ds(0, span), :] = ref[pl.ds(lo, span), :]
    return [functools.partial(lambda o: stage[o % 8, pl.ds(o - lows[o % 8], n), :], o) for o in offsets]


def _conv_fwd(xins, w, woffs, params, poffs, pre, post, outs, *, mode, s, k_taps, hb, ts, tc, rc, n_cb, name):
    n_s, nx, ncv, npar, no = s // ts, len(xins), len(woffs), len(params), len(outs)
    rpb = ts // hb
    kp = TAP_ROWS * ((k_taps + TAP_ROWS - 1) // TAP_ROWS)
    pr = 1 if mode == 'col' else TAP_ROWS

    def body(*refs):
        mains, halos = refs[:nx], refs[nx:2 * nx]
        w_refs = refs[2 * nx:2 * nx + ncv]
        p_refs = refs[2 * nx + ncv:2 * nx + ncv + npar]
        o_refs = refs[2 * nx + ncv + npar:2 * nx + ncv + npar + no]
        u_s = refs[2 * nx + ncv + npar + no:2 * nx + ncv + npar + no + ncv]
        win_s = refs[2 * nx + ncv + npar + no + ncv:2 * nx + ncv + npar + no + 2 * ncv]
        stage_s = refs[2 * nx + ncv + npar + no + 2 * ncv:]
        i = pl.program_id(1)
        um = pre(*[r[...].astype(F32) for r in mains])
        uh = pre(*[r[...].astype(F32) for r in halos])
        first = (i > 0).astype(F32)
        for q in range(ncv):
            u_s[q][pl.ds(0, hb), :] = uh[q] * first
            u_s[q][pl.ds(hb, ts), :] = um[q]
        pv = [r[0:1, :] for r in p_refs]

        def chunk(ci, carry):
            r0 = pl.multiple_of(ci * rc, rc)
            cs = []
            for q in range(ncv):
                win_s[q][...] = u_s[q][pl.ds(r0, rc + hb), :]
                acc = jnp.zeros((rc, tc), F32)
                offsets = [hb - (k_taps - 1) + t for t in range(k_taps)]
                for t, ut in enumerate(_tap_windows(win_s[q], offsets, rc, stage_s[q])):
                    acc = acc + w_refs[q][pl.ds(t, 1), :] * ut()
                cs.append(acc)
            res = post(cs, pv)
            for q in range(no):
                o_refs[q][pl.ds(r0, rc), :] = res[q].astype(o_refs[q].dtype)
            return carry

        lax.fori_loop(0, ts // rc, chunk, 0)

    main = lambda i: i
    prev = lambda i: jnp.maximum(i * rpb - 1, 0)
    zero = lambda i: 0
    in_specs = [_cspec(mode, off, ts, tc, main, n_s) for _, off in xins]
    in_specs += [_cspec(mode, off, hb, tc, prev, s // hb) for _, off in xins]
    in_specs += [_cspec(mode, off, kp, tc, zero, 1) for off in woffs]
    in_specs += [_cspec(mode, off, pr, tc, zero, 1) for off in poffs]
    out_specs = [_cspec(mode, 0, ts, tc, main, n_s) for _ in outs]
    oshape = (s, n_cb * tc) if mode == 'col' else (n_cb * s, tc)
    out_shape = [jax.ShapeDtypeStruct(oshape, dt) for dt in outs]
    return pl.pallas_call(
        body, grid=(n_cb, n_s), in_specs=in_specs, out_specs=out_specs, out_shape=out_shape,
        scratch_shapes=[pltpu.VMEM((hb + ts, tc), F32) for _ in range(ncv)]
        + [pltpu.VMEM((hb + rc, tc), F32) for _ in range(ncv)]
        + [pltpu.VMEM(_stage_shape(k_taps, rc, tc), F32) for _ in range(ncv)],
        compiler_params=pltpu.CompilerParams(dimension_semantics=("parallel", "arbitrary")), name=name,
    )(*[a for a, _ in xins], *[a for a, _ in xins], *([w] * ncv), *params)


def _conv_bwd(xins, w, woffs, params, poffs, pre, post, douts, dx_dtype, *, mode, s, k_taps, hb, ts, tc, rc, n_cb, name):
    n_s, nx, ncv, npar, ndo = s // ts, len(xins), len(woffs), len(params), len(douts)
    rpb = ts // hb
    n_hb = s // hb
    kp = TAP_ROWS * ((k_taps + TAP_ROWS - 1) // TAP_ROWS)
    pr = 1 if mode == 'col' else TAP_ROWS

    def body(*refs):
        pos = 0

        def take(n):
            nonlocal pos
            out = refs[pos:pos + n]
            pos += n
            return out

        mains, prevs, nexts = take(nx), take(nx), take(nx)
        d_mains, d_nexts = take(ndo), take(ndo)
        w_refs, p_refs = take(ncv), take(npar)
        dx_refs, dw_refs, dp_refs = take(nx), take(ncv), take(npar)
        u_s, d_s = take(ncv), take(ndo)
        win_s, dcw_s, dwa_s = take(ncv), take(ncv), take(ncv)
        stu_s, std_s = take(ncv), take(ncv)
        i = pl.program_id(1)
        um = pre(*[r[...].astype(F32) for r in mains])
        up = pre(*[r[...].astype(F32) for r in prevs])
        un = pre(*[r[...].astype(F32) for r in nexts])
        first = (i > 0).astype(F32)
        last = (i < n_s - 1).astype(F32)
        for q in range(ncv):
            u_s[q][pl.ds(0, hb), :] = up[q] * first
            u_s[q][pl.ds(hb, ts), :] = um[q]
            u_s[q][pl.ds(hb + ts, hb), :] = un[q]
            dwa_s[q][...] = jnp.zeros((kp * 8, tc), F32)
        for q in range(ndo):
            d_s[q][pl.ds(0, ts), :] = d_mains[q][...].astype(F32)
            d_s[q][pl.ds(ts, hb), :] = d_nexts[q][...].astype(F32) * last
        pv = [r[0:1, :] for r in p_refs]
        fwd_offsets = [hb - (k_taps - 1) + t for t in range(k_taps)]

        def chunk(ci, dpar):
            r0 = pl.multiple_of(ci * rc, rc)
            c_own, c_next, shifted = [], [], []
            for q in range(ncv):
                win_s[q][...] = u_s[q][pl.ds(r0, rc + 2 * hb), :]
                acc = jnp.zeros((rc + hb, tc), F32)
                taps = _tap_windows(win_s[q], fwd_offsets, rc + hb, stu_s[q])
                for t, ut in enumerate(taps):
                    acc = acc + w_refs[q][pl.ds(t, 1), :] * ut()
                c_own.append(acc[:rc])
                c_next.append(acc[rc:])
                shifted.append(taps)
            _, vjp_o = jax.vjp(lambda c, p: tuple(post(c, p)), c_own, pv)
            dc_own, dp_own = vjp_o(tuple(r[pl.ds(r0, rc), :] for r in d_s))
            _, vjp_n = jax.vjp(lambda c: tuple(post(c, pv)), c_next)
            (dc_next,) = vjp_n(tuple(r[pl.ds(r0 + rc, hb), :] for r in d_s))
            dus = []
            for q in range(ncv):
                dcw_s[q][pl.ds(0, rc), :] = dc_own[q]
                dcw_s[q][pl.ds(rc, hb), :] = dc_next[q]
                acc = jnp.zeros((rc, tc), F32)
                bwd_offsets = [k_taps - 1 - t for t in range(k_taps)]
                for t, dct in enumerate(_tap_windows(dcw_s[q], bwd_offsets, rc, std_s[q])):
                    acc = acc + w_refs[q][pl.ds(t, 1), :] * dct()
                    prod = dc_own[q] * shifted[q][t]()[:rc]
                    dwa_s[q][pl.ds(8 * t, 8), :] += jnp.sum(prod.reshape(rc // 8, 8, tc), axis=0)
                dus.append(acc)
            xm = [r[pl.ds(r0, rc), :].astype(F32) for r in mains]
            _, vjp_p = jax.vjp(lambda *xv: tuple(pre(*xv)), *xm)
            dxs = vjp_p(tuple(dus))
            for q in range(nx):
                dx_refs[q][pl.ds(r0, rc), :] = dxs[q].astype(dx_refs[q].dtype)
            return tuple(a + b for a, b in zip(dpar, dp_own))

        dpar = lax.fori_loop(0, ts // rc, chunk, tuple(jnp.zeros((1, tc), F32) for _ in range(npar)))
        for q in range(ncv):
            @pl.when(i == 0)
            def _(q=q):
                dw_refs[q][...] = jnp.zeros((kp, tc), F32)

            for t in range(k_taps):
                dw_refs[q][pl.ds(t, 1), :] += jnp.sum(dwa_s[q][pl.ds(8 * t, 8), :], axis=0, keepdims=True)
        for q in range(npar):
            @pl.when(i == 0)
            def _(q=q):
                dp_refs[q][...] = jnp.zeros((pr, tc), F32)

            dp_refs[q][0:1, :] += dpar[q]

    main = lambda i: i
    prev = lambda i: jnp.maximum(i * rpb - 1, 0)
    nxt = lambda i: jnp.minimum((i + 1) * rpb, n_hb - 1)
    zero = lambda i: 0
    in_specs = [_cspec(mode, off, ts, tc, main, n_s) for _, off in xins]
    in_specs += [_cspec(mode, off, hb, tc, prev, n_hb) for _, off in xins]
    in_specs += [_cspec(mode, off, hb, tc, nxt, n_hb) for _, off in xins]
    in_specs += [_cspec(mode, off, ts, tc, main, n_s) for _, off in douts]
    in_specs += [_cspec(mode, off, hb, tc, nxt, n_hb) for _, off in douts]
    in_specs += [_cspec(mode, off, kp, tc, zero, 1) for off in woffs]
    in_specs += [_cspec(mode, off, pr, tc, zero, 1) for off in poffs]
    out_specs = [_cspec(mode, 0, ts, tc, main, n_s) for _ in xins]
    out_specs += [_cspec(mode, 0, kp, tc, zero, 1) for _ in woffs]
    out_specs += [_cspec(mode, 0, pr, tc, zero, 1) for _ in params]

    def shape(rows):
        return (rows, n_cb * tc) if mode == 'col' else (n_cb * rows, tc)

    out_shape = [jax.ShapeDtypeStruct(shape(s), dx_dtype) for _ in xins]
    out_shape += [jax.ShapeDtypeStruct(shape(kp), F32) for _ in woffs]
    out_shape += [jax.ShapeDtypeStruct(shape(pr), F32) for _ in params]
    xa = [a for a, _ in xins]
    da = [a for a, _ in douts]
    return pl.pallas_call(
        body, grid=(n_cb, n_s), in_specs=in_specs, out_specs=out_specs, out_shape=out_shape,
        scratch_shapes=[pltpu.VMEM((hb + ts + hb, tc), F32) for _ in range(ncv)]
        + [pltpu.VMEM((ts + hb, tc), F32) for _ in range(ndo)]
        + [pltpu.VMEM((rc + 2 * hb, tc), F32) for _ in range(ncv)]
        + [pltpu.VMEM((rc + hb, tc), F32) for _ in range(ncv)]
        + [pltpu.VMEM((kp * 8, tc), F32) for _ in range(ncv)]
        + [pltpu.VMEM(_stage_shape(k_taps, rc + hb, tc), F32) for _ in range(ncv)]
        + [pltpu.VMEM(_stage_shape(k_taps, rc, tc), F32) for _ in range(ncv)],
        compiler_params=pltpu.CompilerParams(dimension_semantics=("parallel", "arbitrary")), name=name,
    )(*xa, *xa, *xa, *da, *da, *([w] * ncv), *params)


def _glu_pre(a, g):
    return [a * jax.nn.sigmoid(g)]


def _ln_silu_post(cs, ps):
    c = cs[0] + ps[0]
    mu = jnp.mean(c, axis=-1, keepdims=True)
    var = jnp.mean(jnp.square(c - mu), axis=-1, keepdims=True)
    y = (c - mu) * lax.rsqrt(var + LN_EPS) * ps[1] + ps[2]
    return [jax.nn.silu(y)]


def _pair_pre(a, b):
    return [a, b]


def _gate_post(cs, ps):
    return [jax.nn.silu(cs[0] + ps[0]) * (cs[1] + ps[1])]


def _conva(s):
    return dict(mode='col', s=s, k_taps=CONV_K, hb=32, ts=512, tc=CONV_WIDTH, rc=64, n_cb=1)


def _gate(s):
    return dict(mode='row', s=s, k_taps=FFN_K, hb=8, ts=512, tc=FF_SHARD, rc=32, n_cb=FF_HALF)


def _rope_tables(s):
    half = QK_ROPE // 2
    inv = ROPE_BASE ** (-jnp.arange(half, dtype=F32) / half)
    ang = jnp.arange(s, dtype=F32)[:, None] * inv[None, :]
    cos, sin = jnp.cos(ang), jnp.sin(ang)
    z = lambda n: jnp.zeros((s, n), F32)
    c = jnp.concatenate([jnp.ones((s, QK_NOPE), F32), cos, cos, z(HEAD_PAD - QK_NOPE - QK_ROPE)], axis=1)
    s1 = jnp.concatenate([z(QK_NOPE), -sin, z(HEAD_PAD - QK_NOPE - half)], axis=1)
    s2 = jnp.concatenate([z(QK_NOPE + half), sin, z(HEAD_PAD - QK_NOPE - QK_ROPE)], axis=1)
    return c, s1, s2


def _rot(t, c, s1, s2):
    half = QK_ROPE // 2
    return t * c + pltpu.roll(t, HEAD_PAD - half, 1) * s1 + pltpu.roll(t, half, 1) * s2


def _rot_t(d, c, s1, s2):
    half = QK_ROPE // 2
    return d * c + pltpu.roll(d * s1, half, 1) + pltpu.roll(d * s2, HEAD_PAD - half, 1)


def _heads(v):
    return [v[:, h * HEAD_PAD:(h + 1) * HEAD_PAD] for h in range(N_HEADS)]


def _rope_fwd(qf, kvf, hmat, tabs, name):
    w = N_HEADS * HEAD_PAD

    def fn(q, k, v, kr, c, s1, s2):
        krr = _rot(kr, c, s1, s2)
        qo = jnp.concatenate([_rot(t, c, s1, s2) for t in _heads(q)], axis=1)
        ko = jnp.concatenate([t + krr for t in _heads(k)], axis=1)
        lane = lax.broadcasted_iota(jnp.int32, v.shape, 1) & (HEAD_PAD - 1)
        return qo, ko, jnp.where(lane == ONES_LANE, 1.0, v)

    rows = [(qf, w, 0), (kvf, w, 0), (kvf, w, 1), (hmat, HEAD_PAD, IN_PAD // HEAD_PAD - 1)]
    rows += [(t, HEAD_PAD, 0) for t in tabs]
    return _rowwise(fn, rows, [], [(w, BF16)] * 3, [], ts=512, name=name)


def _rope_bwd(dq, dk, dv, tabs, name):
    w = N_HEADS * HEAD_PAD

    def fn(dqv, dkv, dvv, c, s1, s2):
        dqo = jnp.concatenate([_rot_t(t, c, s1, s2) for t in _heads(dqv)], axis=1)
        ksum = functools.reduce(lambda a, b: a + b, _heads(dkv))
        return dqo, jnp.concatenate([dkv, dvv], axis=1), _rot_t(ksum, c, s1, s2)

    rows = [(dq, w, 0), (dk, w, 0), (dv, w, 0)] + [(t, HEAD_PAD, 0) for t in tabs]
    return _rowwise(fn, rows, [], [(w, BF16), (2 * w, BF16), (HEAD_PAD, F32)], [], ts=512, name=name)


ATT_Q = 1024
ATT_SUB = 512
ATT_KV = 1024
ATT_SCALE = (QK_NOPE + QK_ROPE) ** -0.5
LOG2E = 1.4426950408889634
ATT_C2 = ATT_SCALE * LOG2E
ONES_LANE = V_DIM


def _nt(a, b):
    return lax.dot_general(a, b, (((1,), (1,)), ((), ())), preferred_element_type=F32)


def _lanes(x, w):
    return x if w == HEAD_PAD else jnp.tile(x, (1, w // HEAD_PAD))


def _tri(w, transposed):
    r = lax.broadcasted_iota(jnp.int32, (w, w), 0)
    c = lax.broadcasted_iota(jnp.int32, (w, w), 1)
    return (r <= c) if transposed else (c <= r)


def _attn_fwd(q, k, v, name):
    s = q.shape[0]
    tq, kvc = min(ATT_Q, s), min(ATT_KV, s)
    nsub, per = tq // ATT_SUB, tq // kvc

    def body(q_ref, k_ref, v_ref, o_ref, lse_ref, m_s, acc_s):
        i = pl.program_id(1)
        m_s[...] = jnp.full((tq, HEAD_PAD), -jnp.inf, F32)
        acc_s[...] = jnp.zeros((tq, HEAD_PAD), F32)

        def update(r0, n, kb, vb, diag):
            rows = pl.ds(r0, n)
            w = kb.shape[0]
            sc = _nt(q_ref[rows, :], kb)
            if diag:
                sc = jnp.where(_tri(w, False), sc, -jnp.inf)
            m_prev = m_s[rows, :]
            m_next = jnp.maximum(m_prev, jnp.max(sc, axis=1, keepdims=True))
            p = jnp.exp2((sc - _lanes(m_next, w)) * ATT_C2)
            alpha = jnp.exp2((m_prev - m_next) * ATT_C2)
            acc_s[rows, :] = alpha * acc_s[rows, :] + jnp.dot(p.astype(BF16), vb, preferred_element_type=F32)
            m_s[rows, :] = m_next

        def below(j, carry):
            at = pl.ds(pl.multiple_of(j * kvc, kvc), kvc)
            update(0, tq, k_ref[at, :], v_ref[at, :], False)
            return carry

        lax.fori_loop(0, i * per, below, 0)
        for r in range(nsub):
            for c in range(r + 1):
                at = pl.ds(pl.multiple_of(i * tq + c * ATT_SUB, ATT_SUB), ATT_SUB)
                update(r * ATT_SUB, ATT_SUB, k_ref[at, :], v_ref[at, :], c == r)
        l = acc_s[:, ONES_LANE:ONES_LANE + 1]
        o_ref[...] = (acc_s[...] / l).astype(BF16)
        lse_ref[...] = m_s[...] * ATT_SCALE + jnp.log(l)

    q_spec = pl.BlockSpec((tq, HEAD_PAD), lambda h, i: (i, h))
    kv_spec = pl.BlockSpec((s, HEAD_PAD), lambda h, i: (0, h))
    return pl.pallas_call(
        body, grid=(N_HEADS, s // tq), in_specs=[q_spec, kv_spec, kv_spec], out_specs=[q_spec, q_spec],
        out_shape=[jax.ShapeDtypeStruct(q.shape, BF16), jax.ShapeDtypeStruct(q.shape, F32)],
        scratch_shapes=[pltpu.VMEM((tq, HEAD_PAD), F32)] * 2,
        compiler_params=pltpu.CompilerParams(dimension_semantics=("parallel", "arbitrary")), name=name,
    )(q, k, v)


def _attn_rows(cat, dcat, lse, name):
    s = lse.shape[0]
    tq, kvc = min(ATT_Q, s), min(ATT_KV, s)
    per = tq // kvc
    ob = CONV_WIDTH // HEAD_PAD

    def body(o_ref, do_ref, lse_ref, lser_ref, dltr_ref):
        dl = jnp.broadcast_to(jnp.sum(do_ref[...] * o_ref[...].astype(F32), axis=1, keepdims=True), (tq, HEAD_PAD))
        l2 = lse_ref[...] * LOG2E
        for c in range(per):
            lser_ref[c] = jnp.transpose(l2[c * kvc:(c + 1) * kvc])[0:8, :]
            dltr_ref[c] = jnp.transpose(dl[c * kvc:(c + 1) * kvc])[0:8, :]

    q_spec = pl.BlockSpec((tq, HEAD_PAD), lambda h, i: (i, h))
    o_spec = pl.BlockSpec((tq, HEAD_PAD), lambda h, i: (i, ob + h))
    row_spec = pl.BlockSpec((None, per, 8, kvc), lambda h, i: (h, i, 0, 0))
    rows = jax.ShapeDtypeStruct((N_HEADS, s // kvc, 8, kvc), F32)
    return pl.pallas_call(
        body, grid=(N_HEADS, s // tq), in_specs=[o_spec, o_spec, q_spec], out_specs=[row_spec, row_spec],
        out_shape=[rows, rows],
        compiler_params=pltpu.CompilerParams(dimension_semantics=("parallel", "parallel")), name=name,
    )(cat, dcat, lse)


def _attn_bwd(q, k, v, dcat, lse_r, dlt_r, name):
    s = q.shape[0]
    tk, kvc = min(ATT_Q, s), min(ATT_KV, s)
    nsub, per, n_chunks = tk // ATT_SUB, tk // kvc, s // kvc
    n_j = s // tk
    ob = CONV_WIDTH // HEAD_PAD

    def body(k_ref, v_ref, q_ref, do_ref, lse_ref, dl_ref, dk_ref, dv_ref, dq_ref, dk_s, dv_s):
        j = pl.program_id(1)
        dk_s[...] = jnp.zeros((tk, HEAD_PAD), F32)
        dv_s[...] = jnp.zeros((tk, HEAD_PAD), F32)

        @pl.when(j == 0)
        def _():
            dq_ref[...] = jnp.zeros((s, HEAD_PAD), F32)

        def update(r0, n, at, lrow, drow, diag):
            rows = pl.ds(r0, n)
            qb, dob = q_ref[at, :], do_ref[at, :].astype(BF16)
            sc = _nt(k_ref[rows, :], qb)
            if diag:
                sc = jnp.where(_tri(qb.shape[0], True), sc, -jnp.inf)
            p = jnp.exp2(sc * ATT_C2 - lrow)
            dp = _nt(v_ref[rows, :], dob)
            ds = (p * (dp - drow)).astype(BF16)
            dv_s[rows, :] += jnp.dot(p.astype(BF16), dob, preferred_element_type=F32)
            dk_s[rows, :] += jnp.dot(ds, qb, preferred_element_type=F32)
            dq_ref[at, :] += lax.dot_general(ds, k_ref[rows, :], (((0,), (0,)), ((), ())), preferred_element_type=F32)

        def above(ic, carry):
            at = pl.ds(pl.multiple_of(ic * kvc, kvc), kvc)
            update(0, tk, at, lse_ref[ic, 0:1, :], dl_ref[ic, 0:1, :], False)
            return carry

        lax.fori_loop((j + 1) * per, n_chunks, above, 0)
        for r in range(nsub):
            for c in range(r, nsub):
                at = pl.ds(pl.multiple_of(j * tk + c * ATT_SUB, ATT_SUB), ATT_SUB)
                ic = j * per + (c * ATT_SUB) // kvc
                lo = (c * ATT_SUB) % kvc
                update(r * ATT_SUB, ATT_SUB, at, lse_ref[ic, 0:1, lo:lo + ATT_SUB], dl_ref[ic, 0:1, lo:lo + ATT_SUB],
                       c == r)
        dk_ref[...] = dk_s[...] * ATT_SCALE
        dv_ref[...] = dv_s[...]

        @pl.when(j == n_j - 1)
        def _():
            dq_ref[...] = dq_ref[...] * ATT_SCALE

    kv_spec = pl.BlockSpec((tk, HEAD_PAD), lambda h, j: (j, h))
    q_spec = pl.BlockSpec((s, HEAD_PAD), lambda h, j: (0, h))
    do_spec = pl.BlockSpec((s, HEAD_PAD), lambda h, j: (0, ob + h))
    row_spec = pl.BlockSpec((None, n_chunks, 8, kvc), lambda h, j: (h, 0, 0, 0))
    full = jax.ShapeDtypeStruct(q.shape, F32)
    dk, dv, dq = pl.pallas_call(
        body, grid=(N_HEADS, n_j), in_specs=[kv_spec, kv_spec, q_spec, do_spec, row_spec, row_spec],
        out_specs=[kv_spec, kv_spec, q_spec], out_shape=[full, full, full],
        scratch_shapes=[pltpu.VMEM((tk, HEAD_PAD), F32)] * 2,
        compiler_params=pltpu.CompilerParams(dimension_semantics=("parallel", "arbitrary")), name=name,
    )(k, v, q, dcat, lse_r, dlt_r)
    return dq, dk, dv


SCAN_T = 256
SCAN_C = 512


def _scan(b, lam, *, reverse, xs=None, name):
    s = b.shape[0]
    t = min(SCAN_T, s)
    n_t, n_c = s // t, SSM_CH // SCAN_C
    with_dlam = xs is not None

    def shift(a, d, row):
        if d >= 8:
            z = jnp.zeros((d, SCAN_C), F32)
            return jnp.concatenate([a[d:], z], axis=0) if reverse else jnp.concatenate([z, a[:t - d]], axis=0)
        if reverse:
            return jnp.where(row < t - d, pltpu.roll(a, t - d, 0), 0.0)
        return jnp.where(row >= d, pltpu.roll(a, d, 0), 0.0)

    def body(*refs):
        if with_dlam:
            b_ref, lam_ref, x_ref, o_ref, dl_ref, c_s = refs
        else:
            b_ref, lam_ref, o_ref, c_s = refs
        k = pl.program_id(0)

        @pl.when(k == 0)
        def _():
            c_s[...] = jnp.zeros((1, 2 * SSM_CH), F32)
            if with_dlam:
                dl_ref[...] = jnp.zeros((1, 2 * SSM_CH), F32)

        row = lax.broadcasted_iota(jnp.int32, (t, SCAN_C), 0)
        edge = (row == t - 1) if reverse else (row == 0)
        for ch in range(n_c):
            re = pl.ds(ch * SCAN_C, SCAN_C)
            im = pl.ds(SSM_CH + ch * SCAN_C, SCAN_C)
            lr = lam_ref[:, re]
            li = -lam_ref[:, im] if reverse else lam_ref[:, im]
            cr, ci = c_s[:, re], c_s[:, im]
            ar = b_ref[:, re] + jnp.where(edge, lr * cr - li * ci, 0.0)
            ai = b_ref[:, im] + jnp.where(edge, lr * ci + li * cr, 0.0)
            d = 1
            while d < t:
                sr, si = shift(ar, d, row), shift(ai, d, row)
                ar, ai = ar + lr * sr - li * si, ai + lr * si + li * sr
                lr, li = lr * lr - li * li, 2.0 * lr * li
                d *= 2
            o_ref[:, re] = ar.astype(o_ref.dtype)
            o_ref[:, im] = ai.astype(o_ref.dtype)
            if with_dlam:
                gr = jnp.where(edge, cr, shift(ar, 1, row))
                gi = jnp.where(edge, ci, shift(ai, 1, row))
                xr, xi = x_ref[:, re].astype(F32), x_ref[:, im].astype(F32)
                dl_ref[:, re] += jnp.sum(xr * gr + xi * gi, axis=0, keepdims=True)
                dl_ref[:, im] += jnp.sum(xr * gi - xi * gr, axis=0, keepdims=True)
            last = 0 if reverse else t - 1
            c_s[:, re] = ar[last:last + 1, :]
            c_s[:, im] = ai[last:last + 1, :]

    tm = (lambda k: (n_t - 1 - k, 0)) if reverse else (lambda k: (k, 0))
    blk = pl.BlockSpec((t, 2 * SSM_CH), tm)
    vec = pl.BlockSpec((1, 2 * SSM_CH), lambda k: (0, 0))
    in_specs, args = [blk, vec], [b, lam]
    out_specs, out_shape = [blk], [jax.ShapeDtypeStruct((s, 2 * SSM_CH), BF16)]
    if with_dlam:
        in_specs.append(blk)
        args.append(xs)
        out_specs.append(vec)
        out_shape.append(jax.ShapeDtypeStruct((1, 2 * SSM_CH), F32))
    return pl.pallas_call(
        body, grid=(n_t,), in_specs=in_specs, out_specs=out_specs, out_shape=out_shape,
        scratch_shapes=[pltpu.VMEM((1, 2 * SSM_CH), F32)],
        compiler_params=pltpu.CompilerParams(dimension_semantics=("arbitrary",)), name=name,
    )(*args)


def _s5_disc(log_dt, a_re, a_im, b_re, b_im):
    dt = jnp.exp(log_dt)[:, None]
    mag = jnp.exp(a_re * dt)
    lb_re, lb_im = mag * jnp.cos(a_im * dt), mag * jnp.sin(a_im * dt)
    den = a_re * a_re + a_im * a_im
    nr, ni = lb_re - 1.0, lb_im
    f_re = (nr * a_re + ni * a_im) / den
    f_im = (ni * a_re - nr * a_im) / den
    bb_re = f_re[..., None] * b_re - f_im[..., None] * b_im
    bb_im = f_re[..., None] * b_im + f_im[..., None] * b_re
    return lb_re, lb_im, bb_re, bb_im


def _bd(a):
    g, i, j = a.shape
    eye = jnp.eye(g, dtype=a.dtype)
    return (a[:, :, None, :] * eye[:, None, :, None]).reshape(g * i, g * j)


S5_TILE_GROUPS = HEAD_PAD // SSM_GROUP


def _s5_sparse(s):
    nb = SSM_GROUPS // S5_TILE_GROUPS
    cw, sw = S5_TILE_GROUPS * SSM_GROUP, S5_TILE_GROUPS * SSM_STATE
    tm = min(1024, s)
    return dict(
        expand=dict(tiles=(tm, sw, cw), out=(s, 2 * SSM_CH), nk=1, k_of=lambda i, j, k: j % nb),
        reduce=dict(tiles=(tm, cw, sw), out=(s, SSM_WIDTH), nk=2, k_of=lambda i, j, k: j + nb * k),
        wide_t=dict(tiles=(sw, cw, tm), out=(2 * SSM_CH, cw), nk=s // tm, b_n=lambda i, j: i % nb),
        narrow_t=dict(tiles=(cw, sw, tm), out=(cw, 2 * SSM_CH), nk=s // tm, a_m=lambda i, j: j % nb))


def _exchange(arrs, modes, name):
    n = len(arrs)
    shapes = [a.shape if md == 'scatter' else (N_DEV,) + a.shape for a, md in zip(arrs, modes)]

    def body(*refs):
        srcs, outs = refs[:n], refs[n:2 * n]
        send_sems, recv_sems, local_sems = refs[2 * n:]
        x, y, c = lax.axis_index("x"), lax.axis_index("y"), lax.axis_index("c")
        me = 4 * x + 2 * y + c

        def piece(q, slot):
            return srcs[q].at[slot] if modes[q] == 'scatter' else srcs[q]

        mine = [pltpu.make_async_copy(piece(q, me), outs[q].at[me], local_sems.at[q]) for q in range(n)]
        for cp in mine:
            cp.start()
        copies = []
        for r in range(1, N_DEV):
            px, py, pc = x ^ (r >> 2), y ^ ((r >> 1) & 1), c ^ (r & 1)
            peer = 4 * px + 2 * py + pc
            for q in range(n):
                copies.append(pltpu.make_async_remote_copy(
                    src_ref=piece(q, peer), dst_ref=outs[q].at[me], send_sem=send_sems.at[(r - 1) * n + q],
                    recv_sem=recv_sems.at[(r - 1) * n + q], device_id=(px, py, pc),
                    device_id_type=pl.DeviceIdType.MESH))
        for cp in copies:
            cp.start()
        for cp in copies:
            cp.wait_recv()
        for cp in copies:
            cp.wait_send()
        for cp in mine:
            cp.wait()

    any_spec = pl.BlockSpec(memory_space=pl.ANY)
    return pl.pallas_call(
        body, out_shape=[jax.ShapeDtypeStruct(sh, a.dtype) for sh, a in zip(shapes, arrs)],
        in_specs=[any_spec] * n, out_specs=[any_spec] * n,
        scratch_shapes=[pltpu.SemaphoreType.DMA(((N_DEV - 1) * n,)), pltpu.SemaphoreType.DMA(((N_DEV - 1) * n,)),
                        pltpu.SemaphoreType.DMA((n,))],
        compiler_params=pltpu.CompilerParams(has_side_effects=True), name=name,
    )(*arrs)


def _peers(x, y, c):
    out = []
    for r in range(1, N_DEV):
        px, py, pc = x ^ (r >> 2), y ^ ((r >> 1) & 1), c ^ (r & 1)
        out.append((r, (px, py, pc), 4 * px + 2 * py + pc))
    return out


def _exchange_start(arrs, modes, carry, name):
    n = len(arrs)
    shapes = [a.shape if md == 'scatter' else (N_DEV,) + a.shape for a, md in zip(arrs, modes)]
    lands = [lax.empty(sh, a.dtype) for sh, a in zip(shapes, arrs)]

    def body(*refs):
        srcs, zones = refs[:n], refs[n:2 * n]
        send_sems, recv_sems = refs[2 * n + 1], refs[2 * n + 2]
        local_sems = refs[-1]
        x, y, c = lax.axis_index("x"), lax.axis_index("y"), lax.axis_index("c")
        me = 4 * x + 2 * y + c

        def piece(q, slot):
            return srcs[q].at[slot] if modes[q] == 'scatter' else srcs[q]

        mine = [pltpu.make_async_copy(piece(q, me), zones[q].at[me], local_sems.at[q]) for q in range(n)]
        for cp in mine:
            cp.start()
        for r, pos, peer in _peers(x, y, c):
            for q in range(n):
                pltpu.make_async_remote_copy(
                    src_ref=piece(q, peer), dst_ref=zones[q].at[me], send_sem=send_sems.at[(r - 1) * n + q],
                    recv_sem=recv_sems.at[(r - 1) * n + q], device_id=pos, device_id_type=pl.DeviceIdType.MESH).start()
        for cp in mine:
            cp.wait()

    hbm = pl.BlockSpec(memory_space=pltpu.HBM)
    sem = pl.BlockSpec(memory_space=pltpu.SEMAPHORE)
    thru = arrs + lands + [carry]
    sems = pltpu.SemaphoreType.DMA(((N_DEV - 1) * n,))
    outs = pl.pallas_call(
        body, name=name, out_shape=(sems, sems, *[pltpu.HBM(a.shape, a.dtype) for a in thru]),
        in_specs=[hbm] * len(thru), out_specs=(sem, sem, *[hbm] * len(thru)),
        input_output_aliases={q: 2 + q for q in range(len(thru))},
        scratch_shapes=[pltpu.SemaphoreType.DMA((n,))],
        compiler_params=pltpu.CompilerParams(has_side_effects=pltpu.SideEffectType.DATAFLOW_SIDE_EFFECTING),
    )(*[pltpu.with_memory_space_constraint(a, pltpu.HBM) for a in thru])
    return dict(send=outs[0], recv=outs[1], srcs=list(outs[2:2 + n]), lands=list(outs[2 + n:2 + 2 * n]),
                modes=modes), outs[-1]


def _exchange_wait(flight, after, name):
    n = len(flight['srcs'])
    modes = flight['modes']

    def body(*refs):
        srcs, zones = refs[:n], refs[n:2 * n]
        send_sems, recv_sems = refs[2 * n], refs[2 * n + 1]
        x, y, c = lax.axis_index("x"), lax.axis_index("y"), lax.axis_index("c")
        me = 4 * x + 2 * y + c
        for r, pos, peer in _peers(x, y, c):
            for q in range(n):
                src = srcs[q].at[peer] if modes[q] == 'scatter' else srcs[q]
                cp = pltpu.make_async_remote_copy(
                    src_ref=src, dst_ref=zones[q].at[me], send_sem=send_sems.at[(r - 1) * n + q],
                    recv_sem=recv_sems.at[(r - 1) * n + q], device_id=pos, device_id_type=pl.DeviceIdType.MESH)
                cp.wait_send()
                cp.wait_recv()

    hbm = pl.BlockSpec(memory_space=pltpu.HBM)
    sem = pl.BlockSpec(memory_space=pltpu.SEMAPHORE)
    bufs = flight['srcs'] + flight['lands']
    outs = pl.pallas_call(
        body, name=name, out_shape=tuple(pltpu.HBM(a.shape, a.dtype) for a in bufs),
        in_specs=[hbm] * (2 * n) + [sem, sem, pl.BlockSpec(memory_space=pl.ANY)], out_specs=tuple([hbm] * (2 * n)),
        input_output_aliases={q: q for q in range(2 * n)},
        compiler_params=pltpu.CompilerParams(has_side_effects=pltpu.SideEffectType.DATAFLOW_SIDE_EFFECTING),
    )(*bufs, flight['send'], flight['recv'], after)
    return list(outs[n:])


def _adamw(parts, w, m, v, name):
    r, c = w.shape
    tr = _tile(r, (256, 128))

    def body(p_ref, w_ref, m_ref, v_ref, g_ref, d_ref, nm_ref, nv_ref):
        g = p_ref[0].astype(F32)
        for d in range(1, N_DEV):
            g = g + p_ref[d].astype(F32)
        m2 = ADAM_B1 * m_ref[...] + (1.0 - ADAM_B1) * g
        v2 = ADAM_B2 * v_ref[...] + (1.0 - ADAM_B2) * jnp.square(g)
        m_hat = m2 / (1.0 - ADAM_B1 ** ADAM_STEP)
        v_hat = v2 / (1.0 - ADAM_B2 ** ADAM_STEP)
        g_ref[...] = g
        d_ref[...] = -ADAM_LR * (m_hat / (jnp.sqrt(v_hat) + ADAM_EPS) + ADAM_WD * w_ref[...])
        nm_ref[...] = m2
        nv_ref[...] = v2

    spec = pl.BlockSpec((tr, c), lambda i: (i, 0))
    return pl.pallas_call(
        body, grid=(r // tr,), in_specs=[pl.BlockSpec((N_DEV, tr, c), lambda i: (0, i, 0)), spec, spec, spec],
        out_specs=[spec] * 4, out_shape=[jax.ShapeDtypeStruct((r, c), F32)] * 4,
        compiler_params=pltpu.CompilerParams(dimension_semantics=("parallel",)), name=name,
    )(parts, w, m, v)


FLAT_W = 512
FLAT_ROWS = 256


def _flat(arrs):
    v = jnp.concatenate([a.reshape(-1) for a in arrs])
    return jnp.pad(v, (0, (-v.shape[0]) % (FLAT_ROWS * FLAT_W))).reshape(-1, FLAT_W)


def _unflat(flat, shapes):
    v = flat.reshape(-1)
    out, off = [], 0
    for sh in shapes:
        n = 1
        for d in sh:
            n *= d
        out.append(v[off:off + n].reshape(sh))
        off += n
    return out


def _full(name, stacked):
    if SHARDED[name] == 0:
        return stacked.reshape((-1,) + stacked.shape[2:])
    return jnp.transpose(stacked, (1, 0, 2)).reshape(stacked.shape[1], -1)


def _shards(name, full):
    if SHARDED[name] == 0:
        return full.reshape((N_DEV, -1) + full.shape[1:])
    r, c = full.shape
    return jnp.transpose(full.reshape(r, N_DEV, c // N_DEV), (1, 0, 2))


def _prep_weights(p):
    q = {}
    w_in = p['l0_w_in']
    z = lambda n: jnp.zeros((D_MODEL, n), w_in.dtype)
    q['w_in'] = jnp.concatenate([w_in[:, :IN_EVEN - QK_ROPE], z(KR_LANE), w_in[:, IN_EVEN - QK_ROPE:],
                                 z(HEAD_PAD - KR_LANE - QK_ROPE)], axis=1)
    dqk = QK_NOPE + QK_ROPE
    q['w_uq'] = jnp.pad(p['l0_w_uq'].reshape(Q_RANK, N_HEADS, dqk), ((0, 0), (0, 0), (0, HEAD_PAD - dqk))
                        ).reshape(Q_RANK, N_HEADS * HEAD_PAD)
    ukv = p['l0_w_ukv'].reshape(KV_RANK, N_HEADS, 2, QK_NOPE)
    padh = lambda a: jnp.pad(a, ((0, 0), (0, 0), (0, HEAD_PAD - QK_NOPE))).reshape(KV_RANK, N_HEADS * HEAD_PAD)
    q['w_ukv'] = jnp.concatenate([padh(ukv[:, :, 0]), padh(ukv[:, :, 1])], axis=1)
    wo = p['l0_w_out']
    wo_a = jnp.pad(wo[CONV_WIDTH:].reshape(N_HEADS, V_DIM, D_MODEL), ((0, 0), (0, HEAD_PAD - V_DIM), (0, 0)))
    q['w_out'] = jnp.concatenate([wo[:CONV_WIDTH], wo_a.reshape(N_HEADS * HEAD_PAD, D_MODEL)], axis=0)
    return q


def _unprep_grads(g):
    out = {}
    d = g['w_in']
    out['l0_w_in'] = jnp.concatenate([d[:, :IN_EVEN - QK_ROPE],
                                      d[:, IN_EVEN - QK_ROPE + KR_LANE:IN_EVEN + KR_LANE]], axis=1)
    dqk = QK_NOPE + QK_ROPE
    out['l0_w_uq'] = g['w_uq'].reshape(Q_RANK, N_HEADS, HEAD_PAD)[:, :, :dqk].reshape(Q_RANK, N_HEADS * dqk)
    d = g['w_ukv'].reshape(KV_RANK, 2, N_HEADS, HEAD_PAD)[:, :, :, :QK_NOPE]
    out['l0_w_ukv'] = jnp.transpose(d, (0, 2, 1, 3)).reshape(KV_RANK, N_HEADS * 2 * QK_NOPE)
    d = g['w_out']
    da = d[CONV_WIDTH:].reshape(N_HEADS, HEAD_PAD, D_MODEL)[:, :V_DIM].reshape(N_HEADS * V_DIM, D_MODEL)
    out['l0_w_out'] = jnp.concatenate([d[:CONV_WIDTH], da], axis=0)
    return out


def _pad_rows(w, rows):
    return jnp.pad(w, [(0, 0)] * (w.ndim - 2) + [(0, rows - w.shape[-2]), (0, 0)])


def _ffn_fwd(x, rep, got, pre, tag):
    s = x.shape[0]
    xn = _rms_fwd(x, rep[pre + 'ffn_norm'], f"{tag}_ffn_norm")
    w_up = got[pre + 'w_up']
    hu = _mm(xn, w_up, gb=_same, go=_same, groups=N_DEV, name=f"{tag}_ffn_up").reshape(N_DEV * s, FF_SHARD)
    taps = _pad_rows(got[pre + 'ffn_conv_w'], TAP_ROWS).reshape(N_DEV * TAP_ROWS, FF_SHARD)
    bias = _pad_rows(rep[pre + 'ffn_conv_b'].reshape(N_DEV, 1, FF_SHARD), TAP_ROWS).reshape(N_DEV * TAP_ROWS, FF_SHARD)
    (act,) = _conv_fwd([(hu, 0), (hu, FF_HALF)], taps, [0, FF_HALF], [bias, bias], [0, FF_HALF], _pair_pre,
                       _gate_post, [BF16], name=f"{tag}_ffn_gate", **_gate(s))
    act = act.reshape(FF_HALF, s, FF_SHARD)
    w_down = got[pre + 'w_down'].reshape(FF_HALF, FF_SHARD, D_MODEL)
    y = _mm(act, w_down, ga=_same, gb=_same, groups=FF_HALF, res=x, name=f"{tag}_ffn_down")
    return y, (x, xn, hu, act, taps, bias, w_up, w_down)


def _ffn_bwd(dy, saved, rep, pre, tag, grads, gsh):
    x, xn, hu, act, taps, bias, w_up, w_down = saved
    s = x.shape[0]
    dact = _mm(dy, w_down, tb=True, gb=_same, go=_same, groups=FF_HALF, name=f"{tag}_ffn_down_dx")
    gsh[pre + 'w_down'] = _mm(act, dy, ta=True, ga=_same, go=_same, groups=FF_HALF, out_dtype=BF16,
                              name=f"{tag}_ffn_down_dw").reshape(N_DEV, FF_SHARD // 2, D_MODEL)
    dha, dhb, dwa, dwb, dba, dbb = _conv_bwd(
        [(hu, 0), (hu, FF_HALF)], taps, [0, FF_HALF], [bias, bias], [0, FF_HALF], _pair_pre, _gate_post,
        [(dact.reshape(FF_HALF * s, FF_SHARD), 0)], BF16, name=f"{tag}_ffn_gate_bwd", **_gate(s))
    dha, dhb = dha.reshape(FF_HALF, s, FF_SHARD), dhb.reshape(FF_HALF, s, FF_SHARD)
    dtaps = jnp.concatenate([dwa, dwb], axis=0).reshape(N_DEV, TAP_ROWS, FF_SHARD)
    gsh[pre + 'ffn_conv_w'] = dtaps[:, :FFN_K].astype(BF16)
    grads[pre + 'ffn_conv_b'] = jnp.concatenate([dba, dbb], axis=0).reshape(N_DEV, TAP_ROWS, FF_SHARD)[:, 0].reshape(-1)
    upper = lambda g: g + FF_HALF
    dxn = _mm(dha, w_up, tb=True, ga=_same, gb=_same, groups=FF_HALF, name=f"{tag}_ffn_up_dx_a")
    dxn = _mm(dhb, w_up, tb=True, ga=_same, gb=upper, groups=FF_HALF, res=dxn, name=f"{tag}_ffn_up_dx_b")
    dwu = [_mm(xn, dh, ta=True, gb=_same, go=_same, groups=FF_HALF, out_dtype=BF16, name=f"{tag}_ffn_up_dw_{t}")
           for t, dh in (("a", dha), ("b", dhb))]
    gsh[pre + 'w_up'] = jnp.concatenate(dwu, axis=0)
    dx, dg = _rms_bwd(x, rep[pre + 'ffn_norm'], dxn, dy, f"{tag}_ffn_norm_bwd")
    grads[pre + 'ffn_norm'] = dg.reshape(-1)
    return dx


def _mla_fwd(x, rep, taps, q, tabs):
    s = x.shape[0]
    xn = _rms_fwd(x, rep['l0_mix_norm'], "l0_mix_norm")
    hmat = _mm(xn, q['w_in'], name="l0_in")
    wt = _pad_rows(taps, 4 * TAP_ROWS)
    cpar = [rep['l0_conv_b'].reshape(1, -1), rep['l0_conv_ln_g'].reshape(1, -1), rep['l0_conv_ln_b'].reshape(1, -1)]
    (u,) = _conv_fwd([(hmat, 0), (hmat, 1)], wt, [0], cpar, [0, 0, 0], _glu_pre, _ln_silu_post, [BF16],
                     name="l0_conv", **_conva(s))
    qn, kvn = rep['l0_q_norm'].reshape(1, -1), rep['l0_kv_norm'].reshape(1, -1)
    cqn, ckvn = _rowwise(lambda a, b, ga, gb: (_rms(a, ga), _rms(b, gb)),
                         [(hmat, Q_RANK, 2 * CONV_WIDTH // Q_RANK), (hmat, KV_RANK, (2 * CONV_WIDTH + Q_RANK) // KV_RANK)],
                         [qn, kvn], [(Q_RANK, BF16), (KV_RANK, BF16)], [], ts=512, name="l0_latent_norm")
    qf = _mm(cqn, q['w_uq'], name="l0_uq")
    kvf = _mm(ckvn, q['w_ukv'], name="l0_ukv")
    q_rot, k_full, v = _rope_fwd(qf, kvf, hmat, tabs, "l0_rope")
    o, lse = _attn_fwd(q_rot, k_full, v, "l0_attn")
    cat = jnp.concatenate([u, o], axis=1)
    y = _mm(cat, q['w_out'], res=x, name="l0_out")
    return y, (x, xn, hmat, wt, cpar, qn, kvn, cqn, ckvn, q_rot, k_full, v, lse, cat)


def _mla_bwd(dy, saved, rep, q, tabs, grads, gq):
    x, xn, hmat, wt, cpar, qn, kvn, cqn, ckvn, q_rot, k_full, v, lse, cat = saved
    s = x.shape[0]
    dcat = _mm(dy, q['w_out'], tb=True, name="l0_out_dx")
    gq['w_out'] = _mm(cat, dy, ta=True, name="l0_out_dw")
    lse_r, dlt_r = _attn_rows(cat, dcat, lse, "l0_attn_rows")
    dq, dk, dv = _attn_bwd(q_rot, k_full, v, dcat, lse_r, dlt_r, "l0_attn_bwd")
    dqf, dkvf, dkr = _rope_bwd(dq, dk, dv, tabs, "l0_rope_bwd")
    dcqn = _mm(dqf, q['w_uq'], tb=True, name="l0_uq_dx")
    gq['w_uq'] = _mm(cqn, dqf, ta=True, name="l0_uq_dw")
    dckvn = _mm(dkvf, q['w_ukv'], tb=True, name="l0_ukv_dx")
    gq['w_ukv'] = _mm(ckvn, dkvf, ta=True, name="l0_ukv_dw")

    def lat_bwd(a, b, da, db, ga, gb):
        _, vjp = jax.vjp(lambda a_, b_, ga_, gb_: (_rms(a_, ga_), _rms(b_, gb_)), a, b, ga, gb)
        return vjp((da, db))

    dcq, dckv, dqn, dkvn = _rowwise(
        lat_bwd, [(hmat, Q_RANK, 2 * CONV_WIDTH // Q_RANK), (hmat, KV_RANK, (2 * CONV_WIDTH + Q_RANK) // KV_RANK),
                  (dcqn, Q_RANK, 0), (dckvn, KV_RANK, 0)],
        [qn, kvn], [(Q_RANK, F32), (KV_RANK, F32)], [(1, Q_RANK), (1, KV_RANK)], ts=512, name="l0_latent_norm_bwd")
    grads['l0_q_norm'], grads['l0_kv_norm'] = dqn.reshape(-1), dkvn.reshape(-1)
    da, dg, dwt, dcb, dlg, dlb = _conv_bwd(
        [(hmat, 0), (hmat, 1)], wt, [0], cpar, [0, 0, 0], _glu_pre, _ln_silu_post, [(dcat, 0)], F32,
        name="l0_conv_bwd", **_conva(s))
    gq['conv_w'] = dwt[:CONV_K]
    grads['l0_conv_b'], grads['l0_conv_ln_g'], grads['l0_conv_ln_b'] = dcb.reshape(-1), dlg.reshape(-1), dlb.reshape(-1)
    dh = jnp.concatenate([da, dg, dcq, dckv, dkr], axis=1)
    dxn = _mm(dh, q['w_in'], tb=True, name="l0_in_dx")
    gq['w_in'] = _mm(xn, dh, ta=True, name="l0_in_dw")
    dx, dgn = _rms_bwd(x, rep['l0_mix_norm'], dxn, dy, "l0_mix_norm_bwd")
    grads['l0_mix_norm'] = dgn.reshape(-1)
    return dx


def _gelu_skip(yc, u, d):
    return jax.nn.gelu(yc + d * u)


def _glu_out(z1, z2, b1, b2, x):
    return x + (z1 + b1) * jax.nn.sigmoid(z2 + b2)


def _s5_fwd(x, rep, w_in, w_glu):
    xn = _rms_fwd(x, rep['l1_mix_norm'], "l1_mix_norm")
    u = _mm(xn, w_in, name="l1_in")
    lb_re, lb_im, bb_re, bb_im = _s5_disc(rep['l1_log_dt'], rep['l1_a_re'], rep['l1_a_im'], rep['l1_b_re'],
                                          rep['l1_b_im'])
    lam = jnp.concatenate([lb_re.reshape(1, -1), lb_im.reshape(1, -1)], axis=1)
    tr = lambda a: jnp.transpose(a, (0, 2, 1))
    bmat = jnp.concatenate([_bd(tr(bb_re)), _bd(tr(bb_im))], axis=1)
    cmat = jnp.concatenate([_bd(tr(rep['l1_c_re'])), -_bd(tr(rep['l1_c_im']))], axis=0)
    sp = _s5_sparse(x.shape[0])
    bu = _mm(u, bmat, sparse=sp['expand'], name="l1_bu")
    (xs,) = _scan(bu, lam, reverse=False, name="l1_scan")
    yc = _mm(xs, cmat, sparse=sp['reduce'], name="l1_cx")
    dsk = rep['l1_d'].reshape(1, -1)
    (y,) = _rowwise(_gelu_skip, [(yc, SSM_WIDTH, 0), (u, SSM_WIDTH, 0)], [dsk], [(SSM_WIDTH, BF16)], [],
                    ts=512, name="l1_gelu")
    z = _mm(y, w_glu, name="l1_glu")
    bg = rep['l1_b_glu'].reshape(1, -1)
    (out,) = _rowwise(lambda z1, z2, xv, b1, b2: _glu_out(z1, z2, b1, b2, xv),
                      [(z, D_MODEL, 0), (z, D_MODEL, 1), (x, D_MODEL, 0)], [bg[:, :D_MODEL], bg[:, D_MODEL:]],
                      [(D_MODEL, F32)], [], ts=512, name="l1_glu_out")
    return out, (x, xn, u, lam, bmat, cmat, xs, yc, dsk, y, z, bg, w_in, w_glu)


def _s5_bwd(dy, saved, rep, grads, gq):
    x, xn, u, lam, bmat, cmat, xs, yc, dsk, y, z, bg, w_in, w_glu = saved

    def glu_bwd(z1, z2, dv, b1, b2):
        _, vjp = jax.vjp(lambda a, b, c, d: (a + c) * jax.nn.sigmoid(b + d), z1, z2, b1, b2)
        d1, d2, db1, db2 = vjp(dv)
        return jnp.concatenate([d1, d2], axis=1), db1, db2

    dz, db1, db2 = _rowwise(glu_bwd, [(z, D_MODEL, 0), (z, D_MODEL, 1), (dy, D_MODEL, 0)],
                            [bg[:, :D_MODEL], bg[:, D_MODEL:]], [(2 * D_MODEL, BF16)], [(1, D_MODEL), (1, D_MODEL)],
                            ts=512, name="l1_glu_out_bwd")
    grads['l1_b_glu'] = jnp.concatenate([db1, db2], axis=1).reshape(-1)
    dyv = _mm(dz, w_glu, tb=True, name="l1_glu_dx")
    gq['l1_w_glu'] = _mm(y, dz, ta=True, name="l1_glu_dw")

    def gelu_bwd(ycv, uv, dv, dk):
        _, vjp = jax.vjp(_gelu_skip, ycv, uv, dk)
        return vjp(dv)

    dyc, du_skip, dd = _rowwise(gelu_bwd, [(yc, SSM_WIDTH, 0), (u, SSM_WIDTH, 0), (dyv, SSM_WIDTH, 0)], [dsk],
                                [(SSM_WIDTH, F32), (SSM_WIDTH, F32)], [(1, SSM_WIDTH)], ts=512, name="l1_gelu_bwd")
    grads['l1_d'] = dd.reshape(-1)
    sp = _s5_sparse(x.shape[0])
    dxs = _mm(dyc, cmat, tb=True, sparse=sp['expand'], name="l1_cx_dx")
    dcm = _mm(xs, dyc, ta=True, sparse=sp['wide_t'], name="l1_cx_dw")
    gs, dlam = _scan(dxs, lam, reverse=True, xs=xs, name="l1_scan_bwd")
    dlr, dli = dlam[:, :SSM_CH], dlam[:, SSM_CH:]
    du = _mm(gs, bmat, tb=True, res=du_skip, sparse=sp['reduce'], name="l1_bu_dx")
    dbm = _mm(u, gs, ta=True, sparse=sp['narrow_t'], name="l1_bu_dw")
    eye = jnp.eye(S5_TILE_GROUPS, dtype=F32)
    nb = SSM_GROUPS // S5_TILE_GROUPS
    dcm = dcm.reshape(2, nb, S5_TILE_GROUPS, SSM_STATE, S5_TILE_GROUPS, SSM_GROUP)
    dcm = jnp.sum(dcm * eye[None, None, :, None, :, None], axis=4).reshape(2, SSM_GROUPS, SSM_STATE, SSM_GROUP)
    tr = lambda a: jnp.transpose(a, (0, 2, 1))
    grads['l1_c_re'], grads['l1_c_im'] = tr(dcm[0]), -tr(dcm[1])
    dbm = dbm.reshape(S5_TILE_GROUPS, SSM_GROUP, 2, nb, S5_TILE_GROUPS, SSM_STATE)
    dbm = jnp.sum(dbm * eye[:, None, None, None, :, None], axis=0)
    dbm = jnp.transpose(dbm, (1, 2, 3, 4, 0)).reshape(2, SSM_GROUPS, SSM_STATE, SSM_GROUP)
    dbb_re, dbb_im = dbm[0], dbm[1]
    names = ['l1_log_dt', 'l1_a_re', 'l1_a_im', 'l1_b_re', 'l1_b_im']
    _, vjp = jax.vjp(_s5_disc, *[rep[n] for n in names])
    for n, gval in zip(names, vjp((dlr.reshape(SSM_GROUPS, SSM_STATE), dli.reshape(SSM_GROUPS, SSM_STATE), dbb_re, dbb_im))):
        grads[n] = gval
    dxn = _mm(du, w_in, tb=True, name="l1_in_dx")
    gq['l1_w_in'] = _mm(xn, du, ta=True, name="l1_in_dw")
    dx, dgn = _rms_bwd(x, rep['l1_mix_norm'], dxn, dy, "l1_mix_norm_bwd")
    grads['l1_mix_norm'] = dgn.reshape(-1)
    return dx


def _loss_head(x, g, target):
    d = x.shape[1]

    def fn(xv, tv, gv):
        y, vjp = jax.vjp(_rms, xv, gv)
        err = y - tv
        part = 0.5 * jnp.sum(jnp.mean(jnp.square(err), axis=-1, keepdims=True), axis=0, keepdims=True)
        dx, dg = vjp(err * (1.0 / d))
        return dx, jnp.broadcast_to(part, (1, 128)), dg

    return _rowwise(fn, [(x, d, 0), (target, d, 0)], [g.reshape(1, -1)], [(d, F32)], [(1, 128), (1, d)], ts=512,
                    name="loss_head")


FIRST = ('l0_w_in', 'l0_conv_w', 'l0_w_uq', 'l0_w_ukv', 'l0_w_out')
REST = tuple(n for n in SHARDED if n not in FIRST)
GRADS_L1 = ('l1_w_in', 'l1_w_glu', 'l1_w_up', 'l1_ffn_conv_w', 'l1_w_down')
GRADS_L0_FFN = ('l0_w_up', 'l0_ffn_conv_w', 'l0_w_down')


def _local_step(x, target, rep, got, wait_rest, send_grads):
    even = {n: _full(n, got[n]) for n in ('l0_w_in', 'l0_w_uq', 'l0_w_ukv', 'l0_w_out')}
    q = _prep_weights(even)
    tabs = _rope_tables(x.shape[0])
    x1, s_mla = _mla_fwd(x, rep, _full('l0_conv_w', got['l0_conv_w']), q, tabs)
    got = wait_rest(x1)
    x2, s_f0 = _ffn_fwd(x1, rep, got, 'l0_', "l0")
    x3, s_s5 = _s5_fwd(x2, rep, _full('l1_w_in', got['l1_w_in']), _full('l1_w_glu', got['l1_w_glu']))
    x4, s_f1 = _ffn_fwd(x3, rep, got, 'l1_', "l1")
    dx4, loss, dgf = _loss_head(x4, rep['final_norm'], target)
    grads, gq, gsh = {'final_norm': dgf.reshape(-1)}, {}, {}
    dx3 = _ffn_bwd(dx4, s_f1, rep, 'l1_', "l1", grads, gsh)
    dx2 = _s5_bwd(dx3, s_s5, rep, grads, gq)
    for n in ('l1_w_in', 'l1_w_glu'):
        gsh[n] = _shards(n, gq[n]).astype(BF16)
    dx2 = send_grads(GRADS_L1, gsh, dx2)
    dx1 = _ffn_bwd(dx2, s_f0, rep, 'l0_', "l0", grads, gsh)
    dx1 = send_grads(GRADS_L0_FFN, gsh, dx1)
    dx0 = _mla_bwd(dx1, s_mla, rep, q, tabs, grads, gq)
    full = _unprep_grads(gq)
    full['l0_conv_w'] = gq['conv_w']
    return loss[0, 0], dx0, grads, {n: _shards(n, full[n]).astype(BF16) for n in FIRST}


def kernel(x, l0_mix_norm, l0_w_in, l0_conv_w, l0_conv_b, l0_conv_ln_g, l0_conv_ln_b, l0_q_norm, l0_kv_norm, l0_w_uq, l0_w_ukv, l0_w_out, l0_ffn_norm, l0_w_up, l0_ffn_conv_w, l0_ffn_conv_b, l0_w_down, l1_mix_norm, l1_w_in, l1_log_dt, l1_a_re, l1_a_im, l1_b_re, l1_b_im, l1_c_re, l1_c_im, l1_d, l1_w_glu, l1_b_glu, l1_ffn_norm, l1_w_up, l1_ffn_conv_w, l1_ffn_conv_b, l1_w_down, final_norm, loss_target, m_l0_mix_norm, m_l0_w_in, m_l0_conv_w, m_l0_conv_b, m_l0_conv_ln_g, m_l0_conv_ln_b, m_l0_q_norm, m_l0_kv_norm, m_l0_w_uq, m_l0_w_ukv, m_l0_w_out, m_l0_ffn_norm, m_l0_w_up, m_l0_ffn_conv_w, m_l0_ffn_conv_b, m_l0_w_down, m_l1_mix_norm, m_l1_w_in, m_l1_log_dt, m_l1_a_re, m_l1_a_im, m_l1_b_re, m_l1_b_im, m_l1_c_re, m_l1_c_im, m_l1_d, m_l1_w_glu, m_l1_b_glu, m_l1_ffn_norm, m_l1_w_up, m_l1_ffn_conv_w, m_l1_ffn_conv_b, m_l1_w_down, m_final_norm, v_l0_mix_norm, v_l0_w_in, v_l0_conv_w, v_l0_conv_b, v_l0_conv_ln_g, v_l0_conv_ln_b, v_l0_q_norm, v_l0_kv_norm, v_l0_w_uq, v_l0_w_ukv, v_l0_w_out, v_l0_ffn_norm, v_l0_w_up, v_l0_ffn_conv_w, v_l0_ffn_conv_b, v_l0_w_down, v_l1_mix_norm, v_l1_w_in, v_l1_log_dt, v_l1_a_re, v_l1_a_im, v_l1_b_re, v_l1_b_im, v_l1_c_re, v_l1_c_im, v_l1_d, v_l1_w_glu, v_l1_b_glu, v_l1_ffn_norm, v_l1_w_up, v_l1_ffn_conv_w, v_l1_ffn_conv_b, v_l1_w_down, v_final_norm):
    args = dict(locals())
    w = {n: args[n] for n in WEIGHTS}
    m = {n: args['m_' + n] for n in WEIGHTS}
    v = {n: args['v_' + n] for n in WEIGHTS}
    payload = lambda n: w[n] if n in TAPS else w[n].astype(BF16)
    got = dict(zip(FIRST, _exchange([payload(n) for n in FIRST], ['gather'] * len(FIRST), "gather_first")))
    rep = {n: w[n] for n in REPLICATED}
    rest, got['l0_conv_w'] = _exchange_start([payload(n) for n in REST], ['gather'] * len(REST), got['l0_conv_w'],
                                             "gather_rest_start")
    wait_rest = lambda after: dict(zip(REST, _exchange_wait(rest, after, "gather_rest_wait")))
    flights = []

    def send_grads(names, gsh, carry):
        tag = "grads_" + names[0][:2]
        flight, carry = _exchange_start([gsh[n] for n in names], ['scatter'] * len(names), carry, tag + "_start")
        flights.append((names, flight, tag + "_wait"))
        return carry

    loss, dx, grads, gsh = _local_step(x[0], loss_target[0], rep, got, wait_rest, send_grads)

    last = _exchange([gsh[n] for n in FIRST] + [_flat([grads[n] for n in REPLICATED])],
                     ['scatter'] * len(FIRST) + ['gather'], "exchange_grads")
    recv = dict(zip(FIRST, last))
    for names, flight, name in flights:
        recv.update(zip(names, _exchange_wait(flight, last[-1], name)))
    res = [dict(), dict(), dict(), dict()]
    for n in SHARDED:
        for kind, a in enumerate(_adamw(recv[n], w[n], m[n], v[n], "adamw_" + n)):
            res[kind][n] = a
    flatr = lambda d: _flat([d[n] for n in REPLICATED])
    rp_out = _adamw(last[-1], flatr(w), flatr(m), flatr(v), "adamw_replicated")
    for kind in range(4):
        for n, a in zip(REPLICATED, _unflat(rp_out[kind], [w[n].shape for n in REPLICATED])):
            res[kind][n] = a
    total = lax.psum(loss, ("x", "y", "c"))
    return (total, dx[None], *[res[0][n] for n in WEIGHTS], *[res[1][n] for n in WEIGHTS],
            *[res[2][n] for n in WEIGHTS], *[res[3][n] for n in WEIGHTS])
```

```python
import functools

import jax
import jax.numpy as jnp
from jax import lax
from jax.experimental import pallas as pl
from jax.experimental.pallas import tpu as pltpu

F32 = jnp.float32
BF16 = jnp.bfloat16

N_DEV = 8
D_MODEL = 1024
EPS = 1e-6
LN_EPS = 1e-5
CONV_WIDTH = 512
CONV_K = 31
N_HEADS = 8
QK_NOPE = 64
QK_ROPE = 32
V_DIM = 64
HEAD_PAD = 128
Q_RANK = 256
KV_RANK = 128
ROPE_BASE = 10000.0
IN_EVEN = 2 * CONV_WIDTH + Q_RANK + KV_RANK + QK_ROPE
IN_PAD = 1536
KR_LANE = 64
SSM_WIDTH = 512
SSM_GROUP = 16
SSM_GROUPS = 32
SSM_STATE = 64
SSM_CH = SSM_GROUPS * SSM_STATE
D_FF = 2816
FF_SHARD = 2 * D_FF // N_DEV
FF_HALF = N_DEV // 2
FFN_K = 3
TAP_ROWS = 8
ADAM_LR, ADAM_B1, ADAM_B2, ADAM_EPS, ADAM_WD, ADAM_STEP = 0.001, 0.9, 0.999, 1e-08, 0.01, 10

WEIGHTS = ['l0_mix_norm', 'l0_w_in', 'l0_conv_w', 'l0_conv_b', 'l0_conv_ln_g', 'l0_conv_ln_b', 'l0_q_norm',
           'l0_kv_norm', 'l0_w_uq', 'l0_w_ukv', 'l0_w_out', 'l0_ffn_norm', 'l0_w_up', 'l0_ffn_conv_w',
           'l0_ffn_conv_b', 'l0_w_down', 'l1_mix_norm', 'l1_w_in', 'l1_log_dt', 'l1_a_re', 'l1_a_im', 'l1_b_re',
           'l1_b_im', 'l1_c_re', 'l1_c_im', 'l1_d', 'l1_w_glu', 'l1_b_glu', 'l1_ffn_norm', 'l1_w_up',
           'l1_ffn_conv_w', 'l1_ffn_conv_b', 'l1_w_down', 'final_norm']
SHARDED = {'l0_w_in': 1, 'l0_conv_w': 1, 'l0_w_uq': 1, 'l0_w_ukv': 1, 'l0_w_out': 0, 'l0_w_up': 1,
           'l0_ffn_conv_w': 1, 'l0_w_down': 0, 'l1_w_in': 0, 'l1_w_glu': 1, 'l1_w_up': 1, 'l1_ffn_conv_w': 1,
           'l1_w_down': 0}
TAPS = ('l0_conv_w', 'l0_ffn_conv_w', 'l1_ffn_conv_w')
REPLICATED = [n for n in WEIGHTS if n not in SHARDED]


def _tile(n, cands):
    for c in cands:
        if n % c == 0:
            return c
    return n


def _same(g):
    return g


def _mm(a, b, *, ta=False, tb=False, res=None, out_dtype=F32, name, ga=None, gb=None, go=None, groups=1,
        sparse=None):
    a2, b2 = (a.shape[1:] if ga else a.shape), (b.shape[1:] if gb else b.shape)
    m, kd = (a2[1], a2[0]) if ta else a2
    kd2, n = (b2[1], b2[0]) if tb else b2
    assert kd == kd2, (a.shape, b.shape, ta, tb)
    tm = _tile(m, (1024, 512, 256, 128))
    tn = _tile(n, (512, 384, 256, 128))
    tk = _tile(kd, (2048, 1024, 512, 256, 128) if ta else (1024, 512, 256, 128))
    nk = kd // tk
    a_m = b_n = k_of = None
    if sparse is not None:
        (tm, tn, tk), (m, n), nk = sparse['tiles'], sparse['out'], sparse['nk']
        a_m, b_n, k_of = sparse.get('a_m'), sparse.get('b_n'), sparse.get('k_of')
    a_m = a_m or (lambda i, j: i)
    b_n = b_n or (lambda i, j: j)
    k_of = k_of or (lambda i, j, k: k)
    summed = go is None and ga is not None and gb is not None
    assert summed or go is not None or (ga is None and gb is None)
    nkk = nk
    dn = (((0 if ta else 1,), (1 if tb else 0,)), ((), ()))

    def body(*refs):
        if res is None:
            a_ref, b_ref, o_ref, acc_ref = refs
            r_ref = None
        else:
            a_ref, b_ref, r_ref, o_ref, acc_ref = refs
        k = pl.program_id(3)
        if summed:
            p = lax.dot_general(a_ref[0].astype(BF16), b_ref[0].astype(BF16), dn, preferred_element_type=F32)
            for g in range(1, groups):
                p = p + lax.dot_general(a_ref[g].astype(BF16), b_ref[g].astype(BF16), dn, preferred_element_type=F32)
        else:
            p = lax.dot_general(a_ref[...].astype(BF16), b_ref[...].astype(BF16), dn, preferred_element_type=F32)

        @pl.when(k == 0)
        def _():
            acc_ref[...] = p

        @pl.when(k > 0)
        def _():
            acc_ref[...] += p

        @pl.when(k == nkk - 1)
        def _():
            out = acc_ref[...]
            if r_ref is not None:
                out = out + r_ref[...]
            o_ref[...] = out.astype(out_dtype)

    def spec(shape2, idx2, gmap):
        if gmap is None:
            return pl.BlockSpec(shape2, lambda g, i, j, kk: idx2(i, j, kk))
        if summed:
            return pl.BlockSpec((groups,) + shape2, lambda g, i, j, kk: (gmap(0) // groups,) + idx2(i, j, kk))
        return pl.BlockSpec((None,) + shape2, lambda g, i, j, kk: (gmap(g),) + idx2(i, j, kk))

    a_idx = (lambda i, j, k: (k_of(i, j, k), a_m(i, j))) if ta else (lambda i, j, k: (a_m(i, j), k_of(i, j, k)))
    b_idx = (lambda i, j, k: (b_n(i, j), k_of(i, j, k))) if tb else (lambda i, j, k: (k_of(i, j, k), b_n(i, j)))
    a_spec = spec((tk, tm) if ta else (tm, tk), a_idx, ga)
    b_spec = spec((tn, tk) if tb else (tk, tn), b_idx, gb)
    o_spec = spec((tm, tn), lambda i, j, k: (i, j), go)
    in_specs, args = [a_spec, b_spec], [a, b]
    if res is not None:
        in_specs.append(o_spec)
        args.append(res)
    out_shape = (groups, m, n) if go else (m, n)
    return pl.pallas_call(
        body, grid=(groups if go else 1, m // tm, n // tn, nkk), in_specs=in_specs, out_specs=o_spec,
        out_shape=jax.ShapeDtypeStruct(out_shape, out_dtype), scratch_shapes=[pltpu.VMEM((tm, tn), F32)],
        compiler_params=pltpu.CompilerParams(dimension_semantics=("parallel", "parallel", "parallel", "arbitrary")),
        name=name)(*args)


def _rowwise(fn, rows, bcasts, row_outs, red_outs, *, ts, name):
    s = rows[0][0].shape[0]
    nr, nb, nro, nre = len(rows), len(bcasts), len(row_outs), len(red_outs)

    def body(*refs):
        i = pl.program_id(0)
        outs = fn(*[r[...] for r in refs[:nr + nb]])
        if not isinstance(outs, (tuple, list)):
            outs = (outs,)
        o_refs = refs[nr + nb:]
        for q in range(nro):
            o_refs[q][...] = outs[q].astype(o_refs[q].dtype)
        for q in range(nro, nro + nre):
            @pl.when(i == 0)
            def _(q=q):
                o_refs[q][...] = outs[q]

            @pl.when(i > 0)
            def _(q=q):
                o_refs[q][...] += outs[q]

    in_specs = [pl.BlockSpec((ts, w), functools.partial(lambda i, cb: (i, cb), cb=cb)) for (_, w, cb) in rows]
    in_specs += [pl.BlockSpec(b.shape, functools.partial(lambda i, nd: (0,) * nd, nd=b.ndim)) for b in bcasts]
    out_specs = [pl.BlockSpec((ts, w), lambda i: (i, 0)) for (w, _) in row_outs]
    out_specs += [pl.BlockSpec((r, w), lambda i: (0, 0)) for (r, w) in red_outs]
    out_shape = [jax.ShapeDtypeStruct((s, w), dt) for (w, dt) in row_outs]
    out_shape += [jax.ShapeDtypeStruct((r, w), F32) for (r, w) in red_outs]
    return pl.pallas_call(
        body, grid=(s // ts,), in_specs=in_specs, out_specs=out_specs, out_shape=out_shape,
        compiler_params=pltpu.CompilerParams(dimension_semantics=("arbitrary",)), name=name,
    )(*[r[0] for r in rows], *bcasts)


def _rms(x, g):
    return x * lax.rsqrt(jnp.mean(x * x, axis=-1, keepdims=True) + EPS) * g


def _rms_fwd(x, g, name):
    return _rowwise(lambda xv, gv: _rms(xv, gv), [(x, x.shape[1], 0)], [g.reshape(1, -1)],
                    [(x.shape[1], BF16)], [], ts=512, name=name)[0]


def _rms_bwd(x, g, dxn, dres, name):
    d = x.shape[1]

    def fn(xv, dv, rv, gv):
        _, vjp = jax.vjp(_rms, xv, gv)
        dx, dg = vjp(dv.astype(F32))
        return rv + dx, dg

    return _rowwise(fn, [(x, d, 0), (dxn, d, 0), (dres, d, 0)], [g.reshape(1, -1)], [(d, F32)], [(1, d)],
                    ts=512, name=name)


def _cspec(mode, off, rows, width, rowblk, n_rb):
    if mode == 'col':
        return pl.BlockSpec((rows, width), lambda jc, i: (rowblk(i), off + jc))
    return pl.BlockSpec((rows, width), lambda jc, i: ((off + jc) * n_rb + rowblk(i), 0))


TAP_SPREAD = 24


def _stage_shape(k_taps, n, tc):
    return (8, n + TAP_SPREAD, tc) if k_taps > 8 else (1, 8, 128)


def _tap_windows(ref, offsets, n, stage):
    if len(offsets) <= 8:
        return [functools.partial(lambda v: v, ref[pl.ds(o, n), :]) for o in offsets]
    lows = {}
    for o in offsets:
        lows[o % 8] = min(o, lows.get(o % 8, o))
    for r, lo in lows.items():
        span = n + max(o for o in offsets if o % 8 == r) - lo
        stage[r, pl.ds(0, span), :] = ref[pl.ds(lo, span), :]
    return [functools.partial(lambda o: stage[o % 8, pl.ds(o - lows[o % 8], n), :], o) for o in offsets]


def _conv_fwd(xins, w, woffs, params, poffs, pre, post, outs, *, mode, s, k_taps, hb, ts, tc, rc, n_cb, name):
    n_s, nx, ncv, npar, no = s // ts, len(xins), len(woffs), len(params), len(outs)
    rpb = ts // hb
    kp = TAP_ROWS * ((k_taps + TAP_ROWS - 1) // TAP_ROWS)
    pr = 1 if mode == 'col' else TAP_ROWS

    def body(*refs):
        mains, halos = refs[:nx], refs[nx:2 * nx]
        w_refs = refs[2 * nx:2 * nx + ncv]
        p_refs = refs[2 * nx + ncv:2 * nx + ncv + npar]
        o_refs = refs[2 * nx + ncv + npar:2 * nx + ncv + npar + no]
        u_s = refs[2 * nx + ncv + npar + no:2 * nx + ncv + npar + no + ncv]
        win_s = refs[2 * nx + ncv + npar + no + ncv:2 * nx + ncv + npar + no + 2 * ncv]
        stage_s = refs[2 * nx + ncv + npar + no + 2 * ncv:]
        i = pl.program_id(1)
        um = pre(*[r[...].astype(F32) for r in mains])
        uh = pre(*[r[...].astype(F32) for r in halos])
        first = (i > 0).astype(F32)
        for q in range(ncv):
            u_s[q][pl.ds(0, hb), :] = uh[q] * first
            u_s[q][pl.ds(hb, ts), :] = um[q]
        pv = [r[0:1, :] for r in p_refs]

        def chunk(ci, carry):
            r0 = pl.multiple_of(ci * rc, rc)
            cs = []
            for q in range(ncv):
                win_s[q][...] = u_s[q][pl.ds(r0, rc + hb), :]
                acc = jnp.zeros((rc, tc), F32)
                offsets = [hb - (k_taps - 1) + t for t in range(k_taps)]
                for t, ut in enumerate(_tap_windows(win_s[q], offsets, rc, stage_s[q])):
                    acc = acc + w_refs[q][pl.ds(t, 1), :] * ut()
                cs.append(acc)
            res = post(cs, pv)
            for q in range(no):
                o_refs[q][pl.ds(r0, rc), :] = res[q].astype(o_refs[q].dtype)
            return carry

        lax.fori_loop(0, ts // rc, chunk, 0)

    main = lambda i: i
    prev = lambda i: jnp.maximum(i * rpb - 1, 0)
    zero = lambda i: 0
    in_specs = [_cspec(mode, off, ts, tc, main, n_s) for _, off in xins]
    in_specs += [_cspec(mode, off, hb, tc, prev, s // hb) for _, off in xins]
    in_specs += [_cspec(mode, off, kp, tc, zero, 1) for off in woffs]
    in_specs += [_cspec(mode, off, pr, tc, zero, 1) for off in poffs]
    out_specs = [_cspec(mode, 0, ts, tc, main, n_s) for _ in outs]
    oshape = (s, n_cb * tc) if mode == 'col' else (n_cb * s, tc)
    out_shape = [jax.ShapeDtypeStruct(oshape, dt) for dt in outs]
    return pl.pallas_call(
        body, grid=(n_cb, n_s), in_specs=in_specs, out_specs=out_specs, out_shape=out_shape,
        scratch_shapes=[pltpu.VMEM((hb + ts, tc), F32) for _ in range(ncv)]
        + [pltpu.VMEM((hb + rc, tc), F32) for _ in range(ncv)]
        + [pltpu.VMEM(_stage_shape(k_taps, rc, tc), F32) for _ in range(ncv)],
        compiler_params=pltpu.CompilerParams(dimension_semantics=("parallel", "arbitrary")), name=name,
    )(*[a for a, _ in xins], *[a for a, _ in xins], *([w] * ncv), *params)


def _conv_bwd(xins, w, woffs, params, poffs, pre, post, douts, dx_dtype, *, mode, s, k_taps, hb, ts, tc, rc, n_cb, name):
    n_s, nx, ncv, npar, ndo = s // ts, len(xins), len(woffs), len(params), len(douts)
    rpb = ts // hb
    n_hb = s // hb
    kp = TAP_ROWS * ((k_taps + TAP_ROWS - 1) // TAP_ROWS)
    pr = 1 if mode == 'col' else TAP_ROWS

    def body(*refs):
        pos = 0

        def take(n):
            nonlocal pos
            out = refs[pos:pos + n]
            pos += n
            return out

        mains, prevs, nexts = take(nx), take(nx), take(nx)
        d_mains, d_nexts = take(ndo), take(ndo)
        w_refs, p_refs = take(ncv), take(npar)
        dx_refs, dw_refs, dp_refs = take(nx), take(ncv), take(npar)
        u_s, d_s = take(ncv), take(ndo)
        win_s, dcw_s, dwa_s = take(ncv), take(ncv), take(ncv)
        stu_s, std_s = take(ncv), take(ncv)
        i = pl.program_id(1)
        um = pre(*[r[...].astype(F32) for r in mains])
        up = pre(*[r[...].astype(F32) for r in prevs])
        un = pre(*[r[...].astype(F32) for r in nexts])
        first = (i > 0).astype(F32)
        last = (i < n_s - 1).astype(F32)
        for q in range(ncv):
            u_s[q][pl.ds(0, hb), :] = up[q] * first
            u_s[q][pl.ds(hb, ts), :] = um[q]
            u_s[q][pl.ds(hb + ts, hb), :] = un[q]
            dwa_s[q][...] = jnp.zeros((kp * 8, tc), F32)
        for q in range(ndo):
            d_s[q][pl.ds(0, ts), :] = d_mains[q][...].astype(F32)
            d_s[q][pl.ds(ts, hb), :] = d_nexts[q][...].astype(F32) * last
        pv = [r[0:1, :] for r in p_refs]
        fwd_offsets = [hb - (k_taps - 1) + t for t in range(k_taps)]

        def chunk(ci, dpar):
            r0 = pl.multiple_of(ci * rc, rc)
            c_own, c_next, shifted = [], [], []
            for q in range(ncv):
                win_s[q][...] = u_s[q][pl.ds(r0, rc + 2 * hb), :]
                acc = jnp.zeros((rc + hb, tc), F32)
                taps = _tap_windows(win_s[q], fwd_offsets, rc + hb, stu_s[q])
                for t, ut in enumerate(taps):
                    acc = acc + w_refs[q][pl.ds(t, 1), :] * ut()
                c_own.append(acc[:rc])
                c_next.append(acc[rc:])
                shifted.append(taps)
            _, vjp_o = jax.vjp(lambda c, p: tuple(post(c, p)), c_own, pv)
            dc_own, dp_own = vjp_o(tuple(r[pl.ds(r0, rc), :] for r in d_s))
            _, vjp_n = jax.vjp(lambda c: tuple(post(c, pv)), c_next)
            (dc_next,) = vjp_n(tuple(r[pl.ds(r0 + rc, hb), :] for r in d_s))
            dus = []
            for q in range(ncv):
                dcw_s[q][pl.ds(0, rc), :] = dc_own[q]
                dcw_s[q][pl.ds(rc, hb), :] = dc_next[q]
                acc = jnp.zeros((rc, tc), F32)
                bwd_offsets = [k_taps - 1 - t for t in range(k_taps)]
                for t, dct in enumerate(_tap_windows(dcw_s[q], bwd_offsets, rc, std_s[q])):
                    acc = acc + w_refs[q][pl.ds(t, 1), :] * dct()
                    prod = dc_own[q] * shifted[q][t]()[:rc]
                    dwa_s[q][pl.ds(8 * t, 8), :] += jnp.sum(prod.reshape(rc // 8, 8, tc), axis=0)
                dus.append(acc)
            xm = [r[pl.ds(r0, rc), :].astype(F32) for r in mains]
            _, vjp_p = jax.vjp(lambda *xv: tuple(pre(*xv)), *xm)
            dxs = vjp_p(tuple(dus))
            for q in range(nx):
                dx_refs[q][pl.ds(r0, rc), :] = dxs[q].astype(dx_refs[q].dtype)
            return tuple(a + b for a, b in zip(dpar, dp_own))

        dpar = lax.fori_loop(0, ts // rc, chunk, tuple(jnp.zeros((1, tc), F32) for _ in range(npar)))
        for q in range(ncv):
            @pl.when(i == 0)
            def _(q=q):
                dw_refs[q][...] = jnp.zeros((kp, tc), F32)

            for t in range(k_taps):
                dw_refs[q][pl.ds(t, 1), :] += jnp.sum(dwa_s[q][pl.ds(8 * t, 8), :], axis=0, keepdims=True)
        for q in range(npar):
            @pl.when(i == 0)
            def _(q=q):
                dp_refs[q][...] = jnp.zeros((pr, tc), F32)

            dp_refs[q][0:1, :] += dpar[q]

    main = lambda i: i
    prev = lambda i: jnp.maximum(i * rpb - 1, 0)
    nxt = lambda i: jnp.minimum((i + 1) * rpb, n_hb - 1)
    zero = lambda i: 0
    in_specs = [_cspec(mode, off, ts, tc, main, n_s) for _, off in xins]
    in_specs += [_cspec(mode, off, hb, tc, prev, n_hb) for _, off in xins]
    in_specs += [_cspec(mode, off, hb, tc, nxt, n_hb) for _, off in xins]
    in_specs += [_cspec(mode, off, ts, tc, main, n_s) for _, off in douts]
    in_specs += [_cspec(mode, off, hb, tc, nxt, n_hb) for _, off in douts]
    in_specs += [_cspec(mode, off, kp, tc, zero, 1) for off in woffs]
    in_specs += [_cspec(mode, off, pr, tc, zero, 1) for off in poffs]
    out_specs = [_cspec(mode, 0, ts, tc, main, n_s) for _ in xins]
    out_specs += [_cspec(mode, 0, kp, tc, zero, 1) for _ in woffs]
    out_specs += [_cspec(mode, 0, pr, tc, zero, 1) for _ in params]

    def shape(rows):
        return (rows, n_cb * tc) if mode == 'col' else (n_cb * rows, tc)

    out_shape = [jax.ShapeDtypeStruct(shape(s), dx_dtype) for _ in xins]
    out_shape += [jax.ShapeDtypeStruct(shape(kp), F32) for _ in woffs]
    out_shape += [jax.ShapeDtypeStruct(shape(pr), F32) for _ in params]
    xa = [a for a, _ in xins]
    da = [a for a, _ in douts]
    return pl.pallas_call(
        body, grid=(n_cb, n_s), in_specs=in_specs, out_specs=out_specs, out_shape=out_shape,
        scratch_shapes=[pltpu.VMEM((hb + ts + hb, tc), F32) for _ in range(ncv)]
        + [pltpu.VMEM((ts + hb, tc), F32) for _ in range(ndo)]
        + [pltpu.VMEM((rc + 2 * hb, tc), F32) for _ in range(ncv)]
        + [pltpu.VMEM((rc + hb, tc), F32) for _ in range(ncv)]
        + [pltpu.VMEM((kp * 8, tc), F32) for _ in range(ncv)]
        + [pltpu.VMEM(_stage_shape(k_taps, rc + hb, tc), F32) for _ in range(ncv)]
        + [pltpu.VMEM(_stage_shape(k_taps, rc, tc), F32) for _ in range(ncv)],
        compiler_params=pltpu.CompilerParams(dimension_semantics=("parallel", "arbitrary")), name=name,
    )(*xa, *xa, *xa, *da, *da, *([w] * ncv), *params)


def _glu_pre(a, g):
    return [a * jax.nn.sigmoid(g)]


def _ln_silu_post(cs, ps):
    c = cs[0] + ps[0]
    mu = jnp.mean(c, axis=-1, keepdims=True)
    var = jnp.mean(jnp.square(c - mu), axis=-1, keepdims=True)
    y = (c - mu) * lax.rsqrt(var + LN_EPS) * ps[1] + ps[2]
    return [jax.nn.silu(y)]


def _pair_pre(a, b):
    return [a, b]


def _gate_post(cs, ps):
    return [jax.nn.silu(cs[0] + ps[0]) * (cs[1] + ps[1])]


def _conva(s):
    return dict(mode='col', s=s, k_taps=CONV_K, hb=32, ts=512, tc=CONV_WIDTH, rc=64, n_cb=1)


def _gate(s):
    return dict(mode='row', s=s, k_taps=FFN_K, hb=8, ts=512, tc=FF_SHARD, rc=32, n_cb=FF_HALF)


def _rope_tables(s):
    half = QK_ROPE // 2
    inv = ROPE_BASE ** (-jnp.arange(half, dtype=F32) / half)
    ang = jnp.arange(s, dtype=F32)[:, None] * inv[None, :]
    cos, sin = jnp.cos(ang), jnp.sin(ang)
    z = lambda n: jnp.zeros((s, n), F32)
    c = jnp.concatenate([jnp.ones((s, QK_NOPE), F32), cos, cos, z(HEAD_PAD - QK_NOPE - QK_ROPE)], axis=1)
    s1 = jnp.concatenate([z(QK_NOPE), -sin, z(HEAD_PAD - QK_NOPE - half)], axis=1)
    s2 = jnp.concatenate([z(QK_NOPE + half), sin, z(HEAD_PAD - QK_NOPE - QK_ROPE)], axis=1)
    return c, s1, s2


def _rot(t, c, s1, s2):
    half = QK_ROPE // 2
    return t * c + pltpu.roll(t, HEAD_PAD - half, 1) * s1 + pltpu.roll(t, half, 1) * s2


def _rot_t(d, c, s1, s2):
    half = QK_ROPE // 2
    return d * c + pltpu.roll(d * s1, half, 1) + pltpu.roll(d * s2, HEAD_PAD - half, 1)


def _heads(v):
    return [v[:, h * HEAD_PAD:(h + 1) * HEAD_PAD] for h in range(N_HEADS)]


def _rope_fwd(qf, kvf, hmat, tabs, name):
    w = N_HEADS * HEAD_PAD

    def fn(q, k, v, kr, c, s1, s2):
        krr = _rot(kr, c, s1, s2)
        qo = jnp.concatenate([_rot(t, c, s1, s2) for t in _heads(q)], axis=1)
        ko = jnp.concatenate([t + krr for t in _heads(k)], axis=1)
        lane = lax.broadcasted_iota(jnp.int32, v.shape, 1) & (HEAD_PAD - 1)
        return qo, ko, jnp.where(lane == ONES_LANE, 1.0, v)

    rows = [(qf, w, 0), (kvf, w, 0), (kvf, w, 1), (hmat, HEAD_PAD, IN_PAD // HEAD_PAD - 1)]
    rows += [(t, HEAD_PAD, 0) for t in tabs]
    return _rowwise(fn, rows, [], [(w, BF16)] * 3, [], ts=512, name=name)


def _rope_bwd(dq, dk, dv, tabs, name):
    w = N_HEADS * HEAD_PAD

    def fn(dqv, dkv, dvv, c, s1, s2):
        dqo = jnp.concatenate([_rot_t(t, c, s1, s2) for t in _heads(dqv)], axis=1)
        ksum = functools.reduce(lambda a, b: a + b, _heads(dkv))
        return dqo, jnp.concatenate([dkv, dvv], axis=1), _rot_t(ksum, c, s1, s2)

    rows = [(dq, w, 0), (dk, w, 0), (dv, w, 0)] + [(t, HEAD_PAD, 0) for t in tabs]
    return _rowwise(fn, rows, [], [(w, BF16), (2 * w, BF16), (HEAD_PAD, F32)], [], ts=512, name=name)


ATT_Q = 1024
ATT_SUB = 512
ATT_KV = 1024
ATT_SCALE = (QK_NOPE + QK_ROPE) ** -0.5
LOG2E = 1.4426950408889634
ATT_C2 = ATT_SCALE * LOG2E
ONES_LANE = V_DIM


def _nt(a, b):
    return lax.dot_general(a, b, (((1,), (1,)), ((), ())), preferred_element_type=F32)


def _lanes(x, w):
    return x if w == HEAD_PAD else jnp.tile(x, (1, w // HEAD_PAD))


def _tri(w, transposed):
    r = lax.broadcasted_iota(jnp.int32, (w, w), 0)
    c = lax.broadcasted_iota(jnp.int32, (w, w), 1)
    return (r <= c) if transposed else (c <= r)


def _attn_fwd(q, k, v, name):
    s = q.shape[0]
    tq, kvc = min(ATT_Q, s), min(ATT_KV, s)
    nsub, per = tq // ATT_SUB, tq // kvc

    def body(q_ref, k_ref, v_ref, o_ref, lse_ref, m_s, acc_s):
        i = pl.program_id(1)
        m_s[...] = jnp.full((tq, HEAD_PAD), -jnp.inf, F32)
        acc_s[...] = jnp.zeros((tq, HEAD_PAD), F32)

        def update(r0, n, kb, vb, diag):
            rows = pl.ds(r0, n)
            w = kb.shape[0]
            sc = _nt(q_ref[rows, :], kb)
            if diag:
                sc = jnp.where(_tri(w, False), sc, -jnp.inf)
            m_prev = m_s[rows, :]
            m_next = jnp.maximum(m_prev, jnp.max(sc, axis=1, keepdims=True))
            p = jnp.exp2((sc - _lanes(m_next, w)) * ATT_C2)
            alpha = jnp.exp2((m_prev - m_next) * ATT_C2)
            acc_s[rows, :] = alpha * acc_s[rows, :] + jnp.dot(p.astype(BF16), vb, preferred_element_type=F32)
            m_s[rows, :] = m_next

        def below(j, carry):
            at = pl.ds(pl.multiple_of(j * kvc, kvc), kvc)
            update(0, tq, k_ref[at, :], v_ref[at, :], False)
            return carry

        lax.fori_loop(0, i * per, below, 0)
        for r in range(nsub):
            for c in range(r + 1):
                at = pl.ds(pl.multiple_of(i * tq + c * ATT_SUB, ATT_SUB), ATT_SUB)
                update(r * ATT_SUB, ATT_SUB, k_ref[at, :], v_ref[at, :], c == r)
        l = acc_s[:, ONES_LANE:ONES_LANE + 1]
        o_ref[...] = (acc_s[...] / l).astype(BF16)
        lse_ref[...] = m_s[...] * ATT_SCALE + jnp.log(l)

    q_spec = pl.BlockSpec((tq, HEAD_PAD), lambda h, i: (i, h))
    kv_spec = pl.BlockSpec((s, HEAD_PAD), lambda h, i: (0, h))
    return pl.pallas_call(
        body, grid=(N_HEADS, s // tq), in_specs=[q_spec, kv_spec, kv_spec], out_specs=[q_spec, q_spec],
        out_shape=[jax.ShapeDtypeStruct(q.shape, BF16), jax.ShapeDtypeStruct(q.shape, F32)],
        scratch_shapes=[pltpu.VMEM((tq, HEAD_PAD), F32)] * 2,
        compiler_params=pltpu.CompilerParams(dimension_semantics=("parallel", "arbitrary")), name=name,
    )(q, k, v)


def _attn_rows(cat, dcat, lse, name):
    s = lse.shape[0]
    tq, kvc = min(ATT_Q, s), min(ATT_KV, s)
    per = tq // kvc
    ob = CONV_WIDTH // HEAD_PAD

    def body(o_ref, do_ref, lse_ref, lser_ref, dltr_ref):
        dl = jnp.broadcast_to(jnp.sum(do_ref[...] * o_ref[...].astype(F32), axis=1, keepdims=True), (tq, HEAD_PAD))
        l2 = lse_ref[...] * LOG2E
        for c in range(per):
            lser_ref[c] = jnp.transpose(l2[c * kvc:(c + 1) * kvc])[0:8, :]
            dltr_ref[c] = jnp.transpose(dl[c * kvc:(c + 1) * kvc])[0:8, :]

    q_spec = pl.BlockSpec((tq, HEAD_PAD), lambda h, i: (i, h))
    o_spec = pl.BlockSpec((tq, HEAD_PAD), lambda h, i: (i, ob + h))
    row_spec = pl.BlockSpec((None, per, 8, kvc), lambda h, i: (h, i, 0, 0))
    rows = jax.ShapeDtypeStruct((N_HEADS, s // kvc, 8, kvc), F32)
    return pl.pallas_call(
        body, grid=(N_HEADS, s // tq), in_specs=[o_spec, o_spec, q_spec], out_specs=[row_spec, row_spec],
        out_shape=[rows, rows],
        compiler_params=pltpu.CompilerParams(dimension_semantics=("parallel", "parallel")), name=name,
    )(cat, dcat, lse)


def _attn_bwd(q, k, v, dcat, lse_r, dlt_r, name):
    s = q.shape[0]
    tk, kvc = min(ATT_Q, s), min(ATT_KV, s)
    nsub, per, n_chunks = tk // ATT_SUB, tk // kvc, s // kvc
    n_j = s // tk
    ob = CONV_WIDTH // HEAD_PAD

    def body(k_ref, v_ref, q_ref, do_ref, lse_ref, dl_ref, dk_ref, dv_ref, dq_ref, dk_s, dv_s):
        j = pl.program_id(1)
        dk_s[...] = jnp.zeros((tk, HEAD_PAD), F32)
        dv_s[...] = jnp.zeros((tk, HEAD_PAD), F32)

        @pl.when(j == 0)
        def _():
            dq_ref[...] = jnp.zeros((s, HEAD_PAD), F32)

        def update(r0, n, at, lrow, drow, diag):
            rows = pl.ds(r0, n)
            qb, dob = q_ref[at, :], do_ref[at, :].astype(BF16)
            sc = _nt(k_ref[rows, :], qb)
            if diag:
                sc = jnp.where(_tri(qb.shape[0], True), sc, -jnp.inf)
            p = jnp.exp2(sc * ATT_C2 - lrow)
            dp = _nt(v_ref[rows, :], dob)
            ds = (p * (dp - drow)).astype(BF16)
            dv_s[rows, :] += jnp.dot(p.astype(BF16), dob, preferred_element_type=F32)
            dk_s[rows, :] += jnp.dot(ds, qb, preferred_element_type=F32)
            dq_ref[at, :] += lax.dot_general(ds, k_ref[rows, :], (((0,), (0,)), ((), ())), preferred_element_type=F32)

        def above(ic, carry):
            at = pl.ds(pl.multiple_of(ic * kvc, kvc), kvc)
            update(0, tk, at, lse_ref[ic, 0:1, :], dl_ref[ic, 0:1, :], False)
            return carry

        lax.fori_loop((j + 1) * per, n_chunks, above, 0)
        for r in range(nsub):
            for c in range(r, nsub):
                at = pl.ds(pl.multiple_of(j * tk + c * ATT_SUB, ATT_SUB), ATT_SUB)
                ic = j * per + (c * ATT_SUB) // kvc
                lo = (c * ATT_SUB) % kvc
                update(r * ATT_SUB, ATT_SUB, at, lse_ref[ic, 0:1, lo:lo + ATT_SUB], dl_ref[ic, 0:1, lo:lo + ATT_SUB],
                       c == r)
        dk_ref[...] = dk_s[...] * ATT_SCALE
        dv_ref[...] = dv_s[...]

        @pl.when(j == n_j - 1)
        def _():
            dq_ref[...] = dq_ref[...] * ATT_SCALE

    kv_spec = pl.BlockSpec((tk, HEAD_PAD), lambda h, j: (j, h))
    q_spec = pl.BlockSpec((s, HEAD_PAD), lambda h, j: (0, h))
    do_spec = pl.BlockSpec((s, HEAD_PAD), lambda h, j: (0, ob + h))
    row_spec = pl.BlockSpec((None, n_chunks, 8, kvc), lambda h, j: (h, 0, 0, 0))
    full = jax.ShapeDtypeStruct(q.shape, F32)
    dk, dv, dq = pl.pallas_call(
        body, grid=(N_HEADS, n_j), in_specs=[kv_spec, kv_spec, q_spec, do_spec, row_spec, row_spec],
        out_specs=[kv_spec, kv_spec, q_spec], out_shape=[full, full, full],
        scratch_shapes=[pltpu.VMEM((tk, HEAD_PAD), F32)] * 2,
        compiler_params=pltpu.CompilerParams(dimension_semantics=("parallel", "arbitrary")), name=name,
    )(k, v, q, dcat, lse_r, dlt_r)
    return dq, dk, dv


SCAN_T = 256
SCAN_C = 512


def _scan(b, lam, *, reverse, xs=None, name):
    s = b.shape[0]
    t = min(SCAN_T, s)
    n_t, n_c = s // t, SSM_CH // SCAN_C
    with_dlam = xs is not None

    def shift(a, d, row):
        if d >= 8:
            z = jnp.zeros((d, SCAN_C), F32)
            return jnp.concatenate([a[d:], z], axis=0) if reverse else jnp.concatenate([z, a[:t - d]], axis=0)
        if reverse:
            return jnp.where(row < t - d, pltpu.roll(a, t - d, 0), 0.0)
        return jnp.where(row >= d, pltpu.roll(a, d, 0), 0.0)

    def body(*refs):
        if with_dlam:
            b_ref, lam_ref, x_ref, o_ref, dl_ref, c_s = refs
        else:
            b_ref, lam_ref, o_ref, c_s = refs
        k = pl.program_id(0)

        @pl.when(k == 0)
        def _():
            c_s[...] = jnp.zeros((1, 2 * SSM_CH), F32)
            if with_dlam:
                dl_ref[...] = jnp.zeros((1, 2 * SSM_CH), F32)

        row = lax.broadcasted_iota(jnp.int32, (t, SCAN_C), 0)
        edge = (row == t - 1) if reverse else (row == 0)
        for ch in range(n_c):
            re = pl.ds(ch * SCAN_C, SCAN_C)
            im = pl.ds(SSM_CH + ch * SCAN_C, SCAN_C)
            lr = lam_ref[:, re]
            li = -lam_ref[:, im] if reverse else lam_ref[:, im]
            cr, ci = c_s[:, re], c_s[:, im]
            ar = b_ref[:, re] + jnp.where(edge, lr * cr - li * ci, 0.0)
            ai = b_ref[:, im] + jnp.where(edge, lr * ci + li * cr, 0.0)
            d = 1
            while d < t:
                sr, si = shift(ar, d, row), shift(ai, d, row)
                ar, ai = ar + lr * sr - li * si, ai + lr * si + li * sr
                lr, li = lr * lr - li * li, 2.0 * lr * li
                d *= 2
            o_ref[:, re] = ar.astype(o_ref.dtype)
            o_ref[:, im] = ai.astype(o_ref.dtype)
            if with_dlam:
                gr = jnp.where(edge, cr, shift(ar, 1, row))
                gi = jnp.where(edge, ci, shift(ai, 1, row))
                xr, xi = x_ref[:, re].astype(F32), x_ref[:, im].astype(F32)
                dl_ref[:, re] += jnp.sum(xr * gr + xi * gi, axis=0, keepdims=True)
                dl_ref[:, im] += jnp.sum(xr * gi - xi * gr, axis=0, keepdims=True)
            last = 0 if reverse else t - 1
            c_s[:, re] = ar[last:last + 1, :]
            c_s[:, im] = ai[last:last + 1, :]

    tm = (lambda k: (n_t - 1 - k, 0)) if reverse else (lambda k: (k, 0))
    blk = pl.BlockSpec((t, 2 * SSM_CH), tm)
    vec = pl.BlockSpec((1, 2 * SSM_CH), lambda k: (0, 0))
    in_specs, args = [blk, vec], [b, lam]
    out_specs, out_shape = [blk], [jax.ShapeDtypeStruct((s, 2 * SSM_CH), BF16)]
    if with_dlam:
        in_specs.append(blk)
        args.append(xs)
        out_specs.append(vec)
        out_shape.append(jax.ShapeDtypeStruct((1, 2 * SSM_CH), F32))
    return pl.pallas_call(
        body, grid=(n_t,), in_specs=in_specs, out_specs=out_specs, out_shape=out_shape,
        scratch_shapes=[pltpu.VMEM((1, 2 * SSM_CH), F32)],
        compiler_params=pltpu.CompilerParams(dimension_semantics=("arbitrary",)), name=name,
    )(*args)


def _s5_disc(log_dt, a_re, a_im, b_re, b_im):
    dt = jnp.exp(log_dt)[:, None]
    mag = jnp.exp(a_re * dt)
    lb_re, lb_im = mag * jnp.cos(a_im * dt), mag * jnp.sin(a_im * dt)
    den = a_re * a_re + a_im * a_im
    nr, ni = lb_re - 1.0, lb_im
    f_re = (nr * a_re + ni * a_im) / den
    f_im = (ni * a_re - nr * a_im) / den
    bb_re = f_re[..., None] * b_re - f_im[..., None] * b_im
    bb_im = f_re[..., None] * b_im + f_im[..., None] * b_re
    return lb_re, lb_im, bb_re, bb_im


def _bd(a):
    g, i, j = a.shape
    eye = jnp.eye(g, dtype=a.dtype)
    return (a[:, :, None, :] * eye[:, None, :, None]).reshape(g * i, g * j)


S5_TILE_GROUPS = HEAD_PAD // SSM_GROUP


def _s5_sparse(s):
    nb = SSM_GROUPS // S5_TILE_GROUPS
    cw, sw = S5_TILE_GROUPS * SSM_GROUP, S5_TILE_GROUPS * SSM_STATE
    tm = min(1024, s)
    return dict(
        expand=dict(tiles=(tm, sw, cw), out=(s, 2 * SSM_CH), nk=1, k_of=lambda i, j, k: j % nb),
        reduce=dict(tiles=(tm, cw, sw), out=(s, SSM_WIDTH), nk=2, k_of=lambda i, j, k: j + nb * k),
        wide_t=dict(tiles=(sw, cw, tm), out=(2 * SSM_CH, cw), nk=s // tm, b_n=lambda i, j: i % nb),
        narrow_t=dict(tiles=(cw, sw, tm), out=(cw, 2 * SSM_CH), nk=s // tm, a_m=lambda i, j: j % nb))


def _exchange(arrs, modes, name):
    n = len(arrs)
    shapes = [a.shape if md == 'scatter' else (N_DEV,) + a.shape for a, md in zip(arrs, modes)]

    def body(*refs):
        srcs, outs = refs[:n], refs[n:2 * n]
        send_sems, recv_sems, local_sems = refs[2 * n:]
        x, y, c = lax.axis_index("x"), lax.axis_index("y"), lax.axis_index("c")
        me = 4 * x + 2 * y + c

        def piece(q, slot):
            return srcs[q].at[slot] if modes[q] == 'scatter' else srcs[q]

        mine = [pltpu.make_async_copy(piece(q, me), outs[q].at[me], local_sems.at[q]) for q in range(n)]
        for cp in mine:
            cp.start()
        copies = []
        for r in range(1, N_DEV):
            px, py, pc = x ^ (r >> 2), y ^ ((r >> 1) & 1), c ^ (r & 1)
            peer = 4 * px + 2 * py + pc
            for q in range(n):
                copies.append(pltpu.make_async_remote_copy(
                    src_ref=piece(q, peer), dst_ref=outs[q].at[me], send_sem=send_sems.at[(r - 1) * n + q],
                    recv_sem=recv_sems.at[(r - 1) * n + q], device_id=(px, py, pc),
                    device_id_type=pl.DeviceIdType.MESH))
        for cp in copies:
            cp.start()
        for cp in copies:
            cp.wait_recv()
        for cp in copies:
            cp.wait_send()
        for cp in mine:
            cp.wait()

    any_spec = pl.BlockSpec(memory_space=pl.ANY)
    return pl.pallas_call(
        body, out_shape=[jax.ShapeDtypeStruct(sh, a.dtype) for sh, a in zip(shapes, arrs)],
        in_specs=[any_spec] * n, out_specs=[any_spec] * n,
        scratch_shapes=[pltpu.SemaphoreType.DMA(((N_DEV - 1) * n,)), pltpu.SemaphoreType.DMA(((N_DEV - 1) * n,)),
                        pltpu.SemaphoreType.DMA((n,))],
        compiler_params=pltpu.CompilerParams(has_side_effects=True), name=name,
    )(*arrs)


def _peers(x, y, c):
    out = []
    for r in range(1, N_DEV):
        px, py, pc = x ^ (r >> 2), y ^ ((r >> 1) & 1), c ^ (r & 1)
        out.append((r, (px, py, pc), 4 * px + 2 * py + pc))
    return out


def _exchange_start(arrs, modes, carry, name):
    n = len(arrs)
    shapes = [a.shape if md == 'scatter' else (N_DEV,) + a.shape for a, md in zip(arrs, modes)]
    lands = [lax.empty(sh, a.dtype) for sh, a in zip(shapes, arrs)]

    def body(*refs):
        srcs, zones = refs[:n], refs[n:2 * n]
        send_sems, recv_sems = refs[2 * n + 1], refs[2 * n + 2]
        x, y, c = lax.axis_index("x"), lax.axis_index("y"), lax.axis_index("c")
        me = 4 * x + 2 * y + c
        for r, pos, peer in _peers(x, y, c):
            for q in range(n):
                src = srcs[q].at[peer] if modes[q] == 'scatter' else srcs[q]
                pltpu.make_async_remote_copy(
                    src_ref=src, dst_ref=zones[q].at[me], send_sem=send_sems.at[(r - 1) * n + q],
                    recv_sem=recv_sems.at[(r - 1) * n + q], device_id=pos, device_id_type=pl.DeviceIdType.MESH).start()

    hbm = pl.BlockSpec(memory_space=pltpu.HBM)
    sem = pl.BlockSpec(memory_space=pltpu.SEMAPHORE)
    thru = arrs + lands + [carry]
    sems = pltpu.SemaphoreType.DMA(((N_DEV - 1) * n,))
    outs = pl.pallas_call(
        body, name=name, out_shape=(sems, sems, *[pltpu.HBM(a.shape, a.dtype) for a in thru]),
        in_specs=[hbm] * len(thru), out_specs=(sem, sem, *[hbm] * len(thru)),
        input_output_aliases={q: 2 + q for q in range(len(thru))},
        compiler_params=pltpu.CompilerParams(has_side_effects=pltpu.SideEffectType.DATAFLOW_SIDE_EFFECTING),
    )(*[pltpu.with_memory_space_constraint(a, pltpu.HBM) for a in thru])
    return dict(send=outs[0], recv=outs[1], srcs=list(outs[2:2 + n]), lands=list(outs[2 + n:2 + 2 * n]),
                modes=modes), outs[-1]


def _exchange_wait(flight, after, name):
    n = len(flight['srcs'])
    modes = flight['modes']

    def body(*refs):
        srcs, zones = refs[:n], refs[n:2 * n]
        send_sems, recv_sems = refs[2 * n], refs[2 * n + 1]
        local_sems = refs[-1]
        x, y, c = lax.axis_index("x"), lax.axis_index("y"), lax.axis_index("c")
        me = 4 * x + 2 * y + c
        mine = [pltpu.make_async_copy(srcs[q].at[me] if modes[q] == 'scatter' else srcs[q], zones[q].at[me],
                                      local_sems.at[q]) for q in range(n)]
        for cp in mine:
            cp.start()
        for r, pos, peer in _peers(x, y, c):
            for q in range(n):
                src = srcs[q].at[peer] if modes[q] == 'scatter' else srcs[q]
                cp = pltpu.make_async_remote_copy(
                    src_ref=src, dst_ref=zones[q].at[me], send_sem=send_sems.at[(r - 1) * n + q],
                    recv_sem=recv_sems.at[(r - 1) * n + q], device_id=pos, device_id_type=pl.DeviceIdType.MESH)
                cp.wait_send()
                cp.wait_recv()
        for cp in mine:
            cp.wait()

    hbm = pl.BlockSpec(memory_space=pltpu.HBM)
    sem = pl.BlockSpec(memory_space=pltpu.SEMAPHORE)
    bufs = flight['srcs'] + flight['lands']
    outs = pl.pallas_call(
        body, name=name, out_shape=tuple(pltpu.HBM(a.shape, a.dtype) for a in bufs),
        in_specs=[hbm] * (2 * n) + [sem, sem, pl.BlockSpec(memory_space=pl.ANY)], out_specs=tuple([hbm] * (2 * n)),
        input_output_aliases={q: q for q in range(2 * n)}, scratch_shapes=[pltpu.SemaphoreType.DMA((n,))],
        compiler_params=pltpu.CompilerParams(has_side_effects=pltpu.SideEffectType.DATAFLOW_SIDE_EFFECTING),
    )(*bufs, flight['send'], flight['recv'], after)
    return list(outs[n:])


def _adamw(parts, w, m, v, name):
    r, c = w.shape
    tr = _tile(r, (256, 128))

    def body(p_ref, w_ref, m_ref, v_ref, g_ref, d_ref, nm_ref, nv_ref):
        g = p_ref[0].astype(F32)
        for d in range(1, N_DEV):
            g = g + p_ref[d].astype(F32)
        m2 = ADAM_B1 * m_ref[...] + (1.0 - ADAM_B1) * g
        v2 = ADAM_B2 * v_ref[...] + (1.0 - ADAM_B2) * jnp.square(g)
        m_hat = m2 / (1.0 - ADAM_B1 ** ADAM_STEP)
        v_hat = v2 / (1.0 - ADAM_B2 ** ADAM_STEP)
        g_ref[...] = g
        d_ref[...] = -ADAM_LR * (m_hat / (jnp.sqrt(v_hat) + ADAM_EPS) + ADAM_WD * w_ref[...])
        nm_ref[...] = m2
        nv_ref[...] = v2

    spec = pl.BlockSpec((tr, c), lambda i: (i, 0))
    return pl.pallas_call(
        body, grid=(r // tr,), in_specs=[pl.BlockSpec((N_DEV, tr, c), lambda i: (0, i, 0)), spec, spec, spec],
        out_specs=[spec] * 4, out_shape=[jax.ShapeDtypeStruct((r, c), F32)] * 4,
        compiler_params=pltpu.CompilerParams(dimension_semantics=("parallel",)), name=name,
    )(parts, w, m, v)


FLAT_W = 512
FLAT_ROWS = 256


def _flat(arrs):
    v = jnp.concatenate([a.reshape(-1) for a in arrs])
    return jnp.pad(v, (0, (-v.shape[0]) % (FLAT_ROWS * FLAT_W))).reshape(-1, FLAT_W)


def _unflat(flat, shapes):
    v = flat.reshape(-1)
    out, off = [], 0
    for sh in shapes:
        n = 1
        for d in sh:
            n *= d
        out.append(v[off:off + n].reshape(sh))
        off += n
    return out


def _full(name, stacked):
    if SHARDED[name] == 0:
        return stacked.reshape((-1,) + stacked.shape[2:])
    return jnp.transpose(stacked, (1, 0, 2)).reshape(stacked.shape[1], -1)


def _shards(name, full):
    if SHARDED[name] == 0:
        return full.reshape((N_DEV, -1) + full.shape[1:])
    r, c = full.shape
    return jnp.transpose(full.reshape(r, N_DEV, c // N_DEV), (1, 0, 2))


def _prep_weights(p):
    q = {}
    w_in = p['l0_w_in']
    z = lambda n: jnp.zeros((D_MODEL, n), w_in.dtype)
    q['w_in'] = jnp.concatenate([w_in[:, :IN_EVEN - QK_ROPE], z(KR_LANE), w_in[:, IN_EVEN - QK_ROPE:],
                                 z(HEAD_PAD - KR_LANE - QK_ROPE)], axis=1)
    dqk = QK_NOPE + QK_ROPE
    q['w_uq'] = jnp.pad(p['l0_w_uq'].reshape(Q_RANK, N_HEADS, dqk), ((0, 0), (0, 0), (0, HEAD_PAD - dqk))
                        ).reshape(Q_RANK, N_HEADS * HEAD_PAD)
    ukv = p['l0_w_ukv'].reshape(KV_RANK, N_HEADS, 2, QK_NOPE)
    padh = lambda a: jnp.pad(a, ((0, 0), (0, 0), (0, HEAD_PAD - QK_NOPE))).reshape(KV_RANK, N_HEADS * HEAD_PAD)
    q['w_ukv'] = jnp.concatenate([padh(ukv[:, :, 0]), padh(ukv[:, :, 1])], axis=1)
    wo = p['l0_w_out']
    wo_a = jnp.pad(wo[CONV_WIDTH:].reshape(N_HEADS, V_DIM, D_MODEL), ((0, 0), (0, HEAD_PAD - V_DIM), (0, 0)))
    q['w_out'] = jnp.concatenate([wo[:CONV_WIDTH], wo_a.reshape(N_HEADS * HEAD_PAD, D_MODEL)], axis=0)
    return q


def _unprep_grads(g):
    out = {}
    d = g['w_in']
    out['l0_w_in'] = jnp.concatenate([d[:, :IN_EVEN - QK_ROPE],
                                      d[:, IN_EVEN - QK_ROPE + KR_LANE:IN_EVEN + KR_LANE]], axis=1)
    dqk = QK_NOPE + QK_ROPE
    out['l0_w_uq'] = g['w_uq'].reshape(Q_RANK, N_HEADS, HEAD_PAD)[:, :, :dqk].reshape(Q_RANK, N_HEADS * dqk)
    d = g['w_ukv'].reshape(KV_RANK, 2, N_HEADS, HEAD_PAD)[:, :, :, :QK_NOPE]
    out['l0_w_ukv'] = jnp.transpose(d, (0, 2, 1, 3)).reshape(KV_RANK, N_HEADS * 2 * QK_NOPE)
    d = g['w_out']
    da = d[CONV_WIDTH:].reshape(N_HEADS, HEAD_PAD, D_MODEL)[:, :V_DIM].reshape(N_HEADS * V_DIM, D_MODEL)
    out['l0_w_out'] = jnp.concatenate([d[:CONV_WIDTH], da], axis=0)
    return out


def _pad_rows(w, rows):
    return jnp.pad(w, [(0, 0)] * (w.ndim - 2) + [(0, rows - w.shape[-2]), (0, 0)])


def _ffn_fwd(x, rep, got, pre, tag):
    s = x.shape[0]
    xn = _rms_fwd(x, rep[pre + 'ffn_norm'], f"{tag}_ffn_norm")
    w_up = got[pre + 'w_up']
    hu = _mm(xn, w_up, gb=_same, go=_same, groups=N_DEV, name=f"{tag}_ffn_up").reshape(N_DEV * s, FF_SHARD)
    taps = _pad_rows(got[pre + 'ffn_conv_w'], TAP_ROWS).reshape(N_DEV * TAP_ROWS, FF_SHARD)
    bias = _pad_rows(rep[pre + 'ffn_conv_b'].reshape(N_DEV, 1, FF_SHARD), TAP_ROWS).reshape(N_DEV * TAP_ROWS, FF_SHARD)
    (act,) = _conv_fwd([(hu, 0), (hu, FF_HALF)], taps, [0, FF_HALF], [bias, bias], [0, FF_HALF], _pair_pre,
                       _gate_post, [BF16], name=f"{tag}_ffn_gate", **_gate(s))
    act = act.reshape(FF_HALF, s, FF_SHARD)
    w_down = got[pre + 'w_down'].reshape(FF_HALF, FF_SHARD, D_MODEL)
    y = _mm(act, w_down, ga=_same, gb=_same, groups=FF_HALF, res=x, name=f"{tag}_ffn_down")
    return y, (x, xn, hu, act, taps, bias, w_up, w_down)


def _ffn_bwd(dy, saved, rep, pre, tag, grads, gsh):
    x, xn, hu, act, taps, bias, w_up, w_down = saved
    s = x.shape[0]
    dact = _mm(dy, w_down, tb=True, gb=_same, go=_same, groups=FF_HALF, name=f"{tag}_ffn_down_dx")
    gsh[pre + 'w_down'] = _mm(act, dy, ta=True, ga=_same, go=_same, groups=FF_HALF, out_dtype=BF16,
                              name=f"{tag}_ffn_down_dw").reshape(N_DEV, FF_SHARD // 2, D_MODEL)
    dha, dhb, dwa, dwb, dba, dbb = _conv_bwd(
        [(hu, 0), (hu, FF_HALF)], taps, [0, FF_HALF], [bias, bias], [0, FF_HALF], _pair_pre, _gate_post,
        [(dact.reshape(FF_HALF * s, FF_SHARD), 0)], BF16, name=f"{tag}_ffn_gate_bwd", **_gate(s))
    dha, dhb = dha.reshape(FF_HALF, s, FF_SHARD), dhb.reshape(FF_HALF, s, FF_SHARD)
    dtaps = jnp.concatenate([dwa, dwb], axis=0).reshape(N_DEV, TAP_ROWS, FF_SHARD)
    gsh[pre + 'ffn_conv_w'] = dtaps[:, :FFN_K].astype(BF16)
    grads[pre + 'ffn_conv_b'] = jnp.concatenate([dba, dbb], axis=0).reshape(N_DEV, TAP_ROWS, FF_SHARD)[:, 0].reshape(-1)
    upper = lambda g: g + FF_HALF
    dxn = _mm(dha, w_up, tb=True, ga=_same, gb=_same, groups=FF_HALF, name=f"{tag}_ffn_up_dx_a")
    dxn = _mm(dhb, w_up, tb=True, ga=_same, gb=upper, groups=FF_HALF, res=dxn, name=f"{tag}_ffn_up_dx_b")
    dwu = [_mm(xn, dh, ta=True, gb=_same, go=_same, groups=FF_HALF, out_dtype=BF16, name=f"{tag}_ffn_up_dw_{t}")
           for t, dh in (("a", dha), ("b", dhb))]
    gsh[pre + 'w_up'] = jnp.concatenate(dwu, axis=0)
    dx, dg = _rms_bwd(x, rep[pre + 'ffn_norm'], dxn, dy, f"{tag}_ffn_norm_bwd")
    grads[pre + 'ffn_norm'] = dg.reshape(-1)
    return dx


def _mla_fwd(x, rep, taps, q, tabs):
    s = x.shape[0]
    xn = _rms_fwd(x, rep['l0_mix_norm'], "l0_mix_norm")
    hmat = _mm(xn, q['w_in'], name="l0_in")
    wt = _pad_rows(taps, 4 * TAP_ROWS)
    cpar = [rep['l0_conv_b'].reshape(1, -1), rep['l0_conv_ln_g'].reshape(1, -1), rep['l0_conv_ln_b'].reshape(1, -1)]
    (u,) = _conv_fwd([(hmat, 0), (hmat, 1)], wt, [0], cpar, [0, 0, 0], _glu_pre, _ln_silu_post, [BF16],
                     name="l0_conv", **_conva(s))
    qn, kvn = rep['l0_q_norm'].reshape(1, -1), rep['l0_kv_norm'].reshape(1, -1)
    cqn, ckvn = _rowwise(lambda a, b, ga, gb: (_rms(a, ga), _rms(b, gb)),
                         [(hmat, Q_RANK, 2 * CONV_WIDTH // Q_RANK), (hmat, KV_RANK, (2 * CONV_WIDTH + Q_RANK) // KV_RANK)],
                         [qn, kvn], [(Q_RANK, BF16), (KV_RANK, BF16)], [], ts=512, name="l0_latent_norm")
    qf = _mm(cqn, q['w_uq'], name="l0_uq")
    kvf = _mm(ckvn, q['w_ukv'], name="l0_ukv")
    q_rot, k_full, v = _rope_fwd(qf, kvf, hmat, tabs, "l0_rope")
    o, lse = _attn_fwd(q_rot, k_full, v, "l0_attn")
    cat = jnp.concatenate([u, o], axis=1)
    y = _mm(cat, q['w_out'], res=x, name="l0_out")
    return y, (x, xn, hmat, wt, cpar, qn, kvn, cqn, ckvn, q_rot, k_full, v, lse, cat)


def _mla_bwd(dy, saved, rep, q, tabs, grads, gq):
    x, xn, hmat, wt, cpar, qn, kvn, cqn, ckvn, q_rot, k_full, v, lse, cat = saved
    s = x.shape[0]
    dcat = _mm(dy, q['w_out'], tb=True, name="l0_out_dx")
    gq['w_out'] = _mm(cat, dy, ta=True, name="l0_out_dw")
    lse_r, dlt_r = _attn_rows(cat, dcat, lse, "l0_attn_rows")
    dq, dk, dv = _attn_bwd(q_rot, k_full, v, dcat, lse_r, dlt_r, "l0_attn_bwd")
    dqf, dkvf, dkr = _rope_bwd(dq, dk, dv, tabs, "l0_rope_bwd")
    dcqn = _mm(dqf, q['w_uq'], tb=True, name="l0_uq_dx")
    gq['w_uq'] = _mm(cqn, dqf, ta=True, name="l0_uq_dw")
    dckvn = _mm(dkvf, q['w_ukv'], tb=True, name="l0_ukv_dx")
    gq['w_ukv'] = _mm(ckvn, dkvf, ta=True, name="l0_ukv_dw")

    def lat_bwd(a, b, da, db, ga, gb):
        _, vjp = jax.vjp(lambda a_, b_, ga_, gb_: (_rms(a_, ga_), _rms(b_, gb_)), a, b, ga, gb)
        return vjp((da, db))

    dcq, dckv, dqn, dkvn = _rowwise(
        lat_bwd, [(hmat, Q_RANK, 2 * CONV_WIDTH // Q_RANK), (hmat, KV_RANK, (2 * CONV_WIDTH + Q_RANK) // KV_RANK),
                  (dcqn, Q_RANK, 0), (dckvn, KV_RANK, 0)],
        [qn, kvn], [(Q_RANK, F32), (KV_RANK, F32)], [(1, Q_RANK), (1, KV_RANK)], ts=512, name="l0_latent_norm_bwd")
    grads['l0_q_norm'], grads['l0_kv_norm'] = dqn.reshape(-1), dkvn.reshape(-1)
    da, dg, dwt, dcb, dlg, dlb = _conv_bwd(
        [(hmat, 0), (hmat, 1)], wt, [0], cpar, [0, 0, 0], _glu_pre, _ln_silu_post, [(dcat, 0)], F32,
        name="l0_conv_bwd", **_conva(s))
    gq['conv_w'] = dwt[:CONV_K]
    grads['l0_conv_b'], grads['l0_conv_ln_g'], grads['l0_conv_ln_b'] = dcb.reshape(-1), dlg.reshape(-1), dlb.reshape(-1)
    dh = jnp.concatenate([da, dg, dcq, dckv, dkr], axis=1)
    dxn = _mm(dh, q['w_in'], tb=True, name="l0_in_dx")
    gq['w_in'] = _mm(xn, dh, ta=True, name="l0_in_dw")
    dx, dgn = _rms_bwd(x, rep['l0_mix_norm'], dxn, dy, "l0_mix_norm_bwd")
    grads['l0_mix_norm'] = dgn.reshape(-1)
    return dx


def _gelu_skip(yc, u, d):
    return jax.nn.gelu(yc + d * u)


def _glu_out(z1, z2, b1, b2, x):
    return x + (z1 + b1) * jax.nn.sigmoid(z2 + b2)


def _s5_fwd(x, rep, w_in, w_glu):
    xn = _rms_fwd(x, rep['l1_mix_norm'], "l1_mix_norm")
    u = _mm(xn, w_in, name="l1_in")
    lb_re, lb_im, bb_re, bb_im = _s5_disc(rep['l1_log_dt'], rep['l1_a_re'], rep['l1_a_im'], rep['l1_b_re'],
                                          rep['l1_b_im'])
    lam = jnp.concatenate([lb_re.reshape(1, -1), lb_im.reshape(1, -1)], axis=1)
    tr = lambda a: jnp.transpose(a, (0, 2, 1))
    bmat = jnp.concatenate([_bd(tr(bb_re)), _bd(tr(bb_im))], axis=1)
    cmat = jnp.concatenate([_bd(tr(rep['l1_c_re'])), -_bd(tr(rep['l1_c_im']))], axis=0)
    sp = _s5_sparse(x.shape[0])
    bu = _mm(u, bmat, sparse=sp['expand'], name="l1_bu")
    (xs,) = _scan(bu, lam, reverse=False, name="l1_scan")
    yc = _mm(xs, cmat, sparse=sp['reduce'], name="l1_cx")
    dsk = rep['l1_d'].reshape(1, -1)
    (y,) = _rowwise(_gelu_skip, [(yc, SSM_WIDTH, 0), (u, SSM_WIDTH, 0)], [dsk], [(SSM_WIDTH, BF16)], [],
                    ts=512, name="l1_gelu")
    z = _mm(y, w_glu, name="l1_glu")
    bg = rep['l1_b_glu'].reshape(1, -1)
    (out,) = _rowwise(lambda z1, z2, xv, b1, b2: _glu_out(z1, z2, b1, b2, xv),
                      [(z, D_MODEL, 0), (z, D_MODEL, 1), (x, D_MODEL, 0)], [bg[:, :D_MODEL], bg[:, D_MODEL:]],
                      [(D_MODEL, F32)], [], ts=512, name="l1_glu_out")
    return out, (x, xn, u, lam, bmat, cmat, xs, yc, dsk, y, z, bg, w_in, w_glu)


def _s5_bwd(dy, saved, rep, grads, gq):
    x, xn, u, lam, bmat, cmat, xs, yc, dsk, y, z, bg, w_in, w_glu = saved

    def glu_bwd(z1, z2, dv, b1, b2):
        _, vjp = jax.vjp(lambda a, b, c, d: (a + c) * jax.nn.sigmoid(b + d), z1, z2, b1, b2)
        d1, d2, db1, db2 = vjp(dv)
        return jnp.concatenate([d1, d2], axis=1), db1, db2

    dz, db1, db2 = _rowwise(glu_bwd, [(z, D_MODEL, 0), (z, D_MODEL, 1), (dy, D_MODEL, 0)],
                            [bg[:, :D_MODEL], bg[:, D_MODEL:]], [(2 * D_MODEL, BF16)], [(1, D_MODEL), (1, D_MODEL)],
                            ts=512, name="l1_glu_out_bwd")
    grads['l1_b_glu'] = jnp.concatenate([db1, db2], axis=1).reshape(-1)
    dyv = _mm(dz, w_glu, tb=True, name="l1_glu_dx")
    gq['l1_w_glu'] = _mm(y, dz, ta=True, name="l1_glu_dw")

    def gelu_bwd(ycv, uv, dv, dk):
        _, vjp = jax.vjp(_gelu_skip, ycv, uv, dk)
        return vjp(dv)

    dyc, du_skip, dd = _rowwise(gelu_bwd, [(yc, SSM_WIDTH, 0), (u, SSM_WIDTH, 0), (dyv, SSM_WIDTH, 0)], [dsk],
                                [(SSM_WIDTH, F32), (SSM_WIDTH, F32)], [(1, SSM_WIDTH)], ts=512, name="l1_gelu_bwd")
    grads['l1_d'] = dd.reshape(-1)
    sp = _s5_sparse(x.shape[0])
    dxs = _mm(dyc, cmat, tb=True, sparse=sp['expand'], name="l1_cx_dx")
    dcm = _mm(xs, dyc, ta=True, sparse=sp['wide_t'], name="l1_cx_dw")
    gs, dlam = _scan(dxs, lam, reverse=True, xs=xs, name="l1_scan_bwd")
    dlr, dli = dlam[:, :SSM_CH], dlam[:, SSM_CH:]
    du = _mm(gs, bmat, tb=True, res=du_skip, sparse=sp['reduce'], name="l1_bu_dx")
    dbm = _mm(u, gs, ta=True, sparse=sp['narrow_t'], name="l1_bu_dw")
    eye = jnp.eye(S5_TILE_GROUPS, dtype=F32)
    nb = SSM_GROUPS // S5_TILE_GROUPS
    dcm = dcm.reshape(2, nb, S5_TILE_GROUPS, SSM_STATE, S5_TILE_GROUPS, SSM_GROUP)
    dcm = jnp.sum(dcm * eye[None, None, :, None, :, None], axis=4).reshape(2, SSM_GROUPS, SSM_STATE, SSM_GROUP)
    tr = lambda a: jnp.transpose(a, (0, 2, 1))
    grads['l1_c_re'], grads['l1_c_im'] = tr(dcm[0]), -tr(dcm[1])
    dbm = dbm.reshape(S5_TILE_GROUPS, SSM_GROUP, 2, nb, S5_TILE_GROUPS, SSM_STATE)
    dbm = jnp.sum(dbm * eye[:, None, None, None, :, None], axis=0)
    dbm = jnp.transpose(dbm, (1, 2, 3, 4, 0)).reshape(2, SSM_GROUPS, SSM_STATE, SSM_GROUP)
    dbb_re, dbb_im = dbm[0], dbm[1]
    names = ['l1_log_dt', 'l1_a_re', 'l1_a_im', 'l1_b_re', 'l1_b_im']
    _, vjp = jax.vjp(_s5_disc, *[rep[n] for n in names])
    for n, gval in zip(names, vjp((dlr.reshape(SSM_GROUPS, SSM_STATE), dli.reshape(SSM_GROUPS, SSM_STATE), dbb_re, dbb_im))):
        grads[n] = gval
    dxn = _mm(du, w_in, tb=True, name="l1_in_dx")
    gq['l1_w_in'] = _mm(xn, du, ta=True, name="l1_in_dw")
    dx, dgn = _rms_bwd(x, rep['l1_mix_norm'], dxn, dy, "l1_mix_norm_bwd")
    grads['l1_mix_norm'] = dgn.reshape(-1)
    return dx


def _loss_head(x, g, target):
    d = x.shape[1]

    def fn(xv, tv, gv):
        y, vjp = jax.vjp(_rms, xv, gv)
        err = y - tv
        part = 0.5 * jnp.sum(jnp.mean(jnp.square(err), axis=-1, keepdims=True), axis=0, keepdims=True)
        dx, dg = vjp(err * (1.0 / d))
        return dx, jnp.broadcast_to(part, (1, 128)), dg

    return _rowwise(fn, [(x, d, 0), (target, d, 0)], [g.reshape(1, -1)], [(d, F32)], [(1, 128), (1, d)], ts=512,
                    name="loss_head")


FIRST = ('l0_w_in', 'l0_conv_w', 'l0_w_uq', 'l0_w_ukv', 'l0_w_out')
REST = tuple(n for n in SHARDED if n not in FIRST)
GRADS_L1 = ('l1_w_in', 'l1_w_glu', 'l1_w_up', 'l1_ffn_conv_w', 'l1_w_down')
GRADS_L0_FFN = ('l0_w_up', 'l0_ffn_conv_w', 'l0_w_down')


def _local_step(x, target, rep, got, wait_rest, send_grads):
    even = {n: _full(n, got[n]) for n in ('l0_w_in', 'l0_w_uq', 'l0_w_ukv', 'l0_w_out')}
    q = _prep_weights(even)
    tabs = _rope_tables(x.shape[0])
    x1, s_mla = _mla_fwd(x, rep, _full('l0_conv_w', got['l0_conv_w']), q, tabs)
    got = wait_rest(x1)
    x2, s_f0 = _ffn_fwd(x1, rep, got, 'l0_', "l0")
    x3, s_s5 = _s5_fwd(x2, rep, _full('l1_w_in', got['l1_w_in']), _full('l1_w_glu', got['l1_w_glu']))
    x4, s_f1 = _ffn_fwd(x3, rep, got, 'l1_', "l1")
    dx4, loss, dgf = _loss_head(x4, rep['final_norm'], target)
    grads, gq, gsh = {'final_norm': dgf.reshape(-1)}, {}, {}
    dx3 = _ffn_bwd(dx4, s_f1, rep, 'l1_', "l1", grads, gsh)
    dx2 = _s5_bwd(dx3, s_s5, rep, grads, gq)
    for n in ('l1_w_in', 'l1_w_glu'):
        gsh[n] = _shards(n, gq[n]).astype(BF16)
    dx2 = send_grads(GRADS_L1, gsh, dx2)
    dx1 = _ffn_bwd(dx2, s_f0, rep, 'l0_', "l0", grads, gsh)
    dx1 = send_grads(GRADS_L0_FFN, gsh, dx1)
    dx0 = _mla_bwd(dx1, s_mla, rep, q, tabs, grads, gq)
    full = _unprep_grads(gq)
    full['l0_conv_w'] = gq['conv_w']
    return loss[0, 0], dx0, grads, {n: _shards(n, full[n]).astype(BF16) for n in FIRST}


def kernel(x, l0_mix_norm, l0_w_in, l0_conv_w, l0_conv_b, l0_conv_ln_g, l0_conv_ln_b, l0_q_norm, l0_kv_norm, l0_w_uq, l0_w_ukv, l0_w_out, l0_ffn_norm, l0_w_up, l0_ffn_conv_w, l0_ffn_conv_b, l0_w_down, l1_mix_norm, l1_w_in, l1_log_dt, l1_a_re, l1_a_im, l1_b_re, l1_b_im, l1_c_re, l1_c_im, l1_d, l1_w_glu, l1_b_glu, l1_ffn_norm, l1_w_up, l1_ffn_conv_w, l1_ffn_conv_b, l1_w_down, final_norm, loss_target, m_l0_mix_norm, m_l0_w_in, m_l0_conv_w, m_l0_conv_b, m_l0_conv_ln_g, m_l0_conv_ln_b, m_l0_q_norm, m_l0_kv_norm, m_l0_w_uq, m_l0_w_ukv, m_l0_w_out, m_l0_ffn_norm, m_l0_w_up, m_l0_ffn_conv_w, m_l0_ffn_conv_b, m_l0_w_down, m_l1_mix_norm, m_l1_w_in, m_l1_log_dt, m_l1_a_re, m_l1_a_im, m_l1_b_re, m_l1_b_im, m_l1_c_re, m_l1_c_im, m_l1_d, m_l1_w_glu, m_l1_b_glu, m_l1_ffn_norm, m_l1_w_up, m_l1_ffn_conv_w, m_l1_ffn_conv_b, m_l1_w_down, m_final_norm, v_l0_mix_norm, v_l0_w_in, v_l0_conv_w, v_l0_conv_b, v_l0_conv_ln_g, v_l0_conv_ln_b, v_l0_q_norm, v_l0_kv_norm, v_l0_w_uq, v_l0_w_ukv, v_l0_w_out, v_l0_ffn_norm, v_l0_w_up, v_l0_ffn_conv_w, v_l0_ffn_conv_b, v_l0_w_down, v_l1_mix_norm, v_l1_w_in, v_l1_log_dt, v_l1_a_re, v_l1_a_im, v_l1_b_re, v_l1_b_im, v_l1_c_re, v_l1_c_im, v_l1_d, v_l1_w_glu, v_l1_b_glu, v_l1_ffn_norm, v_l1_w_up, v_l1_ffn_conv_w, v_l1_ffn_conv_b, v_l1_w_down, v_final_norm):
    args = dict(locals())
    w = {n: args[n] for n in WEIGHTS}
    m = {n: args['m_' + n] for n in WEIGHTS}
    v = {n: args['v_' + n] for n in WEIGHTS}
    payload = lambda n: w[n] if n in TAPS else w[n].astype(BF16)
    got = dict(zip(FIRST, _exchange([payload(n) for n in FIRST], ['gather'] * len(FIRST), "gather_first")))
    rep = {n: w[n] for n in REPLICATED}
    rest, got['l0_conv_w'] = _exchange_start([payload(n) for n in REST], ['gather'] * len(REST), got['l0_conv_w'],
                                             "gather_rest_start")
    wait_rest = lambda after: dict(zip(REST, _exchange_wait(rest, after, "gather_rest_wait")))
    flights = []

    def send_grads(names, gsh, carry):
        tag = "grads_" + names[0][:2]
        flight, carry = _exchange_start([gsh[n] for n in names], ['scatter'] * len(names), carry, tag + "_start")
        flights.append((names, flight, tag + "_wait"))
        return carry

    loss, dx, grads, gsh = _local_step(x[0], loss_target[0], rep, got, wait_rest, send_grads)

    last = _exchange([gsh[n] for n in FIRST] + [_flat([grads[n] for n in REPLICATED])],
                     ['scatter'] * len(FIRST) + ['gather'], "exchange_grads")
    recv = dict(zip(FIRST, last))
    for names, flight, name in flights:
        recv.update(zip(names, _exchange_wait(flight, last[-1], name)))
    res = [dict(), dict(), dict(), dict()]
    for n in SHARDED:
        for kind, a in enumerate(_adamw(recv[n], w[n], m[n], v[n], "adamw_" + n)):
            res[kind][n] = a
    flatr = lambda d: _flat([d[n] for n in REPLICATED])
    rp_out = _adamw(last[-1], flatr(w), flatr(m), flatr(v), "adamw_replicated")
    for kind in range(4):
        for n, a in zip(REPLICATED, _unflat(rp_out[kind], [w[n].shape for n in REPLICATED])):
            res[kind][n] = a
    total = lax.psum(loss, ("x", "y", "c"))
    return (total, dx[None], *[res[0][n] for n in WEIGHTS], *[res[1][n] for n in WEIGHTS],
            *[res[2][n] for n in WEIGHTS], *[res[3][n] for n in WEIGHTS])
```

```python
import functools

import jax
import jax.numpy as jnp
from jax import lax
from jax.experimental import pallas as pl
from jax.experimental.pallas import tpu as pltpu

F32 = jnp.float32
BF16 = jnp.bfloat16

N_DEV = 8
D_MODEL = 1024
EPS = 1e-6
LN_EPS = 1e-5
CONV_WIDTH = 512
CONV_K = 31
N_HEADS = 8
QK_NOPE = 64
QK_ROPE = 32
V_DIM = 64
HEAD_PAD = 128
Q_RANK = 256
KV_RANK = 128
ROPE_BASE = 10000.0
IN_EVEN = 2 * CONV_WIDTH + Q_RANK + KV_RANK + QK_ROPE
IN_PAD = 1536
KR_LANE = 64
SSM_WIDTH = 512
SSM_GROUP = 16
SSM_GROUPS = 32
SSM_STATE = 64
SSM_CH = SSM_GROUPS * SSM_STATE
D_FF = 2816
FF_SHARD = 2 * D_FF // N_DEV
FF_HALF = N_DEV // 2
FFN_K = 3
TAP_ROWS = 8
ADAM_LR, ADAM_B1, ADAM_B2, ADAM_EPS, ADAM_WD, ADAM_STEP = 0.001, 0.9, 0.999, 1e-08, 0.01, 10

WEIGHTS = ['l0_mix_norm', 'l0_w_in', 'l0_conv_w', 'l0_conv_b', 'l0_conv_ln_g', 'l0_conv_ln_b', 'l0_q_norm',
           'l0_kv_norm', 'l0_w_uq', 'l0_w_ukv', 'l0_w_out', 'l0_ffn_norm', 'l0_w_up', 'l0_ffn_conv_w',
           'l0_ffn_conv_b', 'l0_w_down', 'l1_mix_norm', 'l1_w_in', 'l1_log_dt', 'l1_a_re', 'l1_a_im', 'l1_b_re',
           'l1_b_im', 'l1_c_re', 'l1_c_im', 'l1_d', 'l1_w_glu', 'l1_b_glu', 'l1_ffn_norm', 'l1_w_up',
           'l1_ffn_conv_w', 'l1_ffn_conv_b', 'l1_w_down', 'final_norm']
SHARDED = {'l0_w_in': 1, 'l0_conv_w': 1, 'l0_w_uq': 1, 'l0_w_ukv': 1, 'l0_w_out': 0, 'l0_w_up': 1,
           'l0_ffn_conv_w': 1, 'l0_w_down': 0, 'l1_w_in': 0, 'l1_w_glu': 1, 'l1_w_up': 1, 'l1_ffn_conv_w': 1,
           'l1_w_down': 0}
TAPS = ('l0_conv_w', 'l0_ffn_conv_w', 'l1_ffn_conv_w')
REPLICATED = [n for n in WEIGHTS if n not in SHARDED]


def _tile(n, cands):
    for c in cands:
        if n % c == 0:
            return c
    return n


def _same(g):
    return g


def _mm(a, b, *, ta=False, tb=False, res=None, out_dtype=F32, name, ga=None, gb=None, go=None, groups=1,
        sparse=None):
    a2, b2 = (a.shape[1:] if ga else a.shape), (b.shape[1:] if gb else b.shape)
    m, kd = (a2[1], a2[0]) if ta else a2
    kd2, n = (b2[1], b2[0]) if tb else b2
    assert kd == kd2, (a.shape, b.shape, ta, tb)
    tm = _tile(m, (1024, 512, 256, 128))
    tn = _tile(n, (512, 384, 256, 128))
    tk = _tile(kd, (2048, 1024, 512, 256, 128) if ta else (1024, 512, 256, 128))
    nk = kd // tk
    a_m = b_n = k_of = None
    if sparse is not None:
        (tm, tn, tk), (m, n), nk = sparse['tiles'], sparse['out'], sparse['nk']
        a_m, b_n, k_of = sparse.get('a_m'), sparse.get('b_n'), sparse.get('k_of')
    a_m = a_m or (lambda i, j: i)
    b_n = b_n or (lambda i, j: j)
    k_of = k_of or (lambda i, j, k: k)
    summed = go is None and ga is not None and gb is not None
    assert summed or go is not None or (ga is None and gb is None)
    nkk = nk
    dn = (((0 if ta else 1,), (1 if tb else 0,)), ((), ()))

    def body(*refs):
        if res is None:
            a_ref, b_ref, o_ref, acc_ref = refs
            r_ref = None
        else:
            a_ref, b_ref, r_ref, o_ref, acc_ref = refs
        k = pl.program_id(3)
        if summed:
            p = lax.dot_general(a_ref[0].astype(BF16), b_ref[0].astype(BF16), dn, preferred_element_type=F32)
            for g in range(1, groups):
                p = p + lax.dot_general(a_ref[g].astype(BF16), b_ref[g].astype(BF16), dn, preferred_element_type=F32)
        else:
            p = lax.dot_general(a_ref[...].astype(BF16), b_ref[...].astype(BF16), dn, preferred_element_type=F32)

        @pl.when(k == 0)
        def _():
            acc_ref[...] = p

        @pl.when(k > 0)
        def _():
            acc_ref[...] += p

        @pl.when(k == nkk - 1)
        def _():
            out = acc_ref[...]
            if r_ref is not None:
                out = out + r_ref[...]
            o_ref[...] = out.astype(out_dtype)

    def spec(shape2, idx2, gmap):
        if gmap is None:
            return pl.BlockSpec(shape2, lambda g, i, j, kk: idx2(i, j, kk))
        if summed:
            return pl.BlockSpec((groups,) + shape2, lambda g, i, j, kk: (gmap(0) // groups,) + idx2(i, j, kk))
        return pl.BlockSpec((None,) + shape2, lambda g, i, j, kk: (gmap(g),) + idx2(i, j, kk))

    a_idx = (lambda i, j, k: (k_of(i, j, k), a_m(i, j))) if ta else (lambda i, j, k: (a_m(i, j), k_of(i, j, k)))
    b_idx = (lambda i, j, k: (b_n(i, j), k_of(i, j, k))) if tb else (lambda i, j, k: (k_of(i, j, k), b_n(i, j)))
    a_spec = spec((tk, tm) if ta else (tm, tk), a_idx, ga)
    b_spec = spec((tn, tk) if tb else (tk, tn), b_idx, gb)
    o_spec = spec((tm, tn), lambda i, j, k: (i, j), go)
    in_specs, args = [a_spec, b_spec], [a, b]
    if res is not None:
        in_specs.append(o_spec)
        args.append(res)
    out_shape = (groups, m, n) if go else (m, n)
    return pl.pallas_call(
        body, grid=(groups if go else 1, m // tm, n // tn, nkk), in_specs=in_specs, out_specs=o_spec,
        out_shape=jax.ShapeDtypeStruct(out_shape, out_dtype), scratch_shapes=[pltpu.VMEM((tm, tn), F32)],
        compiler_params=pltpu.CompilerParams(dimension_semantics=("parallel", "parallel", "parallel", "arbitrary")),
        name=name)(*args)


def _rowwise(fn, rows, bcasts, row_outs, red_outs, *, ts, name):
    s = rows[0][0].shape[0]
    nr, nb, nro, nre = len(rows), len(bcasts), len(row_outs), len(red_outs)

    def body(*refs):
        i = pl.program_id(0)
        outs = fn(*[r[...] for r in refs[:nr + nb]])
        if not isinstance(outs, (tuple, list)):
            outs = (outs,)
        o_refs = refs[nr + nb:]
        for q in range(nro):
            o_refs[q][...] = outs[q].astype(o_refs[q].dtype)
        for q in range(nro, nro + nre):
            @pl.when(i == 0)
            def _(q=q):
                o_refs[q][...] = outs[q]

            @pl.when(i > 0)
            def _(q=q):
                o_refs[q][...] += outs[q]

    in_specs = [pl.BlockSpec((ts, w), functools.partial(lambda i, cb: (i, cb), cb=cb)) for (_, w, cb) in rows]
    in_specs += [pl.BlockSpec(b.shape, functools.partial(lambda i, nd: (0,) * nd, nd=b.ndim)) for b in bcasts]
    out_specs = [pl.BlockSpec((ts, w), lambda i: (i, 0)) for (w, _) in row_outs]
    out_specs += [pl.BlockSpec((r, w), lambda i: (0, 0)) for (r, w) in red_outs]
    out_shape = [jax.ShapeDtypeStruct((s, w), dt) for (w, dt) in row_outs]
    out_shape += [jax.ShapeDtypeStruct((r, w), F32) for (r, w) in red_outs]
    return pl.pallas_call(
        body, grid=(s // ts,), in_specs=in_specs, out_specs=out_specs, out_shape=out_shape,
        compiler_params=pltpu.CompilerParams(dimension_semantics=("arbitrary",)), name=name,
    )(*[r[0] for r in rows], *bcasts)


def _rms(x, g):
    return x * lax.rsqrt(jnp.mean(x * x, axis=-1, keepdims=True) + EPS) * g


def _rms_fwd(x, g, name):
    return _rowwise(lambda xv, gv: _rms(xv, gv), [(x, x.shape[1], 0)], [g.reshape(1, -1)],
                    [(x.shape[1], BF16)], [], ts=512, name=name)[0]


def _rms_bwd(x, g, dxn, dres, name):
    d = x.shape[1]

    def fn(xv, dv, rv, gv):
        _, vjp = jax.vjp(_rms, xv, gv)
        dx, dg = vjp(dv.astype(F32))
        return rv + dx, dg

    return _rowwise(fn, [(x, d, 0), (dxn, d, 0), (dres, d, 0)], [g.reshape(1, -1)], [(d, F32)], [(1, d)],
                    ts=512, name=name)


def _cspec(mode, off, rows, width, rowblk, n_rb):
    if mode == 'col':
        return pl.BlockSpec((rows, width), lambda jc, i: (rowblk(i), off + jc))
    return pl.BlockSpec((rows, width), lambda jc, i: ((off + jc) * n_rb + rowblk(i), 0))


TAP_SPREAD = 24


def _stage_shape(k_taps, n, tc):
    return (8, n + TAP_SPREAD, tc) if k_taps > 8 else (1, 8, 128)


def _tap_windows(ref, offsets, n, stage):
    if len(offsets) <= 8:
        return [functools.partial(lambda v: v, ref[pl.ds(o, n), :]) for o in offsets]
    lows = {}
    for o in offsets:
        lows[o % 8] = min(o, lows.get(o % 8, o))
    for r, lo in lows.items():
        span = n + max(o for o in offsets if o % 8 == r) - lo
        stage[r, pl.ds(0, span), :] = ref[pl.ds(lo, span), :]
    return [functools.partial(lambda o: stage[o % 8, pl.ds(o - lows[o % 8], n), :], o) for o in offsets]


def _conv_fwd(xins, w, woffs, params, poffs, pre, post, outs, *, mode, s, k_taps, hb, ts, tc, rc, n_cb, name):
    n_s, nx, ncv, npar, no = s // ts, len(xins), len(woffs), len(params), len(outs)
    rpb = ts // hb
    kp = TAP_ROWS * ((k_taps + TAP_ROWS - 1) // TAP_ROWS)
    pr = 1 if mode == 'col' else TAP_ROWS

    def body(*refs):
        mains, halos = refs[:nx], refs[nx:2 * nx]
        w_refs = refs[2 * nx:2 * nx + ncv]
        p_refs = refs[2 * nx + ncv:2 * nx + ncv + npar]
        o_refs = refs[2 * nx + ncv + npar:2 * nx + ncv + npar + no]
        u_s = refs[2 * nx + ncv + npar + no:2 * nx + ncv + npar + no + ncv]
        win_s = refs[2 * nx + ncv + npar + no + ncv:2 * nx + ncv + npar + no + 2 * ncv]
        stage_s = refs[2 * nx + ncv + npar + no + 2 * ncv:]
        i = pl.program_id(1)
        um = pre(*[r[...].astype(F32) for r in mains])
        uh = pre(*[r[...].astype(F32) for r in halos])
        first = (i > 0).astype(F32)
        for q in range(ncv):
            u_s[q][pl.ds(0, hb), :] = uh[q] * first
            u_s[q][pl.ds(hb, ts), :] = um[q]
        pv = [r[0:1, :] for r in p_refs]

        def chunk(ci, carry):
            r0 = pl.multiple_of(ci * rc, rc)
            cs = []
            for q in range(ncv):
                win_s[q][...] = u_s[q][pl.ds(r0, rc + hb), :]
                acc = jnp.zeros((rc, tc), F32)
                offsets = [hb - (k_taps - 1) + t for t in range(k_taps)]
                for t, ut in enumerate(_tap_windows(win_s[q], offsets, rc, stage_s[q])):
                    acc = acc + w_refs[q][pl.ds(t, 1), :] * ut()
                cs.append(acc)
            res = post(cs, pv)
            for q in range(no):
                o_refs[q][pl.ds(r0, rc), :] = res[q].astype(o_refs[q].dtype)
            return carry

        lax.fori_loop(0, ts // rc, chunk, 0)

    main = lambda i: i
    prev = lambda i: jnp.maximum(i * rpb - 1, 0)
    zero = lambda i: 0
    in_specs = [_cspec(mode, off, ts, tc, main, n_s) for _, off in xins]
    in_specs += [_cspec(mode, off, hb, tc, prev, s // hb) for _, off in xins]
    in_specs += [_cspec(mode, off, kp, tc, zero, 1) for off in woffs]
    in_specs += [_cspec(mode, off, pr, tc, zero, 1) for off in poffs]
    out_specs = [_cspec(mode, 0, ts, tc, main, n_s) for _ in outs]
    oshape = (s, n_cb * tc) if mode == 'col' else (n_cb * s, tc)
    out_shape = [jax.ShapeDtypeStruct(oshape, dt) for dt in outs]
    return pl.pallas_call(
        body, grid=(n_cb, n_s), in_specs=in_specs, out_specs=out_specs, out_shape=out_shape,
        scratch_shapes=[pltpu.VMEM((hb + ts, tc), F32) for _ in range(ncv)]
        + [pltpu.VMEM((hb + rc, tc), F32) for _ in range(ncv)]
        + [pltpu.VMEM(_stage_shape(k_taps, rc, tc), F32) for _ in range(ncv)],
        compiler_params=pltpu.CompilerParams(dimension_semantics=("parallel", "arbitrary")), name=name,
    )(*[a for a, _ in xins], *[a for a, _ in xins], *([w] * ncv), *params)


def _conv_bwd(xins, w, woffs, params, poffs, pre, post, douts, dx_dtype, *, mode, s, k_taps, hb, ts, tc, rc, n_cb, name):
    n_s, nx, ncv, npar, ndo = s // ts, len(xins), len(woffs), len(params), len(douts)
    rpb = ts // hb
    n_hb = s // hb
    kp = TAP_ROWS * ((k_taps + TAP_ROWS - 1) // TAP_ROWS)
    pr = 1 if mode == 'col' else TAP_ROWS

    def body(*refs):
        pos = 0

        def take(n):
            nonlocal pos
            out = refs[pos:pos + n]
            pos += n
            return out

        mains, prevs, nexts = take(nx), take(nx), take(nx)
        d_mains, d_nexts = take(ndo), take(ndo)
        w_refs, p_refs = take(ncv), take(npar)
        dx_refs, dw_refs, dp_refs = take(nx), take(ncv), take(npar)
        u_s, d_s = take(ncv), take(ndo)
        win_s, dcw_s, dwa_s = take(ncv), take(ncv), take(ncv)
        stu_s, std_s = take(ncv), take(ncv)
        i = pl.program_id(1)
        um = pre(*[r[...].astype(F32) for r in mains])
        up = pre(*[r[...].astype(F32) for r in prevs])
        un = pre(*[r[...].astype(F32) for r in nexts])
        first = (i > 0).astype(F32)
        last = (i < n_s - 1).astype(F32)
        for q in range(ncv):
            u_s[q][pl.ds(0, hb), :] = up[q] * first
            u_s[q][pl.ds(hb, ts), :] = um[q]
            u_s[q][pl.ds(hb + ts, hb), :] = un[q]
            dwa_s[q][...] = jnp.zeros((kp * 8, tc), F32)
        for q in range(ndo):
            d_s[q][pl.ds(0, ts), :] = d_mains[q][...].astype(F32)
            d_s[q][pl.ds(ts, hb), :] = d_nexts[q][...].astype(F32) * last
        pv = [r[0:1, :] for r in p_refs]
        fwd_offsets = [hb - (k_taps - 1) + t for t in range(k_taps)]

        def chunk(ci, dpar):
            r0 = pl.multiple_of(ci * rc, rc)
            c_own, c_next, shifted = [], [], []
            for q in range(ncv):
                win_s[q][...] = u_s[q][pl.ds(r0, rc + 2 * hb), :]
                acc = jnp.zeros((rc + hb, tc), F32)
                taps = _tap_windows(win_s[q], fwd_offsets, rc + hb, stu_s[q])
                for t, ut in enumerate(taps):
                    acc = acc + w_refs[q][pl.ds(t, 1), :] * ut()
                c_own.append(acc[:rc])
                c_next.append(acc[rc:])
                shifted.append(taps)
            _, vjp_o = jax.vjp(lambda c, p: tuple(post(c, p)), c_own, pv)
            dc_own, dp_own = vjp_o(tuple(r[pl.ds(r0, rc), :] for r in d_s))
            _, vjp_n = jax.vjp(lambda c: tuple(post(c, pv)), c_next)
            (dc_next,) = vjp_n(tuple(r[pl.ds(r0 + rc, hb), :] for r in d_s))
            dus = []
            for q in range(ncv):
                dcw_s[q][pl.ds(0, rc), :] = dc_own[q]
                dcw_s[q][pl.ds(rc, hb), :] = dc_next[q]
                acc = jnp.zeros((rc, tc), F32)
                bwd_offsets = [k_taps - 1 - t for t in range(k_taps)]
                for t, dct in enumerate(_tap_windows(dcw_s[q], bwd_offsets, rc, std_s[q])):
                    acc = acc + w_refs[q][pl.ds(t, 1), :] * dct()
                    prod = dc_own[q] * shifted[q][t]()[:rc]
                    dwa_s[q][pl.ds(8 * t, 8), :] += jnp.sum(prod.reshape(rc // 8, 8, tc), axis=0)
                dus.append(acc)
            xm = [r[pl.ds(r0, rc), :].astype(F32) for r in mains]
            _, vjp_p = jax.vjp(lambda *xv: tuple(pre(*xv)), *xm)
            dxs = vjp_p(tuple(dus))
            for q in range(nx):
                dx_refs[q][pl.ds(r0, rc), :] = dxs[q].astype(dx_refs[q].dtype)
            return tuple(a + b for a, b in zip(dpar, dp_own))

        dpar = lax.fori_loop(0, ts // rc, chunk, tuple(jnp.zeros((1, tc), F32) for _ in range(npar)))
        for q in range(ncv):
            @pl.when(i == 0)
            def _(q=q):
                dw_refs[q][...] = jnp.zeros((kp, tc), F32)

            for t in range(k_taps):
                dw_refs[q][pl.ds(t, 1), :] += jnp.sum(dwa_s[q][pl.ds(8 * t, 8), :], axis=0, keepdims=True)
        for q in range(npar):
            @pl.when(i == 0)
            def _(q=q):
                dp_refs[q][...] = jnp.zeros((pr, tc), F32)

            dp_refs[q][0:1, :] += dpar[q]

    main = lambda i: i
    prev = lambda i: jnp.maximum(i * rpb - 1, 0)
    nxt = lambda i: jnp.minimum((i + 1) * rpb, n_hb - 1)
    zero = lambda i: 0
    in_specs = [_cspec(mode, off, ts, tc, main, n_s) for _, off in xins]
    in_specs += [_cspec(mode, off, hb, tc, prev, n_hb) for _, off in xins]
    in_specs += [_cspec(mode, off, hb, tc, nxt, n_hb) for _, off in xins]
    in_specs += [_cspec(mode, off, ts, tc, main, n_s) for _, off in douts]
    in_specs += [_cspec(mode, off, hb, tc, nxt, n_hb) for _, off in douts]
    in_specs += [_cspec(mode, off, kp, tc, zero, 1) for off in woffs]
    in_specs += [_cspec(mode, off, pr, tc, zero, 1) for off in poffs]
    out_specs = [_cspec(mode, 0, ts, tc, main, n_s) for _ in xins]
    out_specs += [_cspec(mode, 0, kp, tc, zero, 1) for _ in woffs]
    out_specs += [_cspec(mode, 0, pr, tc, zero, 1) for _ in params]

    def shape(rows):
        return (rows, n_cb * tc) if mode == 'col' else (n_cb * rows, tc)

    out_shape = [jax.ShapeDtypeStruct(shape(s), dx_dtype) for _ in xins]
    out_shape += [jax.ShapeDtypeStruct(shape(kp), F32) for _ in woffs]
    out_shape += [jax.ShapeDtypeStruct(shape(pr), F32) for _ in params]
    xa = [a for a, _ in xins]
    da = [a for a, _ in douts]
    return pl.pallas_call(
        body, grid=(n_cb, n_s), in_specs=in_specs, out_specs=out_specs, out_shape=out_shape,
        scratch_shapes=[pltpu.VMEM((hb + ts + hb, tc), F32) for _ in range(ncv)]
        + [pltpu.VMEM((ts + hb, tc), F32) for _ in range(ndo)]
        + [pltpu.VMEM((rc + 2 * hb, tc), F32) for _ in range(ncv)]
        + [pltpu.VMEM((rc + hb, tc), F32) for _ in range(ncv)]
        + [pltpu.VMEM((kp * 8, tc), F32) for _ in range(ncv)]
        + [pltpu.VMEM(_stage_shape(k_taps, rc + hb, tc), F32) for _ in range(ncv)]
        + [pltpu.VMEM(_stage_shape(k_taps, rc, tc), F32) for _ in range(ncv)],
        compiler_params=pltpu.CompilerParams(dimension_semantics=("parallel", "arbitrary")), name=name,
    )(*xa, *xa, *xa, *da, *da, *([w] * ncv), *params)


def _glu_pre(a, g):
    return [a * jax.nn.sigmoid(g)]


def _ln_silu_post(cs, ps):
    c = cs[0] + ps[0]
    mu = jnp.mean(c, axis=-1, keepdims=True)
    var = jnp.mean(jnp.square(c - mu), axis=-1, keepdims=True)
    y = (c - mu) * lax.rsqrt(var + LN_EPS) * ps[1] + ps[2]
    return [jax.nn.silu(y)]


def _pair_pre(a, b):
    return [a, b]


def _gate_post(cs, ps):
    return [jax.nn.silu(cs[0] + ps[0]) * (cs[1] + ps[1])]


def _conva(s):
    return dict(mode='col', s=s, k_taps=CONV_K, hb=32, ts=512, tc=CONV_WIDTH, rc=64, n_cb=1)


def _gate(s):
    return dict(mode='row', s=s, k_taps=FFN_K, hb=8, ts=512, tc=FF_SHARD, rc=32, n_cb=FF_HALF)


def _rope_tables(s):
    half = QK_ROPE // 2
    inv = ROPE_BASE ** (-jnp.arange(half, dtype=F32) / half)
    ang = jnp.arange(s, dtype=F32)[:, None] * inv[None, :]
    cos, sin = jnp.cos(ang), jnp.sin(ang)
    z = lambda n: jnp.zeros((s, n), F32)
    c = jnp.concatenate([jnp.ones((s, QK_NOPE), F32), cos, cos, z(HEAD_PAD - QK_NOPE - QK_ROPE)], axis=1)
    s1 = jnp.concatenate([z(QK_NOPE), -sin, z(HEAD_PAD - QK_NOPE - half)], axis=1)
    s2 = jnp.concatenate([z(QK_NOPE + half), sin, z(HEAD_PAD - QK_NOPE - QK_ROPE)], axis=1)
    return c, s1, s2


def _rot(t, c, s1, s2):
    half = QK_ROPE // 2
    return t * c + pltpu.roll(t, HEAD_PAD - half, 1) * s1 + pltpu.roll(t, half, 1) * s2


def _rot_t(d, c, s1, s2):
    half = QK_ROPE // 2
    return d * c + pltpu.roll(d * s1, half, 1) + pltpu.roll(d * s2, HEAD_PAD - half, 1)


def _heads(v):
    return [v[:, h * HEAD_PAD:(h + 1) * HEAD_PAD] for h in range(N_HEADS)]


def _rope_fwd(qf, kvf, hmat, tabs, name):
    w = N_HEADS * HEAD_PAD

    def fn(q, k, v, kr, c, s1, s2):
        krr = _rot(kr, c, s1, s2)
        qo = jnp.concatenate([_rot(t, c, s1, s2) for t in _heads(q)], axis=1)
        ko = jnp.concatenate([t + krr for t in _heads(k)], axis=1)
        lane = lax.broadcasted_iota(jnp.int32, v.shape, 1) & (HEAD_PAD - 1)
        return qo, ko, jnp.where(lane == ONES_LANE, 1.0, v)

    rows = [(qf, w, 0), (kvf, w, 0), (kvf, w, 1), (hmat, HEAD_PAD, IN_PAD // HEAD_PAD - 1)]
    rows += [(t, HEAD_PAD, 0) for t in tabs]
    return _rowwise(fn, rows, [], [(w, BF16)] * 3, [], ts=512, name=name)


def _rope_bwd(dq, dk, dv, tabs, name):
    w = N_HEADS * HEAD_PAD

    def fn(dqv, dkv, dvv, c, s1, s2):
        dqo = jnp.concatenate([_rot_t(t, c, s1, s2) for t in _heads(dqv)], axis=1)
        ksum = functools.reduce(lambda a, b: a + b, _heads(dkv))
        return dqo, jnp.concatenate([dkv, dvv], axis=1), _rot_t(ksum, c, s1, s2)

    rows = [(dq, w, 0), (dk, w, 0), (dv, w, 0)] + [(t, HEAD_PAD, 0) for t in tabs]
    return _rowwise(fn, rows, [], [(w, BF16), (2 * w, BF16), (HEAD_PAD, F32)], [], ts=512, name=name)


ATT_Q = 1024
ATT_SUB = 512
ATT_KV = 1024
ATT_SCALE = (QK_NOPE + QK_ROPE) ** -0.5
LOG2E = 1.4426950408889634
ATT_C2 = ATT_SCALE * LOG2E
ONES_LANE = V_DIM


def _nt(a, b):
    return lax.dot_general(a, b, (((1,), (1,)), ((), ())), preferred_element_type=F32)


def _lanes(x, w):
    return x if w == HEAD_PAD else jnp.tile(x, (1, w // HEAD_PAD))


def _tri(w, transposed):
    r = lax.broadcasted_iota(jnp.int32, (w, w), 0)
    c = lax.broadcasted_iota(jnp.int32, (w, w), 1)
    return (r <= c) if transposed else (c <= r)


def _attn_fwd(q, k, v, name):
    s = q.shape[0]
    tq, kvc = min(ATT_Q, s), min(ATT_KV, s)
    nsub, per = tq // ATT_SUB, tq // kvc

    def body(q_ref, k_ref, v_ref, o_ref, lse_ref, m_s, acc_s):
        i = pl.program_id(1)
        m_s[...] = jnp.full((tq, HEAD_PAD), -jnp.inf, F32)
        acc_s[...] = jnp.zeros((tq, HEAD_PAD), F32)

        def update(r0, n, kb, vb, diag):
            rows = pl.ds(r0, n)
            w = kb.shape[0]
            sc = _nt(q_ref[rows, :], kb)
            if diag:
                sc = jnp.where(_tri(w, False), sc, -jnp.inf)
            m_prev = m_s[rows, :]
            m_next = jnp.maximum(m_prev, jnp.max(sc, axis=1, keepdims=True))
            p = jnp.exp2((sc - _lanes(m_next, w)) * ATT_C2)
            alpha = jnp.exp2((m_prev - m_next) * ATT_C2)
            acc_s[rows, :] = alpha * acc_s[rows, :] + jnp.dot(p.astype(BF16), vb, preferred_element_type=F32)
            m_s[rows, :] = m_next

        def below(j, carry):
            at = pl.ds(pl.multiple_of(j * kvc, kvc), kvc)
            update(0, tq, k_ref[at, :], v_ref[at, :], False)
            return carry

        lax.fori_loop(0, i * per, below, 0)
        for r in range(nsub):
            for c in range(r + 1):
                at = pl.ds(pl.multiple_of(i * tq + c * ATT_SUB, ATT_SUB), ATT_SUB)
                update(r * ATT_SUB, ATT_SUB, k_ref[at, :], v_ref[at, :], c == r)
        l = acc_s[:, ONES_LANE:ONES_LANE + 1]
        o_ref[...] = (acc_s[...] / l).astype(BF16)
        lse_ref[...] = m_s[...] * ATT_SCALE + jnp.log(l)

    q_spec = pl.BlockSpec((tq, HEAD_PAD), lambda h, i: (i, h))
    kv_spec = pl.BlockSpec((s, HEAD_PAD), lambda h, i: (0, h))
    return pl.pallas_call(
        body, grid=(N_HEADS, s // tq), in_specs=[q_spec, kv_spec, kv_spec], out_specs=[q_spec, q_spec],
        out_shape=[jax.ShapeDtypeStruct(q.shape, BF16), jax.ShapeDtypeStruct(q.shape, F32)],
        scratch_shapes=[pltpu.VMEM((tq, HEAD_PAD), F32)] * 2,
        compiler_params=pltpu.CompilerParams(dimension_semantics=("parallel", "arbitrary")), name=name,
    )(q, k, v)


def _attn_rows(cat, dcat, lse, name):
    s = lse.shape[0]
    tq, kvc = min(ATT_Q, s), min(ATT_KV, s)
    per = tq // kvc
    ob = CONV_WIDTH // HEAD_PAD

    def body(o_ref, do_ref, lse_ref, lser_ref, dltr_ref):
        dl = jnp.broadcast_to(jnp.sum(do_ref[...] * o_ref[...].astype(F32), axis=1, keepdims=True), (tq, HEAD_PAD))
        l2 = lse_ref[...] * LOG2E
        for c in range(per):
            lser_ref[c] = jnp.transpose(l2[c * kvc:(c + 1) * kvc])[0:8, :]
            dltr_ref[c] = jnp.transpose(dl[c * kvc:(c + 1) * kvc])[0:8, :]

    q_spec = pl.BlockSpec((tq, HEAD_PAD), lambda h, i: (i, h))
    o_spec = pl.BlockSpec((tq, HEAD_PAD), lambda h, i: (i, ob + h))
    row_spec = pl.BlockSpec((None, per, 8, kvc), lambda h, i: (h, i, 0, 0))
    rows = jax.ShapeDtypeStruct((N_HEADS, s // kvc, 8, kvc), F32)
    return pl.pallas_call(
        body, grid=(N_HEADS, s // tq), in_specs=[o_spec, o_spec, q_spec], out_specs=[row_spec, row_spec],
        out_shape=[rows, rows],
        compiler_params=pltpu.CompilerParams(dimension_semantics=("parallel", "parallel")), name=name,
    )(cat, dcat, lse)


def _attn_bwd(q, k, v, dcat, lse_r, dlt_r, name):
    s = q.shape[0]
    tk, kvc = min(ATT_Q, s), min(ATT_KV, s)
    nsub, per, n_chunks = tk // ATT_SUB, tk // kvc, s // kvc
    n_j = s // tk
    ob = CONV_WIDTH // HEAD_PAD

    def body(k_ref, v_ref, q_ref, do_ref, lse_ref, dl_ref, dk_ref, dv_ref, dq_ref, dk_s, dv_s):
        j = pl.program_id(1)
        dk_s[...] = jnp.zeros((tk, HEAD_PAD), F32)
        dv_s[...] = jnp.zeros((tk, HEAD_PAD), F32)

        @pl.when(j == 0)
        def _():
            dq_ref[...] = jnp.zeros((s, HEAD_PAD), F32)

        def update(r0, n, at, lrow, drow, diag):
            rows = pl.ds(r0, n)
            qb, dob = q_ref[at, :], do_ref[at, :].astype(BF16)
            sc = _nt(k_ref[rows, :], qb)
            if diag:
                sc = jnp.where(_tri(qb.shape[0], True), sc, -jnp.inf)
            p = jnp.exp2(sc * ATT_C2 - lrow)
            dp = _nt(v_ref[rows, :], dob)
            ds = (p * (dp - drow)).astype(BF16)
            dv_s[rows, :] += jnp.dot(p.astype(BF16), dob, preferred_element_type=F32)
            dk_s[rows, :] += jnp.dot(ds, qb, preferred_element_type=F32)
            dq_ref[at, :] += lax.dot_general(ds, k_ref[rows, :], (((0,), (0,)), ((), ())), preferred_element_type=F32)

        def above(ic, carry):
            at = pl.ds(pl.multiple_of(ic * kvc, kvc), kvc)
            update(0, tk, at, lse_ref[ic, 0:1, :], dl_ref[ic, 0:1, :], False)
            return carry

        lax.fori_loop((j + 1) * per, n_chunks, above, 0)
        for r in range(nsub):
            for c in range(r, nsub):
                at = pl.ds(pl.multiple_of(j * tk + c * ATT_SUB, ATT_SUB), ATT_SUB)
                ic = j * per + (c * ATT_SUB) // kvc
                lo = (c * ATT_SUB) % kvc
                update(r * ATT_SUB, ATT_SUB, at, lse_ref[ic, 0:1, lo:lo + ATT_SUB], dl_ref[ic, 0:1, lo:lo + ATT_SUB],
                       c == r)
        dk_ref[...] = dk_s[...] * ATT_SCALE
        dv_ref[...] = dv_s[...]

        @pl.when(j == n_j - 1)
        def _():
            dq_ref[...] = dq_ref[...] * ATT_SCALE

    kv_spec = pl.BlockSpec((tk, HEAD_PAD), lambda h, j: (j, h))
    q_spec = pl.BlockSpec((s, HEAD_PAD), lambda h, j: (0, h))
    do_spec = pl.BlockSpec((s, HEAD_PAD), lambda h, j: (0, ob + h))
    row_spec = pl.BlockSpec((None, n_chunks, 8, kvc), lambda h, j: (h, 0, 0, 0))
    full = jax.ShapeDtypeStruct(q.shape, F32)
    dk, dv, dq = pl.pallas_call(
        body, grid=(N_HEADS, n_j), in_specs=[kv_spec, kv_spec, q_spec, do_spec, row_spec, row_spec],
        out_specs=[kv_spec, kv_spec, q_spec], out_shape=[full, full, full],
        scratch_shapes=[pltpu.VMEM((tk, HEAD_PAD), F32)] * 2,
        compiler_params=pltpu.CompilerParams(dimension_semantics=("parallel", "arbitrary")), name=name,
    )(k, v, q, dcat, lse_r, dlt_r)
    return dq, dk, dv


SCAN_T = 256
SCAN_C = 512


def _scan(b, lam, *, reverse, xs=None, name):
    s = b.shape[0]
    t = min(SCAN_T, s)
    n_t, n_c = s // t, SSM_CH // SCAN_C
    with_dlam = xs is not None

    def shift(a, d, row):
        if d >= 8:
            z = jnp.zeros((d, SCAN_C), F32)
            return jnp.concatenate([a[d:], z], axis=0) if reverse else jnp.concatenate([z, a[:t - d]], axis=0)
        if reverse:
            return jnp.where(row < t - d, pltpu.roll(a, t - d, 0), 0.0)
        return jnp.where(row >= d, pltpu.roll(a, d, 0), 0.0)

    def body(*refs):
        if with_dlam:
            b_ref, lam_ref, x_ref, o_ref, dl_ref, c_s = refs
        else:
            b_ref, lam_ref, o_ref, c_s = refs
        k = pl.program_id(0)

        @pl.when(k == 0)
        def _():
            c_s[...] = jnp.zeros((1, 2 * SSM_CH), F32)
            if with_dlam:
                dl_ref[...] = jnp.zeros((1, 2 * SSM_CH), F32)

        row = lax.broadcasted_iota(jnp.int32, (t, SCAN_C), 0)
        edge = (row == t - 1) if reverse else (row == 0)
        for ch in range(n_c):
            re = pl.ds(ch * SCAN_C, SCAN_C)
            im = pl.ds(SSM_CH + ch * SCAN_C, SCAN_C)
            lr = lam_ref[:, re]
            li = -lam_ref[:, im] if reverse else lam_ref[:, im]
            cr, ci = c_s[:, re], c_s[:, im]
            ar = b_ref[:, re] + jnp.where(edge, lr * cr - li * ci, 0.0)
            ai = b_ref[:, im] + jnp.where(edge, lr * ci + li * cr, 0.0)
            d = 1
            while d < t:
                sr, si = shift(ar, d, row), shift(ai, d, row)
                ar, ai = ar + lr * sr - li * si, ai + lr * si + li * sr
                lr, li = lr * lr - li * li, 2.0 * lr * li
                d *= 2
            o_ref[:, re] = ar.astype(o_ref.dtype)
            o_ref[:, im] = ai.astype(o_ref.dtype)
            if with_dlam:
                gr = jnp.where(edge, cr, shift(ar, 1, row))
                gi = jnp.where(edge, ci, shift(ai, 1, row))
                xr, xi = x_ref[:, re].astype(F32), x_ref[:, im].astype(F32)
                dl_ref[:, re] += jnp.sum(xr * gr + xi * gi, axis=0, keepdims=True)
                dl_ref[:, im] += jnp.sum(xr * gi - xi * gr, axis=0, keepdims=True)
            last = 0 if reverse else t - 1
            c_s[:, re] = ar[last:last + 1, :]
            c_s[:, im] = ai[last:last + 1, :]

    tm = (lambda k: (n_t - 1 - k, 0)) if reverse else (lambda k: (k, 0))
    blk = pl.BlockSpec((t, 2 * SSM_CH), tm)
    vec = pl.BlockSpec((1, 2 * SSM_CH), lambda k: (0, 0))
    in_specs, args = [blk, vec], [b, lam]
    out_specs, out_shape = [blk], [jax.ShapeDtypeStruct((s, 2 * SSM_CH), BF16)]
    if with_dlam:
        in_specs.append(blk)
        args.append(xs)
        out_specs.append(vec)
        out_shape.append(jax.ShapeDtypeStruct((1, 2 * SSM_CH), F32))
    return pl.pallas_call(
        body, grid=(n_t,), in_specs=in_specs, out_specs=out_specs, out_shape=out_shape,
        scratch_shapes=[pltpu.VMEM((1, 2 * SSM_CH), F32)],
        compiler_params=pltpu.CompilerParams(dimension_semantics=("arbitrary",)), name=name,
    )(*args)


def _s5_disc(log_dt, a_re, a_im, b_re, b_im):
    dt = jnp.exp(log_dt)[:, None]
    mag = jnp.exp(a_re * dt)
    lb_re, lb_im = mag * jnp.cos(a_im * dt), mag * jnp.sin(a_im * dt)
    den = a_re * a_re + a_im * a_im
    nr, ni = lb_re - 1.0, lb_im
    f_re = (nr * a_re + ni * a_im) / den
    f_im = (ni * a_re - nr * a_im) / den
    bb_re = f_re[..., None] * b_re - f_im[..., None] * b_im
    bb_im = f_re[..., None] * b_im + f_im[..., None] * b_re
    return lb_re, lb_im, bb_re, bb_im


def _bd(a):
    g, i, j = a.shape
    eye = jnp.eye(g, dtype=a.dtype)
    return (a[:, :, None, :] * eye[:, None, :, None]).reshape(g * i, g * j)


S5_TILE_GROUPS = HEAD_PAD // SSM_GROUP


def _s5_sparse(s):
    nb = SSM_GROUPS // S5_TILE_GROUPS
    cw, sw = S5_TILE_GROUPS * SSM_GROUP, S5_TILE_GROUPS * SSM_STATE
    tm = min(1024, s)
    return dict(
        expand=dict(tiles=(tm, sw, cw), out=(s, 2 * SSM_CH), nk=1, k_of=lambda i, j, k: j % nb),
        reduce=dict(tiles=(tm, cw, sw), out=(s, SSM_WIDTH), nk=2, k_of=lambda i, j, k: j + nb * k),
        wide_t=dict(tiles=(sw, cw, tm), out=(2 * SSM_CH, cw), nk=s // tm, b_n=lambda i, j: i % nb),
        narrow_t=dict(tiles=(cw, sw, tm), out=(cw, 2 * SSM_CH), nk=s // tm, a_m=lambda i, j: j % nb))


def _exchange(arrs, modes, name):
    n = len(arrs)
    shapes = [a.shape if md == 'scatter' else (N_DEV,) + a.shape for a, md in zip(arrs, modes)]

    def body(*refs):
        srcs, outs = refs[:n], refs[n:2 * n]
        send_sems, recv_sems = refs[2 * n:]
        x, y, c = lax.axis_index("x"), lax.axis_index("y"), lax.axis_index("c")
        me = 4 * x + 2 * y + c
        copies = []
        for r, pos, peer in _peers(x, y, c):
            for q in range(n):
                copies.append(pltpu.make_async_remote_copy(
                    src_ref=srcs[q].at[peer] if modes[q] == 'scatter' else srcs[q], dst_ref=outs[q].at[me],
                    send_sem=send_sems.at[(r - 1) * n + q], recv_sem=recv_sems.at[(r - 1) * n + q], device_id=pos,
                    device_id_type=pl.DeviceIdType.MESH))
        for cp in copies:
            cp.start()
        for cp in copies:
            cp.wait_recv()
        for cp in copies:
            cp.wait_send()

    any_spec = pl.BlockSpec(memory_space=pl.ANY)
    outs = pl.pallas_call(
        body, out_shape=[jax.ShapeDtypeStruct(sh, a.dtype) for sh, a in zip(shapes, arrs)],
        in_specs=[any_spec] * n, out_specs=[any_spec] * n,
        scratch_shapes=[pltpu.SemaphoreType.DMA(((N_DEV - 1) * n,)), pltpu.SemaphoreType.DMA(((N_DEV - 1) * n,))],
        compiler_params=pltpu.CompilerParams(has_side_effects=True), name=name,
    )(*arrs)
    return _own_slots(outs, arrs, modes)


def _peers(x, y, c):
    out = []
    for r in range(1, N_DEV):
        px, py, pc = x ^ (r >> 2), y ^ ((r >> 1) & 1), c ^ (r & 1)
        out.append((r, (px, py, pc), 4 * px + 2 * py + pc))
    return out


def _own_slots(lands, srcs, modes):
    me = 4 * lax.axis_index("x") + 2 * lax.axis_index("y") + lax.axis_index("c")
    out = []
    for land, src, md in zip(lands, srcs, modes):
        own = lax.dynamic_index_in_dim(src, me, 0, keepdims=False) if md == 'scatter' else src
        out.append(lax.dynamic_update_index_in_dim(land, own, me, 0))
    return out


def _exchange_start(arrs, modes, carry, name):
    n = len(arrs)
    shapes = [a.shape if md == 'scatter' else (N_DEV,) + a.shape for a, md in zip(arrs, modes)]
    lands = [lax.empty(sh, a.dtype) for sh, a in zip(shapes, arrs)]

    def body(*refs):
        srcs, zones = refs[:n], refs[n:2 * n]
        send_sems, recv_sems = refs[2 * n + 1], refs[2 * n + 2]
        x, y, c = lax.axis_index("x"), lax.axis_index("y"), lax.axis_index("c")
        me = 4 * x + 2 * y + c
        for r, pos, peer in _peers(x, y, c):
            for q in range(n):
                src = srcs[q].at[peer] if modes[q] == 'scatter' else srcs[q]
                pltpu.make_async_remote_copy(
                    src_ref=src, dst_ref=zones[q].at[me], send_sem=send_sems.at[(r - 1) * n + q],
                    recv_sem=recv_sems.at[(r - 1) * n + q], device_id=pos, device_id_type=pl.DeviceIdType.MESH).start()

    hbm = pl.BlockSpec(memory_space=pltpu.HBM)
    sem = pl.BlockSpec(memory_space=pltpu.SEMAPHORE)
    thru = arrs + lands + [carry]
    sems = pltpu.SemaphoreType.DMA(((N_DEV - 1) * n,))
    outs = pl.pallas_call(
        body, name=name, out_shape=(sems, sems, *[pltpu.HBM(a.shape, a.dtype) for a in thru]),
        in_specs=[hbm] * len(thru), out_specs=(sem, sem, *[hbm] * len(thru)),
        input_output_aliases={q: 2 + q for q in range(len(thru))},
        compiler_params=pltpu.CompilerParams(has_side_effects=pltpu.SideEffectType.DATAFLOW_SIDE_EFFECTING),
    )(*[pltpu.with_memory_space_constraint(a, pltpu.HBM) for a in thru])
    return dict(send=outs[0], recv=outs[1], srcs=list(outs[2:2 + n]), lands=list(outs[2 + n:2 + 2 * n]),
                modes=modes), outs[-1]


def _exchange_wait(flight, after, name):
    n = len(flight['srcs'])
    modes = flight['modes']

    def body(*refs):
        srcs, zones = refs[:n], refs[n:2 * n]
        send_sems, recv_sems = refs[2 * n], refs[2 * n + 1]
        x, y, c = lax.axis_index("x"), lax.axis_index("y"), lax.axis_index("c")
        me = 4 * x + 2 * y + c
        for r, pos, peer in _peers(x, y, c):
            for q in range(n):
                src = srcs[q].at[peer] if modes[q] == 'scatter' else srcs[q]
                cp = pltpu.make_async_remote_copy(
                    src_ref=src, dst_ref=zones[q].at[me], send_sem=send_sems.at[(r - 1) * n + q],
                    recv_sem=recv_sems.at[(r - 1) * n + q], device_id=pos, device_id_type=pl.DeviceIdType.MESH)
                cp.wait_send()
                cp.wait_recv()

    hbm = pl.BlockSpec(memory_space=pltpu.HBM)
    sem = pl.BlockSpec(memory_space=pltpu.SEMAPHORE)
    bufs = flight['srcs'] + flight['lands']
    outs = pl.pallas_call(
        body, name=name, out_shape=tuple(pltpu.HBM(a.shape, a.dtype) for a in bufs),
        in_specs=[hbm] * (2 * n) + [sem, sem, pl.BlockSpec(memory_space=pl.ANY)], out_specs=tuple([hbm] * (2 * n)),
        input_output_aliases={q: q for q in range(2 * n)},
        compiler_params=pltpu.CompilerParams(has_side_effects=pltpu.SideEffectType.DATAFLOW_SIDE_EFFECTING),
    )(*bufs, flight['send'], flight['recv'], after)
    return _own_slots(outs[n:], outs[:n], modes)


def _adamw(parts, w, m, v, name):
    r, c = w.shape
    tr = _tile(r, (256, 128))

    def body(p_ref, w_ref, m_ref, v_ref, g_ref, d_ref, nm_ref, nv_ref):
        g = p_ref[0].astype(F32)
        for d in range(1, N_DEV):
            g = g + p_ref[d].astype(F32)
        m2 = ADAM_B1 * m_ref[...] + (1.0 - ADAM_B1) * g
        v2 = ADAM_B2 * v_ref[...] + (1.0 - ADAM_B2) * jnp.square(g)
        m_hat = m2 / (1.0 - ADAM_B1 ** ADAM_STEP)
        v_hat = v2 / (1.0 - ADAM_B2 ** ADAM_STEP)
        g_ref[...] = g
        d_ref[...] = -ADAM_LR * (m_hat / (jnp.sqrt(v_hat) + ADAM_EPS) + ADAM_WD * w_ref[...])
        nm_ref[...] = m2
        nv_ref[...] = v2

    spec = pl.BlockSpec((tr, c), lambda i: (i, 0))
    return pl.pallas_call(
        body, grid=(r // tr,), in_specs=[pl.BlockSpec((N_DEV, tr, c), lambda i: (0, i, 0)), spec, spec, spec],
        out_specs=[spec] * 4, out_shape=[jax.ShapeDtypeStruct((r, c), F32)] * 4,
        compiler_params=pltpu.CompilerParams(dimension_semantics=("parallel",)), name=name,
    )(parts, w, m, v)


FLAT_W = 512
FLAT_ROWS = 256


def _flat(arrs):
    v = jnp.concatenate([a.reshape(-1) for a in arrs])
    return jnp.pad(v, (0, (-v.shape[0]) % (FLAT_ROWS * FLAT_W))).reshape(-1, FLAT_W)


def _unflat(flat, shapes):
    v = flat.reshape(-1)
    out, off = [], 0
    for sh in shapes:
        n = 1
        for d in sh:
            n *= d
        out.append(v[off:off + n].reshape(sh))
        off += n
    return out


def _full(name, stacked):
    if SHARDED[name] == 0:
        return stacked.reshape((-1,) + stacked.shape[2:])
    return jnp.transpose(stacked, (1, 0, 2)).reshape(stacked.shape[1], -1)


def _shards(name, full):
    if SHARDED[name] == 0:
        return full.reshape((N_DEV, -1) + full.shape[1:])
    r, c = full.shape
    return jnp.transpose(full.reshape(r, N_DEV, c // N_DEV), (1, 0, 2))


def _prep_weights(p):
    q = {}
    w_in = p['l0_w_in']
    z = lambda n: jnp.zeros((D_MODEL, n), w_in.dtype)
    q['w_in'] = jnp.concatenate([w_in[:, :IN_EVEN - QK_ROPE], z(KR_LANE), w_in[:, IN_EVEN - QK_ROPE:],
                                 z(HEAD_PAD - KR_LANE - QK_ROPE)], axis=1)
    dqk = QK_NOPE + QK_ROPE
    q['w_uq'] = jnp.pad(p['l0_w_uq'].reshape(Q_RANK, N_HEADS, dqk), ((0, 0), (0, 0), (0, HEAD_PAD - dqk))
                        ).reshape(Q_RANK, N_HEADS * HEAD_PAD)
    ukv = p['l0_w_ukv'].reshape(KV_RANK, N_HEADS, 2, QK_NOPE)
    padh = lambda a: jnp.pad(a, ((0, 0), (0, 0), (0, HEAD_PAD - QK_NOPE))).reshape(KV_RANK, N_HEADS * HEAD_PAD)
    q['w_ukv'] = jnp.concatenate([padh(ukv[:, :, 0]), padh(ukv[:, :, 1])], axis=1)
    wo = p['l0_w_out']
    wo_a = jnp.pad(wo[CONV_WIDTH:].reshape(N_HEADS, V_DIM, D_MODEL), ((0, 0), (0, HEAD_PAD - V_DIM), (0, 0)))
    q['w_out'] = jnp.concatenate([wo[:CONV_WIDTH], wo_a.reshape(N_HEADS * HEAD_PAD, D_MODEL)], axis=0)
    return q


def _unprep_grads(g):
    out = {}
    d = g['w_in']
    out['l0_w_in'] = jnp.concatenate([d[:, :IN_EVEN - QK_ROPE],
                                      d[:, IN_EVEN - QK_ROPE + KR_LANE:IN_EVEN + KR_LANE]], axis=1)
    dqk = QK_NOPE + QK_ROPE
    out['l0_w_uq'] = g['w_uq'].reshape(Q_RANK, N_HEADS, HEAD_PAD)[:, :, :dqk].reshape(Q_RANK, N_HEADS * dqk)
    d = g['w_ukv'].reshape(KV_RANK, 2, N_HEADS, HEAD_PAD)[:, :, :, :QK_NOPE]
    out['l0_w_ukv'] = jnp.transpose(d, (0, 2, 1, 3)).reshape(KV_RANK, N_HEADS * 2 * QK_NOPE)
    d = g['w_out']
    da = d[CONV_WIDTH:].reshape(N_HEADS, HEAD_PAD, D_MODEL)[:, :V_DIM].reshape(N_HEADS * V_DIM, D_MODEL)
    out['l0_w_out'] = jnp.concatenate([d[:CONV_WIDTH], da], axis=0)
    return out


def _pad_rows(w, rows):
    return jnp.pad(w, [(0, 0)] * (w.ndim - 2) + [(0, rows - w.shape[-2]), (0, 0)])


def _ffn_fwd(x, rep, got, pre, tag):
    s = x.shape[0]
    xn = _rms_fwd(x, rep[pre + 'ffn_norm'], f"{tag}_ffn_norm")
    w_up = got[pre + 'w_up']
    hu = _mm(xn, w_up, gb=_same, go=_same, groups=N_DEV, name=f"{tag}_ffn_up").reshape(N_DEV * s, FF_SHARD)
    taps = _pad_rows(got[pre + 'ffn_conv_w'], TAP_ROWS).reshape(N_DEV * TAP_ROWS, FF_SHARD)
    bias = _pad_rows(rep[pre + 'ffn_conv_b'].reshape(N_DEV, 1, FF_SHARD), TAP_ROWS).reshape(N_DEV * TAP_ROWS, FF_SHARD)
    (act,) = _conv_fwd([(hu, 0), (hu, FF_HALF)], taps, [0, FF_HALF], [bias, bias], [0, FF_HALF], _pair_pre,
                       _gate_post, [BF16], name=f"{tag}_ffn_gate", **_gate(s))
    act = act.reshape(FF_HALF, s, FF_SHARD)
    w_down = got[pre + 'w_down'].reshape(FF_HALF, FF_SHARD, D_MODEL)
    y = _mm(act, w_down, ga=_same, gb=_same, groups=FF_HALF, res=x, name=f"{tag}_ffn_down")
    return y, (x, xn, hu, act, taps, bias, w_up, w_down)


def _ffn_bwd(dy, saved, rep, pre, tag, grads, gsh):
    x, xn, hu, act, taps, bias, w_up, w_down = saved
    s = x.shape[0]
    dact = _mm(dy, w_down, tb=True, gb=_same, go=_same, groups=FF_HALF, name=f"{tag}_ffn_down_dx")
    gsh[pre + 'w_down'] = _mm(act, dy, ta=True, ga=_same, go=_same, groups=FF_HALF, out_dtype=BF16,
                              name=f"{tag}_ffn_down_dw").reshape(N_DEV, FF_SHARD // 2, D_MODEL)
    dha, dhb, dwa, dwb, dba, dbb = _conv_bwd(
        [(hu, 0), (hu, FF_HALF)], taps, [0, FF_HALF], [bias, bias], [0, FF_HALF], _pair_pre, _gate_post,
        [(dact.reshape(FF_HALF * s, FF_SHARD), 0)], BF16, name=f"{tag}_ffn_gate_bwd", **_gate(s))
    dha, dhb = dha.reshape(FF_HALF, s, FF_SHARD), dhb.reshape(FF_HALF, s, FF_SHARD)
    dtaps = jnp.concatenate([dwa, dwb], axis=0).reshape(N_DEV, TAP_ROWS, FF_SHARD)
    gsh[pre + 'ffn_conv_w'] = dtaps[:, :FFN_K].astype(BF16)
    grads[pre + 'ffn_conv_b'] = jnp.concatenate([dba, dbb], axis=0).reshape(N_DEV, TAP_ROWS, FF_SHARD)[:, 0].reshape(-1)
    upper = lambda g: g + FF_HALF
    dxn = _mm(dha, w_up, tb=True, ga=_same, gb=_same, groups=FF_HALF, name=f"{tag}_ffn_up_dx_a")
    dxn = _mm(dhb, w_up, tb=True, ga=_same, gb=upper, groups=FF_HALF, res=dxn, name=f"{tag}_ffn_up_dx_b")
    dwu = [_mm(xn, dh, ta=True, gb=_same, go=_same, groups=FF_HALF, out_dtype=BF16, name=f"{tag}_ffn_up_dw_{t}")
           for t, dh in (("a", dha), ("b", dhb))]
    gsh[pre + 'w_up'] = jnp.concatenate(dwu, axis=0)
    dx, dg = _rms_bwd(x, rep[pre + 'ffn_norm'], dxn, dy, f"{tag}_ffn_norm_bwd")
    grads[pre + 'ffn_norm'] = dg.reshape(-1)
    return dx


def _mla_fwd(x, rep, taps, q, tabs):
    s = x.shape[0]
    xn = _rms_fwd(x, rep['l0_mix_norm'], "l0_mix_norm")
    hmat = _mm(xn, q['w_in'], name="l0_in")
    wt = _pad_rows(taps, 4 * TAP_ROWS)
    cpar = [rep['l0_conv_b'].reshape(1, -1), rep['l0_conv_ln_g'].reshape(1, -1), rep['l0_conv_ln_b'].reshape(1, -1)]
    (u,) = _conv_fwd([(hmat, 0), (hmat, 1)], wt, [0], cpar, [0, 0, 0], _glu_pre, _ln_silu_post, [BF16],
                     name="l0_conv", **_conva(s))
    qn, kvn = rep['l0_q_norm'].reshape(1, -1), rep['l0_kv_norm'].reshape(1, -1)
    cqn, ckvn = _rowwise(lambda a, b, ga, gb: (_rms(a, ga), _rms(b, gb)),
                         [(hmat, Q_RANK, 2 * CONV_WIDTH // Q_RANK), (hmat, KV_RANK, (2 * CONV_WIDTH + Q_RANK) // KV_RANK)],
                         [qn, kvn], [(Q_RANK, BF16), (KV_RANK, BF16)], [], ts=512, name="l0_latent_norm")
    qf = _mm(cqn, q['w_uq'], name="l0_uq")
    kvf = _mm(ckvn, q['w_ukv'], name="l0_ukv")
    q_rot, k_full, v = _rope_fwd(qf, kvf, hmat, tabs, "l0_rope")
    o, lse = _attn_fwd(q_rot, k_full, v, "l0_attn")
    cat = jnp.concatenate([u, o], axis=1)
    y = _mm(cat, q['w_out'], res=x, name="l0_out")
    return y, (x, xn, hmat, wt, cpar, qn, kvn, cqn, ckvn, q_rot, k_full, v, lse, cat)


def _mla_bwd(dy, saved, rep, q, tabs, grads, gq):
    x, xn, hmat, wt, cpar, qn, kvn, cqn, ckvn, q_rot, k_full, v, lse, cat = saved
    s = x.shape[0]
    dcat = _mm(dy, q['w_out'], tb=True, name="l0_out_dx")
    gq['w_out'] = _mm(cat, dy, ta=True, name="l0_out_dw")
    lse_r, dlt_r = _attn_rows(cat, dcat, lse, "l0_attn_rows")
    dq, dk, dv = _attn_bwd(q_rot, k_full, v, dcat, lse_r, dlt_r, "l0_attn_bwd")
    dqf, dkvf, dkr = _rope_bwd(dq, dk, dv, tabs, "l0_rope_bwd")
    dcqn = _mm(dqf, q['w_uq'], tb=True, name="l0_uq_dx")
    gq['w_uq'] = _mm(cqn, dqf, ta=True, name="l0_uq_dw")
    dckvn = _mm(dkvf, q['w_ukv'], tb=True, name="l0_ukv_dx")
    gq['w_ukv'] = _mm(ckvn, dkvf, ta=True, name="l0_ukv_dw")

    def lat_bwd(a, b, da, db, ga, gb):
        _, vjp = jax.vjp(lambda a_, b_, ga_, gb_: (_rms(a_, ga_), _rms(b_, gb_)), a, b, ga, gb)
        return vjp((da, db))

    dcq, dckv, dqn, dkvn = _rowwise(
        lat_bwd, [(hmat, Q_RANK, 2 * CONV_WIDTH // Q_RANK), (hmat, KV_RANK, (2 * CONV_WIDTH + Q_RANK) // KV_RANK),
                  (dcqn, Q_RANK, 0), (dckvn, KV_RANK, 0)],
        [qn, kvn], [(Q_RANK, F32), (KV_RANK, F32)], [(1, Q_RANK), (1, KV_RANK)], ts=512, name="l0_latent_norm_bwd")
    grads['l0_q_norm'], grads['l0_kv_norm'] = dqn.reshape(-1), dkvn.reshape(-1)
    da, dg, dwt, dcb, dlg, dlb = _conv_bwd(
        [(hmat, 0), (hmat, 1)], wt, [0], cpar, [0, 0, 0], _glu_pre, _ln_silu_post, [(dcat, 0)], F32,
        name="l0_conv_bwd", **_conva(s))
    gq['conv_w'] = dwt[:CONV_K]
    grads['l0_conv_b'], grads['l0_conv_ln_g'], grads['l0_conv_ln_b'] = dcb.reshape(-1), dlg.reshape(-1), dlb.reshape(-1)
    dh = jnp.concatenate([da, dg, dcq, dckv, dkr], axis=1)
    dxn = _mm(dh, q['w_in'], tb=True, name="l0_in_dx")
    gq['w_in'] = _mm(xn, dh, ta=True, name="l0_in_dw")
    dx, dgn = _rms_bwd(x, rep['l0_mix_norm'], dxn, dy, "l0_mix_norm_bwd")
    grads['l0_mix_norm'] = dgn.reshape(-1)
    return dx


def _gelu_skip(yc, u, d):
    return jax.nn.gelu(yc + d * u)


def _glu_out(z1, z2, b1, b2, x):
    return x + (z1 + b1) * jax.nn.sigmoid(z2 + b2)


def _s5_fwd(x, rep, w_in, w_glu):
    xn = _rms_fwd(x, rep['l1_mix_norm'], "l1_mix_norm")
    u = _mm(xn, w_in, name="l1_in")
    lb_re, lb_im, bb_re, bb_im = _s5_disc(rep['l1_log_dt'], rep['l1_a_re'], rep['l1_a_im'], rep['l1_b_re'],
                                          rep['l1_b_im'])
    lam = jnp.concatenate([lb_re.reshape(1, -1), lb_im.reshape(1, -1)], axis=1)
    tr = lambda a: jnp.transpose(a, (0, 2, 1))
    bmat = jnp.concatenate([_bd(tr(bb_re)), _bd(tr(bb_im))], axis=1)
    cmat = jnp.concatenate([_bd(tr(rep['l1_c_re'])), -_bd(tr(rep['l1_c_im']))], axis=0)
    sp = _s5_sparse(x.shape[0])
    bu = _mm(u, bmat, sparse=sp['expand'], name="l1_bu")
    (xs,) = _scan(bu, lam, reverse=False, name="l1_scan")
    yc = _mm(xs, cmat, sparse=sp['reduce'], name="l1_cx")
    dsk = rep['l1_d'].reshape(1, -1)
    (y,) = _rowwise(_gelu_skip, [(yc, SSM_WIDTH, 0), (u, SSM_WIDTH, 0)], [dsk], [(SSM_WIDTH, BF16)], [],
                    ts=512, name="l1_gelu")
    z = _mm(y, w_glu, name="l1_glu")
    bg = rep['l1_b_glu'].reshape(1, -1)
    (out,) = _rowwise(lambda z1, z2, xv, b1, b2: _glu_out(z1, z2, b1, b2, xv),
                      [(z, D_MODEL, 0), (z, D_MODEL, 1), (x, D_MODEL, 0)], [bg[:, :D_MODEL], bg[:, D_MODEL:]],
                      [(D_MODEL, F32)], [], ts=512, name="l1_glu_out")
    return out, (x, xn, u, lam, bmat, cmat, xs, yc, dsk, y, z, bg, w_in, w_glu)


def _s5_bwd(dy, saved, rep, grads, gq):
    x, xn, u, lam, bmat, cmat, xs, yc, dsk, y, z, bg, w_in, w_glu = saved

    def glu_bwd(z1, z2, dv, b1, b2):
        _, vjp = jax.vjp(lambda a, b, c, d: (a + c) * jax.nn.sigmoid(b + d), z1, z2, b1, b2)
        d1, d2, db1, db2 = vjp(dv)
        return jnp.concatenate([d1, d2], axis=1), db1, db2

    dz, db1, db2 = _rowwise(glu_bwd, [(z, D_MODEL, 0), (z, D_MODEL, 1), (dy, D_MODEL, 0)],
                            [bg[:, :D_MODEL], bg[:, D_MODEL:]], [(2 * D_MODEL, BF16)], [(1, D_MODEL), (1, D_MODEL)],
                            ts=512, name="l1_glu_out_bwd")
    grads['l1_b_glu'] = jnp.concatenate([db1, db2], axis=1).reshape(-1)
    dyv = _mm(dz, w_glu, tb=True, name="l1_glu_dx")
    gq['l1_w_glu'] = _mm(y, dz, ta=True, name="l1_glu_dw")

    def gelu_bwd(ycv, uv, dv, dk):
        _, vjp = jax.vjp(_gelu_skip, ycv, uv, dk)
        return vjp(dv)

    dyc, du_skip, dd = _rowwise(gelu_bwd, [(yc, SSM_WIDTH, 0), (u, SSM_WIDTH, 0), (dyv, SSM_WIDTH, 0)], [dsk],
                                [(SSM_WIDTH, F32), (SSM_WIDTH, F32)], [(1, SSM_WIDTH)], ts=512, name="l1_gelu_bwd")
    grads['l1_d'] = dd.reshape(-1)
    sp = _s5_sparse(x.shape[0])
    dxs = _mm(dyc, cmat, tb=True, sparse=sp['expand'], name="l1_cx_dx")
    dcm = _mm(xs, dyc, ta=True, sparse=sp['wide_t'], name="l1_cx_dw")
    gs, dlam = _scan(dxs, lam, reverse=True, xs=xs, name="l1_scan_bwd")
    dlr, dli = dlam[:, :SSM_CH], dlam[:, SSM_CH:]
    du = _mm(gs, bmat, tb=True, res=du_skip, sparse=sp['reduce'], name="l1_bu_dx")
    dbm = _mm(u, gs, ta=True, sparse=sp['narrow_t'], name="l1_bu_dw")
    eye = jnp.eye(S5_TILE_GROUPS, dtype=F32)
    nb = SSM_GROUPS // S5_TILE_GROUPS
    dcm = dcm.reshape(2, nb, S5_TILE_GROUPS, SSM_STATE, S5_TILE_GROUPS, SSM_GROUP)
    dcm = jnp.sum(dcm * eye[None, None, :, None, :, None], axis=4).reshape(2, SSM_GROUPS, SSM_STATE, SSM_GROUP)
    tr = lambda a: jnp.transpose(a, (0, 2, 1))
    grads['l1_c_re'], grads['l1_c_im'] = tr(dcm[0]), -tr(dcm[1])
    dbm = dbm.reshape(S5_TILE_GROUPS, SSM_GROUP, 2, nb, S5_TILE_GROUPS, SSM_STATE)
    dbm = jnp.sum(dbm * eye[:, None, None, None, :, None], axis=0)
    dbm = jnp.transpose(dbm, (1, 2, 3, 4, 0)).reshape(2, SSM_GROUPS, SSM_STATE, SSM_GROUP)
    dbb_re, dbb_im = dbm[0], dbm[1]
    names = ['l1_log_dt', 'l1_a_re', 'l1_a_im', 'l1_b_re', 'l1_b_im']
    _, vjp = jax.vjp(_s5_disc, *[rep[n] for n in names])
    for n, gval in zip(names, vjp((dlr.reshape(SSM_GROUPS, SSM_STATE), dli.reshape(SSM_GROUPS, SSM_STATE), dbb_re, dbb_im))):
        grads[n] = gval
    dxn = _mm(du, w_in, tb=True, name="l1_in_dx")
    gq['l1_w_in'] = _mm(xn, du, ta=True, name="l1_in_dw")
    dx, dgn = _rms_bwd(x, rep['l1_mix_norm'], dxn, dy, "l1_mix_norm_bwd")
    grads['l1_mix_norm'] = dgn.reshape(-1)
    return dx


def _loss_head(x, g, target):
    d = x.shape[1]

    def fn(xv, tv, gv):
        y, vjp = jax.vjp(_rms, xv, gv)
        err = y - tv
        part = 0.5 * jnp.sum(jnp.mean(jnp.square(err), axis=-1, keepdims=True), axis=0, keepdims=True)
        dx, dg = vjp(err * (1.0 / d))
        return dx, jnp.broadcast_to(part, (1, 128)), dg

    return _rowwise(fn, [(x, d, 0), (target, d, 0)], [g.reshape(1, -1)], [(d, F32)], [(1, 128), (1, d)], ts=512,
                    name="loss_head")


FIRST = ('l0_w_in', 'l0_conv_w', 'l0_w_uq', 'l0_w_ukv', 'l0_w_out')
REST = tuple(n for n in SHARDED if n not in FIRST)
GRADS_L1 = ('l1_w_in', 'l1_w_glu', 'l1_w_up', 'l1_ffn_conv_w', 'l1_w_down')
GRADS_L0_FFN = ('l0_w_up', 'l0_ffn_conv_w', 'l0_w_down')


def _local_step(x, target, rep, got, wait_rest, send_grads):
    even = {n: _full(n, got[n]) for n in ('l0_w_in', 'l0_w_uq', 'l0_w_ukv', 'l0_w_out')}
    q = _prep_weights(even)
    tabs = _rope_tables(x.shape[0])
    x1, s_mla = _mla_fwd(x, rep, _full('l0_conv_w', got['l0_conv_w']), q, tabs)
    got = wait_rest(x1)
    x2, s_f0 = _ffn_fwd(x1, rep, got, 'l0_', "l0")
    x3, s_s5 = _s5_fwd(x2, rep, _full('l1_w_in', got['l1_w_in']), _full('l1_w_glu', got['l1_w_glu']))
    x4, s_f1 = _ffn_fwd(x3, rep, got, 'l1_', "l1")
    dx4, loss, dgf = _loss_head(x4, rep['final_norm'], target)
    grads, gq, gsh = {'final_norm': dgf.reshape(-1)}, {}, {}
    dx3 = _ffn_bwd(dx4, s_f1, rep, 'l1_', "l1", grads, gsh)
    dx2 = _s5_bwd(dx3, s_s5, rep, grads, gq)
    for n in ('l1_w_in', 'l1_w_glu'):
        gsh[n] = _shards(n, gq[n]).astype(BF16)
    dx2 = send_grads(GRADS_L1, gsh, dx2)
    dx1 = _ffn_bwd(dx2, s_f0, rep, 'l0_', "l0", grads, gsh)
    dx1 = send_grads(GRADS_L0_FFN, gsh, dx1)
    dx0 = _mla_bwd(dx1, s_mla, rep, q, tabs, grads, gq)
    full = _unprep_grads(gq)
    full['l0_conv_w'] = gq['conv_w']
    return loss[0, 0], dx0, grads, {n: _shards(n, full[n]).astype(BF16) for n in FIRST}


def kernel(x, l0_mix_norm, l0_w_in, l0_conv_w, l0_conv_b, l0_conv_ln_g, l0_conv_ln_b, l0_q_norm, l0_kv_norm, l0_w_uq, l0_w_ukv, l0_w_out, l0_ffn_norm, l0_w_up, l0_ffn_conv_w, l0_ffn_conv_b, l0_w_down, l1_mix_norm, l1_w_in, l1_log_dt, l1_a_re, l1_a_im, l1_b_re, l1_b_im, l1_c_re, l1_c_im, l1_d, l1_w_glu, l1_b_glu, l1_ffn_norm, l1_w_up, l1_ffn_conv_w, l1_ffn_conv_b, l1_w_down, final_norm, loss_target, m_l0_mix_norm, m_l0_w_in, m_l0_conv_w, m_l0_conv_b, m_l0_conv_ln_g, m_l0_conv_ln_b, m_l0_q_norm, m_l0_kv_norm, m_l0_w_uq, m_l0_w_ukv, m_l0_w_out, m_l0_ffn_norm, m_l0_w_up, m_l0_ffn_conv_w, m_l0_ffn_conv_b, m_l0_w_down, m_l1_mix_norm, m_l1_w_in, m_l1_log_dt, m_l1_a_re, m_l1_a_im, m_l1_b_re, m_l1_b_im, m_l1_c_re, m_l1_c_im, m_l1_d, m_l1_w_glu, m_l1_b_glu, m_l1_ffn_norm, m_l1_w_up, m_l1_ffn_conv_w, m_l1_ffn_conv_b, m_l1_w_down, m_final_norm, v_l0_mix_norm, v_l0_w_in, v_l0_conv_w, v_l0_conv_b, v_l0_conv_ln_g, v_l0_conv_ln_b, v_l0_q_norm, v_l0_kv_norm, v_l0_w_uq, v_l0_w_ukv, v_l0_w_out, v_l0_ffn_norm, v_l0_w_up, v_l0_ffn_conv_w, v_l0_ffn_conv_b, v_l0_w_down, v_l1_mix_norm, v_l1_w_in, v_l1_log_dt, v_l1_a_re, v_l1_a_im, v_l1_b_re, v_l1_b_im, v_l1_c_re, v_l1_c_im, v_l1_d, v_l1_w_glu, v_l1_b_glu, v_l1_ffn_norm, v_l1_w_up, v_l1_ffn_conv_w, v_l1_ffn_conv_b, v_l1_w_down, v_final_norm):
    args = dict(locals())
    w = {n: args[n] for n in WEIGHTS}
    m = {n: args['m_' + n] for n in WEIGHTS}
    v = {n: args['v_' + n] for n in WEIGHTS}
    payload = lambda n: w[n] if n in TAPS else w[n].astype(BF16)
    got = dict(zip(FIRST, _exchange([payload(n) for n in FIRST], ['gather'] * len(FIRST), "gather_first")))
    rep = {n: w[n] for n in REPLICATED}
    rest, got['l0_conv_w'] = _exchange_start([payload(n) for n in REST], ['gather'] * len(REST), got['l0_conv_w'],
                                             "gather_rest_start")
    wait_rest = lambda after: dict(zip(REST, _exchange_wait(rest, after, "gather_rest_wait")))
    flights = []

    def send_grads(names, gsh, carry):
        tag = "grads_" + names[0][:2]
        flight, carry = _exchange_start([gsh[n] for n in names], ['scatter'] * len(names), carry, tag + "_start")
        flights.append((names, flight, tag + "_wait"))
        return carry

    loss, dx, grads, gsh = _local_step(x[0], loss_target[0], rep, got, wait_rest, send_grads)

    last = _exchange([gsh[n] for n in FIRST] + [_flat([grads[n] for n in REPLICATED])],
                     ['scatter'] * len(FIRST) + ['gather'], "exchange_grads")
    recv = dict(zip(FIRST, last))
    for names, flight, name in flights:
        recv.update(zip(names, _exchange_wait(flight, last[-1], name)))
    res = [dict(), dict(), dict(), dict()]
    for n in SHARDED:
        for kind, a in enumerate(_adamw(recv[n], w[n], m[n], v[n], "adamw_" + n)):
            res[kind][n] = a
    flatr = lambda d: _flat([d[n] for n in REPLICATED])
    rp_out = _adamw(last[-1], flatr(w), flatr(m), flatr(v), "adamw_replicated")
    for kind in range(4):
        for n, a in zip(REPLICATED, _unflat(rp_out[kind], [w[n].shape for n in REPLICATED])):
            res[kind][n] = a
    total = lax.psum(loss, ("x", "y", "c"))
    return (total, dx[None], *[res[0][n] for n in WEIGHTS], *[res[1][n] for n in WEIGHTS],
            *[res[2][n] for n in WEIGHTS], *[res[3][n] for n in WEIGHTS])
```

```python
import functools

import jax
import jax.numpy as jnp
from jax import lax
from jax.experimental import pallas as pl
from jax.experimental.pallas import tpu as pltpu

F32 = jnp.float32
BF16 = jnp.bfloat16

N_DEV = 8
D_MODEL = 1024
EPS = 1e-6
LN_EPS = 1e-5
CONV_WIDTH = 512
CONV_K = 31
N_HEADS = 8
QK_NOPE = 64
QK_ROPE = 32
V_DIM = 64
HEAD_PAD = 128
Q_RANK = 256
KV_RANK = 128
ROPE_BASE = 10000.0
IN_EVEN = 2 * CONV_WIDTH + Q_RANK + KV_RANK + QK_ROPE
IN_PAD = 1536
KR_LANE = 64
SSM_WIDTH = 512
SSM_GROUP = 16
SSM_GROUPS = 32
SSM_STATE = 64
SSM_CH = SSM_GROUPS * SSM_STATE
D_FF = 2816
FF_SHARD = 2 * D_FF // N_DEV
FF_HALF = N_DEV // 2
FFN_K = 3
TAP_ROWS = 8
ADAM_LR, ADAM_B1, ADAM_B2, ADAM_EPS, ADAM_WD, ADAM_STEP = 0.001, 0.9, 0.999, 1e-08, 0.01, 10

WEIGHTS = ['l0_mix_norm', 'l0_w_in', 'l0_conv_w', 'l0_conv_b', 'l0_conv_ln_g', 'l0_conv_ln_b', 'l0_q_norm',
           'l0_kv_norm', 'l0_w_uq', 'l0_w_ukv', 'l0_w_out', 'l0_ffn_norm', 'l0_w_up', 'l0_ffn_conv_w',
           'l0_ffn_conv_b', 'l0_w_down', 'l1_mix_norm', 'l1_w_in', 'l1_log_dt', 'l1_a_re', 'l1_a_im', 'l1_b_re',
           'l1_b_im', 'l1_c_re', 'l1_c_im', 'l1_d', 'l1_w_glu', 'l1_b_glu', 'l1_ffn_norm', 'l1_w_up',
           'l1_ffn_conv_w', 'l1_ffn_conv_b', 'l1_w_down', 'final_norm']
SHARDED = {'l0_w_in': 1, 'l0_conv_w': 1, 'l0_w_uq': 1, 'l0_w_ukv': 1, 'l0_w_out': 0, 'l0_w_up': 1,
           'l0_ffn_conv_w': 1, 'l0_w_down': 0, 'l1_w_in': 0, 'l1_w_glu': 1, 'l1_w_up': 1, 'l1_ffn_conv_w': 1,
           'l1_w_down': 0}
TAPS = ('l0_conv_w', 'l0_ffn_conv_w', 'l1_ffn_conv_w')
REPLICATED = [n for n in WEIGHTS if n not in SHARDED]


def _tile(n, cands):
    for c in cands:
        if n % c == 0:
            return c
    return n


def _same(g):
    return g


def _mm(a, b, *, ta=False, tb=False, res=None, out_dtype=F32, name, ga=None, gb=None, go=None, groups=1,
        sparse=None):
    a2, b2 = (a.shape[1:] if ga else a.shape), (b.shape[1:] if gb else b.shape)
    m, kd = (a2[1], a2[0]) if ta else a2
    kd2, n = (b2[1], b2[0]) if tb else b2
    assert kd == kd2, (a.shape, b.shape, ta, tb)
    tm = _tile(m, (1024, 512, 256, 128))
    tn = _tile(n, (512, 384, 256, 128))
    tk = _tile(kd, (2048, 1024, 512, 256, 128) if ta else (1024, 512, 256, 128))
    nk = kd // tk
    a_m = b_n = k_of = None
    if sparse is not None:
        (tm, tn, tk), (m, n), nk = sparse['tiles'], sparse['out'], sparse['nk']
        a_m, b_n, k_of = sparse.get('a_m'), sparse.get('b_n'), sparse.get('k_of')
    a_m = a_m or (lambda i, j: i)
    b_n = b_n or (lambda i, j: j)
    k_of = k_of or (lambda i, j, k: k)
    summed = go is None and ga is not None and gb is not None
    assert summed or go is not None or (ga is None and gb is None)
    nkk = nk
    dn = (((0 if ta else 1,), (1 if tb else 0,)), ((), ()))

    def body(*refs):
        if res is None:
            a_ref, b_ref, o_ref, acc_ref = refs
            r_ref = None
        else:
            a_ref, b_ref, r_ref, o_ref, acc_ref = refs
        k = pl.program_id(3)
        if summed:
            p = lax.dot_general(a_ref[0].astype(BF16), b_ref[0].astype(BF16), dn, preferred_element_type=F32)
            for g in range(1, groups):
                p = p + lax.dot_general(a_ref[g].astype(BF16), b_ref[g].astype(BF16), dn, preferred_element_type=F32)
        else:
            p = lax.dot_general(a_ref[...].astype(BF16), b_ref[...].astype(BF16), dn, preferred_element_type=F32)

        @pl.when(k == 0)
        def _():
            acc_ref[...] = p

        @pl.when(k > 0)
        def _():
            acc_ref[...] += p

        @pl.when(k == nkk - 1)
        def _():
            out = acc_ref[...]
            if r_ref is not None:
                out = out + r_ref[...]
            o_ref[...] = out.astype(out_dtype)

    def spec(shape2, idx2, gmap):
        if gmap is None:
            return pl.BlockSpec(shape2, lambda g, i, j, kk: idx2(i, j, kk))
        if summed:
            return pl.BlockSpec((groups,) + shape2, lambda g, i, j, kk: (gmap(0) // groups,) + idx2(i, j, kk))
        return pl.BlockSpec((None,) + shape2, lambda g, i, j, kk: (gmap(g),) + idx2(i, j, kk))

    a_idx = (lambda i, j, k: (k_of(i, j, k), a_m(i, j))) if ta else (lambda i, j, k: (a_m(i, j), k_of(i, j, k)))
    b_idx = (lambda i, j, k: (b_n(i, j), k_of(i, j, k))) if tb else (lambda i, j, k: (k_of(i, j, k), b_n(i, j)))
    a_spec = spec((tk, tm) if ta else (tm, tk), a_idx, ga)
    b_spec = spec((tn, tk) if tb else (tk, tn), b_idx, gb)
    o_spec = spec((tm, tn), lambda i, j, k: (i, j), go)
    in_specs, args = [a_spec, b_spec], [a, b]
    if res is not None:
        in_specs.append(o_spec)
        args.append(res)
    out_shape = (groups, m, n) if go else (m, n)
    return pl.pallas_call(
        body, grid=(groups if go else 1, m // tm, n // tn, nkk), in_specs=in_specs, out_specs=o_spec,
        out_shape=jax.ShapeDtypeStruct(out_shape, out_dtype), scratch_shapes=[pltpu.VMEM((tm, tn), F32)],
        compiler_params=pltpu.CompilerParams(dimension_semantics=("parallel", "parallel", "parallel", "arbitrary")),
        name=name)(*args)


def _rowwise(fn, rows, bcasts, row_outs, red_outs, *, ts, name):
    s = rows[0][0].shape[0]
    nr, nb, nro, nre = len(rows), len(bcasts), len(row_outs), len(red_outs)

    def body(*refs):
        i = pl.program_id(0)
        outs = fn(*[r[...] for r in refs[:nr + nb]])
        if not isinstance(outs, (tuple, list)):
            outs = (outs,)
        o_refs = refs[nr + nb:]
        for q in range(nro):
            o_refs[q][...] = outs[q].astype(o_refs[q].dtype)
        for q in range(nro, nro + nre):
            @pl.when(i == 0)
            def _(q=q):
                o_refs[q][...] = outs[q]

            @pl.when(i > 0)
            def _(q=q):
                o_refs[q][...] += outs[q]

    in_specs = [pl.BlockSpec((ts, w), functools.partial(lambda i, cb: (i, cb), cb=cb)) for (_, w, cb) in rows]
    in_specs += [pl.BlockSpec(b.shape, functools.partial(lambda i, nd: (0,) * nd, nd=b.ndim)) for b in bcasts]
    out_specs = [pl.BlockSpec((ts, w), lambda i: (i, 0)) for (w, _) in row_outs]
    out_specs += [pl.BlockSpec((r, w), lambda i: (0, 0)) for (r, w) in red_outs]
    out_shape = [jax.ShapeDtypeStruct((s, w), dt) for (w, dt) in row_outs]
    out_shape += [jax.ShapeDtypeStruct((r, w), F32) for (r, w) in red_outs]
    return pl.pallas_call(
        body, grid=(s // ts,), in_specs=in_specs, out_specs=out_specs, out_shape=out_shape,
        compiler_params=pltpu.CompilerParams(dimension_semantics=("arbitrary",)), name=name,
    )(*[r[0] for r in rows], *bcasts)


def _rms(x, g):
    return x * lax.rsqrt(jnp.mean(x * x, axis=-1, keepdims=True) + EPS) * g


def _rms_fwd(x, g, name):
    return _rowwise(lambda xv, gv: _rms(xv, gv), [(x, x.shape[1], 0)], [g.reshape(1, -1)],
                    [(x.shape[1], BF16)], [], ts=512, name=name)[0]


def _rms_bwd(x, g, dxn, dres, name):
    d = x.shape[1]

    def fn(xv, dv, rv, gv):
        _, vjp = jax.vjp(_rms, xv, gv)
        dx, dg = vjp(dv.astype(F32))
        return rv + dx, dg

    return _rowwise(fn, [(x, d, 0), (dxn, d, 0), (dres, d, 0)], [g.reshape(1, -1)], [(d, F32)], [(1, d)],
                    ts=512, name=name)


def _cspec(mode, off, rows, width, rowblk, n_rb):
    if mode == 'col':
        return pl.BlockSpec((rows, width), lambda jc, i: (rowblk(i), off + jc))
    return pl.BlockSpec((rows, width), lambda jc, i: ((off + jc) * n_rb + rowblk(i), 0))


TAP_SPREAD = 24


def _stage_shape(k_taps, n, tc):
    return (8, n + TAP_SPREAD, tc) if k_taps > 8 else (1, 8, 128)


def _tap_windows(ref, offsets, n, stage):
    if len(offsets) <= 8:
        return [functools.partial(lambda v: v, ref[pl.ds(o, n), :]) for o in offsets]
    lows = {}
    for o in offsets:
        lows[o % 8] = min(o, lows.get(o % 8, o))
    for r, lo in lows.items():
        span = n + max(o for o in offsets if o % 8 == r) - lo
        stage[r, pl.ds(0, span), :] = ref[pl.ds(lo, span), :]
    return [functools.partial(lambda o: stage[o % 8, pl.ds(o - lows[o % 8], n), :], o) for o in offsets]


def _conv_fwd(xins, w, woffs, params, poffs, pre, post, outs, *, mode, s, k_taps, hb, ts, tc, rc, n_cb, name):
    n_s, nx, ncv, npar, no = s // ts, len(xins), len(woffs), len(params), len(outs)
    rpb = ts // hb
    kp = TAP_ROWS * ((k_taps + TAP_ROWS - 1) // TAP_ROWS)
    pr = 1 if mode == 'col' else TAP_ROWS

    def body(*refs):
        mains, halos = refs[:nx], refs[nx:2 * nx]
        w_refs = refs[2 * nx:2 * nx + ncv]
        p_refs = refs[2 * nx + ncv:2 * nx + ncv + npar]
        o_refs = refs[2 * nx + ncv + npar:2 * nx + ncv + npar + no]
        u_s = refs[2 * nx + ncv + npar + no:2 * nx + ncv + npar + no + ncv]
        win_s = refs[2 * nx + ncv + npar + no + ncv:2 * nx + ncv + npar + no + 2 * ncv]
        stage_s = refs[2 * nx + ncv + npar + no + 2 * ncv:]
        i = pl.program_id(1)
        um = pre(*[r[...].astype(F32) for r in mains])
        uh = pre(*[r[...].astype(F32) for r in halos])
        first = (i > 0).astype(F32)
        for q in range(ncv):
            u_s[q][pl.ds(0, hb), :] = uh[q] * first
            u_s[q][pl.ds(hb, ts), :] = um[q]
        pv = [r[0:1, :] for r in p_refs]

        def chunk(ci, carry):
            r0 = pl.multiple_of(ci * rc, rc)
            cs = []
            for q in range(ncv):
                win_s[q][...] = u_s[q][pl.ds(r0, rc + hb), :]
                acc = jnp.zeros((rc, tc), F32)
                offsets = [hb - (k_taps - 1) + t for t in range(k_taps)]
                for t, ut in enumerate(_tap_windows(win_s[q], offsets, rc, stage_s[q])):
                    acc = acc + w_refs[q][pl.ds(t, 1), :] * ut()
                cs.append(acc)
            res = post(cs, pv)
            for q in range(no):
                o_refs[q][pl.ds(r0, rc), :] = res[q].astype(o_refs[q].dtype)
            return carry

        lax.fori_loop(0, ts // rc, chunk, 0)

    main = lambda i: i
    prev = lambda i: jnp.maximum(i * rpb - 1, 0)
    zero = lambda i: 0
    in_specs = [_cspec(mode, off, ts, tc, main, n_s) for _, off in xins]
    in_specs += [_cspec(mode, off, hb, tc, prev, s // hb) for _, off in xins]
    in_specs += [_cspec(mode, off, kp, tc, zero, 1) for off in woffs]
    in_specs += [_cspec(mode, off, pr, tc, zero, 1) for off in poffs]
    out_specs = [_cspec(mode, 0, ts, tc, main, n_s) for _ in outs]
    oshape = (s, n_cb * tc) if mode == 'col' else (n_cb * s, tc)
    out_shape = [jax.ShapeDtypeStruct(oshape, dt) for dt in outs]
    return pl.pallas_call(
        body, grid=(n_cb, n_s), in_specs=in_specs, out_specs=out_specs, out_shape=out_shape,
        scratch_shapes=[pltpu.VMEM((hb + ts, tc), F32) for _ in range(ncv)]
        + [pltpu.VMEM((hb + rc, tc), F32) for _ in range(ncv)]
        + [pltpu.VMEM(_stage_shape(k_taps, rc, tc), F32) for _ in range(ncv)],
        compiler_params=pltpu.CompilerParams(dimension_semantics=("parallel", "arbitrary")), name=name,
    )(*[a for a, _ in xins], *[a for a, _ in xins], *([w] * ncv), *params)


def _conv_bwd(xins, w, woffs, params, poffs, pre, post, douts, dx_dtype, *, mode, s, k_taps, hb, ts, tc, rc, n_cb, name):
    n_s, nx, ncv, npar, ndo = s // ts, len(xins), len(woffs), len(params), len(douts)
    rpb = ts // hb
    n_hb = s // hb
    kp = TAP_ROWS * ((k_taps + TAP_ROWS - 1) // TAP_ROWS)
    pr = 1 if mode == 'col' else TAP_ROWS

    def body(*refs):
        pos = 0

        def take(n):
            nonlocal pos
            out = refs[pos:pos + n]
            pos += n
            return out

        mains, prevs, nexts = take(nx), take(nx), take(nx)
        d_mains, d_nexts = take(ndo), take(ndo)
        w_refs, p_refs = take(ncv), take(npar)
        dx_refs, dw_refs, dp_refs = take(nx), take(ncv), take(npar)
        u_s, d_s = take(ncv), take(ndo)
        win_s, dcw_s, dwa_s = take(ncv), take(ncv), take(ncv)
        stu_s, std_s = take(ncv), take(ncv)
        i = pl.program_id(1)
        um = pre(*[r[...].astype(F32) for r in mains])
        up = pre(*[r[...].astype(F32) for r in prevs])
        un = pre(*[r[...].astype(F32) for r in nexts])
        first = (i > 0).astype(F32)
        last = (i < n_s - 1).astype(F32)
        for q in range(ncv):
            u_s[q][pl.ds(0, hb), :] = up[q] * first
            u_s[q][pl.ds(hb, ts), :] = um[q]
            u_s[q][pl.ds(hb + ts, hb), :] = un[q]
            dwa_s[q][...] = jnp.zeros((kp * 8, tc), F32)
        for q in range(ndo):
            d_s[q][pl.ds(0, ts), :] = d_mains[q][...].astype(F32)
            d_s[q][pl.ds(ts, hb), :] = d_nexts[q][...].astype(F32) * last
        pv = [r[0:1, :] for r in p_refs]
        fwd_offsets = [hb - (k_taps - 1) + t for t in range(k_taps)]

        def chunk(ci, dpar):
            r0 = pl.multiple_of(ci * rc, rc)
            c_own, c_next, shifted = [], [], []
            for q in range(ncv):
                win_s[q][...] = u_s[q][pl.ds(r0, rc + 2 * hb), :]
                acc = jnp.zeros((rc + hb, tc), F32)
                taps = _tap_windows(win_s[q], fwd_offsets, rc + hb, stu_s[q])
                for t, ut in enumerate(taps):
                    acc = acc + w_refs[q][pl.ds(t, 1), :] * ut()
                c_own.append(acc[:rc])
                c_next.append(acc[rc:])
                shifted.append(taps)
            _, vjp_o = jax.vjp(lambda c, p: tuple(post(c, p)), c_own, pv)
            dc_own, dp_own = vjp_o(tuple(r[pl.ds(r0, rc), :] for r in d_s))
            _, vjp_n = jax.vjp(lambda c: tuple(post(c, pv)), c_next)
            (dc_next,) = vjp_n(tuple(r[pl.ds(r0 + rc, hb), :] for r in d_s))
            dus = []
            for q in range(ncv):
                dcw_s[q][pl.ds(0, rc), :] = dc_own[q]
                dcw_s[q][pl.ds(rc, hb), :] = dc_next[q]
                acc = jnp.zeros((rc, tc), F32)
                bwd_offsets = [k_taps - 1 - t for t in range(k_taps)]
                for t, dct in enumerate(_tap_windows(dcw_s[q], bwd_offsets, rc, std_s[q])):
                    acc = acc + w_refs[q][pl.ds(t, 1), :] * dct()
                    prod = dc_own[q] * shifted[q][t]()[:rc]
                    dwa_s[q][pl.ds(8 * t, 8), :] += jnp.sum(prod.reshape(rc // 8, 8, tc), axis=0)
                dus.append(acc)
            xm = [r[pl.ds(r0, rc), :].astype(F32) for r in mains]
            _, vjp_p = jax.vjp(lambda *xv: tuple(pre(*xv)), *xm)
            dxs = vjp_p(tuple(dus))
            for q in range(nx):
                dx_refs[q][pl.ds(r0, rc), :] = dxs[q].astype(dx_refs[q].dtype)
            return tuple(a + b for a, b in zip(dpar, dp_own))

        dpar = lax.fori_loop(0, ts // rc, chunk, tuple(jnp.zeros((1, tc), F32) for _ in range(npar)))
        for q in range(ncv):
            @pl.when(i == 0)
            def _(q=q):
                dw_refs[q][...] = jnp.zeros((kp, tc), F32)

            for t in range(k_taps):
                dw_refs[q][pl.ds(t, 1), :] += jnp.sum(dwa_s[q][pl.ds(8 * t, 8), :], axis=0, keepdims=True)
        for q in range(npar):
            @pl.when(i == 0)
            def _(q=q):
                dp_refs[q][...] = jnp.zeros((pr, tc), F32)

            dp_refs[q][0:1, :] += dpar[q]

    main = lambda i: i
    prev = lambda i: jnp.maximum(i * rpb - 1, 0)
    nxt = lambda i: jnp.minimum((i + 1) * rpb, n_hb - 1)
    zero = lambda i: 0
    in_specs = [_cspec(mode, off, ts, tc, main, n_s) for _, off in xins]
    in_specs += [_cspec(mode, off, hb, tc, prev, n_hb) for _, off in xins]
    in_specs += [_cspec(mode, off, hb, tc, nxt, n_hb) for _, off in xins]
    in_specs += [_cspec(mode, off, ts, tc, main, n_s) for _, off in douts]
    in_specs += [_cspec(mode, off, hb, tc, nxt, n_hb) for _, off in douts]
    in_specs += [_cspec(mode, off, kp, tc, zero, 1) for off in woffs]
    in_specs += [_cspec(mode, off, pr, tc, zero, 1) for off in poffs]
    out_specs = [_cspec(mode, 0, ts, tc, main, n_s) for _ in xins]
    out_specs += [_cspec(mode, 0, kp, tc, zero, 1) for _ in woffs]
    out_specs += [_cspec(mode, 0, pr, tc, zero, 1) for _ in params]

    def shape(rows):
        return (rows, n_cb * tc) if mode == 'col' else (n_cb * rows, tc)

    out_shape = [jax.ShapeDtypeStruct(shape(s), dx_dtype) for _ in xins]
    out_shape += [jax.ShapeDtypeStruct(shape(kp), F32) for _ in woffs]
    out_shape += [jax.ShapeDtypeStruct(shape(pr), F32) for _ in params]
    xa = [a for a, _ in xins]
    da = [a for a, _ in douts]
    return pl.pallas_call(
        body, grid=(n_cb, n_s), in_specs=in_specs, out_specs=out_specs, out_shape=out_shape,
        scratch_shapes=[pltpu.VMEM((hb + ts + hb, tc), F32) for _ in range(ncv)]
        + [pltpu.VMEM((ts + hb, tc), F32) for _ in range(ndo)]
        + [pltpu.VMEM((rc + 2 * hb, tc), F32) for _ in range(ncv)]
        + [pltpu.VMEM((rc + hb, tc), F32) for _ in range(ncv)]
        + [pltpu.VMEM((kp * 8, tc), F32) for _ in range(ncv)]
        + [pltpu.VMEM(_stage_shape(k_taps, rc + hb, tc), F32) for _ in range(ncv)]
        + [pltpu.VMEM(_stage_shape(k_taps, rc, tc), F32) for _ in range(ncv)],
        compiler_params=pltpu.CompilerParams(dimension_semantics=("parallel", "arbitrary")), name=name,
    )(*xa, *xa, *xa, *da, *da, *([w] * ncv), *params)


def _glu_pre(a, g):
    return [a * jax.nn.sigmoid(g)]


def _ln_silu_post(cs, ps):
    c = cs[0] + ps[0]
    mu = jnp.mean(c, axis=-1, keepdims=True)
    var = jnp.mean(jnp.square(c - mu), axis=-1, keepdims=True)
    y = (c - mu) * lax.rsqrt(var + LN_EPS) * ps[1] + ps[2]
    return [jax.nn.silu(y)]


def _pair_pre(a, b):
    return [a, b]


def _gate_post(cs, ps):
    return [jax.nn.silu(cs[0] + ps[0]) * (cs[1] + ps[1])]


def _conva(s):
    return dict(mode='col', s=s, k_taps=CONV_K, hb=32, ts=512, tc=CONV_WIDTH, rc=64, n_cb=1)


def _gate(s):
    return dict(mode='row', s=s, k_taps=FFN_K, hb=8, ts=512, tc=FF_SHARD, rc=32, n_cb=FF_HALF)


def _rope_tables(s):
    half = QK_ROPE // 2
    inv = ROPE_BASE ** (-jnp.arange(half, dtype=F32) / half)
    ang = jnp.arange(s, dtype=F32)[:, None] * inv[None, :]
    cos, sin = jnp.cos(ang), jnp.sin(ang)
    z = lambda n: jnp.zeros((s, n), F32)
    c = jnp.concatenate([jnp.ones((s, QK_NOPE), F32), cos, cos, z(HEAD_PAD - QK_NOPE - QK_ROPE)], axis=1)
    s1 = jnp.concatenate([z(QK_NOPE), -sin, z(HEAD_PAD - QK_NOPE - half)], axis=1)
    s2 = jnp.concatenate([z(QK_NOPE + half), sin, z(HEAD_PAD - QK_NOPE - QK_ROPE)], axis=1)
    return c, s1, s2


def _rot(t, c, s1, s2):
    half = QK_ROPE // 2
    return t * c + pltpu.roll(t, HEAD_PAD - half, 1) * s1 + pltpu.roll(t, half, 1) * s2


def _rot_t(d, c, s1, s2):
    half = QK_ROPE // 2
    return d * c + pltpu.roll(d * s1, half, 1) + pltpu.roll(d * s2, HEAD_PAD - half, 1)


def _heads(v):
    return [v[:, h * HEAD_PAD:(h + 1) * HEAD_PAD] for h in range(N_HEADS)]


def _rope_fwd(qf, kvf, hmat, tabs, name):
    w = N_HEADS * HEAD_PAD

    def fn(q, k, v, kr, c, s1, s2):
        krr = _rot(kr, c, s1, s2)
        qo = jnp.concatenate([_rot(t, c, s1, s2) for t in _heads(q)], axis=1)
        ko = jnp.concatenate([t + krr for t in _heads(k)], axis=1)
        lane = lax.broadcasted_iota(jnp.int32, v.shape, 1) & (HEAD_PAD - 1)
        return qo, ko, jnp.where(lane == ONES_LANE, 1.0, v)

    rows = [(qf, w, 0), (kvf, w, 0), (kvf, w, 1), (hmat, HEAD_PAD, IN_PAD // HEAD_PAD - 1)]
    rows += [(t, HEAD_PAD, 0) for t in tabs]
    return _rowwise(fn, rows, [], [(w, BF16)] * 3, [], ts=512, name=name)


def _rope_bwd(dq, dk, dv, tabs, name):
    w = N_HEADS * HEAD_PAD

    def fn(dqv, dkv, dvv, c, s1, s2):
        dqo = jnp.concatenate([_rot_t(t, c, s1, s2) for t in _heads(dqv)], axis=1)
        ksum = functools.reduce(lambda a, b: a + b, _heads(dkv))
        return dqo, jnp.concatenate([dkv, dvv], axis=1), _rot_t(ksum, c, s1, s2)

    rows = [(dq, w, 0), (dk, w, 0), (dv, w, 0)] + [(t, HEAD_PAD, 0) for t in tabs]
    return _rowwise(fn, rows, [], [(w, BF16), (2 * w, BF16), (HEAD_PAD, F32)], [], ts=512, name=name)


ATT_Q = 1024
ATT_SUB = 512
ATT_KV = 1024
ATT_SCALE = (QK_NOPE + QK_ROPE) ** -0.5
LOG2E = 1.4426950408889634
ATT_C2 = ATT_SCALE * LOG2E
ONES_LANE = V_DIM


def _nt(a, b):
    return lax.dot_general(a, b, (((1,), (1,)), ((), ())), preferred_element_type=F32)


def _lanes(x, w):
    return x if w == HEAD_PAD else jnp.tile(x, (1, w // HEAD_PAD))


def _tri(w, transposed):
    r = lax.broadcasted_iota(jnp.int32, (w, w), 0)
    c = lax.broadcasted_iota(jnp.int32, (w, w), 1)
    return (r <= c) if transposed else (c <= r)


def _attn_fwd(q, k, v, name):
    s = q.shape[0]
    tq, kvc = min(ATT_Q, s), min(ATT_KV, s)
    nsub, per = tq // ATT_SUB, tq // kvc

    def body(q_ref, k_ref, v_ref, o_ref, lse_ref, m_s, acc_s):
        i = pl.program_id(1)
        m_s[...] = jnp.full((tq, HEAD_PAD), -jnp.inf, F32)
        acc_s[...] = jnp.zeros((tq, HEAD_PAD), F32)

        def update(r0, n, kb, vb, diag):
            rows = pl.ds(r0, n)
            w = kb.shape[0]
            sc = _nt(q_ref[rows, :], kb)
            if diag:
                sc = jnp.where(_tri(w, False), sc, -jnp.inf)
            m_prev = m_s[rows, :]
            m_next = jnp.maximum(m_prev, jnp.max(sc, axis=1, keepdims=True))
            p = jnp.exp2((sc - _lanes(m_next, w)) * ATT_C2)
            alpha = jnp.exp2((m_prev - m_next) * ATT_C2)
            acc_s[rows, :] = alpha * acc_s[rows, :] + jnp.dot(p.astype(BF16), vb, preferred_element_type=F32)
            m_s[rows, :] = m_next

        def below(j, carry):
            at = pl.ds(pl.multiple_of(j * kvc, kvc), kvc)
            update(0, tq, k_ref[at, :], v_ref[at, :], False)
            return carry

        lax.fori_loop(0, i * per, below, 0)
        for r in range(nsub):
            for c in range(r + 1):
                at = pl.ds(pl.multiple_of(i * tq + c * ATT_SUB, ATT_SUB), ATT_SUB)
                update(r * ATT_SUB, ATT_SUB, k_ref[at, :], v_ref[at, :], c == r)
        l = acc_s[:, ONES_LANE:ONES_LANE + 1]
        o_ref[...] = (acc_s[...] / l).astype(BF16)
        lse_ref[...] = m_s[...] * ATT_SCALE + jnp.log(l)

    q_spec = pl.BlockSpec((tq, HEAD_PAD), lambda h, i: (i, h))
    kv_spec = pl.BlockSpec((s, HEAD_PAD), lambda h, i: (0, h))
    return pl.pallas_call(
        body, grid=(N_HEADS, s // tq), in_specs=[q_spec, kv_spec, kv_spec], out_specs=[q_spec, q_spec],
        out_shape=[jax.ShapeDtypeStruct(q.shape, BF16), jax.ShapeDtypeStruct(q.shape, F32)],
        scratch_shapes=[pltpu.VMEM((tq, HEAD_PAD), F32)] * 2,
        compiler_params=pltpu.CompilerParams(dimension_semantics=("parallel", "arbitrary")), name=name,
    )(q, k, v)


def _attn_rows(cat, dcat, lse, name):
    s = lse.shape[0]
    tq, kvc = min(ATT_Q, s), min(ATT_KV, s)
    per = tq // kvc
    ob = CONV_WIDTH // HEAD_PAD

    def body(o_ref, do_ref, lse_ref, lser_ref, dltr_ref):
        dl = jnp.broadcast_to(jnp.sum(do_ref[...] * o_ref[...].astype(F32), axis=1, keepdims=True), (tq, HEAD_PAD))
        l2 = lse_ref[...] * LOG2E
        for c in range(per):
            lser_ref[c] = jnp.transpose(l2[c * kvc:(c + 1) * kvc])[0:8, :]
            dltr_ref[c] = jnp.transpose(dl[c * kvc:(c + 1) * kvc])[0:8, :]

    q_spec = pl.BlockSpec((tq, HEAD_PAD), lambda h, i: (i, h))
    o_spec = pl.BlockSpec((tq, HEAD_PAD), lambda h, i: (i, ob + h))
    row_spec = pl.BlockSpec((None, per, 8, kvc), lambda h, i: (h, i, 0, 0))
    rows = jax.ShapeDtypeStruct((N_HEADS, s // kvc, 8, kvc), F32)
    return pl.pallas_call(
        body, grid=(N_HEADS, s // tq), in_specs=[o_spec, o_spec, q_spec], out_specs=[row_spec, row_spec],
        out_shape=[rows, rows],
        compiler_params=pltpu.CompilerParams(dimension_semantics=("parallel", "parallel")), name=name,
    )(cat, dcat, lse)


def _attn_bwd(q, k, v, dcat, lse_r, dlt_r, name):
    s = q.shape[0]
    tk, kvc = min(ATT_Q, s), min(ATT_KV, s)
    nsub, per, n_chunks = tk // ATT_SUB, tk // kvc, s // kvc
    n_j = s // tk
    ob = CONV_WIDTH // HEAD_PAD

    def body(k_ref, v_ref, q_ref, do_ref, lse_ref, dl_ref, dk_ref, dv_ref, dq_ref, dk_s, dv_s):
        j = pl.program_id(1)
        dk_s[...] = jnp.zeros((tk, HEAD_PAD), F32)
        dv_s[...] = jnp.zeros((tk, HEAD_PAD), F32)

        @pl.when(j == 0)
        def _():
            dq_ref[...] = jnp.zeros((s, HEAD_PAD), F32)

        def update(r0, n, at, lrow, drow, diag):
            rows = pl.ds(r0, n)
            qb, dob = q_ref[at, :], do_ref[at, :].astype(BF16)
            sc = _nt(k_ref[rows, :], qb)
            if diag:
                sc = jnp.where(_tri(qb.shape[0], True), sc, -jnp.inf)
            p = jnp.exp2(sc * ATT_C2 - lrow)
            dp = _nt(v_ref[rows, :], dob)
            ds = (p * (dp - drow)).astype(BF16)
            dv_s[rows, :] += jnp.dot(p.astype(BF16), dob, preferred_element_type=F32)
            dk_s[rows, :] += jnp.dot(ds, qb, preferred_element_type=F32)
            dq_ref[at, :] += lax.dot_general(ds, k_ref[rows, :], (((0,), (0,)), ((), ())), preferred_element_type=F32)

        def above(ic, carry):
            at = pl.ds(pl.multiple_of(ic * kvc, kvc), kvc)
            update(0, tk, at, lse_ref[ic, 0:1, :], dl_ref[ic, 0:1, :], False)
            return carry

        lax.fori_loop((j + 1) * per, n_chunks, above, 0)
        for r in range(nsub):
            for c in range(r, nsub):
                at = pl.ds(pl.multiple_of(j * tk + c * ATT_SUB, ATT_SUB), ATT_SUB)
                ic = j * per + (c * ATT_SUB) // kvc
                lo = (c * ATT_SUB) % kvc
                update(r * ATT_SUB, ATT_SUB, at, lse_ref[ic, 0:1, lo:lo + ATT_SUB], dl_ref[ic, 0:1, lo:lo + ATT_SUB],
                       c == r)
        dk_ref[...] = dk_s[...] * ATT_SCALE
        dv_ref[...] = dv_s[...]

        @pl.when(j == n_j - 1)
        def _():
            dq_ref[...] = dq_ref[...] * ATT_SCALE

    kv_spec = pl.BlockSpec((tk, HEAD_PAD), lambda h, j: (j, h))
    q_spec = pl.BlockSpec((s, HEAD_PAD), lambda h, j: (0, h))
    do_spec = pl.BlockSpec((s, HEAD_PAD), lambda h, j: (0, ob + h))
    row_spec = pl.BlockSpec((None, n_chunks, 8, kvc), lambda h, j: (h, 0, 0, 0))
    full = jax.ShapeDtypeStruct(q.shape, F32)
    dk, dv, dq = pl.pallas_call(
        body, grid=(N_HEADS, n_j), in_specs=[kv_spec, kv_spec, q_spec, do_spec, row_spec, row_spec],
        out_specs=[kv_spec, kv_spec, q_spec], out_shape=[full, full, full],
        scratch_shapes=[pltpu.VMEM((tk, HEAD_PAD), F32)] * 2,
        compiler_params=pltpu.CompilerParams(dimension_semantics=("parallel", "arbitrary")), name=name,
    )(k, v, q, dcat, lse_r, dlt_r)
    return dq, dk, dv


SCAN_T = 256
SCAN_C = 512


def _scan(b, lam, *, reverse, xs=None, name):
    s = b.shape[0]
    t = min(SCAN_T, s)
    n_t, n_c = s // t, SSM_CH // SCAN_C
    with_dlam = xs is not None

    def shift(a, d, row):
        if d >= 8:
            z = jnp.zeros((d, SCAN_C), F32)
            return jnp.concatenate([a[d:], z], axis=0) if reverse else jnp.concatenate([z, a[:t - d]], axis=0)
        if reverse:
            return jnp.where(row < t - d, pltpu.roll(a, t - d, 0), 0.0)
        return jnp.where(row >= d, pltpu.roll(a, d, 0), 0.0)

    def body(*refs):
        if with_dlam:
            b_ref, lam_ref, x_ref, o_ref, dl_ref, c_s = refs
        else:
            b_ref, lam_ref, o_ref, c_s = refs
        k = pl.program_id(0)

        @pl.when(k == 0)
        def _():
            c_s[...] = jnp.zeros((1, 2 * SSM_CH), F32)
            if with_dlam:
                dl_ref[...] = jnp.zeros((1, 2 * SSM_CH), F32)

        row = lax.broadcasted_iota(jnp.int32, (t, SCAN_C), 0)
        edge = (row == t - 1) if reverse else (row == 0)
        for ch in range(n_c):
            re = pl.ds(ch * SCAN_C, SCAN_C)
            im = pl.ds(SSM_CH + ch * SCAN_C, SCAN_C)
            lr = lam_ref[:, re]
            li = -lam_ref[:, im] if reverse else lam_ref[:, im]
            cr, ci = c_s[:, re], c_s[:, im]
            ar = b_ref[:, re] + jnp.where(edge, lr * cr - li * ci, 0.0)
            ai = b_ref[:, im] + jnp.where(edge, lr * ci + li * cr, 0.0)
            d = 1
            while d < t:
                sr, si = shift(ar, d, row), shift(ai, d, row)
                ar, ai = ar + lr * sr - li * si, ai + lr * si + li * sr
                lr, li = lr * lr - li * li, 2.0 * lr * li
                d *= 2
            o_ref[:, re] = ar.astype(o_ref.dtype)
            o_ref[:, im] = ai.astype(o_ref.dtype)
            if with_dlam:
                gr = jnp.where(edge, cr, shift(ar, 1, row))
                gi = jnp.where(edge, ci, shift(ai, 1, row))
                xr, xi = x_ref[:, re].astype(F32), x_ref[:, im].astype(F32)
                dl_ref[:, re] += jnp.sum(xr * gr + xi * gi, axis=0, keepdims=True)
                dl_ref[:, im] += jnp.sum(xr * gi - xi * gr, axis=0, keepdims=True)
            last = 0 if reverse else t - 1
            c_s[:, re] = ar[last:last + 1, :]
            c_s[:, im] = ai[last:last + 1, :]

    tm = (lambda k: (n_t - 1 - k, 0)) if reverse else (lambda k: (k, 0))
    blk = pl.BlockSpec((t, 2 * SSM_CH), tm)
    vec = pl.BlockSpec((1, 2 * SSM_CH), lambda k: (0, 0))
    in_specs, args = [blk, vec], [b, lam]
    out_specs, out_shape = [blk], [jax.ShapeDtypeStruct((s, 2 * SSM_CH), BF16)]
    if with_dlam:
        in_specs.append(blk)
        args.append(xs)
        out_specs.append(vec)
        out_shape.append(jax.ShapeDtypeStruct((1, 2 * SSM_CH), F32))
    return pl.pallas_call(
        body, grid=(n_t,), in_specs=in_specs, out_specs=out_specs, out_shape=out_shape,
        scratch_shapes=[pltpu.VMEM((1, 2 * SSM_CH), F32)],
        compiler_params=pltpu.CompilerParams(dimension_semantics=("arbitrary",)), name=name,
    )(*args)


def _s5_disc(log_dt, a_re, a_im, b_re, b_im):
    dt = jnp.exp(log_dt)[:, None]
    mag = jnp.exp(a_re * dt)
    lb_re, lb_im = mag * jnp.cos(a_im * dt), mag * jnp.sin(a_im * dt)
    den = a_re * a_re + a_im * a_im
    nr, ni = lb_re - 1.0, lb_im
    f_re = (nr * a_re + ni * a_im) / den
    f_im = (ni * a_re - nr * a_im) / den
    bb_re = f_re[..., None] * b_re - f_im[..., None] * b_im
    bb_im = f_re[..., None] * b_im + f_im[..., None] * b_re
    return lb_re, lb_im, bb_re, bb_im


def _bd(a):
    g, i, j = a.shape
    eye = jnp.eye(g, dtype=a.dtype)
    return (a[:, :, None, :] * eye[:, None, :, None]).reshape(g * i, g * j)


S5_TILE_GROUPS = HEAD_PAD // SSM_GROUP


def _s5_sparse(s):
    nb = SSM_GROUPS // S5_TILE_GROUPS
    cw, sw = S5_TILE_GROUPS * SSM_GROUP, S5_TILE_GROUPS * SSM_STATE
    tm = min(1024, s)
    return dict(
        expand=dict(tiles=(tm, sw, cw), out=(s, 2 * SSM_CH), nk=1, k_of=lambda i, j, k: j % nb),
        reduce=dict(tiles=(tm, cw, sw), out=(s, SSM_WIDTH), nk=2, k_of=lambda i, j, k: j + nb * k),
        wide_t=dict(tiles=(sw, cw, tm), out=(2 * SSM_CH, cw), nk=s // tm, b_n=lambda i, j: i % nb),
        narrow_t=dict(tiles=(cw, sw, tm), out=(cw, 2 * SSM_CH), nk=s // tm, a_m=lambda i, j: j % nb))


def _exchange(arrs, modes, name):
    n = len(arrs)
    shapes = [a.shape if md == 'scatter' else (N_DEV,) + a.shape for a, md in zip(arrs, modes)]

    def body(*refs):
        srcs, outs = refs[:n], refs[n:2 * n]
        send_sems, recv_sems = refs[2 * n:]
        x, y, c = lax.axis_index("x"), lax.axis_index("y"), lax.axis_index("c")
        me = 4 * x + 2 * y + c
        copies = []
        for r, pos, peer in _peers(x, y, c):
            for q in range(n):
                copies.append(pltpu.make_async_remote_copy(
                    src_ref=srcs[q].at[peer] if modes[q] == 'scatter' else srcs[q], dst_ref=outs[q].at[me],
                    send_sem=send_sems.at[(r - 1) * n + q], recv_sem=recv_sems.at[(r - 1) * n + q], device_id=pos,
                    device_id_type=pl.DeviceIdType.MESH))
        for cp in copies:
            cp.start()
        for cp in copies:
            cp.wait_recv()
        for cp in copies:
            cp.wait_send()

    any_spec = pl.BlockSpec(memory_space=pl.ANY)
    outs = pl.pallas_call(
        body, out_shape=[jax.ShapeDtypeStruct(sh, a.dtype) for sh, a in zip(shapes, arrs)],
        in_specs=[any_spec] * n, out_specs=[any_spec] * n,
        scratch_shapes=[pltpu.SemaphoreType.DMA(((N_DEV - 1) * n,)), pltpu.SemaphoreType.DMA(((N_DEV - 1) * n,))],
        compiler_params=pltpu.CompilerParams(has_side_effects=True), name=name,
    )(*arrs)
    return _own_slots(outs, arrs, modes)


def _peers(x, y, c):
    out = []
    for r in range(1, N_DEV):
        px, py, pc = x ^ (r >> 2), y ^ ((r >> 1) & 1), c ^ (r & 1)
        out.append((r, (px, py, pc), 4 * px + 2 * py + pc))
    return out


def _own_slots(lands, srcs, modes):
    me = 4 * lax.axis_index("x") + 2 * lax.axis_index("y") + lax.axis_index("c")
    out = []
    for land, src, md in zip(lands, srcs, modes):
        own = lax.dynamic_index_in_dim(src, me, 0, keepdims=False) if md == 'scatter' else src
        out.append(lax.dynamic_update_index_in_dim(land, own, me, 0))
    return out


def _exchange_start(arrs, modes, carry, name):
    n = len(arrs)
    shapes = [a.shape if md == 'scatter' else (N_DEV,) + a.shape for a, md in zip(arrs, modes)]
    lands = [lax.empty(sh, a.dtype) for sh, a in zip(shapes, arrs)]

    def body(*refs):
        srcs, zones = refs[:n], refs[n:2 * n]
        send_sems, recv_sems = refs[2 * n + 1], refs[2 * n + 2]
        x, y, c = lax.axis_index("x"), lax.axis_index("y"), lax.axis_index("c")
        me = 4 * x + 2 * y + c
        for r, pos, peer in _peers(x, y, c):
            for q in range(n):
                src = srcs[q].at[peer] if modes[q] == 'scatter' else srcs[q]
                pltpu.make_async_remote_copy(
                    src_ref=src, dst_ref=zones[q].at[me], send_sem=send_sems.at[(r - 1) * n + q],
                    recv_sem=recv_sems.at[(r - 1) * n + q], device_id=pos, device_id_type=pl.DeviceIdType.MESH).start()

    hbm = pl.BlockSpec(memory_space=pltpu.HBM)
    sem = pl.BlockSpec(memory_space=pltpu.SEMAPHORE)
    thru = arrs + lands + [carry]
    sems = pltpu.SemaphoreType.DMA(((N_DEV - 1) * n,))
    outs = pl.pallas_call(
        body, name=name, out_shape=(sems, sems, *[pltpu.HBM(a.shape, a.dtype) for a in thru]),
        in_specs=[hbm] * len(thru), out_specs=(sem, sem, *[hbm] * len(thru)),
        input_output_aliases={q: 2 + q for q in range(len(thru))},
        compiler_params=pltpu.CompilerParams(has_side_effects=pltpu.SideEffectType.DATAFLOW_SIDE_EFFECTING),
    )(*[pltpu.with_memory_space_constraint(a, pltpu.HBM) for a in thru])
    return dict(send=outs[0], recv=outs[1], srcs=list(outs[2:2 + n]), lands=list(outs[2 + n:2 + 2 * n]),
                modes=modes), outs[-1]


def _exchange_wait(flight, after, name):
    n = len(flight['srcs'])
    modes = flight['modes']

    def body(*refs):
        srcs, zones = refs[:n], refs[n:2 * n]
        send_sems, recv_sems = refs[2 * n], refs[2 * n + 1]
        x, y, c = lax.axis_index("x"), lax.axis_index("y"), lax.axis_index("c")
        me = 4 * x + 2 * y + c
        for r, pos, peer in _peers(x, y, c):
            for q in range(n):
                src = srcs[q].at[peer] if modes[q] == 'scatter' else srcs[q]
                cp = pltpu.make_async_remote_copy(
                    src_ref=src, dst_ref=zones[q].at[me], send_sem=send_sems.at[(r - 1) * n + q],
                    recv_sem=recv_sems.at[(r - 1) * n + q], device_id=pos, device_id_type=pl.DeviceIdType.MESH)
                cp.wait_send()
                cp.wait_recv()

    hbm = pl.BlockSpec(memory_space=pltpu.HBM)
    sem = pl.BlockSpec(memory_space=pltpu.SEMAPHORE)
    bufs = flight['srcs'] + flight['lands']
    outs = pl.pallas_call(
        body, name=name, out_shape=tuple(pltpu.HBM(a.shape, a.dtype) for a in bufs),
        in_specs=[hbm] * (2 * n) + [sem, sem, pl.BlockSpec(memory_space=pl.ANY)], out_specs=tuple([hbm] * (2 * n)),
        input_output_aliases={q: q for q in range(2 * n)},
        compiler_params=pltpu.CompilerParams(has_side_effects=pltpu.SideEffectType.DATAFLOW_SIDE_EFFECTING),
    )(*bufs, flight['send'], flight['recv'], after)
    return _own_slots(outs[n:], outs[:n], modes)


def _adamw(parts, w, m, v, name):
    r, c = w.shape
    tr = _tile(r, (256, 128))

    def body(p_ref, w_ref, m_ref, v_ref, g_ref, d_ref, nm_ref, nv_ref):
        g = p_ref[0].astype(F32)
        for d in range(1, N_DEV):
            g = g + p_ref[d].astype(F32)
        m2 = ADAM_B1 * m_ref[...] + (1.0 - ADAM_B1) * g
        v2 = ADAM_B2 * v_ref[...] + (1.0 - ADAM_B2) * jnp.square(g)
        m_hat = m2 / (1.0 - ADAM_B1 ** ADAM_STEP)
        v_hat = v2 / (1.0 - ADAM_B2 ** ADAM_STEP)
        g_ref[...] = g
        d_ref[...] = -ADAM_LR * (m_hat / (jnp.sqrt(v_hat) + ADAM_EPS) + ADAM_WD * w_ref[...])
        nm_ref[...] = m2
        nv_ref[...] = v2

    spec = pl.BlockSpec((tr, c), lambda i: (i, 0))
    return pl.pallas_call(
        body, grid=(r // tr,), in_specs=[pl.BlockSpec((N_DEV, tr, c), lambda i: (0, i, 0)), spec, spec, spec],
        out_specs=[spec] * 4, out_shape=[jax.ShapeDtypeStruct((r, c), F32)] * 4,
        compiler_params=pltpu.CompilerParams(dimension_semantics=("parallel",)), name=name,
    )(parts, w, m, v)


FLAT_W = 512
FLAT_ROWS = 256


def _flat(arrs):
    v = jnp.concatenate([a.reshape(-1) for a in arrs])
    return jnp.pad(v, (0, (-v.shape[0]) % (FLAT_ROWS * FLAT_W))).reshape(-1, FLAT_W)


def _unflat(flat, shapes):
    v = flat.reshape(-1)
    out, off = [], 0
    for sh in shapes:
        n = 1
        for d in sh:
            n *= d
        out.append(v[off:off + n].reshape(sh))
        off += n
    return out


def _full(name, stacked):
    if SHARDED[name] == 0:
        return stacked.reshape((-1,) + stacked.shape[2:])
    return jnp.transpose(stacked, (1, 0, 2)).reshape(stacked.shape[1], -1)


def _shards(name, full):
    if SHARDED[name] == 0:
        return full.reshape((N_DEV, -1) + full.shape[1:])
    r, c = full.shape
    return jnp.transpose(full.reshape(r, N_DEV, c // N_DEV), (1, 0, 2))


def _prep_weights(p):
    q = {}
    w_in = p['l0_w_in']
    z = lambda n: jnp.zeros((D_MODEL, n), w_in.dtype)
    q['w_in'] = jnp.concatenate([w_in[:, :IN_EVEN - QK_ROPE], z(KR_LANE), w_in[:, IN_EVEN - QK_ROPE:],
                                 z(HEAD_PAD - KR_LANE - QK_ROPE)], axis=1)
    dqk = QK_NOPE + QK_ROPE
    q['w_uq'] = jnp.pad(p['l0_w_uq'].reshape(Q_RANK, N_HEADS, dqk), ((0, 0), (0, 0), (0, HEAD_PAD - dqk))
                        ).reshape(Q_RANK, N_HEADS * HEAD_PAD)
    ukv = p['l0_w_ukv'].reshape(KV_RANK, N_HEADS, 2, QK_NOPE)
    padh = lambda a: jnp.pad(a, ((0, 0), (0, 0), (0, HEAD_PAD - QK_NOPE))).reshape(KV_RANK, N_HEADS * HEAD_PAD)
    q['w_ukv'] = jnp.concatenate([padh(ukv[:, :, 0]), padh(ukv[:, :, 1])], axis=1)
    wo = p['l0_w_out']
    wo_a = jnp.pad(wo[CONV_WIDTH:].reshape(N_HEADS, V_DIM, D_MODEL), ((0, 0), (0, HEAD_PAD - V_DIM), (0, 0)))
    q['w_out'] = jnp.concatenate([wo[:CONV_WIDTH], wo_a.reshape(N_HEADS * HEAD_PAD, D_MODEL)], axis=0)
    return q


def _unprep_grads(g):
    out = {}
    d = g['w_in']
    out['l0_w_in'] = jnp.concatenate([d[:, :IN_EVEN - QK_ROPE],
                                      d[:, IN_EVEN - QK_ROPE + KR_LANE:IN_EVEN + KR_LANE]], axis=1)
    dqk = QK_NOPE + QK_ROPE
    out['l0_w_uq'] = g['w_uq'].reshape(Q_RANK, N_HEADS, HEAD_PAD)[:, :, :dqk].reshape(Q_RANK, N_HEADS * dqk)
    d = g['w_ukv'].reshape(KV_RANK, 2, N_HEADS, HEAD_PAD)[:, :, :, :QK_NOPE]
    out['l0_w_ukv'] = jnp.transpose(d, (0, 2, 1, 3)).reshape(KV_RANK, N_HEADS * 2 * QK_NOPE)
    d = g['w_out']
    da = d[CONV_WIDTH:].reshape(N_HEADS, HEAD_PAD, D_MODEL)[:, :V_DIM].reshape(N_HEADS * V_DIM, D_MODEL)
    out['l0_w_out'] = jnp.concatenate([d[:CONV_WIDTH], da], axis=0)
    return out


def _pad_rows(w, rows):
    return jnp.pad(w, [(0, 0)] * (w.ndim - 2) + [(0, rows - w.shape[-2]), (0, 0)])


def _ffn_fwd(x, rep, got, pre, tag):
    s = x.shape[0]
    xn = _rms_fwd(x, rep[pre + 'ffn_norm'], f"{tag}_ffn_norm")
    w_up = got[pre + 'w_up']
    hu = _mm(xn, w_up, gb=_same, go=_same, groups=N_DEV, name=f"{tag}_ffn_up").reshape(N_DEV * s, FF_SHARD)
    taps = _pad_rows(got[pre + 'ffn_conv_w'], TAP_ROWS).reshape(N_DEV * TAP_ROWS, FF_SHARD)
    bias = _pad_rows(rep[pre + 'ffn_conv_b'].reshape(N_DEV, 1, FF_SHARD), TAP_ROWS).reshape(N_DEV * TAP_ROWS, FF_SHARD)
    (act,) = _conv_fwd([(hu, 0), (hu, FF_HALF)], taps, [0, FF_HALF], [bias, bias], [0, FF_HALF], _pair_pre,
                       _gate_post, [BF16], name=f"{tag}_ffn_gate", **_gate(s))
    act = act.reshape(FF_HALF, s, FF_SHARD)
    w_down = got[pre + 'w_down'].reshape(FF_HALF, FF_SHARD, D_MODEL)
    y = _mm(act, w_down, ga=_same, gb=_same, groups=FF_HALF, res=x, name=f"{tag}_ffn_down")
    return y, (x, xn, hu, act, taps, bias, w_up, w_down)


def _ffn_bwd(dy, saved, rep, pre, tag, grads, gsh):
    x, xn, hu, act, taps, bias, w_up, w_down = saved
    s = x.shape[0]
    dact = _mm(dy, w_down, tb=True, gb=_same, go=_same, groups=FF_HALF, name=f"{tag}_ffn_down_dx")
    gsh[pre + 'w_down'] = _mm(act, dy, ta=True, ga=_same, go=_same, groups=FF_HALF, out_dtype=BF16,
                              name=f"{tag}_ffn_down_dw").reshape(N_DEV, FF_SHARD // 2, D_MODEL)
    dha, dhb, dwa, dwb, dba, dbb = _conv_bwd(
        [(hu, 0), (hu, FF_HALF)], taps, [0, FF_HALF], [bias, bias], [0, FF_HALF], _pair_pre, _gate_post,
        [(dact.reshape(FF_HALF * s, FF_SHARD), 0)], BF16, name=f"{tag}_ffn_gate_bwd", **_gate(s))
    dha, dhb = dha.reshape(FF_HALF, s, FF_SHARD), dhb.reshape(FF_HALF, s, FF_SHARD)
    dtaps = jnp.concatenate([dwa, dwb], axis=0).reshape(N_DEV, TAP_ROWS, FF_SHARD)
    gsh[pre + 'ffn_conv_w'] = dtaps[:, :FFN_K].astype(BF16)
    grads[pre + 'ffn_conv_b'] = jnp.concatenate([dba, dbb], axis=0).reshape(N_DEV, TAP_ROWS, FF_SHARD)[:, 0].reshape(-1)
    upper = lambda g: g + FF_HALF
    dxn = _mm(dha, w_up, tb=True, ga=_same, gb=_same, groups=FF_HALF, name=f"{tag}_ffn_up_dx_a")
    dxn = _mm(dhb, w_up, tb=True, ga=_same, gb=upper, groups=FF_HALF, res=dxn, name=f"{tag}_ffn_up_dx_b")
    dwu = [_mm(xn, dh, ta=True, gb=_same, go=_same, groups=FF_HALF, out_dtype=BF16, name=f"{tag}_ffn_up_dw_{t}")
           for t, dh in (("a", dha), ("b", dhb))]
    gsh[pre + 'w_up'] = jnp.concatenate(dwu, axis=0)
    dx, dg = _rms_bwd(x, rep[pre + 'ffn_norm'], dxn, dy, f"{tag}_ffn_norm_bwd")
    grads[pre + 'ffn_norm'] = dg.reshape(-1)
    return dx


def _mla_fwd(x, rep, taps, q, tabs):
    s = x.shape[0]
    xn = _rms_fwd(x, rep['l0_mix_norm'], "l0_mix_norm")
    hmat = _mm(xn, q['w_in'], name="l0_in")
    wt = _pad_rows(taps, 4 * TAP_ROWS)
    cpar = [rep['l0_conv_b'].reshape(1, -1), rep['l0_conv_ln_g'].reshape(1, -1), rep['l0_conv_ln_b'].reshape(1, -1)]
    (u,) = _conv_fwd([(hmat, 0), (hmat, 1)], wt, [0], cpar, [0, 0, 0], _glu_pre, _ln_silu_post, [BF16],
                     name="l0_conv", **_conva(s))
    qn, kvn = rep['l0_q_norm'].reshape(1, -1), rep['l0_kv_norm'].reshape(1, -1)
    cqn, ckvn = _rowwise(lambda a, b, ga, gb: (_rms(a, ga), _rms(b, gb)),
                         [(hmat, Q_RANK, 2 * CONV_WIDTH // Q_RANK), (hmat, KV_RANK, (2 * CONV_WIDTH + Q_RANK) // KV_RANK)],
                         [qn, kvn], [(Q_RANK, BF16), (KV_RANK, BF16)], [], ts=512, name="l0_latent_norm")
    qf = _mm(cqn, q['w_uq'], name="l0_uq")
    kvf = _mm(ckvn, q['w_ukv'], name="l0_ukv")
    q_rot, k_full, v = _rope_fwd(qf, kvf, hmat, tabs, "l0_rope")
    o, lse = _attn_fwd(q_rot, k_full, v, "l0_attn")
    cat = jnp.concatenate([u, o], axis=1)
    y = _mm(cat, q['w_out'], res=x, name="l0_out")
    return y, (x, xn, hmat, wt, cpar, qn, kvn, cqn, ckvn, q_rot, k_full, v, lse, cat)


def _mla_bwd(dy, saved, rep, q, tabs, grads, gq):
    x, xn, hmat, wt, cpar, qn, kvn, cqn, ckvn, q_rot, k_full, v, lse, cat = saved
    s = x.shape[0]
    dcat = _mm(dy, q['w_out'], tb=True, name="l0_out_dx")
    gq['w_out'] = _mm(cat, dy, ta=True, name="l0_out_dw")
    lse_r, dlt_r = _attn_rows(cat, dcat, lse, "l0_attn_rows")
    dq, dk, dv = _attn_bwd(q_rot, k_full, v, dcat, lse_r, dlt_r, "l0_attn_bwd")
    dqf, dkvf, dkr = _rope_bwd(dq, dk, dv, tabs, "l0_rope_bwd")
    dcqn = _mm(dqf, q['w_uq'], tb=True, name="l0_uq_dx")
    gq['w_uq'] = _mm(cqn, dqf, ta=True, name="l0_uq_dw")
    dckvn = _mm(dkvf, q['w_ukv'], tb=True, name="l0_ukv_dx")
    gq['w_ukv'] = _mm(ckvn, dkvf, ta=True, name="l0_ukv_dw")

    def lat_bwd(a, b, da, db, ga, gb):
        _, vjp = jax.vjp(lambda a_, b_, ga_, gb_: (_rms(a_, ga_), _rms(b_, gb_)), a, b, ga, gb)
        return vjp((da, db))

    dcq, dckv, dqn, dkvn = _rowwise(
        lat_bwd, [(hmat, Q_RANK, 2 * CONV_WIDTH // Q_RANK), (hmat, KV_RANK, (2 * CONV_WIDTH + Q_RANK) // KV_RANK),
                  (dcqn, Q_RANK, 0), (dckvn, KV_RANK, 0)],
        [qn, kvn], [(Q_RANK, F32), (KV_RANK, F32)], [(1, Q_RANK), (1, KV_RANK)], ts=512, name="l0_latent_norm_bwd")
    grads['l0_q_norm'], grads['l0_kv_norm'] = dqn.reshape(-1), dkvn.reshape(-1)
    da, dg, dwt, dcb, dlg, dlb = _conv_bwd(
        [(hmat, 0), (hmat, 1)], wt, [0], cpar, [0, 0, 0], _glu_pre, _ln_silu_post, [(dcat, 0)], F32,
        name="l0_conv_bwd", **_conva(s))
    gq['conv_w'] = dwt[:CONV_K]
    grads['l0_conv_b'], grads['l0_conv_ln_g'], grads['l0_conv_ln_b'] = dcb.reshape(-1), dlg.reshape(-1), dlb.reshape(-1)
    dh = jnp.concatenate([da, dg, dcq, dckv, dkr], axis=1)
    dxn = _mm(dh, q['w_in'], tb=True, name="l0_in_dx")
    gq['w_in'] = _mm(xn, dh, ta=True, name="l0_in_dw")
    dx, dgn = _rms_bwd(x, rep['l0_mix_norm'], dxn, dy, "l0_mix_norm_bwd")
    grads['l0_mix_norm'] = dgn.reshape(-1)
    return dx


def _gelu_skip(yc, u, d):
    return jax.nn.gelu(yc + d * u)


def _glu_out(z1, z2, b1, b2, x):
    return x + (z1 + b1) * jax.nn.sigmoid(z2 + b2)


def _s5_fwd(x, rep, w_in, w_glu):
    xn = _rms_fwd(x, rep['l1_mix_norm'], "l1_mix_norm")
    u = _mm(xn, w_in, name="l1_in")
    lb_re, lb_im, bb_re, bb_im = _s5_disc(rep['l1_log_dt'], rep['l1_a_re'], rep['l1_a_im'], rep['l1_b_re'],
                                          rep['l1_b_im'])
    lam = jnp.concatenate([lb_re.reshape(1, -1), lb_im.reshape(1, -1)], axis=1)
    tr = lambda a: jnp.transpose(a, (0, 2, 1))
    bmat = jnp.concatenate([_bd(tr(bb_re)), _bd(tr(bb_im))], axis=1)
    cmat = jnp.concatenate([_bd(tr(rep['l1_c_re'])), -_bd(tr(rep['l1_c_im']))], axis=0)
    sp = _s5_sparse(x.shape[0])
    bu = _mm(u, bmat, sparse=sp['expand'], name="l1_bu")
    (xs,) = _scan(bu, lam, reverse=False, name="l1_scan")
    yc = _mm(xs, cmat, sparse=sp['reduce'], name="l1_cx")
    dsk = rep['l1_d'].reshape(1, -1)
    (y,) = _rowwise(_gelu_skip, [(yc, SSM_WIDTH, 0), (u, SSM_WIDTH, 0)], [dsk], [(SSM_WIDTH, BF16)], [],
                    ts=512, name="l1_gelu")
    z = _mm(y, w_glu, name="l1_glu")
    bg = rep['l1_b_glu'].reshape(1, -1)
    (out,) = _rowwise(lambda z1, z2, xv, b1, b2: _glu_out(z1, z2, b1, b2, xv),
                      [(z, D_MODEL, 0), (z, D_MODEL, 1), (x, D_MODEL, 0)], [bg[:, :D_MODEL], bg[:, D_MODEL:]],
                      [(D_MODEL, F32)], [], ts=512, name="l1_glu_out")
    return out, (x, xn, u, lam, bmat, cmat, xs, yc, dsk, y, z, bg, w_in, w_glu)


def _s5_bwd(dy, saved, rep, grads, gq):
    x, xn, u, lam, bmat, cmat, xs, yc, dsk, y, z, bg, w_in, w_glu = saved

    def glu_bwd(z1, z2, dv, b1, b2):
        _, vjp = jax.vjp(lambda a, b, c, d: (a + c) * jax.nn.sigmoid(b + d), z1, z2, b1, b2)
        d1, d2, db1, db2 = vjp(dv)
        return jnp.concatenate([d1, d2], axis=1), db1, db2

    dz, db1, db2 = _rowwise(glu_bwd, [(z, D_MODEL, 0), (z, D_MODEL, 1), (dy, D_MODEL, 0)],
                            [bg[:, :D_MODEL], bg[:, D_MODEL:]], [(2 * D_MODEL, BF16)], [(1, D_MODEL), (1, D_MODEL)],
                            ts=512, name="l1_glu_out_bwd")
    grads['l1_b_glu'] = jnp.concatenate([db1, db2], axis=1).reshape(-1)
    dyv = _mm(dz, w_glu, tb=True, name="l1_glu_dx")
    gq['l1_w_glu'] = _mm(y, dz, ta=True, name="l1_glu_dw")

    def gelu_bwd(ycv, uv, dv, dk):
        _, vjp = jax.vjp(_gelu_skip, ycv, uv, dk)
        return vjp(dv)

    dyc, du_skip, dd = _rowwise(gelu_bwd, [(yc, SSM_WIDTH, 0), (u, SSM_WIDTH, 0), (dyv, SSM_WIDTH, 0)], [dsk],
                                [(SSM_WIDTH, F32), (SSM_WIDTH, F32)], [(1, SSM_WIDTH)], ts=512, name="l1_gelu_bwd")
    grads['l1_d'] = dd.reshape(-1)
    sp = _s5_sparse(x.shape[0])
    dxs = _mm(dyc, cmat, tb=True, sparse=sp['expand'], name="l1_cx_dx")
    dcm = _mm(xs, dyc, ta=True, sparse=sp['wide_t'], name="l1_cx_dw")
    gs, dlam = _scan(dxs, lam, reverse=True, xs=xs, name="l1_scan_bwd")
    dlr, dli = dlam[:, :SSM_CH], dlam[:, SSM_CH:]
    du = _mm(gs, bmat, tb=True, res=du_skip, sparse=sp['reduce'], name="l1_bu_dx")
    dbm = _mm(u, gs, ta=True, sparse=sp['narrow_t'], name="l1_bu_dw")
    eye = jnp.eye(S5_TILE_GROUPS, dtype=F32)
    nb = SSM_GROUPS // S5_TILE_GROUPS
    dcm = dcm.reshape(2, nb, S5_TILE_GROUPS, SSM_STATE, S5_TILE_GROUPS, SSM_GROUP)
    dcm = jnp.sum(dcm * eye[None, None, :, None, :, None], axis=4).reshape(2, SSM_GROUPS, SSM_STATE, SSM_GROUP)
    tr = lambda a: jnp.transpose(a, (0, 2, 1))
    grads['l1_c_re'], grads['l1_c_im'] = tr(dcm[0]), -tr(dcm[1])
    dbm = dbm.reshape(S5_TILE_GROUPS, SSM_GROUP, 2, nb, S5_TILE_GROUPS, SSM_STATE)
    dbm = jnp.sum(dbm * eye[:, None, None, None, :, None], axis=0)
    dbm = jnp.transpose(dbm, (1, 2, 3, 4, 0)).reshape(2, SSM_GROUPS, SSM_STATE, SSM_GROUP)
    dbb_re, dbb_im = dbm[0], dbm[1]
    names = ['l1_log_dt', 'l1_a_re', 'l1_a_im', 'l1_b_re', 'l1_b_im']
    _, vjp = jax.vjp(_s5_disc, *[rep[n] for n in names])
    for n, gval in zip(names, vjp((dlr.reshape(SSM_GROUPS, SSM_STATE), dli.reshape(SSM_GROUPS, SSM_STATE), dbb_re, dbb_im))):
        grads[n] = gval
    dxn = _mm(du, w_in, tb=True, name="l1_in_dx")
    gq['l1_w_in'] = _mm(xn, du, ta=True, name="l1_in_dw")
    dx, dgn = _rms_bwd(x, rep['l1_mix_norm'], dxn, dy, "l1_mix_norm_bwd")
    grads['l1_mix_norm'] = dgn.reshape(-1)
    return dx


def _loss_head(x, g, target):
    d = x.shape[1]

    def fn(xv, tv, gv):
        y, vjp = jax.vjp(_rms, xv, gv)
        err = y - tv
        part = 0.5 * jnp.sum(jnp.mean(jnp.square(err), axis=-1, keepdims=True), axis=0, keepdims=True)
        dx, dg = vjp(err * (1.0 / d))
        return dx, jnp.broadcast_to(part, (1, 128)), dg

    return _rowwise(fn, [(x, d, 0), (target, d, 0)], [g.reshape(1, -1)], [(d, F32)], [(1, 128), (1, d)], ts=512,
                    name="loss_head")


FIRST = ('l0_w_in', 'l0_conv_w', 'l0_w_uq', 'l0_w_ukv', 'l0_w_out')
REST = (('l0_w_up', 'l0_ffn_conv_w', 'l0_w_down'),
        ('l1_w_in', 'l1_w_glu', 'l1_w_up', 'l1_ffn_conv_w', 'l1_w_down'))
GRADS_L1 = ('l1_w_in', 'l1_w_glu', 'l1_w_up', 'l1_ffn_conv_w', 'l1_w_down')
GRADS_L0_FFN = ('l0_w_up', 'l0_ffn_conv_w', 'l0_w_down')


def _local_step(x, target, rep, got, wait_rest, send_grads):
    even = {n: _full(n, got[n]) for n in ('l0_w_in', 'l0_w_uq', 'l0_w_ukv', 'l0_w_out')}
    q = _prep_weights(even)
    tabs = _rope_tables(x.shape[0])
    x1, s_mla = _mla_fwd(x, rep, _full('l0_conv_w', got['l0_conv_w']), q, tabs)
    x2, s_f0 = _ffn_fwd(x1, rep, wait_rest(0, x1), 'l0_', "l0")
    got = wait_rest(1, x2)
    x3, s_s5 = _s5_fwd(x2, rep, _full('l1_w_in', got['l1_w_in']), _full('l1_w_glu', got['l1_w_glu']))
    x4, s_f1 = _ffn_fwd(x3, rep, got, 'l1_', "l1")
    dx4, loss, dgf = _loss_head(x4, rep['final_norm'], target)
    grads, gq, gsh = {'final_norm': dgf.reshape(-1)}, {}, {}
    dx3 = _ffn_bwd(dx4, s_f1, rep, 'l1_', "l1", grads, gsh)
    dx2 = _s5_bwd(dx3, s_s5, rep, grads, gq)
    for n in ('l1_w_in', 'l1_w_glu'):
        gsh[n] = _shards(n, gq[n]).astype(BF16)
    dx2 = send_grads(GRADS_L1, gsh, dx2)
    dx1 = _ffn_bwd(dx2, s_f0, rep, 'l0_', "l0", grads, gsh)
    dx1 = send_grads(GRADS_L0_FFN, gsh, dx1)
    dx0 = _mla_bwd(dx1, s_mla, rep, q, tabs, grads, gq)
    full = _unprep_grads(gq)
    full['l0_conv_w'] = gq['conv_w']
    return loss[0, 0], dx0, grads, {n: _shards(n, full[n]).astype(BF16) for n in FIRST}


def kernel(x, l0_mix_norm, l0_w_in, l0_conv_w, l0_conv_b, l0_conv_ln_g, l0_conv_ln_b, l0_q_norm, l0_kv_norm, l0_w_uq, l0_w_ukv, l0_w_out, l0_ffn_norm, l0_w_up, l0_ffn_conv_w, l0_ffn_conv_b, l0_w_down, l1_mix_norm, l1_w_in, l1_log_dt, l1_a_re, l1_a_im, l1_b_re, l1_b_im, l1_c_re, l1_c_im, l1_d, l1_w_glu, l1_b_glu, l1_ffn_norm, l1_w_up, l1_ffn_conv_w, l1_ffn_conv_b, l1_w_down, final_norm, loss_target, m_l0_mix_norm, m_l0_w_in, m_l0_conv_w, m_l0_conv_b, m_l0_conv_ln_g, m_l0_conv_ln_b, m_l0_q_norm, m_l0_kv_norm, m_l0_w_uq, m_l0_w_ukv, m_l0_w_out, m_l0_ffn_norm, m_l0_w_up, m_l0_ffn_conv_w, m_l0_ffn_conv_b, m_l0_w_down, m_l1_mix_norm, m_l1_w_in, m_l1_log_dt, m_l1_a_re, m_l1_a_im, m_l1_b_re, m_l1_b_im, m_l1_c_re, m_l1_c_im, m_l1_d, m_l1_w_glu, m_l1_b_glu, m_l1_ffn_norm, m_l1_w_up, m_l1_ffn_conv_w, m_l1_ffn_conv_b, m_l1_w_down, m_final_norm, v_l0_mix_norm, v_l0_w_in, v_l0_conv_w, v_l0_conv_b, v_l0_conv_ln_g, v_l0_conv_ln_b, v_l0_q_norm, v_l0_kv_norm, v_l0_w_uq, v_l0_w_ukv, v_l0_w_out, v_l0_ffn_norm, v_l0_w_up, v_l0_ffn_conv_w, v_l0_ffn_conv_b, v_l0_w_down, v_l1_mix_norm, v_l1_w_in, v_l1_log_dt, v_l1_a_re, v_l1_a_im, v_l1_b_re, v_l1_b_im, v_l1_c_re, v_l1_c_im, v_l1_d, v_l1_w_glu, v_l1_b_glu, v_l1_ffn_norm, v_l1_w_up, v_l1_ffn_conv_w, v_l1_ffn_conv_b, v_l1_w_down, v_final_norm):
    args = dict(locals())
    w = {n: args[n] for n in WEIGHTS}
    m = {n: args['m_' + n] for n in WEIGHTS}
    v = {n: args['v_' + n] for n in WEIGHTS}
    payload = lambda n: w[n] if n in TAPS else w[n].astype(BF16)
    got = dict(zip(FIRST, _exchange([payload(n) for n in FIRST], ['gather'] * len(FIRST), "gather_first")))
    rep = {n: w[n] for n in REPLICATED}
    rest = []
    for k, names in enumerate(REST):
        flight, got['l0_conv_w'] = _exchange_start([payload(n) for n in names], ['gather'] * len(names),
                                                   got['l0_conv_w'], f"gather_rest{k}_start")
        rest.append(flight)
    wait_rest = lambda k, after: dict(zip(REST[k], _exchange_wait(rest[k], after, f"gather_rest{k}_wait")))
    flights = []

    def send_grads(names, gsh, carry):
        tag = "grads_" + names[0][:2]
        flight, carry = _exchange_start([gsh[n] for n in names], ['scatter'] * len(names), carry, tag + "_start")
        flights.append((names, flight, tag + "_wait"))
        return carry

    loss, dx, grads, gsh = _local_step(x[0], loss_target[0], rep, got, wait_rest, send_grads)

    last = _exchange([gsh[n] for n in FIRST] + [_flat([grads[n] for n in REPLICATED])],
                     ['scatter'] * len(FIRST) + ['gather'], "exchange_grads")
    recv = dict(zip(FIRST, last))
    for names, flight, name in flights:
        recv.update(zip(names, _exchange_wait(flight, last[-1], name)))
    res = [dict(), dict(), dict(), dict()]
    for n in SHARDED:
        for kind, a in enumerate(_adamw(recv[n], w[n], m[n], v[n], "adamw_" + n)):
            res[kind][n] = a
    flatr = lambda d: _flat([d[n] for n in REPLICATED])
    rp_out = _adamw(last[-1], flatr(w), flatr(m), flatr(v), "adamw_replicated")
    for kind in range(4):
        for n, a in zip(REPLICATED, _unflat(rp_out[kind], [w[n].shape for n in REPLICATED])):
            res[kind][n] = a
    total = lax.psum(loss, ("x", "y", "c"))
    return (total, dx[None], *[res[0][n] for n in WEIGHTS], *[res[1][n] for n in WEIGHTS],
            *[res[2][n] for n in WEIGHTS], *[res[3][n] for n in WEIGHTS])
```

```python
import functools

import jax
import jax.numpy as jnp
from jax import lax
from jax.experimental import pallas as pl
from jax.experimental.pallas import tpu as pltpu

F32 = jnp.float32
BF16 = jnp.bfloat16

N_DEV = 8
D_MODEL = 1024
EPS = 1e-6
LN_EPS = 1e-5
CONV_WIDTH = 512
CONV_K = 31
N_HEADS = 8
QK_NOPE = 64
QK_ROPE = 32
V_DIM = 64
HEAD_PAD = 128
Q_RANK = 256
KV_RANK = 128
ROPE_BASE = 10000.0
IN_EVEN = 2 * CONV_WIDTH + Q_RANK + KV_RANK + QK_ROPE
IN_PAD = 1536
KR_LANE = 64
SSM_WIDTH = 512
SSM_GROUP = 16
SSM_GROUPS = 32
SSM_STATE = 64
SSM_CH = SSM_GROUPS * SSM_STATE
D_FF = 2816
FF_SHARD = 2 * D_FF // N_DEV
FF_HALF = N_DEV // 2
FFN_K = 3
TAP_ROWS = 8
ADAM_LR, ADAM_B1, ADAM_B2, ADAM_EPS, ADAM_WD, ADAM_STEP = 0.001, 0.9, 0.999, 1e-08, 0.01, 10

WEIGHTS = ['l0_mix_norm', 'l0_w_in', 'l0_conv_w', 'l0_conv_b', 'l0_conv_ln_g', 'l0_conv_ln_b', 'l0_q_norm',
           'l0_kv_norm', 'l0_w_uq', 'l0_w_ukv', 'l0_w_out', 'l0_ffn_norm', 'l0_w_up', 'l0_ffn_conv_w',
           'l0_ffn_conv_b', 'l0_w_down', 'l1_mix_norm', 'l1_w_in', 'l1_log_dt', 'l1_a_re', 'l1_a_im', 'l1_b_re',
           'l1_b_im', 'l1_c_re', 'l1_c_im', 'l1_d', 'l1_w_glu', 'l1_b_glu', 'l1_ffn_norm', 'l1_w_up',
           'l1_ffn_conv_w', 'l1_ffn_conv_b', 'l1_w_down', 'final_norm']
SHARDED = {'l0_w_in': 1, 'l0_conv_w': 1, 'l0_w_uq': 1, 'l0_w_ukv': 1, 'l0_w_out': 0, 'l0_w_up': 1,
           'l0_ffn_conv_w': 1, 'l0_w_down': 0, 'l1_w_in': 0, 'l1_w_glu': 1, 'l1_w_up': 1, 'l1_ffn_conv_w': 1,
           'l1_w_down': 0}
TAPS = ('l0_conv_w', 'l0_ffn_conv_w', 'l1_ffn_conv_w')
REPLICATED = [n for n in WEIGHTS if n not in SHARDED]


def _tile(n, cands):
    for c in cands:
        if n % c == 0:
            return c
    return n


def _same(g):
    return g


def _mm(a, b, *, ta=False, tb=False, res=None, out_dtype=F32, name, ga=None, gb=None, go=None, groups=1,
        sparse=None):
    a2, b2 = (a.shape[1:] if ga else a.shape), (b.shape[1:] if gb else b.shape)
    m, kd = (a2[1], a2[0]) if ta else a2
    kd2, n = (b2[1], b2[0]) if tb else b2
    assert kd == kd2, (a.shape, b.shape, ta, tb)
    tm = _tile(m, (1024, 512, 256, 128))
    tn = _tile(n, (512, 384, 256, 128))
    tk = _tile(kd, (2048, 1024, 512, 256, 128) if ta else (1024, 512, 256, 128))
    nk = kd // tk
    a_m = b_n = k_of = None
    if sparse is not None:
        (tm, tn, tk), (m, n), nk = sparse['tiles'], sparse['out'], sparse['nk']
        a_m, b_n, k_of = sparse.get('a_m'), sparse.get('b_n'), sparse.get('k_of')
    a_m = a_m or (lambda i, j: i)
    b_n = b_n or (lambda i, j: j)
    k_of = k_of or (lambda i, j, k: k)
    summed = go is None and ga is not None and gb is not None
    assert summed or go is not None or (ga is None and gb is None)
    nkk = nk
    dn = (((0 if ta else 1,), (1 if tb else 0,)), ((), ()))

    def body(*refs):
        if res is None:
            a_ref, b_ref, o_ref, acc_ref = refs
            r_ref = None
        else:
            a_ref, b_ref, r_ref, o_ref, acc_ref = refs
        k = pl.program_id(3)
        if summed:
            p = lax.dot_general(a_ref[0].astype(BF16), b_ref[0].astype(BF16), dn, preferred_element_type=F32)
            for g in range(1, groups):
                p = p + lax.dot_general(a_ref[g].astype(BF16), b_ref[g].astype(BF16), dn, preferred_element_type=F32)
        else:
            p = lax.dot_general(a_ref[...].astype(BF16), b_ref[...].astype(BF16), dn, preferred_element_type=F32)

        @pl.when(k == 0)
        def _():
            acc_ref[...] = p

        @pl.when(k > 0)
        def _():
            acc_ref[...] += p

        @pl.when(k == nkk - 1)
        def _():
            out = acc_ref[...]
            if r_ref is not None:
                out = out + r_ref[...]
            o_ref[...] = out.astype(out_dtype)

    def spec(shape2, idx2, gmap):
        if gmap is None:
            return pl.BlockSpec(shape2, lambda g, i, j, kk: idx2(i, j, kk))
        if summed:
            return pl.BlockSpec((groups,) + shape2, lambda g, i, j, kk: (gmap(0) // groups,) + idx2(i, j, kk))
        return pl.BlockSpec((None,) + shape2, lambda g, i, j, kk: (gmap(g),) + idx2(i, j, kk))

    a_idx = (lambda i, j, k: (k_of(i, j, k), a_m(i, j))) if ta else (lambda i, j, k: (a_m(i, j), k_of(i, j, k)))
    b_idx = (lambda i, j, k: (b_n(i, j), k_of(i, j, k))) if tb else (lambda i, j, k: (k_of(i, j, k), b_n(i, j)))
    a_spec = spec((tk, tm) if ta else (tm, tk), a_idx, ga)
    b_spec = spec((tn, tk) if tb else (tk, tn), b_idx, gb)
    o_spec = spec((tm, tn), lambda i, j, k: (i, j), go)
    in_specs, args = [a_spec, b_spec], [a, b]
    if res is not None:
        in_specs.append(o_spec)
        args.append(res)
    out_shape = (groups, m, n) if go else (m, n)
    return pl.pallas_call(
        body, grid=(groups if go else 1, m // tm, n // tn, nkk), in_specs=in_specs, out_specs=o_spec,
        out_shape=jax.ShapeDtypeStruct(out_shape, out_dtype), scratch_shapes=[pltpu.VMEM((tm, tn), F32)],
        compiler_params=pltpu.CompilerParams(dimension_semantics=("parallel", "parallel", "parallel", "arbitrary")),
        name=name)(*args)


def _rowwise(fn, rows, bcasts, row_outs, red_outs, *, ts, name):
    s = rows[0][0].shape[0]
    nr, nb, nro, nre = len(rows), len(bcasts), len(row_outs), len(red_outs)

    def body(*refs):
        i = pl.program_id(0)
        outs = fn(*[r[...] for r in refs[:nr + nb]])
        if not isinstance(outs, (tuple, list)):
            outs = (outs,)
        o_refs = refs[nr + nb:]
        for q in range(nro):
            o_refs[q][...] = outs[q].astype(o_refs[q].dtype)
        for q in range(nro, nro + nre):
            @pl.when(i == 0)
            def _(q=q):
                o_refs[q][...] = outs[q]

            @pl.when(i > 0)
            def _(q=q):
                o_refs[q][...] += outs[q]

    in_specs = [pl.BlockSpec((ts, w), functools.partial(lambda i, cb: (i, cb), cb=cb)) for (_, w, cb) in rows]
    in_specs += [pl.BlockSpec(b.shape, functools.partial(lambda i, nd: (0,) * nd, nd=b.ndim)) for b in bcasts]
    out_specs = [pl.BlockSpec((ts, w), lambda i: (i, 0)) for (w, _) in row_outs]
    out_specs += [pl.BlockSpec((r, w), lambda i: (0, 0)) for (r, w) in red_outs]
    out_shape = [jax.ShapeDtypeStruct((s, w), dt) for (w, dt) in row_outs]
    out_shape += [jax.ShapeDtypeStruct((r, w), F32) for (r, w) in red_outs]
    return pl.pallas_call(
        body, grid=(s // ts,), in_specs=in_specs, out_specs=out_specs, out_shape=out_shape,
        compiler_params=pltpu.CompilerParams(dimension_semantics=("arbitrary",)), name=name,
    )(*[r[0] for r in rows], *bcasts)


def _rms(x, g):
    return x * lax.rsqrt(jnp.mean(x * x, axis=-1, keepdims=True) + EPS) * g


def _rms_fwd(x, g, name):
    return _rowwise(lambda xv, gv: _rms(xv, gv), [(x, x.shape[1], 0)], [g.reshape(1, -1)],
                    [(x.shape[1], BF16)], [], ts=512, name=name)[0]


def _rms_bwd(x, g, dxn, dres, name):
    d = x.shape[1]

    def fn(xv, dv, rv, gv):
        _, vjp = jax.vjp(_rms, xv, gv)
        dx, dg = vjp(dv.astype(F32))
        return rv + dx, dg

    return _rowwise(fn, [(x, d, 0), (dxn, d, 0), (dres, d, 0)], [g.reshape(1, -1)], [(d, F32)], [(1, d)],
                    ts=512, name=name)


def _cspec(mode, off, rows, width, rowblk, n_rb):
    if mode == 'col':
        return pl.BlockSpec((rows, width), lambda jc, i: (rowblk(i), off + jc))
    return pl.BlockSpec((rows, width), lambda jc, i: ((off + jc) * n_rb + rowblk(i), 0))


TAP_SPREAD = 24


def _stage_shape(k_taps, n, tc):
    return (8, n + TAP_SPREAD, tc) if k_taps > 8 else (1, 8, 128)


def _tap_windows(ref, offsets, n, stage):
    if len(offsets) <= 8:
        return [functools.partial(lambda v: v, ref[pl.ds(o, n), :]) for o in offsets]
    lows = {}
    for o in offsets:
        lows[o % 8] = min(o, lows.get(o % 8, o))
    for r, lo in lows.items():
        span = n + max(o for o in offsets if o % 8 == r) - lo
        stage[r, pl.ds(0, span), :] = ref[pl.ds(lo, span), :]
    return [functools.partial(lambda o: stage[o % 8, pl.ds(o - lows[o % 8], n), :], o) for o in offsets]


def _conv_fwd(xins, w, woffs, params, poffs, pre, post, outs, *, mode, s, k_taps, hb, ts, tc, rc, n_cb, name):
    n_s, nx, ncv, npar, no = s // ts, len(xins), len(woffs), len(params), len(outs)
    rpb = ts // hb
    kp = TAP_ROWS * ((k_taps + TAP_ROWS - 1) // TAP_ROWS)
    pr = 1 if mode == 'col' else TAP_ROWS

    def body(*refs):
        mains, halos = refs[:nx], refs[nx:2 * nx]
        w_refs = refs[2 * nx:2 * nx + ncv]
        p_refs = refs[2 * nx + ncv:2 * nx + ncv + npar]
        o_refs = refs[2 * nx + ncv + npar:2 * nx + ncv + npar + no]
        u_s = refs[2 * nx + ncv + npar + no:2 * nx + ncv + npar + no + ncv]
        win_s = refs[2 * nx + ncv + npar + no + ncv:2 * nx + ncv + npar + no + 2 * ncv]
        stage_s = refs[2 * nx + ncv + npar + no + 2 * ncv:]
        i = pl.program_id(1)
        um = pre(*[r[...].astype(F32) for r in mains])
        uh = pre(*[r[...].astype(F32) for r in halos])
        first = (i > 0).astype(F32)
        for q in range(ncv):
            u_s[q][pl.ds(0, hb), :] = uh[q] * first
            u_s[q][pl.ds(hb, ts), :] = um[q]
        pv = [r[0:1, :] for r in p_refs]

        def chunk(ci, carry):
            r0 = pl.multiple_of(ci * rc, rc)
            cs = []
            for q in range(ncv):
                win_s[q][...] = u_s[q][pl.ds(r0, rc + hb), :]
                acc = jnp.zeros((rc, tc), F32)
                offsets = [hb - (k_taps - 1) + t for t in range(k_taps)]
                for t, ut in enumerate(_tap_windows(win_s[q], offsets, rc, stage_s[q])):
                    acc = acc + w_refs[q][pl.ds(t, 1), :] * ut()
                cs.append(acc)
            res = post(cs, pv)
            for q in range(no):
                o_refs[q][pl.ds(r0, rc), :] = res[q].astype(o_refs[q].dtype)
            return carry

        lax.fori_loop(0, ts // rc, chunk, 0)

    main = lambda i: i
    prev = lambda i: jnp.maximum(i * rpb - 1, 0)
    zero = lambda i: 0
    in_specs = [_cspec(mode, off, ts, tc, main, n_s) for _, off in xins]
    in_specs += [_cspec(mode, off, hb, tc, prev, s // hb) for _, off in xins]
    in_specs += [_cspec(mode, off, kp, tc, zero, 1) for off in woffs]
    in_specs += [_cspec(mode, off, pr, tc, zero, 1) for off in poffs]
    out_specs = [_cspec(mode, 0, ts, tc, main, n_s) for _ in outs]
    oshape = (s, n_cb * tc) if mode == 'col' else (n_cb * s, tc)
    out_shape = [jax.ShapeDtypeStruct(oshape, dt) for dt in outs]
    return pl.pallas_call(
        body, grid=(n_cb, n_s), in_specs=in_specs, out_specs=out_specs, out_shape=out_shape,
        scratch_shapes=[pltpu.VMEM((hb + ts, tc), F32) for _ in range(ncv)]
        + [pltpu.VMEM((hb + rc, tc), F32) for _ in range(ncv)]
        + [pltpu.VMEM(_stage_shape(k_taps, rc, tc), F32) for _ in range(ncv)],
        compiler_params=pltpu.CompilerParams(dimension_semantics=("parallel", "arbitrary")), name=name,
    )(*[a for a, _ in xins], *[a for a, _ in xins], *([w] * ncv), *params)


def _conv_bwd(xins, w, woffs, params, poffs, pre, post, douts, dx_dtype, *, mode, s, k_taps, hb, ts, tc, rc, n_cb, name):
    n_s, nx, ncv, npar, ndo = s // ts, len(xins), len(woffs), len(params), len(douts)
    rpb = ts // hb
    n_hb = s // hb
    kp = TAP_ROWS * ((k_taps + TAP_ROWS - 1) // TAP_ROWS)
    pr = 1 if mode == 'col' else TAP_ROWS

    def body(*refs):
        pos = 0

        def take(n):
            nonlocal pos
            out = refs[pos:pos + n]
            pos += n
            return out

        mains, prevs, nexts = take(nx), take(nx), take(nx)
        d_mains, d_nexts = take(ndo), take(ndo)
        w_refs, p_refs = take(ncv), take(npar)
        dx_refs, dw_refs, dp_refs = take(nx), take(ncv), take(npar)
        u_s, d_s = take(ncv), take(ndo)
        win_s, dcw_s, dwa_s = take(ncv), take(ncv), take(ncv)
        stu_s, std_s = take(ncv), take(ncv)
        i = pl.program_id(1)
        um = pre(*[r[...].astype(F32) for r in mains])
        up = pre(*[r[...].astype(F32) for r in prevs])
        un = pre(*[r[...].astype(F32) for r in nexts])
        first = (i > 0).astype(F32)
        last = (i < n_s - 1).astype(F32)
        for q in range(ncv):
            u_s[q][pl.ds(0, hb), :] = up[q] * first
            u_s[q][pl.ds(hb, ts), :] = um[q]
            u_s[q][pl.ds(hb + ts, hb), :] = un[q]
            dwa_s[q][...] = jnp.zeros((kp * 8, tc), F32)
        for q in range(ndo):
            d_s[q][pl.ds(0, ts), :] = d_mains[q][...].astype(F32)
            d_s[q][pl.ds(ts, hb), :] = d_nexts[q][...].astype(F32) * last
        pv = [r[0:1, :] for r in p_refs]
        fwd_offsets = [hb - (k_taps - 1) + t for t in range(k_taps)]

        def chunk(ci, dpar):
            r0 = pl.multiple_of(ci * rc, rc)
            c_own, c_next, shifted = [], [], []
            for q in range(ncv):
                win_s[q][...] = u_s[q][pl.ds(r0, rc + 2 * hb), :]
                acc = jnp.zeros((rc + hb, tc), F32)
                taps = _tap_windows(win_s[q], fwd_offsets, rc + hb, stu_s[q])
                for t, ut in enumerate(taps):
                    acc = acc + w_refs[q][pl.ds(t, 1), :] * ut()
                c_own.append(acc[:rc])
                c_next.append(acc[rc:])
                shifted.append(taps)
            _, vjp_o = jax.vjp(lambda c, p: tuple(post(c, p)), c_own, pv)
            dc_own, dp_own = vjp_o(tuple(r[pl.ds(r0, rc), :] for r in d_s))
            _, vjp_n = jax.vjp(lambda c: tuple(post(c, pv)), c_next)
            (dc_next,) = vjp_n(tuple(r[pl.ds(r0 + rc, hb), :] for r in d_s))
            dus = []
            for q in range(ncv):
                dcw_s[q][pl.ds(0, rc), :] = dc_own[q]
                dcw_s[q][pl.ds(rc, hb), :] = dc_next[q]
                acc = jnp.zeros((rc, tc), F32)
                bwd_offsets = [k_taps - 1 - t for t in range(k_taps)]
                for t, dct in enumerate(_tap_windows(dcw_s[q], bwd_offsets, rc, std_s[q])):
                    acc = acc + w_refs[q][pl.ds(t, 1), :] * dct()
                    prod = dc_own[q] * shifted[q][t]()[:rc]
                    dwa_s[q][pl.ds(8 * t, 8), :] += jnp.sum(prod.reshape(rc // 8, 8, tc), axis=0)
                dus.append(acc)
            xm = [r[pl.ds(r0, rc), :].astype(F32) for r in mains]
            _, vjp_p = jax.vjp(lambda *xv: tuple(pre(*xv)), *xm)
            dxs = vjp_p(tuple(dus))
            for q in range(nx):
                dx_refs[q][pl.ds(r0, rc), :] = dxs[q].astype(dx_refs[q].dtype)
            return tuple(a + b for a, b in zip(dpar, dp_own))

        dpar = lax.fori_loop(0, ts // rc, chunk, tuple(jnp.zeros((1, tc), F32) for _ in range(npar)))
        for q in range(ncv):
            @pl.when(i == 0)
            def _(q=q):
                dw_refs[q][...] = jnp.zeros((kp, tc), F32)

            for t in range(k_taps):
                dw_refs[q][pl.ds(t, 1), :] += jnp.sum(dwa_s[q][pl.ds(8 * t, 8), :], axis=0, keepdims=True)
        for q in range(npar):
            @pl.when(i == 0)
            def _(q=q):
                dp_refs[q][...] = jnp.zeros((pr, tc), F32)

            dp_refs[q][0:1, :] += dpar[q]

    main = lambda i: i
    prev = lambda i: jnp.maximum(i * rpb - 1, 0)
    nxt = lambda i: jnp.minimum((i + 1) * rpb, n_hb - 1)
    zero = lambda i: 0
    in_specs = [_cspec(mode, off, ts, tc, main, n_s) for _, off in xins]
    in_specs += [_cspec(mode, off, hb, tc, prev, n_hb) for _, off in xins]
    in_specs += [_cspec(mode, off, hb, tc, nxt, n_hb) for _, off in xins]
    in_specs += [_cspec(mode, off, ts, tc, main, n_s) for _, off in douts]
    in_specs += [_cspec(mode, off, hb, tc, nxt, n_hb) for _, off in douts]
    in_specs += [_cspec(mode, off, kp, tc, zero, 1) for off in woffs]
    in_specs += [_cspec(mode, off, pr, tc, zero, 1) for off in poffs]
    out_specs = [_cspec(mode, 0, ts, tc, main, n_s) for _ in xins]
    out_specs += [_cspec(mode, 0, kp, tc, zero, 1) for _ in woffs]
    out_specs += [_cspec(mode, 0, pr, tc, zero, 1) for _ in params]

    def shape(rows):
        return (rows, n_cb * tc) if mode == 'col' else (n_cb * rows, tc)

    out_shape = [jax.ShapeDtypeStruct(shape(s), dx_dtype) for _ in xins]
    out_shape += [jax.ShapeDtypeStruct(shape(kp), F32) for _ in woffs]
    out_shape += [jax.ShapeDtypeStruct(shape(pr), F32) for _ in params]
    xa = [a for a, _ in xins]
    da = [a for a, _ in douts]
    return pl.pallas_call(
        body, grid=(n_cb, n_s), in_specs=in_specs, out_specs=out_specs, out_shape=out_shape,
        scratch_shapes=[pltpu.VMEM((hb + ts + hb, tc), F32) for _ in range(ncv)]
        + [pltpu.VMEM((ts + hb, tc), F32) for _ in range(ndo)]
        + [pltpu.VMEM((rc + 2 * hb, tc), F32) for _ in range(ncv)]
        + [pltpu.VMEM((rc + hb, tc), F32) for _ in range(ncv)]
        + [pltpu.VMEM((kp * 8, tc), F32) for _ in range(ncv)]
        + [pltpu.VMEM(_stage_shape(k_taps, rc + hb, tc), F32) for _ in range(ncv)]
        + [pltpu.VMEM(_stage_shape(k_taps, rc, tc), F32) for _ in range(ncv)],
        compiler_params=pltpu.CompilerParams(dimension_semantics=("parallel", "arbitrary")), name=name,
    )(*xa, *xa, *xa, *da, *da, *([w] * ncv), *params)


def _glu_pre(a, g):
    return [a * jax.nn.sigmoid(g)]


def _ln_silu_post(cs, ps):
    c = cs[0] + ps[0]
    mu = jnp.mean(c, axis=-1, keepdims=True)
    var = jnp.mean(jnp.square(c - mu), axis=-1, keepdims=True)
    y = (c - mu) * lax.rsqrt(var + LN_EPS) * ps[1] + ps[2]
    return [jax.nn.silu(y)]


def _pair_pre(a, b):
    return [a, b]


def _gate_post(cs, ps):
    return [jax.nn.silu(cs[0] + ps[0]) * (cs[1] + ps[1])]


def _conva(s):
    return dict(mode='col', s=s, k_taps=CONV_K, hb=32, ts=512, tc=CONV_WIDTH, rc=64, n_cb=1)


def _gate(s):
    return dict(mode='row', s=s, k_taps=FFN_K, hb=8, ts=512, tc=FF_SHARD, rc=32, n_cb=FF_HALF)


def _rope_tables(s):
    half = QK_ROPE // 2
    inv = ROPE_BASE ** (-jnp.arange(half, dtype=F32) / half)
    ang = jnp.arange(s, dtype=F32)[:, None] * inv[None, :]
    cos, sin = jnp.cos(ang), jnp.sin(ang)
    z = lambda n: jnp.zeros((s, n), F32)
    c = jnp.concatenate([jnp.ones((s, QK_NOPE), F32), cos, cos, z(HEAD_PAD - QK_NOPE - QK_ROPE)], axis=1)
    s1 = jnp.concatenate([z(QK_NOPE), -sin, z(HEAD_PAD - QK_NOPE - half)], axis=1)
    s2 = jnp.concatenate([z(QK_NOPE + half), sin, z(HEAD_PAD - QK_NOPE - QK_ROPE)], axis=1)
    return c, s1, s2


def _rot(t, c, s1, s2):
    half = QK_ROPE // 2
    return t * c + pltpu.roll(t, HEAD_PAD - half, 1) * s1 + pltpu.roll(t, half, 1) * s2


def _rot_t(d, c, s1, s2):
    half = QK_ROPE // 2
    return d * c + pltpu.roll(d * s1, half, 1) + pltpu.roll(d * s2, HEAD_PAD - half, 1)


def _heads(v):
    return [v[:, h * HEAD_PAD:(h + 1) * HEAD_PAD] for h in range(N_HEADS)]


def _rope_fwd(qf, kvf, hmat, tabs, name):
    w = N_HEADS * HEAD_PAD

    def fn(q, k, v, kr, c, s1, s2):
        krr = _rot(kr, c, s1, s2)
        qo = jnp.concatenate([_rot(t, c, s1, s2) for t in _heads(q)], axis=1)
        ko = jnp.concatenate([t + krr for t in _heads(k)], axis=1)
        lane = lax.broadcasted_iota(jnp.int32, v.shape, 1) & (HEAD_PAD - 1)
        return qo, ko, jnp.where(lane == ONES_LANE, 1.0, v)

    rows = [(qf, w, 0), (kvf, w, 0), (kvf, w, 1), (hmat, HEAD_PAD, IN_PAD // HEAD_PAD - 1)]
    rows += [(t, HEAD_PAD, 0) for t in tabs]
    return _rowwise(fn, rows, [], [(w, BF16)] * 3, [], ts=512, name=name)


def _rope_bwd(dq, dk, dv, tabs, name):
    w = N_HEADS * HEAD_PAD

    def fn(dqv, dkv, dvv, c, s1, s2):
        dqo = jnp.concatenate([_rot_t(t, c, s1, s2) for t in _heads(dqv)], axis=1)
        ksum = functools.reduce(lambda a, b: a + b, _heads(dkv))
        return dqo, jnp.concatenate([dkv, dvv], axis=1), _rot_t(ksum, c, s1, s2)

    rows = [(dq, w, 0), (dk, w, 0), (dv, w, 0)] + [(t, HEAD_PAD, 0) for t in tabs]
    return _rowwise(fn, rows, [], [(w, BF16), (2 * w, BF16), (HEAD_PAD, F32)], [], ts=512, name=name)


ATT_Q = 1024
ATT_SUB = 512
ATT_KV = 1024
ATT_SCALE = (QK_NOPE + QK_ROPE) ** -0.5
LOG2E = 1.4426950408889634
ATT_C2 = ATT_SCALE * LOG2E
ONES_LANE = V_DIM


def _nt(a, b):
    return lax.dot_general(a, b, (((1,), (1,)), ((), ())), preferred_element_type=F32)


def _lanes(x, w):
    return x if w == HEAD_PAD else jnp.tile(x, (1, w // HEAD_PAD))


def _tri(w, transposed):
    r = lax.broadcasted_iota(jnp.int32, (w, w), 0)
    c = lax.broadcasted_iota(jnp.int32, (w, w), 1)
    return (r <= c) if transposed else (c <= r)


def _attn_fwd(q, k, v, name):
    s = q.shape[0]
    tq, kvc = min(ATT_Q, s), min(ATT_KV, s)
    nsub, per = tq // ATT_SUB, tq // kvc

    def body(q_ref, k_ref, v_ref, o_ref, lse_ref, m_s, acc_s):
        i = pl.program_id(1)
        m_s[...] = jnp.full((tq, HEAD_PAD), -jnp.inf, F32)
        acc_s[...] = jnp.zeros((tq, HEAD_PAD), F32)

        def update(r0, n, kb, vb, diag):
            rows = pl.ds(r0, n)
            w = kb.shape[0]
            sc = _nt(q_ref[rows, :], kb)
            if diag:
                sc = jnp.where(_tri(w, False), sc, -jnp.inf)
            m_prev = m_s[rows, :]
            m_next = jnp.maximum(m_prev, jnp.max(sc, axis=1, keepdims=True))
            p = jnp.exp2((sc - _lanes(m_next, w)) * ATT_C2)
            alpha = jnp.exp2((m_prev - m_next) * ATT_C2)
            acc_s[rows, :] = alpha * acc_s[rows, :] + jnp.dot(p.astype(BF16), vb, preferred_element_type=F32)
            m_s[rows, :] = m_next

        def below(j, carry):
            at = pl.ds(pl.multiple_of(j * kvc, kvc), kvc)
            update(0, tq, k_ref[at, :], v_ref[at, :], False)
            return carry

        lax.fori_loop(0, i * per, below, 0)
        for r in range(nsub):
            for c in range(r + 1):
                at = pl.ds(pl.multiple_of(i * tq + c * ATT_SUB, ATT_SUB), ATT_SUB)
                update(r * ATT_SUB, ATT_SUB, k_ref[at, :], v_ref[at, :], c == r)
        l = acc_s[:, ONES_LANE:ONES_LANE + 1]
        o_ref[...] = (acc_s[...] / l).astype(BF16)
        lse_ref[...] = m_s[...] * ATT_SCALE + jnp.log(l)

    q_spec = pl.BlockSpec((tq, HEAD_PAD), lambda h, i: (i, h))
    kv_spec = pl.BlockSpec((s, HEAD_PAD), lambda h, i: (0, h))
    return pl.pallas_call(
        body, grid=(N_HEADS, s // tq), in_specs=[q_spec, kv_spec, kv_spec], out_specs=[q_spec, q_spec],
        out_shape=[jax.ShapeDtypeStruct(q.shape, BF16), jax.ShapeDtypeStruct(q.shape, F32)],
        scratch_shapes=[pltpu.VMEM((tq, HEAD_PAD), F32)] * 2,
        compiler_params=pltpu.CompilerParams(dimension_semantics=("parallel", "arbitrary")), name=name,
    )(q, k, v)


def _attn_rows(cat, dcat, lse, name):
    s = lse.shape[0]
    tq, kvc = min(ATT_Q, s), min(ATT_KV, s)
    per = tq // kvc
    ob = CONV_WIDTH // HEAD_PAD

    def body(o_ref, do_ref, lse_ref, lser_ref, dltr_ref):
        dl = jnp.broadcast_to(jnp.sum(do_ref[...] * o_ref[...].astype(F32), axis=1, keepdims=True), (tq, HEAD_PAD))
        l2 = lse_ref[...] * LOG2E
        for c in range(per):
            lser_ref[c] = jnp.transpose(l2[c * kvc:(c + 1) * kvc])[0:8, :]
            dltr_ref[c] = jnp.transpose(dl[c * kvc:(c + 1) * kvc])[0:8, :]

    q_spec = pl.BlockSpec((tq, HEAD_PAD), lambda h, i: (i, h))
    o_spec = pl.BlockSpec((tq, HEAD_PAD), lambda h, i: (i, ob + h))
    row_spec = pl.BlockSpec((None, per, 8, kvc), lambda h, i: (h, i, 0, 0))
    rows = jax.ShapeDtypeStruct((N_HEADS, s // kvc, 8, kvc), F32)
    return pl.pallas_call(
        body, grid=(N_HEADS, s // tq), in_specs=[o_spec, o_spec, q_spec], out_specs=[row_spec, row_spec],
        out_shape=[rows, rows],
        compiler_params=pltpu.CompilerParams(dimension_semantics=("parallel", "parallel")), name=name,
    )(cat, dcat, lse)


def _attn_bwd(q, k, v, dcat, lse_r, dlt_r, name):
    s = q.shape[0]
    tk, kvc = min(ATT_Q, s), min(ATT_KV, s)
    nsub, per, n_chunks = tk // ATT_SUB, tk // kvc, s // kvc
    n_j = s // tk
    ob = CONV_WIDTH // HEAD_PAD

    def body(k_ref, v_ref, q_ref, do_ref, lse_ref, dl_ref, dk_ref, dv_ref, dq_ref, dk_s, dv_s):
        j = pl.program_id(1)
        dk_s[...] = jnp.zeros((tk, HEAD_PAD), F32)
        dv_s[...] = jnp.zeros((tk, HEAD_PAD), F32)

        @pl.when(j == 0)
        def _():
            dq_ref[...] = jnp.zeros((s, HEAD_PAD), F32)

        def update(r0, n, at, lrow, drow, diag):
            rows = pl.ds(r0, n)
            qb, dob = q_ref[at, :], do_ref[at, :].astype(BF16)
            sc = _nt(k_ref[rows, :], qb)
            if diag:
                sc = jnp.where(_tri(qb.shape[0], True), sc, -jnp.inf)
            p = jnp.exp2(sc * ATT_C2 - lrow)
            dp = _nt(v_ref[rows, :], dob)
            ds = (p * (dp - drow)).astype(BF16)
            dv_s[rows, :] += jnp.dot(p.astype(BF16), dob, preferred_element_type=F32)
            dk_s[rows, :] += jnp.dot(ds, qb, preferred_element_type=F32)
            dq_ref[at, :] += lax.dot_general(ds, k_ref[rows, :], (((0,), (0,)), ((), ())), preferred_element_type=F32)

        def above(ic, carry):
            at = pl.ds(pl.multiple_of(ic * kvc, kvc), kvc)
            update(0, tk, at, lse_ref[ic, 0:1, :], dl_ref[ic, 0:1, :], False)
            return carry

        lax.fori_loop((j + 1) * per, n_chunks, above, 0)
        for r in range(nsub):
            for c in range(r, nsub):
                at = pl.ds(pl.multiple_of(j * tk + c * ATT_SUB, ATT_SUB), ATT_SUB)
                ic = j * per + (c * ATT_SUB) // kvc
                lo = (c * ATT_SUB) % kvc
                update(r * ATT_SUB, ATT_SUB, at, lse_ref[ic, 0:1, lo:lo + ATT_SUB], dl_ref[ic, 0:1, lo:lo + ATT_SUB],
                       c == r)
        dk_ref[...] = dk_s[...] * ATT_SCALE
        dv_ref[...] = dv_s[...]

        @pl.when(j == n_j - 1)
        def _():
            dq_ref[...] = dq_ref[...] * ATT_SCALE

    kv_spec = pl.BlockSpec((tk, HEAD_PAD), lambda h, j: (j, h))
    q_spec = pl.BlockSpec((s, HEAD_PAD), lambda h, j: (0, h))
    do_spec = pl.BlockSpec((s, HEAD_PAD), lambda h, j: (0, ob + h))
    row_spec = pl.BlockSpec((None, n_chunks, 8, kvc), lambda h, j: (h, 0, 0, 0))
    full = jax.ShapeDtypeStruct(q.shape, F32)
    dk, dv, dq = pl.pallas_call(
        body, grid=(N_HEADS, n_j), in_specs=[kv_spec, kv_spec, q_spec, do_spec, row_spec, row_spec],
        out_specs=[kv_spec, kv_spec, q_spec], out_shape=[full, full, full],
        scratch_shapes=[pltpu.VMEM((tk, HEAD_PAD), F32)] * 2,
        compiler_params=pltpu.CompilerParams(dimension_semantics=("parallel", "arbitrary")), name=name,
    )(k, v, q, dcat, lse_r, dlt_r)
    return dq, dk, dv


SCAN_T = 256
SCAN_C = 512


def _scan(b, lam, *, reverse, xs=None, name):
    s = b.shape[0]
    t = min(SCAN_T, s)
    n_t, n_c = s // t, SSM_CH // SCAN_C
    with_dlam = xs is not None

    def shift(a, d, row):
        if d >= 8:
            z = jnp.zeros((d, SCAN_C), F32)
            return jnp.concatenate([a[d:], z], axis=0) if reverse else jnp.concatenate([z, a[:t - d]], axis=0)
        if reverse:
            return jnp.where(row < t - d, pltpu.roll(a, t - d, 0), 0.0)
        return jnp.where(row >= d, pltpu.roll(a, d, 0), 0.0)

    def body(*refs):
        if with_dlam:
            b_ref, lam_ref, x_ref, o_ref, dl_ref, c_s = refs
        else:
            b_ref, lam_ref, o_ref, c_s = refs
        k = pl.program_id(0)

        @pl.when(k == 0)
        def _():
            c_s[...] = jnp.zeros((1, 2 * SSM_CH), F32)
            if with_dlam:
                dl_ref[...] = jnp.zeros((1, 2 * SSM_CH), F32)

        row = lax.broadcasted_iota(jnp.int32, (t, SCAN_C), 0)
        edge = (row == t - 1) if reverse else (row == 0)
        for ch in range(n_c):
            re = pl.ds(ch * SCAN_C, SCAN_C)
            im = pl.ds(SSM_CH + ch * SCAN_C, SCAN_C)
            lr = lam_ref[:, re]
            li = -lam_ref[:, im] if reverse else lam_ref[:, im]
            cr, ci = c_s[:, re], c_s[:, im]
            ar = b_ref[:, re] + jnp.where(edge, lr * cr - li * ci, 0.0)
            ai = b_ref[:, im] + jnp.where(edge, lr * ci + li * cr, 0.0)
            d = 1
            while d < t:
                sr, si = shift(ar, d, row), shift(ai, d, row)
                ar, ai = ar + lr * sr - li * si, ai + lr * si + li * sr
                lr, li = lr * lr - li * li, 2.0 * lr * li
                d *= 2
            o_ref[:, re] = ar.astype(o_ref.dtype)
            o_ref[:, im] = ai.astype(o_ref.dtype)
            if with_dlam:
                gr = jnp.where(edge, cr, shift(ar, 1, row))
                gi = jnp.where(edge, ci, shift(ai, 1, row))
                xr, xi = x_ref[:, re].astype(F32), x_ref[:, im].astype(F32)
                dl_ref[:, re] += jnp.sum(xr * gr + xi * gi, axis=0, keepdims=True)
                dl_ref[:, im] += jnp.sum(xr * gi - xi * gr, axis=0, keepdims=True)
            last = 0 if reverse else t - 1
            c_s[:, re] = ar[last:last + 1, :]
            c_s[:, im] = ai[last:last + 1, :]

    tm = (lambda k: (n_t - 1 - k, 0)) if reverse else (lambda k: (k, 0))
    blk = pl.BlockSpec((t, 2 * SSM_CH), tm)
    vec = pl.BlockSpec((1, 2 * SSM_CH), lambda k: (0, 0))
    in_specs, args = [blk, vec], [b, lam]
    out_specs, out_shape = [blk], [jax.ShapeDtypeStruct((s, 2 * SSM_CH), BF16)]
    if with_dlam:
        in_specs.append(blk)
        args.append(xs)
        out_specs.append(vec)
        out_shape.append(jax.ShapeDtypeStruct((1, 2 * SSM_CH), F32))
    return pl.pallas_call(
        body, grid=(n_t,), in_specs=in_specs, out_specs=out_specs, out_shape=out_shape,
        scratch_shapes=[pltpu.VMEM((1, 2 * SSM_CH), F32)],
        compiler_params=pltpu.CompilerParams(dimension_semantics=("arbitrary",)), name=name,
    )(*args)


def _s5_disc(log_dt, a_re, a_im, b_re, b_im):
    dt = jnp.exp(log_dt)[:, None]
    mag = jnp.exp(a_re * dt)
    lb_re, lb_im = mag * jnp.cos(a_im * dt), mag * jnp.sin(a_im * dt)
    den = a_re * a_re + a_im * a_im
    nr, ni = lb_re - 1.0, lb_im
    f_re = (nr * a_re + ni * a_im) / den
    f_im = (ni * a_re - nr * a_im) / den
    bb_re = f_re[..., None] * b_re - f_im[..., None] * b_im
    bb_im = f_re[..., None] * b_im + f_im[..., None] * b_re
    return lb_re, lb_im, bb_re, bb_im


def _bd(a):
    g, i, j = a.shape
    eye = jnp.eye(g, dtype=a.dtype)
    return (a[:, :, None, :] * eye[:, None, :, None]).reshape(g * i, g * j)


S5_TILE_GROUPS = HEAD_PAD // SSM_GROUP


def _s5_sparse(s):
    nb = SSM_GROUPS // S5_TILE_GROUPS
    cw, sw = S5_TILE_GROUPS * SSM_GROUP, S5_TILE_GROUPS * SSM_STATE
    tm = min(1024, s)
    return dict(
        expand=dict(tiles=(tm, sw, cw), out=(s, 2 * SSM_CH), nk=1, k_of=lambda i, j, k: j % nb),
        reduce=dict(tiles=(tm, cw, sw), out=(s, SSM_WIDTH), nk=2, k_of=lambda i, j, k: j + nb * k),
        wide_t=dict(tiles=(sw, cw, tm), out=(2 * SSM_CH, cw), nk=s // tm, b_n=lambda i, j: i % nb),
        narrow_t=dict(tiles=(cw, sw, tm), out=(cw, 2 * SSM_CH), nk=s // tm, a_m=lambda i, j: j % nb))


def _exchange(arrs, modes, name):
    n = len(arrs)
    shapes = [a.shape if md == 'scatter' else (N_DEV,) + a.shape for a, md in zip(arrs, modes)]

    def body(*refs):
        srcs, outs = refs[:n], refs[n:2 * n]
        send_sems, recv_sems = refs[2 * n:]
        x, y, c = lax.axis_index("x"), lax.axis_index("y"), lax.axis_index("c")
        me = 4 * x + 2 * y + c
        copies = []
        for r, pos, peer in _peers(x, y, c):
            for q in range(n):
                copies.append(pltpu.make_async_remote_copy(
                    src_ref=srcs[q].at[peer] if modes[q] == 'scatter' else srcs[q], dst_ref=outs[q].at[me],
                    send_sem=send_sems.at[(r - 1) * n + q], recv_sem=recv_sems.at[(r - 1) * n + q], device_id=pos,
                    device_id_type=pl.DeviceIdType.MESH))
        for cp in copies:
            cp.start()
        for cp in copies:
            cp.wait_recv()
        for cp in copies:
            cp.wait_send()

    any_spec = pl.BlockSpec(memory_space=pl.ANY)
    outs = pl.pallas_call(
        body, out_shape=[jax.ShapeDtypeStruct(sh, a.dtype) for sh, a in zip(shapes, arrs)],
        in_specs=[any_spec] * n, out_specs=[any_spec] * n,
        scratch_shapes=[pltpu.SemaphoreType.DMA(((N_DEV - 1) * n,)), pltpu.SemaphoreType.DMA(((N_DEV - 1) * n,))],
        compiler_params=pltpu.CompilerParams(has_side_effects=True), name=name,
    )(*arrs)
    return _own_slots(outs, arrs, modes)


def _peers(x, y, c):
    out = []
    for r in range(1, N_DEV):
        px, py, pc = x ^ (r >> 2), y ^ ((r >> 1) & 1), c ^ (r & 1)
        out.append((r, (px, py, pc), 4 * px + 2 * py + pc))
    return out


def _own_slots(lands, srcs, modes):
    me = 4 * lax.axis_index("x") + 2 * lax.axis_index("y") + lax.axis_index("c")
    out = []
    for land, src, md in zip(lands, srcs, modes):
        own = lax.dynamic_index_in_dim(src, me, 0, keepdims=False) if md == 'scatter' else src
        out.append(lax.dynamic_update_index_in_dim(land, own, me, 0))
    return out


def _exchange_start(arrs, modes, carry, name):
    n = len(arrs)
    shapes = [a.shape if md == 'scatter' else (N_DEV,) + a.shape for a, md in zip(arrs, modes)]
    lands = [lax.empty(sh, a.dtype) for sh, a in zip(shapes, arrs)]

    def body(*refs):
        srcs, zones = refs[:n], refs[n:2 * n]
        send_sems, recv_sems = refs[2 * n + 1], refs[2 * n + 2]
        x, y, c = lax.axis_index("x"), lax.axis_index("y"), lax.axis_index("c")
        me = 4 * x + 2 * y + c
        for r, pos, peer in _peers(x, y, c):
            for q in range(n):
                src = srcs[q].at[peer] if modes[q] == 'scatter' else srcs[q]
                pltpu.make_async_remote_copy(
                    src_ref=src, dst_ref=zones[q].at[me], send_sem=send_sems.at[(r - 1) * n + q],
                    recv_sem=recv_sems.at[(r - 1) * n + q], device_id=pos, device_id_type=pl.DeviceIdType.MESH).start()

    hbm = pl.BlockSpec(memory_space=pltpu.HBM)
    sem = pl.BlockSpec(memory_space=pltpu.SEMAPHORE)
    thru = arrs + lands + [carry]
    sems = pltpu.SemaphoreType.DMA(((N_DEV - 1) * n,))
    outs = pl.pallas_call(
        body, name=name, out_shape=(sems, sems, *[pltpu.HBM(a.shape, a.dtype) for a in thru]),
        in_specs=[hbm] * len(thru), out_specs=(sem, sem, *[hbm] * len(thru)),
        input_output_aliases={q: 2 + q for q in range(len(thru))},
        compiler_params=pltpu.CompilerParams(has_side_effects=pltpu.SideEffectType.DATAFLOW_SIDE_EFFECTING),
    )(*[pltpu.with_memory_space_constraint(a, pltpu.HBM) for a in thru])
    return dict(send=outs[0], recv=outs[1], srcs=list(outs[2:2 + n]), lands=list(outs[2 + n:2 + 2 * n]),
                modes=modes), outs[-1]


def _exchange_wait(flight, after, name):
    n = len(flight['srcs'])
    modes = flight['modes']

    def body(*refs):
        srcs, zones = refs[:n], refs[n:2 * n]
        send_sems, recv_sems = refs[2 * n], refs[2 * n + 1]
        x, y, c = lax.axis_index("x"), lax.axis_index("y"), lax.axis_index("c")
        me = 4 * x + 2 * y + c
        for r, pos, peer in _peers(x, y, c):
            for q in range(n):
                src = srcs[q].at[peer] if modes[q] == 'scatter' else srcs[q]
                cp = pltpu.make_async_remote_copy(
                    src_ref=src, dst_ref=zones[q].at[me], send_sem=send_sems.at[(r - 1) * n + q],
                    recv_sem=recv_sems.at[(r - 1) * n + q], device_id=pos, device_id_type=pl.DeviceIdType.MESH)
                cp.wait_send()
                cp.wait_recv()

    hbm = pl.BlockSpec(memory_space=pltpu.HBM)
    sem = pl.BlockSpec(memory_space=pltpu.SEMAPHORE)
    bufs = flight['srcs'] + flight['lands']
    outs = pl.pallas_call(
        body, name=name, out_shape=tuple(pltpu.HBM(a.shape, a.dtype) for a in bufs),
        in_specs=[hbm] * (2 * n) + [sem, sem, pl.BlockSpec(memory_space=pl.ANY)], out_specs=tuple([hbm] * (2 * n)),
        input_output_aliases={q: q for q in range(2 * n)},
        compiler_params=pltpu.CompilerParams(has_side_effects=pltpu.SideEffectType.DATAFLOW_SIDE_EFFECTING),
    )(*bufs, flight['send'], flight['recv'], after)
    return _own_slots(outs[n:], outs[:n], modes)


def _adamw(parts, w, m, v, name):
    r, c = w.shape
    tr = _tile(r, (256, 128))

    def body(p_ref, w_ref, m_ref, v_ref, g_ref, d_ref, nm_ref, nv_ref):
        g = p_ref[0].astype(F32)
        for d in range(1, N_DEV):
            g = g + p_ref[d].astype(F32)
        m2 = ADAM_B1 * m_ref[...] + (1.0 - ADAM_B1) * g
        v2 = ADAM_B2 * v_ref[...] + (1.0 - ADAM_B2) * jnp.square(g)
        m_hat = m2 / (1.0 - ADAM_B1 ** ADAM_STEP)
        v_hat = v2 / (1.0 - ADAM_B2 ** ADAM_STEP)
        g_ref[...] = g
        d_ref[...] = -ADAM_LR * (m_hat / (jnp.sqrt(v_hat) + ADAM_EPS) + ADAM_WD * w_ref[...])
        nm_ref[...] = m2
        nv_ref[...] = v2

    spec = pl.BlockSpec((tr, c), lambda i: (i, 0))
    return pl.pallas_call(
        body, grid=(r // tr,), in_specs=[pl.BlockSpec((N_DEV, tr, c), lambda i: (0, i, 0)), spec, spec, spec],
        out_specs=[spec] * 4, out_shape=[jax.ShapeDtypeStruct((r, c), F32)] * 4,
        compiler_params=pltpu.CompilerParams(dimension_semantics=("parallel",)), name=name,
    )(parts, w, m, v)


FLAT_W = 512
FLAT_ROWS = 256


def _flat(arrs, rows=FLAT_ROWS):
    v = jnp.concatenate([a.reshape(-1) for a in arrs])
    return jnp.pad(v, (0, (-v.shape[0]) % (rows * FLAT_W))).reshape(-1, FLAT_W)


def _unflat(flat, shapes):
    v = flat.reshape(-1)
    out, off = [], 0
    for sh in shapes:
        n = 1
        for d in sh:
            n *= d
        out.append(v[off:off + n].reshape(sh))
        off += n
    return out


def _full(name, stacked):
    if SHARDED[name] == 0:
        return stacked.reshape((-1,) + stacked.shape[2:])
    return jnp.transpose(stacked, (1, 0, 2)).reshape(stacked.shape[1], -1)


def _shards(name, full):
    if SHARDED[name] == 0:
        return full.reshape((N_DEV, -1) + full.shape[1:])
    r, c = full.shape
    return jnp.transpose(full.reshape(r, N_DEV, c // N_DEV), (1, 0, 2))


def _prep_weights(p):
    q = {}
    if 'l0_w_in' in p:
        w_in = p['l0_w_in']
        z = lambda n: jnp.zeros((D_MODEL, n), w_in.dtype)
        q['w_in'] = jnp.concatenate([w_in[:, :IN_EVEN - QK_ROPE], z(KR_LANE), w_in[:, IN_EVEN - QK_ROPE:],
                                     z(HEAD_PAD - KR_LANE - QK_ROPE)], axis=1)
    dqk = QK_NOPE + QK_ROPE
    if 'l0_w_uq' in p:
        q['w_uq'] = jnp.pad(p['l0_w_uq'].reshape(Q_RANK, N_HEADS, dqk), ((0, 0), (0, 0), (0, HEAD_PAD - dqk))
                            ).reshape(Q_RANK, N_HEADS * HEAD_PAD)
    if 'l0_w_ukv' in p:
        ukv = p['l0_w_ukv'].reshape(KV_RANK, N_HEADS, 2, QK_NOPE)
        padh = lambda a: jnp.pad(a, ((0, 0), (0, 0), (0, HEAD_PAD - QK_NOPE))).reshape(KV_RANK, N_HEADS * HEAD_PAD)
        q['w_ukv'] = jnp.concatenate([padh(ukv[:, :, 0]), padh(ukv[:, :, 1])], axis=1)
    if 'l0_w_out' in p:
        wo = p['l0_w_out']
        wo_a = jnp.pad(wo[CONV_WIDTH:].reshape(N_HEADS, V_DIM, D_MODEL), ((0, 0), (0, HEAD_PAD - V_DIM), (0, 0)))
        q['w_out'] = jnp.concatenate([wo[:CONV_WIDTH], wo_a.reshape(N_HEADS * HEAD_PAD, D_MODEL)], axis=0)
    return q


def _unprep_grads(g):
    out = {}
    if 'w_in' in g:
        d = g['w_in']
        out['l0_w_in'] = jnp.concatenate([d[:, :IN_EVEN - QK_ROPE],
                                          d[:, IN_EVEN - QK_ROPE + KR_LANE:IN_EVEN + KR_LANE]], axis=1)
    dqk = QK_NOPE + QK_ROPE
    if 'w_uq' in g:
        out['l0_w_uq'] = g['w_uq'].reshape(Q_RANK, N_HEADS, HEAD_PAD)[:, :, :dqk].reshape(Q_RANK, N_HEADS * dqk)
    if 'w_ukv' in g:
        d = g['w_ukv'].reshape(KV_RANK, 2, N_HEADS, HEAD_PAD)[:, :, :, :QK_NOPE]
        out['l0_w_ukv'] = jnp.transpose(d, (0, 2, 1, 3)).reshape(KV_RANK, N_HEADS * 2 * QK_NOPE)
    if 'w_out' in g:
        d = g['w_out']
        da = d[CONV_WIDTH:].reshape(N_HEADS, HEAD_PAD, D_MODEL)[:, :V_DIM].reshape(N_HEADS * V_DIM, D_MODEL)
        out['l0_w_out'] = jnp.concatenate([d[:CONV_WIDTH], da], axis=0)
    return out


def _pad_rows(w, rows):
    return jnp.pad(w, [(0, 0)] * (w.ndim - 2) + [(0, rows - w.shape[-2]), (0, 0)])


def _ffn_fwd(x, rep, got, pre, tag):
    s = x.shape[0]
    xn = _rms_fwd(x, rep[pre + 'ffn_norm'], f"{tag}_ffn_norm")
    w_up = got[pre + 'w_up']
    hu = _mm(xn, w_up, gb=_same, go=_same, groups=N_DEV, name=f"{tag}_ffn_up").reshape(N_DEV * s, FF_SHARD)
    taps = _pad_rows(got[pre + 'ffn_conv_w'], TAP_ROWS).reshape(N_DEV * TAP_ROWS, FF_SHARD)
    bias = _pad_rows(rep[pre + 'ffn_conv_b'].reshape(N_DEV, 1, FF_SHARD), TAP_ROWS).reshape(N_DEV * TAP_ROWS, FF_SHARD)
    (act,) = _conv_fwd([(hu, 0), (hu, FF_HALF)], taps, [0, FF_HALF], [bias, bias], [0, FF_HALF], _pair_pre,
                       _gate_post, [BF16], name=f"{tag}_ffn_gate", **_gate(s))
    act = act.reshape(FF_HALF, s, FF_SHARD)
    w_down = got[pre + 'w_down'].reshape(FF_HALF, FF_SHARD, D_MODEL)
    y = _mm(act, w_down, ga=_same, gb=_same, groups=FF_HALF, res=x, name=f"{tag}_ffn_down")
    return y, (x, xn, hu, act, taps, bias, w_up, w_down)


def _ffn_bwd(dy, saved, rep, pre, tag, grads, gsh):
    x, xn, hu, act, taps, bias, w_up, w_down = saved
    s = x.shape[0]
    dact = _mm(dy, w_down, tb=True, gb=_same, go=_same, groups=FF_HALF, name=f"{tag}_ffn_down_dx")
    gsh[pre + 'w_down'] = _mm(act, dy, ta=True, ga=_same, go=_same, groups=FF_HALF, out_dtype=BF16,
                              name=f"{tag}_ffn_down_dw").reshape(N_DEV, FF_SHARD // 2, D_MODEL)
    dha, dhb, dwa, dwb, dba, dbb = _conv_bwd(
        [(hu, 0), (hu, FF_HALF)], taps, [0, FF_HALF], [bias, bias], [0, FF_HALF], _pair_pre, _gate_post,
        [(dact.reshape(FF_HALF * s, FF_SHARD), 0)], BF16, name=f"{tag}_ffn_gate_bwd", **_gate(s))
    dha, dhb = dha.reshape(FF_HALF, s, FF_SHARD), dhb.reshape(FF_HALF, s, FF_SHARD)
    dtaps = jnp.concatenate([dwa, dwb], axis=0).reshape(N_DEV, TAP_ROWS, FF_SHARD)
    gsh[pre + 'ffn_conv_w'] = dtaps[:, :FFN_K].astype(BF16)
    grads[pre + 'ffn_conv_b'] = jnp.concatenate([dba, dbb], axis=0).reshape(N_DEV, TAP_ROWS, FF_SHARD)[:, 0].reshape(-1)
    upper = lambda g: g + FF_HALF
    dxn = _mm(dha, w_up, tb=True, ga=_same, gb=_same, groups=FF_HALF, name=f"{tag}_ffn_up_dx_a")
    dxn = _mm(dhb, w_up, tb=True, ga=_same, gb=upper, groups=FF_HALF, res=dxn, name=f"{tag}_ffn_up_dx_b")
    dwu = [_mm(xn, dh, ta=True, gb=_same, go=_same, groups=FF_HALF, out_dtype=BF16, name=f"{tag}_ffn_up_dw_{t}")
           for t, dh in (("a", dha), ("b", dhb))]
    gsh[pre + 'w_up'] = jnp.concatenate(dwu, axis=0)
    dx, dg = _rms_bwd(x, rep[pre + 'ffn_norm'], dxn, dy, f"{tag}_ffn_norm_bwd")
    grads[pre + 'ffn_norm'] = dg.reshape(-1)
    return dx


def _mla_fwd(x, rep, q, wait_mixer, tabs):
    s = x.shape[0]
    xn = _rms_fwd(x, rep['l0_mix_norm'], "l0_mix_norm")
    hmat = _mm(xn, q['w_in'], name="l0_in")
    got = wait_mixer(hmat)
    q.update(_prep_weights({n: _full(n, got[n]) for n in ('l0_w_uq', 'l0_w_ukv', 'l0_w_out')}))
    wt = _pad_rows(_full('l0_conv_w', got['l0_conv_w']), 4 * TAP_ROWS)
    cpar = [rep['l0_conv_b'].reshape(1, -1), rep['l0_conv_ln_g'].reshape(1, -1), rep['l0_conv_ln_b'].reshape(1, -1)]
    (u,) = _conv_fwd([(hmat, 0), (hmat, 1)], wt, [0], cpar, [0, 0, 0], _glu_pre, _ln_silu_post, [BF16],
                     name="l0_conv", **_conva(s))
    qn, kvn = rep['l0_q_norm'].reshape(1, -1), rep['l0_kv_norm'].reshape(1, -1)
    cqn, ckvn = _rowwise(lambda a, b, ga, gb: (_rms(a, ga), _rms(b, gb)),
                         [(hmat, Q_RANK, 2 * CONV_WIDTH // Q_RANK), (hmat, KV_RANK, (2 * CONV_WIDTH + Q_RANK) // KV_RANK)],
                         [qn, kvn], [(Q_RANK, BF16), (KV_RANK, BF16)], [], ts=512, name="l0_latent_norm")
    qf = _mm(cqn, q['w_uq'], name="l0_uq")
    kvf = _mm(ckvn, q['w_ukv'], name="l0_ukv")
    q_rot, k_full, v = _rope_fwd(qf, kvf, hmat, tabs, "l0_rope")
    o, lse = _attn_fwd(q_rot, k_full, v, "l0_attn")
    cat = jnp.concatenate([u, o], axis=1)
    y = _mm(cat, q['w_out'], res=x, name="l0_out")
    return y, (x, xn, hmat, wt, cpar, qn, kvn, cqn, ckvn, q_rot, k_full, v, lse, cat)


def _mla_bwd(dy, saved, rep, q, tabs, grads, gq, send_grads):
    x, xn, hmat, wt, cpar, qn, kvn, cqn, ckvn, q_rot, k_full, v, lse, cat = saved
    s = x.shape[0]
    dcat = _mm(dy, q['w_out'], tb=True, name="l0_out_dx")
    gq['w_out'] = _mm(cat, dy, ta=True, name="l0_out_dw")
    lse_r, dlt_r = _attn_rows(cat, dcat, lse, "l0_attn_rows")
    dq, dk, dv = _attn_bwd(q_rot, k_full, v, dcat, lse_r, dlt_r, "l0_attn_bwd")
    dqf, dkvf, dkr = _rope_bwd(dq, dk, dv, tabs, "l0_rope_bwd")
    dcqn = _mm(dqf, q['w_uq'], tb=True, name="l0_uq_dx")
    gq['w_uq'] = _mm(cqn, dqf, ta=True, name="l0_uq_dw")
    dckvn = _mm(dkvf, q['w_ukv'], tb=True, name="l0_ukv_dx")
    gq['w_ukv'] = _mm(ckvn, dkvf, ta=True, name="l0_ukv_dw")
    early = _unprep_grads({n: gq[n] for n in ('w_out', 'w_uq', 'w_ukv')})
    dckvn = send_grads(GRADS_MIXER, {n: _shards(n, g).astype(BF16) for n, g in early.items()}, dckvn)

    def lat_bwd(a, b, da, db, ga, gb):
        _, vjp = jax.vjp(lambda a_, b_, ga_, gb_: (_rms(a_, ga_), _rms(b_, gb_)), a, b, ga, gb)
        return vjp((da, db))

    dcq, dckv, dqn, dkvn = _rowwise(
        lat_bwd, [(hmat, Q_RANK, 2 * CONV_WIDTH // Q_RANK), (hmat, KV_RANK, (2 * CONV_WIDTH + Q_RANK) // KV_RANK),
                  (dcqn, Q_RANK, 0), (dckvn, KV_RANK, 0)],
        [qn, kvn], [(Q_RANK, F32), (KV_RANK, F32)], [(1, Q_RANK), (1, KV_RANK)], ts=512, name="l0_latent_norm_bwd")
    grads['l0_q_norm'], grads['l0_kv_norm'] = dqn.reshape(-1), dkvn.reshape(-1)
    da, dg, dwt, dcb, dlg, dlb = _conv_bwd(
        [(hmat, 0), (hmat, 1)], wt, [0], cpar, [0, 0, 0], _glu_pre, _ln_silu_post, [(dcat, 0)], F32,
        name="l0_conv_bwd", **_conva(s))
    gq['conv_w'] = dwt[:CONV_K]
    grads['l0_conv_b'], grads['l0_conv_ln_g'], grads['l0_conv_ln_b'] = dcb.reshape(-1), dlg.reshape(-1), dlb.reshape(-1)
    dh = jnp.concatenate([da, dg, dcq, dckv, dkr], axis=1)
    dxn = _mm(dh, q['w_in'], tb=True, name="l0_in_dx")
    gq['w_in'] = _mm(xn, dh, ta=True, name="l0_in_dw")
    dx, dgn = _rms_bwd(x, rep['l0_mix_norm'], dxn, dy, "l0_mix_norm_bwd")
    grads['l0_mix_norm'] = dgn.reshape(-1)
    return dx


def _gelu_skip(yc, u, d):
    return jax.nn.gelu(yc + d * u)


def _glu_out(z1, z2, b1, b2, x):
    return x + (z1 + b1) * jax.nn.sigmoid(z2 + b2)


def _s5_fwd(x, rep, w_in, w_glu):
    xn = _rms_fwd(x, rep['l1_mix_norm'], "l1_mix_norm")
    u = _mm(xn, w_in, name="l1_in")
    lb_re, lb_im, bb_re, bb_im = _s5_disc(rep['l1_log_dt'], rep['l1_a_re'], rep['l1_a_im'], rep['l1_b_re'],
                                          rep['l1_b_im'])
    lam = jnp.concatenate([lb_re.reshape(1, -1), lb_im.reshape(1, -1)], axis=1)
    tr = lambda a: jnp.transpose(a, (0, 2, 1))
    bmat = jnp.concatenate([_bd(tr(bb_re)), _bd(tr(bb_im))], axis=1)
    cmat = jnp.concatenate([_bd(tr(rep['l1_c_re'])), -_bd(tr(rep['l1_c_im']))], axis=0)
    sp = _s5_sparse(x.shape[0])
    bu = _mm(u, bmat, sparse=sp['expand'], name="l1_bu")
    (xs,) = _scan(bu, lam, reverse=False, name="l1_scan")
    yc = _mm(xs, cmat, sparse=sp['reduce'], name="l1_cx")
    dsk = rep['l1_d'].reshape(1, -1)
    (y,) = _rowwise(_gelu_skip, [(yc, SSM_WIDTH, 0), (u, SSM_WIDTH, 0)], [dsk], [(SSM_WIDTH, BF16)], [],
                    ts=512, name="l1_gelu")
    z = _mm(y, w_glu, name="l1_glu")
    bg = rep['l1_b_glu'].reshape(1, -1)
    (out,) = _rowwise(lambda z1, z2, xv, b1, b2: _glu_out(z1, z2, b1, b2, xv),
                      [(z, D_MODEL, 0), (z, D_MODEL, 1), (x, D_MODEL, 0)], [bg[:, :D_MODEL], bg[:, D_MODEL:]],
                      [(D_MODEL, F32)], [], ts=512, name="l1_glu_out")
    return out, (x, xn, u, lam, bmat, cmat, xs, yc, dsk, y, z, bg, w_in, w_glu)


def _s5_bwd(dy, saved, rep, grads, gq):
    x, xn, u, lam, bmat, cmat, xs, yc, dsk, y, z, bg, w_in, w_glu = saved

    def glu_bwd(z1, z2, dv, b1, b2):
        _, vjp = jax.vjp(lambda a, b, c, d: (a + c) * jax.nn.sigmoid(b + d), z1, z2, b1, b2)
        d1, d2, db1, db2 = vjp(dv)
        return jnp.concatenate([d1, d2], axis=1), db1, db2

    dz, db1, db2 = _rowwise(glu_bwd, [(z, D_MODEL, 0), (z, D_MODEL, 1), (dy, D_MODEL, 0)],
                            [bg[:, :D_MODEL], bg[:, D_MODEL:]], [(2 * D_MODEL, BF16)], [(1, D_MODEL), (1, D_MODEL)],
                            ts=512, name="l1_glu_out_bwd")
    grads['l1_b_glu'] = jnp.concatenate([db1, db2], axis=1).reshape(-1)
    dyv = _mm(dz, w_glu, tb=True, name="l1_glu_dx")
    gq['l1_w_glu'] = _mm(y, dz, ta=True, name="l1_glu_dw")

    def gelu_bwd(ycv, uv, dv, dk):
        _, vjp = jax.vjp(_gelu_skip, ycv, uv, dk)
        return vjp(dv)

    dyc, du_skip, dd = _rowwise(gelu_bwd, [(yc, SSM_WIDTH, 0), (u, SSM_WIDTH, 0), (dyv, SSM_WIDTH, 0)], [dsk],
                                [(SSM_WIDTH, F32), (SSM_WIDTH, F32)], [(1, SSM_WIDTH)], ts=512, name="l1_gelu_bwd")
    grads['l1_d'] = dd.reshape(-1)
    sp = _s5_sparse(x.shape[0])
    dxs = _mm(dyc, cmat, tb=True, sparse=sp['expand'], name="l1_cx_dx")
    dcm = _mm(xs, dyc, ta=True, sparse=sp['wide_t'], name="l1_cx_dw")
    gs, dlam = _scan(dxs, lam, reverse=True, xs=xs, name="l1_scan_bwd")
    dlr, dli = dlam[:, :SSM_CH], dlam[:, SSM_CH:]
    du = _mm(gs, bmat, tb=True, res=du_skip, sparse=sp['reduce'], name="l1_bu_dx")
    dbm = _mm(u, gs, ta=True, sparse=sp['narrow_t'], name="l1_bu_dw")
    eye = jnp.eye(S5_TILE_GROUPS, dtype=F32)
    nb = SSM_GROUPS // S5_TILE_GROUPS
    dcm = dcm.reshape(2, nb, S5_TILE_GROUPS, SSM_STATE, S5_TILE_GROUPS, SSM_GROUP)
    dcm = jnp.sum(dcm * eye[None, None, :, None, :, None], axis=4).reshape(2, SSM_GROUPS, SSM_STATE, SSM_GROUP)
    tr = lambda a: jnp.transpose(a, (0, 2, 1))
    grads['l1_c_re'], grads['l1_c_im'] = tr(dcm[0]), -tr(dcm[1])
    dbm = dbm.reshape(S5_TILE_GROUPS, SSM_GROUP, 2, nb, S5_TILE_GROUPS, SSM_STATE)
    dbm = jnp.sum(dbm * eye[:, None, None, None, :, None], axis=0)
    dbm = jnp.transpose(dbm, (1, 2, 3, 4, 0)).reshape(2, SSM_GROUPS, SSM_STATE, SSM_GROUP)
    dbb_re, dbb_im = dbm[0], dbm[1]
    names = ['l1_log_dt', 'l1_a_re', 'l1_a_im', 'l1_b_re', 'l1_b_im']
    _, vjp = jax.vjp(_s5_disc, *[rep[n] for n in names])
    for n, gval in zip(names, vjp((dlr.reshape(SSM_GROUPS, SSM_STATE), dli.reshape(SSM_GROUPS, SSM_STATE), dbb_re, dbb_im))):
        grads[n] = gval
    dxn = _mm(du, w_in, tb=True, name="l1_in_dx")
    gq['l1_w_in'] = _mm(xn, du, ta=True, name="l1_in_dw")
    dx, dgn = _rms_bwd(x, rep['l1_mix_norm'], dxn, dy, "l1_mix_norm_bwd")
    grads['l1_mix_norm'] = dgn.reshape(-1)
    return dx


def _loss_head(x, g, target):
    d = x.shape[1]

    def fn(xv, tv, gv):
        y, vjp = jax.vjp(_rms, xv, gv)
        err = y - tv
        part = 0.5 * jnp.sum(jnp.mean(jnp.square(err), axis=-1, keepdims=True), axis=0, keepdims=True)
        dx, dg = vjp(err * (1.0 / d))
        return dx, jnp.broadcast_to(part, (1, 128)), dg

    return _rowwise(fn, [(x, d, 0), (target, d, 0)], [g.reshape(1, -1)], [(d, F32)], [(1, 128), (1, d)], ts=512,
                    name="loss_head")


FIRST = ('l0_w_in',)
REST = (('l0_conv_w', 'l0_w_uq', 'l0_w_ukv', 'l0_w_out'),
        ('l0_w_up', 'l0_ffn_conv_w', 'l0_w_down'),
        ('l1_w_in', 'l1_w_glu', 'l1_w_up', 'l1_ffn_conv_w', 'l1_w_down'))
GRADS_L1 = ('l1_w_in', 'l1_w_glu', 'l1_w_up', 'l1_ffn_conv_w', 'l1_w_down')
GRADS_L0_FFN = ('l0_w_up', 'l0_ffn_conv_w', 'l0_w_down')
GRADS_MIXER = ('l0_w_out', 'l0_w_uq', 'l0_w_ukv')
LAST = ('l0_w_in', 'l0_conv_w')
REP_LATE = ('l0_mix_norm', 'l0_conv_b', 'l0_conv_ln_g', 'l0_conv_ln_b', 'l0_q_norm', 'l0_kv_norm')
REP_EARLY = tuple(n for n in REPLICATED if n not in REP_LATE)


def _local_step(x, target, rep, got, wait_rest, send_grads):
    q = _prep_weights({'l0_w_in': _full('l0_w_in', got['l0_w_in'])})
    tabs = _rope_tables(x.shape[0])
    x1, s_mla = _mla_fwd(x, rep, q, lambda after: wait_rest(0, after), tabs)
    x2, s_f0 = _ffn_fwd(x1, rep, wait_rest(1, x1), 'l0_', "l0")
    got = wait_rest(2, x2)
    x3, s_s5 = _s5_fwd(x2, rep, _full('l1_w_in', got['l1_w_in']), _full('l1_w_glu', got['l1_w_glu']))
    x4, s_f1 = _ffn_fwd(x3, rep, got, 'l1_', "l1")
    dx4, loss, dgf = _loss_head(x4, rep['final_norm'], target)
    grads, gq, gsh = {'final_norm': dgf.reshape(-1)}, {}, {}
    dx3 = _ffn_bwd(dx4, s_f1, rep, 'l1_', "l1", grads, gsh)
    dx2 = _s5_bwd(dx3, s_s5, rep, grads, gq)
    for n in ('l1_w_in', 'l1_w_glu'):
        gsh[n] = _shards(n, gq[n]).astype(BF16)
    dx2 = send_grads(GRADS_L1, gsh, dx2)
    dx1 = _ffn_bwd(dx2, s_f0, rep, 'l0_', "l0", grads, gsh)
    dx1 = send_grads(GRADS_L0_FFN, gsh, dx1, small=[grads[n] for n in REP_EARLY])
    dx0 = _mla_bwd(dx1, s_mla, rep, q, tabs, grads, gq, send_grads)
    full = _unprep_grads({'w_in': gq['w_in']})
    full['l0_conv_w'] = gq['conv_w']
    return loss[0, 0], dx0, grads, {n: _shards(n, full[n]).astype(BF16) for n in LAST}


def kernel(x, l0_mix_norm, l0_w_in, l0_conv_w, l0_conv_b, l0_conv_ln_g, l0_conv_ln_b, l0_q_norm, l0_kv_norm, l0_w_uq, l0_w_ukv, l0_w_out, l0_ffn_norm, l0_w_up, l0_ffn_conv_w, l0_ffn_conv_b, l0_w_down, l1_mix_norm, l1_w_in, l1_log_dt, l1_a_re, l1_a_im, l1_b_re, l1_b_im, l1_c_re, l1_c_im, l1_d, l1_w_glu, l1_b_glu, l1_ffn_norm, l1_w_up, l1_ffn_conv_w, l1_ffn_conv_b, l1_w_down, final_norm, loss_target, m_l0_mix_norm, m_l0_w_in, m_l0_conv_w, m_l0_conv_b, m_l0_conv_ln_g, m_l0_conv_ln_b, m_l0_q_norm, m_l0_kv_norm, m_l0_w_uq, m_l0_w_ukv, m_l0_w_out, m_l0_ffn_norm, m_l0_w_up, m_l0_ffn_conv_w, m_l0_ffn_conv_b, m_l0_w_down, m_l1_mix_norm, m_l1_w_in, m_l1_log_dt, m_l1_a_re, m_l1_a_im, m_l1_b_re, m_l1_b_im, m_l1_c_re, m_l1_c_im, m_l1_d, m_l1_w_glu, m_l1_b_glu, m_l1_ffn_norm, m_l1_w_up, m_l1_ffn_conv_w, m_l1_ffn_conv_b, m_l1_w_down, m_final_norm, v_l0_mix_norm, v_l0_w_in, v_l0_conv_w, v_l0_conv_b, v_l0_conv_ln_g, v_l0_conv_ln_b, v_l0_q_norm, v_l0_kv_norm, v_l0_w_uq, v_l0_w_ukv, v_l0_w_out, v_l0_ffn_norm, v_l0_w_up, v_l0_ffn_conv_w, v_l0_ffn_conv_b, v_l0_w_down, v_l1_mix_norm, v_l1_w_in, v_l1_log_dt, v_l1_a_re, v_l1_a_im, v_l1_b_re, v_l1_b_im, v_l1_c_re, v_l1_c_im, v_l1_d, v_l1_w_glu, v_l1_b_glu, v_l1_ffn_norm, v_l1_w_up, v_l1_ffn_conv_w, v_l1_ffn_conv_b, v_l1_w_down, v_final_norm):
    args = dict(locals())
    w = {n: args[n] for n in WEIGHTS}
    m = {n: args['m_' + n] for n in WEIGHTS}
    v = {n: args['v_' + n] for n in WEIGHTS}
    payload = lambda n: w[n] if n in TAPS else w[n].astype(BF16)
    got = dict(zip(FIRST, _exchange([payload(n) for n in FIRST], ['gather'] * len(FIRST), "gather_first")))
    rep = {n: w[n] for n in REPLICATED}
    rest = []
    for k, names in enumerate(REST):
        flight, got['l0_w_in'] = _exchange_start([payload(n) for n in names], ['gather'] * len(names),
                                                 got['l0_w_in'], f"gather_rest{k}_start")
        rest.append(flight)
    wait_rest = lambda k, after: dict(zip(REST[k], _exchange_wait(rest[k], after, f"gather_rest{k}_wait")))
    flights = []

    def send_grads(names, gsh, carry, small=None):
        tag = f"grads{len(flights)}"
        arrs, modes = [gsh[n] for n in names], ['scatter'] * len(names)
        if small is not None:
            arrs, modes = arrs + [_flat(small)], modes + ['gather']
        flight, carry = _exchange_start(arrs, modes, carry, tag + "_start")
        flights.append((names, flight, tag + "_wait"))
        return carry

    loss, dx, grads, gsh = _local_step(x[0], loss_target[0], rep, got, wait_rest, send_grads)

    last = _exchange([gsh[n] for n in LAST] + [_flat([grads[n] for n in REP_LATE], rows=8)],
                     ['scatter'] * len(LAST) + ['gather'], "exchange_grads")
    recv = dict(zip(LAST, last))
    small = {REP_LATE: last[-1]}
    for names, flight, name in flights:
        lands = _exchange_wait(flight, last[-1], name)
        recv.update(zip(names, lands))
        if len(lands) > len(names):
            small[REP_EARLY] = lands[-1]
    res = [dict(), dict(), dict(), dict()]
    for n in SHARDED:
        for kind, a in enumerate(_adamw(recv[n], w[n], m[n], v[n], "adamw_" + n)):
            res[kind][n] = a
    for names, parts in small.items():
        flatr = lambda d: _flat([d[n] for n in names], rows=parts.shape[1])
        rp_out = _adamw(parts, flatr(w), flatr(m), flatr(v), "adamw_replicated_" + names[0])
        for kind in range(4):
            for n, a in zip(names, _unflat(rp_out[kind], [w[n].shape for n in names])):
                res[kind][n] = a
    total = lax.psum(loss, ("x", "y", "c"))
    return (total, dx[None], *[res[0][n] for n in WEIGHTS], *[res[1][n] for n in WEIGHTS],
            *[res[2][n] for n in WEIGHTS], *[res[3][n] for n in WEIGHTS])
```

```python
import functools

import jax
import jax.numpy as jnp
from jax import lax
from jax.experimental import pallas as pl
from jax.experimental.pallas import tpu as pltpu

F32 = jnp.float32
BF16 = jnp.bfloat16

N_DEV = 8
D_MODEL = 1024
EPS = 1e-6
LN_EPS = 1e-5
CONV_WIDTH = 512
CONV_K = 31
N_HEADS = 8
QK_NOPE = 64
QK_ROPE = 32
V_DIM = 64
HEAD_PAD = 128
Q_RANK = 256
KV_RANK = 128
ROPE_BASE = 10000.0
IN_EVEN = 2 * CONV_WIDTH + Q_RANK + KV_RANK + QK_ROPE
IN_PAD = 1536
KR_LANE = 64
SSM_WIDTH = 512
SSM_GROUP = 16
SSM_GROUPS = 32
SSM_STATE = 64
SSM_CH = SSM_GROUPS * SSM_STATE
D_FF = 2816
FF_SHARD = 2 * D_FF // N_DEV
FF_HALF = N_DEV // 2
FFN_K = 3
TAP_ROWS = 8
ADAM_LR, ADAM_B1, ADAM_B2, ADAM_EPS, ADAM_WD, ADAM_STEP = 0.001, 0.9, 0.999, 1e-08, 0.01, 10

WEIGHTS = ['l0_mix_norm', 'l0_w_in', 'l0_conv_w', 'l0_conv_b', 'l0_conv_ln_g', 'l0_conv_ln_b', 'l0_q_norm',
           'l0_kv_norm', 'l0_w_uq', 'l0_w_ukv', 'l0_w_out', 'l0_ffn_norm', 'l0_w_up', 'l0_ffn_conv_w',
           'l0_ffn_conv_b', 'l0_w_down', 'l1_mix_norm', 'l1_w_in', 'l1_log_dt', 'l1_a_re', 'l1_a_im', 'l1_b_re',
           'l1_b_im', 'l1_c_re', 'l1_c_im', 'l1_d', 'l1_w_glu', 'l1_b_glu', 'l1_ffn_norm', 'l1_w_up',
           'l1_ffn_conv_w', 'l1_ffn_conv_b', 'l1_w_down', 'final_norm']
SHARDED = {'l0_w_in': 1, 'l0_conv_w': 1, 'l0_w_uq': 1, 'l0_w_ukv': 1, 'l0_w_out': 0, 'l0_w_up': 1,
           'l0_ffn_conv_w': 1, 'l0_w_down': 0, 'l1_w_in': 0, 'l1_w_glu': 1, 'l1_w_up': 1, 'l1_ffn_conv_w': 1,
           'l1_w_down': 0}
TAPS = ('l0_conv_w', 'l0_ffn_conv_w', 'l1_ffn_conv_w')
REPLICATED = [n for n in WEIGHTS if n not in SHARDED]


def _tile(n, cands):
    for c in cands:
        if n % c == 0:
            return c
    return n


def _same(g):
    return g


def _mm(a, b, *, ta=False, tb=False, res=None, out_dtype=F32, name, ga=None, gb=None, go=None, groups=1,
        sparse=None):
    a2, b2 = (a.shape[1:] if ga else a.shape), (b.shape[1:] if gb else b.shape)
    m, kd = (a2[1], a2[0]) if ta else a2
    kd2, n = (b2[1], b2[0]) if tb else b2
    assert kd == kd2, (a.shape, b.shape, ta, tb)
    tm = _tile(m, (1024, 512, 256, 128))
    tn = _tile(n, (512, 384, 256, 128))
    tk = _tile(kd, (2048, 1024, 512, 256, 128) if ta else (1024, 512, 256, 128))
    nk = kd // tk
    a_m = b_n = k_of = None
    if sparse is not None:
        (tm, tn, tk), (m, n), nk = sparse['tiles'], sparse['out'], sparse['nk']
        a_m, b_n, k_of = sparse.get('a_m'), sparse.get('b_n'), sparse.get('k_of')
    a_m = a_m or (lambda i, j: i)
    b_n = b_n or (lambda i, j: j)
    k_of = k_of or (lambda i, j, k: k)
    summed = go is None and ga is not None and gb is not None
    assert summed or go is not None or (ga is None and gb is None)
    nkk = nk
    dn = (((0 if ta else 1,), (1 if tb else 0,)), ((), ()))

    def body(*refs):
        if res is None:
            a_ref, b_ref, o_ref, acc_ref = refs
            r_ref = None
        else:
            a_ref, b_ref, r_ref, o_ref, acc_ref = refs
        k = pl.program_id(3)
        if summed:
            p = lax.dot_general(a_ref[0].astype(BF16), b_ref[0].astype(BF16), dn, preferred_element_type=F32)
            for g in range(1, groups):
                p = p + lax.dot_general(a_ref[g].astype(BF16), b_ref[g].astype(BF16), dn, preferred_element_type=F32)
        else:
            p = lax.dot_general(a_ref[...].astype(BF16), b_ref[...].astype(BF16), dn, preferred_element_type=F32)

        @pl.when(k == 0)
        def _():
            acc_ref[...] = p

        @pl.when(k > 0)
        def _():
            acc_ref[...] += p

        @pl.when(k == nkk - 1)
        def _():
            out = acc_ref[...]
            if r_ref is not None:
                out = out + r_ref[...]
            o_ref[...] = out.astype(out_dtype)

    def spec(shape2, idx2, gmap):
        if gmap is None:
            return pl.BlockSpec(shape2, lambda g, i, j, kk: idx2(i, j, kk))
        if summed:
            return pl.BlockSpec((groups,) + shape2, lambda g, i, j, kk: (gmap(0) // groups,) + idx2(i, j, kk))
        return pl.BlockSpec((None,) + shape2, lambda g, i, j, kk: (gmap(g),) + idx2(i, j, kk))

    a_idx = (lambda i, j, k: (k_of(i, j, k), a_m(i, j))) if ta else (lambda i, j, k: (a_m(i, j), k_of(i, j, k)))
    b_idx = (lambda i, j, k: (b_n(i, j), k_of(i, j, k))) if tb else (lambda i, j, k: (k_of(i, j, k), b_n(i, j)))
    a_spec = spec((tk, tm) if ta else (tm, tk), a_idx, ga)
    b_spec = spec((tn, tk) if tb else (tk, tn), b_idx, gb)
    o_spec = spec((tm, tn), lambda i, j, k: (i, j), go)
    in_specs, args = [a_spec, b_spec], [a, b]
    if res is not None:
        in_specs.append(o_spec)
        args.append(res)
    out_shape = (groups, m, n) if go else (m, n)
    return pl.pallas_call(
        body, grid=(groups if go else 1, m // tm, n // tn, nkk), in_specs=in_specs, out_specs=o_spec,
        out_shape=jax.ShapeDtypeStruct(out_shape, out_dtype), scratch_shapes=[pltpu.VMEM((tm, tn), F32)],
        compiler_params=pltpu.CompilerParams(dimension_semantics=("parallel", "parallel", "parallel", "arbitrary")),
        name=name)(*args)


ROW_TS = 1024
def _rowwise(fn, rows, bcasts, row_outs, red_outs, *, ts, name):
    s = rows[0][0].shape[0]
    ts = min(ts, s)
    nr, nb, nro, nre = len(rows), len(bcasts), len(row_outs), len(red_outs)

    def body(*refs):
        i = pl.program_id(0)
        outs = fn(*[r[...] for r in refs[:nr + nb]])
        if not isinstance(outs, (tuple, list)):
            outs = (outs,)
        o_refs = refs[nr + nb:]
        for q in range(nro):
            o_refs[q][...] = outs[q].astype(o_refs[q].dtype)
        for q in range(nro, nro + nre):
            @pl.when(i == 0)
            def _(q=q):
                o_refs[q][...] = outs[q]

            @pl.when(i > 0)
            def _(q=q):
                o_refs[q][...] += outs[q]

    in_specs = [pl.BlockSpec((ts, w), functools.partial(lambda i, cb: (i, cb), cb=cb)) for (_, w, cb) in rows]
    in_specs += [pl.BlockSpec(b.shape, functools.partial(lambda i, nd: (0,) * nd, nd=b.ndim)) for b in bcasts]
    out_specs = [pl.BlockSpec((ts, w), lambda i: (i, 0)) for (w, _) in row_outs]
    out_specs += [pl.BlockSpec((r, w), lambda i: (0, 0)) for (r, w) in red_outs]
    out_shape = [jax.ShapeDtypeStruct((s, w), dt) for (w, dt) in row_outs]
    out_shape += [jax.ShapeDtypeStruct((r, w), F32) for (r, w) in red_outs]
    return pl.pallas_call(
        body, grid=(s // ts,), in_specs=in_specs, out_specs=out_specs, out_shape=out_shape,
        compiler_params=pltpu.CompilerParams(dimension_semantics=("arbitrary",)), name=name,
    )(*[r[0] for r in rows], *bcasts)


def _rms(x, g):
    return x * lax.rsqrt(jnp.mean(x * x, axis=-1, keepdims=True) + EPS) * g


def _rms_fwd(x, g, name):
    return _rowwise(lambda xv, gv: _rms(xv, gv), [(x, x.shape[1], 0)], [g.reshape(1, -1)],
                    [(x.shape[1], BF16)], [], ts=ROW_TS, name=name)[0]


def _rms_bwd(x, g, dxn, dres, name):
    d = x.shape[1]

    def fn(xv, dv, rv, gv):
        _, vjp = jax.vjp(_rms, xv, gv)
        dx, dg = vjp(dv.astype(F32))
        return rv + dx, dg

    return _rowwise(fn, [(x, d, 0), (dxn, d, 0), (dres, d, 0)], [g.reshape(1, -1)], [(d, F32)], [(1, d)],
                    ts=ROW_TS, name=name)


def _cspec(mode, off, rows, width, rowblk, n_rb):
    if mode == 'col':
        return pl.BlockSpec((rows, width), lambda jc, i: (rowblk(i), off + jc))
    return pl.BlockSpec((rows, width), lambda jc, i: ((off + jc) * n_rb + rowblk(i), 0))


TAP_SPREAD = 24


def _stage_shape(k_taps, n, tc):
    return (8, n + TAP_SPREAD, tc) if k_taps > 8 else (1, 8, 128)


def _tap_windows(ref, offsets, n, stage):
    if len(offsets) <= 8:
        return [functools.partial(lambda v: v, ref[pl.ds(o, n), :]) for o in offsets]
    lows = {}
    for o in offsets:
        lows[o % 8] = min(o, lows.get(o % 8, o))
    for r, lo in lows.items():
        span = n + max(o for o in offsets if o % 8 == r) - lo
        stage[r, pl.ds(0, span), :] = ref[pl.ds(lo, span), :]
    return [functools.partial(lambda o: stage[o % 8, pl.ds(o - lows[o % 8], n), :], o) for o in offsets]


def _conv_fwd(xins, w, woffs, params, poffs, pre, post, outs, *, mode, s, k_taps, hb, ts, tc, rc, n_cb, name):
    n_s, nx, ncv, npar, no = s // ts, len(xins), len(woffs), len(params), len(outs)
    rpb = ts // hb
    kp = TAP_ROWS * ((k_taps + TAP_ROWS - 1) // TAP_ROWS)
    pr = 1 if mode == 'col' else TAP_ROWS

    def body(*refs):
        mains, halos = refs[:nx], refs[nx:2 * nx]
        w_refs = refs[2 * nx:2 * nx + ncv]
        p_refs = refs[2 * nx + ncv:2 * nx + ncv + npar]
        o_refs = refs[2 * nx + ncv + npar:2 * nx + ncv + npar + no]
        u_s = refs[2 * nx + ncv + npar + no:2 * nx + ncv + npar + no + ncv]
        win_s = refs[2 * nx + ncv + npar + no + ncv:2 * nx + ncv + npar + no + 2 * ncv]
        stage_s = refs[2 * nx + ncv + npar + no + 2 * ncv:]
        i = pl.program_id(1)
        um = pre(*[r[...].astype(F32) for r in mains])
        uh = pre(*[r[...].astype(F32) for r in halos])
        first = (i > 0).astype(F32)
        for q in range(ncv):
            u_s[q][pl.ds(0, hb), :] = uh[q] * first
            u_s[q][pl.ds(hb, ts), :] = um[q]
        pv = [r[0:1, :] for r in p_refs]

        def chunk(ci, carry):
            r0 = pl.multiple_of(ci * rc, rc)
            cs = []
            for q in range(ncv):
                win_s[q][...] = u_s[q][pl.ds(r0, rc + hb), :]
                acc = jnp.zeros((rc, tc), F32)
                offsets = [hb - (k_taps - 1) + t for t in range(k_taps)]
                for t, ut in enumerate(_tap_windows(win_s[q], offsets, rc, stage_s[q])):
                    acc = acc + w_refs[q][pl.ds(t, 1), :] * ut()
                cs.append(acc)
            res = post(cs, pv)
            for q in range(no):
                o_refs[q][pl.ds(r0, rc), :] = res[q].astype(o_refs[q].dtype)
            return carry

        lax.fori_loop(0, ts // rc, chunk, 0)

    main = lambda i: i
    prev = lambda i: jnp.maximum(i * rpb - 1, 0)
    zero = lambda i: 0
    in_specs = [_cspec(mode, off, ts, tc, main, n_s) for _, off in xins]
    in_specs += [_cspec(mode, off, hb, tc, prev, s // hb) for _, off in xins]
    in_specs += [_cspec(mode, off, kp, tc, zero, 1) for off in woffs]
    in_specs += [_cspec(mode, off, pr, tc, zero, 1) for off in poffs]
    out_specs = [_cspec(mode, 0, ts, tc, main, n_s) for _ in outs]
    oshape = (s, n_cb * tc) if mode == 'col' else (n_cb * s, tc)
    out_shape = [jax.ShapeDtypeStruct(oshape, dt) for dt in outs]
    return pl.pallas_call(
        body, grid=(n_cb, n_s), in_specs=in_specs, out_specs=out_specs, out_shape=out_shape,
        scratch_shapes=[pltpu.VMEM((hb + ts, tc), F32) for _ in range(ncv)]
        + [pltpu.VMEM((hb + rc, tc), F32) for _ in range(ncv)]
        + [pltpu.VMEM(_stage_shape(k_taps, rc, tc), F32) for _ in range(ncv)],
        compiler_params=pltpu.CompilerParams(dimension_semantics=("parallel", "arbitrary")), name=name,
    )(*[a for a, _ in xins], *[a for a, _ in xins], *([w] * ncv), *params)


def _conv_bwd(xins, w, woffs, params, poffs, pre, post, douts, dx_dtype, *, mode, s, k_taps, hb, ts, tc, rc, n_cb, name):
    n_s, nx, ncv, npar, ndo = s // ts, len(xins), len(woffs), len(params), len(douts)
    rpb = ts // hb
    n_hb = s // hb
    kp = TAP_ROWS * ((k_taps + TAP_ROWS - 1) // TAP_ROWS)
    pr = 1 if mode == 'col' else TAP_ROWS

    def body(*refs):
        pos = 0

        def take(n):
            nonlocal pos
            out = refs[pos:pos + n]
            pos += n
            return out

        mains, prevs, nexts = take(nx), take(nx), take(nx)
        d_mains, d_nexts = take(ndo), take(ndo)
        w_refs, p_refs = take(ncv), take(npar)
        dx_refs, dw_refs, dp_refs = take(nx), take(ncv), take(npar)
        u_s, d_s = take(ncv), take(ndo)
        win_s, dcw_s, dwa_s = take(ncv), take(ncv), take(ncv)
        stu_s, std_s = take(ncv), take(ncv)
        i = pl.program_id(1)
        um = pre(*[r[...].astype(F32) for r in mains])
        up = pre(*[r[...].astype(F32) for r in prevs])
        un = pre(*[r[...].astype(F32) for r in nexts])
        first = (i > 0).astype(F32)
        last = (i < n_s - 1).astype(F32)
        for q in range(ncv):
            u_s[q][pl.ds(0, hb), :] = up[q] * first
            u_s[q][pl.ds(hb, ts), :] = um[q]
            u_s[q][pl.ds(hb + ts, hb), :] = un[q]
            dwa_s[q][...] = jnp.zeros((kp * 8, tc), F32)
        for q in range(ndo):
            d_s[q][pl.ds(0, ts), :] = d_mains[q][...].astype(F32)
            d_s[q][pl.ds(ts, hb), :] = d_nexts[q][...].astype(F32) * last
        pv = [r[0:1, :] for r in p_refs]
        fwd_offsets = [hb - (k_taps - 1) + t for t in range(k_taps)]

        def chunk(ci, dpar):
            r0 = pl.multiple_of(ci * rc, rc)
            c_own, c_next, shifted = [], [], []
            for q in range(ncv):
                win_s[q][...] = u_s[q][pl.ds(r0, rc + 2 * hb), :]
                acc = jnp.zeros((rc + hb, tc), F32)
                taps = _tap_windows(win_s[q], fwd_offsets, rc + hb, stu_s[q])
                for t, ut in enumerate(taps):
                    acc = acc + w_refs[q][pl.ds(t, 1), :] * ut()
                c_own.append(acc[:rc])
                c_next.append(acc[rc:])
                shifted.append(taps)
            _, vjp_o = jax.vjp(lambda c, p: tuple(post(c, p)), c_own, pv)
            dc_own, dp_own = vjp_o(tuple(r[pl.ds(r0, rc), :] for r in d_s))
            _, vjp_n = jax.vjp(lambda c: tuple(post(c, pv)), c_next)
            (dc_next,) = vjp_n(tuple(r[pl.ds(r0 + rc, hb), :] for r in d_s))
            dus = []
            for q in range(ncv):
                dcw_s[q][pl.ds(0, rc), :] = dc_own[q]
                dcw_s[q][pl.ds(rc, hb), :] = dc_next[q]
                acc = jnp.zeros((rc, tc), F32)
                bwd_offsets = [k_taps - 1 - t for t in range(k_taps)]
                for t, dct in enumerate(_tap_windows(dcw_s[q], bwd_offsets, rc, std_s[q])):
                    acc = acc + w_refs[q][pl.ds(t, 1), :] * dct()
                    prod = dc_own[q] * shifted[q][t]()[:rc]
                    dwa_s[q][pl.ds(8 * t, 8), :] += jnp.sum(prod.reshape(rc // 8, 8, tc), axis=0)
                dus.append(acc)
            xm = [r[pl.ds(r0, rc), :].astype(F32) for r in mains]
            _, vjp_p = jax.vjp(lambda *xv: tuple(pre(*xv)), *xm)
            dxs = vjp_p(tuple(dus))
            for q in range(nx):
                dx_refs[q][pl.ds(r0, rc), :] = dxs[q].astype(dx_refs[q].dtype)
            return tuple(a + b for a, b in zip(dpar, dp_own))

        dpar = lax.fori_loop(0, ts // rc, chunk, tuple(jnp.zeros((1, tc), F32) for _ in range(npar)))
        for q in range(ncv):
            @pl.when(i == 0)
            def _(q=q):
                dw_refs[q][...] = jnp.zeros((kp, tc), F32)

            for t in range(k_taps):
                dw_refs[q][pl.ds(t, 1), :] += jnp.sum(dwa_s[q][pl.ds(8 * t, 8), :], axis=0, keepdims=True)
        for q in range(npar):
            @pl.when(i == 0)
            def _(q=q):
                dp_refs[q][...] = jnp.zeros((pr, tc), F32)

            dp_refs[q][0:1, :] += dpar[q]

    main = lambda i: i
    prev = lambda i: jnp.maximum(i * rpb - 1, 0)
    nxt = lambda i: jnp.minimum((i + 1) * rpb, n_hb - 1)
    zero = lambda i: 0
    in_specs = [_cspec(mode, off, ts, tc, main, n_s) for _, off in xins]
    in_specs += [_cspec(mode, off, hb, tc, prev, n_hb) for _, off in xins]
    in_specs += [_cspec(mode, off, hb, tc, nxt, n_hb) for _, off in xins]
    in_specs += [_cspec(mode, off, ts, tc, main, n_s) for _, off in douts]
    in_specs += [_cspec(mode, off, hb, tc, nxt, n_hb) for _, off in douts]
    in_specs += [_cspec(mode, off, kp, tc, zero, 1) for off in woffs]
    in_specs += [_cspec(mode, off, pr, tc, zero, 1) for off in poffs]
    out_specs = [_cspec(mode, 0, ts, tc, main, n_s) for _ in xins]
    out_specs += [_cspec(mode, 0, kp, tc, zero, 1) for _ in woffs]
    out_specs += [_cspec(mode, 0, pr, tc, zero, 1) for _ in params]

    def shape(rows):
        return (rows, n_cb * tc) if mode == 'col' else (n_cb * rows, tc)

    out_shape = [jax.ShapeDtypeStruct(shape(s), dx_dtype) for _ in xins]
    out_shape += [jax.ShapeDtypeStruct(shape(kp), F32) for _ in woffs]
    out_shape += [jax.ShapeDtypeStruct(shape(pr), F32) for _ in params]
    xa = [a for a, _ in xins]
    da = [a for a, _ in douts]
    return pl.pallas_call(
        body, grid=(n_cb, n_s), in_specs=in_specs, out_specs=out_specs, out_shape=out_shape,
        scratch_shapes=[pltpu.VMEM((hb + ts + hb, tc), F32) for _ in range(ncv)]
        + [pltpu.VMEM((ts + hb, tc), F32) for _ in range(ndo)]
        + [pltpu.VMEM((rc + 2 * hb, tc), F32) for _ in range(ncv)]
        + [pltpu.VMEM((rc + hb, tc), F32) for _ in range(ncv)]
        + [pltpu.VMEM((kp * 8, tc), F32) for _ in range(ncv)]
        + [pltpu.VMEM(_stage_shape(k_taps, rc + hb, tc), F32) for _ in range(ncv)]
        + [pltpu.VMEM(_stage_shape(k_taps, rc, tc), F32) for _ in range(ncv)],
        compiler_params=pltpu.CompilerParams(dimension_semantics=("parallel", "arbitrary")), name=name,
    )(*xa, *xa, *xa, *da, *da, *([w] * ncv), *params)


def _glu_pre(a, g):
    return [a * jax.nn.sigmoid(g)]


def _ln_silu_post(cs, ps):
    c = cs[0] + ps[0]
    mu = jnp.mean(c, axis=-1, keepdims=True)
    var = jnp.mean(jnp.square(c - mu), axis=-1, keepdims=True)
    y = (c - mu) * lax.rsqrt(var + LN_EPS) * ps[1] + ps[2]
    return [jax.nn.silu(y)]


def _pair_pre(a, b):
    return [a, b]


def _gate_post(cs, ps):
    return [jax.nn.silu(cs[0] + ps[0]) * (cs[1] + ps[1])]


def _conva(s):
    return dict(mode='col', s=s, k_taps=CONV_K, hb=32, ts=512, tc=CONV_WIDTH, rc=64, n_cb=1)


def _gate(s):
    return dict(mode='row', s=s, k_taps=FFN_K, hb=8, ts=512, tc=FF_SHARD, rc=32, n_cb=FF_HALF)


def _rope_tables(s):
    half = QK_ROPE // 2
    inv = ROPE_BASE ** (-jnp.arange(half, dtype=F32) / half)
    ang = jnp.arange(s, dtype=F32)[:, None] * inv[None, :]
    cos, sin = jnp.cos(ang), jnp.sin(ang)
    z = lambda n: jnp.zeros((s, n), F32)
    c = jnp.concatenate([jnp.ones((s, QK_NOPE), F32), cos, cos, z(HEAD_PAD - QK_NOPE - QK_ROPE)], axis=1)
    s1 = jnp.concatenate([z(QK_NOPE), -sin, z(HEAD_PAD - QK_NOPE - half)], axis=1)
    s2 = jnp.concatenate([z(QK_NOPE + half), sin, z(HEAD_PAD - QK_NOPE - QK_ROPE)], axis=1)
    return c, s1, s2


def _rot(t, c, s1, s2):
    half = QK_ROPE // 2
    return t * c + pltpu.roll(t, HEAD_PAD - half, 1) * s1 + pltpu.roll(t, half, 1) * s2


def _rot_t(d, c, s1, s2):
    half = QK_ROPE // 2
    return d * c + pltpu.roll(d * s1, half, 1) + pltpu.roll(d * s2, HEAD_PAD - half, 1)


def _heads(v):
    return [v[:, h * HEAD_PAD:(h + 1) * HEAD_PAD] for h in range(N_HEADS)]


def _rope_fwd(qf, kvf, hmat, tabs, name):
    w = N_HEADS * HEAD_PAD

    def fn(q, k, v, kr, c, s1, s2):
        krr = _rot(kr, c, s1, s2)
        qo = jnp.concatenate([_rot(t, c, s1, s2) for t in _heads(q)], axis=1)
        ko = jnp.concatenate([t + krr for t in _heads(k)], axis=1)
        lane = lax.broadcasted_iota(jnp.int32, v.shape, 1) & (HEAD_PAD - 1)
        return qo, ko, jnp.where(lane == ONES_LANE, 1.0, v)

    rows = [(qf, w, 0), (kvf, w, 0), (kvf, w, 1), (hmat, HEAD_PAD, IN_PAD // HEAD_PAD - 1)]
    rows += [(t, HEAD_PAD, 0) for t in tabs]
    return _rowwise(fn, rows, [], [(w, BF16)] * 3, [], ts=ROW_TS, name=name)


def _rope_bwd(dq, dk, dv, tabs, name):
    w = N_HEADS * HEAD_PAD

    def fn(dqv, dkv, dvv, c, s1, s2):
        dqo = jnp.concatenate([_rot_t(t, c, s1, s2) for t in _heads(dqv)], axis=1)
        ksum = functools.reduce(lambda a, b: a + b, _heads(dkv))
        return dqo, jnp.concatenate([dkv, dvv], axis=1), _rot_t(ksum, c, s1, s2)

    rows = [(dq, w, 0), (dk, w, 0), (dv, w, 0)] + [(t, HEAD_PAD, 0) for t in tabs]
    return _rowwise(fn, rows, [], [(w, BF16), (2 * w, BF16), (HEAD_PAD, F32)], [], ts=ROW_TS, name=name)


ATT_Q = 1024
ATT_SUB = 512
ATT_KV = 1024
ATT_SCALE = (QK_NOPE + QK_ROPE) ** -0.5
LOG2E = 1.4426950408889634
ATT_C2 = ATT_SCALE * LOG2E
ONES_LANE = V_DIM


def _nt(a, b):
    return lax.dot_general(a, b, (((1,), (1,)), ((), ())), preferred_element_type=F32)


def _lanes(x, w):
    return x if w == HEAD_PAD else jnp.tile(x, (1, w // HEAD_PAD))


def _tri(w, transposed):
    r = lax.broadcasted_iota(jnp.int32, (w, w), 0)
    c = lax.broadcasted_iota(jnp.int32, (w, w), 1)
    return (r <= c) if transposed else (c <= r)


def _attn_fwd(q, k, v, name):
    s = q.shape[0]
    tq, kvc = min(ATT_Q, s), min(ATT_KV, s)
    nsub, per = tq // ATT_SUB, tq // kvc

    def body(q_ref, k_ref, v_ref, o_ref, lse_ref, m_s, acc_s):
        i = pl.program_id(1)
        m_s[...] = jnp.full((tq, HEAD_PAD), -jnp.inf, F32)
        acc_s[...] = jnp.zeros((tq, HEAD_PAD), F32)

        def update(r0, n, kb, vb, diag):
            rows = pl.ds(r0, n)
            w = kb.shape[0]
            sc = _nt(q_ref[rows, :], kb)
            if diag:
                sc = jnp.where(_tri(w, False), sc, -jnp.inf)
            m_prev = m_s[rows, :]
            m_next = jnp.maximum(m_prev, jnp.max(sc, axis=1, keepdims=True))
            p = jnp.exp2((sc - _lanes(m_next, w)) * ATT_C2)
            alpha = jnp.exp2((m_prev - m_next) * ATT_C2)
            acc_s[rows, :] = alpha * acc_s[rows, :] + jnp.dot(p.astype(BF16), vb, preferred_element_type=F32)
            m_s[rows, :] = m_next

        def below(j, carry):
            at = pl.ds(pl.multiple_of(j * kvc, kvc), kvc)
            update(0, tq, k_ref[at, :], v_ref[at, :], False)
            return carry

        lax.fori_loop(0, i * per, below, 0)
        for r in range(nsub):
            for c in range(r + 1):
                at = pl.ds(pl.multiple_of(i * tq + c * ATT_SUB, ATT_SUB), ATT_SUB)
                update(r * ATT_SUB, ATT_SUB, k_ref[at, :], v_ref[at, :], c == r)
        l = acc_s[:, ONES_LANE:ONES_LANE + 1]
        o_ref[...] = (acc_s[...] / l).astype(BF16)
        lse_ref[...] = m_s[...] * ATT_SCALE + jnp.log(l)

    q_spec = pl.BlockSpec((tq, HEAD_PAD), lambda h, i: (i, h))
    kv_spec = pl.BlockSpec((s, HEAD_PAD), lambda h, i: (0, h))
    return pl.pallas_call(
        body, grid=(N_HEADS, s // tq), in_specs=[q_spec, kv_spec, kv_spec], out_specs=[q_spec, q_spec],
        out_shape=[jax.ShapeDtypeStruct(q.shape, BF16), jax.ShapeDtypeStruct(q.shape, F32)],
        scratch_shapes=[pltpu.VMEM((tq, HEAD_PAD), F32)] * 2,
        compiler_params=pltpu.CompilerParams(dimension_semantics=("parallel", "arbitrary")), name=name,
    )(q, k, v)


def _attn_rows(cat, dcat, lse, name):
    s = lse.shape[0]
    tq, kvc = min(ATT_Q, s), min(ATT_KV, s)
    per = tq // kvc
    ob = CONV_WIDTH // HEAD_PAD

    def body(o_ref, do_ref, lse_ref, lser_ref, dltr_ref):
        dl = jnp.broadcast_to(jnp.sum(do_ref[...] * o_ref[...].astype(F32), axis=1, keepdims=True), (tq, HEAD_PAD))
        l2 = lse_ref[...] * LOG2E
        for c in range(per):
            lser_ref[c] = jnp.transpose(l2[c * kvc:(c + 1) * kvc])[0:8, :]
            dltr_ref[c] = jnp.transpose(dl[c * kvc:(c + 1) * kvc])[0:8, :]

    q_spec = pl.BlockSpec((tq, HEAD_PAD), lambda h, i: (i, h))
    o_spec = pl.BlockSpec((tq, HEAD_PAD), lambda h, i: (i, ob + h))
    row_spec = pl.BlockSpec((None, per, 8, kvc), lambda h, i: (h, i, 0, 0))
    rows = jax.ShapeDtypeStruct((N_HEADS, s // kvc, 8, kvc), F32)
    return pl.pallas_call(
        body, grid=(N_HEADS, s // tq), in_specs=[o_spec, o_spec, q_spec], out_specs=[row_spec, row_spec],
        out_shape=[rows, rows],
        compiler_params=pltpu.CompilerParams(dimension_semantics=("parallel", "parallel")), name=name,
    )(cat, dcat, lse)


def _attn_bwd(q, k, v, dcat, lse_r, dlt_r, name):
    s = q.shape[0]
    tk, kvc = min(ATT_Q, s), min(ATT_KV, s)
    nsub, per, n_chunks = tk // ATT_SUB, tk // kvc, s // kvc
    n_j = s // tk
    ob = CONV_WIDTH // HEAD_PAD

    def body(k_ref, v_ref, q_ref, do_ref, lse_ref, dl_ref, dk_ref, dv_ref, dq_ref, dk_s, dv_s):
        j = pl.program_id(1)
        dk_s[...] = jnp.zeros((tk, HEAD_PAD), F32)
        dv_s[...] = jnp.zeros((tk, HEAD_PAD), F32)

        @pl.when(j == 0)
        def _():
            dq_ref[...] = jnp.zeros((s, HEAD_PAD), F32)

        def update(r0, n, at, lrow, drow, diag):
            rows = pl.ds(r0, n)
            qb, dob = q_ref[at, :], do_ref[at, :].astype(BF16)
            sc = _nt(k_ref[rows, :], qb)
            if diag:
                sc = jnp.where(_tri(qb.shape[0], True), sc, -jnp.inf)
            p = jnp.exp2(sc * ATT_C2 - lrow)
            dp = _nt(v_ref[rows, :], dob)
            ds = (p * (dp - drow)).astype(BF16)
            dv_s[rows, :] += jnp.dot(p.astype(BF16), dob, preferred_element_type=F32)
            dk_s[rows, :] += jnp.dot(ds, qb, preferred_element_type=F32)
            dq_ref[at, :] += lax.dot_general(ds, k_ref[rows, :], (((0,), (0,)), ((), ())), preferred_element_type=F32)

        def above(ic, carry):
            at = pl.ds(pl.multiple_of(ic * kvc, kvc), kvc)
            update(0, tk, at, lse_ref[ic, 0:1, :], dl_ref[ic, 0:1, :], False)
            return carry

        lax.fori_loop((j + 1) * per, n_chunks, above, 0)
        for r in range(nsub):
            for c in range(r, nsub):
                at = pl.ds(pl.multiple_of(j * tk + c * ATT_SUB, ATT_SUB), ATT_SUB)
                ic = j * per + (c * ATT_SUB) // kvc
                lo = (c * ATT_SUB) % kvc
                update(r * ATT_SUB, ATT_SUB, at, lse_ref[ic, 0:1, lo:lo + ATT_SUB], dl_ref[ic, 0:1, lo:lo + ATT_SUB],
                       c == r)
        dk_ref[...] = dk_s[...] * ATT_SCALE
        dv_ref[...] = dv_s[...]

        @pl.when(j == n_j - 1)
        def _():
            dq_ref[...] = dq_ref[...] * ATT_SCALE

    kv_spec = pl.BlockSpec((tk, HEAD_PAD), lambda h, j: (j, h))
    q_spec = pl.BlockSpec((s, HEAD_PAD), lambda h, j: (0, h))
    do_spec = pl.BlockSpec((s, HEAD_PAD), lambda h, j: (0, ob + h))
    row_spec = pl.BlockSpec((None, n_chunks, 8, kvc), lambda h, j: (h, 0, 0, 0))
    full = jax.ShapeDtypeStruct(q.shape, F32)
    dk, dv, dq = pl.pallas_call(
        body, grid=(N_HEADS, n_j), in_specs=[kv_spec, kv_spec, q_spec, do_spec, row_spec, row_spec],
        out_specs=[kv_spec, kv_spec, q_spec], out_shape=[full, full, full],
        scratch_shapes=[pltpu.VMEM((tk, HEAD_PAD), F32)] * 2,
        compiler_params=pltpu.CompilerParams(dimension_semantics=("parallel", "arbitrary")), name=name,
    )(k, v, q, dcat, lse_r, dlt_r)
    return dq, dk, dv


SCAN_T = 128
SCAN_C = 512


def _scan(b, lam, *, reverse, xs=None, name):
    s = b.shape[0]
    t = min(SCAN_T, s)
    n_t, n_c = s // t, SSM_CH // SCAN_C
    with_dlam = xs is not None

    def shift(a, d, row):
        if d >= 8:
            z = jnp.zeros((d, SCAN_C), F32)
            return jnp.concatenate([a[d:], z], axis=0) if reverse else jnp.concatenate([z, a[:t - d]], axis=0)
        if reverse:
            return jnp.where(row < t - d, pltpu.roll(a, t - d, 0), 0.0)
        return jnp.where(row >= d, pltpu.roll(a, d, 0), 0.0)

    def body(*refs):
        if with_dlam:
            b_ref, lam_ref, x_ref, o_ref, dl_ref, c_s = refs
        else:
            b_ref, lam_ref, o_ref, c_s = refs
        k = pl.program_id(0)

        @pl.when(k == 0)
        def _():
            c_s[...] = jnp.zeros((1, 2 * SSM_CH), F32)
            if with_dlam:
                dl_ref[...] = jnp.zeros((1, 2 * SSM_CH), F32)

        row = lax.broadcasted_iota(jnp.int32, (t, SCAN_C), 0)
        edge = (row == t - 1) if reverse else (row == 0)
        for ch in range(n_c):
            re = pl.ds(ch * SCAN_C, SCAN_C)
            im = pl.ds(SSM_CH + ch * SCAN_C, SCAN_C)
            lr = lam_ref[:, re]
            li = -lam_ref[:, im] if reverse else lam_ref[:, im]
            cr, ci = c_s[:, re], c_s[:, im]
            ar = b_ref[:, re] + jnp.where(edge, lr * cr - li * ci, 0.0)
            ai = b_ref[:, im] + jnp.where(edge, lr * ci + li * cr, 0.0)
            d = 1
            while d < t:
                sr, si = shift(ar, d, row), shift(ai, d, row)
                ar, ai = ar + lr * sr - li * si, ai + lr * si + li * sr
                lr, li = lr * lr - li * li, 2.0 * lr * li
                d *= 2
            o_ref[:, re] = ar.astype(o_ref.dtype)
            o_ref[:, im] = ai.astype(o_ref.dtype)
            if with_dlam:
                gr = jnp.where(edge, cr, shift(ar, 1, row))
                gi = jnp.where(edge, ci, shift(ai, 1, row))
                xr, xi = x_ref[:, re].astype(F32), x_ref[:, im].astype(F32)
                dl_ref[:, re] += jnp.sum(xr * gr + xi * gi, axis=0, keepdims=True)
                dl_ref[:, im] += jnp.sum(xr * gi - xi * gr, axis=0, keepdims=True)
            last = 0 if reverse else t - 1
            c_s[:, re] = ar[last:last + 1, :]
            c_s[:, im] = ai[last:last + 1, :]

    tm = (lambda k: (n_t - 1 - k, 0)) if reverse else (lambda k: (k, 0))
    blk = pl.BlockSpec((t, 2 * SSM_CH), tm)
    vec = pl.BlockSpec((1, 2 * SSM_CH), lambda k: (0, 0))
    in_specs, args = [blk, vec], [b, lam]
    out_specs, out_shape = [blk], [jax.ShapeDtypeStruct((s, 2 * SSM_CH), BF16)]
    if with_dlam:
        in_specs.append(blk)
        args.append(xs)
        out_specs.append(vec)
        out_shape.append(jax.ShapeDtypeStruct((1, 2 * SSM_CH), F32))
    return pl.pallas_call(
        body, grid=(n_t,), in_specs=in_specs, out_specs=out_specs, out_shape=out_shape,
        scratch_shapes=[pltpu.VMEM((1, 2 * SSM_CH), F32)],
        compiler_params=pltpu.CompilerParams(dimension_semantics=("arbitrary",)), name=name,
    )(*args)


def _s5_disc(log_dt, a_re, a_im, b_re, b_im):
    dt = jnp.exp(log_dt)[:, None]
    mag = jnp.exp(a_re * dt)
    lb_re, lb_im = mag * jnp.cos(a_im * dt), mag * jnp.sin(a_im * dt)
    den = a_re * a_re + a_im * a_im
    nr, ni = lb_re - 1.0, lb_im
    f_re = (nr * a_re + ni * a_im) / den
    f_im = (ni * a_re - nr * a_im) / den
    bb_re = f_re[..., None] * b_re - f_im[..., None] * b_im
    bb_im = f_re[..., None] * b_im + f_im[..., None] * b_re
    return lb_re, lb_im, bb_re, bb_im


def _bd(a):
    g, i, j = a.shape
    eye = jnp.eye(g, dtype=a.dtype)
    return (a[:, :, None, :] * eye[:, None, :, None]).reshape(g * i, g * j)


S5_TILE_GROUPS = HEAD_PAD // SSM_GROUP


def _s5_sparse(s):
    nb = SSM_GROUPS // S5_TILE_GROUPS
    cw, sw = S5_TILE_GROUPS * SSM_GROUP, S5_TILE_GROUPS * SSM_STATE
    tm = min(1024, s)
    return dict(
        expand=dict(tiles=(tm, sw, cw), out=(s, 2 * SSM_CH), nk=1, k_of=lambda i, j, k: j % nb),
        reduce=dict(tiles=(tm, cw, sw), out=(s, SSM_WIDTH), nk=2, k_of=lambda i, j, k: j + nb * k),
        wide_t=dict(tiles=(sw, cw, tm), out=(2 * SSM_CH, cw), nk=s // tm, b_n=lambda i, j: i % nb),
        narrow_t=dict(tiles=(cw, sw, tm), out=(cw, 2 * SSM_CH), nk=s // tm, a_m=lambda i, j: j % nb))


def _exchange(arrs, modes, name):
    n = len(arrs)
    shapes = [a.shape if md == 'scatter' else (N_DEV,) + a.shape for a, md in zip(arrs, modes)]

    def body(*refs):
        srcs, outs = refs[:n], refs[n:2 * n]
        send_sems, recv_sems = refs[2 * n:]
        x, y, c = lax.axis_index("x"), lax.axis_index("y"), lax.axis_index("c")
        me = 4 * x + 2 * y + c
        copies = []
        for r, pos, peer in _peers(x, y, c):
            for q in range(n):
                copies.append(pltpu.make_async_remote_copy(
                    src_ref=srcs[q].at[peer] if modes[q] == 'scatter' else srcs[q], dst_ref=outs[q].at[me],
                    send_sem=send_sems.at[(r - 1) * n + q], recv_sem=recv_sems.at[(r - 1) * n + q], device_id=pos,
                    device_id_type=pl.DeviceIdType.MESH))
        for cp in copies:
            cp.start()
        for cp in copies:
            cp.wait_recv()
        for cp in copies:
            cp.wait_send()

    any_spec = pl.BlockSpec(memory_space=pl.ANY)
    outs = pl.pallas_call(
        body, out_shape=[jax.ShapeDtypeStruct(sh, a.dtype) for sh, a in zip(shapes, arrs)],
        in_specs=[any_spec] * n, out_specs=[any_spec] * n,
        scratch_shapes=[pltpu.SemaphoreType.DMA(((N_DEV - 1) * n,)), pltpu.SemaphoreType.DMA(((N_DEV - 1) * n,))],
        compiler_params=pltpu.CompilerParams(has_side_effects=True), name=name,
    )(*arrs)
    return _own_slots(outs, arrs, modes)


def _peers(x, y, c):
    out = []
    for r in range(1, N_DEV):
        px, py, pc = x ^ (r >> 2), y ^ ((r >> 1) & 1), c ^ (r & 1)
        out.append((r, (px, py, pc), 4 * px + 2 * py + pc))
    return out


def _own_slots(lands, srcs, modes):
    me = 4 * lax.axis_index("x") + 2 * lax.axis_index("y") + lax.axis_index("c")
    out = []
    for land, src, md in zip(lands, srcs, modes):
        own = lax.dynamic_index_in_dim(src, me, 0, keepdims=False) if md == 'scatter' else src
        out.append(lax.dynamic_update_index_in_dim(land, own, me, 0))
    return out


def _exchange_start(arrs, modes, carry, name):
    n = len(arrs)
    shapes = [a.shape if md == 'scatter' else (N_DEV,) + a.shape for a, md in zip(arrs, modes)]
    lands = [lax.empty(sh, a.dtype) for sh, a in zip(shapes, arrs)]

    def body(*refs):
        srcs, zones = refs[:n], refs[n:2 * n]
        send_sems, recv_sems = refs[2 * n + 1], refs[2 * n + 2]
        x, y, c = lax.axis_index("x"), lax.axis_index("y"), lax.axis_index("c")
        me = 4 * x + 2 * y + c
        for r, pos, peer in _peers(x, y, c):
            for q in range(n):
                src = srcs[q].at[peer] if modes[q] == 'scatter' else srcs[q]
                pltpu.make_async_remote_copy(
                    src_ref=src, dst_ref=zones[q].at[me], send_sem=send_sems.at[(r - 1) * n + q],
                    recv_sem=recv_sems.at[(r - 1) * n + q], device_id=pos, device_id_type=pl.DeviceIdType.MESH).start()

    hbm = pl.BlockSpec(memory_space=pltpu.HBM)
    sem = pl.BlockSpec(memory_space=pltpu.SEMAPHORE)
    thru = arrs + lands + [carry]
    sems = pltpu.SemaphoreType.DMA(((N_DEV - 1) * n,))
    outs = pl.pallas_call(
        body, name=name, out_shape=(sems, sems, *[pltpu.HBM(a.shape, a.dtype) for a in thru]),
        in_specs=[hbm] * len(thru), out_specs=(sem, sem, *[hbm] * len(thru)),
        input_output_aliases={q: 2 + q for q in range(len(thru))},
        compiler_params=pltpu.CompilerParams(has_side_effects=pltpu.SideEffectType.DATAFLOW_SIDE_EFFECTING),
    )(*[pltpu.with_memory_space_constraint(a, pltpu.HBM) for a in thru])
    return dict(send=outs[0], recv=outs[1], srcs=list(outs[2:2 + n]), lands=list(outs[2 + n:2 + 2 * n]),
                modes=modes), outs[-1]


def _exchange_wait(flight, after, name):
    n = len(flight['srcs'])
    modes = flight['modes']

    def body(*refs):
        srcs, zones = refs[:n], refs[n:2 * n]
        send_sems, recv_sems = refs[2 * n], refs[2 * n + 1]
        x, y, c = lax.axis_index("x"), lax.axis_index("y"), lax.axis_index("c")
        me = 4 * x + 2 * y + c
        for r, pos, peer in _peers(x, y, c):
            for q in range(n):
                src = srcs[q].at[peer] if modes[q] == 'scatter' else srcs[q]
                cp = pltpu.make_async_remote_copy(
                    src_ref=src, dst_ref=zones[q].at[me], send_sem=send_sems.at[(r - 1) * n + q],
                    recv_sem=recv_sems.at[(r - 1) * n + q], device_id=pos, device_id_type=pl.DeviceIdType.MESH)
                cp.wait_send()
                cp.wait_recv()

    hbm = pl.BlockSpec(memory_space=pltpu.HBM)
    sem = pl.BlockSpec(memory_space=pltpu.SEMAPHORE)
    bufs = flight['srcs'] + flight['lands']
    outs = pl.pallas_call(
        body, name=name, out_shape=tuple(pltpu.HBM(a.shape, a.dtype) for a in bufs),
        in_specs=[hbm] * (2 * n) + [sem, sem, pl.BlockSpec(memory_space=pl.ANY)], out_specs=tuple([hbm] * (2 * n)),
        input_output_aliases={q: q for q in range(2 * n)},
        compiler_params=pltpu.CompilerParams(has_side_effects=pltpu.SideEffectType.DATAFLOW_SIDE_EFFECTING),
    )(*bufs, flight['send'], flight['recv'], after)
    return _own_slots(outs[n:], outs[:n], modes)


def _adamw(parts, w, m, v, name):
    r, c = w.shape
    tr = _tile(r, (256, 128))

    def body(p_ref, w_ref, m_ref, v_ref, g_ref, d_ref, nm_ref, nv_ref):
        g = p_ref[0].astype(F32)
        for d in range(1, N_DEV):
            g = g + p_ref[d].astype(F32)
        m2 = ADAM_B1 * m_ref[...] + (1.0 - ADAM_B1) * g
        v2 = ADAM_B2 * v_ref[...] + (1.0 - ADAM_B2) * jnp.square(g)
        m_hat = m2 / (1.0 - ADAM_B1 ** ADAM_STEP)
        v_hat = v2 / (1.0 - ADAM_B2 ** ADAM_STEP)
        g_ref[...] = g
        d_ref[...] = -ADAM_LR * (m_hat / (jnp.sqrt(v_hat) + ADAM_EPS) + ADAM_WD * w_ref[...])
        nm_ref[...] = m2
        nv_ref[...] = v2

    spec = pl.BlockSpec((tr, c), lambda i: (i, 0))
    return pl.pallas_call(
        body, grid=(r // tr,), in_specs=[pl.BlockSpec((N_DEV, tr, c), lambda i: (0, i, 0)), spec, spec, spec],
        out_specs=[spec] * 4, out_shape=[jax.ShapeDtypeStruct((r, c), F32)] * 4,
        compiler_params=pltpu.CompilerParams(dimension_semantics=("parallel",)), name=name,
    )(parts, w, m, v)


FLAT_W = 512
FLAT_ROWS = 256


def _flat(arrs, rows=FLAT_ROWS):
    v = jnp.concatenate([a.reshape(-1) for a in arrs])
    return jnp.pad(v, (0, (-v.shape[0]) % (rows * FLAT_W))).reshape(-1, FLAT_W)


def _unflat(flat, shapes):
    v = flat.reshape(-1)
    out, off = [], 0
    for sh in shapes:
        n = 1
        for d in sh:
            n *= d
        out.append(v[off:off + n].reshape(sh))
        off += n
    return out


def _full(name, stacked):
    if SHARDED[name] == 0:
        return stacked.reshape((-1,) + stacked.shape[2:])
    return jnp.transpose(stacked, (1, 0, 2)).reshape(stacked.shape[1], -1)


def _shards(name, full):
    if SHARDED[name] == 0:
        return full.reshape((N_DEV, -1) + full.shape[1:])
    r, c = full.shape
    return jnp.transpose(full.reshape(r, N_DEV, c // N_DEV), (1, 0, 2))


def _prep_weights(p):
    q = {}
    if 'l0_w_in' in p:
        w_in = p['l0_w_in']
        z = lambda n: jnp.zeros((D_MODEL, n), w_in.dtype)
        q['w_in'] = jnp.concatenate([w_in[:, :IN_EVEN - QK_ROPE], z(KR_LANE), w_in[:, IN_EVEN - QK_ROPE:],
                                     z(HEAD_PAD - KR_LANE - QK_ROPE)], axis=1)
    dqk = QK_NOPE + QK_ROPE
    if 'l0_w_uq' in p:
        q['w_uq'] = jnp.pad(p['l0_w_uq'].reshape(Q_RANK, N_HEADS, dqk), ((0, 0), (0, 0), (0, HEAD_PAD - dqk))
                            ).reshape(Q_RANK, N_HEADS * HEAD_PAD)
    if 'l0_w_ukv' in p:
        ukv = p['l0_w_ukv'].reshape(KV_RANK, N_HEADS, 2, QK_NOPE)
        padh = lambda a: jnp.pad(a, ((0, 0), (0, 0), (0, HEAD_PAD - QK_NOPE))).reshape(KV_RANK, N_HEADS * HEAD_PAD)
        q['w_ukv'] = jnp.concatenate([padh(ukv[:, :, 0]), padh(ukv[:, :, 1])], axis=1)
    if 'l0_w_out' in p:
        wo = p['l0_w_out']
        wo_a = jnp.pad(wo[CONV_WIDTH:].reshape(N_HEADS, V_DIM, D_MODEL), ((0, 0), (0, HEAD_PAD - V_DIM), (0, 0)))
        q['w_out'] = jnp.concatenate([wo[:CONV_WIDTH], wo_a.reshape(N_HEADS * HEAD_PAD, D_MODEL)], axis=0)
    return q


def _unprep_grads(g):
    out = {}
    if 'w_in' in g:
        d = g['w_in']
        out['l0_w_in'] = jnp.concatenate([d[:, :IN_EVEN - QK_ROPE],
                                          d[:, IN_EVEN - QK_ROPE + KR_LANE:IN_EVEN + KR_LANE]], axis=1)
    dqk = QK_NOPE + QK_ROPE
    if 'w_uq' in g:
        out['l0_w_uq'] = g['w_uq'].reshape(Q_RANK, N_HEADS, HEAD_PAD)[:, :, :dqk].reshape(Q_RANK, N_HEADS * dqk)
    if 'w_ukv' in g:
        d = g['w_ukv'].reshape(KV_RANK, 2, N_HEADS, HEAD_PAD)[:, :, :, :QK_NOPE]
        out['l0_w_ukv'] = jnp.transpose(d, (0, 2, 1, 3)).reshape(KV_RANK, N_HEADS * 2 * QK_NOPE)
    if 'w_out' in g:
        d = g['w_out']
        da = d[CONV_WIDTH:].reshape(N_HEADS, HEAD_PAD, D_MODEL)[:, :V_DIM].reshape(N_HEADS * V_DIM, D_MODEL)
        out['l0_w_out'] = jnp.concatenate([d[:CONV_WIDTH], da], axis=0)
    return out


def _pad_rows(w, rows):
    return jnp.pad(w, [(0, 0)] * (w.ndim - 2) + [(0, rows - w.shape[-2]), (0, 0)])


def _ffn_fwd(x, rep, got, pre, tag):
    s = x.shape[0]
    xn = _rms_fwd(x, rep[pre + 'ffn_norm'], f"{tag}_ffn_norm")
    w_up = got[pre + 'w_up']
    hu = _mm(xn, w_up, gb=_same, go=_same, groups=N_DEV, name=f"{tag}_ffn_up").reshape(N_DEV * s, FF_SHARD)
    taps = _pad_rows(got[pre + 'ffn_conv_w'], TAP_ROWS).reshape(N_DEV * TAP_ROWS, FF_SHARD)
    bias = _pad_rows(rep[pre + 'ffn_conv_b'].reshape(N_DEV, 1, FF_SHARD), TAP_ROWS).reshape(N_DEV * TAP_ROWS, FF_SHARD)
    (act,) = _conv_fwd([(hu, 0), (hu, FF_HALF)], taps, [0, FF_HALF], [bias, bias], [0, FF_HALF], _pair_pre,
                       _gate_post, [BF16], name=f"{tag}_ffn_gate", **_gate(s))
    act = act.reshape(FF_HALF, s, FF_SHARD)
    w_down = got[pre + 'w_down'].reshape(FF_HALF, FF_SHARD, D_MODEL)
    y = _mm(act, w_down, ga=_same, gb=_same, groups=FF_HALF, res=x, name=f"{tag}_ffn_down")
    return y, (x, xn, hu, act, taps, bias, w_up, w_down)


def _ffn_bwd(dy, saved, rep, pre, tag, grads, gsh):
    x, xn, hu, act, taps, bias, w_up, w_down = saved
    s = x.shape[0]
    dact = _mm(dy, w_down, tb=True, gb=_same, go=_same, groups=FF_HALF, name=f"{tag}_ffn_down_dx")
    gsh[pre + 'w_down'] = _mm(act, dy, ta=True, ga=_same, go=_same, groups=FF_HALF, out_dtype=BF16,
                              name=f"{tag}_ffn_down_dw").reshape(N_DEV, FF_SHARD // 2, D_MODEL)
    dha, dhb, dwa, dwb, dba, dbb = _conv_bwd(
        [(hu, 0), (hu, FF_HALF)], taps, [0, FF_HALF], [bias, bias], [0, FF_HALF], _pair_pre, _gate_post,
        [(dact.reshape(FF_HALF * s, FF_SHARD), 0)], BF16, name=f"{tag}_ffn_gate_bwd", **_gate(s))
    dha, dhb = dha.reshape(FF_HALF, s, FF_SHARD), dhb.reshape(FF_HALF, s, FF_SHARD)
    dtaps = jnp.concatenate([dwa, dwb], axis=0).reshape(N_DEV, TAP_ROWS, FF_SHARD)
    gsh[pre + 'ffn_conv_w'] = dtaps[:, :FFN_K].astype(BF16)
    grads[pre + 'ffn_conv_b'] = jnp.concatenate([dba, dbb], axis=0).reshape(N_DEV, TAP_ROWS, FF_SHARD)[:, 0].reshape(-1)
    upper = lambda g: g + FF_HALF
    dxn = _mm(dha, w_up, tb=True, ga=_same, gb=_same, groups=FF_HALF, name=f"{tag}_ffn_up_dx_a")
    dxn = _mm(dhb, w_up, tb=True, ga=_same, gb=upper, groups=FF_HALF, res=dxn, name=f"{tag}_ffn_up_dx_b")
    dwu = [_mm(xn, dh, ta=True, gb=_same, go=_same, groups=FF_HALF, out_dtype=BF16, name=f"{tag}_ffn_up_dw_{t}")
           for t, dh in (("a", dha), ("b", dhb))]
    gsh[pre + 'w_up'] = jnp.concatenate(dwu, axis=0)
    dx, dg = _rms_bwd(x, rep[pre + 'ffn_norm'], dxn, dy, f"{tag}_ffn_norm_bwd")
    grads[pre + 'ffn_norm'] = dg.reshape(-1)
    return dx


def _mla_fwd(x, rep, q, wait_mixer, tabs):
    s = x.shape[0]
    xn = _rms_fwd(x, rep['l0_mix_norm'], "l0_mix_norm")
    hmat = _mm(xn, q['w_in'], name="l0_in")
    got = wait_mixer(hmat)
    q.update(_prep_weights({n: _full(n, got[n]) for n in ('l0_w_uq', 'l0_w_ukv', 'l0_w_out')}))
    wt = _pad_rows(_full('l0_conv_w', got['l0_conv_w']), 4 * TAP_ROWS)
    cpar = [rep['l0_conv_b'].reshape(1, -1), rep['l0_conv_ln_g'].reshape(1, -1), rep['l0_conv_ln_b'].reshape(1, -1)]
    (u,) = _conv_fwd([(hmat, 0), (hmat, 1)], wt, [0], cpar, [0, 0, 0], _glu_pre, _ln_silu_post, [BF16],
                     name="l0_conv", **_conva(s))
    qn, kvn = rep['l0_q_norm'].reshape(1, -1), rep['l0_kv_norm'].reshape(1, -1)
    cqn, ckvn = _rowwise(lambda a, b, ga, gb: (_rms(a, ga), _rms(b, gb)),
                         [(hmat, Q_RANK, 2 * CONV_WIDTH // Q_RANK), (hmat, KV_RANK, (2 * CONV_WIDTH + Q_RANK) // KV_RANK)],
                         [qn, kvn], [(Q_RANK, BF16), (KV_RANK, BF16)], [], ts=ROW_TS, name="l0_latent_norm")
    qf = _mm(cqn, q['w_uq'], name="l0_uq")
    kvf = _mm(ckvn, q['w_ukv'], name="l0_ukv")
    q_rot, k_full, v = _rope_fwd(qf, kvf, hmat, tabs, "l0_rope")
    o, lse = _attn_fwd(q_rot, k_full, v, "l0_attn")
    cat = jnp.concatenate([u, o], axis=1)
    y = _mm(cat, q['w_out'], res=x, name="l0_out")
    return y, (x, xn, hmat, wt, cpar, qn, kvn, cqn, ckvn, q_rot, k_full, v, lse, cat)


def _mla_bwd(dy, saved, rep, q, tabs, grads, gq, send_grads):
    x, xn, hmat, wt, cpar, qn, kvn, cqn, ckvn, q_rot, k_full, v, lse, cat = saved
    s = x.shape[0]
    dcat = _mm(dy, q['w_out'], tb=True, name="l0_out_dx")
    gq['w_out'] = _mm(cat, dy, ta=True, name="l0_out_dw")
    lse_r, dlt_r = _attn_rows(cat, dcat, lse, "l0_attn_rows")
    dq, dk, dv = _attn_bwd(q_rot, k_full, v, dcat, lse_r, dlt_r, "l0_attn_bwd")
    dqf, dkvf, dkr = _rope_bwd(dq, dk, dv, tabs, "l0_rope_bwd")
    dcqn = _mm(dqf, q['w_uq'], tb=True, name="l0_uq_dx")
    gq['w_uq'] = _mm(cqn, dqf, ta=True, name="l0_uq_dw")
    dckvn = _mm(dkvf, q['w_ukv'], tb=True, name="l0_ukv_dx")
    gq['w_ukv'] = _mm(ckvn, dkvf, ta=True, name="l0_ukv_dw")
    early = _unprep_grads({n: gq[n] for n in ('w_out', 'w_uq', 'w_ukv')})
    dckvn = send_grads(GRADS_MIXER, {n: _shards(n, g).astype(BF16) for n, g in early.items()}, dckvn)

    def lat_bwd(a, b, da, db, ga, gb):
        _, vjp = jax.vjp(lambda a_, b_, ga_, gb_: (_rms(a_, ga_), _rms(b_, gb_)), a, b, ga, gb)
        return vjp((da, db))

    dcq, dckv, dqn, dkvn = _rowwise(
        lat_bwd, [(hmat, Q_RANK, 2 * CONV_WIDTH // Q_RANK), (hmat, KV_RANK, (2 * CONV_WIDTH + Q_RANK) // KV_RANK),
                  (dcqn, Q_RANK, 0), (dckvn, KV_RANK, 0)],
        [qn, kvn], [(Q_RANK, F32), (KV_RANK, F32)], [(1, Q_RANK), (1, KV_RANK)], ts=ROW_TS, name="l0_latent_norm_bwd")
    grads['l0_q_norm'], grads['l0_kv_norm'] = dqn.reshape(-1), dkvn.reshape(-1)
    da, dg, dwt, dcb, dlg, dlb = _conv_bwd(
        [(hmat, 0), (hmat, 1)], wt, [0], cpar, [0, 0, 0], _glu_pre, _ln_silu_post, [(dcat, 0)], F32,
        name="l0_conv_bwd", **_conva(s))
    gq['conv_w'] = dwt[:CONV_K]
    grads['l0_conv_b'], grads['l0_conv_ln_g'], grads['l0_conv_ln_b'] = dcb.reshape(-1), dlg.reshape(-1), dlb.reshape(-1)
    dh = jnp.concatenate([da, dg, dcq, dckv, dkr], axis=1)
    dxn = _mm(dh, q['w_in'], tb=True, name="l0_in_dx")
    gq['w_in'] = _mm(xn, dh, ta=True, name="l0_in_dw")
    dx, dgn = _rms_bwd(x, rep['l0_mix_norm'], dxn, dy, "l0_mix_norm_bwd")
    grads['l0_mix_norm'] = dgn.reshape(-1)
    return dx


def _gelu_skip(yc, u, d):
    return jax.nn.gelu(yc + d * u)


def _glu_out(z1, z2, b1, b2, x):
    return x + (z1 + b1) * jax.nn.sigmoid(z2 + b2)


def _s5_fwd(x, rep, w_in, w_glu):
    xn = _rms_fwd(x, rep['l1_mix_norm'], "l1_mix_norm")
    u = _mm(xn, w_in, name="l1_in")
    lb_re, lb_im, bb_re, bb_im = _s5_disc(rep['l1_log_dt'], rep['l1_a_re'], rep['l1_a_im'], rep['l1_b_re'],
                                          rep['l1_b_im'])
    lam = jnp.concatenate([lb_re.reshape(1, -1), lb_im.reshape(1, -1)], axis=1)
    tr = lambda a: jnp.transpose(a, (0, 2, 1))
    bmat = jnp.concatenate([_bd(tr(bb_re)), _bd(tr(bb_im))], axis=1)
    cmat = jnp.concatenate([_bd(tr(rep['l1_c_re'])), -_bd(tr(rep['l1_c_im']))], axis=0)
    sp = _s5_sparse(x.shape[0])
    bu = _mm(u, bmat, sparse=sp['expand'], name="l1_bu")
    (xs,) = _scan(bu, lam, reverse=False, name="l1_scan")
    yc = _mm(xs, cmat, sparse=sp['reduce'], name="l1_cx")
    dsk = rep['l1_d'].reshape(1, -1)
    (y,) = _rowwise(_gelu_skip, [(yc, SSM_WIDTH, 0), (u, SSM_WIDTH, 0)], [dsk], [(SSM_WIDTH, BF16)], [],
                    ts=ROW_TS, name="l1_gelu")
    z = _mm(y, w_glu, name="l1_glu")
    bg = rep['l1_b_glu'].reshape(1, -1)
    (out,) = _rowwise(lambda z1, z2, xv, b1, b2: _glu_out(z1, z2, b1, b2, xv),
                      [(z, D_MODEL, 0), (z, D_MODEL, 1), (x, D_MODEL, 0)], [bg[:, :D_MODEL], bg[:, D_MODEL:]],
                      [(D_MODEL, F32)], [], ts=ROW_TS, name="l1_glu_out")
    return out, (x, xn, u, lam, bmat, cmat, xs, yc, dsk, y, z, bg, w_in, w_glu)


def _s5_bwd(dy, saved, rep, grads, gq):
    x, xn, u, lam, bmat, cmat, xs, yc, dsk, y, z, bg, w_in, w_glu = saved

    def glu_bwd(z1, z2, dv, b1, b2):
        _, vjp = jax.vjp(lambda a, b, c, d: (a + c) * jax.nn.sigmoid(b + d), z1, z2, b1, b2)
        d1, d2, db1, db2 = vjp(dv)
        return jnp.concatenate([d1, d2], axis=1), db1, db2

    dz, db1, db2 = _rowwise(glu_bwd, [(z, D_MODEL, 0), (z, D_MODEL, 1), (dy, D_MODEL, 0)],
                            [bg[:, :D_MODEL], bg[:, D_MODEL:]], [(2 * D_MODEL, BF16)], [(1, D_MODEL), (1, D_MODEL)],
                            ts=ROW_TS, name="l1_glu_out_bwd")
    grads['l1_b_glu'] = jnp.concatenate([db1, db2], axis=1).reshape(-1)
    dyv = _mm(dz, w_glu, tb=True, name="l1_glu_dx")
    gq['l1_w_glu'] = _mm(y, dz, ta=True, name="l1_glu_dw")

    def gelu_bwd(ycv, uv, dv, dk):
        _, vjp = jax.vjp(_gelu_skip, ycv, uv, dk)
        return vjp(dv)

    dyc, du_skip, dd = _rowwise(gelu_bwd, [(yc, SSM_WIDTH, 0), (u, SSM_WIDTH, 0), (dyv, SSM_WIDTH, 0)], [dsk],
                                [(SSM_WIDTH, F32), (SSM_WIDTH, F32)], [(1, SSM_WIDTH)], ts=ROW_TS, name="l1_gelu_bwd")
    grads['l1_d'] = dd.reshape(-1)
    sp = _s5_sparse(x.shape[0])
    dxs = _mm(dyc, cmat, tb=True, sparse=sp['expand'], name="l1_cx_dx")
    dcm = _mm(xs, dyc, ta=True, sparse=sp['wide_t'], name="l1_cx_dw")
    gs, dlam = _scan(dxs, lam, reverse=True, xs=xs, name="l1_scan_bwd")
    dlr, dli = dlam[:, :SSM_CH], dlam[:, SSM_CH:]
    du = _mm(gs, bmat, tb=True, res=du_skip, sparse=sp['reduce'], name="l1_bu_dx")
    dbm = _mm(u, gs, ta=True, sparse=sp['narrow_t'], name="l1_bu_dw")
    eye = jnp.eye(S5_TILE_GROUPS, dtype=F32)
    nb = SSM_GROUPS // S5_TILE_GROUPS
    dcm = dcm.reshape(2, nb, S5_TILE_GROUPS, SSM_STATE, S5_TILE_GROUPS, SSM_GROUP)
    dcm = jnp.sum(dcm * eye[None, None, :, None, :, None], axis=4).reshape(2, SSM_GROUPS, SSM_STATE, SSM_GROUP)
    tr = lambda a: jnp.transpose(a, (0, 2, 1))
    grads['l1_c_re'], grads['l1_c_im'] = tr(dcm[0]), -tr(dcm[1])
    dbm = dbm.reshape(S5_TILE_GROUPS, SSM_GROUP, 2, nb, S5_TILE_GROUPS, SSM_STATE)
    dbm = jnp.sum(dbm * eye[:, None, None, None, :, None], axis=0)
    dbm = jnp.transpose(dbm, (1, 2, 3, 4, 0)).reshape(2, SSM_GROUPS, SSM_STATE, SSM_GROUP)
    dbb_re, dbb_im = dbm[0], dbm[1]
    names = ['l1_log_dt', 'l1_a_re', 'l1_a_im', 'l1_b_re', 'l1_b_im']
    _, vjp = jax.vjp(_s5_disc, *[rep[n] for n in names])
    for n, gval in zip(names, vjp((dlr.reshape(SSM_GROUPS, SSM_STATE), dli.reshape(SSM_GROUPS, SSM_STATE), dbb_re, dbb_im))):
        grads[n] = gval
    dxn = _mm(du, w_in, tb=True, name="l1_in_dx")
    gq['l1_w_in'] = _mm(xn, du, ta=True, name="l1_in_dw")
    dx, dgn = _rms_bwd(x, rep['l1_mix_norm'], dxn, dy, "l1_mix_norm_bwd")
    grads['l1_mix_norm'] = dgn.reshape(-1)
    return dx


def _loss_head(x, g, target):
    d = x.shape[1]

    def fn(xv, tv, gv):
        y, vjp = jax.vjp(_rms, xv, gv)
        err = y - tv
        part = 0.5 * jnp.sum(jnp.mean(jnp.square(err), axis=-1, keepdims=True), axis=0, keepdims=True)
        dx, dg = vjp(err * (1.0 / d))
        return dx, jnp.broadcast_to(part, (1, 128)), dg

    return _rowwise(fn, [(x, d, 0), (target, d, 0)], [g.reshape(1, -1)], [(d, F32)], [(1, 128), (1, d)], ts=ROW_TS,
                    name="loss_head")


FIRST = ('l0_w_in',)
REST = (('l0_conv_w', 'l0_w_uq', 'l0_w_ukv', 'l0_w_out'),
        ('l0_w_up', 'l0_ffn_conv_w', 'l0_w_down'),
        ('l1_w_in', 'l1_w_glu', 'l1_w_up', 'l1_ffn_conv_w', 'l1_w_down'))
GRADS_L1 = ('l1_w_in', 'l1_w_glu', 'l1_w_up', 'l1_ffn_conv_w', 'l1_w_down')
GRADS_L0_FFN = ('l0_w_up', 'l0_ffn_conv_w', 'l0_w_down')
GRADS_MIXER = ('l0_w_out', 'l0_w_uq', 'l0_w_ukv')
LAST = ('l0_w_in', 'l0_conv_w')
REP_LATE = ('l0_mix_norm', 'l0_conv_b', 'l0_conv_ln_g', 'l0_conv_ln_b', 'l0_q_norm', 'l0_kv_norm')
REP_EARLY = tuple(n for n in REPLICATED if n not in REP_LATE)


def _local_step(x, target, rep, got, wait_rest, send_grads):
    q = _prep_weights({'l0_w_in': _full('l0_w_in', got['l0_w_in'])})
    tabs = _rope_tables(x.shape[0])
    x1, s_mla = _mla_fwd(x, rep, q, lambda after: wait_rest(0, after), tabs)
    x2, s_f0 = _ffn_fwd(x1, rep, wait_rest(1, x1), 'l0_', "l0")
    got = wait_rest(2, x2)
    x3, s_s5 = _s5_fwd(x2, rep, _full('l1_w_in', got['l1_w_in']), _full('l1_w_glu', got['l1_w_glu']))
    x4, s_f1 = _ffn_fwd(x3, rep, got, 'l1_', "l1")
    dx4, loss, dgf = _loss_head(x4, rep['final_norm'], target)
    grads, gq, gsh = {'final_norm': dgf.reshape(-1)}, {}, {}
    dx3 = _ffn_bwd(dx4, s_f1, rep, 'l1_', "l1", grads, gsh)
    dx2 = _s5_bwd(dx3, s_s5, rep, grads, gq)
    for n in ('l1_w_in', 'l1_w_glu'):
        gsh[n] = _shards(n, gq[n]).astype(BF16)
    dx2 = send_grads(GRADS_L1, gsh, dx2)
    dx1 = _ffn_bwd(dx2, s_f0, rep, 'l0_', "l0", grads, gsh)
    dx1 = send_grads(GRADS_L0_FFN, gsh, dx1, small=[grads[n] for n in REP_EARLY])
    dx0 = _mla_bwd(dx1, s_mla, rep, q, tabs, grads, gq, send_grads)
    full = _unprep_grads({'w_in': gq['w_in']})
    full['l0_conv_w'] = gq['conv_w']
    return loss[0, 0], dx0, grads, {n: _shards(n, full[n]).astype(BF16) for n in LAST}


def kernel(x, l0_mix_norm, l0_w_in, l0_conv_w, l0_conv_b, l0_conv_ln_g, l0_conv_ln_b, l0_q_norm, l0_kv_norm, l0_w_uq, l0_w_ukv, l0_w_out, l0_ffn_norm, l0_w_up, l0_ffn_conv_w, l0_ffn_conv_b, l0_w_down, l1_mix_norm, l1_w_in, l1_log_dt, l1_a_re, l1_a_im, l1_b_re, l1_b_im, l1_c_re, l1_c_im, l1_d, l1_w_glu, l1_b_glu, l1_ffn_norm, l1_w_up, l1_ffn_conv_w, l1_ffn_conv_b, l1_w_down, final_norm, loss_target, m_l0_mix_norm, m_l0_w_in, m_l0_conv_w, m_l0_conv_b, m_l0_conv_ln_g, m_l0_conv_ln_b, m_l0_q_norm, m_l0_kv_norm, m_l0_w_uq, m_l0_w_ukv, m_l0_w_out, m_l0_ffn_norm, m_l0_w_up, m_l0_ffn_conv_w, m_l0_ffn_conv_b, m_l0_w_down, m_l1_mix_norm, m_l1_w_in, m_l1_log_dt, m_l1_a_re, m_l1_a_im, m_l1_b_re, m_l1_b_im, m_l1_c_re, m_l1_c_im, m_l1_d, m_l1_w_glu, m_l1_b_glu, m_l1_ffn_norm, m_l1_w_up, m_l1_ffn_conv_w, m_l1_ffn_conv_b, m_l1_w_down, m_final_norm, v_l0_mix_norm, v_l0_w_in, v_l0_conv_w, v_l0_conv_b, v_l0_conv_ln_g, v_l0_conv_ln_b, v_l0_q_norm, v_l0_kv_norm, v_l0_w_uq, v_l0_w_ukv, v_l0_w_out, v_l0_ffn_norm, v_l0_w_up, v_l0_ffn_conv_w, v_l0_ffn_conv_b, v_l0_w_down, v_l1_mix_norm, v_l1_w_in, v_l1_log_dt, v_l1_a_re, v_l1_a_im, v_l1_b_re, v_l1_b_im, v_l1_c_re, v_l1_c_im, v_l1_d, v_l1_w_glu, v_l1_b_glu, v_l1_ffn_norm, v_l1_w_up, v_l1_ffn_conv_w, v_l1_ffn_conv_b, v_l1_w_down, v_final_norm):
    args = dict(locals())
    w = {n: args[n] for n in WEIGHTS}
    m = {n: args['m_' + n] for n in WEIGHTS}
    v = {n: args['v_' + n] for n in WEIGHTS}
    payload = lambda n: w[n] if n in TAPS else w[n].astype(BF16)
    got = dict(zip(FIRST, _exchange([payload(n) for n in FIRST], ['gather'] * len(FIRST), "gather_first")))
    rep = {n: w[n] for n in REPLICATED}
    rest = []
    for k, names in enumerate(REST):
        flight, got['l0_w_in'] = _exchange_start([payload(n) for n in names], ['gather'] * len(names),
                                                 got['l0_w_in'], f"gather_rest{k}_start")
        rest.append(flight)
    wait_rest = lambda k, after: dict(zip(REST[k], _exchange_wait(rest[k], after, f"gather_rest{k}_wait")))
    flights = []

    def send_grads(names, gsh, carry, small=None):
        tag = f"grads{len(flights)}"
        arrs, modes = [gsh[n] for n in names], ['scatter'] * len(names)
        if small is not None:
            arrs, modes = arrs + [_flat(small)], modes + ['gather']
        flight, carry = _exchange_start(arrs, modes, carry, tag + "_start")
        flights.append((names, flight, tag + "_wait"))
        return carry

    loss, dx, grads, gsh = _local_step(x[0], loss_target[0], rep, got, wait_rest, send_grads)

    last = _exchange([gsh[n] for n in LAST] + [_flat([grads[n] for n in REP_LATE], rows=8)],
                     ['scatter'] * len(LAST) + ['gather'], "exchange_grads")
    recv = dict(zip(LAST, last))
    small = {REP_LATE: last[-1]}
    for names, flight, name in flights:
        lands = _exchange_wait(flight, last[-1], name)
        recv.update(zip(names, lands))
        if len(lands) > len(names):
            small[REP_EARLY] = lands[-1]
    res = [dict(), dict(), dict(), dict()]
    for n in SHARDED:
        for kind, a in enumerate(_adamw(recv[n], w[n], m[n], v[n], "adamw_" + n)):
            res[kind][n] = a
    for names, parts in small.items():
        flatr = lambda d: _flat([d[n] for n in names], rows=parts.shape[1])
        rp_out = _adamw(parts, flatr(w), flatr(m), flatr(v), "adamw_replicated_" + names[0])
        for kind in range(4):
            for n, a in zip(names, _unflat(rp_out[kind], [w[n].shape for n in names])):
                res[kind][n] = a
    total = lax.psum(loss, ("x", "y", "c"))
    return (total, dx[None], *[res[0][n] for n in WEIGHTS], *[res[1][n] for n in WEIGHTS],
            *[res[2][n] for n in WEIGHTS], *[res[3][n] for n in WEIGHTS])
```

```python
import functools

import jax
import jax.numpy as jnp
from jax import lax
from jax.experimental import pallas as pl
from jax.experimental.pallas import tpu as pltpu

F32 = jnp.float32
BF16 = jnp.bfloat16

N_DEV = 8
D_MODEL = 1024
EPS = 1e-6
LN_EPS = 1e-5
CONV_WIDTH = 512
CONV_K = 31
N_HEADS = 8
QK_NOPE = 64
QK_ROPE = 32
V_DIM = 64
HEAD_PAD = 128
Q_RANK = 256
KV_RANK = 128
ROPE_BASE = 10000.0
IN_EVEN = 2 * CONV_WIDTH + Q_RANK + KV_RANK + QK_ROPE
IN_PAD = 1536
KR_LANE = 64
SSM_WIDTH = 512
SSM_GROUP = 16
SSM_GROUPS = 32
SSM_STATE = 64
SSM_CH = SSM_GROUPS * SSM_STATE
D_FF = 2816
FF_SHARD = 2 * D_FF // N_DEV
FF_HALF = N_DEV // 2
FFN_K = 3
TAP_ROWS = 8
ADAM_LR, ADAM_B1, ADAM_B2, ADAM_EPS, ADAM_WD, ADAM_STEP = 0.001, 0.9, 0.999, 1e-08, 0.01, 10

WEIGHTS = ['l0_mix_norm', 'l0_w_in', 'l0_conv_w', 'l0_conv_b', 'l0_conv_ln_g', 'l0_conv_ln_b', 'l0_q_norm',
           'l0_kv_norm', 'l0_w_uq', 'l0_w_ukv', 'l0_w_out', 'l0_ffn_norm', 'l0_w_up', 'l0_ffn_conv_w',
           'l0_ffn_conv_b', 'l0_w_down', 'l1_mix_norm', 'l1_w_in', 'l1_log_dt', 'l1_a_re', 'l1_a_im', 'l1_b_re',
           'l1_b_im', 'l1_c_re', 'l1_c_im', 'l1_d', 'l1_w_glu', 'l1_b_glu', 'l1_ffn_norm', 'l1_w_up',
           'l1_ffn_conv_w', 'l1_ffn_conv_b', 'l1_w_down', 'final_norm']
SHARDED = {'l0_w_in': 1, 'l0_conv_w': 1, 'l0_w_uq': 1, 'l0_w_ukv': 1, 'l0_w_out': 0, 'l0_w_up': 1,
           'l0_ffn_conv_w': 1, 'l0_w_down': 0, 'l1_w_in': 0, 'l1_w_glu': 1, 'l1_w_up': 1, 'l1_ffn_conv_w': 1,
           'l1_w_down': 0}
TAPS = ('l0_conv_w', 'l0_ffn_conv_w', 'l1_ffn_conv_w')
REPLICATED = [n for n in WEIGHTS if n not in SHARDED]


def _tile(n, cands):
    for c in cands:
        if n % c == 0:
            return c
    return n


def _same(g):
    return g


def _mm(a, b, *, ta=False, tb=False, res=None, out_dtype=F32, name, ga=None, gb=None, go=None, groups=1,
        sparse=None):
    a2, b2 = (a.shape[1:] if ga else a.shape), (b.shape[1:] if gb else b.shape)
    m, kd = (a2[1], a2[0]) if ta else a2
    kd2, n = (b2[1], b2[0]) if tb else b2
    assert kd == kd2, (a.shape, b.shape, ta, tb)
    tm = _tile(m, (1024, 512, 256, 128))
    tn = _tile(n, (512, 384, 256, 128))
    tk = _tile(kd, (2048, 1024, 512, 256, 128) if ta else (1024, 512, 256, 128))
    nk = kd // tk
    a_m = b_n = k_of = None
    if sparse is not None:
        (tm, tn, tk), (m, n), nk = sparse['tiles'], sparse['out'], sparse['nk']
        a_m, b_n, k_of = sparse.get('a_m'), sparse.get('b_n'), sparse.get('k_of')
    a_m = a_m or (lambda i, j: i)
    b_n = b_n or (lambda i, j: j)
    k_of = k_of or (lambda i, j, k: k)
    summed = go is None and ga is not None and gb is not None
    assert summed or go is not None or (ga is None and gb is None)
    nkk = nk
    dn = (((0 if ta else 1,), (1 if tb else 0,)), ((), ()))

    def body(*refs):
        if res is None:
            a_ref, b_ref, o_ref, acc_ref = refs
            r_ref = None
        else:
            a_ref, b_ref, r_ref, o_ref, acc_ref = refs
        k = pl.program_id(3)
        if summed:
            p = lax.dot_general(a_ref[0].astype(BF16), b_ref[0].astype(BF16), dn, preferred_element_type=F32)
            for g in range(1, groups):
                p = p + lax.dot_general(a_ref[g].astype(BF16), b_ref[g].astype(BF16), dn, preferred_element_type=F32)
        else:
            p = lax.dot_general(a_ref[...].astype(BF16), b_ref[...].astype(BF16), dn, preferred_element_type=F32)

        @pl.when(k == 0)
        def _():
            acc_ref[...] = p

        @pl.when(k > 0)
        def _():
            acc_ref[...] += p

        @pl.when(k == nkk - 1)
        def _():
            out = acc_ref[...]
            if r_ref is not None:
                out = out + r_ref[...]
            o_ref[...] = out.astype(out_dtype)

    def spec(shape2, idx2, gmap):
        if gmap is None:
            return pl.BlockSpec(shape2, lambda g, i, j, kk: idx2(i, j, kk))
        if summed:
            return pl.BlockSpec((groups,) + shape2, lambda g, i, j, kk: (gmap(0) // groups,) + idx2(i, j, kk))
        return pl.BlockSpec((None,) + shape2, lambda g, i, j, kk: (gmap(g),) + idx2(i, j, kk))

    a_idx = (lambda i, j, k: (k_of(i, j, k), a_m(i, j))) if ta else (lambda i, j, k: (a_m(i, j), k_of(i, j, k)))
    b_idx = (lambda i, j, k: (b_n(i, j), k_of(i, j, k))) if tb else (lambda i, j, k: (k_of(i, j, k), b_n(i, j)))
    a_spec = spec((tk, tm) if ta else (tm, tk), a_idx, ga)
    b_spec = spec((tn, tk) if tb else (tk, tn), b_idx, gb)
    o_spec = spec((tm, tn), lambda i, j, k: (i, j), go)
    in_specs, args = [a_spec, b_spec], [a, b]
    if res is not None:
        in_specs.append(o_spec)
        args.append(res)
    out_shape = (groups, m, n) if go else (m, n)
    return pl.pallas_call(
        body, grid=(groups if go else 1, m // tm, n // tn, nkk), in_specs=in_specs, out_specs=o_spec,
        out_shape=jax.ShapeDtypeStruct(out_shape, out_dtype), scratch_shapes=[pltpu.VMEM((tm, tn), F32)],
        compiler_params=pltpu.CompilerParams(dimension_semantics=("parallel", "parallel", "parallel", "arbitrary")),
        name=name)(*args)


ROW_TS = 1024
def _rowwise(fn, rows, bcasts, row_outs, red_outs, *, ts, name):
    s = rows[0][0].shape[0]
    ts = min(ts, s)
    nr, nb, nro, nre = len(rows), len(bcasts), len(row_outs), len(red_outs)

    def body(*refs):
        i = pl.program_id(0)
        outs = fn(*[r[...] for r in refs[:nr + nb]])
        if not isinstance(outs, (tuple, list)):
            outs = (outs,)
        o_refs = refs[nr + nb:]
        for q in range(nro):
            o_refs[q][...] = outs[q].astype(o_refs[q].dtype)
        for q in range(nro, nro + nre):
            @pl.when(i == 0)
            def _(q=q):
                o_refs[q][...] = outs[q]

            @pl.when(i > 0)
            def _(q=q):
                o_refs[q][...] += outs[q]

    in_specs = [pl.BlockSpec((ts, w), functools.partial(lambda i, cb: (i, cb), cb=cb)) for (_, w, cb) in rows]
    in_specs += [pl.BlockSpec(b.shape, functools.partial(lambda i, nd: (0,) * nd, nd=b.ndim)) for b in bcasts]
    out_specs = [pl.BlockSpec((ts, w), lambda i: (i, 0)) for (w, _) in row_outs]
    out_specs += [pl.BlockSpec((r, w), lambda i: (0, 0)) for (r, w) in red_outs]
    out_shape = [jax.ShapeDtypeStruct((s, w), dt) for (w, dt) in row_outs]
    out_shape += [jax.ShapeDtypeStruct((r, w), F32) for (r, w) in red_outs]
    return pl.pallas_call(
        body, grid=(s // ts,), in_specs=in_specs, out_specs=out_specs, out_shape=out_shape,
        compiler_params=pltpu.CompilerParams(dimension_semantics=("arbitrary",)), name=name,
    )(*[r[0] for r in rows], *bcasts)


def _rms(x, g):
    return x * lax.rsqrt(jnp.mean(x * x, axis=-1, keepdims=True) + EPS) * g


def _rms_fwd(x, g, name):
    return _rowwise(lambda xv, gv: _rms(xv, gv), [(x, x.shape[1], 0)], [g.reshape(1, -1)],
                    [(x.shape[1], BF16)], [], ts=ROW_TS, name=name)[0]


def _rms_bwd(x, g, dxn, dres, name):
    d = x.shape[1]

    def fn(xv, dv, rv, gv):
        _, vjp = jax.vjp(_rms, xv, gv)
        dx, dg = vjp(dv.astype(F32))
        return rv + dx, dg

    return _rowwise(fn, [(x, d, 0), (dxn, d, 0), (dres, d, 0)], [g.reshape(1, -1)], [(d, F32)], [(1, d)],
                    ts=ROW_TS, name=name)


def _cspec(mode, off, rows, width, rowblk, n_rb):
    if mode == 'col':
        return pl.BlockSpec((rows, width), lambda jc, i: (rowblk(i), off + jc))
    return pl.BlockSpec((rows, width), lambda jc, i: ((off + jc) * n_rb + rowblk(i), 0))


TAP_SPREAD = 24


def _stage_shape(k_taps, n, tc):
    return (8, n + TAP_SPREAD, tc) if k_taps > 8 else (1, 8, 128)


def _tap_windows(ref, offsets, n, stage):
    if len(offsets) <= 8:
        return [functools.partial(lambda v: v, ref[pl.ds(o, n), :]) for o in offsets]
    lows = {}
    for o in offsets:
        lows[o % 8] = min(o, lows.get(o % 8, o))
    for r, lo in lows.items():
        span = n + max(o for o in offsets if o % 8 == r) - lo
        stage[r, pl.ds(0, span), :] = ref[pl.ds(lo, span), :]
    return [functools.partial(lambda o: stage[o % 8, pl.ds(o - lows[o % 8], n), :], o) for o in offsets]


def _conv_fwd(xins, w, woffs, params, poffs, pre, post, outs, *, mode, s, k_taps, hb, ts, tc, rc, n_cb, name):
    n_s, nx, ncv, npar, no = s // ts, len(xins), len(woffs), len(params), len(outs)
    rpb = ts // hb
    kp = TAP_ROWS * ((k_taps + TAP_ROWS - 1) // TAP_ROWS)
    pr = 1 if mode == 'col' else TAP_ROWS

    def body(*refs):
        mains, halos = refs[:nx], refs[nx:2 * nx]
        w_refs = refs[2 * nx:2 * nx + ncv]
        p_refs = refs[2 * nx + ncv:2 * nx + ncv + npar]
        o_refs = refs[2 * nx + ncv + npar:2 * nx + ncv + npar + no]
        u_s = refs[2 * nx + ncv + npar + no:2 * nx + ncv + npar + no + ncv]
        win_s = refs[2 * nx + ncv + npar + no + ncv:2 * nx + ncv + npar + no + 2 * ncv]
        stage_s = refs[2 * nx + ncv + npar + no + 2 * ncv:]
        i = pl.program_id(1)
        um = pre(*[r[...].astype(F32) for r in mains])
        uh = pre(*[r[...].astype(F32) for r in halos])
        first = (i > 0).astype(F32)
        for q in range(ncv):
            u_s[q][pl.ds(0, hb), :] = uh[q] * first
            u_s[q][pl.ds(hb, ts), :] = um[q]
        pv = [r[0:1, :] for r in p_refs]

        def chunk(ci, carry):
            r0 = pl.multiple_of(ci * rc, rc)
            cs = []
            for q in range(ncv):
                win_s[q][...] = u_s[q][pl.ds(r0, rc + hb), :]
                acc = jnp.zeros((rc, tc), F32)
                offsets = [hb - (k_taps - 1) + t for t in range(k_taps)]
                for t, ut in enumerate(_tap_windows(win_s[q], offsets, rc, stage_s[q])):
                    acc = acc + w_refs[q][pl.ds(t, 1), :] * ut()
                cs.append(acc)
            res = post(cs, pv)
            for q in range(no):
                o_refs[q][pl.ds(r0, rc), :] = res[q].astype(o_refs[q].dtype)
            return carry

        lax.fori_loop(0, ts // rc, chunk, 0)

    main = lambda i: i
    prev = lambda i: jnp.maximum(i * rpb - 1, 0)
    zero = lambda i: 0
    in_specs = [_cspec(mode, off, ts, tc, main, n_s) for _, off in xins]
    in_specs += [_cspec(mode, off, hb, tc, prev, s // hb) for _, off in xins]
    in_specs += [_cspec(mode, off, kp, tc, zero, 1) for off in woffs]
    in_specs += [_cspec(mode, off, pr, tc, zero, 1) for off in poffs]
    out_specs = [_cspec(mode, 0, ts, tc, main, n_s) for _ in outs]
    oshape = (s, n_cb * tc) if mode == 'col' else (n_cb * s, tc)
    out_shape = [jax.ShapeDtypeStruct(oshape, dt) for dt in outs]
    return pl.pallas_call(
        body, grid=(n_cb, n_s), in_specs=in_specs, out_specs=out_specs, out_shape=out_shape,
        scratch_shapes=[pltpu.VMEM((hb + ts, tc), F32) for _ in range(ncv)]
        + [pltpu.VMEM((hb + rc, tc), F32) for _ in range(ncv)]
        + [pltpu.VMEM(_stage_shape(k_taps, rc, tc), F32) for _ in range(ncv)],
        compiler_params=pltpu.CompilerParams(dimension_semantics=("parallel", "arbitrary")), name=name,
    )(*[a for a, _ in xins], *[a for a, _ in xins], *([w] * ncv), *params)


def _conv_bwd(xins, w, woffs, params, poffs, pre, post, douts, dx_dtype, *, mode, s, k_taps, hb, ts, tc, rc, n_cb, name):
    n_s, nx, ncv, npar, ndo = s // ts, len(xins), len(woffs), len(params), len(douts)
    rpb = ts // hb
    n_hb = s // hb
    kp = TAP_ROWS * ((k_taps + TAP_ROWS - 1) // TAP_ROWS)
    pr = 1 if mode == 'col' else TAP_ROWS

    def body(*refs):
        pos = 0

        def take(n):
            nonlocal pos
            out = refs[pos:pos + n]
            pos += n
            return out

        mains, prevs, nexts = take(nx), take(nx), take(nx)
        d_mains, d_nexts = take(ndo), take(ndo)
        w_refs, p_refs = take(ncv), take(npar)
        dx_refs, dw_refs, dp_refs = take(nx), take(ncv), take(npar)
        u_s, d_s = take(ncv), take(ndo)
        win_s, dcw_s, dwa_s = take(ncv), take(ncv), take(ncv)
        stu_s, std_s = take(ncv), take(ncv)
        i = pl.program_id(1)
        um = pre(*[r[...].astype(F32) for r in mains])
        up = pre(*[r[...].astype(F32) for r in prevs])
        un = pre(*[r[...].astype(F32) for r in nexts])
        first = (i > 0).astype(F32)
        last = (i < n_s - 1).astype(F32)
        for q in range(ncv):
            u_s[q][pl.ds(0, hb), :] = up[q] * first
            u_s[q][pl.ds(hb, ts), :] = um[q]
            u_s[q][pl.ds(hb + ts, hb), :] = un[q]
            dwa_s[q][...] = jnp.zeros((kp * 8, tc), F32)
        for q in range(ndo):
            d_s[q][pl.ds(0, ts), :] = d_mains[q][...].astype(F32)
            d_s[q][pl.ds(ts, hb), :] = d_nexts[q][...].astype(F32) * last
        pv = [r[0:1, :] for r in p_refs]
        fwd_offsets = [hb - (k_taps - 1) + t for t in range(k_taps)]

        def chunk(ci, dpar):
            r0 = pl.multiple_of(ci * rc, rc)
            c_own, c_next, shifted = [], [], []
            for q in range(ncv):
                win_s[q][...] = u_s[q][pl.ds(r0, rc + 2 * hb), :]
                acc = jnp.zeros((rc + hb, tc), F32)
                taps = _tap_windows(win_s[q], fwd_offsets, rc + hb, stu_s[q])
                for t, ut in enumerate(taps):
                    acc = acc + w_refs[q][pl.ds(t, 1), :] * ut()
                c_own.append(acc[:rc])
                c_next.append(acc[rc:])
                shifted.append(taps)
            _, vjp_o = jax.vjp(lambda c, p: tuple(post(c, p)), c_own, pv)
            dc_own, dp_own = vjp_o(tuple(r[pl.ds(r0, rc), :] for r in d_s))
            _, vjp_n = jax.vjp(lambda c: tuple(post(c, pv)), c_next)
            (dc_next,) = vjp_n(tuple(r[pl.ds(r0 + rc, hb), :] for r in d_s))
            dus = []
            for q in range(ncv):
                dcw_s[q][pl.ds(0, rc), :] = dc_own[q]
                dcw_s[q][pl.ds(rc, hb), :] = dc_next[q]
                acc = jnp.zeros((rc, tc), F32)
                bwd_offsets = [k_taps - 1 - t for t in range(k_taps)]
                for t, dct in enumerate(_tap_windows(dcw_s[q], bwd_offsets, rc, std_s[q])):
                    acc = acc + w_refs[q][pl.ds(t, 1), :] * dct()
                    prod = dc_own[q] * shifted[q][t]()[:rc]
                    dwa_s[q][pl.ds(8 * t, 8), :] += jnp.sum(prod.reshape(rc // 8, 8, tc), axis=0)
                dus.append(acc)
            xm = [r[pl.ds(r0, rc), :].astype(F32) for r in mains]
            _, vjp_p = jax.vjp(lambda *xv: tuple(pre(*xv)), *xm)
            dxs = vjp_p(tuple(dus))
            for q in range(nx):
                dx_refs[q][pl.ds(r0, rc), :] = dxs[q].astype(dx_refs[q].dtype)
            return tuple(a + b for a, b in zip(dpar, dp_own))

        dpar = lax.fori_loop(0, ts // rc, chunk, tuple(jnp.zeros((1, tc), F32) for _ in range(npar)))
        for q in range(ncv):
            @pl.when(i == 0)
            def _(q=q):
                dw_refs[q][...] = jnp.zeros((kp, tc), F32)

            for t in range(k_taps):
                dw_refs[q][pl.ds(t, 1), :] += jnp.sum(dwa_s[q][pl.ds(8 * t, 8), :], axis=0, keepdims=True)
        for q in range(npar):
            @pl.when(i == 0)
            def _(q=q):
                dp_refs[q][...] = jnp.zeros((pr, tc), F32)

            dp_refs[q][0:1, :] += dpar[q]

    main = lambda i: i
    prev = lambda i: jnp.maximum(i * rpb - 1, 0)
    nxt = lambda i: jnp.minimum((i + 1) * rpb, n_hb - 1)
    zero = lambda i: 0
    in_specs = [_cspec(mode, off, ts, tc, main, n_s) for _, off in xins]
    in_specs += [_cspec(mode, off, hb, tc, prev, n_hb) for _, off in xins]
    in_specs += [_cspec(mode, off, hb, tc, nxt, n_hb) for _, off in xins]
    in_specs += [_cspec(mode, off, ts, tc, main, n_s) for _, off in douts]
    in_specs += [_cspec(mode, off, hb, tc, nxt, n_hb) for _, off in douts]
    in_specs += [_cspec(mode, off, kp, tc, zero, 1) for off in woffs]
    in_specs += [_cspec(mode, off, pr, tc, zero, 1) for off in poffs]
    out_specs = [_cspec(mode, 0, ts, tc, main, n_s) for _ in xins]
    out_specs += [_cspec(mode, 0, kp, tc, zero, 1) for _ in woffs]
    out_specs += [_cspec(mode, 0, pr, tc, zero, 1) for _ in params]

    def shape(rows):
        return (rows, n_cb * tc) if mode == 'col' else (n_cb * rows, tc)

    out_shape = [jax.ShapeDtypeStruct(shape(s), dx_dtype) for _ in xins]
    out_shape += [jax.ShapeDtypeStruct(shape(kp), F32) for _ in woffs]
    out_shape += [jax.ShapeDtypeStruct(shape(pr), F32) for _ in params]
    xa = [a for a, _ in xins]
    da = [a for a, _ in douts]
    return pl.pallas_call(
        body, grid=(n_cb, n_s), in_specs=in_specs, out_specs=out_specs, out_shape=out_shape,
        scratch_shapes=[pltpu.VMEM((hb + ts + hb, tc), F32) for _ in range(ncv)]
        + [pltpu.VMEM((ts + hb, tc), F32) for _ in range(ndo)]
        + [pltpu.VMEM((rc + 2 * hb, tc), F32) for _ in range(ncv)]
        + [pltpu.VMEM((rc + hb, tc), F32) for _ in range(ncv)]
        + [pltpu.VMEM((kp * 8, tc), F32) for _ in range(ncv)]
        + [pltpu.VMEM(_stage_shape(k_taps, rc + hb, tc), F32) for _ in range(ncv)]
        + [pltpu.VMEM(_stage_shape(k_taps, rc, tc), F32) for _ in range(ncv)],
        compiler_params=pltpu.CompilerParams(dimension_semantics=("parallel", "arbitrary")), name=name,
    )(*xa, *xa, *xa, *da, *da, *([w] * ncv), *params)


def _glu_pre(a, g):
    return [a * jax.nn.sigmoid(g)]


def _ln_silu_post(cs, ps):
    c = cs[0] + ps[0]
    mu = jnp.mean(c, axis=-1, keepdims=True)
    var = jnp.mean(jnp.square(c - mu), axis=-1, keepdims=True)
    y = (c - mu) * lax.rsqrt(var + LN_EPS) * ps[1] + ps[2]
    return [jax.nn.silu(y)]


def _pair_pre(a, b):
    return [a, b]


def _gate_post(cs, ps):
    return [jax.nn.silu(cs[0] + ps[0]) * (cs[1] + ps[1])]


def _conva(s):
    return dict(mode='col', s=s, k_taps=CONV_K, hb=32, ts=512, tc=CONV_WIDTH, rc=64, n_cb=1)


def _gate(s):
    return dict(mode='row', s=s, k_taps=FFN_K, hb=8, ts=min(1024, s), tc=FF_SHARD, rc=32, n_cb=FF_HALF)


def _rope_tables(s):
    half = QK_ROPE // 2
    inv = ROPE_BASE ** (-jnp.arange(half, dtype=F32) / half)
    ang = jnp.arange(s, dtype=F32)[:, None] * inv[None, :]
    cos, sin = jnp.cos(ang), jnp.sin(ang)
    z = lambda n: jnp.zeros((s, n), F32)
    c = jnp.concatenate([jnp.ones((s, QK_NOPE), F32), cos, cos, z(HEAD_PAD - QK_NOPE - QK_ROPE)], axis=1)
    s1 = jnp.concatenate([z(QK_NOPE), -sin, z(HEAD_PAD - QK_NOPE - half)], axis=1)
    s2 = jnp.concatenate([z(QK_NOPE + half), sin, z(HEAD_PAD - QK_NOPE - QK_ROPE)], axis=1)
    return c, s1, s2


def _rot(t, c, s1, s2):
    half = QK_ROPE // 2
    return t * c + pltpu.roll(t, HEAD_PAD - half, 1) * s1 + pltpu.roll(t, half, 1) * s2


def _rot_t(d, c, s1, s2):
    half = QK_ROPE // 2
    return d * c + pltpu.roll(d * s1, half, 1) + pltpu.roll(d * s2, HEAD_PAD - half, 1)


def _heads(v):
    return [v[:, h * HEAD_PAD:(h + 1) * HEAD_PAD] for h in range(N_HEADS)]


def _rope_fwd(qf, kvf, hmat, tabs, name):
    w = N_HEADS * HEAD_PAD

    def fn(q, k, v, kr, c, s1, s2):
        krr = _rot(kr, c, s1, s2)
        qo = jnp.concatenate([_rot(t, c, s1, s2) for t in _heads(q)], axis=1)
        ko = jnp.concatenate([t + krr for t in _heads(k)], axis=1)
        lane = lax.broadcasted_iota(jnp.int32, v.shape, 1) & (HEAD_PAD - 1)
        return qo, ko, jnp.where(lane == ONES_LANE, 1.0, v)

    rows = [(qf, w, 0), (kvf, w, 0), (kvf, w, 1), (hmat, HEAD_PAD, IN_PAD // HEAD_PAD - 1)]
    rows += [(t, HEAD_PAD, 0) for t in tabs]
    return _rowwise(fn, rows, [], [(w, BF16)] * 3, [], ts=ROW_TS, name=name)


def _rope_bwd(dq, dk, dv, tabs, name):
    w = N_HEADS * HEAD_PAD

    def fn(dqv, dkv, dvv, c, s1, s2):
        dqo = jnp.concatenate([_rot_t(t, c, s1, s2) for t in _heads(dqv)], axis=1)
        ksum = functools.reduce(lambda a, b: a + b, _heads(dkv))
        return dqo, jnp.concatenate([dkv, dvv], axis=1), _rot_t(ksum, c, s1, s2)

    rows = [(dq, w, 0), (dk, w, 0), (dv, w, 0)] + [(t, HEAD_PAD, 0) for t in tabs]
    return _rowwise(fn, rows, [], [(w, BF16), (2 * w, BF16), (HEAD_PAD, F32)], [], ts=ROW_TS, name=name)


ATT_Q = 1024
ATT_SUB = 512
ATT_KV = 1024
ATT_SCALE = (QK_NOPE + QK_ROPE) ** -0.5
LOG2E = 1.4426950408889634
ATT_C2 = ATT_SCALE * LOG2E
ONES_LANE = V_DIM


def _nt(a, b):
    return lax.dot_general(a, b, (((1,), (1,)), ((), ())), preferred_element_type=F32)


def _lanes(x, w):
    return x if w == HEAD_PAD else jnp.tile(x, (1, w // HEAD_PAD))


def _tri(w, transposed):
    r = lax.broadcasted_iota(jnp.int32, (w, w), 0)
    c = lax.broadcasted_iota(jnp.int32, (w, w), 1)
    return (r <= c) if transposed else (c <= r)


def _attn_fwd(q, k, v, name):
    s = q.shape[0]
    tq, kvc = min(ATT_Q, s), min(ATT_KV, s)
    nsub, per = tq // ATT_SUB, tq // kvc

    def body(q_ref, k_ref, v_ref, o_ref, lse_ref, m_s, acc_s):
        i = pl.program_id(1)
        m_s[...] = jnp.full((tq, HEAD_PAD), -jnp.inf, F32)
        acc_s[...] = jnp.zeros((tq, HEAD_PAD), F32)

        def update(r0, n, kb, vb, diag):
            rows = pl.ds(r0, n)
            w = kb.shape[0]
            sc = _nt(q_ref[rows, :], kb)
            if diag:
                sc = jnp.where(_tri(w, False), sc, -jnp.inf)
            m_prev = m_s[rows, :]
            m_next = jnp.maximum(m_prev, jnp.max(sc, axis=1, keepdims=True))
            p = jnp.exp2((sc - _lanes(m_next, w)) * ATT_C2)
            alpha = jnp.exp2((m_prev - m_next) * ATT_C2)
            acc_s[rows, :] = alpha * acc_s[rows, :] + jnp.dot(p.astype(BF16), vb, preferred_element_type=F32)
            m_s[rows, :] = m_next

        def below(j, carry):
            at = pl.ds(pl.multiple_of(j * kvc, kvc), kvc)
            update(0, tq, k_ref[at, :], v_ref[at, :], False)
            return carry

        lax.fori_loop(0, i * per, below, 0)
        for r in range(nsub):
            for c in range(r + 1):
                at = pl.ds(pl.multiple_of(i * tq + c * ATT_SUB, ATT_SUB), ATT_SUB)
                update(r * ATT_SUB, ATT_SUB, k_ref[at, :], v_ref[at, :], c == r)
        l = acc_s[:, ONES_LANE:ONES_LANE + 1]
        o_ref[...] = (acc_s[...] / l).astype(BF16)
        lse_ref[...] = m_s[...] * ATT_SCALE + jnp.log(l)

    q_spec = pl.BlockSpec((tq, HEAD_PAD), lambda h, i: (i, h))
    kv_spec = pl.BlockSpec((s, HEAD_PAD), lambda h, i: (0, h))
    return pl.pallas_call(
        body, grid=(N_HEADS, s // tq), in_specs=[q_spec, kv_spec, kv_spec], out_specs=[q_spec, q_spec],
        out_shape=[jax.ShapeDtypeStruct(q.shape, BF16), jax.ShapeDtypeStruct(q.shape, F32)],
        scratch_shapes=[pltpu.VMEM((tq, HEAD_PAD), F32)] * 2,
        compiler_params=pltpu.CompilerParams(dimension_semantics=("parallel", "arbitrary")), name=name,
    )(q, k, v)


def _attn_rows(cat, dcat, lse, name):
    s = lse.shape[0]
    tq, kvc = min(ATT_Q, s), min(ATT_KV, s)
    per = tq // kvc
    ob = CONV_WIDTH // HEAD_PAD

    def body(o_ref, do_ref, lse_ref, lser_ref, dltr_ref):
        dl = jnp.broadcast_to(jnp.sum(do_ref[...] * o_ref[...].astype(F32), axis=1, keepdims=True), (tq, HEAD_PAD))
        l2 = lse_ref[...] * LOG2E
        for c in range(per):
            lser_ref[c] = jnp.transpose(l2[c * kvc:(c + 1) * kvc])[0:8, :]
            dltr_ref[c] = jnp.transpose(dl[c * kvc:(c + 1) * kvc])[0:8, :]

    q_spec = pl.BlockSpec((tq, HEAD_PAD), lambda h, i: (i, h))
    o_spec = pl.BlockSpec((tq, HEAD_PAD), lambda h, i: (i, ob + h))
    row_spec = pl.BlockSpec((None, per, 8, kvc), lambda h, i: (h, i, 0, 0))
    rows = jax.ShapeDtypeStruct((N_HEADS, s // kvc, 8, kvc), F32)
    return pl.pallas_call(
        body, grid=(N_HEADS, s // tq), in_specs=[o_spec, o_spec, q_spec], out_specs=[row_spec, row_spec],
        out_shape=[rows, rows],
        compiler_params=pltpu.CompilerParams(dimension_semantics=("parallel", "parallel")), name=name,
    )(cat, dcat, lse)


def _attn_bwd(q, k, v, dcat, lse_r, dlt_r, name):
    s = q.shape[0]
    tk, kvc = min(ATT_Q, s), min(ATT_KV, s)
    nsub, per, n_chunks = tk // ATT_SUB, tk // kvc, s // kvc
    n_j = s // tk
    ob = CONV_WIDTH // HEAD_PAD

    def body(k_ref, v_ref, q_ref, do_ref, lse_ref, dl_ref, dk_ref, dv_ref, dq_ref, dk_s, dv_s):
        j = pl.program_id(1)
        dk_s[...] = jnp.zeros((tk, HEAD_PAD), F32)
        dv_s[...] = jnp.zeros((tk, HEAD_PAD), F32)

        @pl.when(j == 0)
        def _():
            dq_ref[...] = jnp.zeros((s, HEAD_PAD), F32)

        def update(r0, n, at, lrow, drow, diag):
            rows = pl.ds(r0, n)
            qb, dob = q_ref[at, :], do_ref[at, :].astype(BF16)
            sc = _nt(k_ref[rows, :], qb)
            if diag:
                sc = jnp.where(_tri(qb.shape[0], True), sc, -jnp.inf)
            p = jnp.exp2(sc * ATT_C2 - lrow)
            dp = _nt(v_ref[rows, :], dob)
            ds = (p * (dp - drow)).astype(BF16)
            dv_s[rows, :] += jnp.dot(p.astype(BF16), dob, preferred_element_type=F32)
            dk_s[rows, :] += jnp.dot(ds, qb, preferred_element_type=F32)
            dq_ref[at, :] += lax.dot_general(ds, k_ref[rows, :], (((0,), (0,)), ((), ())), preferred_element_type=F32)

        def above(ic, carry):
            at = pl.ds(pl.multiple_of(ic * kvc, kvc), kvc)
            update(0, tk, at, lse_ref[ic, 0:1, :], dl_ref[ic, 0:1, :], False)
            return carry

        lax.fori_loop((j + 1) * per, n_chunks, above, 0)
        for r in range(nsub):
            for c in range(r, nsub):
                at = pl.ds(pl.multiple_of(j * tk + c * ATT_SUB, ATT_SUB), ATT_SUB)
                ic = j * per + (c * ATT_SUB) // kvc
                lo = (c * ATT_SUB) % kvc
                update(r * ATT_SUB, ATT_SUB, at, lse_ref[ic, 0:1, lo:lo + ATT_SUB], dl_ref[ic, 0:1, lo:lo + ATT_SUB],
                       c == r)
        dk_ref[...] = dk_s[...] * ATT_SCALE
        dv_ref[...] = dv_s[...]

        @pl.when(j == n_j - 1)
        def _():
            dq_ref[...] = dq_ref[...] * ATT_SCALE

    kv_spec = pl.BlockSpec((tk, HEAD_PAD), lambda h, j: (j, h))
    q_spec = pl.BlockSpec((s, HEAD_PAD), lambda h, j: (0, h))
    do_spec = pl.BlockSpec((s, HEAD_PAD), lambda h, j: (0, ob + h))
    row_spec = pl.BlockSpec((None, n_chunks, 8, kvc), lambda h, j: (h, 0, 0, 0))
    full = jax.ShapeDtypeStruct(q.shape, F32)
    dk, dv, dq = pl.pallas_call(
        body, grid=(N_HEADS, n_j), in_specs=[kv_spec, kv_spec, q_spec, do_spec, row_spec, row_spec],
        out_specs=[kv_spec, kv_spec, q_spec], out_shape=[full, full, full],
        scratch_shapes=[pltpu.VMEM((tk, HEAD_PAD), F32)] * 2,
        compiler_params=pltpu.CompilerParams(dimension_semantics=("parallel", "arbitrary")), name=name,
    )(k, v, q, dcat, lse_r, dlt_r)
    return dq, dk, dv


SCAN_T = 128
SCAN_C = 512


def _scan(b, lam, *, reverse, xs=None, name):
    s = b.shape[0]
    t = min(SCAN_T, s)
    n_t, n_c = s // t, SSM_CH // SCAN_C
    with_dlam = xs is not None

    def shift(a, d, row):
        if d >= 8:
            z = jnp.zeros((d, SCAN_C), F32)
            return jnp.concatenate([a[d:], z], axis=0) if reverse else jnp.concatenate([z, a[:t - d]], axis=0)
        if reverse:
            return jnp.where(row < t - d, pltpu.roll(a, t - d, 0), 0.0)
        return jnp.where(row >= d, pltpu.roll(a, d, 0), 0.0)

    def body(*refs):
        if with_dlam:
            b_ref, lam_ref, x_ref, o_ref, dl_ref, c_s = refs
        else:
            b_ref, lam_ref, o_ref, c_s = refs
        k = pl.program_id(0)

        @pl.when(k == 0)
        def _():
            c_s[...] = jnp.zeros((1, 2 * SSM_CH), F32)
            if with_dlam:
                dl_ref[...] = jnp.zeros((1, 2 * SSM_CH), F32)

        row = lax.broadcasted_iota(jnp.int32, (t, SCAN_C), 0)
        edge = (row == t - 1) if reverse else (row == 0)
        for ch in range(n_c):
            re = pl.ds(ch * SCAN_C, SCAN_C)
            im = pl.ds(SSM_CH + ch * SCAN_C, SCAN_C)
            lr = lam_ref[:, re]
            li = -lam_ref[:, im] if reverse else lam_ref[:, im]
            cr, ci = c_s[:, re], c_s[:, im]
            ar = b_ref[:, re] + jnp.where(edge, lr * cr - li * ci, 0.0)
            ai = b_ref[:, im] + jnp.where(edge, lr * ci + li * cr, 0.0)
            d = 1
            while d < t:
                sr, si = shift(ar, d, row), shift(ai, d, row)
                ar, ai = ar + lr * sr - li * si, ai + lr * si + li * sr
                lr, li = lr * lr - li * li, 2.0 * lr * li
                d *= 2
            o_ref[:, re] = ar.astype(o_ref.dtype)
            o_ref[:, im] = ai.astype(o_ref.dtype)
            if with_dlam:
                gr = jnp.where(edge, cr, shift(ar, 1, row))
                gi = jnp.where(edge, ci, shift(ai, 1, row))
                xr, xi = x_ref[:, re].astype(F32), x_ref[:, im].astype(F32)
                dl_ref[:, re] += jnp.sum(xr * gr + xi * gi, axis=0, keepdims=True)
                dl_ref[:, im] += jnp.sum(xr * gi - xi * gr, axis=0, keepdims=True)
            last = 0 if reverse else t - 1
            c_s[:, re] = ar[last:last + 1, :]
            c_s[:, im] = ai[last:last + 1, :]

    tm = (lambda k: (n_t - 1 - k, 0)) if reverse else (lambda k: (k, 0))
    blk = pl.BlockSpec((t, 2 * SSM_CH), tm)
    vec = pl.BlockSpec((1, 2 * SSM_CH), lambda k: (0, 0))
    in_specs, args = [blk, vec], [b, lam]
    out_specs, out_shape = [blk], [jax.ShapeDtypeStruct((s, 2 * SSM_CH), BF16)]
    if with_dlam:
        in_specs.append(blk)
        args.append(xs)
        out_specs.append(vec)
        out_shape.append(jax.ShapeDtypeStruct((1, 2 * SSM_CH), F32))
    return pl.pallas_call(
        body, grid=(n_t,), in_specs=in_specs, out_specs=out_specs, out_shape=out_shape,
        scratch_shapes=[pltpu.VMEM((1, 2 * SSM_CH), F32)],
        compiler_params=pltpu.CompilerParams(dimension_semantics=("arbitrary",)), name=name,
    )(*args)


def _s5_disc(log_dt, a_re, a_im, b_re, b_im):
    dt = jnp.exp(log_dt)[:, None]
    mag = jnp.exp(a_re * dt)
    lb_re, lb_im = mag * jnp.cos(a_im * dt), mag * jnp.sin(a_im * dt)
    den = a_re * a_re + a_im * a_im
    nr, ni = lb_re - 1.0, lb_im
    f_re = (nr * a_re + ni * a_im) / den
    f_im = (ni * a_re - nr * a_im) / den
    bb_re = f_re[..., None] * b_re - f_im[..., None] * b_im
    bb_im = f_re[..., None] * b_im + f_im[..., None] * b_re
    return lb_re, lb_im, bb_re, bb_im


def _bd(a):
    g, i, j = a.shape
    eye = jnp.eye(g, dtype=a.dtype)
    return (a[:, :, None, :] * eye[:, None, :, None]).reshape(g * i, g * j)


S5_TILE_GROUPS = HEAD_PAD // SSM_GROUP


def _s5_sparse(s):
    nb = SSM_GROUPS // S5_TILE_GROUPS
    cw, sw = S5_TILE_GROUPS * SSM_GROUP, S5_TILE_GROUPS * SSM_STATE
    tm = min(1024, s)
    return dict(
        expand=dict(tiles=(tm, sw, cw), out=(s, 2 * SSM_CH), nk=1, k_of=lambda i, j, k: j % nb),
        reduce=dict(tiles=(tm, cw, sw), out=(s, SSM_WIDTH), nk=2, k_of=lambda i, j, k: j + nb * k),
        wide_t=dict(tiles=(sw, cw, tm), out=(2 * SSM_CH, cw), nk=s // tm, b_n=lambda i, j: i % nb),
        narrow_t=dict(tiles=(cw, sw, tm), out=(cw, 2 * SSM_CH), nk=s // tm, a_m=lambda i, j: j % nb))


def _exchange(arrs, modes, name):
    n = len(arrs)
    shapes = [a.shape if md == 'scatter' else (N_DEV,) + a.shape for a, md in zip(arrs, modes)]

    def body(*refs):
        srcs, outs = refs[:n], refs[n:2 * n]
        send_sems, recv_sems = refs[2 * n:]
        x, y, c = lax.axis_index("x"), lax.axis_index("y"), lax.axis_index("c")
        me = 4 * x + 2 * y + c
        copies = []
        for r, pos, peer in _peers(x, y, c):
            for q in range(n):
                copies.append(pltpu.make_async_remote_copy(
                    src_ref=srcs[q].at[peer] if modes[q] == 'scatter' else srcs[q], dst_ref=outs[q].at[me],
                    send_sem=send_sems.at[(r - 1) * n + q], recv_sem=recv_sems.at[(r - 1) * n + q], device_id=pos,
                    device_id_type=pl.DeviceIdType.MESH))
        for cp in copies:
            cp.start()
        for cp in copies:
            cp.wait_recv()
        for cp in copies:
            cp.wait_send()

    any_spec = pl.BlockSpec(memory_space=pl.ANY)
    outs = pl.pallas_call(
        body, out_shape=[jax.ShapeDtypeStruct(sh, a.dtype) for sh, a in zip(shapes, arrs)],
        in_specs=[any_spec] * n, out_specs=[any_spec] * n,
        scratch_shapes=[pltpu.SemaphoreType.DMA(((N_DEV - 1) * n,)), pltpu.SemaphoreType.DMA(((N_DEV - 1) * n,))],
        compiler_params=pltpu.CompilerParams(has_side_effects=True), name=name,
    )(*arrs)
    return _own_slots(outs, arrs, modes)


def _peers(x, y, c):
    out = []
    for r in range(1, N_DEV):
        px, py, pc = x ^ (r >> 2), y ^ ((r >> 1) & 1), c ^ (r & 1)
        out.append((r, (px, py, pc), 4 * px + 2 * py + pc))
    return out


def _own_slots(lands, srcs, modes):
    me = 4 * lax.axis_index("x") + 2 * lax.axis_index("y") + lax.axis_index("c")
    out = []
    for land, src, md in zip(lands, srcs, modes):
        own = lax.dynamic_index_in_dim(src, me, 0, keepdims=False) if md == 'scatter' else src
        out.append(lax.dynamic_update_index_in_dim(land, own, me, 0))
    return out


def _exchange_start(arrs, modes, carry, name):
    n = len(arrs)
    shapes = [a.shape if md == 'scatter' else (N_DEV,) + a.shape for a, md in zip(arrs, modes)]
    lands = [lax.empty(sh, a.dtype) for sh, a in zip(shapes, arrs)]

    def body(*refs):
        srcs, zones = refs[:n], refs[n:2 * n]
        send_sems, recv_sems = refs[2 * n + 1], refs[2 * n + 2]
        x, y, c = lax.axis_index("x"), lax.axis_index("y"), lax.axis_index("c")
        me = 4 * x + 2 * y + c
        for r, pos, peer in _peers(x, y, c):
            for q in range(n):
                src = srcs[q].at[peer] if modes[q] == 'scatter' else srcs[q]
                pltpu.make_async_remote_copy(
                    src_ref=src, dst_ref=zones[q].at[me], send_sem=send_sems.at[(r - 1) * n + q],
                    recv_sem=recv_sems.at[(r - 1) * n + q], device_id=pos, device_id_type=pl.DeviceIdType.MESH).start()

    hbm = pl.BlockSpec(memory_space=pltpu.HBM)
    sem = pl.BlockSpec(memory_space=pltpu.SEMAPHORE)
    thru = arrs + lands + [carry]
    sems = pltpu.SemaphoreType.DMA(((N_DEV - 1) * n,))
    outs = pl.pallas_call(
        body, name=name, out_shape=(sems, sems, *[pltpu.HBM(a.shape, a.dtype) for a in thru]),
        in_specs=[hbm] * len(thru), out_specs=(sem, sem, *[hbm] * len(thru)),
        input_output_aliases={q: 2 + q for q in range(len(thru))},
        compiler_params=pltpu.CompilerParams(has_side_effects=pltpu.SideEffectType.DATAFLOW_SIDE_EFFECTING),
    )(*[pltpu.with_memory_space_constraint(a, pltpu.HBM) for a in thru])
    return dict(send=outs[0], recv=outs[1], srcs=list(outs[2:2 + n]), lands=list(outs[2 + n:2 + 2 * n]),
                modes=modes), outs[-1]


def _exchange_wait(flight, after, name):
    n = len(flight['srcs'])
    modes = flight['modes']

    def body(*refs):
        srcs, zones = refs[:n], refs[n:2 * n]
        send_sems, recv_sems = refs[2 * n], refs[2 * n + 1]
        x, y, c = lax.axis_index("x"), lax.axis_index("y"), lax.axis_index("c")
        me = 4 * x + 2 * y + c
        for r, pos, peer in _peers(x, y, c):
            for q in range(n):
                src = srcs[q].at[peer] if modes[q] == 'scatter' else srcs[q]
                cp = pltpu.make_async_remote_copy(
                    src_ref=src, dst_ref=zones[q].at[me], send_sem=send_sems.at[(r - 1) * n + q],
                    recv_sem=recv_sems.at[(r - 1) * n + q], device_id=pos, device_id_type=pl.DeviceIdType.MESH)
                cp.wait_send()
                cp.wait_recv()

    hbm = pl.BlockSpec(memory_space=pltpu.HBM)
    sem = pl.BlockSpec(memory_space=pltpu.SEMAPHORE)
    bufs = flight['srcs'] + flight['lands']
    outs = pl.pallas_call(
        body, name=name, out_shape=tuple(pltpu.HBM(a.shape, a.dtype) for a in bufs),
        in_specs=[hbm] * (2 * n) + [sem, sem, pl.BlockSpec(memory_space=pl.ANY)], out_specs=tuple([hbm] * (2 * n)),
        input_output_aliases={q: q for q in range(2 * n)},
        compiler_params=pltpu.CompilerParams(has_side_effects=pltpu.SideEffectType.DATAFLOW_SIDE_EFFECTING),
    )(*bufs, flight['send'], flight['recv'], after)
    return _own_slots(outs[n:], outs[:n], modes)


def _adamw(parts, w, m, v, name):
    r, c = w.shape
    tr = _tile(r, (256, 128))

    def body(p_ref, w_ref, m_ref, v_ref, g_ref, d_ref, nm_ref, nv_ref):
        g = p_ref[0].astype(F32)
        for d in range(1, N_DEV):
            g = g + p_ref[d].astype(F32)
        m2 = ADAM_B1 * m_ref[...] + (1.0 - ADAM_B1) * g
        v2 = ADAM_B2 * v_ref[...] + (1.0 - ADAM_B2) * jnp.square(g)
        m_hat = m2 / (1.0 - ADAM_B1 ** ADAM_STEP)
        v_hat = v2 / (1.0 - ADAM_B2 ** ADAM_STEP)
        g_ref[...] = g
        d_ref[...] = -ADAM_LR * (m_hat / (jnp.sqrt(v_hat) + ADAM_EPS) + ADAM_WD * w_ref[...])
        nm_ref[...] = m2
        nv_ref[...] = v2

    spec = pl.BlockSpec((tr, c), lambda i: (i, 0))
    return pl.pallas_call(
        body, grid=(r // tr,), in_specs=[pl.BlockSpec((N_DEV, tr, c), lambda i: (0, i, 0)), spec, spec, spec],
        out_specs=[spec] * 4, out_shape=[jax.ShapeDtypeStruct((r, c), F32)] * 4,
        compiler_params=pltpu.CompilerParams(dimension_semantics=("parallel",)), name=name,
    )(parts, w, m, v)


FLAT_W = 512
FLAT_ROWS = 256


def _flat(arrs, rows=FLAT_ROWS):
    v = jnp.concatenate([a.reshape(-1) for a in arrs])
    return jnp.pad(v, (0, (-v.shape[0]) % (rows * FLAT_W))).reshape(-1, FLAT_W)


def _unflat(flat, shapes):
    v = flat.reshape(-1)
    out, off = [], 0
    for sh in shapes:
        n = 1
        for d in sh:
            n *= d
        out.append(v[off:off + n].reshape(sh))
        off += n
    return out


def _full(name, stacked):
    if SHARDED[name] == 0:
        return stacked.reshape((-1,) + stacked.shape[2:])
    return jnp.transpose(stacked, (1, 0, 2)).reshape(stacked.shape[1], -1)


def _shards(name, full):
    if SHARDED[name] == 0:
        return full.reshape((N_DEV, -1) + full.shape[1:])
    r, c = full.shape
    return jnp.transpose(full.reshape(r, N_DEV, c // N_DEV), (1, 0, 2))


def _prep_weights(p):
    q = {}
    if 'l0_w_in' in p:
        w_in = p['l0_w_in']
        z = lambda n: jnp.zeros((D_MODEL, n), w_in.dtype)
        q['w_in'] = jnp.concatenate([w_in[:, :IN_EVEN - QK_ROPE], z(KR_LANE), w_in[:, IN_EVEN - QK_ROPE:],
                                     z(HEAD_PAD - KR_LANE - QK_ROPE)], axis=1)
    dqk = QK_NOPE + QK_ROPE
    if 'l0_w_uq' in p:
        q['w_uq'] = jnp.pad(p['l0_w_uq'].reshape(Q_RANK, N_HEADS, dqk), ((0, 0), (0, 0), (0, HEAD_PAD - dqk))
                            ).reshape(Q_RANK, N_HEADS * HEAD_PAD)
    if 'l0_w_ukv' in p:
        ukv = p['l0_w_ukv'].reshape(KV_RANK, N_HEADS, 2, QK_NOPE)
        padh = lambda a: jnp.pad(a, ((0, 0), (0, 0), (0, HEAD_PAD - QK_NOPE))).reshape(KV_RANK, N_HEADS * HEAD_PAD)
        q['w_ukv'] = jnp.concatenate([padh(ukv[:, :, 0]), padh(ukv[:, :, 1])], axis=1)
    if 'l0_w_out' in p:
        wo = p['l0_w_out']
        wo_a = jnp.pad(wo[CONV_WIDTH:].reshape(N_HEADS, V_DIM, D_MODEL), ((0, 0), (0, HEAD_PAD - V_DIM), (0, 0)))
        q['w_out'] = jnp.concatenate([wo[:CONV_WIDTH], wo_a.reshape(N_HEADS * HEAD_PAD, D_MODEL)], axis=0)
    return q


def _unprep_grads(g):
    out = {}
    if 'w_in' in g:
        d = g['w_in']
        out['l0_w_in'] = jnp.concatenate([d[:, :IN_EVEN - QK_ROPE],
                                          d[:, IN_EVEN - QK_ROPE + KR_LANE:IN_EVEN + KR_LANE]], axis=1)
    dqk = QK_NOPE + QK_ROPE
    if 'w_uq' in g:
        out['l0_w_uq'] = g['w_uq'].reshape(Q_RANK, N_HEADS, HEAD_PAD)[:, :, :dqk].reshape(Q_RANK, N_HEADS * dqk)
    if 'w_ukv' in g:
        d = g['w_ukv'].reshape(KV_RANK, 2, N_HEADS, HEAD_PAD)[:, :, :, :QK_NOPE]
        out['l0_w_ukv'] = jnp.transpose(d, (0, 2, 1, 3)).reshape(KV_RANK, N_HEADS * 2 * QK_NOPE)
    if 'w_out' in g:
        d = g['w_out']
        da = d[CONV_WIDTH:].reshape(N_HEADS, HEAD_PAD, D_MODEL)[:, :V_DIM].reshape(N_HEADS * V_DIM, D_MODEL)
        out['l0_w_out'] = jnp.concatenate([d[:CONV_WIDTH], da], axis=0)
    return out


def _pad_rows(w, rows):
    return jnp.pad(w, [(0, 0)] * (w.ndim - 2) + [(0, rows - w.shape[-2]), (0, 0)])


def _ffn_fwd(x, rep, got, pre, tag):
    s = x.shape[0]
    xn = _rms_fwd(x, rep[pre + 'ffn_norm'], f"{tag}_ffn_norm")
    w_up = got[pre + 'w_up']
    hu = _mm(xn, w_up, gb=_same, go=_same, groups=N_DEV, name=f"{tag}_ffn_up").reshape(N_DEV * s, FF_SHARD)
    taps = _pad_rows(got[pre + 'ffn_conv_w'], TAP_ROWS).reshape(N_DEV * TAP_ROWS, FF_SHARD)
    bias = _pad_rows(rep[pre + 'ffn_conv_b'].reshape(N_DEV, 1, FF_SHARD), TAP_ROWS).reshape(N_DEV * TAP_ROWS, FF_SHARD)
    (act,) = _conv_fwd([(hu, 0), (hu, FF_HALF)], taps, [0, FF_HALF], [bias, bias], [0, FF_HALF], _pair_pre,
                       _gate_post, [BF16], name=f"{tag}_ffn_gate", **_gate(s))
    act = act.reshape(FF_HALF, s, FF_SHARD)
    w_down = got[pre + 'w_down'].reshape(FF_HALF, FF_SHARD, D_MODEL)
    y = _mm(act, w_down, ga=_same, gb=_same, groups=FF_HALF, res=x, name=f"{tag}_ffn_down")
    return y, (x, xn, hu, act, taps, bias, w_up, w_down)


def _ffn_bwd(dy, saved, rep, pre, tag, grads, gsh):
    x, xn, hu, act, taps, bias, w_up, w_down = saved
    s = x.shape[0]
    dact = _mm(dy, w_down, tb=True, gb=_same, go=_same, groups=FF_HALF, name=f"{tag}_ffn_down_dx")
    gsh[pre + 'w_down'] = _mm(act, dy, ta=True, ga=_same, go=_same, groups=FF_HALF, out_dtype=BF16,
                              name=f"{tag}_ffn_down_dw").reshape(N_DEV, FF_SHARD // 2, D_MODEL)
    dha, dhb, dwa, dwb, dba, dbb = _conv_bwd(
        [(hu, 0), (hu, FF_HALF)], taps, [0, FF_HALF], [bias, bias], [0, FF_HALF], _pair_pre, _gate_post,
        [(dact.reshape(FF_HALF * s, FF_SHARD), 0)], BF16, name=f"{tag}_ffn_gate_bwd", **_gate(s))
    dha, dhb = dha.reshape(FF_HALF, s, FF_SHARD), dhb.reshape(FF_HALF, s, FF_SHARD)
    dtaps = jnp.concatenate([dwa, dwb], axis=0).reshape(N_DEV, TAP_ROWS, FF_SHARD)
    gsh[pre + 'ffn_conv_w'] = dtaps[:, :FFN_K].astype(BF16)
    grads[pre + 'ffn_conv_b'] = jnp.concatenate([dba, dbb], axis=0).reshape(N_DEV, TAP_ROWS, FF_SHARD)[:, 0].reshape(-1)
    upper = lambda g: g + FF_HALF
    dxn = _mm(dha, w_up, tb=True, ga=_same, gb=_same, groups=FF_HALF, name=f"{tag}_ffn_up_dx_a")
    dxn = _mm(dhb, w_up, tb=True, ga=_same, gb=upper, groups=FF_HALF, res=dxn, name=f"{tag}_ffn_up_dx_b")
    dwu = [_mm(xn, dh, ta=True, gb=_same, go=_same, groups=FF_HALF, out_dtype=BF16, name=f"{tag}_ffn_up_dw_{t}")
           for t, dh in (("a", dha), ("b", dhb))]
    gsh[pre + 'w_up'] = jnp.concatenate(dwu, axis=0)
    dx, dg = _rms_bwd(x, rep[pre + 'ffn_norm'], dxn, dy, f"{tag}_ffn_norm_bwd")
    grads[pre + 'ffn_norm'] = dg.reshape(-1)
    return dx


def _mla_fwd(x, rep, q, wait_mixer, tabs):
    s = x.shape[0]
    xn = _rms_fwd(x, rep['l0_mix_norm'], "l0_mix_norm")
    hmat = _mm(xn, q['w_in'], name="l0_in")
    got = wait_mixer(hmat)
    q.update(_prep_weights({n: _full(n, got[n]) for n in ('l0_w_uq', 'l0_w_ukv', 'l0_w_out')}))
    wt = _pad_rows(_full('l0_conv_w', got['l0_conv_w']), 4 * TAP_ROWS)
    cpar = [rep['l0_conv_b'].reshape(1, -1), rep['l0_conv_ln_g'].reshape(1, -1), rep['l0_conv_ln_b'].reshape(1, -1)]
    (u,) = _conv_fwd([(hmat, 0), (hmat, 1)], wt, [0], cpar, [0, 0, 0], _glu_pre, _ln_silu_post, [BF16],
                     name="l0_conv", **_conva(s))
    qn, kvn = rep['l0_q_norm'].reshape(1, -1), rep['l0_kv_norm'].reshape(1, -1)
    cqn, ckvn = _rowwise(lambda a, b, ga, gb: (_rms(a, ga), _rms(b, gb)),
                         [(hmat, Q_RANK, 2 * CONV_WIDTH // Q_RANK), (hmat, KV_RANK, (2 * CONV_WIDTH + Q_RANK) // KV_RANK)],
                         [qn, kvn], [(Q_RANK, BF16), (KV_RANK, BF16)], [], ts=ROW_TS, name="l0_latent_norm")
    qf = _mm(cqn, q['w_uq'], name="l0_uq")
    kvf = _mm(ckvn, q['w_ukv'], name="l0_ukv")
    q_rot, k_full, v = _rope_fwd(qf, kvf, hmat, tabs, "l0_rope")
    o, lse = _attn_fwd(q_rot, k_full, v, "l0_attn")
    cat = jnp.concatenate([u, o], axis=1)
    y = _mm(cat, q['w_out'], res=x, name="l0_out")
    return y, (x, xn, hmat, wt, cpar, qn, kvn, cqn, ckvn, q_rot, k_full, v, lse, cat)


def _mla_bwd(dy, saved, rep, q, tabs, grads, gq, send_grads):
    x, xn, hmat, wt, cpar, qn, kvn, cqn, ckvn, q_rot, k_full, v, lse, cat = saved
    s = x.shape[0]
    dcat = _mm(dy, q['w_out'], tb=True, name="l0_out_dx")
    gq['w_out'] = _mm(cat, dy, ta=True, name="l0_out_dw")
    lse_r, dlt_r = _attn_rows(cat, dcat, lse, "l0_attn_rows")
    dq, dk, dv = _attn_bwd(q_rot, k_full, v, dcat, lse_r, dlt_r, "l0_attn_bwd")
    dqf, dkvf, dkr = _rope_bwd(dq, dk, dv, tabs, "l0_rope_bwd")
    dcqn = _mm(dqf, q['w_uq'], tb=True, name="l0_uq_dx")
    gq['w_uq'] = _mm(cqn, dqf, ta=True, name="l0_uq_dw")
    dckvn = _mm(dkvf, q['w_ukv'], tb=True, name="l0_ukv_dx")
    gq['w_ukv'] = _mm(ckvn, dkvf, ta=True, name="l0_ukv_dw")
    early = _unprep_grads({n: gq[n] for n in ('w_out', 'w_uq', 'w_ukv')})
    dckvn = send_grads(GRADS_MIXER, {n: _shards(n, g).astype(BF16) for n, g in early.items()}, dckvn)

    def lat_bwd(a, b, da, db, ga, gb):
        _, vjp = jax.vjp(lambda a_, b_, ga_, gb_: (_rms(a_, ga_), _rms(b_, gb_)), a, b, ga, gb)
        return vjp((da, db))

    dcq, dckv, dqn, dkvn = _rowwise(
        lat_bwd, [(hmat, Q_RANK, 2 * CONV_WIDTH // Q_RANK), (hmat, KV_RANK, (2 * CONV_WIDTH + Q_RANK) // KV_RANK),
                  (dcqn, Q_RANK, 0), (dckvn, KV_RANK, 0)],
        [qn, kvn], [(Q_RANK, F32), (KV_RANK, F32)], [(1, Q_RANK), (1, KV_RANK)], ts=ROW_TS, name="l0_latent_norm_bwd")
    grads['l0_q_norm'], grads['l0_kv_norm'] = dqn.reshape(-1), dkvn.reshape(-1)
    da, dg, dwt, dcb, dlg, dlb = _conv_bwd(
        [(hmat, 0), (hmat, 1)], wt, [0], cpar, [0, 0, 0], _glu_pre, _ln_silu_post, [(dcat, 0)], F32,
        name="l0_conv_bwd", **_conva(s))
    gq['conv_w'] = dwt[:CONV_K]
    grads['l0_conv_b'], grads['l0_conv_ln_g'], grads['l0_conv_ln_b'] = dcb.reshape(-1), dlg.reshape(-1), dlb.reshape(-1)
    dh = jnp.concatenate([da, dg, dcq, dckv, dkr], axis=1)
    dxn = _mm(dh, q['w_in'], tb=True, name="l0_in_dx")
    gq['w_in'] = _mm(xn, dh, ta=True, name="l0_in_dw")
    dx, dgn = _rms_bwd(x, rep['l0_mix_norm'], dxn, dy, "l0_mix_norm_bwd")
    grads['l0_mix_norm'] = dgn.reshape(-1)
    return dx


def _gelu_skip(yc, u, d):
    return jax.nn.gelu(yc + d * u)


def _glu_out(z1, z2, b1, b2, x):
    return x + (z1 + b1) * jax.nn.sigmoid(z2 + b2)


def _s5_fwd(x, rep, w_in, w_glu):
    xn = _rms_fwd(x, rep['l1_mix_norm'], "l1_mix_norm")
    u = _mm(xn, w_in, name="l1_in")
    lb_re, lb_im, bb_re, bb_im = _s5_disc(rep['l1_log_dt'], rep['l1_a_re'], rep['l1_a_im'], rep['l1_b_re'],
                                          rep['l1_b_im'])
    lam = jnp.concatenate([lb_re.reshape(1, -1), lb_im.reshape(1, -1)], axis=1)
    tr = lambda a: jnp.transpose(a, (0, 2, 1))
    bmat = jnp.concatenate([_bd(tr(bb_re)), _bd(tr(bb_im))], axis=1)
    cmat = jnp.concatenate([_bd(tr(rep['l1_c_re'])), -_bd(tr(rep['l1_c_im']))], axis=0)
    sp = _s5_sparse(x.shape[0])
    bu = _mm(u, bmat, sparse=sp['expand'], name="l1_bu")
    (xs,) = _scan(bu, lam, reverse=False, name="l1_scan")
    yc = _mm(xs, cmat, sparse=sp['reduce'], name="l1_cx")
    dsk = rep['l1_d'].reshape(1, -1)
    (y,) = _rowwise(_gelu_skip, [(yc, SSM_WIDTH, 0), (u, SSM_WIDTH, 0)], [dsk], [(SSM_WIDTH, BF16)], [],
                    ts=ROW_TS, name="l1_gelu")
    z = _mm(y, w_glu, name="l1_glu")
    bg = rep['l1_b_glu'].reshape(1, -1)
    (out,) = _rowwise(lambda z1, z2, xv, b1, b2: _glu_out(z1, z2, b1, b2, xv),
                      [(z, D_MODEL, 0), (z, D_MODEL, 1), (x, D_MODEL, 0)], [bg[:, :D_MODEL], bg[:, D_MODEL:]],
                      [(D_MODEL, F32)], [], ts=ROW_TS, name="l1_glu_out")
    return out, (x, xn, u, lam, bmat, cmat, xs, yc, dsk, y, z, bg, w_in, w_glu)


def _s5_bwd(dy, saved, rep, grads, gq):
    x, xn, u, lam, bmat, cmat, xs, yc, dsk, y, z, bg, w_in, w_glu = saved

    def glu_bwd(z1, z2, dv, b1, b2):
        _, vjp = jax.vjp(lambda a, b, c, d: (a + c) * jax.nn.sigmoid(b + d), z1, z2, b1, b2)
        d1, d2, db1, db2 = vjp(dv)
        return jnp.concatenate([d1, d2], axis=1), db1, db2

    dz, db1, db2 = _rowwise(glu_bwd, [(z, D_MODEL, 0), (z, D_MODEL, 1), (dy, D_MODEL, 0)],
                            [bg[:, :D_MODEL], bg[:, D_MODEL:]], [(2 * D_MODEL, BF16)], [(1, D_MODEL), (1, D_MODEL)],
                            ts=ROW_TS, name="l1_glu_out_bwd")
    grads['l1_b_glu'] = jnp.concatenate([db1, db2], axis=1).reshape(-1)
    dyv = _mm(dz, w_glu, tb=True, name="l1_glu_dx")
    gq['l1_w_glu'] = _mm(y, dz, ta=True, name="l1_glu_dw")

    def gelu_bwd(ycv, uv, dv, dk):
        _, vjp = jax.vjp(_gelu_skip, ycv, uv, dk)
        return vjp(dv)

    dyc, du_skip, dd = _rowwise(gelu_bwd, [(yc, SSM_WIDTH, 0), (u, SSM_WIDTH, 0), (dyv, SSM_WIDTH, 0)], [dsk],
                                [(SSM_WIDTH, F32), (SSM_WIDTH, F32)], [(1, SSM_WIDTH)], ts=ROW_TS, name="l1_gelu_bwd")
    grads['l1_d'] = dd.reshape(-1)
    sp = _s5_sparse(x.shape[0])
    dxs = _mm(dyc, cmat, tb=True, sparse=sp['expand'], name="l1_cx_dx")
    dcm = _mm(xs, dyc, ta=True, sparse=sp['wide_t'], name="l1_cx_dw")
    gs, dlam = _scan(dxs, lam, reverse=True, xs=xs, name="l1_scan_bwd")
    dlr, dli = dlam[:, :SSM_CH], dlam[:, SSM_CH:]
    du = _mm(gs, bmat, tb=True, res=du_skip, sparse=sp['reduce'], name="l1_bu_dx")
    dbm = _mm(u, gs, ta=True, sparse=sp['narrow_t'], name="l1_bu_dw")
    eye = jnp.eye(S5_TILE_GROUPS, dtype=F32)
    nb = SSM_GROUPS // S5_TILE_GROUPS
    dcm = dcm.reshape(2, nb, S5_TILE_GROUPS, SSM_STATE, S5_TILE_GROUPS, SSM_GROUP)
    dcm = jnp.sum(dcm * eye[None, None, :, None, :, None], axis=4).reshape(2, SSM_GROUPS, SSM_STATE, SSM_GROUP)
    tr = lambda a: jnp.transpose(a, (0, 2, 1))
    grads['l1_c_re'], grads['l1_c_im'] = tr(dcm[0]), -tr(dcm[1])
    dbm = dbm.reshape(S5_TILE_GROUPS, SSM_GROUP, 2, nb, S5_TILE_GROUPS, SSM_STATE)
    dbm = jnp.sum(dbm * eye[:, None, None, None, :, None], axis=0)
    dbm = jnp.transpose(dbm, (1, 2, 3, 4, 0)).reshape(2, SSM_GROUPS, SSM_STATE, SSM_GROUP)
    dbb_re, dbb_im = dbm[0], dbm[1]
    names = ['l1_log_dt', 'l1_a_re', 'l1_a_im', 'l1_b_re', 'l1_b_im']
    _, vjp = jax.vjp(_s5_disc, *[rep[n] for n in names])
    for n, gval in zip(names, vjp((dlr.reshape(SSM_GROUPS, SSM_STATE), dli.reshape(SSM_GROUPS, SSM_STATE), dbb_re, dbb_im))):
        grads[n] = gval
    dxn = _mm(du, w_in, tb=True, name="l1_in_dx")
    gq['l1_w_in'] = _mm(xn, du, ta=True, name="l1_in_dw")
    dx, dgn = _rms_bwd(x, rep['l1_mix_norm'], dxn, dy, "l1_mix_norm_bwd")
    grads['l1_mix_norm'] = dgn.reshape(-1)
    return dx


def _loss_head(x, g, target):
    d = x.shape[1]

    def fn(xv, tv, gv):
        y, vjp = jax.vjp(_rms, xv, gv)
        err = y - tv
        part = 0.5 * jnp.sum(jnp.mean(jnp.square(err), axis=-1, keepdims=True), axis=0, keepdims=True)
        dx, dg = vjp(err * (1.0 / d))
        return dx, jnp.broadcast_to(part, (1, 128)), dg

    return _rowwise(fn, [(x, d, 0), (target, d, 0)], [g.reshape(1, -1)], [(d, F32)], [(1, 128), (1, d)], ts=ROW_TS,
                    name="loss_head")


FIRST = ('l0_w_in',)
REST = (('l0_conv_w', 'l0_w_uq', 'l0_w_ukv', 'l0_w_out'),
        ('l0_w_up', 'l0_ffn_conv_w', 'l0_w_down'),
        ('l1_w_in', 'l1_w_glu', 'l1_w_up', 'l1_ffn_conv_w', 'l1_w_down'))
GRADS_L1 = ('l1_w_in', 'l1_w_glu', 'l1_w_up', 'l1_ffn_conv_w', 'l1_w_down')
GRADS_L0_FFN = ('l0_w_up', 'l0_ffn_conv_w', 'l0_w_down')
GRADS_MIXER = ('l0_w_out', 'l0_w_uq', 'l0_w_ukv')
LAST = ('l0_w_in', 'l0_conv_w')
REP_LATE = ('l0_mix_norm', 'l0_conv_b', 'l0_conv_ln_g', 'l0_conv_ln_b', 'l0_q_norm', 'l0_kv_norm')
REP_EARLY = tuple(n for n in REPLICATED if n not in REP_LATE)


def _local_step(x, target, rep, got, wait_rest, send_grads):
    q = _prep_weights({'l0_w_in': _full('l0_w_in', got['l0_w_in'])})
    tabs = _rope_tables(x.shape[0])
    x1, s_mla = _mla_fwd(x, rep, q, lambda after: wait_rest(0, after), tabs)
    x2, s_f0 = _ffn_fwd(x1, rep, wait_rest(1, x1), 'l0_', "l0")
    got = wait_rest(2, x2)
    x3, s_s5 = _s5_fwd(x2, rep, _full('l1_w_in', got['l1_w_in']), _full('l1_w_glu', got['l1_w_glu']))
    x4, s_f1 = _ffn_fwd(x3, rep, got, 'l1_', "l1")
    dx4, loss, dgf = _loss_head(x4, rep['final_norm'], target)
    grads, gq, gsh = {'final_norm': dgf.reshape(-1)}, {}, {}
    dx3 = _ffn_bwd(dx4, s_f1, rep, 'l1_', "l1", grads, gsh)
    dx2 = _s5_bwd(dx3, s_s5, rep, grads, gq)
    for n in ('l1_w_in', 'l1_w_glu'):
        gsh[n] = _shards(n, gq[n]).astype(BF16)
    dx2 = send_grads(GRADS_L1, gsh, dx2)
    dx1 = _ffn_bwd(dx2, s_f0, rep, 'l0_', "l0", grads, gsh)
    dx1 = send_grads(GRADS_L0_FFN, gsh, dx1, small=[grads[n] for n in REP_EARLY])
    dx0 = _mla_bwd(dx1, s_mla, rep, q, tabs, grads, gq, send_grads)
    full = _unprep_grads({'w_in': gq['w_in']})
    full['l0_conv_w'] = gq['conv_w']
    return loss[0, 0], dx0, grads, {n: _shards(n, full[n]).astype(BF16) for n in LAST}


def kernel(x, l0_mix_norm, l0_w_in, l0_conv_w, l0_conv_b, l0_conv_ln_g, l0_conv_ln_b, l0_q_norm, l0_kv_norm, l0_w_uq, l0_w_ukv, l0_w_out, l0_ffn_norm, l0_w_up, l0_ffn_conv_w, l0_ffn_conv_b, l0_w_down, l1_mix_norm, l1_w_in, l1_log_dt, l1_a_re, l1_a_im, l1_b_re, l1_b_im, l1_c_re, l1_c_im, l1_d, l1_w_glu, l1_b_glu, l1_ffn_norm, l1_w_up, l1_ffn_conv_w, l1_ffn_conv_b, l1_w_down, final_norm, loss_target, m_l0_mix_norm, m_l0_w_in, m_l0_conv_w, m_l0_conv_b, m_l0_conv_ln_g, m_l0_conv_ln_b, m_l0_q_norm, m_l0_kv_norm, m_l0_w_uq, m_l0_w_ukv, m_l0_w_out, m_l0_ffn_norm, m_l0_w_up, m_l0_ffn_conv_w, m_l0_ffn_conv_b, m_l0_w_down, m_l1_mix_norm, m_l1_w_in, m_l1_log_dt, m_l1_a_re, m_l1_a_im, m_l1_b_re, m_l1_b_im, m_l1_c_re, m_l1_c_im, m_l1_d, m_l1_w_glu, m_l1_b_glu, m_l1_ffn_norm, m_l1_w_up, m_l1_ffn_conv_w, m_l1_ffn_conv_b, m_l1_w_down, m_final_norm, v_l0_mix_norm, v_l0_w_in, v_l0_conv_w, v_l0_conv_b, v_l0_conv_ln_g, v_l0_conv_ln_b, v_l0_q_norm, v_l0_kv_norm, v_l0_w_uq, v_l0_w_ukv, v_l0_w_out, v_l0_ffn_norm, v_l0_w_up, v_l0_ffn_conv_w, v_l0_ffn_conv_b, v_l0_w_down, v_l1_mix_norm, v_l1_w_in, v_l1_log_dt, v_l1_a_re, v_l1_a_im, v_l1_b_re, v_l1_b_im, v_l1_c_re, v_l1_c_im, v_l1_d, v_l1_w_glu, v_l1_b_glu, v_l1_ffn_norm, v_l1_w_up, v_l1_ffn_conv_w, v_l1_ffn_conv_b, v_l1_w_down, v_final_norm):
    args = dict(locals())
    w = {n: args[n] for n in WEIGHTS}
    m = {n: args['m_' + n] for n in WEIGHTS}
    v = {n: args['v_' + n] for n in WEIGHTS}
    payload = lambda n: w[n] if n in TAPS else w[n].astype(BF16)
    got = dict(zip(FIRST, _exchange([payload(n) for n in FIRST], ['gather'] * len(FIRST), "gather_first")))
    rep = {n: w[n] for n in REPLICATED}
    rest = []
    for k, names in enumerate(REST):
        flight, got['l0_w_in'] = _exchange_start([payload(n) for n in names], ['gather'] * len(names),
                                                 got['l0_w_in'], f"gather_rest{k}_start")
        rest.append(flight)
    wait_rest = lambda k, after: dict(zip(REST[k], _exchange_wait(rest[k], after, f"gather_rest{k}_wait")))
    flights = []

    def send_grads(names, gsh, carry, small=None):
        tag = f"grads{len(flights)}"
        arrs, modes = [gsh[n] for n in names], ['scatter'] * len(names)
        if small is not None:
            arrs, modes = arrs + [_flat(small)], modes + ['gather']
        flight, carry = _exchange_start(arrs, modes, carry, tag + "_start")
        flights.append((names, flight, tag + "_wait"))
        return carry

    loss, dx, grads, gsh = _local_step(x[0], loss_target[0], rep, got, wait_rest, send_grads)

    last = _exchange([gsh[n] for n in LAST] + [_flat([grads[n] for n in REP_LATE], rows=8)],
                     ['scatter'] * len(LAST) + ['gather'], "exchange_grads")
    recv = dict(zip(LAST, last))
    small = {REP_LATE: last[-1]}
    for names, flight, name in flights:
        lands = _exchange_wait(flight, last[-1], name)
        recv.update(zip(names, lands))
        if len(lands) > len(names):
            small[REP_EARLY] = lands[-1]
    res = [dict(), dict(), dict(), dict()]
    for n in SHARDED:
        for kind, a in enumerate(_adamw(recv[n], w[n], m[n], v[n], "adamw_" + n)):
            res[kind][n] = a
    for names, parts in small.items():
        flatr = lambda d: _flat([d[n] for n in names], rows=parts.shape[1])
        rp_out = _adamw(parts, flatr(w), flatr(m), flatr(v), "adamw_replicated_" + names[0])
        for kind in range(4):
            for n, a in zip(names, _unflat(rp_out[kind], [w[n].shape for n in names])):
                res[kind][n] = a
    total = lax.psum(loss, ("x", "y", "c"))
    return (total, dx[None], *[res[0][n] for n in WEIGHTS], *[res[1][n] for n in WEIGHTS],
            *[res[2][n] for n in WEIGHTS], *[res[3][n] for n in WEIGHTS])
```

```python
import functools

import jax
import jax.numpy as jnp
from jax import lax
from jax.experimental import pallas as pl
from jax.experimental.pallas import tpu as pltpu

F32 = jnp.float32
BF16 = jnp.bfloat16

N_DEV = 8
D_MODEL = 1024
EPS = 1e-6
LN_EPS = 1e-5
CONV_WIDTH = 512
CONV_K = 31
N_HEADS = 8
QK_NOPE = 64
QK_ROPE = 32
V_DIM = 64
HEAD_PAD = 128
Q_RANK = 256
KV_RANK = 128
ROPE_BASE = 10000.0
IN_EVEN = 2 * CONV_WIDTH + Q_RANK + KV_RANK + QK_ROPE
IN_PAD = 1536
KR_LANE = 64
SSM_WIDTH = 512
SSM_GROUP = 16
SSM_GROUPS = 32
SSM_STATE = 64
SSM_CH = SSM_GROUPS * SSM_STATE
D_FF = 2816
FF_SHARD = 2 * D_FF // N_DEV
FF_HALF = N_DEV // 2
FFN_K = 3
TAP_ROWS = 8
ADAM_LR, ADAM_B1, ADAM_B2, ADAM_EPS, ADAM_WD, ADAM_STEP = 0.001, 0.9, 0.999, 1e-08, 0.01, 10

WEIGHTS = ['l0_mix_norm', 'l0_w_in', 'l0_conv_w', 'l0_conv_b', 'l0_conv_ln_g', 'l0_conv_ln_b', 'l0_q_norm',
           'l0_kv_norm', 'l0_w_uq', 'l0_w_ukv', 'l0_w_out', 'l0_ffn_norm', 'l0_w_up', 'l0_ffn_conv_w',
           'l0_ffn_conv_b', 'l0_w_down', 'l1_mix_norm', 'l1_w_in', 'l1_log_dt', 'l1_a_re', 'l1_a_im', 'l1_b_re',
           'l1_b_im', 'l1_c_re', 'l1_c_im', 'l1_d', 'l1_w_glu', 'l1_b_glu', 'l1_ffn_norm', 'l1_w_up',
           'l1_ffn_conv_w', 'l1_ffn_conv_b', 'l1_w_down', 'final_norm']
SHARDED = {'l0_w_in': 1, 'l0_conv_w': 1, 'l0_w_uq': 1, 'l0_w_ukv': 1, 'l0_w_out': 0, 'l0_w_up': 1,
           'l0_ffn_conv_w': 1, 'l0_w_down': 0, 'l1_w_in': 0, 'l1_w_glu': 1, 'l1_w_up': 1, 'l1_ffn_conv_w': 1,
           'l1_w_down': 0}
TAPS = ('l0_conv_w', 'l0_ffn_conv_w', 'l1_ffn_conv_w')
REPLICATED = [n for n in WEIGHTS if n not in SHARDED]


def _tile(n, cands):
    for c in cands:
        if n % c == 0:
            return c
    return n


def _same(g):
    return g


def _mm(a, b, *, ta=False, tb=False, res=None, out_dtype=F32, name, ga=None, gb=None, go=None, groups=1,
        sparse=None):
    a2, b2 = (a.shape[1:] if ga else a.shape), (b.shape[1:] if gb else b.shape)
    m, kd = (a2[1], a2[0]) if ta else a2
    kd2, n = (b2[1], b2[0]) if tb else b2
    assert kd == kd2, (a.shape, b.shape, ta, tb)
    tm = _tile(m, (1024, 512, 256, 128))
    tn = _tile(n, (512, 384, 256, 128))
    tk = _tile(kd, (2048, 1024, 512, 256, 128) if ta else (1024, 512, 256, 128))
    nk = kd // tk
    a_m = b_n = k_of = None
    if sparse is not None:
        (tm, tn, tk), (m, n), nk = sparse['tiles'], sparse['out'], sparse['nk']
        a_m, b_n, k_of = sparse.get('a_m'), sparse.get('b_n'), sparse.get('k_of')
    a_m = a_m or (lambda i, j: i)
    b_n = b_n or (lambda i, j: j)
    k_of = k_of or (lambda i, j, k: k)
    summed = go is None and ga is not None and gb is not None
    assert summed or go is not None or (ga is None and gb is None)
    nkk = nk
    dn = (((0 if ta else 1,), (1 if tb else 0,)), ((), ()))

    def body(*refs):
        if res is None:
            a_ref, b_ref, o_ref, acc_ref = refs
            r_ref = None
        else:
            a_ref, b_ref, r_ref, o_ref, acc_ref = refs
        k = pl.program_id(3)
        if summed:
            p = lax.dot_general(a_ref[0].astype(BF16), b_ref[0].astype(BF16), dn, preferred_element_type=F32)
            for g in range(1, groups):
                p = p + lax.dot_general(a_ref[g].astype(BF16), b_ref[g].astype(BF16), dn, preferred_element_type=F32)
        else:
            p = lax.dot_general(a_ref[...].astype(BF16), b_ref[...].astype(BF16), dn, preferred_element_type=F32)

        @pl.when(k == 0)
        def _():
            acc_ref[...] = p

        @pl.when(k > 0)
        def _():
            acc_ref[...] += p

        @pl.when(k == nkk - 1)
        def _():
            out = acc_ref[...]
            if r_ref is not None:
                out = out + r_ref[...]
            o_ref[...] = out.astype(out_dtype)

    def spec(shape2, idx2, gmap):
        if gmap is None:
            return pl.BlockSpec(shape2, lambda g, i, j, kk: idx2(i, j, kk))
        if summed:
            return pl.BlockSpec((groups,) + shape2, lambda g, i, j, kk: (gmap(0) // groups,) + idx2(i, j, kk))
        return pl.BlockSpec((None,) + shape2, lambda g, i, j, kk: (gmap(g),) + idx2(i, j, kk))

    a_idx = (lambda i, j, k: (k_of(i, j, k), a_m(i, j))) if ta else (lambda i, j, k: (a_m(i, j), k_of(i, j, k)))
    b_idx = (lambda i, j, k: (b_n(i, j), k_of(i, j, k))) if tb else (lambda i, j, k: (k_of(i, j, k), b_n(i, j)))
    a_spec = spec((tk, tm) if ta else (tm, tk), a_idx, ga)
    b_spec = spec((tn, tk) if tb else (tk, tn), b_idx, gb)
    o_spec = spec((tm, tn), lambda i, j, k: (i, j), go)
    in_specs, args = [a_spec, b_spec], [a, b]
    if res is not None:
        in_specs.append(o_spec)
        args.append(res)
    out_shape = (groups, m, n) if go else (m, n)
    return pl.pallas_call(
        body, grid=(groups if go else 1, m // tm, n // tn, nkk), in_specs=in_specs, out_specs=o_spec,
        out_shape=jax.ShapeDtypeStruct(out_shape, out_dtype), scratch_shapes=[pltpu.VMEM((tm, tn), F32)],
        compiler_params=pltpu.CompilerParams(dimension_semantics=("parallel", "parallel", "parallel", "arbitrary")),
        name=name)(*args)


ROW_TS = 1024
def _rowwise(fn, rows, bcasts, row_outs, red_outs, *, ts, name):
    s = rows[0][0].shape[0]
    ts = min(ts, s)
    nr, nb, nro, nre = len(rows), len(bcasts), len(row_outs), len(red_outs)

    def body(*refs):
        i = pl.program_id(0)
        outs = fn(*[r[...] for r in refs[:nr + nb]])
        if not isinstance(outs, (tuple, list)):
            outs = (outs,)
        o_refs = refs[nr + nb:]
        for q in range(nro):
            o_refs[q][...] = outs[q].astype(o_refs[q].dtype)
        for q in range(nro, nro + nre):
            @pl.when(i == 0)
            def _(q=q):
                o_refs[q][...] = outs[q]

            @pl.when(i > 0)
            def _(q=q):
                o_refs[q][...] += outs[q]

    in_specs = [pl.BlockSpec((ts, w), functools.partial(lambda i, cb: (i, cb), cb=cb)) for (_, w, cb) in rows]
    in_specs += [pl.BlockSpec(b.shape, functools.partial(lambda i, nd: (0,) * nd, nd=b.ndim)) for b in bcasts]
    out_specs = [pl.BlockSpec((ts, w), lambda i: (i, 0)) for (w, _) in row_outs]
    out_specs += [pl.BlockSpec((r, w), lambda i: (0, 0)) for (r, w) in red_outs]
    out_shape = [jax.ShapeDtypeStruct((s, w), dt) for (w, dt) in row_outs]
    out_shape += [jax.ShapeDtypeStruct((r, w), F32) for (r, w) in red_outs]
    return pl.pallas_call(
        body, grid=(s // ts,), in_specs=in_specs, out_specs=out_specs, out_shape=out_shape,
        compiler_params=pltpu.CompilerParams(dimension_semantics=("arbitrary",)), name=name,
    )(*[r[0] for r in rows], *bcasts)


def _rms(x, g):
    return x * lax.rsqrt(jnp.mean(x * x, axis=-1, keepdims=True) + EPS) * g


def _rms_fwd(x, g, name):
    return _rowwise(lambda xv, gv: _rms(xv, gv), [(x, x.shape[1], 0)], [g.reshape(1, -1)],
                    [(x.shape[1], BF16)], [], ts=ROW_TS, name=name)[0]


def _rms_bwd(x, g, dxn, dres, name):
    d = x.shape[1]

    def fn(xv, dv, rv, gv):
        _, vjp = jax.vjp(_rms, xv, gv)
        dx, dg = vjp(dv.astype(F32))
        return rv + dx, dg

    return _rowwise(fn, [(x, d, 0), (dxn, d, 0), (dres, d, 0)], [g.reshape(1, -1)], [(d, F32)], [(1, d)],
                    ts=ROW_TS, name=name)


def _cspec(mode, off, rows, width, rowblk, n_rb):
    if mode == 'col':
        return pl.BlockSpec((rows, width), lambda jc, i: (rowblk(i), off + jc))
    return pl.BlockSpec((rows, width), lambda jc, i: ((off + jc) * n_rb + rowblk(i), 0))


TAP_SPREAD = 24


def _stage_shape(k_taps, n, tc):
    return (8, n + TAP_SPREAD, tc) if k_taps > 8 else (1, 8, 128)


def _tap_windows(ref, offsets, n, stage):
    if len(offsets) <= 8:
        return [functools.partial(lambda v: v, ref[pl.ds(o, n), :]) for o in offsets]
    lows = {}
    for o in offsets:
        lows[o % 8] = min(o, lows.get(o % 8, o))
    for r, lo in lows.items():
        span = n + max(o for o in offsets if o % 8 == r) - lo
        stage[r, pl.ds(0, span), :] = ref[pl.ds(lo, span), :]
    return [functools.partial(lambda o: stage[o % 8, pl.ds(o - lows[o % 8], n), :], o) for o in offsets]


def _conv_fwd(xins, w, woffs, params, poffs, pre, post, outs, *, mode, s, k_taps, hb, ts, tc, rc, n_cb, name):
    n_s, nx, ncv, npar, no = s // ts, len(xins), len(woffs), len(params), len(outs)
    rpb = ts // hb
    kp = TAP_ROWS * ((k_taps + TAP_ROWS - 1) // TAP_ROWS)
    pr = 1 if mode == 'col' else TAP_ROWS

    def body(*refs):
        mains, halos = refs[:nx], refs[nx:2 * nx]
        w_refs = refs[2 * nx:2 * nx + ncv]
        p_refs = refs[2 * nx + ncv:2 * nx + ncv + npar]
        o_refs = refs[2 * nx + ncv + npar:2 * nx + ncv + npar + no]
        u_s = refs[2 * nx + ncv + npar + no:2 * nx + ncv + npar + no + ncv]
        win_s = refs[2 * nx + ncv + npar + no + ncv:2 * nx + ncv + npar + no + 2 * ncv]
        stage_s = refs[2 * nx + ncv + npar + no + 2 * ncv:]
        i = pl.program_id(1)
        um = pre(*[r[...].astype(F32) for r in mains])
        uh = pre(*[r[...].astype(F32) for r in halos])
        first = (i > 0).astype(F32)
        for q in range(ncv):
            u_s[q][pl.ds(0, hb), :] = uh[q] * first
            u_s[q][pl.ds(hb, ts), :] = um[q]
        pv = [r[0:1, :] for r in p_refs]

        def chunk(ci, carry):
            r0 = pl.multiple_of(ci * rc, rc)
            cs = []
            for q in range(ncv):
                win_s[q][...] = u_s[q][pl.ds(r0, rc + hb), :]
                acc = jnp.zeros((rc, tc), F32)
                offsets = [hb - (k_taps - 1) + t for t in range(k_taps)]
                for t, ut in enumerate(_tap_windows(win_s[q], offsets, rc, stage_s[q])):
                    acc = acc + w_refs[q][pl.ds(t, 1), :] * ut()
                cs.append(acc)
            res = post(cs, pv)
            for q in range(no):
                o_refs[q][pl.ds(r0, rc), :] = res[q].astype(o_refs[q].dtype)
            return carry

        lax.fori_loop(0, ts // rc, chunk, 0)

    main = lambda i: i
    prev = lambda i: jnp.maximum(i * rpb - 1, 0)
    zero = lambda i: 0
    in_specs = [_cspec(mode, off, ts, tc, main, n_s) for _, off in xins]
    in_specs += [_cspec(mode, off, hb, tc, prev, s // hb) for _, off in xins]
    in_specs += [_cspec(mode, off, kp, tc, zero, 1) for off in woffs]
    in_specs += [_cspec(mode, off, pr, tc, zero, 1) for off in poffs]
    out_specs = [_cspec(mode, 0, ts, tc, main, n_s) for _ in outs]
    oshape = (s, n_cb * tc) if mode == 'col' else (n_cb * s, tc)
    out_shape = [jax.ShapeDtypeStruct(oshape, dt) for dt in outs]
    return pl.pallas_call(
        body, grid=(n_cb, n_s), in_specs=in_specs, out_specs=out_specs, out_shape=out_shape,
        scratch_shapes=[pltpu.VMEM((hb + ts, tc), F32) for _ in range(ncv)]
        + [pltpu.VMEM((hb + rc, tc), F32) for _ in range(ncv)]
        + [pltpu.VMEM(_stage_shape(k_taps, rc, tc), F32) for _ in range(ncv)],
        compiler_params=pltpu.CompilerParams(dimension_semantics=("parallel", "arbitrary")), name=name,
    )(*[a for a, _ in xins], *[a for a, _ in xins], *([w] * ncv), *params)


def _conv_bwd(xins, w, woffs, params, poffs, pre, post, douts, dx_dtype, *, mode, s, k_taps, hb, ts, tc, rc, n_cb, name):
    n_s, nx, ncv, npar, ndo = s // ts, len(xins), len(woffs), len(params), len(douts)
    rpb = ts // hb
    n_hb = s // hb
    kp = TAP_ROWS * ((k_taps + TAP_ROWS - 1) // TAP_ROWS)
    pr = 1 if mode == 'col' else TAP_ROWS

    def body(*refs):
        pos = 0

        def take(n):
            nonlocal pos
            out = refs[pos:pos + n]
            pos += n
            return out

        mains, prevs, nexts = take(nx), take(nx), take(nx)
        d_mains, d_nexts = take(ndo), take(ndo)
        w_refs, p_refs = take(ncv), take(npar)
        dx_refs, dw_refs, dp_refs = take(nx), take(ncv), take(npar)
        u_s, d_s = take(ncv), take(ndo)
        win_s, dcw_s, dwa_s = take(ncv), take(ncv), take(ncv)
        stu_s, std_s = take(ncv), take(ncv)
        i = pl.program_id(1)
        um = pre(*[r[...].astype(F32) for r in mains])
        up = pre(*[r[...].astype(F32) for r in prevs])
        un = pre(*[r[...].astype(F32) for r in nexts])
        first = (i > 0).astype(F32)
        last = (i < n_s - 1).astype(F32)
        for q in range(ncv):
            u_s[q][pl.ds(0, hb), :] = up[q] * first
            u_s[q][pl.ds(hb, ts), :] = um[q]
            u_s[q][pl.ds(hb + ts, hb), :] = un[q]
            dwa_s[q][...] = jnp.zeros((kp * 8, tc), F32)
        for q in range(ndo):
            d_s[q][pl.ds(0, ts), :] = d_mains[q][...].astype(F32)
            d_s[q][pl.ds(ts, hb), :] = d_nexts[q][...].astype(F32) * last
        pv = [r[0:1, :] for r in p_refs]
        fwd_offsets = [hb - (k_taps - 1) + t for t in range(k_taps)]

        def chunk(ci, dpar):
            r0 = pl.multiple_of(ci * rc, rc)
            c_own, c_next, shifted = [], [], []
            for q in range(ncv):
                win_s[q][...] = u_s[q][pl.ds(r0, rc + 2 * hb), :]
                acc = jnp.zeros((rc + hb, tc), F32)
                taps = _tap_windows(win_s[q], fwd_offsets, rc + hb, stu_s[q])
                for t, ut in enumerate(taps):
                    acc = acc + w_refs[q][pl.ds(t, 1), :] * ut()
                c_own.append(acc[:rc])
                c_next.append(acc[rc:])
                shifted.append(taps)
            _, vjp_o = jax.vjp(lambda c, p: tuple(post(c, p)), c_own, pv)
            dc_own, dp_own = vjp_o(tuple(r[pl.ds(r0, rc), :] for r in d_s))
            _, vjp_n = jax.vjp(lambda c: tuple(post(c, pv)), c_next)
            (dc_next,) = vjp_n(tuple(r[pl.ds(r0 + rc, hb), :] for r in d_s))
            dus = []
            for q in range(ncv):
                dcw_s[q][pl.ds(0, rc), :] = dc_own[q]
                dcw_s[q][pl.ds(rc, hb), :] = dc_next[q]
                acc = jnp.zeros((rc, tc), F32)
                bwd_offsets = [k_taps - 1 - t for t in range(k_taps)]
                for t, dct in enumerate(_tap_windows(dcw_s[q], bwd_offsets, rc, std_s[q])):
                    acc = acc + w_refs[q][pl.ds(t, 1), :] * dct()
                    prod = dc_own[q] * shifted[q][t]()[:rc]
                    dwa_s[q][pl.ds(8 * t, 8), :] += jnp.sum(prod.reshape(rc // 8, 8, tc), axis=0)
                dus.append(acc)
            xm = [r[pl.ds(r0, rc), :].astype(F32) for r in mains]
            _, vjp_p = jax.vjp(lambda *xv: tuple(pre(*xv)), *xm)
            dxs = vjp_p(tuple(dus))
            for q in range(nx):
                dx_refs[q][pl.ds(r0, rc), :] = dxs[q].astype(dx_refs[q].dtype)
            return tuple(a + b for a, b in zip(dpar, dp_own))

        dpar = lax.fori_loop(0, ts // rc, chunk, tuple(jnp.zeros((1, tc), F32) for _ in range(npar)))
        for q in range(ncv):
            @pl.when(i == 0)
            def _(q=q):
                dw_refs[q][...] = jnp.zeros((kp, tc), F32)

            for t in range(k_taps):
                dw_refs[q][pl.ds(t, 1), :] += jnp.sum(dwa_s[q][pl.ds(8 * t, 8), :], axis=0, keepdims=True)
        for q in range(npar):
            @pl.when(i == 0)
            def _(q=q):
                dp_refs[q][...] = jnp.zeros((pr, tc), F32)

            dp_refs[q][0:1, :] += dpar[q]

    main = lambda i: i
    prev = lambda i: jnp.maximum(i * rpb - 1, 0)
    nxt = lambda i: jnp.minimum((i + 1) * rpb, n_hb - 1)
    zero = lambda i: 0
    in_specs = [_cspec(mode, off, ts, tc, main, n_s) for _, off in xins]
    in_specs += [_cspec(mode, off, hb, tc, prev, n_hb) for _, off in xins]
    in_specs += [_cspec(mode, off, hb, tc, nxt, n_hb) for _, off in xins]
    in_specs += [_cspec(mode, off, ts, tc, main, n_s) for _, off in douts]
    in_specs += [_cspec(mode, off, hb, tc, nxt, n_hb) for _, off in douts]
    in_specs += [_cspec(mode, off, kp, tc, zero, 1) for off in woffs]
    in_specs += [_cspec(mode, off, pr, tc, zero, 1) for off in poffs]
    out_specs = [_cspec(mode, 0, ts, tc, main, n_s) for _ in xins]
    out_specs += [_cspec(mode, 0, kp, tc, zero, 1) for _ in woffs]
    out_specs += [_cspec(mode, 0, pr, tc, zero, 1) for _ in params]

    def shape(rows):
        return (rows, n_cb * tc) if mode == 'col' else (n_cb * rows, tc)

    out_shape = [jax.ShapeDtypeStruct(shape(s), dx_dtype) for _ in xins]
    out_shape += [jax.ShapeDtypeStruct(shape(kp), F32) for _ in woffs]
    out_shape += [jax.ShapeDtypeStruct(shape(pr), F32) for _ in params]
    xa = [a for a, _ in xins]
    da = [a for a, _ in douts]
    return pl.pallas_call(
        body, grid=(n_cb, n_s), in_specs=in_specs, out_specs=out_specs, out_shape=out_shape,
        scratch_shapes=[pltpu.VMEM((hb + ts + hb, tc), F32) for _ in range(ncv)]
        + [pltpu.VMEM((ts + hb, tc), F32) for _ in range(ndo)]
        + [pltpu.VMEM((rc + 2 * hb, tc), F32) for _ in range(ncv)]
        + [pltpu.VMEM((rc + hb, tc), F32) for _ in range(ncv)]
        + [pltpu.VMEM((kp * 8, tc), F32) for _ in range(ncv)]
        + [pltpu.VMEM(_stage_shape(k_taps, rc + hb, tc), F32) for _ in range(ncv)]
        + [pltpu.VMEM(_stage_shape(k_taps, rc, tc), F32) for _ in range(ncv)],
        compiler_params=pltpu.CompilerParams(dimension_semantics=("parallel", "arbitrary")), name=name,
    )(*xa, *xa, *xa, *da, *da, *([w] * ncv), *params)


def _glu_pre(a, g):
    return [a * jax.nn.sigmoid(g)]


def _ln_silu_post(cs, ps):
    c = cs[0] + ps[0]
    mu = jnp.mean(c, axis=-1, keepdims=True)
    var = jnp.mean(jnp.square(c - mu), axis=-1, keepdims=True)
    y = (c - mu) * lax.rsqrt(var + LN_EPS) * ps[1] + ps[2]
    return [jax.nn.silu(y)]


def _pair_pre(a, b):
    return [a, b]


def _gate_post(cs, ps):
    return [jax.nn.silu(cs[0] + ps[0]) * (cs[1] + ps[1])]


def _conva(s):
    return dict(mode='col', s=s, k_taps=CONV_K, hb=32, ts=512, tc=CONV_WIDTH, rc=64, n_cb=1)


def _gate(s):
    return dict(mode='row', s=s, k_taps=FFN_K, hb=8, ts=min(1024, s), tc=FF_SHARD, rc=32, n_cb=FF_HALF)


def _rope_tables(s):
    half = QK_ROPE // 2
    inv = ROPE_BASE ** (-jnp.arange(half, dtype=F32) / half)
    ang = jnp.arange(s, dtype=F32)[:, None] * inv[None, :]
    cos, sin = jnp.cos(ang), jnp.sin(ang)
    z = lambda n: jnp.zeros((s, n), F32)
    c = jnp.concatenate([jnp.ones((s, QK_NOPE), F32), cos, cos, z(HEAD_PAD - QK_NOPE - QK_ROPE)], axis=1)
    s1 = jnp.concatenate([z(QK_NOPE), -sin, z(HEAD_PAD - QK_NOPE - half)], axis=1)
    s2 = jnp.concatenate([z(QK_NOPE + half), sin, z(HEAD_PAD - QK_NOPE - QK_ROPE)], axis=1)
    return c, s1, s2


def _rot(t, c, s1, s2):
    half = QK_ROPE // 2
    return t * c + pltpu.roll(t, HEAD_PAD - half, 1) * s1 + pltpu.roll(t, half, 1) * s2


def _rot_t(d, c, s1, s2):
    half = QK_ROPE // 2
    return d * c + pltpu.roll(d * s1, half, 1) + pltpu.roll(d * s2, HEAD_PAD - half, 1)


def _heads(v):
    return [v[:, h * HEAD_PAD:(h + 1) * HEAD_PAD] for h in range(N_HEADS)]


def _rope_fwd(qf, kvf, hmat, tabs, name):
    w = N_HEADS * HEAD_PAD

    def fn(q, k, v, kr, c, s1, s2):
        krr = _rot(kr, c, s1, s2)
        qo = jnp.concatenate([_rot(t, c, s1, s2) for t in _heads(q)], axis=1)
        ko = jnp.concatenate([t + krr for t in _heads(k)], axis=1)
        lane = lax.broadcasted_iota(jnp.int32, v.shape, 1) & (HEAD_PAD - 1)
        return qo, ko, jnp.where(lane == ONES_LANE, 1.0, v)

    rows = [(qf, w, 0), (kvf, w, 0), (kvf, w, 1), (hmat, HEAD_PAD, IN_PAD // HEAD_PAD - 1)]
    rows += [(t, HEAD_PAD, 0) for t in tabs]
    return _rowwise(fn, rows, [], [(w, BF16)] * 3, [], ts=ROW_TS, name=name)


def _rope_bwd(dq, dk, dv, tabs, name):
    w = N_HEADS * HEAD_PAD

    def fn(dqv, dkv, dvv, c, s1, s2):
        dqo = jnp.concatenate([_rot_t(t, c, s1, s2) for t in _heads(dqv)], axis=1)
        ksum = functools.reduce(lambda a, b: a + b, _heads(dkv))
        return dqo, jnp.concatenate([dkv, dvv], axis=1), _rot_t(ksum, c, s1, s2)

    rows = [(dq, w, 0), (dk, w, 0), (dv, w, 0)] + [(t, HEAD_PAD, 0) for t in tabs]
    return _rowwise(fn, rows, [], [(w, BF16), (2 * w, BF16), (HEAD_PAD, F32)], [], ts=ROW_TS, name=name)


ATT_Q = 1024
ATT_SUB = 512
ATT_KV = 1024
ATT_SCALE = (QK_NOPE + QK_ROPE) ** -0.5
LOG2E = 1.4426950408889634
ATT_C2 = ATT_SCALE * LOG2E
ONES_LANE = V_DIM


def _nt(a, b):
    return lax.dot_general(a, b, (((1,), (1,)), ((), ())), preferred_element_type=F32)


def _lanes(x, w):
    return x if w == HEAD_PAD else jnp.tile(x, (1, w // HEAD_PAD))


def _tri(w, transposed):
    r = lax.broadcasted_iota(jnp.int32, (w, w), 0)
    c = lax.broadcasted_iota(jnp.int32, (w, w), 1)
    return (r <= c) if transposed else (c <= r)


def _attn_fwd(q, k, v, name):
    s = q.shape[0]
    tq, kvc = min(ATT_Q, s), min(ATT_KV, s)
    nsub, per = tq // ATT_SUB, tq // kvc

    def body(q_ref, k_ref, v_ref, o_ref, lser_ref, m_s, acc_s):
        i = pl.program_id(1)
        m_s[...] = jnp.full((tq, HEAD_PAD), -jnp.inf, F32)
        acc_s[...] = jnp.zeros((tq, HEAD_PAD), F32)

        def update(r0, n, kb, vb, diag):
            rows = pl.ds(r0, n)
            w = kb.shape[0]
            sc = _nt(q_ref[rows, :], kb)
            if diag:
                sc = jnp.where(_tri(w, False), sc, -jnp.inf)
            m_prev = m_s[rows, :]
            m_next = jnp.maximum(m_prev, jnp.max(sc, axis=1, keepdims=True))
            p = jnp.exp2((sc - _lanes(m_next, w)) * ATT_C2)
            alpha = jnp.exp2((m_prev - m_next) * ATT_C2)
            acc_s[rows, :] = alpha * acc_s[rows, :] + jnp.dot(p.astype(BF16), vb, preferred_element_type=F32)
            m_s[rows, :] = m_next

        def below(j, carry):
            at = pl.ds(pl.multiple_of(j * kvc, kvc), kvc)
            update(0, tq, k_ref[at, :], v_ref[at, :], False)
            return carry

        lax.fori_loop(0, i * per, below, 0)
        for r in range(nsub):
            for c in range(r + 1):
                at = pl.ds(pl.multiple_of(i * tq + c * ATT_SUB, ATT_SUB), ATT_SUB)
                update(r * ATT_SUB, ATT_SUB, k_ref[at, :], v_ref[at, :], c == r)
        l = acc_s[:, ONES_LANE:ONES_LANE + 1]
        o_ref[...] = (acc_s[...] / l).astype(BF16)
        l2 = (m_s[...] * ATT_SCALE + jnp.log(l)) * LOG2E
        for c in range(per):
            lser_ref[c] = jnp.transpose(l2[c * kvc:(c + 1) * kvc])[0:8, :]

    q_spec = pl.BlockSpec((tq, HEAD_PAD), lambda h, i: (i, h))
    kv_spec = pl.BlockSpec((s, HEAD_PAD), lambda h, i: (0, h))
    row_spec = pl.BlockSpec((None, per, 8, kvc), lambda h, i: (h, i, 0, 0))
    return pl.pallas_call(
        body, grid=(N_HEADS, s // tq), in_specs=[q_spec, kv_spec, kv_spec], out_specs=[q_spec, row_spec],
        out_shape=[jax.ShapeDtypeStruct(q.shape, BF16), jax.ShapeDtypeStruct((N_HEADS, s // kvc, 8, kvc), F32)],
        scratch_shapes=[pltpu.VMEM((tq, HEAD_PAD), F32)] * 2,
        compiler_params=pltpu.CompilerParams(dimension_semantics=("parallel", "arbitrary")), name=name,
    )(q, k, v)


def _attn_rows(cat, dcat, name):
    s = cat.shape[0]
    tq, kvc = min(ATT_Q, s), min(ATT_KV, s)
    per = tq // kvc
    ob = CONV_WIDTH // HEAD_PAD

    def body(o_ref, do_ref, dltr_ref):
        dl = jnp.broadcast_to(jnp.sum(do_ref[...] * o_ref[...].astype(F32), axis=1, keepdims=True), (tq, HEAD_PAD))
        for c in range(per):
            dltr_ref[c] = jnp.transpose(dl[c * kvc:(c + 1) * kvc])[0:8, :]

    o_spec = pl.BlockSpec((tq, HEAD_PAD), lambda h, i: (i, ob + h))
    row_spec = pl.BlockSpec((None, per, 8, kvc), lambda h, i: (h, i, 0, 0))
    return pl.pallas_call(
        body, grid=(N_HEADS, s // tq), in_specs=[o_spec, o_spec], out_specs=row_spec,
        out_shape=jax.ShapeDtypeStruct((N_HEADS, s // kvc, 8, kvc), F32),
        compiler_params=pltpu.CompilerParams(dimension_semantics=("parallel", "parallel")), name=name,
    )(cat, dcat)


def _attn_bwd(q, k, v, dcat, lse_r, dlt_r, name):
    s = q.shape[0]
    tk, kvc = min(ATT_Q, s), min(ATT_KV, s)
    nsub, per, n_chunks = tk // ATT_SUB, tk // kvc, s // kvc
    n_j = s // tk
    ob = CONV_WIDTH // HEAD_PAD

    def body(k_ref, v_ref, q_ref, do_ref, lse_ref, dl_ref, dk_ref, dv_ref, dq_ref, dk_s, dv_s):
        j = pl.program_id(1)
        dk_s[...] = jnp.zeros((tk, HEAD_PAD), F32)
        dv_s[...] = jnp.zeros((tk, HEAD_PAD), F32)

        @pl.when(j == 0)
        def _():
            dq_ref[...] = jnp.zeros((s, HEAD_PAD), F32)

        def update(r0, n, at, lrow, drow, diag):
            rows = pl.ds(r0, n)
            qb, dob = q_ref[at, :], do_ref[at, :].astype(BF16)
            sc = _nt(k_ref[rows, :], qb)
            if diag:
                sc = jnp.where(_tri(qb.shape[0], True), sc, -jnp.inf)
            p = jnp.exp2(sc * ATT_C2 - lrow)
            dp = _nt(v_ref[rows, :], dob)
            ds = (p * (dp - drow)).astype(BF16)
            dv_s[rows, :] += jnp.dot(p.astype(BF16), dob, preferred_element_type=F32)
            dk_s[rows, :] += jnp.dot(ds, qb, preferred_element_type=F32)
            dq_ref[at, :] += lax.dot_general(ds, k_ref[rows, :], (((0,), (0,)), ((), ())), preferred_element_type=F32)

        def above(ic, carry):
            at = pl.ds(pl.multiple_of(ic * kvc, kvc), kvc)
            update(0, tk, at, lse_ref[ic, 0:1, :], dl_ref[ic, 0:1, :], False)
            return carry

        lax.fori_loop((j + 1) * per, n_chunks, above, 0)
        for r in range(nsub):
            for c in range(r, nsub):
                at = pl.ds(pl.multiple_of(j * tk + c * ATT_SUB, ATT_SUB), ATT_SUB)
                ic = j * per + (c * ATT_SUB) // kvc
                lo = (c * ATT_SUB) % kvc
                update(r * ATT_SUB, ATT_SUB, at, lse_ref[ic, 0:1, lo:lo + ATT_SUB], dl_ref[ic, 0:1, lo:lo + ATT_SUB],
                       c == r)
        dk_ref[...] = dk_s[...] * ATT_SCALE
        dv_ref[...] = dv_s[...]

        @pl.when(j == n_j - 1)
        def _():
            dq_ref[...] = dq_ref[...] * ATT_SCALE

    kv_spec = pl.BlockSpec((tk, HEAD_PAD), lambda h, j: (j, h))
    q_spec = pl.BlockSpec((s, HEAD_PAD), lambda h, j: (0, h))
    do_spec = pl.BlockSpec((s, HEAD_PAD), lambda h, j: (0, ob + h))
    row_spec = pl.BlockSpec((None, n_chunks, 8, kvc), lambda h, j: (h, 0, 0, 0))
    full = jax.ShapeDtypeStruct(q.shape, F32)
    dk, dv, dq = pl.pallas_call(
        body, grid=(N_HEADS, n_j), in_specs=[kv_spec, kv_spec, q_spec, do_spec, row_spec, row_spec],
        out_specs=[kv_spec, kv_spec, q_spec], out_shape=[full, full, full],
        scratch_shapes=[pltpu.VMEM((tk, HEAD_PAD), F32)] * 2,
        compiler_params=pltpu.CompilerParams(dimension_semantics=("parallel", "arbitrary")), name=name,
    )(k, v, q, dcat, lse_r, dlt_r)
    return dq, dk, dv


SCAN_T = 128
SCAN_C = 512


def _scan(b, lam, *, reverse, xs=None, name):
    s = b.shape[0]
    t = min(SCAN_T, s)
    n_t, n_c = s // t, SSM_CH // SCAN_C
    with_dlam = xs is not None

    def shift(a, d, row):
        if d >= 8:
            z = jnp.zeros((d, SCAN_C), F32)
            return jnp.concatenate([a[d:], z], axis=0) if reverse else jnp.concatenate([z, a[:t - d]], axis=0)
        if reverse:
            return jnp.where(row < t - d, pltpu.roll(a, t - d, 0), 0.0)
        return jnp.where(row >= d, pltpu.roll(a, d, 0), 0.0)

    def body(*refs):
        if with_dlam:
            b_ref, lam_ref, x_ref, o_ref, dl_ref, c_s = refs
        else:
            b_ref, lam_ref, o_ref, c_s = refs
        k = pl.program_id(0)

        @pl.when(k == 0)
        def _():
            c_s[...] = jnp.zeros((1, 2 * SSM_CH), F32)
            if with_dlam:
                dl_ref[...] = jnp.zeros((1, 2 * SSM_CH), F32)

        row = lax.broadcasted_iota(jnp.int32, (t, SCAN_C), 0)
        edge = (row == t - 1) if reverse else (row == 0)
        for ch in range(n_c):
            re = pl.ds(ch * SCAN_C, SCAN_C)
            im = pl.ds(SSM_CH + ch * SCAN_C, SCAN_C)
            lr = lam_ref[:, re]
            li = -lam_ref[:, im] if reverse else lam_ref[:, im]
            cr, ci = c_s[:, re], c_s[:, im]
            ar = b_ref[:, re] + jnp.where(edge, lr * cr - li * ci, 0.0)
            ai = b_ref[:, im] + jnp.where(edge, lr * ci + li * cr, 0.0)
            d = 1
            while d < t:
                sr, si = shift(ar, d, row), shift(ai, d, row)
                ar, ai = ar + lr * sr - li * si, ai + lr * si + li * sr
                lr, li = lr * lr - li * li, 2.0 * lr * li
                d *= 2
            o_ref[:, re] = ar.astype(o_ref.dtype)
            o_ref[:, im] = ai.astype(o_ref.dtype)
            if with_dlam:
                gr = jnp.where(edge, cr, shift(ar, 1, row))
                gi = jnp.where(edge, ci, shift(ai, 1, row))
                xr, xi = x_ref[:, re].astype(F32), x_ref[:, im].astype(F32)
                dl_ref[:, re] += jnp.sum(xr * gr + xi * gi, axis=0, keepdims=True)
                dl_ref[:, im] += jnp.sum(xr * gi - xi * gr, axis=0, keepdims=True)
            last = 0 if reverse else t - 1
            c_s[:, re] = ar[last:last + 1, :]
            c_s[:, im] = ai[last:last + 1, :]

    tm = (lambda k: (n_t - 1 - k, 0)) if reverse else (lambda k: (k, 0))
    blk = pl.BlockSpec((t, 2 * SSM_CH), tm)
    vec = pl.BlockSpec((1, 2 * SSM_CH), lambda k: (0, 0))
    in_specs, args = [blk, vec], [b, lam]
    out_specs, out_shape = [blk], [jax.ShapeDtypeStruct((s, 2 * SSM_CH), BF16)]
    if with_dlam:
        in_specs.append(blk)
        args.append(xs)
        out_specs.append(vec)
        out_shape.append(jax.ShapeDtypeStruct((1, 2 * SSM_CH), F32))
    return pl.pallas_call(
        body, grid=(n_t,), in_specs=in_specs, out_specs=out_specs, out_shape=out_shape,
        scratch_shapes=[pltpu.VMEM((1, 2 * SSM_CH), F32)],
        compiler_params=pltpu.CompilerParams(dimension_semantics=("arbitrary",)), name=name,
    )(*args)


def _s5_disc(log_dt, a_re, a_im, b_re, b_im):
    dt = jnp.exp(log_dt)[:, None]
    mag = jnp.exp(a_re * dt)
    lb_re, lb_im = mag * jnp.cos(a_im * dt), mag * jnp.sin(a_im * dt)
    den = a_re * a_re + a_im * a_im
    nr, ni = lb_re - 1.0, lb_im
    f_re = (nr * a_re + ni * a_im) / den
    f_im = (ni * a_re - nr * a_im) / den
    bb_re = f_re[..., None] * b_re - f_im[..., None] * b_im
    bb_im = f_re[..., None] * b_im + f_im[..., None] * b_re
    return lb_re, lb_im, bb_re, bb_im


def _bd(a):
    g, i, j = a.shape
    eye = jnp.eye(g, dtype=a.dtype)
    return (a[:, :, None, :] * eye[:, None, :, None]).reshape(g * i, g * j)


S5_TILE_GROUPS = HEAD_PAD // SSM_GROUP


def _s5_sparse(s):
    nb = SSM_GROUPS // S5_TILE_GROUPS
    cw, sw = S5_TILE_GROUPS * SSM_GROUP, S5_TILE_GROUPS * SSM_STATE
    tm = min(1024, s)
    return dict(
        expand=dict(tiles=(tm, sw, cw), out=(s, 2 * SSM_CH), nk=1, k_of=lambda i, j, k: j % nb),
        reduce=dict(tiles=(tm, cw, sw), out=(s, SSM_WIDTH), nk=2, k_of=lambda i, j, k: j + nb * k),
        wide_t=dict(tiles=(sw, cw, tm), out=(2 * SSM_CH, cw), nk=s // tm, b_n=lambda i, j: i % nb),
        narrow_t=dict(tiles=(cw, sw, tm), out=(cw, 2 * SSM_CH), nk=s // tm, a_m=lambda i, j: j % nb))


def _exchange(arrs, modes, name):
    n = len(arrs)
    shapes = [a.shape if md == 'scatter' else (N_DEV,) + a.shape for a, md in zip(arrs, modes)]

    def body(*refs):
        srcs, outs = refs[:n], refs[n:2 * n]
        send_sems, recv_sems = refs[2 * n:]
        x, y, c = lax.axis_index("x"), lax.axis_index("y"), lax.axis_index("c")
        me = 4 * x + 2 * y + c
        copies = []
        for r, pos, peer in _peers(x, y, c):
            for q in range(n):
                copies.append(pltpu.make_async_remote_copy(
                    src_ref=srcs[q].at[peer] if modes[q] == 'scatter' else srcs[q], dst_ref=outs[q].at[me],
                    send_sem=send_sems.at[(r - 1) * n + q], recv_sem=recv_sems.at[(r - 1) * n + q], device_id=pos,
                    device_id_type=pl.DeviceIdType.MESH))
        for cp in copies:
            cp.start()
        for cp in copies:
            cp.wait_recv()
        for cp in copies:
            cp.wait_send()

    any_spec = pl.BlockSpec(memory_space=pl.ANY)
    outs = pl.pallas_call(
        body, out_shape=[jax.ShapeDtypeStruct(sh, a.dtype) for sh, a in zip(shapes, arrs)],
        in_specs=[any_spec] * n, out_specs=[any_spec] * n,
        scratch_shapes=[pltpu.SemaphoreType.DMA(((N_DEV - 1) * n,)), pltpu.SemaphoreType.DMA(((N_DEV - 1) * n,))],
        compiler_params=pltpu.CompilerParams(has_side_effects=True), name=name,
    )(*arrs)
    return _own_slots(outs, arrs, modes)


def _peers(x, y, c):
    out = []
    for r in range(1, N_DEV):
        px, py, pc = x ^ (r >> 2), y ^ ((r >> 1) & 1), c ^ (r & 1)
        out.append((r, (px, py, pc), 4 * px + 2 * py + pc))
    return out


def _own_slots(lands, srcs, modes):
    me = 4 * lax.axis_index("x") + 2 * lax.axis_index("y") + lax.axis_index("c")
    out = []
    for land, src, md in zip(lands, srcs, modes):
        own = lax.dynamic_index_in_dim(src, me, 0, keepdims=False) if md == 'scatter' else src
        out.append(lax.dynamic_update_index_in_dim(land, own, me, 0))
    return out


def _exchange_start(arrs, modes, carry, name):
    n = len(arrs)
    shapes = [a.shape if md == 'scatter' else (N_DEV,) + a.shape for a, md in zip(arrs, modes)]
    lands = [lax.empty(sh, a.dtype) for sh, a in zip(shapes, arrs)]

    def body(*refs):
        srcs, zones = refs[:n], refs[n:2 * n]
        send_sems, recv_sems = refs[2 * n + 1], refs[2 * n + 2]
        x, y, c = lax.axis_index("x"), lax.axis_index("y"), lax.axis_index("c")
        me = 4 * x + 2 * y + c
        for r, pos, peer in _peers(x, y, c):
            for q in range(n):
                src = srcs[q].at[peer] if modes[q] == 'scatter' else srcs[q]
                pltpu.make_async_remote_copy(
                    src_ref=src, dst_ref=zones[q].at[me], send_sem=send_sems.at[(r - 1) * n + q],
                    recv_sem=recv_sems.at[(r - 1) * n + q], device_id=pos, device_id_type=pl.DeviceIdType.MESH).start()

    hbm = pl.BlockSpec(memory_space=pltpu.HBM)
    sem = pl.BlockSpec(memory_space=pltpu.SEMAPHORE)
    thru = arrs + lands + [carry]
    sems = pltpu.SemaphoreType.DMA(((N_DEV - 1) * n,))
    outs = pl.pallas_call(
        body, name=name, out_shape=(sems, sems, *[pltpu.HBM(a.shape, a.dtype) for a in thru]),
        in_specs=[hbm] * len(thru), out_specs=(sem, sem, *[hbm] * len(thru)),
        input_output_aliases={q: 2 + q for q in range(len(thru))},
        compiler_params=pltpu.CompilerParams(has_side_effects=pltpu.SideEffectType.DATAFLOW_SIDE_EFFECTING),
    )(*[pltpu.with_memory_space_constraint(a, pltpu.HBM) for a in thru])
    return dict(send=outs[0], recv=outs[1], srcs=list(outs[2:2 + n]), lands=list(outs[2 + n:2 + 2 * n]),
                modes=modes), outs[-1]


def _exchange_wait(flight, after, name):
    n = len(flight['srcs'])
    modes = flight['modes']

    def body(*refs):
        srcs, zones = refs[:n], refs[n:2 * n]
        send_sems, recv_sems = refs[2 * n], refs[2 * n + 1]
        x, y, c = lax.axis_index("x"), lax.axis_index("y"), lax.axis_index("c")
        me = 4 * x + 2 * y + c
        for r, pos, peer in _peers(x, y, c):
            for q in range(n):
                src = srcs[q].at[peer] if modes[q] == 'scatter' else srcs[q]
                cp = pltpu.make_async_remote_copy(
                    src_ref=src, dst_ref=zones[q].at[me], send_sem=send_sems.at[(r - 1) * n + q],
                    recv_sem=recv_sems.at[(r - 1) * n + q], device_id=pos, device_id_type=pl.DeviceIdType.MESH)
                cp.wait_send()
                cp.wait_recv()

    hbm = pl.BlockSpec(memory_space=pltpu.HBM)
    sem = pl.BlockSpec(memory_space=pltpu.SEMAPHORE)
    bufs = flight['srcs'] + flight['lands']
    outs = pl.pallas_call(
        body, name=name, out_shape=tuple(pltpu.HBM(a.shape, a.dtype) for a in bufs),
        in_specs=[hbm] * (2 * n) + [sem, sem, pl.BlockSpec(memory_space=pl.ANY)], out_specs=tuple([hbm] * (2 * n)),
        input_output_aliases={q: q for q in range(2 * n)},
        compiler_params=pltpu.CompilerParams(has_side_effects=pltpu.SideEffectType.DATAFLOW_SIDE_EFFECTING),
    )(*bufs, flight['send'], flight['recv'], after)
    return _own_slots(outs[n:], outs[:n], modes)


def _adamw(parts, w, m, v, name):
    r, c = w.shape
    tr = _tile(r, (256, 128))

    def body(p_ref, w_ref, m_ref, v_ref, g_ref, d_ref, nm_ref, nv_ref):
        g = p_ref[0].astype(F32)
        for d in range(1, N_DEV):
            g = g + p_ref[d].astype(F32)
        m2 = ADAM_B1 * m_ref[...] + (1.0 - ADAM_B1) * g
        v2 = ADAM_B2 * v_ref[...] + (1.0 - ADAM_B2) * jnp.square(g)
        m_hat = m2 / (1.0 - ADAM_B1 ** ADAM_STEP)
        v_hat = v2 / (1.0 - ADAM_B2 ** ADAM_STEP)
        g_ref[...] = g
        d_ref[...] = -ADAM_LR * (m_hat / (jnp.sqrt(v_hat) + ADAM_EPS) + ADAM_WD * w_ref[...])
        nm_ref[...] = m2
        nv_ref[...] = v2

    spec = pl.BlockSpec((tr, c), lambda i: (i, 0))
    return pl.pallas_call(
        body, grid=(r // tr,), in_specs=[pl.BlockSpec((N_DEV, tr, c), lambda i: (0, i, 0)), spec, spec, spec],
        out_specs=[spec] * 4, out_shape=[jax.ShapeDtypeStruct((r, c), F32)] * 4,
        compiler_params=pltpu.CompilerParams(dimension_semantics=("parallel",)), name=name,
    )(parts, w, m, v)


FLAT_W = 512
FLAT_ROWS = 256


def _flat(arrs, rows=FLAT_ROWS):
    v = jnp.concatenate([a.reshape(-1) for a in arrs])
    return jnp.pad(v, (0, (-v.shape[0]) % (rows * FLAT_W))).reshape(-1, FLAT_W)


def _unflat(flat, shapes):
    v = flat.reshape(-1)
    out, off = [], 0
    for sh in shapes:
        n = 1
        for d in sh:
            n *= d
        out.append(v[off:off + n].reshape(sh))
        off += n
    return out


def _full(name, stacked):
    if SHARDED[name] == 0:
        return stacked.reshape((-1,) + stacked.shape[2:])
    return jnp.transpose(stacked, (1, 0, 2)).reshape(stacked.shape[1], -1)


def _shards(name, full):
    if SHARDED[name] == 0:
        return full.reshape((N_DEV, -1) + full.shape[1:])
    r, c = full.shape
    return jnp.transpose(full.reshape(r, N_DEV, c // N_DEV), (1, 0, 2))


def _prep_weights(p):
    q = {}
    if 'l0_w_in' in p:
        w_in = p['l0_w_in']
        z = lambda n: jnp.zeros((D_MODEL, n), w_in.dtype)
        q['w_in'] = jnp.concatenate([w_in[:, :IN_EVEN - QK_ROPE], z(KR_LANE), w_in[:, IN_EVEN - QK_ROPE:],
                                     z(HEAD_PAD - KR_LANE - QK_ROPE)], axis=1)
    dqk = QK_NOPE + QK_ROPE
    if 'l0_w_uq' in p:
        q['w_uq'] = jnp.pad(p['l0_w_uq'].reshape(Q_RANK, N_HEADS, dqk), ((0, 0), (0, 0), (0, HEAD_PAD - dqk))
                            ).reshape(Q_RANK, N_HEADS * HEAD_PAD)
    if 'l0_w_ukv' in p:
        ukv = p['l0_w_ukv'].reshape(KV_RANK, N_HEADS, 2, QK_NOPE)
        padh = lambda a: jnp.pad(a, ((0, 0), (0, 0), (0, HEAD_PAD - QK_NOPE))).reshape(KV_RANK, N_HEADS * HEAD_PAD)
        q['w_ukv'] = jnp.concatenate([padh(ukv[:, :, 0]), padh(ukv[:, :, 1])], axis=1)
    if 'l0_w_out' in p:
        wo = p['l0_w_out']
        wo_a = jnp.pad(wo[CONV_WIDTH:].reshape(N_HEADS, V_DIM, D_MODEL), ((0, 0), (0, HEAD_PAD - V_DIM), (0, 0)))
        q['w_out'] = jnp.concatenate([wo[:CONV_WIDTH], wo_a.reshape(N_HEADS * HEAD_PAD, D_MODEL)], axis=0)
    return q


def _unprep_grads(g):
    out = {}
    if 'w_in' in g:
        d = g['w_in']
        out['l0_w_in'] = jnp.concatenate([d[:, :IN_EVEN - QK_ROPE],
                                          d[:, IN_EVEN - QK_ROPE + KR_LANE:IN_EVEN + KR_LANE]], axis=1)
    dqk = QK_NOPE + QK_ROPE
    if 'w_uq' in g:
        out['l0_w_uq'] = g['w_uq'].reshape(Q_RANK, N_HEADS, HEAD_PAD)[:, :, :dqk].reshape(Q_RANK, N_HEADS * dqk)
    if 'w_ukv' in g:
        d = g['w_ukv'].reshape(KV_RANK, 2, N_HEADS, HEAD_PAD)[:, :, :, :QK_NOPE]
        out['l0_w_ukv'] = jnp.transpose(d, (0, 2, 1, 3)).reshape(KV_RANK, N_HEADS * 2 * QK_NOPE)
    if 'w_out' in g:
        d = g['w_out']
        da = d[CONV_WIDTH:].reshape(N_HEADS, HEAD_PAD, D_MODEL)[:, :V_DIM].reshape(N_HEADS * V_DIM, D_MODEL)
        out['l0_w_out'] = jnp.concatenate([d[:CONV_WIDTH], da], axis=0)
    return out


def _pad_rows(w, rows):
    return jnp.pad(w, [(0, 0)] * (w.ndim - 2) + [(0, rows - w.shape[-2]), (0, 0)])


def _ffn_fwd(x, rep, got, pre, tag):
    s = x.shape[0]
    xn = _rms_fwd(x, rep[pre + 'ffn_norm'], f"{tag}_ffn_norm")
    w_up = got[pre + 'w_up']
    hu = _mm(xn, w_up, gb=_same, go=_same, groups=N_DEV, name=f"{tag}_ffn_up").reshape(N_DEV * s, FF_SHARD)
    taps = _pad_rows(got[pre + 'ffn_conv_w'], TAP_ROWS).reshape(N_DEV * TAP_ROWS, FF_SHARD)
    bias = _pad_rows(rep[pre + 'ffn_conv_b'].reshape(N_DEV, 1, FF_SHARD), TAP_ROWS).reshape(N_DEV * TAP_ROWS, FF_SHARD)
    (act,) = _conv_fwd([(hu, 0), (hu, FF_HALF)], taps, [0, FF_HALF], [bias, bias], [0, FF_HALF], _pair_pre,
                       _gate_post, [BF16], name=f"{tag}_ffn_gate", **_gate(s))
    act = act.reshape(FF_HALF, s, FF_SHARD)
    w_down = got[pre + 'w_down'].reshape(FF_HALF, FF_SHARD, D_MODEL)
    y = _mm(act, w_down, ga=_same, gb=_same, groups=FF_HALF, res=x, name=f"{tag}_ffn_down")
    return y, (x, xn, hu, act, taps, bias, w_up, w_down)


def _ffn_bwd(dy, saved, rep, pre, tag, grads, gsh):
    x, xn, hu, act, taps, bias, w_up, w_down = saved
    s = x.shape[0]
    dact = _mm(dy, w_down, tb=True, gb=_same, go=_same, groups=FF_HALF, name=f"{tag}_ffn_down_dx")
    gsh[pre + 'w_down'] = _mm(act, dy, ta=True, ga=_same, go=_same, groups=FF_HALF, out_dtype=BF16,
                              name=f"{tag}_ffn_down_dw").reshape(N_DEV, FF_SHARD // 2, D_MODEL)
    dha, dhb, dwa, dwb, dba, dbb = _conv_bwd(
        [(hu, 0), (hu, FF_HALF)], taps, [0, FF_HALF], [bias, bias], [0, FF_HALF], _pair_pre, _gate_post,
        [(dact.reshape(FF_HALF * s, FF_SHARD), 0)], BF16, name=f"{tag}_ffn_gate_bwd", **_gate(s))
    dha, dhb = dha.reshape(FF_HALF, s, FF_SHARD), dhb.reshape(FF_HALF, s, FF_SHARD)
    dtaps = jnp.concatenate([dwa, dwb], axis=0).reshape(N_DEV, TAP_ROWS, FF_SHARD)
    gsh[pre + 'ffn_conv_w'] = dtaps[:, :FFN_K].astype(BF16)
    grads[pre + 'ffn_conv_b'] = jnp.concatenate([dba, dbb], axis=0).reshape(N_DEV, TAP_ROWS, FF_SHARD)[:, 0].reshape(-1)
    upper = lambda g: g + FF_HALF
    dxn = _mm(dha, w_up, tb=True, ga=_same, gb=_same, groups=FF_HALF, name=f"{tag}_ffn_up_dx_a")
    dxn = _mm(dhb, w_up, tb=True, ga=_same, gb=upper, groups=FF_HALF, res=dxn, name=f"{tag}_ffn_up_dx_b")
    dwu = [_mm(xn, dh, ta=True, gb=_same, go=_same, groups=FF_HALF, out_dtype=BF16, name=f"{tag}_ffn_up_dw_{t}")
           for t, dh in (("a", dha), ("b", dhb))]
    gsh[pre + 'w_up'] = jnp.concatenate(dwu, axis=0)
    dx, dg = _rms_bwd(x, rep[pre + 'ffn_norm'], dxn, dy, f"{tag}_ffn_norm_bwd")
    grads[pre + 'ffn_norm'] = dg.reshape(-1)
    return dx


def _mla_fwd(x, rep, q, wait_mixer, tabs):
    s = x.shape[0]
    xn = _rms_fwd(x, rep['l0_mix_norm'], "l0_mix_norm")
    hmat = _mm(xn, q['w_in'], name="l0_in")
    got = wait_mixer(hmat)
    q.update(_prep_weights({n: _full(n, got[n]) for n in ('l0_w_uq', 'l0_w_ukv', 'l0_w_out')}))
    wt = _pad_rows(_full('l0_conv_w', got['l0_conv_w']), 4 * TAP_ROWS)
    cpar = [rep['l0_conv_b'].reshape(1, -1), rep['l0_conv_ln_g'].reshape(1, -1), rep['l0_conv_ln_b'].reshape(1, -1)]
    (u,) = _conv_fwd([(hmat, 0), (hmat, 1)], wt, [0], cpar, [0, 0, 0], _glu_pre, _ln_silu_post, [BF16],
                     name="l0_conv", **_conva(s))
    qn, kvn = rep['l0_q_norm'].reshape(1, -1), rep['l0_kv_norm'].reshape(1, -1)
    cqn, ckvn = _rowwise(lambda a, b, ga, gb: (_rms(a, ga), _rms(b, gb)),
                         [(hmat, Q_RANK, 2 * CONV_WIDTH // Q_RANK), (hmat, KV_RANK, (2 * CONV_WIDTH + Q_RANK) // KV_RANK)],
                         [qn, kvn], [(Q_RANK, BF16), (KV_RANK, BF16)], [], ts=ROW_TS, name="l0_latent_norm")
    qf = _mm(cqn, q['w_uq'], name="l0_uq")
    kvf = _mm(ckvn, q['w_ukv'], name="l0_ukv")
    q_rot, k_full, v = _rope_fwd(qf, kvf, hmat, tabs, "l0_rope")
    o, lse_r = _attn_fwd(q_rot, k_full, v, "l0_attn")
    cat = jnp.concatenate([u, o], axis=1)
    y = _mm(cat, q['w_out'], res=x, name="l0_out")
    return y, (x, xn, hmat, wt, cpar, qn, kvn, cqn, ckvn, q_rot, k_full, v, lse_r, cat)


def _mla_bwd(dy, saved, rep, q, tabs, grads, gq, send_grads):
    x, xn, hmat, wt, cpar, qn, kvn, cqn, ckvn, q_rot, k_full, v, lse_r, cat = saved
    s = x.shape[0]
    dcat = _mm(dy, q['w_out'], tb=True, name="l0_out_dx")
    gq['w_out'] = _mm(cat, dy, ta=True, name="l0_out_dw")
    dlt_r = _attn_rows(cat, dcat, "l0_attn_rows")
    dq, dk, dv = _attn_bwd(q_rot, k_full, v, dcat, lse_r, dlt_r, "l0_attn_bwd")
    dqf, dkvf, dkr = _rope_bwd(dq, dk, dv, tabs, "l0_rope_bwd")
    dcqn = _mm(dqf, q['w_uq'], tb=True, name="l0_uq_dx")
    gq['w_uq'] = _mm(cqn, dqf, ta=True, name="l0_uq_dw")
    dckvn = _mm(dkvf, q['w_ukv'], tb=True, name="l0_ukv_dx")
    gq['w_ukv'] = _mm(ckvn, dkvf, ta=True, name="l0_ukv_dw")
    early = _unprep_grads({n: gq[n] for n in ('w_out', 'w_uq', 'w_ukv')})
    dckvn = send_grads(GRADS_MIXER, {n: _shards(n, g).astype(BF16) for n, g in early.items()}, dckvn)

    def lat_bwd(a, b, da, db, ga, gb):
        _, vjp = jax.vjp(lambda a_, b_, ga_, gb_: (_rms(a_, ga_), _rms(b_, gb_)), a, b, ga, gb)
        return vjp((da, db))

    dcq, dckv, dqn, dkvn = _rowwise(
        lat_bwd, [(hmat, Q_RANK, 2 * CONV_WIDTH // Q_RANK), (hmat, KV_RANK, (2 * CONV_WIDTH + Q_RANK) // KV_RANK),
                  (dcqn, Q_RANK, 0), (dckvn, KV_RANK, 0)],
        [qn, kvn], [(Q_RANK, F32), (KV_RANK, F32)], [(1, Q_RANK), (1, KV_RANK)], ts=ROW_TS, name="l0_latent_norm_bwd")
    grads['l0_q_norm'], grads['l0_kv_norm'] = dqn.reshape(-1), dkvn.reshape(-1)
    da, dg, dwt, dcb, dlg, dlb = _conv_bwd(
        [(hmat, 0), (hmat, 1)], wt, [0], cpar, [0, 0, 0], _glu_pre, _ln_silu_post, [(dcat, 0)], F32,
        name="l0_conv_bwd", **_conva(s))
    gq['conv_w'] = dwt[:CONV_K]
    grads['l0_conv_b'], grads['l0_conv_ln_g'], grads['l0_conv_ln_b'] = dcb.reshape(-1), dlg.reshape(-1), dlb.reshape(-1)
    dh = jnp.concatenate([da, dg, dcq, dckv, dkr], axis=1)
    dxn = _mm(dh, q['w_in'], tb=True, name="l0_in_dx")
    gq['w_in'] = _mm(xn, dh, ta=True, name="l0_in_dw")
    dx, dgn = _rms_bwd(x, rep['l0_mix_norm'], dxn, dy, "l0_mix_norm_bwd")
    grads['l0_mix_norm'] = dgn.reshape(-1)
    return dx


def _gelu_skip(yc, u, d):
    return jax.nn.gelu(yc + d * u)


def _glu_out(z1, z2, b1, b2, x):
    return x + (z1 + b1) * jax.nn.sigmoid(z2 + b2)


def _s5_fwd(x, rep, w_in, w_glu):
    xn = _rms_fwd(x, rep['l1_mix_norm'], "l1_mix_norm")
    u = _mm(xn, w_in, name="l1_in")
    lb_re, lb_im, bb_re, bb_im = _s5_disc(rep['l1_log_dt'], rep['l1_a_re'], rep['l1_a_im'], rep['l1_b_re'],
                                          rep['l1_b_im'])
    lam = jnp.concatenate([lb_re.reshape(1, -1), lb_im.reshape(1, -1)], axis=1)
    tr = lambda a: jnp.transpose(a, (0, 2, 1))
    bmat = jnp.concatenate([_bd(tr(bb_re)), _bd(tr(bb_im))], axis=1)
    cmat = jnp.concatenate([_bd(tr(rep['l1_c_re'])), -_bd(tr(rep['l1_c_im']))], axis=0)
    sp = _s5_sparse(x.shape[0])
    bu = _mm(u, bmat, sparse=sp['expand'], name="l1_bu")
    (xs,) = _scan(bu, lam, reverse=False, name="l1_scan")
    yc = _mm(xs, cmat, sparse=sp['reduce'], name="l1_cx")
    dsk = rep['l1_d'].reshape(1, -1)
    (y,) = _rowwise(_gelu_skip, [(yc, SSM_WIDTH, 0), (u, SSM_WIDTH, 0)], [dsk], [(SSM_WIDTH, BF16)], [],
                    ts=ROW_TS, name="l1_gelu")
    z = _mm(y, w_glu, name="l1_glu")
    bg = rep['l1_b_glu'].reshape(1, -1)
    (out,) = _rowwise(lambda z1, z2, xv, b1, b2: _glu_out(z1, z2, b1, b2, xv),
                      [(z, D_MODEL, 0), (z, D_MODEL, 1), (x, D_MODEL, 0)], [bg[:, :D_MODEL], bg[:, D_MODEL:]],
                      [(D_MODEL, F32)], [], ts=ROW_TS, name="l1_glu_out")
    return out, (x, xn, u, lam, bmat, cmat, xs, yc, dsk, y, z, bg, w_in, w_glu)


def _s5_bwd(dy, saved, rep, grads, gq):
    x, xn, u, lam, bmat, cmat, xs, yc, dsk, y, z, bg, w_in, w_glu = saved

    def glu_bwd(z1, z2, dv, b1, b2):
        _, vjp = jax.vjp(lambda a, b, c, d: (a + c) * jax.nn.sigmoid(b + d), z1, z2, b1, b2)
        d1, d2, db1, db2 = vjp(dv)
        return jnp.concatenate([d1, d2], axis=1), db1, db2

    dz, db1, db2 = _rowwise(glu_bwd, [(z, D_MODEL, 0), (z, D_MODEL, 1), (dy, D_MODEL, 0)],
                            [bg[:, :D_MODEL], bg[:, D_MODEL:]], [(2 * D_MODEL, BF16)], [(1, D_MODEL), (1, D_MODEL)],
                            ts=ROW_TS, name="l1_glu_out_bwd")
    grads['l1_b_glu'] = jnp.concatenate([db1, db2], axis=1).reshape(-1)
    dyv = _mm(dz, w_glu, tb=True, name="l1_glu_dx")
    gq['l1_w_glu'] = _mm(y, dz, ta=True, name="l1_glu_dw")

    def gelu_bwd(ycv, uv, dv, dk):
        _, vjp = jax.vjp(_gelu_skip, ycv, uv, dk)
        return vjp(dv)

    dyc, du_skip, dd = _rowwise(gelu_bwd, [(yc, SSM_WIDTH, 0), (u, SSM_WIDTH, 0), (dyv, SSM_WIDTH, 0)], [dsk],
                                [(SSM_WIDTH, F32), (SSM_WIDTH, F32)], [(1, SSM_WIDTH)], ts=ROW_TS, name="l1_gelu_bwd")
    grads['l1_d'] = dd.reshape(-1)
    sp = _s5_sparse(x.shape[0])
    dxs = _mm(dyc, cmat, tb=True, sparse=sp['expand'], name="l1_cx_dx")
    dcm = _mm(xs, dyc, ta=True, sparse=sp['wide_t'], name="l1_cx_dw")
    gs, dlam = _scan(dxs, lam, reverse=True, xs=xs, name="l1_scan_bwd")
    dlr, dli = dlam[:, :SSM_CH], dlam[:, SSM_CH:]
    du = _mm(gs, bmat, tb=True, res=du_skip, sparse=sp['reduce'], name="l1_bu_dx")
    dbm = _mm(u, gs, ta=True, sparse=sp['narrow_t'], name="l1_bu_dw")
    eye = jnp.eye(S5_TILE_GROUPS, dtype=F32)
    nb = SSM_GROUPS // S5_TILE_GROUPS
    dcm = dcm.reshape(2, nb, S5_TILE_GROUPS, SSM_STATE, S5_TILE_GROUPS, SSM_GROUP)
    dcm = jnp.sum(dcm * eye[None, None, :, None, :, None], axis=4).reshape(2, SSM_GROUPS, SSM_STATE, SSM_GROUP)
    tr = lambda a: jnp.transpose(a, (0, 2, 1))
    grads['l1_c_re'], grads['l1_c_im'] = tr(dcm[0]), -tr(dcm[1])
    dbm = dbm.reshape(S5_TILE_GROUPS, SSM_GROUP, 2, nb, S5_TILE_GROUPS, SSM_STATE)
    dbm = jnp.sum(dbm * eye[:, None, None, None, :, None], axis=0)
    dbm = jnp.transpose(dbm, (1, 2, 3, 4, 0)).reshape(2, SSM_GROUPS, SSM_STATE, SSM_GROUP)
    dbb_re, dbb_im = dbm[0], dbm[1]
    names = ['l1_log_dt', 'l1_a_re', 'l1_a_im', 'l1_b_re', 'l1_b_im']
    _, vjp = jax.vjp(_s5_disc, *[rep[n] for n in names])
    for n, gval in zip(names, vjp((dlr.reshape(SSM_GROUPS, SSM_STATE), dli.reshape(SSM_GROUPS, SSM_STATE), dbb_re, dbb_im))):
        grads[n] = gval
    dxn = _mm(du, w_in, tb=True, name="l1_in_dx")
    gq['l1_w_in'] = _mm(xn, du, ta=True, name="l1_in_dw")
    dx, dgn = _rms_bwd(x, rep['l1_mix_norm'], dxn, dy, "l1_mix_norm_bwd")
    grads['l1_mix_norm'] = dgn.reshape(-1)
    return dx


def _loss_head(x, g, target):
    d = x.shape[1]

    def fn(xv, tv, gv):
        y, vjp = jax.vjp(_rms, xv, gv)
        err = y - tv
        part = 0.5 * jnp.sum(jnp.mean(jnp.square(err), axis=-1, keepdims=True), axis=0, keepdims=True)
        dx, dg = vjp(err * (1.0 / d))
        return dx, jnp.broadcast_to(part, (1, 128)), dg

    return _rowwise(fn, [(x, d, 0), (target, d, 0)], [g.reshape(1, -1)], [(d, F32)], [(1, 128), (1, d)], ts=ROW_TS,
                    name="loss_head")


FIRST = ('l0_w_in',)
REST = (('l0_conv_w', 'l0_w_uq', 'l0_w_ukv', 'l0_w_out'),
        ('l0_w_up', 'l0_ffn_conv_w', 'l0_w_down'),
        ('l1_w_in', 'l1_w_glu', 'l1_w_up', 'l1_ffn_conv_w', 'l1_w_down'))
GRADS_L1 = ('l1_w_in', 'l1_w_glu', 'l1_w_up', 'l1_ffn_conv_w', 'l1_w_down')
GRADS_L0_FFN = ('l0_w_up', 'l0_ffn_conv_w', 'l0_w_down')
GRADS_MIXER = ('l0_w_out', 'l0_w_uq', 'l0_w_ukv')
LAST = ('l0_w_in', 'l0_conv_w')
REP_LATE = ('l0_mix_norm', 'l0_conv_b', 'l0_conv_ln_g', 'l0_conv_ln_b', 'l0_q_norm', 'l0_kv_norm')
REP_EARLY = tuple(n for n in REPLICATED if n not in REP_LATE)


def _local_step(x, target, rep, got, wait_rest, send_grads):
    q = _prep_weights({'l0_w_in': _full('l0_w_in', got['l0_w_in'])})
    tabs = _rope_tables(x.shape[0])
    x1, s_mla = _mla_fwd(x, rep, q, lambda after: wait_rest(0, after), tabs)
    x2, s_f0 = _ffn_fwd(x1, rep, wait_rest(1, x1), 'l0_', "l0")
    got = wait_rest(2, x2)
    x3, s_s5 = _s5_fwd(x2, rep, _full('l1_w_in', got['l1_w_in']), _full('l1_w_glu', got['l1_w_glu']))
    x4, s_f1 = _ffn_fwd(x3, rep, got, 'l1_', "l1")
    dx4, loss, dgf = _loss_head(x4, rep['final_norm'], target)
    grads, gq, gsh = {'final_norm': dgf.reshape(-1)}, {}, {}
    dx3 = _ffn_bwd(dx4, s_f1, rep, 'l1_', "l1", grads, gsh)
    dx2 = _s5_bwd(dx3, s_s5, rep, grads, gq)
    for n in ('l1_w_in', 'l1_w_glu'):
        gsh[n] = _shards(n, gq[n]).astype(BF16)
    dx2 = send_grads(GRADS_L1, gsh, dx2)
    dx1 = _ffn_bwd(dx2, s_f0, rep, 'l0_', "l0", grads, gsh)
    dx1 = send_grads(GRADS_L0_FFN, gsh, dx1, small=[grads[n] for n in REP_EARLY])
    dx0 = _mla_bwd(dx1, s_mla, rep, q, tabs, grads, gq, send_grads)
    full = _unprep_grads({'w_in': gq['w_in']})
    full['l0_conv_w'] = gq['conv_w']
    return loss[0, 0], dx0, grads, {n: _shards(n, full[n]).astype(BF16) for n in LAST}


def kernel(x, l0_mix_norm, l0_w_in, l0_conv_w, l0_conv_b, l0_conv_ln_g, l0_conv_ln_b, l0_q_norm, l0_kv_norm, l0_w_uq, l0_w_ukv, l0_w_out, l0_ffn_norm, l0_w_up, l0_ffn_conv_w, l0_ffn_conv_b, l0_w_down, l1_mix_norm, l1_w_in, l1_log_dt, l1_a_re, l1_a_im, l1_b_re, l1_b_im, l1_c_re, l1_c_im, l1_d, l1_w_glu, l1_b_glu, l1_ffn_norm, l1_w_up, l1_ffn_conv_w, l1_ffn_conv_b, l1_w_down, final_norm, loss_target, m_l0_mix_norm, m_l0_w_in, m_l0_conv_w, m_l0_conv_b, m_l0_conv_ln_g, m_l0_conv_ln_b, m_l0_q_norm, m_l0_kv_norm, m_l0_w_uq, m_l0_w_ukv, m_l0_w_out, m_l0_ffn_norm, m_l0_w_up, m_l0_ffn_conv_w, m_l0_ffn_conv_b, m_l0_w_down, m_l1_mix_norm, m_l1_w_in, m_l1_log_dt, m_l1_a_re, m_l1_a_im, m_l1_b_re, m_l1_b_im, m_l1_c_re, m_l1_c_im, m_l1_d, m_l1_w_glu, m_l1_b_glu, m_l1_ffn_norm, m_l1_w_up, m_l1_ffn_conv_w, m_l1_ffn_conv_b, m_l1_w_down, m_final_norm, v_l0_mix_norm, v_l0_w_in, v_l0_conv_w, v_l0_conv_b, v_l0_conv_ln_g, v_l0_conv_ln_b, v_l0_q_norm, v_l0_kv_norm, v_l0_w_uq, v_l0_w_ukv, v_l0_w_out, v_l0_ffn_norm, v_l0_w_up, v_l0_ffn_conv_w, v_l0_ffn_conv_b, v_l0_w_down, v_l1_mix_norm, v_l1_w_in, v_l1_log_dt, v_l1_a_re, v_l1_a_im, v_l1_b_re, v_l1_b_im, v_l1_c_re, v_l1_c_im, v_l1_d, v_l1_w_glu, v_l1_b_glu, v_l1_ffn_norm, v_l1_w_up, v_l1_ffn_conv_w, v_l1_ffn_conv_b, v_l1_w_down, v_final_norm):
    args = dict(locals())
    w = {n: args[n] for n in WEIGHTS}
    m = {n: args['m_' + n] for n in WEIGHTS}
    v = {n: args['v_' + n] for n in WEIGHTS}
    payload = lambda n: w[n] if n in TAPS else w[n].astype(BF16)
    got = dict(zip(FIRST, _exchange([payload(n) for n in FIRST], ['gather'] * len(FIRST), "gather_first")))
    rep = {n: w[n] for n in REPLICATED}
    rest = []
    for k, names in enumerate(REST):
        flight, got['l0_w_in'] = _exchange_start([payload(n) for n in names], ['gather'] * len(names),
                                                 got['l0_w_in'], f"gather_rest{k}_start")
        rest.append(flight)
    wait_rest = lambda k, after: dict(zip(REST[k], _exchange_wait(rest[k], after, f"gather_rest{k}_wait")))
    flights = []

    def send_grads(names, gsh, carry, small=None):
        tag = f"grads{len(flights)}"
        arrs, modes = [gsh[n] for n in names], ['scatter'] * len(names)
        if small is not None:
            arrs, modes = arrs + [_flat(small)], modes + ['gather']
        flight, carry = _exchange_start(arrs, modes, carry, tag + "_start")
        flights.append((names, flight, tag + "_wait"))
        return carry

    loss, dx, grads, gsh = _local_step(x[0], loss_target[0], rep, got, wait_rest, send_grads)

    last = _exchange([gsh[n] for n in LAST] + [_flat([grads[n] for n in REP_LATE], rows=8)],
                     ['scatter'] * len(LAST) + ['gather'], "exchange_grads")
    recv = dict(zip(LAST, last))
    small = {REP_LATE: last[-1]}
    for names, flight, name in flights:
        lands = _exchange_wait(flight, last[-1], name)
        recv.update(zip(names, lands))
        if len(lands) > len(names):
            small[REP_EARLY] = lands[-1]
    res = [dict(), dict(), dict(), dict()]
    for n in SHARDED:
        for kind, a in enumerate(_adamw(recv[n], w[n], m[n], v[n], "adamw_" + n)):
            res[kind][n] = a
    for names, parts in small.items():
        flatr = lambda d: _flat([d[n] for n in names], rows=parts.shape[1])
        rp_out = _adamw(parts, flatr(w), flatr(m), flatr(v), "adamw_replicated_" + names[0])
        for kind in range(4):
            for n, a in zip(names, _unflat(rp_out[kind], [w[n].shape for n in names])):
                res[kind][n] = a
    total = lax.psum(loss, ("x", "y", "c"))
    return (total, dx[None], *[res[0][n] for n in WEIGHTS], *[res[1][n] for n in WEIGHTS],
            *[res[2][n] for n in WEIGHTS], *[res[3][n] for n in WEIGHTS])
```

```python
import functools

import jax
import jax.numpy as jnp
from jax import lax
from jax.experimental import pallas as pl
from jax.experimental.pallas import tpu as pltpu

F32 = jnp.float32
BF16 = jnp.bfloat16

N_DEV = 8
D_MODEL = 1024
EPS = 1e-6
LN_EPS = 1e-5
CONV_WIDTH = 512
CONV_K = 31
N_HEADS = 8
QK_NOPE = 64
QK_ROPE = 32
V_DIM = 64
HEAD_PAD = 128
Q_RANK = 256
KV_RANK = 128
ROPE_BASE = 10000.0
IN_EVEN = 2 * CONV_WIDTH + Q_RANK + KV_RANK + QK_ROPE
IN_PAD = 1536
KR_LANE = 64
SSM_WIDTH = 512
SSM_GROUP = 16
SSM_GROUPS = 32
SSM_STATE = 64
SSM_CH = SSM_GROUPS * SSM_STATE
D_FF = 2816
FF_SHARD = 2 * D_FF // N_DEV
FF_HALF = N_DEV // 2
FFN_K = 3
TAP_ROWS = 8
ADAM_LR, ADAM_B1, ADAM_B2, ADAM_EPS, ADAM_WD, ADAM_STEP = 0.001, 0.9, 0.999, 1e-08, 0.01, 10

WEIGHTS = ['l0_mix_norm', 'l0_w_in', 'l0_conv_w', 'l0_conv_b', 'l0_conv_ln_g', 'l0_conv_ln_b', 'l0_q_norm',
           'l0_kv_norm', 'l0_w_uq', 'l0_w_ukv', 'l0_w_out', 'l0_ffn_norm', 'l0_w_up', 'l0_ffn_conv_w',
           'l0_ffn_conv_b', 'l0_w_down', 'l1_mix_norm', 'l1_w_in', 'l1_log_dt', 'l1_a_re', 'l1_a_im', 'l1_b_re',
           'l1_b_im', 'l1_c_re', 'l1_c_im', 'l1_d', 'l1_w_glu', 'l1_b_glu', 'l1_ffn_norm', 'l1_w_up',
           'l1_ffn_conv_w', 'l1_ffn_conv_b', 'l1_w_down', 'final_norm']
SHARDED = {'l0_w_in': 1, 'l0_conv_w': 1, 'l0_w_uq': 1, 'l0_w_ukv': 1, 'l0_w_out': 0, 'l0_w_up': 1,
           'l0_ffn_conv_w': 1, 'l0_w_down': 0, 'l1_w_in': 0, 'l1_w_glu': 1, 'l1_w_up': 1, 'l1_ffn_conv_w': 1,
           'l1_w_down': 0}
TAPS = ('l0_conv_w', 'l0_ffn_conv_w', 'l1_ffn_conv_w')
REPLICATED = [n for n in WEIGHTS if n not in SHARDED]


def _tile(n, cands):
    for c in cands:
        if n % c == 0:
            return c
    return n


def _same(g):
    return g


def _mm(a, b, *, ta=False, tb=False, res=None, out_dtype=F32, name, ga=None, gb=None, go=None, groups=1,
        sparse=None):
    a2, b2 = (a.shape[1:] if ga else a.shape), (b.shape[1:] if gb else b.shape)
    m, kd = (a2[1], a2[0]) if ta else a2
    kd2, n = (b2[1], b2[0]) if tb else b2
    assert kd == kd2, (a.shape, b.shape, ta, tb)
    tm = _tile(m, (1024, 512, 256, 128))
    tn = _tile(n, (512, 384, 256, 128))
    tk = _tile(kd, (2048, 1024, 512, 256, 128) if ta else (1024, 512, 256, 128))
    nk = kd // tk
    a_m = b_n = k_of = None
    if sparse is not None:
        (tm, tn, tk), (m, n), nk = sparse['tiles'], sparse['out'], sparse['nk']
        a_m, b_n, k_of = sparse.get('a_m'), sparse.get('b_n'), sparse.get('k_of')
    a_m = a_m or (lambda i, j: i)
    b_n = b_n or (lambda i, j: j)
    k_of = k_of or (lambda i, j, k: k)
    summed = go is None and ga is not None and gb is not None
    assert summed or go is not None or (ga is None and gb is None)
    nkk = nk
    dn = (((0 if ta else 1,), (1 if tb else 0,)), ((), ()))

    def body(*refs):
        if res is None:
            a_ref, b_ref, o_ref, acc_ref = refs
            r_ref = None
        else:
            a_ref, b_ref, r_ref, o_ref, acc_ref = refs
        k = pl.program_id(3)
        if summed:
            p = lax.dot_general(a_ref[0].astype(BF16), b_ref[0].astype(BF16), dn, preferred_element_type=F32)
            for g in range(1, groups):
                p = p + lax.dot_general(a_ref[g].astype(BF16), b_ref[g].astype(BF16), dn, preferred_element_type=F32)
        else:
            p = lax.dot_general(a_ref[...].astype(BF16), b_ref[...].astype(BF16), dn, preferred_element_type=F32)

        @pl.when(k == 0)
        def _():
            acc_ref[...] = p

        @pl.when(k > 0)
        def _():
            acc_ref[...] += p

        @pl.when(k == nkk - 1)
        def _():
            out = acc_ref[...]
            if r_ref is not None:
                out = out + r_ref[...]
            o_ref[...] = out.astype(out_dtype)

    def spec(shape2, idx2, gmap):
        if gmap is None:
            return pl.BlockSpec(shape2, lambda g, i, j, kk: idx2(i, j, kk))
        if summed:
            return pl.BlockSpec((groups,) + shape2, lambda g, i, j, kk: (gmap(0) // groups,) + idx2(i, j, kk))
        return pl.BlockSpec((None,) + shape2, lambda g, i, j, kk: (gmap(g),) + idx2(i, j, kk))

    a_idx = (lambda i, j, k: (k_of(i, j, k), a_m(i, j))) if ta else (lambda i, j, k: (a_m(i, j), k_of(i, j, k)))
    b_idx = (lambda i, j, k: (b_n(i, j), k_of(i, j, k))) if tb else (lambda i, j, k: (k_of(i, j, k), b_n(i, j)))
    a_spec = spec((tk, tm) if ta else (tm, tk), a_idx, ga)
    b_spec = spec((tn, tk) if tb else (tk, tn), b_idx, gb)
    o_spec = spec((tm, tn), lambda i, j, k: (i, j), go)
    in_specs, args = [a_spec, b_spec], [a, b]
    if res is not None:
        in_specs.append(o_spec)
        args.append(res)
    out_shape = (groups, m, n) if go else (m, n)
    return pl.pallas_call(
        body, grid=(groups if go else 1, m // tm, n // tn, nkk), in_specs=in_specs, out_specs=o_spec,
        out_shape=jax.ShapeDtypeStruct(out_shape, out_dtype), scratch_shapes=[pltpu.VMEM((tm, tn), F32)],
        compiler_params=pltpu.CompilerParams(dimension_semantics=("parallel", "parallel", "parallel", "arbitrary")),
        name=name)(*args)


ROW_TS = 1024
def _rowwise(fn, rows, bcasts, row_outs, red_outs, *, ts, name):
    s = rows[0][0].shape[0]
    ts = min(ts, s)
    nr, nb, nro, nre = len(rows), len(bcasts), len(row_outs), len(red_outs)

    def body(*refs):
        i = pl.program_id(0)
        outs = fn(*[r[...] for r in refs[:nr + nb]])
        if not isinstance(outs, (tuple, list)):
            outs = (outs,)
        o_refs = refs[nr + nb:]
        for q in range(nro):
            o_refs[q][...] = outs[q].astype(o_refs[q].dtype)
        for q in range(nro, nro + nre):
            @pl.when(i == 0)
            def _(q=q):
                o_refs[q][...] = outs[q]

            @pl.when(i > 0)
            def _(q=q):
                o_refs[q][...] += outs[q]

    in_specs = [pl.BlockSpec((ts, w), functools.partial(lambda i, cb: (i, cb), cb=cb)) for (_, w, cb) in rows]
    in_specs += [pl.BlockSpec(b.shape, functools.partial(lambda i, nd: (0,) * nd, nd=b.ndim)) for b in bcasts]
    out_specs = [pl.BlockSpec((ts, w), lambda i: (i, 0)) for (w, _) in row_outs]
    out_specs += [pl.BlockSpec((r, w), lambda i: (0, 0)) for (r, w) in red_outs]
    out_shape = [jax.ShapeDtypeStruct((s, w), dt) for (w, dt) in row_outs]
    out_shape += [jax.ShapeDtypeStruct((r, w), F32) for (r, w) in red_outs]
    return pl.pallas_call(
        body, grid=(s // ts,), in_specs=in_specs, out_specs=out_specs, out_shape=out_shape,
        compiler_params=pltpu.CompilerParams(dimension_semantics=("arbitrary",)), name=name,
    )(*[r[0] for r in rows], *bcasts)


def _rms(x, g):
    return x * lax.rsqrt(jnp.mean(x * x, axis=-1, keepdims=True) + EPS) * g


def _rms_fwd(x, g, name):
    return _rowwise(lambda xv, gv: _rms(xv, gv), [(x, x.shape[1], 0)], [g.reshape(1, -1)],
                    [(x.shape[1], BF16)], [], ts=ROW_TS, name=name)[0]


def _rms_bwd(x, g, dxn, dres, name):
    d = x.shape[1]

    def fn(xv, dv, rv, gv):
        _, vjp = jax.vjp(_rms, xv, gv)
        dx, dg = vjp(dv.astype(F32))
        return rv + dx, rv + dx, dg

    return _rowwise(fn, [(x, d, 0), (dxn, d, 0), (dres, d, 0)], [g.reshape(1, -1)], [(d, F32), (d, BF16)], [(1, d)],
                    ts=ROW_TS, name=name)


def _cspec(mode, off, rows, width, rowblk, n_rb):
    if mode == 'col':
        return pl.BlockSpec((rows, width), lambda jc, i: (rowblk(i), off + jc))
    return pl.BlockSpec((rows, width), lambda jc, i: ((off + jc) * n_rb + rowblk(i), 0))


TAP_SPREAD = 24


def _stage_shape(k_taps, n, tc):
    return (8, n + TAP_SPREAD, tc) if k_taps > 8 else (1, 8, 128)


def _tap_windows(ref, offsets, n, stage):
    if len(offsets) <= 8:
        return [functools.partial(lambda v: v, ref[pl.ds(o, n), :]) for o in offsets]
    lows = {}
    for o in offsets:
        lows[o % 8] = min(o, lows.get(o % 8, o))
    for r, lo in lows.items():
        span = n + max(o for o in offsets if o % 8 == r) - lo
        stage[r, pl.ds(0, span), :] = ref[pl.ds(lo, span), :]
    return [functools.partial(lambda o: stage[o % 8, pl.ds(o - lows[o % 8], n), :], o) for o in offsets]


def _conv_fwd(xins, w, woffs, params, poffs, pre, post, outs, *, mode, s, k_taps, hb, ts, tc, rc, n_cb, name):
    n_s, nx, ncv, npar, no = s // ts, len(xins), len(woffs), len(params), len(outs)
    rpb = ts // hb
    kp = TAP_ROWS * ((k_taps + TAP_ROWS - 1) // TAP_ROWS)
    pr = 1 if mode == 'col' else TAP_ROWS

    def body(*refs):
        mains, halos = refs[:nx], refs[nx:2 * nx]
        w_refs = refs[2 * nx:2 * nx + ncv]
        p_refs = refs[2 * nx + ncv:2 * nx + ncv + npar]
        o_refs = refs[2 * nx + ncv + npar:2 * nx + ncv + npar + no]
        u_s = refs[2 * nx + ncv + npar + no:2 * nx + ncv + npar + no + ncv]
        win_s = refs[2 * nx + ncv + npar + no + ncv:2 * nx + ncv + npar + no + 2 * ncv]
        stage_s = refs[2 * nx + ncv + npar + no + 2 * ncv:]
        i = pl.program_id(1)
        um = pre(*[r[...].astype(F32) for r in mains])
        uh = pre(*[r[...].astype(F32) for r in halos])
        first = (i > 0).astype(F32)
        for q in range(ncv):
            u_s[q][pl.ds(0, hb), :] = uh[q] * first
            u_s[q][pl.ds(hb, ts), :] = um[q]
        pv = [r[0:1, :] for r in p_refs]

        def chunk(ci, carry):
            r0 = pl.multiple_of(ci * rc, rc)
            cs = []
            for q in range(ncv):
                win_s[q][...] = u_s[q][pl.ds(r0, rc + hb), :]
                acc = jnp.zeros((rc, tc), F32)
                offsets = [hb - (k_taps - 1) + t for t in range(k_taps)]
                for t, ut in enumerate(_tap_windows(win_s[q], offsets, rc, stage_s[q])):
                    acc = acc + w_refs[q][pl.ds(t, 1), :] * ut()
                cs.append(acc)
            res = post(cs, pv)
            for q in range(no):
                o_refs[q][pl.ds(r0, rc), :] = res[q].astype(o_refs[q].dtype)
            return carry

        lax.fori_loop(0, ts // rc, chunk, 0)

    main = lambda i: i
    prev = lambda i: jnp.maximum(i * rpb - 1, 0)
    zero = lambda i: 0
    in_specs = [_cspec(mode, off, ts, tc, main, n_s) for _, off in xins]
    in_specs += [_cspec(mode, off, hb, tc, prev, s // hb) for _, off in xins]
    in_specs += [_cspec(mode, off, kp, tc, zero, 1) for off in woffs]
    in_specs += [_cspec(mode, off, pr, tc, zero, 1) for off in poffs]
    out_specs = [_cspec(mode, 0, ts, tc, main, n_s) for _ in outs]
    oshape = (s, n_cb * tc) if mode == 'col' else (n_cb * s, tc)
    out_shape = [jax.ShapeDtypeStruct(oshape, dt) for dt in outs]
    return pl.pallas_call(
        body, grid=(n_cb, n_s), in_specs=in_specs, out_specs=out_specs, out_shape=out_shape,
        scratch_shapes=[pltpu.VMEM((hb + ts, tc), F32) for _ in range(ncv)]
        + [pltpu.VMEM((hb + rc, tc), F32) for _ in range(ncv)]
        + [pltpu.VMEM(_stage_shape(k_taps, rc, tc), F32) for _ in range(ncv)],
        compiler_params=pltpu.CompilerParams(dimension_semantics=("parallel", "arbitrary")), name=name,
    )(*[a for a, _ in xins], *[a for a, _ in xins], *([w] * ncv), *params)


def _conv_bwd(xins, w, woffs, params, poffs, pre, post, douts, dx_dtype, *, mode, s, k_taps, hb, ts, tc, rc, n_cb, name):
    n_s, nx, ncv, npar, ndo = s // ts, len(xins), len(woffs), len(params), len(douts)
    rpb = ts // hb
    n_hb = s // hb
    kp = TAP_ROWS * ((k_taps + TAP_ROWS - 1) // TAP_ROWS)
    pr = 1 if mode == 'col' else TAP_ROWS

    def body(*refs):
        pos = 0

        def take(n):
            nonlocal pos
            out = refs[pos:pos + n]
            pos += n
            return out

        mains, prevs, nexts = take(nx), take(nx), take(nx)
        d_mains, d_nexts = take(ndo), take(ndo)
        w_refs, p_refs = take(ncv), take(npar)
        dx_refs, dw_refs, dp_refs = take(nx), take(ncv), take(npar)
        u_s, d_s = take(ncv), take(ndo)
        win_s, dcw_s, dwa_s = take(ncv), take(ncv), take(ncv)
        stu_s, std_s = take(ncv), take(ncv)
        i = pl.program_id(1)
        um = pre(*[r[...].astype(F32) for r in mains])
        up = pre(*[r[...].astype(F32) for r in prevs])
        un = pre(*[r[...].astype(F32) for r in nexts])
        first = (i > 0).astype(F32)
        last = (i < n_s - 1).astype(F32)
        for q in range(ncv):
            u_s[q][pl.ds(0, hb), :] = up[q] * first
            u_s[q][pl.ds(hb, ts), :] = um[q]
            u_s[q][pl.ds(hb + ts, hb), :] = un[q]
            dwa_s[q][...] = jnp.zeros((kp * 8, tc), F32)
        for q in range(ndo):
            d_s[q][pl.ds(0, ts), :] = d_mains[q][...].astype(F32)
            d_s[q][pl.ds(ts, hb), :] = d_nexts[q][...].astype(F32) * last
        pv = [r[0:1, :] for r in p_refs]
        fwd_offsets = [hb - (k_taps - 1) + t for t in range(k_taps)]

        def chunk(ci, dpar):
            r0 = pl.multiple_of(ci * rc, rc)
            c_own, c_next, shifted = [], [], []
            for q in range(ncv):
                win_s[q][...] = u_s[q][pl.ds(r0, rc + 2 * hb), :]
                acc = jnp.zeros((rc + hb, tc), F32)
                taps = _tap_windows(win_s[q], fwd_offsets, rc + hb, stu_s[q])
                for t, ut in enumerate(taps):
                    acc = acc + w_refs[q][pl.ds(t, 1), :] * ut()
                c_own.append(acc[:rc])
                c_next.append(acc[rc:])
                shifted.append(taps)
            _, vjp_o = jax.vjp(lambda c, p: tuple(post(c, p)), c_own, pv)
            dc_own, dp_own = vjp_o(tuple(r[pl.ds(r0, rc), :] for r in d_s))
            _, vjp_n = jax.vjp(lambda c: tuple(post(c, pv)), c_next)
            (dc_next,) = vjp_n(tuple(r[pl.ds(r0 + rc, hb), :] for r in d_s))
            dus = []
            for q in range(ncv):
                dcw_s[q][pl.ds(0, rc), :] = dc_own[q]
                dcw_s[q][pl.ds(rc, hb), :] = dc_next[q]
                acc = jnp.zeros((rc, tc), F32)
                bwd_offsets = [k_taps - 1 - t for t in range(k_taps)]
                for t, dct in enumerate(_tap_windows(dcw_s[q], bwd_offsets, rc, std_s[q])):
                    acc = acc + w_refs[q][pl.ds(t, 1), :] * dct()
                    prod = dc_own[q] * shifted[q][t]()[:rc]
                    dwa_s[q][pl.ds(8 * t, 8), :] += jnp.sum(prod.reshape(rc // 8, 8, tc), axis=0)
                dus.append(acc)
            xm = [r[pl.ds(r0, rc), :].astype(F32) for r in mains]
            _, vjp_p = jax.vjp(lambda *xv: tuple(pre(*xv)), *xm)
            dxs = vjp_p(tuple(dus))
            for q in range(nx):
                dx_refs[q][pl.ds(r0, rc), :] = dxs[q].astype(dx_refs[q].dtype)
            return tuple(a + b for a, b in zip(dpar, dp_own))

        dpar = lax.fori_loop(0, ts // rc, chunk, tuple(jnp.zeros((1, tc), F32) for _ in range(npar)))
        for q in range(ncv):
            @pl.when(i == 0)
            def _(q=q):
                dw_refs[q][...] = jnp.zeros((kp, tc), F32)

            for t in range(k_taps):
                dw_refs[q][pl.ds(t, 1), :] += jnp.sum(dwa_s[q][pl.ds(8 * t, 8), :], axis=0, keepdims=True)
        for q in range(npar):
            @pl.when(i == 0)
            def _(q=q):
                dp_refs[q][...] = jnp.zeros((pr, tc), F32)

            dp_refs[q][0:1, :] += dpar[q]

    main = lambda i: i
    prev = lambda i: jnp.maximum(i * rpb - 1, 0)
    nxt = lambda i: jnp.minimum((i + 1) * rpb, n_hb - 1)
    zero = lambda i: 0
    in_specs = [_cspec(mode, off, ts, tc, main, n_s) for _, off in xins]
    in_specs += [_cspec(mode, off, hb, tc, prev, n_hb) for _, off in xins]
    in_specs += [_cspec(mode, off, hb, tc, nxt, n_hb) for _, off in xins]
    in_specs += [_cspec(mode, off, ts, tc, main, n_s) for _, off in douts]
    in_specs += [_cspec(mode, off, hb, tc, nxt, n_hb) for _, off in douts]
    in_specs += [_cspec(mode, off, kp, tc, zero, 1) for off in woffs]
    in_specs += [_cspec(mode, off, pr, tc, zero, 1) for off in poffs]
    out_specs = [_cspec(mode, 0, ts, tc, main, n_s) for _ in xins]
    out_specs += [_cspec(mode, 0, kp, tc, zero, 1) for _ in woffs]
    out_specs += [_cspec(mode, 0, pr, tc, zero, 1) for _ in params]

    def shape(rows):
        return (rows, n_cb * tc) if mode == 'col' else (n_cb * rows, tc)

    out_shape = [jax.ShapeDtypeStruct(shape(s), dx_dtype) for _ in xins]
    out_shape += [jax.ShapeDtypeStruct(shape(kp), F32) for _ in woffs]
    out_shape += [jax.ShapeDtypeStruct(shape(pr), F32) for _ in params]
    xa = [a for a, _ in xins]
    da = [a for a, _ in douts]
    return pl.pallas_call(
        body, grid=(n_cb, n_s), in_specs=in_specs, out_specs=out_specs, out_shape=out_shape,
        scratch_shapes=[pltpu.VMEM((hb + ts + hb, tc), F32) for _ in range(ncv)]
        + [pltpu.VMEM((ts + hb, tc), F32) for _ in range(ndo)]
        + [pltpu.VMEM((rc + 2 * hb, tc), F32) for _ in range(ncv)]
        + [pltpu.VMEM((rc + hb, tc), F32) for _ in range(ncv)]
        + [pltpu.VMEM((kp * 8, tc), F32) for _ in range(ncv)]
        + [pltpu.VMEM(_stage_shape(k_taps, rc + hb, tc), F32) for _ in range(ncv)]
        + [pltpu.VMEM(_stage_shape(k_taps, rc, tc), F32) for _ in range(ncv)],
        compiler_params=pltpu.CompilerParams(dimension_semantics=("parallel", "arbitrary")), name=name,
    )(*xa, *xa, *xa, *da, *da, *([w] * ncv), *params)


def _glu_pre(a, g):
    return [a * jax.nn.sigmoid(g)]


def _ln_silu_post(cs, ps):
    c = cs[0] + ps[0]
    mu = jnp.mean(c, axis=-1, keepdims=True)
    var = jnp.mean(jnp.square(c - mu), axis=-1, keepdims=True)
    y = (c - mu) * lax.rsqrt(var + LN_EPS) * ps[1] + ps[2]
    return [jax.nn.silu(y)]


def _pair_pre(a, b):
    return [a, b]


def _gate_post(cs, ps):
    return [jax.nn.silu(cs[0] + ps[0]) * (cs[1] + ps[1])]


def _conva(s):
    return dict(mode='col', s=s, k_taps=CONV_K, hb=32, ts=512, tc=CONV_WIDTH, rc=64, n_cb=1)


def _gate(s):
    return dict(mode='row', s=s, k_taps=FFN_K, hb=8, ts=min(1024, s), tc=FF_SHARD, rc=32, n_cb=FF_HALF)


def _rope_tables(s):
    half = QK_ROPE // 2
    inv = ROPE_BASE ** (-jnp.arange(half, dtype=F32) / half)
    ang = jnp.arange(s, dtype=F32)[:, None] * inv[None, :]
    cos, sin = jnp.cos(ang), jnp.sin(ang)
    z = lambda n: jnp.zeros((s, n), F32)
    c = jnp.concatenate([jnp.ones((s, QK_NOPE), F32), cos, cos, z(HEAD_PAD - QK_NOPE - QK_ROPE)], axis=1)
    s1 = jnp.concatenate([z(QK_NOPE), -sin, z(HEAD_PAD - QK_NOPE - half)], axis=1)
    s2 = jnp.concatenate([z(QK_NOPE + half), sin, z(HEAD_PAD - QK_NOPE - QK_ROPE)], axis=1)
    return c, s1, s2


def _rot(t, c, s1, s2):
    half = QK_ROPE // 2
    return t * c + pltpu.roll(t, HEAD_PAD - half, 1) * s1 + pltpu.roll(t, half, 1) * s2


def _rot_t(d, c, s1, s2):
    half = QK_ROPE // 2
    return d * c + pltpu.roll(d * s1, half, 1) + pltpu.roll(d * s2, HEAD_PAD - half, 1)


def _heads(v):
    return [v[:, h * HEAD_PAD:(h + 1) * HEAD_PAD] for h in range(N_HEADS)]


def _rope_fwd(qf, kvf, hmat, tabs, name):
    w = N_HEADS * HEAD_PAD

    def fn(q, k, v, kr, c, s1, s2):
        krr = _rot(kr, c, s1, s2)
        qo = jnp.concatenate([_rot(t, c, s1, s2) for t in _heads(q)], axis=1)
        ko = jnp.concatenate([t + krr for t in _heads(k)], axis=1)
        lane = lax.broadcasted_iota(jnp.int32, v.shape, 1) & (HEAD_PAD - 1)
        return qo, ko, jnp.where(lane == ONES_LANE, 1.0, v)

    rows = [(qf, w, 0), (kvf, w, 0), (kvf, w, 1), (hmat, HEAD_PAD, IN_PAD // HEAD_PAD - 1)]
    rows += [(t, HEAD_PAD, 0) for t in tabs]
    return _rowwise(fn, rows, [], [(w, BF16)] * 3, [], ts=ROW_TS, name=name)


def _rope_bwd(dq, dk, dv, tabs, name):
    w = N_HEADS * HEAD_PAD

    def fn(dqv, dkv, dvv, c, s1, s2):
        dqo = jnp.concatenate([_rot_t(t, c, s1, s2) for t in _heads(dqv)], axis=1)
        ksum = functools.reduce(lambda a, b: a + b, _heads(dkv))
        return dqo, jnp.concatenate([dkv, dvv], axis=1), _rot_t(ksum, c, s1, s2)

    rows = [(dq, w, 0), (dk, w, 0), (dv, w, 0)] + [(t, HEAD_PAD, 0) for t in tabs]
    return _rowwise(fn, rows, [], [(w, BF16), (2 * w, BF16), (HEAD_PAD, F32)], [], ts=ROW_TS, name=name)


ATT_Q = 1024
ATT_SUB = 512
ATT_KV = 1024
ATT_SCALE = (QK_NOPE + QK_ROPE) ** -0.5
LOG2E = 1.4426950408889634
ATT_C2 = ATT_SCALE * LOG2E
ONES_LANE = V_DIM


def _nt(a, b):
    return lax.dot_general(a, b, (((1,), (1,)), ((), ())), preferred_element_type=F32)


def _lanes(x, w):
    return x if w == HEAD_PAD else jnp.tile(x, (1, w // HEAD_PAD))


def _tri(w, transposed):
    r = lax.broadcasted_iota(jnp.int32, (w, w), 0)
    c = lax.broadcasted_iota(jnp.int32, (w, w), 1)
    return (r <= c) if transposed else (c <= r)


def _attn_fwd(q, k, v, name):
    s = q.shape[0]
    tq, kvc = min(ATT_Q, s), min(ATT_KV, s)
    nsub, per = tq // ATT_SUB, tq // kvc

    def body(q_ref, k_ref, v_ref, o_ref, lser_ref, m_s, acc_s):
        i = pl.program_id(1)
        m_s[...] = jnp.full((tq, HEAD_PAD), -jnp.inf, F32)
        acc_s[...] = jnp.zeros((tq, HEAD_PAD), F32)

        def update(r0, n, kb, vb, diag):
            rows = pl.ds(r0, n)
            w = kb.shape[0]
            sc = _nt(q_ref[rows, :], kb)
            if diag:
                sc = jnp.where(_tri(w, False), sc, -jnp.inf)
            m_prev = m_s[rows, :]
            m_next = jnp.maximum(m_prev, jnp.max(sc, axis=1, keepdims=True))
            p = jnp.exp2((sc - _lanes(m_next, w)) * ATT_C2)
            alpha = jnp.exp2((m_prev - m_next) * ATT_C2)
            acc_s[rows, :] = alpha * acc_s[rows, :] + jnp.dot(p.astype(BF16), vb, preferred_element_type=F32)
            m_s[rows, :] = m_next

        def below(j, carry):
            at = pl.ds(pl.multiple_of(j * kvc, kvc), kvc)
            update(0, tq, k_ref[at, :], v_ref[at, :], False)
            return carry

        lax.fori_loop(0, i * per, below, 0)
        for r in range(nsub):
            for c in range(r + 1):
                at = pl.ds(pl.multiple_of(i * tq + c * ATT_SUB, ATT_SUB), ATT_SUB)
                update(r * ATT_SUB, ATT_SUB, k_ref[at, :], v_ref[at, :], c == r)
        l = acc_s[:, ONES_LANE:ONES_LANE + 1]
        o_ref[...] = (acc_s[...] / l).astype(BF16)
        l2 = (m_s[...] * ATT_SCALE + jnp.log(l)) * LOG2E
        for c in range(per):
            lser_ref[c] = jnp.transpose(l2[c * kvc:(c + 1) * kvc])[0:8, :]

    q_spec = pl.BlockSpec((tq, HEAD_PAD), lambda h, i: (i, h))
    kv_spec = pl.BlockSpec((s, HEAD_PAD), lambda h, i: (0, h))
    row_spec = pl.BlockSpec((None, per, 8, kvc), lambda h, i: (h, i, 0, 0))
    return pl.pallas_call(
        body, grid=(N_HEADS, s // tq), in_specs=[q_spec, kv_spec, kv_spec], out_specs=[q_spec, row_spec],
        out_shape=[jax.ShapeDtypeStruct(q.shape, BF16), jax.ShapeDtypeStruct((N_HEADS, s // kvc, 8, kvc), F32)],
        scratch_shapes=[pltpu.VMEM((tq, HEAD_PAD), F32)] * 2,
        compiler_params=pltpu.CompilerParams(dimension_semantics=("parallel", "arbitrary")), name=name,
    )(q, k, v)


def _attn_rows(cat, dcat, name):
    s = cat.shape[0]
    tq, kvc = min(ATT_Q, s), min(ATT_KV, s)
    per = tq // kvc
    ob = CONV_WIDTH // HEAD_PAD

    def body(o_ref, do_ref, dltr_ref):
        dl = jnp.broadcast_to(jnp.sum(do_ref[...] * o_ref[...].astype(F32), axis=1, keepdims=True), (tq, HEAD_PAD))
        for c in range(per):
            dltr_ref[c] = jnp.transpose(dl[c * kvc:(c + 1) * kvc])[0:8, :]

    o_spec = pl.BlockSpec((tq, HEAD_PAD), lambda h, i: (i, ob + h))
    row_spec = pl.BlockSpec((None, per, 8, kvc), lambda h, i: (h, i, 0, 0))
    return pl.pallas_call(
        body, grid=(N_HEADS, s // tq), in_specs=[o_spec, o_spec], out_specs=row_spec,
        out_shape=jax.ShapeDtypeStruct((N_HEADS, s // kvc, 8, kvc), F32),
        compiler_params=pltpu.CompilerParams(dimension_semantics=("parallel", "parallel")), name=name,
    )(cat, dcat)


def _attn_bwd(q, k, v, dcat, lse_r, dlt_r, name):
    s = q.shape[0]
    tk, kvc = min(ATT_Q, s), min(ATT_KV, s)
    nsub, per, n_chunks = tk // ATT_SUB, tk // kvc, s // kvc
    n_j = s // tk
    ob = CONV_WIDTH // HEAD_PAD

    def body(k_ref, v_ref, q_ref, do_ref, lse_ref, dl_ref, dk_ref, dv_ref, dq_ref, dk_s, dv_s):
        j = pl.program_id(1)
        dk_s[...] = jnp.zeros((tk, HEAD_PAD), F32)
        dv_s[...] = jnp.zeros((tk, HEAD_PAD), F32)

        @pl.when(j == 0)
        def _():
            dq_ref[...] = jnp.zeros((s, HEAD_PAD), F32)

        def update(r0, n, at, lrow, drow, diag):
            rows = pl.ds(r0, n)
            qb, dob = q_ref[at, :], do_ref[at, :].astype(BF16)
            sc = _nt(k_ref[rows, :], qb)
            if diag:
                sc = jnp.where(_tri(qb.shape[0], True), sc, -jnp.inf)
            p = jnp.exp2(sc * ATT_C2 - lrow)
            dp = _nt(v_ref[rows, :], dob)
            ds = (p * (dp - drow)).astype(BF16)
            dv_s[rows, :] += jnp.dot(p.astype(BF16), dob, preferred_element_type=F32)
            dk_s[rows, :] += jnp.dot(ds, qb, preferred_element_type=F32)
            dq_ref[at, :] += lax.dot_general(ds, k_ref[rows, :], (((0,), (0,)), ((), ())), preferred_element_type=F32)

        def above(ic, carry):
            at = pl.ds(pl.multiple_of(ic * kvc, kvc), kvc)
            update(0, tk, at, lse_ref[ic, 0:1, :], dl_ref[ic, 0:1, :], False)
            return carry

        lax.fori_loop((j + 1) * per, n_chunks, above, 0)
        for r in range(nsub):
            for c in range(r, nsub):
                at = pl.ds(pl.multiple_of(j * tk + c * ATT_SUB, ATT_SUB), ATT_SUB)
                ic = j * per + (c * ATT_SUB) // kvc
                lo = (c * ATT_SUB) % kvc
                update(r * ATT_SUB, ATT_SUB, at, lse_ref[ic, 0:1, lo:lo + ATT_SUB], dl_ref[ic, 0:1, lo:lo + ATT_SUB],
                       c == r)
        dk_ref[...] = dk_s[...] * ATT_SCALE
        dv_ref[...] = dv_s[...]

        @pl.when(j == n_j - 1)
        def _():
            dq_ref[...] = dq_ref[...] * ATT_SCALE

    kv_spec = pl.BlockSpec((tk, HEAD_PAD), lambda h, j: (j, h))
    q_spec = pl.BlockSpec((s, HEAD_PAD), lambda h, j: (0, h))
    do_spec = pl.BlockSpec((s, HEAD_PAD), lambda h, j: (0, ob + h))
    row_spec = pl.BlockSpec((None, n_chunks, 8, kvc), lambda h, j: (h, 0, 0, 0))
    full = jax.ShapeDtypeStruct(q.shape, F32)
    dk, dv, dq = pl.pallas_call(
        body, grid=(N_HEADS, n_j), in_specs=[kv_spec, kv_spec, q_spec, do_spec, row_spec, row_spec],
        out_specs=[kv_spec, kv_spec, q_spec], out_shape=[full, full, full],
        scratch_shapes=[pltpu.VMEM((tk, HEAD_PAD), F32)] * 2,
        compiler_params=pltpu.CompilerParams(dimension_semantics=("parallel", "arbitrary")), name=name,
    )(k, v, q, dcat, lse_r, dlt_r)
    return dq, dk, dv


SCAN_T = 128
SCAN_C = 512


def _scan(b, lam, *, reverse, xs=None, name):
    s = b.shape[0]
    t = min(SCAN_T, s)
    n_t, n_c = s // t, SSM_CH // SCAN_C
    with_dlam = xs is not None

    def shift(a, d, row):
        if d >= 8:
            z = jnp.zeros((d, SCAN_C), F32)
            return jnp.concatenate([a[d:], z], axis=0) if reverse else jnp.concatenate([z, a[:t - d]], axis=0)
        if reverse:
            return jnp.where(row < t - d, pltpu.roll(a, t - d, 0), 0.0)
        return jnp.where(row >= d, pltpu.roll(a, d, 0), 0.0)

    def body(*refs):
        if with_dlam:
            b_ref, lam_ref, x_ref, o_ref, dl_ref, c_s = refs
        else:
            b_ref, lam_ref, o_ref, c_s = refs
        k = pl.program_id(0)

        @pl.when(k == 0)
        def _():
            c_s[...] = jnp.zeros((1, 2 * SSM_CH), F32)
            if with_dlam:
                dl_ref[...] = jnp.zeros((1, 2 * SSM_CH), F32)

        row = lax.broadcasted_iota(jnp.int32, (t, SCAN_C), 0)
        edge = (row == t - 1) if reverse else (row == 0)
        for ch in range(n_c):
            re = pl.ds(ch * SCAN_C, SCAN_C)
            im = pl.ds(SSM_CH + ch * SCAN_C, SCAN_C)
            lr = lam_ref[:, re]
            li = -lam_ref[:, im] if reverse else lam_ref[:, im]
            cr, ci = c_s[:, re], c_s[:, im]
            ar = b_ref[:, re] + jnp.where(edge, lr * cr - li * ci, 0.0)
            ai = b_ref[:, im] + jnp.where(edge, lr * ci + li * cr, 0.0)
            d = 1
            while d < t:
                sr, si = shift(ar, d, row), shift(ai, d, row)
                ar, ai = ar + lr * sr - li * si, ai + lr * si + li * sr
                lr, li = lr * lr - li * li, 2.0 * lr * li
                d *= 2
            o_ref[:, re] = ar.astype(o_ref.dtype)
            o_ref[:, im] = ai.astype(o_ref.dtype)
            if with_dlam:
                gr = jnp.where(edge, cr, shift(ar, 1, row))
                gi = jnp.where(edge, ci, shift(ai, 1, row))
                xr, xi = x_ref[:, re].astype(F32), x_ref[:, im].astype(F32)
                dl_ref[:, re] += jnp.sum(xr * gr + xi * gi, axis=0, keepdims=True)
                dl_ref[:, im] += jnp.sum(xr * gi - xi * gr, axis=0, keepdims=True)
            last = 0 if reverse else t - 1
            c_s[:, re] = ar[last:last + 1, :]
            c_s[:, im] = ai[last:last + 1, :]

    tm = (lambda k: (n_t - 1 - k, 0)) if reverse else (lambda k: (k, 0))
    blk = pl.BlockSpec((t, 2 * SSM_CH), tm)
    vec = pl.BlockSpec((1, 2 * SSM_CH), lambda k: (0, 0))
    in_specs, args = [blk, vec], [b, lam]
    out_specs, out_shape = [blk], [jax.ShapeDtypeStruct((s, 2 * SSM_CH), BF16)]
    if with_dlam:
        in_specs.append(blk)
        args.append(xs)
        out_specs.append(vec)
        out_shape.append(jax.ShapeDtypeStruct((1, 2 * SSM_CH), F32))
    return pl.pallas_call(
        body, grid=(n_t,), in_specs=in_specs, out_specs=out_specs, out_shape=out_shape,
        scratch_shapes=[pltpu.VMEM((1, 2 * SSM_CH), F32)],
        compiler_params=pltpu.CompilerParams(dimension_semantics=("arbitrary",)), name=name,
    )(*args)


def _s5_disc(log_dt, a_re, a_im, b_re, b_im):
    dt = jnp.exp(log_dt)[:, None]
    mag = jnp.exp(a_re * dt)
    lb_re, lb_im = mag * jnp.cos(a_im * dt), mag * jnp.sin(a_im * dt)
    den = a_re * a_re + a_im * a_im
    nr, ni = lb_re - 1.0, lb_im
    f_re = (nr * a_re + ni * a_im) / den
    f_im = (ni * a_re - nr * a_im) / den
    bb_re = f_re[..., None] * b_re - f_im[..., None] * b_im
    bb_im = f_re[..., None] * b_im + f_im[..., None] * b_re
    return lb_re, lb_im, bb_re, bb_im


def _bd(a):
    g, i, j = a.shape
    eye = jnp.eye(g, dtype=a.dtype)
    return (a[:, :, None, :] * eye[:, None, :, None]).reshape(g * i, g * j)


S5_TILE_GROUPS = HEAD_PAD // SSM_GROUP


def _s5_sparse(s):
    nb = SSM_GROUPS // S5_TILE_GROUPS
    cw, sw = S5_TILE_GROUPS * SSM_GROUP, S5_TILE_GROUPS * SSM_STATE
    tm = min(1024, s)
    return dict(
        expand=dict(tiles=(tm, sw, cw), out=(s, 2 * SSM_CH), nk=1, k_of=lambda i, j, k: j % nb),
        reduce=dict(tiles=(tm, cw, sw), out=(s, SSM_WIDTH), nk=2, k_of=lambda i, j, k: j + nb * k),
        wide_t=dict(tiles=(sw, cw, tm), out=(2 * SSM_CH, cw), nk=s // tm, b_n=lambda i, j: i % nb),
        narrow_t=dict(tiles=(cw, sw, tm), out=(cw, 2 * SSM_CH), nk=s // tm, a_m=lambda i, j: j % nb))


def _exchange(arrs, modes, name):
    n = len(arrs)
    shapes = [a.shape if md == 'scatter' else (N_DEV,) + a.shape for a, md in zip(arrs, modes)]

    def body(*refs):
        srcs, outs = refs[:n], refs[n:2 * n]
        send_sems, recv_sems = refs[2 * n:]
        x, y, c = lax.axis_index("x"), lax.axis_index("y"), lax.axis_index("c")
        me = 4 * x + 2 * y + c
        copies = []
        for r, pos, peer in _peers(x, y, c):
            for q in range(n):
                copies.append(pltpu.make_async_remote_copy(
                    src_ref=srcs[q].at[peer] if modes[q] == 'scatter' else srcs[q], dst_ref=outs[q].at[me],
                    send_sem=send_sems.at[(r - 1) * n + q], recv_sem=recv_sems.at[(r - 1) * n + q], device_id=pos,
                    device_id_type=pl.DeviceIdType.MESH))
        for cp in copies:
            cp.start()
        for cp in copies:
            cp.wait_recv()
        for cp in copies:
            cp.wait_send()

    any_spec = pl.BlockSpec(memory_space=pl.ANY)
    outs = pl.pallas_call(
        body, out_shape=[jax.ShapeDtypeStruct(sh, a.dtype) for sh, a in zip(shapes, arrs)],
        in_specs=[any_spec] * n, out_specs=[any_spec] * n,
        scratch_shapes=[pltpu.SemaphoreType.DMA(((N_DEV - 1) * n,)), pltpu.SemaphoreType.DMA(((N_DEV - 1) * n,))],
        compiler_params=pltpu.CompilerParams(has_side_effects=True), name=name,
    )(*arrs)
    return _own_slots(outs, arrs, modes)


def _peers(x, y, c):
    out = []
    for r in range(1, N_DEV):
        px, py, pc = x ^ (r >> 2), y ^ ((r >> 1) & 1), c ^ (r & 1)
        out.append((r, (px, py, pc), 4 * px + 2 * py + pc))
    return out


def _own_slots(lands, srcs, modes):
    me = 4 * lax.axis_index("x") + 2 * lax.axis_index("y") + lax.axis_index("c")
    out = []
    for land, src, md in zip(lands, srcs, modes):
        own = lax.dynamic_index_in_dim(src, me, 0, keepdims=False) if md == 'scatter' else src
        out.append(lax.dynamic_update_index_in_dim(land, own, me, 0))
    return out


def _exchange_start(arrs, modes, carry, name):
    n = len(arrs)
    shapes = [a.shape if md == 'scatter' else (N_DEV,) + a.shape for a, md in zip(arrs, modes)]
    lands = [lax.empty(sh, a.dtype) for sh, a in zip(shapes, arrs)]

    def body(*refs):
        srcs, zones = refs[:n], refs[n:2 * n]
        send_sems, recv_sems = refs[2 * n + 1], refs[2 * n + 2]
        x, y, c = lax.axis_index("x"), lax.axis_index("y"), lax.axis_index("c")
        me = 4 * x + 2 * y + c
        for r, pos, peer in _peers(x, y, c):
            for q in range(n):
                src = srcs[q].at[peer] if modes[q] == 'scatter' else srcs[q]
                pltpu.make_async_remote_copy(
                    src_ref=src, dst_ref=zones[q].at[me], send_sem=send_sems.at[(r - 1) * n + q],
                    recv_sem=recv_sems.at[(r - 1) * n + q], device_id=pos, device_id_type=pl.DeviceIdType.MESH).start()

    hbm = pl.BlockSpec(memory_space=pltpu.HBM)
    sem = pl.BlockSpec(memory_space=pltpu.SEMAPHORE)
    thru = arrs + lands + [carry]
    sems = pltpu.SemaphoreType.DMA(((N_DEV - 1) * n,))
    outs = pl.pallas_call(
        body, name=name, out_shape=(sems, sems, *[pltpu.HBM(a.shape, a.dtype) for a in thru]),
        in_specs=[hbm] * len(thru), out_specs=(sem, sem, *[hbm] * len(thru)),
        input_output_aliases={q: 2 + q for q in range(len(thru))},
        compiler_params=pltpu.CompilerParams(has_side_effects=pltpu.SideEffectType.DATAFLOW_SIDE_EFFECTING),
    )(*[pltpu.with_memory_space_constraint(a, pltpu.HBM) for a in thru])
    return dict(send=outs[0], recv=outs[1], srcs=list(outs[2:2 + n]), lands=list(outs[2 + n:2 + 2 * n]),
                modes=modes), outs[-1]


def _exchange_wait(flight, after, name):
    n = len(flight['srcs'])
    modes = flight['modes']

    def body(*refs):
        srcs, zones = refs[:n], refs[n:2 * n]
        send_sems, recv_sems = refs[2 * n], refs[2 * n + 1]
        x, y, c = lax.axis_index("x"), lax.axis_index("y"), lax.axis_index("c")
        me = 4 * x + 2 * y + c
        for r, pos, peer in _peers(x, y, c):
            for q in range(n):
                src = srcs[q].at[peer] if modes[q] == 'scatter' else srcs[q]
                cp = pltpu.make_async_remote_copy(
                    src_ref=src, dst_ref=zones[q].at[me], send_sem=send_sems.at[(r - 1) * n + q],
                    recv_sem=recv_sems.at[(r - 1) * n + q], device_id=pos, device_id_type=pl.DeviceIdType.MESH)
                cp.wait_send()
                cp.wait_recv()

    hbm = pl.BlockSpec(memory_space=pltpu.HBM)
    sem = pl.BlockSpec(memory_space=pltpu.SEMAPHORE)
    bufs = flight['srcs'] + flight['lands']
    outs = pl.pallas_call(
        body, name=name, out_shape=tuple(pltpu.HBM(a.shape, a.dtype) for a in bufs),
        in_specs=[hbm] * (2 * n) + [sem, sem, pl.BlockSpec(memory_space=pl.ANY)], out_specs=tuple([hbm] * (2 * n)),
        input_output_aliases={q: q for q in range(2 * n)},
        compiler_params=pltpu.CompilerParams(has_side_effects=pltpu.SideEffectType.DATAFLOW_SIDE_EFFECTING),
    )(*bufs, flight['send'], flight['recv'], after)
    return _own_slots(outs[n:], outs[:n], modes)


def _adamw(parts, w, m, v, name):
    r, c = w.shape
    tr = _tile(r, (256, 128))

    def body(p_ref, w_ref, m_ref, v_ref, g_ref, d_ref, nm_ref, nv_ref):
        g = p_ref[0].astype(F32)
        for d in range(1, N_DEV):
            g = g + p_ref[d].astype(F32)
        m2 = ADAM_B1 * m_ref[...] + (1.0 - ADAM_B1) * g
        v2 = ADAM_B2 * v_ref[...] + (1.0 - ADAM_B2) * jnp.square(g)
        m_hat = m2 / (1.0 - ADAM_B1 ** ADAM_STEP)
        v_hat = v2 / (1.0 - ADAM_B2 ** ADAM_STEP)
        g_ref[...] = g
        d_ref[...] = -ADAM_LR * (m_hat / (jnp.sqrt(v_hat) + ADAM_EPS) + ADAM_WD * w_ref[...])
        nm_ref[...] = m2
        nv_ref[...] = v2

    spec = pl.BlockSpec((tr, c), lambda i: (i, 0))
    return pl.pallas_call(
        body, grid=(r // tr,), in_specs=[pl.BlockSpec((N_DEV, tr, c), lambda i: (0, i, 0)), spec, spec, spec],
        out_specs=[spec] * 4, out_shape=[jax.ShapeDtypeStruct((r, c), F32)] * 4,
        compiler_params=pltpu.CompilerParams(dimension_semantics=("parallel",)), name=name,
    )(parts, w, m, v)


FLAT_W = 512
FLAT_ROWS = 256


def _flat(arrs, rows=FLAT_ROWS):
    v = jnp.concatenate([a.reshape(-1) for a in arrs])
    return jnp.pad(v, (0, (-v.shape[0]) % (rows * FLAT_W))).reshape(-1, FLAT_W)


def _unflat(flat, shapes):
    v = flat.reshape(-1)
    out, off = [], 0
    for sh in shapes:
        n = 1
        for d in sh:
            n *= d
        out.append(v[off:off + n].reshape(sh))
        off += n
    return out


def _full(name, stacked):
    if SHARDED[name] == 0:
        return stacked.reshape((-1,) + stacked.shape[2:])
    return jnp.transpose(stacked, (1, 0, 2)).reshape(stacked.shape[1], -1)


def _shards(name, full):
    if SHARDED[name] == 0:
        return full.reshape((N_DEV, -1) + full.shape[1:])
    r, c = full.shape
    return jnp.transpose(full.reshape(r, N_DEV, c // N_DEV), (1, 0, 2))


def _prep_weights(p):
    q = {}
    if 'l0_w_in' in p:
        w_in = p['l0_w_in']
        z = lambda n: jnp.zeros((D_MODEL, n), w_in.dtype)
        q['w_in'] = jnp.concatenate([w_in[:, :IN_EVEN - QK_ROPE], z(KR_LANE), w_in[:, IN_EVEN - QK_ROPE:],
                                     z(HEAD_PAD - KR_LANE - QK_ROPE)], axis=1)
    dqk = QK_NOPE + QK_ROPE
    if 'l0_w_uq' in p:
        q['w_uq'] = jnp.pad(p['l0_w_uq'].reshape(Q_RANK, N_HEADS, dqk), ((0, 0), (0, 0), (0, HEAD_PAD - dqk))
                            ).reshape(Q_RANK, N_HEADS * HEAD_PAD)
    if 'l0_w_ukv' in p:
        ukv = p['l0_w_ukv'].reshape(KV_RANK, N_HEADS, 2, QK_NOPE)
        padh = lambda a: jnp.pad(a, ((0, 0), (0, 0), (0, HEAD_PAD - QK_NOPE))).reshape(KV_RANK, N_HEADS * HEAD_PAD)
        q['w_ukv'] = jnp.concatenate([padh(ukv[:, :, 0]), padh(ukv[:, :, 1])], axis=1)
    if 'l0_w_out' in p:
        wo = p['l0_w_out']
        wo_a = jnp.pad(wo[CONV_WIDTH:].reshape(N_HEADS, V_DIM, D_MODEL), ((0, 0), (0, HEAD_PAD - V_DIM), (0, 0)))
        q['w_out'] = jnp.concatenate([wo[:CONV_WIDTH], wo_a.reshape(N_HEADS * HEAD_PAD, D_MODEL)], axis=0)
    return q


def _unprep_grads(g):
    out = {}
    if 'w_in' in g:
        d = g['w_in']
        out['l0_w_in'] = jnp.concatenate([d[:, :IN_EVEN - QK_ROPE],
                                          d[:, IN_EVEN - QK_ROPE + KR_LANE:IN_EVEN + KR_LANE]], axis=1)
    dqk = QK_NOPE + QK_ROPE
    if 'w_uq' in g:
        out['l0_w_uq'] = g['w_uq'].reshape(Q_RANK, N_HEADS, HEAD_PAD)[:, :, :dqk].reshape(Q_RANK, N_HEADS * dqk)
    if 'w_ukv' in g:
        d = g['w_ukv'].reshape(KV_RANK, 2, N_HEADS, HEAD_PAD)[:, :, :, :QK_NOPE]
        out['l0_w_ukv'] = jnp.transpose(d, (0, 2, 1, 3)).reshape(KV_RANK, N_HEADS * 2 * QK_NOPE)
    if 'w_out' in g:
        d = g['w_out']
        da = d[CONV_WIDTH:].reshape(N_HEADS, HEAD_PAD, D_MODEL)[:, :V_DIM].reshape(N_HEADS * V_DIM, D_MODEL)
        out['l0_w_out'] = jnp.concatenate([d[:CONV_WIDTH], da], axis=0)
    return out


def _pad_rows(w, rows):
    return jnp.pad(w, [(0, 0)] * (w.ndim - 2) + [(0, rows - w.shape[-2]), (0, 0)])


def _ffn_fwd(x, rep, got, pre, tag):
    s = x.shape[0]
    xn = _rms_fwd(x, rep[pre + 'ffn_norm'], f"{tag}_ffn_norm")
    w_up = got[pre + 'w_up']
    hu = _mm(xn, w_up, gb=_same, go=_same, groups=N_DEV, name=f"{tag}_ffn_up").reshape(N_DEV * s, FF_SHARD)
    taps = _pad_rows(got[pre + 'ffn_conv_w'], TAP_ROWS).reshape(N_DEV * TAP_ROWS, FF_SHARD)
    bias = _pad_rows(rep[pre + 'ffn_conv_b'].reshape(N_DEV, 1, FF_SHARD), TAP_ROWS).reshape(N_DEV * TAP_ROWS, FF_SHARD)
    (act,) = _conv_fwd([(hu, 0), (hu, FF_HALF)], taps, [0, FF_HALF], [bias, bias], [0, FF_HALF], _pair_pre,
                       _gate_post, [BF16], name=f"{tag}_ffn_gate", **_gate(s))
    act = act.reshape(FF_HALF, s, FF_SHARD)
    w_down = got[pre + 'w_down'].reshape(FF_HALF, FF_SHARD, D_MODEL)
    y = _mm(act, w_down, ga=_same, gb=_same, groups=FF_HALF, res=x, name=f"{tag}_ffn_down")
    return y, (x, xn, hu, act, taps, bias, w_up, w_down)


def _ffn_bwd(dy, dyb, saved, rep, pre, tag, grads, gsh):
    x, xn, hu, act, taps, bias, w_up, w_down = saved
    s = x.shape[0]
    dact = _mm(dyb, w_down, tb=True, gb=_same, go=_same, groups=FF_HALF, name=f"{tag}_ffn_down_dx")
    gsh[pre + 'w_down'] = _mm(act, dyb, ta=True, ga=_same, go=_same, groups=FF_HALF, out_dtype=BF16,
                              name=f"{tag}_ffn_down_dw").reshape(N_DEV, FF_SHARD // 2, D_MODEL)
    dha, dhb, dwa, dwb, dba, dbb = _conv_bwd(
        [(hu, 0), (hu, FF_HALF)], taps, [0, FF_HALF], [bias, bias], [0, FF_HALF], _pair_pre, _gate_post,
        [(dact.reshape(FF_HALF * s, FF_SHARD), 0)], BF16, name=f"{tag}_ffn_gate_bwd", **_gate(s))
    dha, dhb = dha.reshape(FF_HALF, s, FF_SHARD), dhb.reshape(FF_HALF, s, FF_SHARD)
    dtaps = jnp.concatenate([dwa, dwb], axis=0).reshape(N_DEV, TAP_ROWS, FF_SHARD)
    gsh[pre + 'ffn_conv_w'] = dtaps[:, :FFN_K].astype(BF16)
    grads[pre + 'ffn_conv_b'] = jnp.concatenate([dba, dbb], axis=0).reshape(N_DEV, TAP_ROWS, FF_SHARD)[:, 0].reshape(-1)
    upper = lambda g: g + FF_HALF
    dxn = _mm(dha, w_up, tb=True, ga=_same, gb=_same, groups=FF_HALF, name=f"{tag}_ffn_up_dx_a")
    dxn = _mm(dhb, w_up, tb=True, ga=_same, gb=upper, groups=FF_HALF, res=dxn, name=f"{tag}_ffn_up_dx_b")
    dwu = [_mm(xn, dh, ta=True, gb=_same, go=_same, groups=FF_HALF, out_dtype=BF16, name=f"{tag}_ffn_up_dw_{t}")
           for t, dh in (("a", dha), ("b", dhb))]
    gsh[pre + 'w_up'] = jnp.concatenate(dwu, axis=0)
    dx, dxb, dg = _rms_bwd(x, rep[pre + 'ffn_norm'], dxn, dy, f"{tag}_ffn_norm_bwd")
    grads[pre + 'ffn_norm'] = dg.reshape(-1)
    return dx, dxb


def _mla_fwd(x, rep, q, wait_mixer, tabs):
    s = x.shape[0]
    xn = _rms_fwd(x, rep['l0_mix_norm'], "l0_mix_norm")
    hmat = _mm(xn, q['w_in'], name="l0_in")
    got = wait_mixer(hmat)
    q.update(_prep_weights({n: _full(n, got[n]) for n in ('l0_w_uq', 'l0_w_ukv', 'l0_w_out')}))
    wt = _pad_rows(_full('l0_conv_w', got['l0_conv_w']), 4 * TAP_ROWS)
    cpar = [rep['l0_conv_b'].reshape(1, -1), rep['l0_conv_ln_g'].reshape(1, -1), rep['l0_conv_ln_b'].reshape(1, -1)]
    (u,) = _conv_fwd([(hmat, 0), (hmat, 1)], wt, [0], cpar, [0, 0, 0], _glu_pre, _ln_silu_post, [BF16],
                     name="l0_conv", **_conva(s))
    qn, kvn = rep['l0_q_norm'].reshape(1, -1), rep['l0_kv_norm'].reshape(1, -1)
    cqn, ckvn = _rowwise(lambda a, b, ga, gb: (_rms(a, ga), _rms(b, gb)),
                         [(hmat, Q_RANK, 2 * CONV_WIDTH // Q_RANK), (hmat, KV_RANK, (2 * CONV_WIDTH + Q_RANK) // KV_RANK)],
                         [qn, kvn], [(Q_RANK, BF16), (KV_RANK, BF16)], [], ts=ROW_TS, name="l0_latent_norm")
    qf = _mm(cqn, q['w_uq'], name="l0_uq")
    kvf = _mm(ckvn, q['w_ukv'], name="l0_ukv")
    q_rot, k_full, v = _rope_fwd(qf, kvf, hmat, tabs, "l0_rope")
    o, lse_r = _attn_fwd(q_rot, k_full, v, "l0_attn")
    cat = jnp.concatenate([u, o], axis=1)
    y = _mm(cat, q['w_out'], res=x, name="l0_out")
    return y, (x, xn, hmat, wt, cpar, qn, kvn, cqn, ckvn, q_rot, k_full, v, lse_r, cat)


def _mla_bwd(dy, dyb, saved, rep, q, tabs, grads, gq, send_grads):
    x, xn, hmat, wt, cpar, qn, kvn, cqn, ckvn, q_rot, k_full, v, lse_r, cat = saved
    s = x.shape[0]
    dcat = _mm(dyb, q['w_out'], tb=True, name="l0_out_dx")
    gq['w_out'] = _mm(cat, dyb, ta=True, name="l0_out_dw")
    dlt_r = _attn_rows(cat, dcat, "l0_attn_rows")
    dq, dk, dv = _attn_bwd(q_rot, k_full, v, dcat, lse_r, dlt_r, "l0_attn_bwd")
    dqf, dkvf, dkr = _rope_bwd(dq, dk, dv, tabs, "l0_rope_bwd")
    dcqn = _mm(dqf, q['w_uq'], tb=True, name="l0_uq_dx")
    gq['w_uq'] = _mm(cqn, dqf, ta=True, name="l0_uq_dw")
    dckvn = _mm(dkvf, q['w_ukv'], tb=True, name="l0_ukv_dx")
    gq['w_ukv'] = _mm(ckvn, dkvf, ta=True, name="l0_ukv_dw")
    early = _unprep_grads({n: gq[n] for n in ('w_out', 'w_uq', 'w_ukv')})
    dckvn = send_grads(GRADS_MIXER, {n: _shards(n, g).astype(BF16) for n, g in early.items()}, dckvn)

    def lat_bwd(a, b, da, db, ga, gb):
        _, vjp = jax.vjp(lambda a_, b_, ga_, gb_: (_rms(a_, ga_), _rms(b_, gb_)), a, b, ga, gb)
        return vjp((da, db))

    dcq, dckv, dqn, dkvn = _rowwise(
        lat_bwd, [(hmat, Q_RANK, 2 * CONV_WIDTH // Q_RANK), (hmat, KV_RANK, (2 * CONV_WIDTH + Q_RANK) // KV_RANK),
                  (dcqn, Q_RANK, 0), (dckvn, KV_RANK, 0)],
        [qn, kvn], [(Q_RANK, F32), (KV_RANK, F32)], [(1, Q_RANK), (1, KV_RANK)], ts=ROW_TS, name="l0_latent_norm_bwd")
    grads['l0_q_norm'], grads['l0_kv_norm'] = dqn.reshape(-1), dkvn.reshape(-1)
    da, dg, dwt, dcb, dlg, dlb = _conv_bwd(
        [(hmat, 0), (hmat, 1)], wt, [0], cpar, [0, 0, 0], _glu_pre, _ln_silu_post, [(dcat, 0)], F32,
        name="l0_conv_bwd", **_conva(s))
    gq['conv_w'] = dwt[:CONV_K]
    grads['l0_conv_b'], grads['l0_conv_ln_g'], grads['l0_conv_ln_b'] = dcb.reshape(-1), dlg.reshape(-1), dlb.reshape(-1)
    dh = jnp.concatenate([da, dg, dcq, dckv, dkr], axis=1)
    dxn = _mm(dh, q['w_in'], tb=True, name="l0_in_dx")
    gq['w_in'] = _mm(xn, dh, ta=True, name="l0_in_dw")
    dx, _, dgn = _rms_bwd(x, rep['l0_mix_norm'], dxn, dy, "l0_mix_norm_bwd")
    grads['l0_mix_norm'] = dgn.reshape(-1)
    return dx


def _gelu_skip(yc, u, d):
    return jax.nn.gelu(yc + d * u)


def _glu_out(z1, z2, b1, b2, x):
    return x + (z1 + b1) * jax.nn.sigmoid(z2 + b2)


def _s5_fwd(x, rep, w_in, w_glu):
    xn = _rms_fwd(x, rep['l1_mix_norm'], "l1_mix_norm")
    u = _mm(xn, w_in, name="l1_in")
    lb_re, lb_im, bb_re, bb_im = _s5_disc(rep['l1_log_dt'], rep['l1_a_re'], rep['l1_a_im'], rep['l1_b_re'],
                                          rep['l1_b_im'])
    lam = jnp.concatenate([lb_re.reshape(1, -1), lb_im.reshape(1, -1)], axis=1)
    tr = lambda a: jnp.transpose(a, (0, 2, 1))
    bmat = jnp.concatenate([_bd(tr(bb_re)), _bd(tr(bb_im))], axis=1)
    cmat = jnp.concatenate([_bd(tr(rep['l1_c_re'])), -_bd(tr(rep['l1_c_im']))], axis=0)
    sp = _s5_sparse(x.shape[0])
    bu = _mm(u, bmat, sparse=sp['expand'], name="l1_bu")
    (xs,) = _scan(bu, lam, reverse=False, name="l1_scan")
    yc = _mm(xs, cmat, sparse=sp['reduce'], name="l1_cx")
    dsk = rep['l1_d'].reshape(1, -1)
    (y,) = _rowwise(_gelu_skip, [(yc, SSM_WIDTH, 0), (u, SSM_WIDTH, 0)], [dsk], [(SSM_WIDTH, BF16)], [],
                    ts=ROW_TS, name="l1_gelu")
    z = _mm(y, w_glu, name="l1_glu")
    bg = rep['l1_b_glu'].reshape(1, -1)
    (out,) = _rowwise(lambda z1, z2, xv, b1, b2: _glu_out(z1, z2, b1, b2, xv),
                      [(z, D_MODEL, 0), (z, D_MODEL, 1), (x, D_MODEL, 0)], [bg[:, :D_MODEL], bg[:, D_MODEL:]],
                      [(D_MODEL, F32)], [], ts=ROW_TS, name="l1_glu_out")
    return out, (x, xn, u, lam, bmat, cmat, xs, yc, dsk, y, z, bg, w_in, w_glu)


def _s5_bwd(dy, saved, rep, grads, gq):
    x, xn, u, lam, bmat, cmat, xs, yc, dsk, y, z, bg, w_in, w_glu = saved

    def glu_bwd(z1, z2, dv, b1, b2):
        _, vjp = jax.vjp(lambda a, b, c, d: (a + c) * jax.nn.sigmoid(b + d), z1, z2, b1, b2)
        d1, d2, db1, db2 = vjp(dv)
        return jnp.concatenate([d1, d2], axis=1), db1, db2

    dz, db1, db2 = _rowwise(glu_bwd, [(z, D_MODEL, 0), (z, D_MODEL, 1), (dy, D_MODEL, 0)],
                            [bg[:, :D_MODEL], bg[:, D_MODEL:]], [(2 * D_MODEL, BF16)], [(1, D_MODEL), (1, D_MODEL)],
                            ts=ROW_TS, name="l1_glu_out_bwd")
    grads['l1_b_glu'] = jnp.concatenate([db1, db2], axis=1).reshape(-1)
    dyv = _mm(dz, w_glu, tb=True, name="l1_glu_dx")
    gq['l1_w_glu'] = _mm(y, dz, ta=True, name="l1_glu_dw")

    def gelu_bwd(ycv, uv, dv, dk):
        _, vjp = jax.vjp(_gelu_skip, ycv, uv, dk)
        return vjp(dv)

    dyc, du_skip, dd = _rowwise(gelu_bwd, [(yc, SSM_WIDTH, 0), (u, SSM_WIDTH, 0), (dyv, SSM_WIDTH, 0)], [dsk],
                                [(SSM_WIDTH, F32), (SSM_WIDTH, F32)], [(1, SSM_WIDTH)], ts=ROW_TS, name="l1_gelu_bwd")
    grads['l1_d'] = dd.reshape(-1)
    sp = _s5_sparse(x.shape[0])
    dxs = _mm(dyc, cmat, tb=True, sparse=sp['expand'], name="l1_cx_dx")
    dcm = _mm(xs, dyc, ta=True, sparse=sp['wide_t'], name="l1_cx_dw")
    gs, dlam = _scan(dxs, lam, reverse=True, xs=xs, name="l1_scan_bwd")
    dlr, dli = dlam[:, :SSM_CH], dlam[:, SSM_CH:]
    du = _mm(gs, bmat, tb=True, res=du_skip, sparse=sp['reduce'], name="l1_bu_dx")
    dbm = _mm(u, gs, ta=True, sparse=sp['narrow_t'], name="l1_bu_dw")
    eye = jnp.eye(S5_TILE_GROUPS, dtype=F32)
    nb = SSM_GROUPS // S5_TILE_GROUPS
    dcm = dcm.reshape(2, nb, S5_TILE_GROUPS, SSM_STATE, S5_TILE_GROUPS, SSM_GROUP)
    dcm = jnp.sum(dcm * eye[None, None, :, None, :, None], axis=4).reshape(2, SSM_GROUPS, SSM_STATE, SSM_GROUP)
    tr = lambda a: jnp.transpose(a, (0, 2, 1))
    grads['l1_c_re'], grads['l1_c_im'] = tr(dcm[0]), -tr(dcm[1])
    dbm = dbm.reshape(S5_TILE_GROUPS, SSM_GROUP, 2, nb, S5_TILE_GROUPS, SSM_STATE)
    dbm = jnp.sum(dbm * eye[:, None, None, None, :, None], axis=0)
    dbm = jnp.transpose(dbm, (1, 2, 3, 4, 0)).reshape(2, SSM_GROUPS, SSM_STATE, SSM_GROUP)
    dbb_re, dbb_im = dbm[0], dbm[1]
    names = ['l1_log_dt', 'l1_a_re', 'l1_a_im', 'l1_b_re', 'l1_b_im']
    _, vjp = jax.vjp(_s5_disc, *[rep[n] for n in names])
    for n, gval in zip(names, vjp((dlr.reshape(SSM_GROUPS, SSM_STATE), dli.reshape(SSM_GROUPS, SSM_STATE), dbb_re, dbb_im))):
        grads[n] = gval
    dxn = _mm(du, w_in, tb=True, name="l1_in_dx")
    gq['l1_w_in'] = _mm(xn, du, ta=True, name="l1_in_dw")
    dx, dxb, dgn = _rms_bwd(x, rep['l1_mix_norm'], dxn, dy, "l1_mix_norm_bwd")
    grads['l1_mix_norm'] = dgn.reshape(-1)
    return dx, dxb


def _loss_head(x, g, target):
    d = x.shape[1]

    def fn(xv, tv, gv):
        y, vjp = jax.vjp(_rms, xv, gv)
        err = y - tv
        part = 0.5 * jnp.sum(jnp.mean(jnp.square(err), axis=-1, keepdims=True), axis=0, keepdims=True)
        dx, dg = vjp(err * (1.0 / d))
        return dx, dx, jnp.broadcast_to(part, (1, 128)), dg

    return _rowwise(fn, [(x, d, 0), (target, d, 0)], [g.reshape(1, -1)], [(d, F32), (d, BF16)], [(1, 128), (1, d)], ts=ROW_TS,
                    name="loss_head")


FIRST = ('l0_w_in',)
REST = (('l0_conv_w', 'l0_w_uq', 'l0_w_ukv', 'l0_w_out'),
        ('l0_w_up', 'l0_ffn_conv_w', 'l0_w_down'),
        ('l1_w_in', 'l1_w_glu', 'l1_w_up', 'l1_ffn_conv_w', 'l1_w_down'))
GRADS_L1 = ('l1_w_in', 'l1_w_glu', 'l1_w_up', 'l1_ffn_conv_w', 'l1_w_down')
GRADS_L0_FFN = ('l0_w_up', 'l0_ffn_conv_w', 'l0_w_down')
GRADS_MIXER = ('l0_w_out', 'l0_w_uq', 'l0_w_ukv')
LAST = ('l0_w_in', 'l0_conv_w')
REP_LATE = ('l0_mix_norm', 'l0_conv_b', 'l0_conv_ln_g', 'l0_conv_ln_b', 'l0_q_norm', 'l0_kv_norm')
REP_EARLY = tuple(n for n in REPLICATED if n not in REP_LATE)


def _local_step(x, target, rep, got, wait_rest, send_grads):
    q = _prep_weights({'l0_w_in': _full('l0_w_in', got['l0_w_in'])})
    tabs = _rope_tables(x.shape[0])
    x1, s_mla = _mla_fwd(x, rep, q, lambda after: wait_rest(0, after), tabs)
    x2, s_f0 = _ffn_fwd(x1, rep, wait_rest(1, x1), 'l0_', "l0")
    got = wait_rest(2, x2)
    x3, s_s5 = _s5_fwd(x2, rep, _full('l1_w_in', got['l1_w_in']), _full('l1_w_glu', got['l1_w_glu']))
    x4, s_f1 = _ffn_fwd(x3, rep, got, 'l1_', "l1")
    dx4, dx4b, loss, dgf = _loss_head(x4, rep['final_norm'], target)
    grads, gq, gsh = {'final_norm': dgf.reshape(-1)}, {}, {}
    dx3, _ = _ffn_bwd(dx4, dx4b, s_f1, rep, 'l1_', "l1", grads, gsh)
    dx2, dx2b = _s5_bwd(dx3, s_s5, rep, grads, gq)
    for n in ('l1_w_in', 'l1_w_glu'):
        gsh[n] = _shards(n, gq[n]).astype(BF16)
    dx2b = send_grads(GRADS_L1, gsh, dx2b)
    dx1, dx1b = _ffn_bwd(dx2, dx2b, s_f0, rep, 'l0_', "l0", grads, gsh)
    dx1b = send_grads(GRADS_L0_FFN, gsh, dx1b, small=[grads[n] for n in REP_EARLY])
    dx0 = _mla_bwd(dx1, dx1b, s_mla, rep, q, tabs, grads, gq, send_grads)
    full = _unprep_grads({'w_in': gq['w_in']})
    full['l0_conv_w'] = gq['conv_w']
    return loss[0, 0], dx0, grads, {n: _shards(n, full[n]).astype(BF16) for n in LAST}


def kernel(x, l0_mix_norm, l0_w_in, l0_conv_w, l0_conv_b, l0_conv_ln_g, l0_conv_ln_b, l0_q_norm, l0_kv_norm, l0_w_uq, l0_w_ukv, l0_w_out, l0_ffn_norm, l0_w_up, l0_ffn_conv_w, l0_ffn_conv_b, l0_w_down, l1_mix_norm, l1_w_in, l1_log_dt, l1_a_re, l1_a_im, l1_b_re, l1_b_im, l1_c_re, l1_c_im, l1_d, l1_w_glu, l1_b_glu, l1_ffn_norm, l1_w_up, l1_ffn_conv_w, l1_ffn_conv_b, l1_w_down, final_norm, loss_target, m_l0_mix_norm, m_l0_w_in, m_l0_conv_w, m_l0_conv_b, m_l0_conv_ln_g, m_l0_conv_ln_b, m_l0_q_norm, m_l0_kv_norm, m_l0_w_uq, m_l0_w_ukv, m_l0_w_out, m_l0_ffn_norm, m_l0_w_up, m_l0_ffn_conv_w, m_l0_ffn_conv_b, m_l0_w_down, m_l1_mix_norm, m_l1_w_in, m_l1_log_dt, m_l1_a_re, m_l1_a_im, m_l1_b_re, m_l1_b_im, m_l1_c_re, m_l1_c_im, m_l1_d, m_l1_w_glu, m_l1_b_glu, m_l1_ffn_norm, m_l1_w_up, m_l1_ffn_conv_w, m_l1_ffn_conv_b, m_l1_w_down, m_final_norm, v_l0_mix_norm, v_l0_w_in, v_l0_conv_w, v_l0_conv_b, v_l0_conv_ln_g, v_l0_conv_ln_b, v_l0_q_norm, v_l0_kv_norm, v_l0_w_uq, v_l0_w_ukv, v_l0_w_out, v_l0_ffn_norm, v_l0_w_up, v_l0_ffn_conv_w, v_l0_ffn_conv_b, v_l0_w_down, v_l1_mix_norm, v_l1_w_in, v_l1_log_dt, v_l1_a_re, v_l1_a_im, v_l1_b_re, v_l1_b_im, v_l1_c_re, v_l1_c_im, v_l1_d, v_l1_w_glu, v_l1_b_glu, v_l1_ffn_norm, v_l1_w_up, v_l1_ffn_conv_w, v_l1_ffn_conv_b, v_l1_w_down, v_final_norm):
    args = dict(locals())
    w = {n: args[n] for n in WEIGHTS}
    m = {n: args['m_' + n] for n in WEIGHTS}
    v = {n: args['v_' + n] for n in WEIGHTS}
    payload = lambda n: w[n] if n in TAPS else w[n].astype(BF16)
    got = dict(zip(FIRST, _exchange([payload(n) for n in FIRST], ['gather'] * len(FIRST), "gather_first")))
    rep = {n: w[n] for n in REPLICATED}
    rest = []
    for k, names in enumerate(REST):
        flight, got['l0_w_in'] = _exchange_start([payload(n) for n in names], ['gather'] * len(names),
                                                 got['l0_w_in'], f"gather_rest{k}_start")
        rest.append(flight)
    wait_rest = lambda k, after: dict(zip(REST[k], _exchange_wait(rest[k], after, f"gather_rest{k}_wait")))
    flights = []

    def send_grads(names, gsh, carry, small=None):
        tag = f"grads{len(flights)}"
        arrs, modes = [gsh[n] for n in names], ['scatter'] * len(names)
        if small is not None:
            arrs, modes = arrs + [_flat(small)], modes + ['gather']
        flight, carry = _exchange_start(arrs, modes, carry, tag + "_start")
        flights.append((names, flight, tag + "_wait"))
        return carry

    loss, dx, grads, gsh = _local_step(x[0], loss_target[0], rep, got, wait_rest, send_grads)

    last = _exchange([gsh[n] for n in LAST] + [_flat([grads[n] for n in REP_LATE], rows=8)],
                     ['scatter'] * len(LAST) + ['gather'], "exchange_grads")
    recv = dict(zip(LAST, last))
    small = {REP_LATE: last[-1]}
    for names, flight, name in flights:
        lands = _exchange_wait(flight, last[-1], name)
        recv.update(zip(names, lands))
        if len(lands) > len(names):
            small[REP_EARLY] = lands[-1]
    res = [dict(), dict(), dict(), dict()]
    for n in SHARDED:
        for kind, a in enumerate(_adamw(recv[n], w[n], m[n], v[n], "adamw_" + n)):
            res[kind][n] = a
    for names, parts in small.items():
        flatr = lambda d: _flat([d[n] for n in names], rows=parts.shape[1])
        rp_out = _adamw(parts, flatr(w), flatr(m), flatr(v), "adamw_replicated_" + names[0])
        for kind in range(4):
            for n, a in zip(names, _unflat(rp_out[kind], [w[n].shape for n in names])):
                res[kind][n] = a
    total = lax.psum(loss, ("x", "y", "c"))
    return (total, dx[None], *[res[0][n] for n in WEIGHTS], *[res[1][n] for n in WEIGHTS],
            *[res[2][n] for n in WEIGHTS], *[res[3][n] for n in WEIGHTS])
```
